```python
import jax
import jax.numpy as jnp
from jax import lax
import numpy as np

D_MODEL = 2048
BATCH = 8
SEQ = 4096
DEPTH = 2

GRID_W = 64
CTX_LEN = 256
N_MIXERS = 2
NORM_EPS = 1e-6

RET_HEADS = 8
RET_DK = D_MODEL // RET_HEADS
RET_DV = 2 * RET_DK
RET_CHUNK = 128
RET_ROPE_BASE = 10000.0

ATT_HEAD_DIM = 128
ATT_Q_HEADS = D_MODEL // ATT_HEAD_DIM
ATT_KV_HEADS = 4
ATT_GROUP = ATT_Q_HEADS // ATT_KV_HEADS
WINDOW = 128
ATT_BLOCK = 128
ROPE_BASE = 10000.0

D_FF = 4 * D_MODEL

N_RET_LAYERS = (DEPTH + 1) // 2
N_ATT_LAYERS = DEPTH // 2
NEG_INF = -1e30

kernel_name = "hybrid_retention_swa_dit"


def rms_norm(x, g):
    xf = x.astype(jnp.float32)
    y = xf * lax.rsqrt(jnp.mean(xf * xf, axis=-1, keepdims=True) + NORM_EPS)
    return (y * g.astype(jnp.float32)).astype(x.dtype)


def modulate(h, shift, scale):
    return h * (1 + scale) + shift


def rope_angles(pos, dim, base):
    inv_freq = base ** (-jnp.arange(0, dim, 2, dtype=jnp.float32) / dim)
    return pos.astype(jnp.float32)[:, None] * inv_freq[None, :]


def apply_rope(x, ang):
    x1, x2 = jnp.split(x.astype(jnp.float32), 2, axis=-1)
    cos = jnp.cos(ang)[:, None, :]
    sin = jnp.sin(ang)[:, None, :]
    return jnp.concatenate([x1 * cos - x2 * sin, x2 * cos + x1 * sin], axis=-1).astype(x.dtype)


def axial_rope(x, rows, cols):
    half = x.shape[-1] // 2
    xr = apply_rope(x[..., :half], rope_angles(rows, half, ROPE_BASE))
    xc = apply_rope(x[..., half:], rope_angles(cols, half, ROPE_BASE))
    return jnp.concatenate([xr, xc], axis=-1)


def retention_scan(q, k, v, log_gamma, s0):
    B, L, H, _ = q.shape
    dv = v.shape[-1]
    C = RET_CHUNK
    n = L // C

    def to_chunks(t):
        return t.reshape(B, n, C, H, t.shape[-1]).transpose(1, 0, 3, 2, 4)

    idx = jnp.arange(C, dtype=jnp.float32)
    diff = idx[:, None] - idx[None, :]
    lg = log_gamma.astype(jnp.float32)
    decay_in = jnp.where(diff >= 0, jnp.exp(lg[:, None, None] * jnp.maximum(diff, 0.0)), 0.0)
    decay_q = jnp.exp(lg[:, None] * (idx + 1.0))[None, :, :, None]
    decay_k = jnp.exp(lg[:, None] * (C - 1.0 - idx))[None, :, :, None]
    decay_c = jnp.exp(lg * C)[None, :, None, None]

    def step(s, qkv):
        qc, kc, vc = (t.astype(jnp.float32) for t in qkv)
        scores = jnp.einsum("bhid,bhjd->bhij", qc, kc) * decay_in
        o = (jnp.einsum("bhij,bhjv->bhiv", scores, vc)
             + jnp.einsum("bhid,bhdv->bhiv", qc, s) * decay_q)
        s_new = s * decay_c + jnp.einsum("bhjd,bhjv->bhdv", kc * decay_k, vc)
        return s_new, o

    s_fin, o = lax.scan(step, s0, (to_chunks(q), to_chunks(k), to_chunks(v)))
    o = o.transpose(1, 0, 3, 2, 4).reshape(B, L, H, dv)
    return o, s_fin


def retention_mixer(h_ctx, h_lat, w_in, w_out, decay_fwd, decay_bwd, ang_lat, need_ctx_out):
    qk_w = RET_HEADS * RET_DK
    v_w = RET_HEADS * RET_DV

    def project(h):
        B, L, _ = h.shape
        q, k, v, g = jnp.split(h @ w_in, [qk_w, 2 * qk_w, 2 * qk_w + v_w], axis=-1)
        return (q.reshape(B, L, RET_HEADS, RET_DK),
                k.reshape(B, L, RET_HEADS, RET_DK) * (RET_DK ** -0.5),
                v.reshape(B, L, RET_HEADS, RET_DV),
                g)

    qc, kc, vc, gc = project(h_ctx)
    ql, kl, vl, gl = project(h_lat)
    ql = apply_rope(ql, ang_lat)
    kl = apply_rope(kl, ang_lat)
    lg_f = jax.nn.log_sigmoid(decay_fwd.astype(jnp.float32))
    lg_b = jax.nn.log_sigmoid(decay_bwd.astype(jnp.float32))
    B = h_lat.shape[0]
    s0 = jnp.zeros((B, RET_HEADS, RET_DK, RET_DV), jnp.float32)

    def flip(t):
        return jnp.flip(t, axis=1)

    oc_f, sc_f = retention_scan(qc, kc, vc, lg_f, s0)
    oc_b, sc_b = retention_scan(flip(qc), flip(kc), flip(vc), lg_b, s0)
    ol_f, _ = retention_scan(ql, kl, vl, lg_f, sc_f)
    ol_b, _ = retention_scan(flip(ql), flip(kl), flip(vl), lg_b, sc_b)

    def readout(o, g):
        Bo, Lo = o.shape[:2]
        y = o * lax.rsqrt(jnp.mean(o * o, axis=-1, keepdims=True) + NORM_EPS)
        y = y.reshape(Bo, Lo, RET_HEADS * RET_DV).astype(g.dtype)
        return (jax.nn.silu(g) * y) @ w_out

    out_lat = readout(ol_f + flip(ol_b), gl)
    out_ctx = readout(oc_f + flip(oc_b), gc) if need_ctx_out else None
    return out_ctx, out_lat


def window_attention_mixer(h_ctx, h_lat, w_in, w_out, sink, rows, cols, need_ctx_out):
    Hq, Hkv, G, dh = ATT_Q_HEADS, ATT_KV_HEADS, ATT_GROUP, ATT_HEAD_DIM
    scale = dh ** -0.5
    f32 = jnp.float32

    def project(h):
        B, L, _ = h.shape
        q, k, v = jnp.split(h @ w_in, [Hq * dh, (Hq + Hkv) * dh], axis=-1)
        return q.reshape(B, L, Hq, dh), k.reshape(B, L, Hkv, dh), v.reshape(B, L, Hkv, dh)

    qc, kc, vc = project(h_ctx)
    ql, kl, vl = project(h_lat)
    ql = axial_rope(ql, rows, cols)
    kl = axial_rope(kl, rows, cols)
    B, L = ql.shape[:2]
    nb = L // ATT_BLOCK
    sink_f = sink.astype(f32).reshape(Hkv, G)

    qb = ql.reshape(B, nb, ATT_BLOCK, Hkv, G, dh)

    def band(t):
        tp = jnp.pad(t, ((0, 0), (ATT_BLOCK, ATT_BLOCK), (0, 0), (0, 0)))
        tp = tp.reshape(B, nb + 2, ATT_BLOCK, Hkv, dh)
        return jnp.concatenate([tp[:, :-2], tp[:, 1:-1], tp[:, 2:]], axis=2)

    kb, vb = band(kl), band(vl)
    s_loc = jnp.einsum("bnqkgd,bnskd->bnkgqs", qb, kb, preferred_element_type=f32) * scale
    s_ctx = jnp.einsum("bnqkgd,bckd->bnkgqc", qb, kc, preferred_element_type=f32) * scale
    qi = jnp.arange(ATT_BLOCK)[:, None]
    kj = jnp.arange(3 * ATT_BLOCK)[None, :]
    kpos = (jnp.arange(nb)[:, None, None] - 1) * ATT_BLOCK + kj[None]
    mask = (jnp.abs(kj - ATT_BLOCK - qi)[None] <= WINDOW) & (kpos >= 0) & (kpos < L)
    s_loc = jnp.where(mask[None, :, None, None], s_loc, NEG_INF)
    sink_l = sink_f[None, None, :, :, None, None]
    m = jnp.maximum(jnp.maximum(s_loc.max(-1, keepdims=True), s_ctx.max(-1, keepdims=True)), sink_l)
    p_loc = jnp.exp(s_loc - m)
    p_ctx = jnp.exp(s_ctx - m)
    denom = p_loc.sum(-1, keepdims=True) + p_ctx.sum(-1, keepdims=True) + jnp.exp(sink_l - m)
    o = (jnp.einsum("bnkgqs,bnskd->bnqkgd", p_loc.astype(vl.dtype), vb, preferred_element_type=f32)
         + jnp.einsum("bnkgqc,bckd->bnqkgd", p_ctx.astype(vc.dtype), vc, preferred_element_type=f32))
    o = o / jnp.moveaxis(denom, 4, 2)
    out_lat = o.reshape(B, L, Hq * dh).astype(h_lat.dtype) @ w_out

    if need_ctx_out:
        Bc, Lc = qc.shape[:2]
        qcg = qc.reshape(Bc, Lc, Hkv, G, dh)
        s = jnp.einsum("bqkgd,bckd->bkgqc", qcg, kc, preferred_element_type=f32) * scale
        sink_c = sink_f[None, :, :, None, None]
        mc = jnp.maximum(s.max(-1, keepdims=True), sink_c)
        p = jnp.exp(s - mc)
        denom_c = p.sum(-1, keepdims=True) + jnp.exp(sink_c - mc)
        oc = jnp.einsum("bkgqc,bckd->bqkgd", p.astype(vc.dtype), vc, preferred_element_type=f32)
        oc = oc / jnp.moveaxis(denom_c, 3, 1)
        out_ctx = oc.reshape(Bc, Lc, Hq * dh).astype(h_ctx.dtype) @ w_out
    else:
        out_ctx = None
    return out_ctx, out_lat


def squared_relu_mlp(h, w1, w2):
    return jnp.square(jax.nn.relu(h @ w1)) @ w2


def _fwd_setup_inputs(seed: int = 0) -> dict:
    key = jax.random.key(seed)
    ks = jax.random.split(key, 20)
    f32 = jnp.float32
    D = D_MODEL

    def nrm(k, shape, fan_in):
        return jax.random.normal(k, shape, f32) * (fan_in ** -0.5)

    ret_in_w = RET_HEADS * (2 * RET_DK + 2 * RET_DV)
    att_in_w = (ATT_Q_HEADS + 2 * ATT_KV_HEADS) * ATT_HEAD_DIM
    gamma = 1.0 - 2.0 ** (-5.0 - jnp.arange(RET_HEADS, dtype=f32))
    decay_logit = jnp.log(gamma) - jnp.log1p(-gamma)

    return {
        "x": jax.random.normal(ks[0], (BATCH, SEQ, D), f32),
        "c": jax.random.normal(ks[1], (BATCH, D), f32),
        "ctx": jax.random.normal(ks[2], (BATCH, CTX_LEN, D), f32),
        "c_ctx": jax.random.normal(ks[3], (D,), f32),
        "ada_w": nrm(ks[4], (DEPTH, D, 6 * D), D),
        "ada_b": 0.02 * jax.random.normal(ks[5], (DEPTH, 6 * D), f32),
        "norm_mix_g": 1.0 + 0.02 * jax.random.normal(ks[6], (DEPTH, D), f32),
        "norm_mlp_g": 1.0 + 0.02 * jax.random.normal(ks[7], (DEPTH, D), f32),
        "mlp_w1": nrm(ks[8], (DEPTH, D, D_FF), D),
        "mlp_w2": nrm(ks[9], (DEPTH, D_FF, D), D_FF),
        "ret_w_in": nrm(ks[10], (N_RET_LAYERS, D, ret_in_w), D),
        "ret_w_out": nrm(ks[11], (N_RET_LAYERS, RET_HEADS * RET_DV, D), RET_HEADS * RET_DV),
        "ret_decay_fwd": decay_logit[None, :] + 0.1 * jax.random.normal(ks[12], (N_RET_LAYERS, RET_HEADS), f32),
        "ret_decay_bwd": decay_logit[None, :] + 0.1 * jax.random.normal(ks[13], (N_RET_LAYERS, RET_HEADS), f32),
        "attn_w_in": nrm(ks[14], (N_ATT_LAYERS, D, att_in_w), D),
        "attn_w_out": nrm(ks[15], (N_ATT_LAYERS, ATT_Q_HEADS * ATT_HEAD_DIM, D), ATT_Q_HEADS * ATT_HEAD_DIM),
        "attn_sink": jax.random.normal(ks[16], (N_ATT_LAYERS, ATT_Q_HEADS), f32),
        "final_norm_g": 1.0 + 0.02 * jax.random.normal(ks[17], (D,), f32),
    }


def _fwd_reference(x, c, ctx, c_ctx, ada_w, ada_b, norm_mix_g, norm_mlp_g, mlp_w1, mlp_w2,
              ret_w_in, ret_w_out, ret_decay_fwd, ret_decay_bwd,
              attn_w_in, attn_w_out, attn_sink, final_norm_g):
    L = x.shape[1]
    ROWS = L // GRID_W
    rows = jnp.repeat(jnp.arange(ROWS), GRID_W)
    cols = jnp.arange(ROWS * GRID_W) % GRID_W
    ret_ang = rope_angles(jnp.arange(L), RET_DK, RET_ROPE_BASE)
    cond_lat = jax.nn.silu(c)
    cond_ctx = jax.nn.silu(c_ctx)
    h_ctx = ctx

    for i in range(DEPTH):
        last = i == DEPTH - 1
        j = i // N_MIXERS
        mod_lat = (cond_lat @ ada_w[i] + ada_b[i])[:, None, :]
        mod_ctx = (cond_ctx @ ada_w[i] + ada_b[i])[None, None, :]
        sh1_l, sc1_l, g1_l, sh2_l, sc2_l, g2_l = jnp.split(mod_lat, 6, axis=-1)
        sh1_c, sc1_c, g1_c, sh2_c, sc2_c, g2_c = jnp.split(mod_ctx, 6, axis=-1)

        a_lat = modulate(rms_norm(x, norm_mix_g[i]), sh1_l, sc1_l)
        a_ctx = modulate(rms_norm(h_ctx, norm_mix_g[i]), sh1_c, sc1_c)
        if i % N_MIXERS == 0:
            o_ctx, o_lat = retention_mixer(a_ctx, a_lat, ret_w_in[j], ret_w_out[j],
                                           ret_decay_fwd[j], ret_decay_bwd[j], ret_ang, not last)
        else:
            o_ctx, o_lat = window_attention_mixer(a_ctx, a_lat, attn_w_in[j], attn_w_out[j],
                                                  attn_sink[j], rows, cols, not last)

        x = x + g1_l * o_lat
        x = x + g2_l * squared_relu_mlp(modulate(rms_norm(x, norm_mlp_g[i]), sh2_l, sc2_l),
                                        mlp_w1[i], mlp_w2[i])
        if not last:
            h_ctx = h_ctx + g1_c * o_ctx
            h_ctx = h_ctx + g2_c * squared_relu_mlp(modulate(rms_norm(h_ctx, norm_mlp_g[i]), sh2_c, sc2_c),
                                                    mlp_w1[i], mlp_w2[i])

    return rms_norm(x, final_norm_g)


import jax as _jax
import jax.numpy as _jnp

TWIN_FORMAT = 'train_step'
FWD_PARAMS = ['x', 'c', 'ctx', 'c_ctx', 'ada_w', 'ada_b', 'norm_mix_g', 'norm_mlp_g', 'mlp_w1', 'mlp_w2', 'ret_w_in', 'ret_w_out', 'ret_decay_fwd', 'ret_decay_bwd', 'attn_w_in', 'attn_w_out', 'attn_sink', 'final_norm_g']
TWIN_WEIGHTS = ['c_ctx', 'ada_w', 'ada_b', 'norm_mix_g', 'norm_mlp_g', 'mlp_w1', 'mlp_w2', 'ret_w_in', 'ret_w_out', 'ret_decay_fwd', 'ret_decay_bwd', 'attn_w_in', 'attn_w_out', 'attn_sink', 'final_norm_g']
TWIN_DIFF_INPUT = 'x'
TWIN_INPUTS = ['x', 'c', 'ctx', 'c_ctx', 'ada_w', 'ada_b', 'norm_mix_g', 'norm_mlp_g', 'mlp_w1', 'mlp_w2', 'ret_w_in', 'ret_w_out', 'ret_decay_fwd', 'ret_decay_bwd', 'attn_w_in', 'attn_w_out', 'attn_sink', 'final_norm_g', 'loss_target', 'm_c_ctx', 'm_ada_w', 'm_ada_b', 'm_norm_mix_g', 'm_norm_mlp_g', 'm_mlp_w1', 'm_mlp_w2', 'm_ret_w_in', 'm_ret_w_out', 'm_ret_decay_fwd', 'm_ret_decay_bwd', 'm_attn_w_in', 'm_attn_w_out', 'm_attn_sink', 'm_final_norm_g', 'v_c_ctx', 'v_ada_w', 'v_ada_b', 'v_norm_mix_g', 'v_norm_mlp_g', 'v_mlp_w1', 'v_mlp_w2', 'v_ret_w_in', 'v_ret_w_out', 'v_ret_decay_fwd', 'v_ret_decay_bwd', 'v_attn_w_in', 'v_attn_w_out', 'v_attn_sink', 'v_final_norm_g']
TWIN_OUTPUTS = ['loss', 'grad_x', 'grad_c_ctx', 'grad_ada_w', 'grad_ada_b', 'grad_norm_mix_g', 'grad_norm_mlp_g', 'grad_mlp_w1', 'grad_mlp_w2', 'grad_ret_w_in', 'grad_ret_w_out', 'grad_ret_decay_fwd', 'grad_ret_decay_bwd', 'grad_attn_w_in', 'grad_attn_w_out', 'grad_attn_sink', 'grad_final_norm_g', 'delta_c_ctx', 'delta_ada_w', 'delta_ada_b', 'delta_norm_mix_g', 'delta_norm_mlp_g', 'delta_mlp_w1', 'delta_mlp_w2', 'delta_ret_w_in', 'delta_ret_w_out', 'delta_ret_decay_fwd', 'delta_ret_decay_bwd', 'delta_attn_w_in', 'delta_attn_w_out', 'delta_attn_sink', 'delta_final_norm_g', 'new_m_c_ctx', 'new_m_ada_w', 'new_m_ada_b', 'new_m_norm_mix_g', 'new_m_norm_mlp_g', 'new_m_mlp_w1', 'new_m_mlp_w2', 'new_m_ret_w_in', 'new_m_ret_w_out', 'new_m_ret_decay_fwd', 'new_m_ret_decay_bwd', 'new_m_attn_w_in', 'new_m_attn_w_out', 'new_m_attn_sink', 'new_m_final_norm_g', 'new_v_c_ctx', 'new_v_ada_w', 'new_v_ada_b', 'new_v_norm_mix_g', 'new_v_norm_mlp_g', 'new_v_mlp_w1', 'new_v_mlp_w2', 'new_v_ret_w_in', 'new_v_ret_w_out', 'new_v_ret_decay_fwd', 'new_v_ret_decay_bwd', 'new_v_attn_w_in', 'new_v_attn_w_out', 'new_v_attn_sink', 'new_v_final_norm_g']
TWIN_LEAF_KINDS = {'loss': 'loss', 'grad_x': 'grad_x', 'grad_c_ctx': 'grad_w', 'grad_ada_w': 'grad_w', 'grad_ada_b': 'grad_w', 'grad_norm_mix_g': 'grad_w', 'grad_norm_mlp_g': 'grad_w', 'grad_mlp_w1': 'grad_w', 'grad_mlp_w2': 'grad_w', 'grad_ret_w_in': 'grad_w', 'grad_ret_w_out': 'grad_w', 'grad_ret_decay_fwd': 'grad_w', 'grad_ret_decay_bwd': 'grad_w', 'grad_attn_w_in': 'grad_w', 'grad_attn_w_out': 'grad_w', 'grad_attn_sink': 'grad_w', 'grad_final_norm_g': 'grad_w', 'delta_c_ctx': 'delta_w', 'delta_ada_w': 'delta_w', 'delta_ada_b': 'delta_w', 'delta_norm_mix_g': 'delta_w', 'delta_norm_mlp_g': 'delta_w', 'delta_mlp_w1': 'delta_w', 'delta_mlp_w2': 'delta_w', 'delta_ret_w_in': 'delta_w', 'delta_ret_w_out': 'delta_w', 'delta_ret_decay_fwd': 'delta_w', 'delta_ret_decay_bwd': 'delta_w', 'delta_attn_w_in': 'delta_w', 'delta_attn_w_out': 'delta_w', 'delta_attn_sink': 'delta_w', 'delta_final_norm_g': 'delta_w', 'new_m_c_ctx': 'new_m', 'new_m_ada_w': 'new_m', 'new_m_ada_b': 'new_m', 'new_m_norm_mix_g': 'new_m', 'new_m_norm_mlp_g': 'new_m', 'new_m_mlp_w1': 'new_m', 'new_m_mlp_w2': 'new_m', 'new_m_ret_w_in': 'new_m', 'new_m_ret_w_out': 'new_m', 'new_m_ret_decay_fwd': 'new_m', 'new_m_ret_decay_bwd': 'new_m', 'new_m_attn_w_in': 'new_m', 'new_m_attn_w_out': 'new_m', 'new_m_attn_sink': 'new_m', 'new_m_final_norm_g': 'new_m', 'new_v_c_ctx': 'new_v', 'new_v_ada_w': 'new_v', 'new_v_ada_b': 'new_v', 'new_v_norm_mix_g': 'new_v', 'new_v_norm_mlp_g': 'new_v', 'new_v_mlp_w1': 'new_v', 'new_v_mlp_w2': 'new_v', 'new_v_ret_w_in': 'new_v', 'new_v_ret_w_out': 'new_v', 'new_v_ret_decay_fwd': 'new_v', 'new_v_ret_decay_bwd': 'new_v', 'new_v_attn_w_in': 'new_v', 'new_v_attn_w_out': 'new_v', 'new_v_attn_sink': 'new_v', 'new_v_final_norm_g': 'new_v'}


def _forward(args):
    return _fwd_reference(*[args[k] for k in FWD_PARAMS])


def _output_shape():
    def fwd():
        inp = _fwd_setup_inputs(0)
        return _fwd_reference(*[inp[k] for k in FWD_PARAMS])
    out = _jax.eval_shape(fwd)
    return out.shape, out.dtype

N_MICROBATCH = 1
ADAM_LR = 0.001
ADAM_B1 = 0.9
ADAM_B2 = 0.999
ADAM_EPS = 1e-08
ADAM_WD = 0.01
ADAM_STEP = 10
PER_EXAMPLE_BATCH_AXIS = {'x': 0, 'c': 0, 'ctx': 0, 'loss_target': 0}
SHARED_INPUTS = []
_WEIGHT_DTYPES = {'c_ctx': _jnp.float32, 'ada_w': _jnp.float32, 'ada_b': _jnp.float32, 'norm_mix_g': _jnp.float32, 'norm_mlp_g': _jnp.float32, 'mlp_w1': _jnp.float32, 'mlp_w2': _jnp.float32, 'ret_w_in': _jnp.float32, 'ret_w_out': _jnp.float32, 'ret_decay_fwd': _jnp.float32, 'ret_decay_bwd': _jnp.float32, 'attn_w_in': _jnp.float32, 'attn_w_out': _jnp.float32, 'attn_sink': _jnp.float32, 'final_norm_g': _jnp.float32}
MOMENT_SCALE = {'c_ctx': 5.235778e-02, 'ada_w': 5.097911e-02, 'ada_b': 9.627983e-02, 'norm_mix_g': 5.022924e-02, 'norm_mlp_g': 6.493168e-02, 'mlp_w1': 3.766406e-02, 'mlp_w2': 7.815741e-02, 'ret_w_in': 3.860563e-02, 'ret_w_out': 3.924268e-02, 'ret_decay_fwd': 1.799320e-01, 'ret_decay_bwd': 4.491032e-01, 'attn_w_in': 2.402692e-02, 'attn_w_out': 2.729615e-02, 'attn_sink': 4.210277e-04, 'final_norm_g': 1.686866e+01}


def _to_microbatches(a, axis):
    t = _jnp.moveaxis(a, axis, 0)
    t = t.reshape((N_MICROBATCH, t.shape[0] // N_MICROBATCH) + t.shape[1:])
    return _jnp.moveaxis(t, 1, axis + 1)


def setup_inputs(seed: int = 0) -> dict:
    inp = _fwd_setup_inputs(seed)
    key = _jax.random.fold_in(_jax.random.key(seed), 7919)
    shape, _ = _output_shape()
    out = dict(inp)
    out["loss_target"] = _jax.random.normal(_jax.random.fold_in(key, 0), shape, _jnp.float32)
    for i, name in enumerate(TWIN_WEIGHTS):
        w = inp[name].astype(_jnp.float32)
        if MOMENT_SCALE is None:
            s = _jnp.sqrt(_jnp.mean(_jnp.square(w)) + 1e-30)
        else:
            s = MOMENT_SCALE[name]
        km, kv = _jax.random.split(_jax.random.fold_in(key, i + 1))
        out[name] = w
        out["m_" + name] = s * _jax.random.normal(km, w.shape, _jnp.float32)
        out["v_" + name] = (s * s) * _jax.random.uniform(kv, w.shape, _jnp.float32, 0.5, 1.5)
    if N_MICROBATCH > 1:
        for name, axis in PER_EXAMPLE_BATCH_AXIS.items():
            out[name] = _to_microbatches(out[name], axis)
    return {'x': out['x'], 'c': out['c'], 'ctx': out['ctx'], 'c_ctx': out['c_ctx'], 'ada_w': out['ada_w'], 'ada_b': out['ada_b'], 'norm_mix_g': out['norm_mix_g'], 'norm_mlp_g': out['norm_mlp_g'], 'mlp_w1': out['mlp_w1'], 'mlp_w2': out['mlp_w2'], 'ret_w_in': out['ret_w_in'], 'ret_w_out': out['ret_w_out'], 'ret_decay_fwd': out['ret_decay_fwd'], 'ret_decay_bwd': out['ret_decay_bwd'], 'attn_w_in': out['attn_w_in'], 'attn_w_out': out['attn_w_out'], 'attn_sink': out['attn_sink'], 'final_norm_g': out['final_norm_g'], 'loss_target': out['loss_target'], 'm_c_ctx': out['m_c_ctx'], 'm_ada_w': out['m_ada_w'], 'm_ada_b': out['m_ada_b'], 'm_norm_mix_g': out['m_norm_mix_g'], 'm_norm_mlp_g': out['m_norm_mlp_g'], 'm_mlp_w1': out['m_mlp_w1'], 'm_mlp_w2': out['m_mlp_w2'], 'm_ret_w_in': out['m_ret_w_in'], 'm_ret_w_out': out['m_ret_w_out'], 'm_ret_decay_fwd': out['m_ret_decay_fwd'], 'm_ret_decay_bwd': out['m_ret_decay_bwd'], 'm_attn_w_in': out['m_attn_w_in'], 'm_attn_w_out': out['m_attn_w_out'], 'm_attn_sink': out['m_attn_sink'], 'm_final_norm_g': out['m_final_norm_g'], 'v_c_ctx': out['v_c_ctx'], 'v_ada_w': out['v_ada_w'], 'v_ada_b': out['v_ada_b'], 'v_norm_mix_g': out['v_norm_mix_g'], 'v_norm_mlp_g': out['v_norm_mlp_g'], 'v_mlp_w1': out['v_mlp_w1'], 'v_mlp_w2': out['v_mlp_w2'], 'v_ret_w_in': out['v_ret_w_in'], 'v_ret_w_out': out['v_ret_w_out'], 'v_ret_decay_fwd': out['v_ret_decay_fwd'], 'v_ret_decay_bwd': out['v_ret_decay_bwd'], 'v_attn_w_in': out['v_attn_w_in'], 'v_attn_w_out': out['v_attn_w_out'], 'v_attn_sink': out['v_attn_sink'], 'v_final_norm_g': out['v_final_norm_g']}


def _loss(weights, diff, rest, loss_target):
    with _jax.named_scope("forward"):
        args = {**rest, TWIN_DIFF_INPUT: diff, **{k: w.astype(_WEIGHT_DTYPES[k]) for k, w in weights.items()}}
        y = _forward(args)
    with _jax.named_scope("loss_head"):
        err = _jnp.square(y.astype(_jnp.float32) - loss_target)
        return 0.5 * _jnp.sum(_jnp.mean(err, axis=-1)) if err.ndim else 0.5 * err


def _adamw(w, g, m, v):
    m = ADAM_B1 * m + (1.0 - ADAM_B1) * g
    v = ADAM_B2 * v + (1.0 - ADAM_B2) * _jnp.square(g)
    m_hat = m / (1.0 - ADAM_B1 ** ADAM_STEP)
    v_hat = v / (1.0 - ADAM_B2 ** ADAM_STEP)
    delta = -ADAM_LR * (m_hat / (_jnp.sqrt(v_hat) + ADAM_EPS) + ADAM_WD * w)
    return delta, m, v


def reference(x, c, ctx, c_ctx, ada_w, ada_b, norm_mix_g, norm_mlp_g, mlp_w1, mlp_w2, ret_w_in, ret_w_out, ret_decay_fwd, ret_decay_bwd, attn_w_in, attn_w_out, attn_sink, final_norm_g, loss_target, m_c_ctx, m_ada_w, m_ada_b, m_norm_mix_g, m_norm_mlp_g, m_mlp_w1, m_mlp_w2, m_ret_w_in, m_ret_w_out, m_ret_decay_fwd, m_ret_decay_bwd, m_attn_w_in, m_attn_w_out, m_attn_sink, m_final_norm_g, v_c_ctx, v_ada_w, v_ada_b, v_norm_mix_g, v_norm_mlp_g, v_mlp_w1, v_mlp_w2, v_ret_w_in, v_ret_w_out, v_ret_decay_fwd, v_ret_decay_bwd, v_attn_w_in, v_attn_w_out, v_attn_sink, v_final_norm_g):
    given = dict(x=x, c=c, ctx=ctx, c_ctx=c_ctx, ada_w=ada_w, ada_b=ada_b, norm_mix_g=norm_mix_g, norm_mlp_g=norm_mlp_g, mlp_w1=mlp_w1, mlp_w2=mlp_w2, ret_w_in=ret_w_in, ret_w_out=ret_w_out, ret_decay_fwd=ret_decay_fwd, ret_decay_bwd=ret_decay_bwd, attn_w_in=attn_w_in, attn_w_out=attn_w_out, attn_sink=attn_sink, final_norm_g=final_norm_g, loss_target=loss_target, m_c_ctx=m_c_ctx, m_ada_w=m_ada_w, m_ada_b=m_ada_b, m_norm_mix_g=m_norm_mix_g, m_norm_mlp_g=m_norm_mlp_g, m_mlp_w1=m_mlp_w1, m_mlp_w2=m_mlp_w2, m_ret_w_in=m_ret_w_in, m_ret_w_out=m_ret_w_out, m_ret_decay_fwd=m_ret_decay_fwd, m_ret_decay_bwd=m_ret_decay_bwd, m_attn_w_in=m_attn_w_in, m_attn_w_out=m_attn_w_out, m_attn_sink=m_attn_sink, m_final_norm_g=m_final_norm_g, v_c_ctx=v_c_ctx, v_ada_w=v_ada_w, v_ada_b=v_ada_b, v_norm_mix_g=v_norm_mix_g, v_norm_mlp_g=v_norm_mlp_g, v_mlp_w1=v_mlp_w1, v_mlp_w2=v_mlp_w2, v_ret_w_in=v_ret_w_in, v_ret_w_out=v_ret_w_out, v_ret_decay_fwd=v_ret_decay_fwd, v_ret_decay_bwd=v_ret_decay_bwd, v_attn_w_in=v_attn_w_in, v_attn_w_out=v_attn_w_out, v_attn_sink=v_attn_sink, v_final_norm_g=v_final_norm_g)
    weights = {n: given[n] for n in TWIN_WEIGHTS}
    shared = {n: given[n] for n in SHARED_INPUTS}
    per_example = {n: given[n] for n in ['x', 'c', 'ctx']}
    grad_fn = _jax.value_and_grad(_loss, argnums=(0, 1))

    def one_microbatch(ex, loss_target):
        ex = dict(ex)
        diff = ex.pop(TWIN_DIFF_INPUT)
        return grad_fn(weights, diff, {**shared, **ex}, loss_target)

    if N_MICROBATCH == 1:
        loss, (grad_w, grad_x) = one_microbatch(per_example, given["loss_target"])
    else:
        def body(carry, xs):
            loss_sum, grad_sum = carry
            l_k, (gw_k, gx_k) = one_microbatch(xs[0], xs[1])
            with _jax.named_scope("update"):
                return (loss_sum + l_k, _jax.tree.map(_jnp.add, grad_sum, gw_k)), gx_k

        init = (_jnp.zeros((), _jnp.float32), _jax.tree.map(_jnp.zeros_like, weights))
        (loss, grad_w), grad_x = _jax.lax.scan(body, init, (per_example, given["loss_target"]))
    with _jax.named_scope("update"):
        delta_w, new_m, new_v = {}, {}, {}
        for n in TWIN_WEIGHTS:
            delta_w[n], new_m[n], new_v[n] = _adamw(weights[n], grad_w[n], given["m_" + n], given["v_" + n])
    return (loss, grad_x, *[grad_w[n] for n in TWIN_WEIGHTS], *[delta_w[n] for n in TWIN_WEIGHTS],
            *[new_m[n] for n in TWIN_WEIGHTS], *[new_v[n] for n in TWIN_WEIGHTS])
```

```python
import functools

import jax
import jax.numpy as jnp
from jax import lax
from jax.experimental import pallas as pl
from jax.experimental.pallas import tpu as pltpu

F32 = jnp.float32
BF16 = jnp.bfloat16

N_DEV = 8
NORM_EPS = 1e-6
CHUNK = 128
ATT_HEAD_DIM = 128
GRID_W = 64
ROPE_BASE = 10000.0
NEG_INF = -1e30
ADAM_LR, ADAM_B1, ADAM_B2, ADAM_EPS, ADAM_WD, ADAM_STEP = 0.001, 0.9, 0.999, 1e-08, 0.01, 10

V7X_VMEM_LIMIT_BYTES = 56 * 1024 * 1024
LANES = 128
MESH = pl.DeviceIdType.MESH

_NN = (((1,), (0,)), ((), ()))
_NT = (((1,), (1,)), ((), ()))
_TN = (((0,), (0,)), ((), ()))


def _dot(a, b, dn=_NN):
    return lax.dot_general(a, b, dn, preferred_element_type=F32)


def _cparams(*sem):
    return pltpu.CompilerParams(dimension_semantics=sem, vmem_limit_bytes=V7X_VMEM_LIMIT_BYTES)


def _pick(n, pref, mult=LANES):
    if n <= pref:
        return n
    best = None
    for d in range(mult, pref + 1, mult):
        if n % d == 0:
            best = d
    assert best is not None, (n, pref)
    return best


def _silu(x):
    return x * jax.nn.sigmoid(x)


def _mm(a, b, mode, out_dtypes, *, M, N, K, bm, bn, bk, name, b_col0=0, epilogue=None, extras=()):
    assert M % bm == 0 and N % bn == 0 and K % bk == 0 and b_col0 % bn == 0, (name, M, N, K, bm, bn, bk, b_col0)
    nk = K // bk
    c0 = b_col0 // bn
    if mode == "nn":
        a_spec = pl.BlockSpec((bm, bk), lambda i, j, k: (i, k))
        b_spec = pl.BlockSpec((bk, bn), lambda i, j, k: (k, j + c0))
    elif mode == "nt":
        a_spec = pl.BlockSpec((bm, bk), lambda i, j, k: (i, k))
        b_spec = pl.BlockSpec((bn, bk), lambda i, j, k: (j + c0, k))
    else:
        a_spec = pl.BlockSpec((bk, bm), lambda i, j, k: (k, i))
        b_spec = pl.BlockSpec((bk, bn), lambda i, j, k: (k, j + c0))
    dn = {"nn": _NN, "nt": _NT, "tn": _TN}[mode]
    e_specs = [pl.BlockSpec(bs, (lambda i, j, k, f=f: f(i, j))) for (_, bs, f) in extras]
    ne, no = len(extras), len(out_dtypes)

    def body(a_ref, b_ref, *rest):
        e_refs, o_refs = rest[:ne], rest[ne:ne + no]
        i, j, k = pl.program_id(0), pl.program_id(1), pl.program_id(2)

        def finish(acc):
            outs = (acc,) if epilogue is None else epilogue(acc, i, j, *[e[...] for e in e_refs])
            for o_ref, o in zip(o_refs, outs):
                o_ref[...] = o.astype(o_ref.dtype)

        p = _dot(a_ref[...], b_ref[...], dn)
        if nk == 1:
            finish(p)
        else:
            acc_ref = rest[-1]

            @pl.when(k == 0)
            def _():
                acc_ref[...] = p

            @pl.when(k > 0)
            def _():
                acc_ref[...] += p

            @pl.when(k == nk - 1)
            def _():
                finish(acc_ref[...])

    outs = pl.pallas_call(
        body, name=name, grid=(M // bm, N // bn, nk),
        in_specs=[a_spec, b_spec] + e_specs,
        out_specs=[pl.BlockSpec((bm, bn), lambda i, j, k: (i, j)) for _ in out_dtypes],
        out_shape=[jax.ShapeDtypeStruct((M, N), dt) for dt in out_dtypes],
        scratch_shapes=[pltpu.VMEM((bm, bn), F32)] if nk > 1 else [],
        compiler_params=_cparams("parallel", "parallel", "arbitrary"),
    )(a, b, *[e[0] for e in extras])
    return outs


def _rowwise(body, rows, vecs, outs, n_acc, *, R, L, tr, name, acc_width=None):
    assert R % tr == 0 and L % tr == 0, (name, R, L, tr)
    nl = L // tr
    n_regions = 2 if R > L else 1
    n_rows, n_vecs, n_outs = len(rows), len(vecs), len(outs)
    acc_pad = -(-n_acc // 8) * 8 if n_acc else 0

    in_specs = []
    for (_, w, cb, lat_only) in rows:
        if lat_only:
            in_specs.append(pl.BlockSpec((tr, w), lambda i, cb=cb: (jnp.minimum(i, nl - 1), cb)))
        else:
            in_specs.append(pl.BlockSpec((tr, w), lambda i, cb=cb: (i, cb)))
    for v in vecs:
        in_specs.append(pl.BlockSpec(v.shape, lambda i, nd=v.ndim: (0,) * nd))
    out_specs = [pl.BlockSpec((tr, w), lambda i: (i, 0)) for (w, _) in outs]
    out_shape = [jax.ShapeDtypeStruct((R, w), dt) for (w, dt) in outs]
    if n_acc:
        out_specs.append(pl.BlockSpec((None, acc_pad, acc_width), lambda i: (jnp.where(i >= nl, 1, 0), 0, 0)))
        out_shape.append(jax.ShapeDtypeStruct((n_regions, acc_pad, acc_width), F32))

    def kern(*refs):
        i = pl.program_id(0)
        is_ctx = i >= nl
        ins = [r[...] for r in refs[:n_rows + n_vecs]]
        o_refs = refs[n_rows + n_vecs:]
        out_tiles, acc_rows = body(is_ctx, *ins)
        for o_ref, o in zip(o_refs[:n_outs], out_tiles):
            o_ref[...] = o.astype(o_ref.dtype)
        if n_acc:
            acc_ref = o_refs[n_outs]

            @pl.when((i == 0) | (i == nl))
            def _():
                acc_ref[...] = jnp.zeros_like(acc_ref)

            for r, row in enumerate(acc_rows):
                acc_ref[r:r + 1, :] += row

    res = pl.pallas_call(
        kern, name=name, grid=(R // tr,), in_specs=in_specs, out_specs=out_specs, out_shape=out_shape,
        compiler_params=_cparams("arbitrary"),
    )(*[r[0] for r in rows], *vecs)
    return res


def _colsum(x):
    return jnp.sum(x, axis=0, keepdims=True)


def _rms_stats(x):
    r = lax.rsqrt(jnp.mean(x * x, axis=-1, keepdims=True) + NORM_EPS)
    return x * r, r


def _sel(is_ctx, pk, lat_row, ctx_row):
    return jnp.where(is_ctx, pk[ctx_row:ctx_row + 1, :], pk[lat_row:lat_row + 1, :])


def _normmod(x, g, pk, *, R, L, tr, name):
    D = x.shape[-1]

    def body(is_ctx, xt, gv, pkv):
        xh, _ = _rms_stats(xt)
        sh, sc = _sel(is_ctx, pkv, 0, 2), _sel(is_ctx, pkv, 1, 3)
        return ((xh * gv) * (1.0 + sc) + sh,), ()

    return _rowwise(body, [(x, D, 0, False)], [g, pk], [(D, BF16)], 0, R=R, L=L, tr=tr, name=name)[0]


def _normmod_bwd(x_in, da, dx_out, dx_out_lat_only, g, pk, prev, *, R, L, tr, name):
    D = x_in.shape[-1]
    has_prev = prev is not None

    def body(is_ctx, *t):
        if has_prev:
            xt, dat, dxo, mp, gv, pkv, gates = t
        else:
            xt, dat, dxo, gv, pkv = t
        xh, r = _rms_stats(xt)
        sc = _sel(is_ctx, pkv, 1, 3)
        if dx_out_lat_only:
            dxo = jnp.where(is_ctx, 0.0, dxo)
        dn = dat * (1.0 + sc)
        w = dn * gv
        dxi = dxo + r * (w - xh * jnp.mean(w * xh, axis=-1, keepdims=True))
        accs = [_colsum(dat), _colsum(dat * (xh * gv)), _colsum(dn * xh)]
        outs = [dxi]
        if has_prev:
            gate = _sel(is_ctx, gates, 0, 1)
            outs.append(dxi * gate)
            accs.append(_colsum(dxi * mp.astype(F32)))
        return outs, accs

    rows = [(x_in, D, 0, False), (da, D, 0, False), (dx_out, D, 0, dx_out_lat_only)]
    vecs = [g, pk]
    outs = [(D, F32)]
    if has_prev:
        rows.append((prev[0], D, 0, False))
        vecs.append(prev[1])
        outs.append((D, BF16))
    return _rowwise(body, rows, vecs, outs, 4 if has_prev else 3, R=R, L=L, tr=tr, name=name, acc_width=D)


def _loss_head(x4, target, m_prev, gf, gate, *, L, tr, name):
    D = x4.shape[-1]

    def body(is_ctx, xt, tg, mp, gfv, gatev):
        xh, r = _rms_stats(xt)
        e = xh * gfv - tg
        dy = e * (1.0 / D)
        w = dy * gfv
        dx = r * (w - xh * jnp.mean(w * xh, axis=-1, keepdims=True))
        accs = [_colsum(e * e) * (0.5 / D), _colsum(dy * xh), _colsum(dx * mp.astype(F32))]
        return (dx, dx * gatev), accs

    return _rowwise(body, [(x4, D, 0, False), (target, D, 0, False), (m_prev, D, 0, False)], [gf, gate],
                    [(D, F32), (D, BF16)], 3, R=L, L=L, tr=tr, name=name, acc_width=D)


def _decays(lgh, rev):
    ii = lax.broadcasted_iota(jnp.int32, (CHUNK, CHUNK), 0)
    jj = lax.broadcasted_iota(jnp.int32, (CHUNK, CHUNK), 1)
    ri = lax.broadcasted_iota(jnp.int32, (CHUNK, 1), 0).astype(F32)
    diff = (jj - ii if rev else ii - jj)
    amat = jnp.where(diff >= 0, jnp.exp(lgh * jnp.maximum(diff, 0).astype(F32)), 0.0)
    pos = (CHUNK - ri) if rev else (ri + 1.0)
    bq = jnp.exp(lgh * pos)
    bk = jnp.exp(lgh * (CHUNK - pos))
    return amat, bq, bk, pos


def _ret_specs(H, nT, nL, rev, backward):
    def chunk(s):
        s = (nT - 1 - s) if backward else s
        return (nT - 1 - s) if rev else (s + nL) % nT

    def step(s):
        return (nT - 1 - s) if backward else s

    return chunk, step


def _ret_fwd(qk, vg, lg, *, T, L, H, rev, name):
    dk, dv = 2 * LANES, 4 * LANES
    nT, nL = T // CHUNK, L // CHUNK
    chunk, step = _ret_specs(H, nT, nL, rev, False)

    def body(lg_ref, q_ref, k_ref, v_ref, o_ref, st_ref, s_scr):
        h, s = pl.program_id(0), pl.program_id(1)
        lgh = lg_ref[0, h]
        amat, bq, bk, _ = _decays(lgh, rev)

        @pl.when(s == 0)
        def _():
            s_scr[...] = jnp.zeros_like(s_scr)

        q, k, v = q_ref[...], k_ref[...], v_ref[...]
        st = s_scr[...]
        st_ref[...] = st
        scores = _dot(q, k, _NT) * amat
        o_ref[...] = _dot(scores.astype(BF16), v) + _dot(q, st.astype(BF16)) * bq
        kd = (k.astype(F32) * bk).astype(BF16)
        s_scr[...] = st * jnp.exp(lgh * CHUNK) + _dot(kd, v, _TN)

    return pl.pallas_call(
        body, name=name, grid=(H, nT),
        in_specs=[pl.BlockSpec(memory_space=pltpu.SMEM),
                  pl.BlockSpec((CHUNK, dk), lambda h, s: (chunk(s), h)),
                  pl.BlockSpec((CHUNK, dk), lambda h, s: (chunk(s), H + h)),
                  pl.BlockSpec((CHUNK, dv), lambda h, s: (chunk(s), h))],
        out_specs=[pl.BlockSpec((CHUNK, dv), lambda h, s: (chunk(s), h)),
                   pl.BlockSpec((None, None, dk, dv), lambda h, s: (h, s, 0, 0))],
        out_shape=[jax.ShapeDtypeStruct((T, H * dv), F32), jax.ShapeDtypeStruct((H, nT, dk, dv), F32)],
        scratch_shapes=[pltpu.VMEM((dk, dv), F32)],
        compiler_params=_cparams("parallel", "arbitrary"),
    )(lg, qk, qk, vg)


def _ret_bwd(qk, vg, do, states, lg, *, T, L, H, rev, name):
    dk, dv = 2 * LANES, 4 * LANES
    nT, nL = T // CHUNK, L // CHUNK
    chunk, step = _ret_specs(H, nT, nL, rev, True)

    def body(lg_ref, q_ref, k_ref, v_ref, do_ref, st_ref, dq_ref, dk_ref, dv_ref, dlg_ref, ds_scr):
        h, s = pl.program_id(0), pl.program_id(1)
        lgh = lg_ref[0, h]
        amat, bq, bk, pos = _decays(lgh, rev)

        @pl.when(s == 0)
        def _():
            ds_scr[...] = jnp.zeros_like(ds_scr)
            dlg_ref[...] = jnp.zeros_like(dlg_ref)

        q, k, v, dob = q_ref[...], k_ref[...], v_ref[...], do_ref[...]
        st = st_ref[...]
        stb = st.astype(BF16)
        ds_new = ds_scr[...]
        dsb = ds_new.astype(BF16)
        qf, kf = q.astype(F32), k.astype(F32)
        scores = (_dot(q, k, _NT) * amat).astype(BF16)
        dqk = (_dot(dob, v, _NT) * amat).astype(BF16)
        dq = _dot(dqk, k) + _dot(dob, stb, _NT) * bq
        dkk = _dot(dqk, q, _TN) + _dot(v, dsb, _NT) * bk
        kd = (kf * bk).astype(BF16)
        dvv = _dot(scores, dob, _TN) + _dot(kd, dsb)
        dod = (dob.astype(F32) * bq).astype(BF16)
        ds_prev = ds_new * jnp.exp(lgh * CHUNK) + _dot(q, dod, _TN)
        ds_scr[...] = ds_prev
        dq_ref[...] = dq
        dk_ref[...] = dkk
        dv_ref[...] = dvv
        dlg = (jnp.sum(pos * jnp.sum(qf * dq - kf * dkk, axis=-1, keepdims=True))
               + CHUNK * jnp.sum(ds_prev * st))
        dlg_ref[...] += dlg

    return pl.pallas_call(
        body, name=name, grid=(H, nT),
        in_specs=[pl.BlockSpec(memory_space=pltpu.SMEM),
                  pl.BlockSpec((CHUNK, dk), lambda h, s: (chunk(s), h)),
                  pl.BlockSpec((CHUNK, dk), lambda h, s: (chunk(s), H + h)),
                  pl.BlockSpec((CHUNK, dv), lambda h, s: (chunk(s), h)),
                  pl.BlockSpec((CHUNK, dv), lambda h, s: (chunk(s), h)),
                  pl.BlockSpec((None, None, dk, dv), lambda h, s: (h, step(s), 0, 0))],
        out_specs=[pl.BlockSpec((CHUNK, dk), lambda h, s: (chunk(s), h)),
                   pl.BlockSpec((CHUNK, dk), lambda h, s: (chunk(s), h)),
                   pl.BlockSpec((CHUNK, dv), lambda h, s: (chunk(s), h)),
                   pl.BlockSpec((None, 8, LANES), lambda h, s: (h, 0, 0))],
        out_shape=[jax.ShapeDtypeStruct((T, H * dk), F32), jax.ShapeDtypeStruct((T, H * dk), F32),
                   jax.ShapeDtypeStruct((T, H * dv), F32), jax.ShapeDtypeStruct((H, 8, LANES), F32)],
        scratch_shapes=[pltpu.VMEM((dk, dv), F32)],
        compiler_params=_cparams("parallel", "arbitrary"),
    )(lg, qk, qk, vg, do, states)


def _readout(o_f, o_b, vg, *, T, L, H, tr, name):
    dv = 4 * LANES
    W = H * dv

    def body(is_ctx, of, ob, g):
        o = of + ob
        parts = []
        for h in range(H):
            oh = o[:, h * dv:(h + 1) * dv]
            parts.append(oh * lax.rsqrt(jnp.mean(oh * oh, axis=-1, keepdims=True) + NORM_EPS))
        y = jnp.concatenate(parts, axis=1)
        return (_silu(g.astype(F32)) * y,), ()

    return _rowwise(body, [(o_f, W, 0, False), (o_b, W, 0, False), (vg, W, 1, False)], [], [(W, BF16)], 0,
                    R=T, L=L, tr=tr, name=name)[0]


def _readout_bwd(dz, o_f, o_b, vg, *, T, L, H, tr, name):
    dv = 4 * LANES
    W = H * dv

    def body(is_ctx, dzt, of, ob, g):
        o = of + ob
        gf = g.astype(F32)
        sg = jax.nn.sigmoid(gf)
        dzf = dzt.astype(F32)
        dy = dzf * (gf * sg)
        ys, dos = [], []
        for h in range(H):
            sl = slice(h * dv, (h + 1) * dv)
            oh, dyh = o[:, sl], dy[:, sl]
            r = lax.rsqrt(jnp.mean(oh * oh, axis=-1, keepdims=True) + NORM_EPS)
            yh = oh * r
            ys.append(yh)
            dos.append(r * (dyh - yh * jnp.mean(dyh * yh, axis=-1, keepdims=True)))
        y = jnp.concatenate(ys, axis=1)
        dg = dzf * y * (sg * (1.0 + gf * (1.0 - sg)))
        return (jnp.concatenate(dos, axis=1), dg), ()

    return _rowwise(body, [(dz, W, 0, False), (o_f, W, 0, False), (o_b, W, 0, False), (vg, W, 1, False)], [],
                    [(W, BF16), (W, BF16)], 0, R=T, L=L, tr=tr, name=name)


def _ret_assemble(dq_f, dq_b, dk_f, dk_b, dv_f, dv_b, dg, cos, sin, *, T, L, H, tr, name):
    dk, dv = 2 * LANES, 4 * LANES
    Wq, Wv = H * dk, H * dv
    kscale = float(dk) ** -0.5

    def unrope(d, c, s_, scale):
        parts = []
        for h in range(H):
            d1, d2 = d[:, h * dk:h * dk + LANES], d[:, h * dk + LANES:(h + 1) * dk]
            parts += [(d1 * c + d2 * s_) * scale, (d2 * c - d1 * s_) * scale]
        return jnp.concatenate(parts, axis=1)

    def body(is_ctx, qf, qb, kf, kb, vf, vb, g, c, s_):
        dq = unrope(qf + qb, c, s_, 1.0)
        dkk = unrope(kf + kb, c, s_, kscale)
        return (jnp.concatenate([dq.astype(BF16), dkk.astype(BF16), (vf + vb).astype(BF16), g], axis=1),), ()

    rows = [(dq_f, Wq, 0, False), (dq_b, Wq, 0, False), (dk_f, Wq, 0, False), (dk_b, Wq, 0, False),
            (dv_f, Wv, 0, False), (dv_b, Wv, 0, False), (dg, Wv, 0, False),
            (cos, LANES, 0, False), (sin, LANES, 0, False)]
    return _rowwise(body, rows, [], [(2 * Wq + 2 * Wv, BF16)], 0, R=T, L=L, tr=tr, name=name)[0]


def _swap32(x):
    n = x.shape[-1]
    lane = lax.broadcasted_iota(jnp.int32, x.shape, x.ndim - 1)
    return jnp.where(lane % 64 < 32, pltpu.roll(x, n - 32, x.ndim - 1), pltpu.roll(x, 32, x.ndim - 1))


def _band_masks(n, nb):
    ii = lax.broadcasted_iota(jnp.int32, (CHUNK, CHUNK), 0)
    jj = lax.broadcasted_iota(jnp.int32, (CHUNK, CHUNK), 1)
    return (jj >= ii) & (n > 0), (jj <= ii) & (n < nb - 1)


def _attn_kv_specs(L, CTX, nb):
    blk = lambda f: pl.BlockSpec((CHUNK, LANES), lambda h, n: (f(n), h))
    prev_, cur_, next_ = (lambda n: jnp.maximum(n - 1, 0)), (lambda n: n), (lambda n: jnp.minimum(n + 1, nb - 1))
    ctx_spec = pl.BlockSpec((CTX, LANES), lambda h, n: (L // CTX, h))
    return [blk(prev_), blk(cur_), blk(next_), ctx_spec]


def _attn_fwd(q, k, v, sink, *, L, CTX, Hkv, G, name):
    nb = L // CHUNK
    scale = float(ATT_HEAD_DIM) ** -0.5
    kvs = _attn_kv_specs(L, CTX, nb)

    def body(sink_ref, q_ref, kp, kc, kn, kx, vp, vc, vn, vx, o_ref, lse_ref):
        h, n = pl.program_id(0), pl.program_id(1)
        mp, mn = _band_masks(n, nb)
        for g in range(G):
            qg = q_ref[:, g * LANES:(g + 1) * LANES]
            sp = jnp.where(mp, _dot(qg, kp[...], _NT) * scale, NEG_INF)
            sc = _dot(qg, kc[...], _NT) * scale
            sn = jnp.where(mn, _dot(qg, kn[...], _NT) * scale, NEG_INF)
            sx = _dot(qg, kx[...], _NT) * scale
            sk = sink_ref[0, h * G + g]
            rmax = lambda t: jnp.max(t, axis=-1, keepdims=True)
            m = jnp.maximum(jnp.maximum(jnp.maximum(rmax(sp), rmax(sc)), jnp.maximum(rmax(sn), rmax(sx))), sk)
            pp, pc, pn, px = jnp.exp(sp - m), jnp.exp(sc - m), jnp.exp(sn - m), jnp.exp(sx - m)
            rsum = lambda t: jnp.sum(t, axis=-1, keepdims=True)
            den = rsum(pp) + rsum(pc) + rsum(pn) + rsum(px) + jnp.exp(sk - m)
            o = (_dot(pp.astype(BF16), vp[...]) + _dot(pc.astype(BF16), vc[...])
                 + _dot(pn.astype(BF16), vn[...]) + _dot(px.astype(BF16), vx[...]))
            o_ref[:, g * LANES:(g + 1) * LANES] = (o / den).astype(o_ref.dtype)
            lse_ref[:, g:g + 1] = m + jnp.log(den)

    return pl.pallas_call(
        body, name=name, grid=(Hkv, nb),
        in_specs=[pl.BlockSpec(memory_space=pltpu.SMEM),
                  pl.BlockSpec((CHUNK, G * LANES), lambda h, n: (n, h))] + kvs + kvs,
        out_specs=[pl.BlockSpec((CHUNK, G * LANES), lambda h, n: (n, h)),
                   pl.BlockSpec((None, CHUNK, G), lambda h, n: (h, n, 0))],
        out_shape=[jax.ShapeDtypeStruct((L, Hkv * G * LANES), BF16), jax.ShapeDtypeStruct((Hkv, L, G), F32)],
        compiler_params=_cparams("parallel", "parallel"),
    )(sink, q, k, k, k, k, v, v, v, v)


def _attn_bwd_q(q, k, v, do, lse, sink, *, L, CTX, Hkv, G, name):
    nb = L // CHUNK
    scale = float(ATT_HEAD_DIM) ** -0.5
    kvs = _attn_kv_specs(L, CTX, nb)
    qspec = pl.BlockSpec((CHUNK, G * LANES), lambda h, n: (n, h))
    rowspec = pl.BlockSpec((None, CHUNK, G), lambda h, n: (h, n, 0))

    def body(sink_ref, q_ref, do_ref, lse_ref, kp, kc, kn, kx, vp, vc, vn, vx,
             dq_ref, dl_ref, dkx_ref, dvx_ref, dsk_ref):
        h, n = pl.program_id(0), pl.program_id(1)
        mp, mn = _band_masks(n, nb)

        @pl.when(n == 0)
        def _():
            dkx_ref[...] = jnp.zeros_like(dkx_ref)
            dvx_ref[...] = jnp.zeros_like(dvx_ref)
            dsk_ref[...] = jnp.zeros_like(dsk_ref)

        for g in range(G):
            sl = slice(g * LANES, (g + 1) * LANES)
            qg, dog = q_ref[:, sl], do_ref[:, sl]
            lse_g = lse_ref[:, g:g + 1]
            sk = sink_ref[0, h * G + g]
            ks, vs, masks = (kp, kc, kn, kx), (vp, vc, vn, vx), (mp, None, mn, None)
            ps, dps = [], []
            for kr, vr, msk in zip(ks, vs, masks):
                s_ = _dot(qg, kr[...], _NT) * scale
                p = jnp.exp(s_ - lse_g)
                if msk is not None:
                    p = jnp.where(msk, p, 0.0)
                ps.append(p)
                dps.append(_dot(dog, vr[...], _NT))
            delta = sum(jnp.sum(p * dp, axis=-1, keepdims=True) for p, dp in zip(ps, dps))
            dq = jnp.zeros((CHUNK, LANES), F32)
            for idx, (p, dp, kr) in enumerate(zip(ps, dps, ks)):
                ds = (p * (dp - delta) * scale).astype(BF16)
                dq = dq + _dot(ds, kr[...])
                if idx == 3:
                    dkx_ref[...] += _dot(ds, qg, _TN)
                    dvx_ref[...] += _dot(p.astype(BF16), dog, _TN)
            dq_ref[:, sl] = dq
            dl_ref[:, g:g + 1] = delta
            dsk_ref[g:g + 1, :] += -jnp.sum(jnp.exp(sk - lse_g) * delta)

    return pl.pallas_call(
        body, name=name, grid=(Hkv, nb),
        in_specs=[pl.BlockSpec(memory_space=pltpu.SMEM), qspec, qspec, rowspec] + kvs + kvs,
        out_specs=[qspec, rowspec,
                   pl.BlockSpec((CTX, LANES), lambda h, n: (0, h)), pl.BlockSpec((CTX, LANES), lambda h, n: (0, h)),
                   pl.BlockSpec((None, 8, LANES), lambda h, n: (h, 0, 0))],
        out_shape=[jax.ShapeDtypeStruct((L, Hkv * G * LANES), F32), jax.ShapeDtypeStruct((Hkv, L, G), F32),
                   jax.ShapeDtypeStruct((CTX, Hkv * LANES), F32), jax.ShapeDtypeStruct((CTX, Hkv * LANES), F32),
                   jax.ShapeDtypeStruct((Hkv, 8, LANES), F32)],
        compiler_params=_cparams("parallel", "arbitrary"),
    )(sink, q, do, lse, k, k, k, k, v, v, v, v)


def _attn_bwd_kv(q, k, v, do, lse, delta, *, L, Hkv, G, name):
    nb = L // CHUNK
    scale = float(ATT_HEAD_DIM) ** -0.5
    fs = [(lambda n: jnp.maximum(n - 1, 0)), (lambda n: n), (lambda n: jnp.minimum(n + 1, nb - 1))]
    qspecs = [pl.BlockSpec((CHUNK, G * LANES), lambda h, n, f=f: (f(n), h)) for f in fs]
    rspecs = [pl.BlockSpec((None, CHUNK, G), lambda h, n, f=f: (h, f(n), 0)) for f in fs]
    kspec = pl.BlockSpec((CHUNK, LANES), lambda h, n: (n, h))

    def body(k_ref, v_ref, q0, q1, q2, d0, d1, d2, l0, l1, l2, e0, e1, e2, dk_ref, dv_ref):
        n = pl.program_id(1)
        ii = lax.broadcasted_iota(jnp.int32, (CHUNK, CHUNK), 0)
        jj = lax.broadcasted_iota(jnp.int32, (CHUNK, CHUNK), 1)
        masks = ((jj <= ii) & (n > 0), None, (jj >= ii) & (n < nb - 1))
        kb, vb = k_ref[...], v_ref[...]
        dk = jnp.zeros((CHUNK, LANES), F32)
        dv = jnp.zeros((CHUNK, LANES), F32)
        for qr, dr, lr, er, msk in zip((q0, q1, q2), (d0, d1, d2), (l0, l1, l2), (e0, e1, e2), masks):
            for g in range(G):
                sl = slice(g * LANES, (g + 1) * LANES)
                qg, dog = qr[:, sl], dr[:, sl]
                p = jnp.exp(_dot(qg, kb, _NT) * scale - lr[:, g:g + 1])
                if msk is not None:
                    p = jnp.where(msk, p, 0.0)
                ds = (p * (_dot(dog, vb, _NT) - er[:, g:g + 1]) * scale).astype(BF16)
                dk = dk + _dot(ds, qg, _TN)
                dv = dv + _dot(p.astype(BF16), dog, _TN)
        dk_ref[...] = dk
        dv_ref[...] = dv

    return pl.pallas_call(
        body, name=name, grid=(Hkv, nb),
        in_specs=[kspec, kspec] + qspecs + qspecs + rspecs + rspecs,
        out_specs=[kspec, kspec],
        out_shape=[jax.ShapeDtypeStruct((L, Hkv * LANES), F32), jax.ShapeDtypeStruct((L, Hkv * LANES), F32)],
        compiler_params=_cparams("parallel", "parallel"),
    )(k, v, q, q, q, do, do, do, lse, lse, lse, delta, delta, delta)


def _attn_assemble(dq, dk_lat, dv_lat, dk_ctx, dv_ctx, cos, sin, *, T, L, CTX, Hq, Hkv, tr, name):
    Wq, Wk = Hq * LANES, Hkv * LANES
    ctx_blocks = CTX // tr
    nl = L // tr

    def unrope(d, c, s_, heads):
        return d * jnp.tile(c, (1, heads)) + _swap32(d * jnp.tile(s_, (1, heads)))

    def body(is_ctx, dqt, dkl, dvl, dkc, dvc, c, s_):
        dq_ = jnp.where(is_ctx, 0.0, unrope(dqt, c, s_, Hq))
        dk_ = unrope(jnp.where(is_ctx, dkc, dkl), c, s_, Hkv)
        dv_ = jnp.where(is_ctx, dvc, dvl)
        return (jnp.concatenate([dq_, dk_, dv_], axis=1),), ()

    def ctx_map(i):
        return (jnp.clip(i - nl, 0, ctx_blocks - 1), 0)

    assert T % tr == 0 and L % tr == 0 and CTX % tr == 0
    in_specs = [pl.BlockSpec((tr, Wq), lambda i: (jnp.minimum(i, nl - 1), 0)),
                pl.BlockSpec((tr, Wk), lambda i: (jnp.minimum(i, nl - 1), 0)),
                pl.BlockSpec((tr, Wk), lambda i: (jnp.minimum(i, nl - 1), 0)),
                pl.BlockSpec((tr, Wk), ctx_map), pl.BlockSpec((tr, Wk), ctx_map),
                pl.BlockSpec((tr, LANES), lambda i: (i, 0)), pl.BlockSpec((tr, LANES), lambda i: (i, 0))]

    def kern(dq_r, dkl_r, dvl_r, dkc_r, dvc_r, c_r, s_r, o_ref):
        is_ctx = pl.program_id(0) >= nl
        (out,), _ = body(is_ctx, dq_r[...], dkl_r[...], dvl_r[...], dkc_r[...], dvc_r[...], c_r[...], s_r[...])
        o_ref[...] = out.astype(o_ref.dtype)

    return pl.pallas_call(
        kern, name=name, grid=(T // tr,), in_specs=in_specs,
        out_specs=pl.BlockSpec((tr, Wq + 2 * Wk), lambda i: (i, 0)),
        out_shape=jax.ShapeDtypeStruct((T, Wq + 2 * Wk), BF16),
        compiler_params=_cparams("parallel"),
    )(dq, dk_lat, dv_lat, dk_ctx, dv_ctx, cos, sin)


def _my_place():
    x, y, c = lax.axis_index("x"), lax.axis_index("y"), lax.axis_index("c")
    return x, y, c


def _all_gather_small(v, *, name):
    R, C = v.shape

    def body(x_ref, out_ref, send_sems, recv_sems, local_sem):
        x, y, c = _my_place()
        me, sibling = (x, y, c), (x, y, 1 - c)
        chips = [(1 - x, y), (x, 1 - y), (1 - x, 1 - y)]

        def slot(px, py, pc):
            return out_ref.at[4 * px + 2 * py + pc]

        def copy(k, block, to, src=None):
            return pltpu.make_async_remote_copy(
                src_ref=slot(*block) if src is None else src, dst_ref=slot(*block),
                send_sem=send_sems.at[k], recv_sem=recv_sems.at[k], device_id=to, device_id_type=MESH)

        mine = pltpu.make_async_copy(x_ref, slot(*me), local_sem)
        mine.start()
        first = [copy(0, me, sibling, src=x_ref)]
        first += [copy(1 + j, me, (*chip, c), src=x_ref) for j, chip in enumerate(chips)]
        for cp in first:
            cp.start()
        passed = [copy(4 + j, (*chip, c), sibling) for j, chip in enumerate(chips)]
        for j, chip in enumerate(chips):
            copy(1 + j, (*chip, c), me).wait_recv()
            passed[j].start()
        copy(0, sibling, me).wait_recv()
        for j, chip in enumerate(chips):
            copy(4 + j, (*chip, 1 - c), me).wait_recv()
        for cp in first + passed:
            cp.wait_send()
        mine.wait()

    return pl.pallas_call(
        body, name=name, out_shape=jax.ShapeDtypeStruct((N_DEV, R, C), v.dtype),
        in_specs=[pl.BlockSpec(memory_space=pltpu.VMEM)], out_specs=pl.BlockSpec(memory_space=pltpu.VMEM),
        scratch_shapes=[pltpu.SemaphoreType.DMA((7,)), pltpu.SemaphoreType.DMA((7,)), pltpu.SemaphoreType.DMA],
    )(v)


def _shard_slice(ref, axis, idx, size):
    if axis == 1:
        return ref.at[:, pl.ds(idx * size, size), :]
    return ref.at[:, :, pl.ds(idx * size, size)]


def _all_gather_weights(shards, axes, *, name):
    nt = len(shards)
    sizes = [s.shape[a] for s, a in zip(shards, axes)]
    out_shape = []
    for s, a in zip(shards, axes):
        shp = list(s.shape)
        shp[a] *= N_DEV
        out_shape.append(jax.ShapeDtypeStruct(tuple(shp), s.dtype))

    def body(*refs):
        ins, outs = refs[:nt], refs[nt:2 * nt]
        send_sems, recv_sems, local_sems = refs[2 * nt:]
        x, y, c = _my_place()
        me, sibling = (x, y, c), (x, y, 1 - c)
        chips = [(1 - x, y), (x, 1 - y), (1 - x, 1 - y)]
        all_sends = []
        locals_ = []
        for t in range(nt):
            def slot(px, py, pc, t=t):
                return _shard_slice(outs[t], axes[t], 4 * px + 2 * py + pc, sizes[t])

            def copy(k, block, to, src=None, t=t, slot=slot):
                return pltpu.make_async_remote_copy(
                    src_ref=slot(*block) if src is None else src, dst_ref=slot(*block),
                    send_sem=send_sems.at[t, k], recv_sem=recv_sems.at[t, k], device_id=to, device_id_type=MESH)

            mine = pltpu.make_async_copy(ins[t], slot(*me), local_sems.at[t])
            mine.start()
            locals_.append(mine)
            first = [copy(0, me, sibling, src=ins[t])]
            first += [copy(1 + j, me, (*chip, c), src=ins[t]) for j, chip in enumerate(chips)]
            for cp in first:
                cp.start()
            all_sends += first
        for t in range(nt):
            def slot(px, py, pc, t=t):
                return _shard_slice(outs[t], axes[t], 4 * px + 2 * py + pc, sizes[t])

            def copy(k, block, to, t=t, slot=slot):
                return pltpu.make_async_remote_copy(
                    src_ref=slot(*block), dst_ref=slot(*block),
                    send_sem=send_sems.at[t, k], recv_sem=recv_sems.at[t, k], device_id=to, device_id_type=MESH)

            passed = [copy(4 + j, (*chip, c), sibling) for j, chip in enumerate(chips)]
            for j, chip in enumerate(chips):
                copy(1 + j, (*chip, c), me).wait_recv()
                passed[j].start()
            all_sends += passed
        for t in range(nt):
            def slot(px, py, pc, t=t):
                return _shard_slice(outs[t], axes[t], 4 * px + 2 * py + pc, sizes[t])

            def copy(k, block, to, t=t, slot=slot):
                return pltpu.make_async_remote_copy(
                    src_ref=slot(*block), dst_ref=slot(*block),
                    send_sem=send_sems.at[t, k], recv_sem=recv_sems.at[t, k], device_id=to, device_id_type=MESH)

            copy(0, sibling, me).wait_recv()
            for j, chip in enumerate(chips):
                copy(4 + j, (*chip, 1 - c), me).wait_recv()
        for cp in all_sends:
            cp.wait_send()
        for mine in locals_:
            mine.wait()

    return pl.pallas_call(
        body, name=name, out_shape=out_shape,
        in_specs=[pl.BlockSpec(memory_space=pl.ANY)] * nt, out_specs=[pl.BlockSpec(memory_space=pl.ANY)] * nt,
        scratch_shapes=[pltpu.SemaphoreType.DMA((nt, 7)), pltpu.SemaphoreType.DMA((nt, 7)),
                        pltpu.SemaphoreType.DMA((nt,))],
    )(*shards)


def _exchange_grads(grads, axes, *, name):
    nt = len(grads)
    sizes = [g.shape[a] // N_DEV for g, a in zip(grads, axes)]
    out_shape = []
    for g, a, sz in zip(grads, axes, sizes):
        shp = list(g.shape)
        shp[a] = sz
        out_shape.append(jax.ShapeDtypeStruct((N_DEV, *shp), g.dtype))

    def body(*refs):
        ins, outs = refs[:nt], refs[nt:2 * nt]
        send_sems, recv_sems, local_sems = refs[2 * nt:]
        x, y, c = _my_place()
        my_idx = 4 * x + 2 * y + c

        def peer(r):
            px = (1 - x) if r & 4 else x
            py = (1 - y) if r & 2 else y
            pc = (1 - c) if r & 1 else c
            return (px, py, pc)

        copies, locals_ = [], []
        for t in range(nt):
            mine = pltpu.make_async_copy(_shard_slice(ins[t], axes[t], my_idx, sizes[t]), outs[t].at[my_idx],
                                         local_sems.at[t])
            mine.start()
            locals_.append(mine)
            for r in range(1, N_DEV):
                p = peer(r)
                p_idx = 4 * p[0] + 2 * p[1] + p[2]
                cp = pltpu.make_async_remote_copy(
                    src_ref=_shard_slice(ins[t], axes[t], p_idx, sizes[t]), dst_ref=outs[t].at[my_idx],
                    send_sem=send_sems.at[t, r - 1], recv_sem=recv_sems.at[t, r - 1], device_id=p, device_id_type=MESH)
                cp.start()
                copies.append((cp, t, r, p_idx))
        for cp, t, r, p_idx in copies:
            pltpu.make_async_remote_copy(
                src_ref=_shard_slice(ins[t], axes[t], p_idx, sizes[t]), dst_ref=outs[t].at[p_idx],
                send_sem=send_sems.at[t, r - 1], recv_sem=recv_sems.at[t, r - 1], device_id=peer(r),
                device_id_type=MESH).wait_recv()
        for cp, _, _, _ in copies:
            cp.wait_send()
        for mine in locals_:
            mine.wait()

    return pl.pallas_call(
        body, name=name, out_shape=out_shape,
        in_specs=[pl.BlockSpec(memory_space=pl.ANY)] * nt, out_specs=[pl.BlockSpec(memory_space=pl.ANY)] * nt,
        scratch_shapes=[pltpu.SemaphoreType.DMA((nt, 7)), pltpu.SemaphoreType.DMA((nt, 7)),
                        pltpu.SemaphoreType.DMA((nt,))],
    )(*grads)


def _adamw_math(w, g, m, v):
    m = ADAM_B1 * m + (1.0 - ADAM_B1) * g
    v = ADAM_B2 * v + (1.0 - ADAM_B2) * (g * g)
    m_hat = m / (1.0 - ADAM_B1 ** ADAM_STEP)
    v_hat = v / (1.0 - ADAM_B2 ** ADAM_STEP)
    delta = -ADAM_LR * (m_hat / (jnp.sqrt(v_hat) + ADAM_EPS) + ADAM_WD * w)
    return delta, m, v


def _adamw_sharded(w, m, v, recv, *, name):
    shp = w.shape
    rows, cols = shp[0] * shp[1], shp[2]
    tr = _pick(rows, 128, 8)
    w2, m2, v2 = (t.reshape(rows, cols) for t in (w, m, v))
    r2 = recv.reshape(N_DEV, rows, cols)

    def body(w_ref, m_ref, v_ref, r_ref, g_ref, d_ref, nm_ref, nv_ref):
        g = r_ref[0].astype(F32)
        for d in range(1, N_DEV):
            g = g + r_ref[d].astype(F32)
        delta, nm, nv = _adamw_math(w_ref[...], g, m_ref[...], v_ref[...])
        g_ref[...], d_ref[...], nm_ref[...], nv_ref[...] = g, delta, nm, nv

    spec = pl.BlockSpec((tr, cols), lambda i: (i, 0))
    outs = pl.pallas_call(
        body, name=name, grid=(rows // tr,),
        in_specs=[spec, spec, spec, pl.BlockSpec((N_DEV, tr, cols), lambda i: (0, i, 0))],
        out_specs=[spec] * 4, out_shape=[jax.ShapeDtypeStruct((rows, cols), F32)] * 4,
        compiler_params=_cparams("parallel"),
    )(w2, m2, v2, r2)
    return [o.reshape(shp) for o in outs]


def _adamw_flat(w, g, m, v, *, name):
    def body(w_ref, g_ref, m_ref, v_ref, d_ref, nm_ref, nv_ref):
        d_ref[...], nm_ref[...], nv_ref[...] = _adamw_math(w_ref[...], g_ref[...], m_ref[...], v_ref[...])

    spec = pl.BlockSpec(memory_space=pltpu.VMEM)
    return pl.pallas_call(body, name=name, in_specs=[spec] * 4, out_specs=[spec] * 3,
                          out_shape=[jax.ShapeDtypeStruct(w.shape, F32)] * 3)(w, g, m, v)


def _sum_devices(a, *, name):
    def body(a_ref, o_ref):
        s = a_ref[0]
        for d in range(1, N_DEV):
            s = s + a_ref[d]
        o_ref[...] = s

    spec = pl.BlockSpec(memory_space=pltpu.VMEM)
    return pl.pallas_call(body, name=name, in_specs=[spec], out_specs=spec,
                          out_shape=jax.ShapeDtypeStruct(a.shape[1:], F32))(a)


def _ada_mods(c16, ada_w, ada_b_cols, *, name):
    nl, D, cols = ada_w.shape
    bn = _pick(cols, 512)

    def body(c_ref, w_ref, b_ref, o_ref):
        cond = _silu(c_ref[...]).astype(BF16)
        o_ref[...] = _dot(cond, w_ref[...].astype(BF16)) + b_ref[...]

    return pl.pallas_call(
        body, name=name, grid=(nl, cols // bn),
        in_specs=[pl.BlockSpec((16, D), lambda l, j: (0, 0)), pl.BlockSpec((None, D, bn), lambda l, j: (l, 0, j)),
                  pl.BlockSpec((None, 1, bn), lambda l, j: (l, 0, j))],
        out_specs=pl.BlockSpec((None, 16, bn), lambda l, j: (l, 0, j)),
        out_shape=jax.ShapeDtypeStruct((nl, 16, cols), F32),
        compiler_params=_cparams("parallel", "parallel"),
    )(c16, ada_w, ada_b_cols)


def _ada_bwd(cond_t, dmod, w, m, v, *, name):
    nl, D, cols = w.shape
    tr = _pick(D, 256, 8)

    def body(ct_ref, dm_ref, w_ref, m_ref, v_ref, g_ref, d_ref, nm_ref, nv_ref, dc_ref):
        ct, dm, wt = ct_ref[...], dm_ref[...], w_ref[...]
        g = ct[:, 0:1] * dm[0:1, :]
        for r in range(1, N_DEV + 1):
            g = g + ct[:, r:r + 1] * dm[r:r + 1, :]
        delta, nm, nv = _adamw_math(wt, g, m_ref[...], v_ref[...])
        g_ref[...], d_ref[...], nm_ref[...], nv_ref[...] = g, delta, nm, nv
        dc_ref[...] = jnp.sum(wt * dm[N_DEV:N_DEV + 1, :], axis=-1, keepdims=True)

    wspec = pl.BlockSpec((None, tr, cols), lambda l, i: (l, i, 0))
    return pl.pallas_call(
        body, name=name, grid=(nl, D // tr),
        in_specs=[pl.BlockSpec((tr, 16), lambda l, i: (i, 0)), pl.BlockSpec((None, 16, cols), lambda l, i: (l, 0, 0)),
                  wspec, wspec, wspec],
        out_specs=[wspec] * 4 + [pl.BlockSpec((None, tr, 1), lambda l, i: (l, i, 0))],
        out_shape=[jax.ShapeDtypeStruct((nl, D, cols), F32)] * 4 + [jax.ShapeDtypeStruct((nl, D, 1), F32)],
        compiler_params=_cparams("parallel", "parallel"),
    )(cond_t, dmod, w, m, v)


def _rope_tables(L, CTX):
    def angles(pos, dim):
        inv_freq = ROPE_BASE ** (-jnp.arange(0, dim, 2, dtype=F32) / dim)
        return pos.astype(F32)[:, None] * inv_freq[None, :]

    def pad(cos, sin):
        return (jnp.concatenate([cos, jnp.ones((CTX, LANES), F32)], 0),
                jnp.concatenate([sin, jnp.zeros((CTX, LANES), F32)], 0))

    ret = angles(jnp.arange(L), 2 * LANES)
    ret_cs = pad(jnp.cos(ret), jnp.sin(ret))
    rows = angles(jnp.arange(L) // GRID_W, ATT_HEAD_DIM // 2)
    cols = angles(jnp.arange(L) % GRID_W, ATT_HEAD_DIM // 2)
    cos = jnp.concatenate([jnp.cos(rows)] * 2 + [jnp.cos(cols)] * 2, axis=1)
    sin = jnp.concatenate([-jnp.sin(rows), jnp.sin(rows), -jnp.sin(cols), jnp.sin(cols)], axis=1)
    return ret_cs, pad(cos, sin)


def kernel(x, c, ctx, c_ctx, ada_w, ada_b, norm_mix_g, norm_mlp_g, mlp_w1, mlp_w2, ret_w_in, ret_w_out, ret_decay_fwd, ret_decay_bwd, attn_w_in, attn_w_out, attn_sink, final_norm_g, loss_target, m_c_ctx, m_ada_w, m_ada_b, m_norm_mix_g, m_norm_mlp_g, m_mlp_w1, m_mlp_w2, m_ret_w_in, m_ret_w_out, m_ret_decay_fwd, m_ret_decay_bwd, m_attn_w_in, m_attn_w_out, m_attn_sink, m_final_norm_g, v_c_ctx, v_ada_w, v_ada_b, v_norm_mix_g, v_norm_mlp_g, v_mlp_w1, v_mlp_w2, v_ret_w_in, v_ret_w_out, v_ret_decay_fwd, v_ret_decay_bwd, v_attn_w_in, v_attn_w_out, v_attn_sink, v_final_norm_g):
    L, D = x.shape[1], x.shape[2]
    CTX = ctx.shape[1]
    T = L + CTX
    RH = ret_decay_fwd.shape[-1]
    assert D == RH * 2 * LANES and ada_w.shape[0] == 2 and ret_w_in.shape[0] == 1 and attn_w_in.shape[0] == 1
    Hq = attn_sink.shape[-1]
    Hkv = (attn_w_in.shape[-1] * N_DEV // ATT_HEAD_DIM - Hq) // 2
    G = Hq // Hkv
    FF = mlp_w1.shape[-1] * N_DEV
    Wq_r, Wv_r = RH * 2 * LANES, RH * 4 * LANES
    acols = ada_w.shape[-1]
    tr = _pick(CTX, 256, 8)
    tr_wide = _pick(CTX, 128, 8)
    bmT = T // 4 if (T % 64 == 0) else T
    bmL = L // 4 if (L % 64 == 0) else L
    x_idx, y_idx, c_idx = lax.axis_index("x"), lax.axis_index("y"), lax.axis_index("c")
    me = 4 * x_idx + 2 * y_idx + c_idx

    (rcos, rsin), (acos, asin) = _rope_tables(L, CTX)
    lg_f = jax.nn.log_sigmoid(ret_decay_fwd.astype(F32))
    lg_b = jax.nn.log_sigmoid(ret_decay_bwd.astype(F32))

    c_pad = jnp.concatenate([c.astype(F32), jnp.zeros((7, D), F32)], 0)
    c_all = _all_gather_small(c_pad, name="ag_c")[:, 0, :]
    c16 = jnp.concatenate([c_all, c_ctx[None, :], jnp.zeros((7, D), F32)], 0)
    ada_b_cols = lax.dynamic_slice_in_dim(ada_b, me * acols, acols, axis=1)[:, None, :]
    mods_shard = _ada_mods(c16, ada_w, ada_b_cols, name="ada_mods")
    mods_all = _all_gather_small(mods_shard.reshape(32, acols), name="ag_mods")
    mods_all = mods_all.reshape(N_DEV, 2, 16, acols).transpose(1, 2, 0, 3).reshape(2, 16, 6, D)
    mod_lat = lax.dynamic_index_in_dim(mods_all, me, axis=1, keepdims=False)
    mod_ctx = mods_all[:, N_DEV]

    def pack(i, ks, kc):
        return jnp.stack([mod_lat[i, ks], mod_lat[i, kc], mod_ctx[i, ks], mod_ctx[i, kc]], 0)

    def gates(i, k):
        return jnp.stack([mod_lat[i, k], mod_ctx[i, k]], 0)

    def gate_epilogue(gl, gc, x_rows_lat_only):
        def epi(acc, i, j, xt, gv):
            if x_rows_lat_only:
                gate = gv[0:1, :]
            else:
                row = i * acc.shape[0] + lax.broadcasted_iota(jnp.int32, (acc.shape[0], 1), 0)
                gate = jnp.where(row >= L, gv[1:2, :], gv[0:1, :])
            return xt + gate * acc, acc
        return epi

    shards = [ret_w_in, ret_w_out, attn_w_in, attn_w_out, mlp_w1, mlp_w2]
    axes = [2, 1, 2, 1, 2, 1]
    wr_in, wr_out, wa_in, wa_out, w1, w2 = _all_gather_weights([s.astype(BF16) for s in shards], axes, name="ag_weights")
    wr_in, wr_out, wa_in, wa_out = wr_in[0], wr_out[0], wa_in[0], wa_out[0]

    mmT = dict(M=T, bm=bmT)
    mmL = dict(M=L, bm=bmL)

    def bn_of(n, off=0):
        b = 4 * LANES
        while n % b or off % b:
            b -= LANES
        return b

    X0 = jnp.concatenate([x[0], ctx[0]], axis=0)
    g_mix0, g_mlp0 = norm_mix_g[0:1], norm_mlp_g[0:1]
    g_mix1, g_mlp1 = norm_mix_g[1:2], norm_mlp_g[1:2]
    a0 = _normmod(X0, g_mix0, pack(0, 0, 1), R=T, L=L, tr=tr, name="normmod_mix0")

    bn_qk = _pick(2 * Wq_r, 512, 2 * LANES)
    nq_blocks = Wq_r // bn_qk
    kscale = float(2 * LANES) ** -0.5

    def rope_epi(acc, i, j, cos, sin):
        parts = []
        for h in range(acc.shape[1] // (2 * LANES)):
            x1 = acc[:, h * 2 * LANES:h * 2 * LANES + LANES]
            x2 = acc[:, h * 2 * LANES + LANES:(h + 1) * 2 * LANES]
            parts += [x1 * cos - x2 * sin, x2 * cos + x1 * sin]
        return (jnp.concatenate(parts, axis=1) * jnp.where(j < nq_blocks, 1.0, kscale),)

    def row_tile(arr, bm):
        return (arr, (bm, LANES), lambda i, j: (i, 0))

    (qk0,) = _mm(a0, wr_in, "nn", [BF16], N=2 * Wq_r, K=D, bn=bn_qk, bk=D, name="ret_qk", epilogue=rope_epi,
                 extras=[row_tile(rcos, bmT), row_tile(rsin, bmT)], **mmT)
    bn_vg = bn_of(2 * Wv_r, 2 * Wq_r)
    (vg0,) = _mm(a0, wr_in, "nn", [BF16], N=2 * Wv_r, K=D, bn=bn_vg, bk=D, name="ret_vg", b_col0=2 * Wq_r, **mmT)

    of, st_f = _ret_fwd(qk0, vg0, lg_f, T=T, L=L, H=RH, rev=False, name="ret_scan_f")
    ob, st_b = _ret_fwd(qk0, vg0, lg_b, T=T, L=L, H=RH, rev=True, name="ret_scan_b")
    z0 = _readout(of, ob, vg0, T=T, L=L, H=RH, tr=tr_wide, name="ret_readout")

    bnD = _pick(D, 512)

    def xtile(arr, bm):
        return (arr, (bm, bnD), lambda i, j: (i, j))

    def gtile(gv):
        return (gv, (2, bnD), lambda i, j: (0, j))

    bk_v = _pick(Wv_r, 2048)
    X1, ro0 = _mm(z0, wr_out, "nn", [F32, BF16], N=D, K=Wv_r, bn=bnD, bk=bk_v, name="ret_out",
                  epilogue=gate_epilogue(None, None, False), extras=[xtile(X0, bmT), gtile(gates(0, 2))], **mmT)

    def mlp_fwd(Xin, i, g_mlp, rows, name):
        a = _normmod(Xin, g_mlp, pack(i, 3, 4), R=rows["M"], L=L, tr=tr, name=f"normmod_mlp{name}")

        def relu2(acc, i_, j_):
            u = jnp.maximum(acc, 0.0)
            return u, u * u

        bnF = _pick(FF, 512)
        u, r = _mm(a, w1[i], "nn", [BF16, BF16], N=FF, K=D, bn=bnF, bk=D, name=f"mlp_up{name}", epilogue=relu2, **rows)
        bkF = _pick(FF, 2048)
        Xout, mo = _mm(r, w2[i], "nn", [F32, BF16], N=D, K=FF, bn=bnD, bk=bkF, name=f"mlp_down{name}",
                       epilogue=gate_epilogue(None, None, rows["M"] == L),
                       extras=[xtile(Xin, rows["bm"]), gtile(gates(i, 5))], **rows)
        return a, u, r, Xout, mo

    a1, u0, r0, X2, mo0 = mlp_fwd(X1, 0, g_mlp0, mmT, "0")

    a2 = _normmod(X2, g_mix1, pack(1, 0, 1), R=T, L=L, tr=tr, name="normmod_mix1")
    Wq_a, Wk_a = Hq * LANES, Hkv * LANES

    def arope_epi(acc, i, j, cos, sin):
        heads = acc.shape[1] // LANES
        return (acc * jnp.tile(cos, (1, heads)) + _swap32(acc) * jnp.tile(sin, (1, heads)),)

    bn_q = _pick(Wq_a, 512)
    (q1,) = _mm(a2, wa_in, "nn", [BF16], N=Wq_a, K=D, bn=bn_q, bk=D, name="attn_q", epilogue=arope_epi,
                extras=[row_tile(acos, bmL), row_tile(asin, bmL)], **mmL)
    bn_k = bn_of(Wk_a, Wq_a)
    (k1,) = _mm(a2, wa_in, "nn", [BF16], N=Wk_a, K=D, bn=bn_k, bk=D, name="attn_k", b_col0=Wq_a, epilogue=arope_epi,
                extras=[row_tile(acos, bmT), row_tile(asin, bmT)], **mmT)
    bn_v = bn_of(Wk_a, Wq_a + Wk_a)
    (v1,) = _mm(a2, wa_in, "nn", [BF16], N=Wk_a, K=D, bn=bn_v, bk=D, name="attn_v", b_col0=Wq_a + Wk_a, **mmT)
    o1, lse = _attn_fwd(q1, k1, v1, attn_sink, L=L, CTX=CTX, Hkv=Hkv, G=G, name="attn_fwd")
    X3, ao = _mm(o1, wa_out, "nn", [F32, BF16], N=D, K=Wq_a, bn=bnD, bk=_pick(Wq_a, 2048), name="attn_out",
                 epilogue=gate_epilogue(None, None, True), extras=[xtile(X2, bmL), gtile(gates(1, 2))], **mmL)
    a3, u1, r1, X4, mo1 = mlp_fwd(X3, 1, g_mlp1, mmL, "1")

    dX4, dmo1, acc_head = _loss_head(X4, loss_target[0], mo1, final_norm_g[None, :], gates(1, 5)[0:1], L=L, tr=tr,
                                     name="loss_head")
    loss_part = jnp.sum(acc_head[0, 0])
    d_gf = acc_head[0, 1]
    zeros_d = jnp.zeros((D,), F32)
    dmod_lat = [[zeros_d] * 6, [zeros_d] * 6]
    dmod_ctx = [[zeros_d] * 6, [zeros_d] * 6]
    dmod_lat[1][5] = acc_head[0, 2]

    bn_dw = 512

    def mlp_bwd(dmo, a, u, r, i, rows, name):
        Mr = rows["M"]
        bkr = rows["bm"]

        def times_2u(acc, i_, j_, ut):
            return (acc * (2.0 * ut.astype(F32)),)

        bnF = _pick(FF, 512)
        (dh,) = _mm(dmo, w2[i], "nt", [BF16], N=FF, K=D, bn=bnF, bk=D, name=f"mlp_down_dx{name}", epilogue=times_2u,
                    extras=[(u, (rows["bm"], bnF), lambda i_, j_: (i_, j_))], **rows)
        (dw2,) = _mm(r, dmo, "tn", [BF16], M=FF, N=D, K=Mr, bm=_pick(FF, 1024), bn=_pick(D, bn_dw), bk=bkr,
                     name=f"mlp_down_dw{name}")
        (da,) = _mm(dh, w1[i], "nt", [F32], N=D, K=FF, bn=bnD, bk=_pick(FF, 2048), name=f"mlp_up_dx{name}", **rows)
        (dw1,) = _mm(a, dh, "tn", [BF16], M=D, N=FF, K=Mr, bm=_pick(D, 1024), bn=_pick(FF, bn_dw), bk=bkr,
                     name=f"mlp_up_dw{name}")
        return da, dw1, dw2

    da3, dw1_1, dw2_1 = mlp_bwd(dmo1, a3, u1, r1, 1, mmL, "1")
    dX3, dao, acc = _normmod_bwd(X3, da3, dX4, False, g_mlp1, pack(1, 3, 4), (ao, gates(1, 2)), R=L, L=L, tr=tr,
                                 name="normmod_mlp1_bwd")
    dmod_lat[1][3], dmod_lat[1][4], d_gmlp1, dmod_lat[1][2] = acc[0, 0], acc[0, 1], acc[0, 2], acc[0, 3]

    (do1,) = _mm(dao, wa_out, "nt", [BF16], N=Wq_a, K=D, bn=_pick(Wq_a, 512), bk=D, name="attn_out_dx", **mmL)
    (dwa_out,) = _mm(o1, dao, "tn", [BF16], M=Wq_a, N=D, K=L, bm=_pick(Wq_a, 1024), bn=_pick(D, bn_dw), bk=bmL,
                     name="attn_out_dw")
    dq1, delta1, dkx, dvx, dsink_acc = _attn_bwd_q(q1, k1, v1, do1, lse, attn_sink, L=L, CTX=CTX, Hkv=Hkv, G=G,
                                                   name="attn_bwd_q")
    dk1, dv1 = _attn_bwd_kv(q1, k1, v1, do1, lse, delta1, L=L, Hkv=Hkv, G=G, name="attn_bwd_kv")
    dp1 = _attn_assemble(dq1, dk1, dv1, dkx, dvx, acos, asin, T=T, L=L, CTX=CTX, Hq=Hq, Hkv=Hkv, tr=tr_wide,
                         name="attn_assemble")
    Wa_in = Wq_a + 2 * Wk_a
    (da2,) = _mm(dp1, wa_in, "nt", [F32], N=D, K=Wa_in, bn=bnD, bk=_pick(Wa_in, 2048), name="attn_in_dx", **mmT)
    (dwa_in,) = _mm(a2, dp1, "tn", [BF16], M=D, N=Wa_in, K=T, bm=_pick(D, 1024), bn=_pick(Wa_in, bn_dw), bk=bmT,
                    name="attn_in_dw")
    dX2, dmo0, acc = _normmod_bwd(X2, da2, dX3, True, g_mix1, pack(1, 0, 1), (mo0, gates(0, 5)), R=T, L=L, tr=tr,
                                  name="normmod_mix1_bwd")
    dmod_lat[1][0], dmod_lat[1][1], d_gmix1, dmod_lat[0][5] = acc[0, 0], acc[0, 1], acc[0, 2] + acc[1, 2], acc[0, 3]
    dmod_ctx[1][0], dmod_ctx[1][1], dmod_ctx[0][5] = acc[1, 0], acc[1, 1], acc[1, 3]

    da1, dw1_0, dw2_0 = mlp_bwd(dmo0, a1, u0, r0, 0, mmT, "0")
    dX1, dro0, acc = _normmod_bwd(X1, da1, dX2, False, g_mlp0, pack(0, 3, 4), (ro0, gates(0, 2)), R=T, L=L, tr=tr,
                                  name="normmod_mlp0_bwd")
    dmod_lat[0][3], dmod_lat[0][4], d_gmlp0, dmod_lat[0][2] = acc[0, 0], acc[0, 1], acc[0, 2] + acc[1, 2], acc[0, 3]
    dmod_ctx[0][3], dmod_ctx[0][4], dmod_ctx[0][2] = acc[1, 0], acc[1, 1], acc[1, 3]

    (dz0,) = _mm(dro0, wr_out, "nt", [BF16], N=Wv_r, K=D, bn=_pick(Wv_r, 512), bk=D, name="ret_out_dx", **mmT)
    (dwr_out,) = _mm(z0, dro0, "tn", [BF16], M=Wv_r, N=D, K=T, bm=_pick(Wv_r, 1024), bn=_pick(D, bn_dw), bk=bmT,
                     name="ret_out_dw")
    do0, dg0 = _readout_bwd(dz0, of, ob, vg0, T=T, L=L, H=RH, tr=tr_wide, name="ret_readout_bwd")
    dq_f, dk_f, dv_f, dlg_f = _ret_bwd(qk0, vg0, do0, st_f, lg_f, T=T, L=L, H=RH, rev=False, name="ret_scan_f_bwd")
    dq_b, dk_b, dv_b, dlg_b = _ret_bwd(qk0, vg0, do0, st_b, lg_b, T=T, L=L, H=RH, rev=True, name="ret_scan_b_bwd")
    dp0 = _ret_assemble(dq_f, dq_b, dk_f, dk_b, dv_f, dv_b, dg0, rcos, rsin, T=T, L=L, H=RH, tr=tr_wide,
                        name="ret_assemble")
    Wr_in = 2 * Wq_r + 2 * Wv_r
    (da0,) = _mm(dp0, wr_in, "nt", [F32], N=D, K=Wr_in, bn=bnD, bk=_pick(Wr_in, 2048), name="ret_in_dx", **mmT)
    (dwr_in,) = _mm(a0, dp0, "tn", [BF16], M=D, N=Wr_in, K=T, bm=_pick(D, 1024), bn=_pick(Wr_in, bn_dw), bk=bmT,
                    name="ret_in_dw")
    dX0, acc = _normmod_bwd(X0, da0, dX1, False, g_mix0, pack(0, 0, 1), None, R=T, L=L, tr=tr, name="normmod_mix0_bwd")
    dmod_lat[0][0], dmod_lat[0][1], d_gmix0 = acc[0, 0], acc[0, 1], acc[0, 2] + acc[1, 2]
    dmod_ctx[0][0], dmod_ctx[0][1] = acc[1, 0], acc[1, 1]
    grad_x = dX0[:L][None]

    full_grads = [dwr_in[None], dwr_out[None], dwa_in[None], dwa_out[None], jnp.stack([dw1_0, dw1_1]),
                  jnp.stack([dw2_0, dw2_1])]
    recvs = _exchange_grads(full_grads, axes, name="grad_exchange")
    big = {}
    names = ["ret_w_in", "ret_w_out", "attn_w_in", "attn_w_out", "mlp_w1", "mlp_w2"]
    moments = [(m_ret_w_in, v_ret_w_in), (m_ret_w_out, v_ret_w_out), (m_attn_w_in, v_attn_w_in),
               (m_attn_w_out, v_attn_w_out), (m_mlp_w1, v_mlp_w1), (m_mlp_w2, v_mlp_w2)]
    for nm, w_, (m_, v_), rc in zip(names, shards, moments, recvs):
        big[nm] = _adamw_sharded(w_, m_, v_, rc, name=f"adamw_{nm}")

    misc = jnp.zeros((D,), F32)
    misc = misc.at[0:RH].set(dlg_f[:, 0, 0]).at[RH:2 * RH].set(dlg_b[:, 0, 0])
    misc = misc.at[2 * RH:2 * RH + Hq].set(dsink_acc[:, :G, 0].reshape(Hq)).at[2 * RH + Hq].set(loss_part)
    rows = ([dmod_lat[i][k] for i in range(2) for k in range(6)] + [dmod_ctx[i][k] for i in range(2) for k in range(6)]
            + [d_gmix0, d_gmix1, d_gmlp0, d_gmlp1, d_gf, misc, zeros_d, zeros_d])
    part = jnp.stack(rows, 0)
    part_all = _all_gather_small(part, name="ag_small_grads")
    tot = _sum_devices(part_all, name="sum_small_grads")

    grad_ada_b = (tot[0:12] + tot[12:24]).reshape(2, 6 * D)
    grad_norm_mix_g, grad_norm_mlp_g, grad_final_norm_g = tot[24:26], tot[26:28], tot[28]
    grad_ret_decay_fwd = (tot[29, 0:RH] * jax.nn.sigmoid(-ret_decay_fwd[0]))[None]
    grad_ret_decay_bwd = (tot[29, RH:2 * RH] * jax.nn.sigmoid(-ret_decay_bwd[0]))[None]
    grad_attn_sink = tot[29, 2 * RH:2 * RH + Hq][None]
    loss = tot[29, 2 * RH + Hq]

    dlat_cols = lax.dynamic_slice_in_dim(part_all[:, 0:12].reshape(N_DEV, 2, 6 * D), me * acols, acols, axis=2)
    dctx_cols = lax.dynamic_slice_in_dim(tot[12:24].reshape(2, 6 * D), me * acols, acols, axis=1)
    dmod16 = jnp.concatenate([dlat_cols.transpose(1, 0, 2), dctx_cols[:, None, :], jnp.zeros((2, 7, acols), F32)], 1)
    cond_t = _silu(c16).T
    g_ada, d_ada, nm_ada, nv_ada, dcond_part = _ada_bwd(cond_t, dmod16, ada_w, m_ada_w, v_ada_w, name="ada_bwd")
    dcond = (dcond_part[0, :, 0] + dcond_part[1, :, 0]).reshape(D // LANES, LANES)
    pad_rows = -(D // LANES) % 8
    dcond_pad = jnp.concatenate([dcond, jnp.zeros((pad_rows, LANES), F32)], 0) if pad_rows else dcond
    dcond_all = _all_gather_small(dcond_pad, name="ag_dcond")
    dcond_tot = _sum_devices(dcond_all, name="sum_dcond")[:D // LANES].reshape(D)
    sg = jax.nn.sigmoid(c_ctx)
    grad_c_ctx = dcond_tot * (sg * (1.0 + c_ctx * (1.0 - sg)))

    small_w = [c_ctx, ada_b, norm_mix_g, norm_mlp_g, ret_decay_fwd, ret_decay_bwd, attn_sink, final_norm_g]
    small_g = [grad_c_ctx, grad_ada_b, grad_norm_mix_g, grad_norm_mlp_g, grad_ret_decay_fwd, grad_ret_decay_bwd,
               grad_attn_sink, grad_final_norm_g]
    small_m = [m_c_ctx, m_ada_b, m_norm_mix_g, m_norm_mlp_g, m_ret_decay_fwd, m_ret_decay_bwd, m_attn_sink,
               m_final_norm_g]
    small_v = [v_c_ctx, v_ada_b, v_norm_mix_g, v_norm_mlp_g, v_ret_decay_fwd, v_ret_decay_bwd, v_attn_sink,
               v_final_norm_g]
    sizes = [w_.size for w_ in small_w]
    total = sum(-(-s // LANES) * LANES for s in sizes)
    total_pad = -(-total // (8 * LANES)) * 8 * LANES

    def flat_pack(ts, fill):
        pieces = []
        for t_ in ts:
            f = t_.reshape(-1).astype(F32)
            pad = -f.size % LANES
            pieces.append(jnp.concatenate([f, jnp.full((pad,), fill, F32)]) if pad else f)
        pieces.append(jnp.full((total_pad - total,), fill, F32))
        return jnp.concatenate(pieces).reshape(total_pad // LANES, LANES)

    d_s, nm_s, nv_s = _adamw_flat(flat_pack(small_w, 0.0), flat_pack(small_g, 0.0), flat_pack(small_m, 0.0),
                                  flat_pack(small_v, 1.0), name="adamw_small")

    def unpack(p):
        flat = p.reshape(-1)
        res, off = [], 0
        for w_, s in zip(small_w, sizes):
            res.append(flat[off:off + s].reshape(w_.shape))
            off += -(-s // LANES) * LANES
        return res

    d_small, nm_small, nv_small = unpack(d_s), unpack(nm_s), unpack(nv_s)
    small_names = ["c_ctx", "ada_b", "norm_mix_g", "norm_mlp_g", "ret_decay_fwd", "ret_decay_bwd", "attn_sink",
                   "final_norm_g"]
    sm = {n: (g_, d_, m_, v_) for n, g_, d_, m_, v_ in zip(small_names, small_g, d_small, nm_small, nv_small)}

    def out4(n):
        if n == "ada_w":
            return g_ada, d_ada, nm_ada, nv_ada
        if n in big:
            return tuple(big[n])
        return sm[n]

    order = ["c_ctx", "ada_w", "ada_b", "norm_mix_g", "norm_mlp_g", "mlp_w1", "mlp_w2", "ret_w_in", "ret_w_out",
             "ret_decay_fwd", "ret_decay_bwd", "attn_w_in", "attn_w_out", "attn_sink", "final_norm_g"]
    quads = [out4(n) for n in order]
    return (loss, grad_x, *[q_[0] for q_ in quads], *[q_[1] for q_ in quads], *[q_[2] for q_ in quads],
            *[q_[3] for q_ in quads])
```

```python
import functools

import jax
import jax.numpy as jnp
from jax import lax
from jax.experimental import pallas as pl
from jax.experimental.pallas import tpu as pltpu

F32 = jnp.float32
BF16 = jnp.bfloat16

N_DEV = 8
NORM_EPS = 1e-6
CHUNK = 128
ATT_HEAD_DIM = 128
GRID_W = 64
ROPE_BASE = 10000.0
NEG_INF = -1e30
ADAM_LR, ADAM_B1, ADAM_B2, ADAM_EPS, ADAM_WD, ADAM_STEP = 0.001, 0.9, 0.999, 1e-08, 0.01, 10

V7X_VMEM_LIMIT_BYTES = 56 * 1024 * 1024
LANES = 128
MESH = pl.DeviceIdType.MESH

_NN = (((1,), (0,)), ((), ()))
_NT = (((1,), (1,)), ((), ()))
_TN = (((0,), (0,)), ((), ()))


def _dot(a, b, dn=_NN):
    return lax.dot_general(a, b, dn, preferred_element_type=F32)


def _cparams(*sem):
    return pltpu.CompilerParams(dimension_semantics=sem, vmem_limit_bytes=V7X_VMEM_LIMIT_BYTES)


def _pick(n, pref, mult=LANES):
    if n <= pref:
        return n
    best = None
    for d in range(mult, pref + 1, mult):
        if n % d == 0:
            best = d
    assert best is not None, (n, pref)
    return best


def _silu(x):
    return x * jax.nn.sigmoid(x)


def _mm(a, b, mode, out_dtypes, *, M, N, K, bm, bn, bk, name, b_col0=0, epilogue=None, extras=(), dep=None):
    assert M % bm == 0 and N % bn == 0 and K % bk == 0 and b_col0 % bn == 0, (name, M, N, K, bm, bn, bk, b_col0)
    nk = K // bk
    c0 = b_col0 // bn
    if mode == "nn":
        a_spec = pl.BlockSpec((bm, bk), lambda i, j, k: (i, k))
        b_spec = pl.BlockSpec((bk, bn), lambda i, j, k: (k, j + c0))
    elif mode == "nt":
        a_spec = pl.BlockSpec((bm, bk), lambda i, j, k: (i, k))
        b_spec = pl.BlockSpec((bn, bk), lambda i, j, k: (j + c0, k))
    else:
        a_spec = pl.BlockSpec((bk, bm), lambda i, j, k: (k, i))
        b_spec = pl.BlockSpec((bk, bn), lambda i, j, k: (k, j + c0))
    dn = {"nn": _NN, "nt": _NT, "tn": _TN}[mode]
    e_specs = [pl.BlockSpec(bs, (lambda i, j, k, f=f: f(i, j))) for (_, bs, f) in extras]
    ne, no = len(extras), len(out_dtypes)
    nd = 0 if dep is None else 1

    def body(a_ref, b_ref, *rest):
        e_refs, o_refs = rest[:ne], rest[ne + nd:ne + nd + no]
        i, j, k = pl.program_id(0), pl.program_id(1), pl.program_id(2)

        def finish(acc):
            outs = (acc,) if epilogue is None else epilogue(acc, i, j, *[e[...] for e in e_refs])
            for o_ref, o in zip(o_refs, outs):
                o_ref[...] = o.astype(o_ref.dtype)

        p = _dot(a_ref[...], b_ref[...], dn)
        if nk == 1:
            finish(p)
        else:
            acc_ref = rest[-1]

            @pl.when(k == 0)
            def _():
                acc_ref[...] = p

            @pl.when(k > 0)
            def _():
                acc_ref[...] += p

            @pl.when(k == nk - 1)
            def _():
                finish(acc_ref[...])

    outs = pl.pallas_call(
        body, name=name, grid=(M // bm, N // bn, nk),
        in_specs=[a_spec, b_spec] + e_specs + [pl.BlockSpec(memory_space=pl.ANY)] * nd,
        out_specs=[pl.BlockSpec((bm, bn), lambda i, j, k: (i, j)) for _ in out_dtypes],
        out_shape=[jax.ShapeDtypeStruct((M, N), dt) for dt in out_dtypes],
        scratch_shapes=[pltpu.VMEM((bm, bn), F32)] if nk > 1 else [],
        compiler_params=_cparams("parallel", "parallel", "arbitrary"),
    )(a, b, *[e[0] for e in extras], *([dep] if nd else []))
    return outs


def _rowwise(body, rows, vecs, outs, n_acc, *, R, L, tr, name, acc_width=None):
    assert R % tr == 0 and L % tr == 0, (name, R, L, tr)
    nl = L // tr
    n_regions = 2 if R > L else 1
    n_rows, n_vecs, n_outs = len(rows), len(vecs), len(outs)
    acc_pad = -(-n_acc // 8) * 8 if n_acc else 0

    in_specs = []
    for (_, w, cb, lat_only) in rows:
        if lat_only:
            in_specs.append(pl.BlockSpec((tr, w), lambda i, cb=cb: (jnp.minimum(i, nl - 1), cb)))
        else:
            in_specs.append(pl.BlockSpec((tr, w), lambda i, cb=cb: (i, cb)))
    for v in vecs:
        in_specs.append(pl.BlockSpec(v.shape, lambda i, nd=v.ndim: (0,) * nd))
    out_specs = [pl.BlockSpec((tr, w), lambda i: (i, 0)) for (w, _) in outs]
    out_shape = [jax.ShapeDtypeStruct((R, w), dt) for (w, dt) in outs]
    if n_acc:
        out_specs.append(pl.BlockSpec((None, acc_pad, acc_width), lambda i: (jnp.where(i >= nl, 1, 0), 0, 0)))
        out_shape.append(jax.ShapeDtypeStruct((n_regions, acc_pad, acc_width), F32))

    def kern(*refs):
        i = pl.program_id(0)
        is_ctx = i >= nl
        ins = [r[...] for r in refs[:n_rows + n_vecs]]
        o_refs = refs[n_rows + n_vecs:]
        out_tiles, acc_rows = body(is_ctx, *ins)
        for o_ref, o in zip(o_refs[:n_outs], out_tiles):
            o_ref[...] = o.astype(o_ref.dtype)
        if n_acc:
            acc_ref = o_refs[n_outs]

            @pl.when((i == 0) | (i == nl))
            def _():
                acc_ref[...] = jnp.zeros_like(acc_ref)

            for r, row in enumerate(acc_rows):
                acc_ref[r:r + 1, :] += row

    res = pl.pallas_call(
        kern, name=name, grid=(R // tr,), in_specs=in_specs, out_specs=out_specs, out_shape=out_shape,
        compiler_params=_cparams("arbitrary"),
    )(*[r[0] for r in rows], *vecs)
    return res


def _colsum(x):
    return jnp.sum(x, axis=0, keepdims=True)


def _rms_stats(x):
    r = lax.rsqrt(jnp.mean(x * x, axis=-1, keepdims=True) + NORM_EPS)
    return x * r, r


def _sel(is_ctx, pk, lat_row, ctx_row):
    return jnp.where(is_ctx, pk[ctx_row:ctx_row + 1, :], pk[lat_row:lat_row + 1, :])


def _normmod(x, g, pk, *, R, L, tr, name):
    D = x.shape[-1]

    def body(is_ctx, xt, gv, pkv):
        xh, _ = _rms_stats(xt)
        sh, sc = _sel(is_ctx, pkv, 0, 2), _sel(is_ctx, pkv, 1, 3)
        return ((xh * gv) * (1.0 + sc) + sh,), ()

    return _rowwise(body, [(x, D, 0, False)], [g, pk], [(D, BF16)], 0, R=R, L=L, tr=tr, name=name)[0]


def _normmod_bwd(x_in, da, dx_out, dx_out_lat_only, g, pk, prev, *, R, L, tr, name):
    D = x_in.shape[-1]
    has_prev = prev is not None

    def body(is_ctx, *t):
        if has_prev:
            xt, dat, dxo, mp, gv, pkv, gates = t
        else:
            xt, dat, dxo, gv, pkv = t
        xh, r = _rms_stats(xt)
        sc = _sel(is_ctx, pkv, 1, 3)
        if dx_out_lat_only:
            dxo = jnp.where(is_ctx, 0.0, dxo)
        dn = dat * (1.0 + sc)
        w = dn * gv
        dxi = dxo + r * (w - xh * jnp.mean(w * xh, axis=-1, keepdims=True))
        accs = [_colsum(dat), _colsum(dat * (xh * gv)), _colsum(dn * xh)]
        outs = [dxi]
        if has_prev:
            gate = _sel(is_ctx, gates, 0, 1)
            outs.append(dxi * gate)
            accs.append(_colsum(dxi * mp.astype(F32)))
        return outs, accs

    rows = [(x_in, D, 0, False), (da, D, 0, False), (dx_out, D, 0, dx_out_lat_only)]
    vecs = [g, pk]
    outs = [(D, F32)]
    if has_prev:
        rows.append((prev[0], D, 0, False))
        vecs.append(prev[1])
        outs.append((D, BF16))
    return _rowwise(body, rows, vecs, outs, 4 if has_prev else 3, R=R, L=L, tr=tr, name=name, acc_width=D)


def _loss_head(x4, target, m_prev, gf, gate, *, L, tr, name):
    D = x4.shape[-1]

    def body(is_ctx, xt, tg, mp, gfv, gatev):
        xh, r = _rms_stats(xt)
        e = xh * gfv - tg
        dy = e * (1.0 / D)
        w = dy * gfv
        dx = r * (w - xh * jnp.mean(w * xh, axis=-1, keepdims=True))
        accs = [_colsum(e * e) * (0.5 / D), _colsum(dy * xh), _colsum(dx * mp.astype(F32))]
        return (dx, dx * gatev), accs

    return _rowwise(body, [(x4, D, 0, False), (target, D, 0, False), (m_prev, D, 0, False)], [gf, gate],
                    [(D, F32), (D, BF16)], 3, R=L, L=L, tr=tr, name=name, acc_width=D)


def _decays(lgh, rev):
    ii = lax.broadcasted_iota(jnp.int32, (CHUNK, CHUNK), 0)
    jj = lax.broadcasted_iota(jnp.int32, (CHUNK, CHUNK), 1)
    ri = lax.broadcasted_iota(jnp.int32, (CHUNK, 1), 0).astype(F32)
    diff = (jj - ii if rev else ii - jj)
    amat = jnp.where(diff >= 0, jnp.exp(lgh * jnp.maximum(diff, 0).astype(F32)), 0.0)
    pos = (CHUNK - ri) if rev else (ri + 1.0)
    bq = jnp.exp(lgh * pos)
    bk = jnp.exp(lgh * (CHUNK - pos))
    return amat, bq, bk, pos


def _ret_specs(H, nT, nL, rev, backward):
    def chunk(s):
        s = (nT - 1 - s) if backward else s
        return (nT - 1 - s) if rev else (s + nL) % nT

    def step(s):
        return (nT - 1 - s) if backward else s

    return chunk, step


def _ret_fwd(qk, vg, lg, *, T, L, H, rev, name):
    dk, dv = 2 * LANES, 4 * LANES
    nT, nL = T // CHUNK, L // CHUNK
    chunk, step = _ret_specs(H, nT, nL, rev, False)

    def body(lg_ref, q_ref, k_ref, v_ref, o_ref, st_ref, s_scr):
        h, s = pl.program_id(0), pl.program_id(1)
        lgh = lg_ref[0, h]
        amat, bq, bk, _ = _decays(lgh, rev)

        @pl.when(s == 0)
        def _():
            s_scr[...] = jnp.zeros_like(s_scr)

        q, k, v = q_ref[...], k_ref[...], v_ref[...]
        st = s_scr[...]
        st_ref[...] = st
        scores = _dot(q, k, _NT) * amat
        o_ref[...] = _dot(scores.astype(BF16), v) + _dot(q, st.astype(BF16)) * bq
        kd = (k.astype(F32) * bk).astype(BF16)
        s_scr[...] = st * jnp.exp(lgh * CHUNK) + _dot(kd, v, _TN)

    return pl.pallas_call(
        body, name=name, grid=(H, nT),
        in_specs=[pl.BlockSpec(memory_space=pltpu.SMEM),
                  pl.BlockSpec((CHUNK, dk), lambda h, s: (chunk(s), h)),
                  pl.BlockSpec((CHUNK, dk), lambda h, s: (chunk(s), H + h)),
                  pl.BlockSpec((CHUNK, dv), lambda h, s: (chunk(s), h))],
        out_specs=[pl.BlockSpec((CHUNK, dv), lambda h, s: (chunk(s), h)),
                   pl.BlockSpec((None, None, dk, dv), lambda h, s: (h, s, 0, 0))],
        out_shape=[jax.ShapeDtypeStruct((T, H * dv), F32), jax.ShapeDtypeStruct((H, nT, dk, dv), F32)],
        scratch_shapes=[pltpu.VMEM((dk, dv), F32)],
        compiler_params=_cparams("parallel", "arbitrary"),
    )(lg, qk, qk, vg)


def _ret_bwd(qk, vg, do, states, lg, *, T, L, H, rev, name):
    dk, dv = 2 * LANES, 4 * LANES
    nT, nL = T // CHUNK, L // CHUNK
    chunk, step = _ret_specs(H, nT, nL, rev, True)

    def body(lg_ref, q_ref, k_ref, v_ref, do_ref, st_ref, dq_ref, dk_ref, dv_ref, dlg_ref, ds_scr):
        h, s = pl.program_id(0), pl.program_id(1)
        lgh = lg_ref[0, h]
        amat, bq, bk, pos = _decays(lgh, rev)

        @pl.when(s == 0)
        def _():
            ds_scr[...] = jnp.zeros_like(ds_scr)
            dlg_ref[...] = jnp.zeros_like(dlg_ref)

        q, k, v, dob = q_ref[...], k_ref[...], v_ref[...], do_ref[...]
        st = st_ref[...]
        stb = st.astype(BF16)
        ds_new = ds_scr[...]
        dsb = ds_new.astype(BF16)
        qf, kf = q.astype(F32), k.astype(F32)
        scores = (_dot(q, k, _NT) * amat).astype(BF16)
        dqk = (_dot(dob, v, _NT) * amat).astype(BF16)
        dq = _dot(dqk, k) + _dot(dob, stb, _NT) * bq
        dkk = _dot(dqk, q, _TN) + _dot(v, dsb, _NT) * bk
        kd = (kf * bk).astype(BF16)
        dvv = _dot(scores, dob, _TN) + _dot(kd, dsb)
        dod = (dob.astype(F32) * bq).astype(BF16)
        ds_prev = ds_new * jnp.exp(lgh * CHUNK) + _dot(q, dod, _TN)
        ds_scr[...] = ds_prev
        dq_ref[...] = dq
        dk_ref[...] = dkk
        dv_ref[...] = dvv
        dlg = (jnp.sum(pos * jnp.sum(qf * dq - kf * dkk, axis=-1, keepdims=True))
               + CHUNK * jnp.sum(ds_prev * st))
        dlg_ref[...] += dlg

    return pl.pallas_call(
        body, name=name, grid=(H, nT),
        in_specs=[pl.BlockSpec(memory_space=pltpu.SMEM),
                  pl.BlockSpec((CHUNK, dk), lambda h, s: (chunk(s), h)),
                  pl.BlockSpec((CHUNK, dk), lambda h, s: (chunk(s), H + h)),
                  pl.BlockSpec((CHUNK, dv), lambda h, s: (chunk(s), h)),
                  pl.BlockSpec((CHUNK, dv), lambda h, s: (chunk(s), h)),
                  pl.BlockSpec((None, None, dk, dv), lambda h, s: (h, step(s), 0, 0))],
        out_specs=[pl.BlockSpec((CHUNK, dk), lambda h, s: (chunk(s), h)),
                   pl.BlockSpec((CHUNK, dk), lambda h, s: (chunk(s), h)),
                   pl.BlockSpec((CHUNK, dv), lambda h, s: (chunk(s), h)),
                   pl.BlockSpec((None, 8, LANES), lambda h, s: (h, 0, 0))],
        out_shape=[jax.ShapeDtypeStruct((T, H * dk), F32), jax.ShapeDtypeStruct((T, H * dk), F32),
                   jax.ShapeDtypeStruct((T, H * dv), F32), jax.ShapeDtypeStruct((H, 8, LANES), F32)],
        scratch_shapes=[pltpu.VMEM((dk, dv), F32)],
        compiler_params=_cparams("parallel", "arbitrary"),
    )(lg, qk, qk, vg, do, states)


def _readout(o_f, o_b, vg, *, T, L, H, tr, name):
    dv = 4 * LANES
    W = H * dv

    def body(is_ctx, of, ob, g):
        o = of + ob
        parts = []
        for h in range(H):
            oh = o[:, h * dv:(h + 1) * dv]
            parts.append(oh * lax.rsqrt(jnp.mean(oh * oh, axis=-1, keepdims=True) + NORM_EPS))
        y = jnp.concatenate(parts, axis=1)
        return (_silu(g.astype(F32)) * y,), ()

    return _rowwise(body, [(o_f, W, 0, False), (o_b, W, 0, False), (vg, W, 1, False)], [], [(W, BF16)], 0,
                    R=T, L=L, tr=tr, name=name)[0]


def _readout_bwd(dz, o_f, o_b, vg, *, T, L, H, tr, name):
    dv = 4 * LANES
    W = H * dv

    def body(is_ctx, dzt, of, ob, g):
        o = of + ob
        gf = g.astype(F32)
        sg = jax.nn.sigmoid(gf)
        dzf = dzt.astype(F32)
        dy = dzf * (gf * sg)
        ys, dos = [], []
        for h in range(H):
            sl = slice(h * dv, (h + 1) * dv)
            oh, dyh = o[:, sl], dy[:, sl]
            r = lax.rsqrt(jnp.mean(oh * oh, axis=-1, keepdims=True) + NORM_EPS)
            yh = oh * r
            ys.append(yh)
            dos.append(r * (dyh - yh * jnp.mean(dyh * yh, axis=-1, keepdims=True)))
        y = jnp.concatenate(ys, axis=1)
        dg = dzf * y * (sg * (1.0 + gf * (1.0 - sg)))
        return (jnp.concatenate(dos, axis=1), dg), ()

    return _rowwise(body, [(dz, W, 0, False), (o_f, W, 0, False), (o_b, W, 0, False), (vg, W, 1, False)], [],
                    [(W, BF16), (W, BF16)], 0, R=T, L=L, tr=tr, name=name)


def _ret_assemble(dq_f, dq_b, dk_f, dk_b, dv_f, dv_b, dg, cos, sin, *, T, L, H, tr, name):
    dk, dv = 2 * LANES, 4 * LANES
    Wq, Wv = H * dk, H * dv
    kscale = float(dk) ** -0.5

    def unrope(d, c, s_, scale):
        parts = []
        for h in range(H):
            d1, d2 = d[:, h * dk:h * dk + LANES], d[:, h * dk + LANES:(h + 1) * dk]
            parts += [(d1 * c + d2 * s_) * scale, (d2 * c - d1 * s_) * scale]
        return jnp.concatenate(parts, axis=1)

    def body(is_ctx, qf, qb, kf, kb, vf, vb, g, c, s_):
        dq = unrope(qf + qb, c, s_, 1.0)
        dkk = unrope(kf + kb, c, s_, kscale)
        return (jnp.concatenate([dq.astype(BF16), dkk.astype(BF16), (vf + vb).astype(BF16), g], axis=1),), ()

    rows = [(dq_f, Wq, 0, False), (dq_b, Wq, 0, False), (dk_f, Wq, 0, False), (dk_b, Wq, 0, False),
            (dv_f, Wv, 0, False), (dv_b, Wv, 0, False), (dg, Wv, 0, False),
            (cos, LANES, 0, False), (sin, LANES, 0, False)]
    return _rowwise(body, rows, [], [(2 * Wq + 2 * Wv, BF16)], 0, R=T, L=L, tr=tr, name=name)[0]


def _swap32(x):
    n = x.shape[-1]
    lane = lax.broadcasted_iota(jnp.int32, x.shape, x.ndim - 1)
    return jnp.where(lane % 64 < 32, pltpu.roll(x, n - 32, x.ndim - 1), pltpu.roll(x, 32, x.ndim - 1))


def _band_masks(n, nb):
    ii = lax.broadcasted_iota(jnp.int32, (CHUNK, CHUNK), 0)
    jj = lax.broadcasted_iota(jnp.int32, (CHUNK, CHUNK), 1)
    return (jj >= ii) & (n > 0), (jj <= ii) & (n < nb - 1)


def _attn_kv_specs(L, CTX, nb):
    blk = lambda f: pl.BlockSpec((CHUNK, LANES), lambda h, n: (f(n), h))
    prev_, cur_, next_ = (lambda n: jnp.maximum(n - 1, 0)), (lambda n: n), (lambda n: jnp.minimum(n + 1, nb - 1))
    ctx_spec = pl.BlockSpec((CTX, LANES), lambda h, n: (L // CTX, h))
    return [blk(prev_), blk(cur_), blk(next_), ctx_spec]


def _attn_fwd(q, k, v, sink, *, L, CTX, Hkv, G, name):
    nb = L // CHUNK
    scale = float(ATT_HEAD_DIM) ** -0.5
    kvs = _attn_kv_specs(L, CTX, nb)

    def body(sink_ref, q_ref, kp, kc, kn, kx, vp, vc, vn, vx, o_ref, lse_ref):
        h, n = pl.program_id(0), pl.program_id(1)
        mp, mn = _band_masks(n, nb)
        for g in range(G):
            qg = q_ref[:, g * LANES:(g + 1) * LANES]
            sp = jnp.where(mp, _dot(qg, kp[...], _NT) * scale, NEG_INF)
            sc = _dot(qg, kc[...], _NT) * scale
            sn = jnp.where(mn, _dot(qg, kn[...], _NT) * scale, NEG_INF)
            sx = _dot(qg, kx[...], _NT) * scale
            sk = sink_ref[0, h * G + g]
            rmax = lambda t: jnp.max(t, axis=-1, keepdims=True)
            m = jnp.maximum(jnp.maximum(jnp.maximum(rmax(sp), rmax(sc)), jnp.maximum(rmax(sn), rmax(sx))), sk)
            pp, pc, pn, px = jnp.exp(sp - m), jnp.exp(sc - m), jnp.exp(sn - m), jnp.exp(sx - m)
            rsum = lambda t: jnp.sum(t, axis=-1, keepdims=True)
            den = rsum(pp) + rsum(pc) + rsum(pn) + rsum(px) + jnp.exp(sk - m)
            o = (_dot(pp.astype(BF16), vp[...]) + _dot(pc.astype(BF16), vc[...])
                 + _dot(pn.astype(BF16), vn[...]) + _dot(px.astype(BF16), vx[...]))
            o_ref[:, g * LANES:(g + 1) * LANES] = (o / den).astype(o_ref.dtype)
            lse_ref[:, g:g + 1] = m + jnp.log(den)

    return pl.pallas_call(
        body, name=name, grid=(Hkv, nb),
        in_specs=[pl.BlockSpec(memory_space=pltpu.SMEM),
                  pl.BlockSpec((CHUNK, G * LANES), lambda h, n: (n, h))] + kvs + kvs,
        out_specs=[pl.BlockSpec((CHUNK, G * LANES), lambda h, n: (n, h)),
                   pl.BlockSpec((None, CHUNK, G), lambda h, n: (h, n, 0))],
        out_shape=[jax.ShapeDtypeStruct((L, Hkv * G * LANES), BF16), jax.ShapeDtypeStruct((Hkv, L, G), F32)],
        compiler_params=_cparams("parallel", "parallel"),
    )(sink, q, k, k, k, k, v, v, v, v)


def _attn_bwd_q(q, k, v, do, lse, sink, *, L, CTX, Hkv, G, name):
    nb = L // CHUNK
    scale = float(ATT_HEAD_DIM) ** -0.5
    kvs = _attn_kv_specs(L, CTX, nb)
    qspec = pl.BlockSpec((CHUNK, G * LANES), lambda h, n: (n, h))
    rowspec = pl.BlockSpec((None, CHUNK, G), lambda h, n: (h, n, 0))

    def body(sink_ref, q_ref, do_ref, lse_ref, kp, kc, kn, kx, vp, vc, vn, vx,
             dq_ref, dl_ref, dkx_ref, dvx_ref, dsk_ref):
        h, n = pl.program_id(0), pl.program_id(1)
        mp, mn = _band_masks(n, nb)

        @pl.when(n == 0)
        def _():
            dkx_ref[...] = jnp.zeros_like(dkx_ref)
            dvx_ref[...] = jnp.zeros_like(dvx_ref)
            dsk_ref[...] = jnp.zeros_like(dsk_ref)

        for g in range(G):
            sl = slice(g * LANES, (g + 1) * LANES)
            qg, dog = q_ref[:, sl], do_ref[:, sl]
            lse_g = lse_ref[:, g:g + 1]
            sk = sink_ref[0, h * G + g]
            ks, vs, masks = (kp, kc, kn, kx), (vp, vc, vn, vx), (mp, None, mn, None)
            ps, dps = [], []
            for kr, vr, msk in zip(ks, vs, masks):
                s_ = _dot(qg, kr[...], _NT) * scale
                p = jnp.exp(s_ - lse_g)
                if msk is not None:
                    p = jnp.where(msk, p, 0.0)
                ps.append(p)
                dps.append(_dot(dog, vr[...], _NT))
            delta = sum(jnp.sum(p * dp, axis=-1, keepdims=True) for p, dp in zip(ps, dps))
            dq = jnp.zeros((CHUNK, LANES), F32)
            for idx, (p, dp, kr) in enumerate(zip(ps, dps, ks)):
                ds = (p * (dp - delta) * scale).astype(BF16)
                dq = dq + _dot(ds, kr[...])
                if idx == 3:
                    dkx_ref[...] += _dot(ds, qg, _TN)
                    dvx_ref[...] += _dot(p.astype(BF16), dog, _TN)
            dq_ref[:, sl] = dq
            dl_ref[:, g:g + 1] = delta
            dsk_ref[g:g + 1, :] += -jnp.sum(jnp.exp(sk - lse_g) * delta)

    return pl.pallas_call(
        body, name=name, grid=(Hkv, nb),
        in_specs=[pl.BlockSpec(memory_space=pltpu.SMEM), qspec, qspec, rowspec] + kvs + kvs,
        out_specs=[qspec, rowspec,
                   pl.BlockSpec((CTX, LANES), lambda h, n: (0, h)), pl.BlockSpec((CTX, LANES), lambda h, n: (0, h)),
                   pl.BlockSpec((None, 8, LANES), lambda h, n: (h, 0, 0))],
        out_shape=[jax.ShapeDtypeStruct((L, Hkv * G * LANES), F32), jax.ShapeDtypeStruct((Hkv, L, G), F32),
                   jax.ShapeDtypeStruct((CTX, Hkv * LANES), F32), jax.ShapeDtypeStruct((CTX, Hkv * LANES), F32),
                   jax.ShapeDtypeStruct((Hkv, 8, LANES), F32)],
        compiler_params=_cparams("parallel", "arbitrary"),
    )(sink, q, do, lse, k, k, k, k, v, v, v, v)


def _attn_bwd_kv(q, k, v, do, lse, delta, *, L, Hkv, G, name):
    nb = L // CHUNK
    scale = float(ATT_HEAD_DIM) ** -0.5
    fs = [(lambda n: jnp.maximum(n - 1, 0)), (lambda n: n), (lambda n: jnp.minimum(n + 1, nb - 1))]
    qspecs = [pl.BlockSpec((CHUNK, G * LANES), lambda h, n, f=f: (f(n), h)) for f in fs]
    rspecs = [pl.BlockSpec((None, CHUNK, G), lambda h, n, f=f: (h, f(n), 0)) for f in fs]
    kspec = pl.BlockSpec((CHUNK, LANES), lambda h, n: (n, h))

    def body(k_ref, v_ref, q0, q1, q2, d0, d1, d2, l0, l1, l2, e0, e1, e2, dk_ref, dv_ref):
        n = pl.program_id(1)
        ii = lax.broadcasted_iota(jnp.int32, (CHUNK, CHUNK), 0)
        jj = lax.broadcasted_iota(jnp.int32, (CHUNK, CHUNK), 1)
        masks = ((jj <= ii) & (n > 0), None, (jj >= ii) & (n < nb - 1))
        kb, vb = k_ref[...], v_ref[...]
        dk = jnp.zeros((CHUNK, LANES), F32)
        dv = jnp.zeros((CHUNK, LANES), F32)
        for qr, dr, lr, er, msk in zip((q0, q1, q2), (d0, d1, d2), (l0, l1, l2), (e0, e1, e2), masks):
            for g in range(G):
                sl = slice(g * LANES, (g + 1) * LANES)
                qg, dog = qr[:, sl], dr[:, sl]
                p = jnp.exp(_dot(qg, kb, _NT) * scale - lr[:, g:g + 1])
                if msk is not None:
                    p = jnp.where(msk, p, 0.0)
                ds = (p * (_dot(dog, vb, _NT) - er[:, g:g + 1]) * scale).astype(BF16)
                dk = dk + _dot(ds, qg, _TN)
                dv = dv + _dot(p.astype(BF16), dog, _TN)
        dk_ref[...] = dk
        dv_ref[...] = dv

    return pl.pallas_call(
        body, name=name, grid=(Hkv, nb),
        in_specs=[kspec, kspec] + qspecs + qspecs + rspecs + rspecs,
        out_specs=[kspec, kspec],
        out_shape=[jax.ShapeDtypeStruct((L, Hkv * LANES), F32), jax.ShapeDtypeStruct((L, Hkv * LANES), F32)],
        compiler_params=_cparams("parallel", "parallel"),
    )(k, v, q, q, q, do, do, do, lse, lse, lse, delta, delta, delta)


def _attn_assemble(dq, dk_lat, dv_lat, dk_ctx, dv_ctx, cos, sin, *, T, L, CTX, Hq, Hkv, tr, name):
    Wq, Wk = Hq * LANES, Hkv * LANES
    ctx_blocks = CTX // tr
    nl = L // tr

    def unrope(d, c, s_, heads):
        return d * jnp.tile(c, (1, heads)) + _swap32(d * jnp.tile(s_, (1, heads)))

    def body(is_ctx, dqt, dkl, dvl, dkc, dvc, c, s_):
        dq_ = jnp.where(is_ctx, 0.0, unrope(dqt, c, s_, Hq))
        dk_ = unrope(jnp.where(is_ctx, dkc, dkl), c, s_, Hkv)
        dv_ = jnp.where(is_ctx, dvc, dvl)
        return (jnp.concatenate([dq_, dk_, dv_], axis=1),), ()

    def ctx_map(i):
        return (jnp.clip(i - nl, 0, ctx_blocks - 1), 0)

    assert T % tr == 0 and L % tr == 0 and CTX % tr == 0
    in_specs = [pl.BlockSpec((tr, Wq), lambda i: (jnp.minimum(i, nl - 1), 0)),
                pl.BlockSpec((tr, Wk), lambda i: (jnp.minimum(i, nl - 1), 0)),
                pl.BlockSpec((tr, Wk), lambda i: (jnp.minimum(i, nl - 1), 0)),
                pl.BlockSpec((tr, Wk), ctx_map), pl.BlockSpec((tr, Wk), ctx_map),
                pl.BlockSpec((tr, LANES), lambda i: (i, 0)), pl.BlockSpec((tr, LANES), lambda i: (i, 0))]

    def kern(dq_r, dkl_r, dvl_r, dkc_r, dvc_r, c_r, s_r, o_ref):
        is_ctx = pl.program_id(0) >= nl
        (out,), _ = body(is_ctx, dq_r[...], dkl_r[...], dvl_r[...], dkc_r[...], dvc_r[...], c_r[...], s_r[...])
        o_ref[...] = out.astype(o_ref.dtype)

    return pl.pallas_call(
        kern, name=name, grid=(T // tr,), in_specs=in_specs,
        out_specs=pl.BlockSpec((tr, Wq + 2 * Wk), lambda i: (i, 0)),
        out_shape=jax.ShapeDtypeStruct((T, Wq + 2 * Wk), BF16),
        compiler_params=_cparams("parallel"),
    )(dq, dk_lat, dv_lat, dk_ctx, dv_ctx, cos, sin)


def _my_place():
    x, y, c = lax.axis_index("x"), lax.axis_index("y"), lax.axis_index("c")
    return x, y, c


def _all_gather_small(v, *, name):
    R, C = v.shape

    def body(x_ref, out_ref, send_sems, recv_sems, local_sem):
        x, y, c = _my_place()
        me, sibling = (x, y, c), (x, y, 1 - c)
        chips = [(1 - x, y), (x, 1 - y), (1 - x, 1 - y)]

        def slot(px, py, pc):
            return out_ref.at[4 * px + 2 * py + pc]

        def copy(k, block, to, src=None):
            return pltpu.make_async_remote_copy(
                src_ref=slot(*block) if src is None else src, dst_ref=slot(*block),
                send_sem=send_sems.at[k], recv_sem=recv_sems.at[k], device_id=to, device_id_type=MESH)

        mine = pltpu.make_async_copy(x_ref, slot(*me), local_sem)
        mine.start()
        first = [copy(0, me, sibling, src=x_ref)]
        first += [copy(1 + j, me, (*chip, c), src=x_ref) for j, chip in enumerate(chips)]
        for cp in first:
            cp.start()
        passed = [copy(4 + j, (*chip, c), sibling) for j, chip in enumerate(chips)]
        for j, chip in enumerate(chips):
            copy(1 + j, (*chip, c), me).wait_recv()
            passed[j].start()
        copy(0, sibling, me).wait_recv()
        for j, chip in enumerate(chips):
            copy(4 + j, (*chip, 1 - c), me).wait_recv()
        for cp in first + passed:
            cp.wait_send()
        mine.wait()

    return pl.pallas_call(
        body, name=name, out_shape=jax.ShapeDtypeStruct((N_DEV, R, C), v.dtype),
        in_specs=[pl.BlockSpec(memory_space=pltpu.VMEM)], out_specs=pl.BlockSpec(memory_space=pltpu.VMEM),
        scratch_shapes=[pltpu.SemaphoreType.DMA((7,)), pltpu.SemaphoreType.DMA((7,)), pltpu.SemaphoreType.DMA],
    )(v)


def _shard_slice(ref, axis, idx, size):
    if axis == 1:
        return ref.at[:, pl.ds(idx * size, size), :]
    return ref.at[:, :, pl.ds(idx * size, size)]


def _all_gather_weights(shards, axes, *, name):
    nt = len(shards)
    sizes = [s.shape[a] for s, a in zip(shards, axes)]
    out_shape = []
    for s, a in zip(shards, axes):
        shp = list(s.shape)
        shp[a] *= N_DEV
        out_shape.append(jax.ShapeDtypeStruct(tuple(shp), s.dtype))

    def body(*refs):
        ins, outs = refs[:nt], refs[nt:2 * nt]
        send_sems, recv_sems, local_sems = refs[2 * nt:]
        x, y, c = _my_place()
        me, sibling = (x, y, c), (x, y, 1 - c)
        chips = [(1 - x, y), (x, 1 - y), (1 - x, 1 - y)]
        all_sends = []
        locals_ = []
        for t in range(nt):
            def slot(px, py, pc, t=t):
                return _shard_slice(outs[t], axes[t], 4 * px + 2 * py + pc, sizes[t])

            def copy(k, block, to, src=None, t=t, slot=slot):
                return pltpu.make_async_remote_copy(
                    src_ref=slot(*block) if src is None else src, dst_ref=slot(*block),
                    send_sem=send_sems.at[t, k], recv_sem=recv_sems.at[t, k], device_id=to, device_id_type=MESH)

            mine = pltpu.make_async_copy(ins[t], slot(*me), local_sems.at[t])
            mine.start()
            locals_.append(mine)
            first = [copy(0, me, sibling, src=ins[t])]
            first += [copy(1 + j, me, (*chip, c), src=ins[t]) for j, chip in enumerate(chips)]
            for cp in first:
                cp.start()
            all_sends += first
        for t in range(nt):
            def slot(px, py, pc, t=t):
                return _shard_slice(outs[t], axes[t], 4 * px + 2 * py + pc, sizes[t])

            def copy(k, block, to, t=t, slot=slot):
                return pltpu.make_async_remote_copy(
                    src_ref=slot(*block), dst_ref=slot(*block),
                    send_sem=send_sems.at[t, k], recv_sem=recv_sems.at[t, k], device_id=to, device_id_type=MESH)

            passed = [copy(4 + j, (*chip, c), sibling) for j, chip in enumerate(chips)]
            for j, chip in enumerate(chips):
                copy(1 + j, (*chip, c), me).wait_recv()
                passed[j].start()
            all_sends += passed
        for t in range(nt):
            def slot(px, py, pc, t=t):
                return _shard_slice(outs[t], axes[t], 4 * px + 2 * py + pc, sizes[t])

            def copy(k, block, to, t=t, slot=slot):
                return pltpu.make_async_remote_copy(
                    src_ref=slot(*block), dst_ref=slot(*block),
                    send_sem=send_sems.at[t, k], recv_sem=recv_sems.at[t, k], device_id=to, device_id_type=MESH)

            copy(0, sibling, me).wait_recv()
            for j, chip in enumerate(chips):
                copy(4 + j, (*chip, 1 - c), me).wait_recv()
        for cp in all_sends:
            cp.wait_send()
        for mine in locals_:
            mine.wait()

    return pl.pallas_call(
        body, name=name, out_shape=out_shape,
        in_specs=[pl.BlockSpec(memory_space=pl.ANY)] * nt, out_specs=[pl.BlockSpec(memory_space=pl.ANY)] * nt,
        scratch_shapes=[pltpu.SemaphoreType.DMA((nt, 7)), pltpu.SemaphoreType.DMA((nt, 7)),
                        pltpu.SemaphoreType.DMA((nt,))],
    )(*shards)


def _exchange_grads(grads, axes, *, name):
    nt = len(grads)
    sizes = [g.shape[a] // N_DEV for g, a in zip(grads, axes)]
    out_shape = []
    for g, a, sz in zip(grads, axes, sizes):
        shp = list(g.shape)
        shp[a] = sz
        out_shape.append(jax.ShapeDtypeStruct((N_DEV, *shp), g.dtype))

    def body(*refs):
        ins, outs = refs[:nt], refs[nt:2 * nt]
        send_sems, recv_sems, local_sems = refs[2 * nt:]
        x, y, c = _my_place()
        my_idx = 4 * x + 2 * y + c

        def peer(r):
            px = (1 - x) if r & 4 else x
            py = (1 - y) if r & 2 else y
            pc = (1 - c) if r & 1 else c
            return (px, py, pc)

        copies, locals_ = [], []
        for t in range(nt):
            mine = pltpu.make_async_copy(_shard_slice(ins[t], axes[t], my_idx, sizes[t]), outs[t].at[my_idx],
                                         local_sems.at[t])
            mine.start()
            locals_.append(mine)
            for r in range(1, N_DEV):
                p = peer(r)
                p_idx = 4 * p[0] + 2 * p[1] + p[2]
                cp = pltpu.make_async_remote_copy(
                    src_ref=_shard_slice(ins[t], axes[t], p_idx, sizes[t]), dst_ref=outs[t].at[my_idx],
                    send_sem=send_sems.at[t, r - 1], recv_sem=recv_sems.at[t, r - 1], device_id=p, device_id_type=MESH)
                cp.start()
                copies.append((cp, t, r, p_idx))
        for cp, t, r, p_idx in copies:
            pltpu.make_async_remote_copy(
                src_ref=_shard_slice(ins[t], axes[t], p_idx, sizes[t]), dst_ref=outs[t].at[p_idx],
                send_sem=send_sems.at[t, r - 1], recv_sem=recv_sems.at[t, r - 1], device_id=peer(r),
                device_id_type=MESH).wait_recv()
        for cp, _, _, _ in copies:
            cp.wait_send()
        for mine in locals_:
            mine.wait()

    return pl.pallas_call(
        body, name=name, out_shape=out_shape,
        in_specs=[pl.BlockSpec(memory_space=pl.ANY)] * nt, out_specs=[pl.BlockSpec(memory_space=pl.ANY)] * nt,
        scratch_shapes=[pltpu.SemaphoreType.DMA((nt, 7)), pltpu.SemaphoreType.DMA((nt, 7)),
                        pltpu.SemaphoreType.DMA((nt,))],
    )(*grads)


_HBM_SPEC = pl.BlockSpec(memory_space=pltpu.HBM)
_SEM_SPEC = pl.BlockSpec(memory_space=pltpu.SEMAPHORE)
_ANY_SPEC = pl.BlockSpec(memory_space=pl.ANY)
_DATAFLOW = pltpu.SideEffectType.DATAFLOW_SIDE_EFFECTING
N_PEERS = N_DEV - 1


def _peer(r):
    x, y, c = _my_place()
    return ((1 - x) if r & 4 else x, (1 - y) if r & 2 else y, (1 - c) if r & 1 else c)


def _index_of(place):
    return 4 * place[0] + 2 * place[1] + place[2]


def _slot(ref, axis, idx, size):
    if axis == 0:
        return ref.at[pl.ds(idx * size, size), :]
    return ref.at[:, pl.ds(idx * size, size)]


def _cast_place(w3, layer, axis, me_arr, *, name):
    Ks, Ns = w3.shape[1], w3.shape[2]
    tr = _pick(Ks, 256, 16)
    nblk = Ks // tr
    full = (Ks * N_DEV, Ns) if axis == 0 else (Ks, Ns * N_DEV)
    if axis == 0:
        out_map = lambda i, me: (me[0] * nblk + i, 0)
    else:
        out_map = lambda i, me: (i, me[0])

    def body(me_ref, w_ref, o_ref):
        o_ref[...] = w_ref[...].astype(BF16)

    return pl.pallas_call(
        body, name=name, out_shape=jax.ShapeDtypeStruct(full, BF16),
        grid_spec=pltpu.PrefetchScalarGridSpec(
            num_scalar_prefetch=1, grid=(nblk,),
            in_specs=[pl.BlockSpec((None, tr, Ns), lambda i, me: (layer, i, 0))],
            out_specs=pl.BlockSpec((tr, Ns), out_map)),
        compiler_params=_cparams("parallel"),
    )(me_arr, w3)


def _gather_start(lands, axes, after, *, name):
    nt = len(lands)
    sizes = [l.shape[a] // N_DEV for l, a in zip(lands, axes)]

    def body(*refs):
        ins, send_sems, recv_sems, token = refs[:nt], refs[nt + 1], refs[nt + 2], refs[-1]
        my_idx = _index_of(_my_place())
        for t in range(nt):
            mine = _slot(ins[t], axes[t], my_idx, sizes[t])
            for r in range(1, N_DEV):
                k = t * N_PEERS + r - 1
                pltpu.make_async_remote_copy(src_ref=mine, dst_ref=mine, send_sem=send_sems.at[k],
                                             recv_sem=recv_sems.at[k], device_id=_peer(r), device_id_type=MESH).start()
        token[...] = jnp.zeros_like(token)

    res = pl.pallas_call(
        body, name=name,
        out_shape=(pltpu.SemaphoreType.DMA((nt * N_PEERS,)), pltpu.SemaphoreType.DMA((nt * N_PEERS,)),
                   *[pltpu.HBM(l.shape, l.dtype) for l in lands], jax.ShapeDtypeStruct((8, LANES), F32)),
        in_specs=[_HBM_SPEC] * nt + [_ANY_SPEC],
        out_specs=(_SEM_SPEC, _SEM_SPEC, *[_HBM_SPEC] * nt, pl.BlockSpec(memory_space=pltpu.VMEM)),
        input_output_aliases={t: 2 + t for t in range(nt)},
        compiler_params=pltpu.CompilerParams(has_side_effects=_DATAFLOW),
    )(*[pltpu.with_memory_space_constraint(l, pltpu.HBM) for l in lands], after)
    return res[0], res[1], list(res[2:2 + nt]), res[-1]


def _gather_wait(send_sems, recv_sems, lands, axes, after, *, name):
    nt = len(lands)
    sizes = [l.shape[a] // N_DEV for l, a in zip(lands, axes)]

    def body(*refs):
        ins, send_sems, recv_sems = refs[:nt], refs[nt], refs[nt + 1]
        my_idx = _index_of(_my_place())
        for t in range(nt):
            for r in range(1, N_DEV):
                p = _peer(r)
                k = t * N_PEERS + r - 1
                cp = pltpu.make_async_remote_copy(
                    src_ref=_slot(ins[t], axes[t], my_idx, sizes[t]), dst_ref=_slot(ins[t], axes[t], _index_of(p), sizes[t]),
                    send_sem=send_sems.at[k], recv_sem=recv_sems.at[k], device_id=p, device_id_type=MESH)
                cp.wait_send()
                cp.wait_recv()

    res = pl.pallas_call(
        body, name=name, out_shape=[pltpu.HBM(l.shape, l.dtype) for l in lands],
        in_specs=[_HBM_SPEC] * nt + [_SEM_SPEC, _SEM_SPEC, _ANY_SPEC], out_specs=[_HBM_SPEC] * nt,
        input_output_aliases={t: t for t in range(nt)},
        compiler_params=pltpu.CompilerParams(has_side_effects=_DATAFLOW),
    )(*lands, send_sems, recv_sems, after)
    return list(res)


def _scatter_start(dw, axis, *, name):
    size = dw.shape[axis] // N_DEV
    land_shape = (N_PEERS, size, dw.shape[1]) if axis == 0 else (N_PEERS, dw.shape[0], size)

    def body(dw_ref, land_ref, send_sems, recv_sems, dw_thru, land_thru, token):
        for r in range(1, N_DEV):
            p = _peer(r)
            pltpu.make_async_remote_copy(src_ref=_slot(dw_ref, axis, _index_of(p), size), dst_ref=land_ref.at[r - 1],
                                         send_sem=send_sems.at[r - 1], recv_sem=recv_sems.at[r - 1], device_id=p,
                                         device_id_type=MESH).start()
        token[...] = jnp.zeros_like(token)

    land = pltpu.with_memory_space_constraint(lax.empty(land_shape, dw.dtype), pltpu.HBM)
    return pl.pallas_call(
        body, name=name,
        out_shape=(pltpu.SemaphoreType.DMA((N_PEERS,)), pltpu.SemaphoreType.DMA((N_PEERS,)),
                   pltpu.HBM(dw.shape, dw.dtype), pltpu.HBM(land_shape, dw.dtype), jax.ShapeDtypeStruct((8, LANES), F32)),
        in_specs=[_HBM_SPEC, _HBM_SPEC],
        out_specs=(_SEM_SPEC, _SEM_SPEC, _HBM_SPEC, _HBM_SPEC, pl.BlockSpec(memory_space=pltpu.VMEM)),
        input_output_aliases={0: 2, 1: 3},
        compiler_params=pltpu.CompilerParams(has_side_effects=_DATAFLOW),
    )(pltpu.with_memory_space_constraint(dw, pltpu.HBM), land)


def _scatter_wait(send_sems, recv_sems, dw, land, axis, after, *, name):
    size = dw.shape[axis] // N_DEV

    def body(dw_ref, land_ref, send_sems, recv_sems, after_ref, dw_thru, land_thru):
        for r in range(1, N_DEV):
            p = _peer(r)
            cp = pltpu.make_async_remote_copy(src_ref=_slot(dw_ref, axis, _index_of(p), size), dst_ref=land_ref.at[r - 1],
                                              send_sem=send_sems.at[r - 1], recv_sem=recv_sems.at[r - 1], device_id=p,
                                              device_id_type=MESH)
            cp.wait_send()
            cp.wait_recv()

    return pl.pallas_call(
        body, name=name, out_shape=(pltpu.HBM(dw.shape, dw.dtype), pltpu.HBM(land.shape, land.dtype)),
        in_specs=[_HBM_SPEC, _HBM_SPEC, _SEM_SPEC, _SEM_SPEC, _ANY_SPEC], out_specs=(_HBM_SPEC, _HBM_SPEC),
        input_output_aliases={0: 0, 1: 1},
        compiler_params=pltpu.CompilerParams(has_side_effects=_DATAFLOW),
    )(dw, land, send_sems, recv_sems, after)


def _adamw_math(w, g, m, v):
    m = ADAM_B1 * m + (1.0 - ADAM_B1) * g
    v = ADAM_B2 * v + (1.0 - ADAM_B2) * (g * g)
    m_hat = m / (1.0 - ADAM_B1 ** ADAM_STEP)
    v_hat = v / (1.0 - ADAM_B2 ** ADAM_STEP)
    delta = -ADAM_LR * (m_hat / (jnp.sqrt(v_hat) + ADAM_EPS) + ADAM_WD * w)
    return delta, m, v


def _adamw_sharded(w, m, v, layer, dw, land, axis, me_arr, prev, *, name):
    nl, Ks, Ns = w.shape
    tr = _pick(Ks, 128, 16)
    nblk = Ks // tr
    if axis == 0:
        own_map = lambda i, me: (me[0] * nblk + i, 0)
    else:
        own_map = lambda i, me: (i, me[0])
    wspec = pl.BlockSpec((None, tr, Ns), lambda i, me: (layer, i, 0))
    n_prev = 0 if prev is None else 4

    def body(me_ref, w_ref, m_ref, v_ref, own_ref, r_ref, *rest):
        g_ref, d_ref, nm_ref, nv_ref = rest[n_prev:]
        g = own_ref[...].astype(F32)
        for r in range(N_PEERS):
            g = g + r_ref[r].astype(F32)
        delta, nm, nv = _adamw_math(w_ref[...], g, m_ref[...], v_ref[...])
        g_ref[...], d_ref[...], nm_ref[...], nv_ref[...] = g, delta, nm, nv

    return pl.pallas_call(
        body, name=name, out_shape=[jax.ShapeDtypeStruct((nl, Ks, Ns), F32)] * 4,
        grid_spec=pltpu.PrefetchScalarGridSpec(
            num_scalar_prefetch=1, grid=(nblk,),
            in_specs=[wspec, wspec, wspec, pl.BlockSpec((tr, Ns), own_map),
                      pl.BlockSpec((N_PEERS, tr, Ns), lambda i, me: (0, i, 0))] + [_ANY_SPEC] * n_prev,
            out_specs=[wspec] * 4),
        input_output_aliases={6 + k: k for k in range(n_prev)},
        compiler_params=_cparams("parallel"),
    )(me_arr, w, m, v, dw, land, *(prev or []))


def _adamw_flat(w, g, m, v, *, name):
    def body(w_ref, g_ref, m_ref, v_ref, d_ref, nm_ref, nv_ref):
        d_ref[...], nm_ref[...], nv_ref[...] = _adamw_math(w_ref[...], g_ref[...], m_ref[...], v_ref[...])

    spec = pl.BlockSpec(memory_space=pltpu.VMEM)
    return pl.pallas_call(body, name=name, in_specs=[spec] * 4, out_specs=[spec] * 3,
                          out_shape=[jax.ShapeDtypeStruct(w.shape, F32)] * 3)(w, g, m, v)


def _sum_devices(a, *, name):
    def body(a_ref, o_ref):
        s = a_ref[0]
        for d in range(1, N_DEV):
            s = s + a_ref[d]
        o_ref[...] = s

    spec = pl.BlockSpec(memory_space=pltpu.VMEM)
    return pl.pallas_call(body, name=name, in_specs=[spec], out_specs=spec,
                          out_shape=jax.ShapeDtypeStruct(a.shape[1:], F32))(a)


def _ada_mods(c16, ada_w, ada_b_cols, *, name):
    nl, D, cols = ada_w.shape
    bn = _pick(cols, 512)

    def body(c_ref, w_ref, b_ref, o_ref):
        cond = _silu(c_ref[...]).astype(BF16)
        o_ref[...] = _dot(cond, w_ref[...].astype(BF16)) + b_ref[...]

    return pl.pallas_call(
        body, name=name, grid=(nl, cols // bn),
        in_specs=[pl.BlockSpec((16, D), lambda l, j: (0, 0)), pl.BlockSpec((None, D, bn), lambda l, j: (l, 0, j)),
                  pl.BlockSpec((None, 1, bn), lambda l, j: (l, 0, j))],
        out_specs=pl.BlockSpec((None, 16, bn), lambda l, j: (l, 0, j)),
        out_shape=jax.ShapeDtypeStruct((nl, 16, cols), F32),
        compiler_params=_cparams("parallel", "parallel"),
    )(c16, ada_w, ada_b_cols)


def _ada_bwd(cond_t, dmod, w, m, v, *, name):
    nl, D, cols = w.shape
    tr = _pick(D, 256, 8)

    def body(ct_ref, dm_ref, w_ref, m_ref, v_ref, g_ref, d_ref, nm_ref, nv_ref, dc_ref):
        ct, dm, wt = ct_ref[...], dm_ref[...], w_ref[...]
        g = ct[:, 0:1] * dm[0:1, :]
        for r in range(1, N_DEV + 1):
            g = g + ct[:, r:r + 1] * dm[r:r + 1, :]
        delta, nm, nv = _adamw_math(wt, g, m_ref[...], v_ref[...])
        g_ref[...], d_ref[...], nm_ref[...], nv_ref[...] = g, delta, nm, nv
        dc_ref[...] = jnp.sum(wt * dm[N_DEV:N_DEV + 1, :], axis=-1, keepdims=True)

    wspec = pl.BlockSpec((None, tr, cols), lambda l, i: (l, i, 0))
    return pl.pallas_call(
        body, name=name, grid=(nl, D // tr),
        in_specs=[pl.BlockSpec((tr, 16), lambda l, i: (i, 0)), pl.BlockSpec((None, 16, cols), lambda l, i: (l, 0, 0)),
                  wspec, wspec, wspec],
        out_specs=[wspec] * 4 + [pl.BlockSpec((None, tr, 1), lambda l, i: (l, i, 0))],
        out_shape=[jax.ShapeDtypeStruct((nl, D, cols), F32)] * 4 + [jax.ShapeDtypeStruct((nl, D, 1), F32)],
        compiler_params=_cparams("parallel", "parallel"),
    )(cond_t, dmod, w, m, v)


def _rope_tables(L, CTX):
    def angles(pos, dim):
        inv_freq = ROPE_BASE ** (-jnp.arange(0, dim, 2, dtype=F32) / dim)
        return pos.astype(F32)[:, None] * inv_freq[None, :]

    def pad(cos, sin):
        return (jnp.concatenate([cos, jnp.ones((CTX, LANES), F32)], 0),
                jnp.concatenate([sin, jnp.zeros((CTX, LANES), F32)], 0))

    ret = angles(jnp.arange(L), 2 * LANES)
    ret_cs = pad(jnp.cos(ret), jnp.sin(ret))
    rows = angles(jnp.arange(L) // GRID_W, ATT_HEAD_DIM // 2)
    cols = angles(jnp.arange(L) % GRID_W, ATT_HEAD_DIM // 2)
    cos = jnp.concatenate([jnp.cos(rows)] * 2 + [jnp.cos(cols)] * 2, axis=1)
    sin = jnp.concatenate([-jnp.sin(rows), jnp.sin(rows), -jnp.sin(cols), jnp.sin(cols)], axis=1)
    return ret_cs, pad(cos, sin)


def kernel(x, c, ctx, c_ctx, ada_w, ada_b, norm_mix_g, norm_mlp_g, mlp_w1, mlp_w2, ret_w_in, ret_w_out, ret_decay_fwd, ret_decay_bwd, attn_w_in, attn_w_out, attn_sink, final_norm_g, loss_target, m_c_ctx, m_ada_w, m_ada_b, m_norm_mix_g, m_norm_mlp_g, m_mlp_w1, m_mlp_w2, m_ret_w_in, m_ret_w_out, m_ret_decay_fwd, m_ret_decay_bwd, m_attn_w_in, m_attn_w_out, m_attn_sink, m_final_norm_g, v_c_ctx, v_ada_w, v_ada_b, v_norm_mix_g, v_norm_mlp_g, v_mlp_w1, v_mlp_w2, v_ret_w_in, v_ret_w_out, v_ret_decay_fwd, v_ret_decay_bwd, v_attn_w_in, v_attn_w_out, v_attn_sink, v_final_norm_g):
    L, D = x.shape[1], x.shape[2]
    CTX = ctx.shape[1]
    T = L + CTX
    RH = ret_decay_fwd.shape[-1]
    assert D == RH * 2 * LANES and ada_w.shape[0] == 2 and ret_w_in.shape[0] == 1 and attn_w_in.shape[0] == 1
    Hq = attn_sink.shape[-1]
    Hkv = (attn_w_in.shape[-1] * N_DEV // ATT_HEAD_DIM - Hq) // 2
    G = Hq // Hkv
    FF = mlp_w1.shape[-1] * N_DEV
    Wq_r, Wv_r = RH * 2 * LANES, RH * 4 * LANES
    acols = ada_w.shape[-1]
    tr = _pick(CTX, 256, 8)
    tr_wide = _pick(CTX, 128, 8)
    bmT = T // 4 if (T % 64 == 0) else T
    bmL = L // 4 if (L % 64 == 0) else L
    x_idx, y_idx, c_idx = lax.axis_index("x"), lax.axis_index("y"), lax.axis_index("c")
    me = 4 * x_idx + 2 * y_idx + c_idx
    me_arr = jnp.reshape(me, (1,)).astype(jnp.int32)

    (rcos, rsin), (acos, asin) = _rope_tables(L, CTX)
    lg_f = jax.nn.log_sigmoid(ret_decay_fwd.astype(F32))
    lg_b = jax.nn.log_sigmoid(ret_decay_bwd.astype(F32))

    wdefs = {"ret_in": (ret_w_in, 0, 1), "ret_out": (ret_w_out, 0, 0), "w1_0": (mlp_w1, 0, 1), "w2_0": (mlp_w2, 0, 0),
             "attn_in": (attn_w_in, 0, 1), "attn_out": (attn_w_out, 0, 0), "w1_1": (mlp_w1, 1, 1), "w2_1": (mlp_w2, 1, 0)}
    groups = [["ret_in"], ["ret_out", "w1_0", "w2_0"], ["attn_in", "attn_out", "w1_1", "w2_1"]]
    placed = {k: _cast_place(w3, ly, ax, me_arr, name=f"place_{k}") for k, (w3, ly, ax) in wdefs.items()}
    gathers, tok = [], me_arr
    for gi, keys in enumerate(groups):
        g_axes = [wdefs[k][2] for k in keys]
        ssem, rsem, lands, tok = _gather_start([placed[k] for k in keys], g_axes, tok, name=f"ag_start{gi}")
        gathers.append((ssem, rsem, lands, g_axes))

    c_pad = jnp.concatenate([c.astype(F32) + tok[0, 0], jnp.zeros((7, D), F32)], 0)
    c_all = _all_gather_small(c_pad, name="ag_c")[:, 0, :]
    c16 = jnp.concatenate([c_all, c_ctx[None, :], jnp.zeros((7, D), F32)], 0)
    ada_b_cols = lax.dynamic_slice_in_dim(ada_b, me * acols, acols, axis=1)[:, None, :]
    mods_shard = _ada_mods(c16, ada_w, ada_b_cols, name="ada_mods")
    mods_all = _all_gather_small(mods_shard.reshape(32, acols), name="ag_mods")
    mods_all = mods_all.reshape(N_DEV, 2, 16, acols).transpose(1, 2, 0, 3).reshape(2, 16, 6, D)
    mod_lat = lax.dynamic_index_in_dim(mods_all, me, axis=1, keepdims=False)
    mod_ctx = mods_all[:, N_DEV]

    def pack(i, ks, kc):
        return jnp.stack([mod_lat[i, ks], mod_lat[i, kc], mod_ctx[i, ks], mod_ctx[i, kc]], 0)

    def gates(i, k):
        return jnp.stack([mod_lat[i, k], mod_ctx[i, k]], 0)

    def gate_epilogue(gl, gc, x_rows_lat_only):
        def epi(acc, i, j, xt, gv):
            if x_rows_lat_only:
                gate = gv[0:1, :]
            else:
                row = i * acc.shape[0] + lax.broadcasted_iota(jnp.int32, (acc.shape[0], 1), 0)
                gate = jnp.where(row >= L, gv[1:2, :], gv[0:1, :])
            return xt + gate * acc, acc
        return epi

    w1, w2 = {}, {}

    mmT = dict(M=T, bm=bmT)
    mmL = dict(M=L, bm=bmL)

    def bn_of(n, off=0):
        b = 4 * LANES
        while n % b or off % b:
            b -= LANES
        return b

    X0 = jnp.concatenate([x[0], ctx[0]], axis=0)
    g_mix0, g_mlp0 = norm_mix_g[0:1], norm_mlp_g[0:1]
    g_mix1, g_mlp1 = norm_mix_g[1:2], norm_mlp_g[1:2]
    a0 = _normmod(X0, g_mix0, pack(0, 0, 1), R=T, L=L, tr=tr, name="normmod_mix0")
    (wr_in,) = _gather_wait(*gathers[0][:3], gathers[0][3], a0, name="ag_wait0")

    bn_qk = _pick(2 * Wq_r, 512, 2 * LANES)
    nq_blocks = Wq_r // bn_qk
    kscale = float(2 * LANES) ** -0.5

    def rope_epi(acc, i, j, cos, sin):
        parts = []
        for h in range(acc.shape[1] // (2 * LANES)):
            x1 = acc[:, h * 2 * LANES:h * 2 * LANES + LANES]
            x2 = acc[:, h * 2 * LANES + LANES:(h + 1) * 2 * LANES]
            parts += [x1 * cos - x2 * sin, x2 * cos + x1 * sin]
        return (jnp.concatenate(parts, axis=1) * jnp.where(j < nq_blocks, 1.0, kscale),)

    def row_tile(arr, bm):
        return (arr, (bm, LANES), lambda i, j: (i, 0))

    (qk0,) = _mm(a0, wr_in, "nn", [BF16], N=2 * Wq_r, K=D, bn=bn_qk, bk=D, name="ret_qk", epilogue=rope_epi,
                 extras=[row_tile(rcos, bmT), row_tile(rsin, bmT)], **mmT)
    bn_vg = bn_of(2 * Wv_r, 2 * Wq_r)
    (vg0,) = _mm(a0, wr_in, "nn", [BF16], N=2 * Wv_r, K=D, bn=bn_vg, bk=D, name="ret_vg", b_col0=2 * Wq_r, **mmT)

    of, st_f = _ret_fwd(qk0, vg0, lg_f, T=T, L=L, H=RH, rev=False, name="ret_scan_f")
    ob, st_b = _ret_fwd(qk0, vg0, lg_b, T=T, L=L, H=RH, rev=True, name="ret_scan_b")
    z0 = _readout(of, ob, vg0, T=T, L=L, H=RH, tr=tr_wide, name="ret_readout")
    wr_out, w1[0], w2[0] = _gather_wait(*gathers[1][:3], gathers[1][3], z0, name="ag_wait1")

    bnD = _pick(D, 512)

    def xtile(arr, bm):
        return (arr, (bm, bnD), lambda i, j: (i, j))

    def gtile(gv):
        return (gv, (2, bnD), lambda i, j: (0, j))

    bk_v = _pick(Wv_r, 2048)
    X1, ro0 = _mm(z0, wr_out, "nn", [F32, BF16], N=D, K=Wv_r, bn=bnD, bk=bk_v, name="ret_out",
                  epilogue=gate_epilogue(None, None, False), extras=[xtile(X0, bmT), gtile(gates(0, 2))], **mmT)

    def mlp_fwd(Xin, i, g_mlp, rows, name):
        a = _normmod(Xin, g_mlp, pack(i, 3, 4), R=rows["M"], L=L, tr=tr, name=f"normmod_mlp{name}")

        def relu2(acc, i_, j_):
            u = jnp.maximum(acc, 0.0)
            return u, u * u

        bnF = _pick(FF, 512)
        u, r = _mm(a, w1[i], "nn", [BF16, BF16], N=FF, K=D, bn=bnF, bk=D, name=f"mlp_up{name}", epilogue=relu2, **rows)
        bkF = _pick(FF, 2048)
        Xout, mo = _mm(r, w2[i], "nn", [F32, BF16], N=D, K=FF, bn=bnD, bk=bkF, name=f"mlp_down{name}",
                       epilogue=gate_epilogue(None, None, rows["M"] == L),
                       extras=[xtile(Xin, rows["bm"]), gtile(gates(i, 5))], **rows)
        return a, u, r, Xout, mo

    a1, u0, r0, X2, mo0 = mlp_fwd(X1, 0, g_mlp0, mmT, "0")

    a2 = _normmod(X2, g_mix1, pack(1, 0, 1), R=T, L=L, tr=tr, name="normmod_mix1")
    wa_in, wa_out, w1[1], w2[1] = _gather_wait(*gathers[2][:3], gathers[2][3], a2, name="ag_wait2")
    Wq_a, Wk_a = Hq * LANES, Hkv * LANES

    def arope_epi(acc, i, j, cos, sin):
        heads = acc.shape[1] // LANES
        return (acc * jnp.tile(cos, (1, heads)) + _swap32(acc) * jnp.tile(sin, (1, heads)),)

    bn_q = _pick(Wq_a, 512)
    (q1,) = _mm(a2, wa_in, "nn", [BF16], N=Wq_a, K=D, bn=bn_q, bk=D, name="attn_q", epilogue=arope_epi,
                extras=[row_tile(acos, bmL), row_tile(asin, bmL)], **mmL)
    bn_k = bn_of(Wk_a, Wq_a)
    (k1,) = _mm(a2, wa_in, "nn", [BF16], N=Wk_a, K=D, bn=bn_k, bk=D, name="attn_k", b_col0=Wq_a, epilogue=arope_epi,
                extras=[row_tile(acos, bmT), row_tile(asin, bmT)], **mmT)
    bn_v = bn_of(Wk_a, Wq_a + Wk_a)
    (v1,) = _mm(a2, wa_in, "nn", [BF16], N=Wk_a, K=D, bn=bn_v, bk=D, name="attn_v", b_col0=Wq_a + Wk_a, **mmT)
    o1, lse = _attn_fwd(q1, k1, v1, attn_sink, L=L, CTX=CTX, Hkv=Hkv, G=G, name="attn_fwd")
    X3, ao = _mm(o1, wa_out, "nn", [F32, BF16], N=D, K=Wq_a, bn=bnD, bk=_pick(Wq_a, 2048), name="attn_out",
                 epilogue=gate_epilogue(None, None, True), extras=[xtile(X2, bmL), gtile(gates(1, 2))], **mmL)
    a3, u1, r1, X4, mo1 = mlp_fwd(X3, 1, g_mlp1, mmL, "1")

    dX4, dmo1, acc_head = _loss_head(X4, loss_target[0], mo1, final_norm_g[None, :], gates(1, 5)[0:1], L=L, tr=tr,
                                     name="loss_head")
    loss_part = jnp.sum(acc_head[0, 0])
    d_gf = acc_head[0, 1]
    zeros_d = jnp.zeros((D,), F32)
    dmod_lat = [[zeros_d] * 6, [zeros_d] * 6]
    dmod_ctx = [[zeros_d] * 6, [zeros_d] * 6]
    dmod_lat[1][5] = acc_head[0, 2]

    bn_dw = 512

    def mlp_bwd(dmo, a, u, r, i, rows, name):
        Mr = rows["M"]
        bkr = rows["bm"]

        def times_2u(acc, i_, j_, ut):
            return (acc * (2.0 * ut.astype(F32)),)

        bnF = _pick(FF, 512)
        (dw2,) = _mm(r, dmo, "tn", [BF16], M=FF, N=D, K=Mr, bm=_pick(FF, 1024), bn=_pick(D, bn_dw), bk=bkr,
                     name=f"mlp_down_dw{name}")
        tok_ = send_grad(f"w2_{i}", dw2, 0)
        (dh,) = _mm(dmo, w2[i], "nt", [BF16], N=FF, K=D, bn=bnF, bk=D, name=f"mlp_down_dx{name}", epilogue=times_2u,
                    extras=[(u, (rows["bm"], bnF), lambda i_, j_: (i_, j_))], dep=tok_, **rows)
        (dw1,) = _mm(a, dh, "tn", [BF16], M=D, N=FF, K=Mr, bm=_pick(D, 1024), bn=_pick(FF, bn_dw), bk=bkr,
                     name=f"mlp_up_dw{name}")
        tok_ = send_grad(f"w1_{i}", dw1, 1)
        (da,) = _mm(dh, w1[i], "nt", [F32], N=D, K=FF, bn=bnD, bk=_pick(FF, 2048), name=f"mlp_up_dx{name}", dep=tok_,
                    **rows)
        return da

    pending = []

    def send_grad(key, dw, axis):
        ssem, rsem, dw_thru, land, tok_ = _scatter_start(dw, axis, name=f"rs_start_{key}")
        pending.append((key, axis, ssem, rsem, dw_thru, land))
        return tok_

    da3 = mlp_bwd(dmo1, a3, u1, r1, 1, mmL, "1")
    dX3, dao, acc = _normmod_bwd(X3, da3, dX4, False, g_mlp1, pack(1, 3, 4), (ao, gates(1, 2)), R=L, L=L, tr=tr,
                                 name="normmod_mlp1_bwd")
    dmod_lat[1][3], dmod_lat[1][4], d_gmlp1, dmod_lat[1][2] = acc[0, 0], acc[0, 1], acc[0, 2], acc[0, 3]

    (dwa_out,) = _mm(o1, dao, "tn", [BF16], M=Wq_a, N=D, K=L, bm=_pick(Wq_a, 1024), bn=_pick(D, bn_dw), bk=bmL,
                     name="attn_out_dw")
    tok = send_grad("attn_out", dwa_out, 0)
    (do1,) = _mm(dao, wa_out, "nt", [BF16], N=Wq_a, K=D, bn=_pick(Wq_a, 512), bk=D, name="attn_out_dx", dep=tok, **mmL)
    dq1, delta1, dkx, dvx, dsink_acc = _attn_bwd_q(q1, k1, v1, do1, lse, attn_sink, L=L, CTX=CTX, Hkv=Hkv, G=G,
                                                   name="attn_bwd_q")
    dk1, dv1 = _attn_bwd_kv(q1, k1, v1, do1, lse, delta1, L=L, Hkv=Hkv, G=G, name="attn_bwd_kv")
    dp1 = _attn_assemble(dq1, dk1, dv1, dkx, dvx, acos, asin, T=T, L=L, CTX=CTX, Hq=Hq, Hkv=Hkv, tr=tr_wide,
                         name="attn_assemble")
    Wa_in = Wq_a + 2 * Wk_a
    (dwa_in,) = _mm(a2, dp1, "tn", [BF16], M=D, N=Wa_in, K=T, bm=_pick(D, 1024), bn=_pick(Wa_in, bn_dw), bk=bmT,
                    name="attn_in_dw")
    tok = send_grad("attn_in", dwa_in, 1)
    (da2,) = _mm(dp1, wa_in, "nt", [F32], N=D, K=Wa_in, bn=bnD, bk=_pick(Wa_in, 2048), name="attn_in_dx", dep=tok,
                 **mmT)
    dX2, dmo0, acc = _normmod_bwd(X2, da2, dX3, True, g_mix1, pack(1, 0, 1), (mo0, gates(0, 5)), R=T, L=L, tr=tr,
                                  name="normmod_mix1_bwd")
    dmod_lat[1][0], dmod_lat[1][1], d_gmix1, dmod_lat[0][5] = acc[0, 0], acc[0, 1], acc[0, 2] + acc[1, 2], acc[0, 3]
    dmod_ctx[1][0], dmod_ctx[1][1], dmod_ctx[0][5] = acc[1, 0], acc[1, 1], acc[1, 3]

    da1 = mlp_bwd(dmo0, a1, u0, r0, 0, mmT, "0")
    dX1, dro0, acc = _normmod_bwd(X1, da1, dX2, False, g_mlp0, pack(0, 3, 4), (ro0, gates(0, 2)), R=T, L=L, tr=tr,
                                  name="normmod_mlp0_bwd")
    dmod_lat[0][3], dmod_lat[0][4], d_gmlp0, dmod_lat[0][2] = acc[0, 0], acc[0, 1], acc[0, 2] + acc[1, 2], acc[0, 3]
    dmod_ctx[0][3], dmod_ctx[0][4], dmod_ctx[0][2] = acc[1, 0], acc[1, 1], acc[1, 3]

    (dwr_out,) = _mm(z0, dro0, "tn", [BF16], M=Wv_r, N=D, K=T, bm=_pick(Wv_r, 1024), bn=_pick(D, bn_dw), bk=bmT,
                     name="ret_out_dw")
    tok = send_grad("ret_out", dwr_out, 0)
    (dz0,) = _mm(dro0, wr_out, "nt", [BF16], N=Wv_r, K=D, bn=_pick(Wv_r, 512), bk=D, name="ret_out_dx", dep=tok, **mmT)
    do0, dg0 = _readout_bwd(dz0, of, ob, vg0, T=T, L=L, H=RH, tr=tr_wide, name="ret_readout_bwd")
    dq_f, dk_f, dv_f, dlg_f = _ret_bwd(qk0, vg0, do0, st_f, lg_f, T=T, L=L, H=RH, rev=False, name="ret_scan_f_bwd")
    dq_b, dk_b, dv_b, dlg_b = _ret_bwd(qk0, vg0, do0, st_b, lg_b, T=T, L=L, H=RH, rev=True, name="ret_scan_b_bwd")
    dp0 = _ret_assemble(dq_f, dq_b, dk_f, dk_b, dv_f, dv_b, dg0, rcos, rsin, T=T, L=L, H=RH, tr=tr_wide,
                        name="ret_assemble")
    Wr_in = 2 * Wq_r + 2 * Wv_r
    (dwr_in,) = _mm(a0, dp0, "tn", [BF16], M=D, N=Wr_in, K=T, bm=_pick(D, 1024), bn=_pick(Wr_in, bn_dw), bk=bmT,
                    name="ret_in_dw")
    tok = send_grad("ret_in", dwr_in, 1)
    (da0,) = _mm(dp0, wr_in, "nt", [F32], N=D, K=Wr_in, bn=bnD, bk=_pick(Wr_in, 2048), name="ret_in_dx", dep=tok, **mmT)
    dX0, acc = _normmod_bwd(X0, da0, dX1, False, g_mix0, pack(0, 0, 1), None, R=T, L=L, tr=tr, name="normmod_mix0_bwd")
    dmod_lat[0][0], dmod_lat[0][1], d_gmix0 = acc[0, 0], acc[0, 1], acc[0, 2] + acc[1, 2]
    dmod_ctx[0][0], dmod_ctx[0][1] = acc[1, 0], acc[1, 1]
    grad_x = dX0[:L][None]

    wmv = {"ret_in": (ret_w_in, m_ret_w_in, v_ret_w_in, 0, "ret_w_in"),
           "ret_out": (ret_w_out, m_ret_w_out, v_ret_w_out, 0, "ret_w_out"),
           "attn_in": (attn_w_in, m_attn_w_in, v_attn_w_in, 0, "attn_w_in"),
           "attn_out": (attn_w_out, m_attn_w_out, v_attn_w_out, 0, "attn_w_out"),
           "w1_0": (mlp_w1, m_mlp_w1, v_mlp_w1, 0, "mlp_w1"), "w1_1": (mlp_w1, m_mlp_w1, v_mlp_w1, 1, "mlp_w1"),
           "w2_0": (mlp_w2, m_mlp_w2, v_mlp_w2, 0, "mlp_w2"), "w2_1": (mlp_w2, m_mlp_w2, v_mlp_w2, 1, "mlp_w2")}
    big = {}

    def finish_grad(entry, after):
        key, axis, ssem, rsem, dw_thru, land = entry
        dw_done, land_done = _scatter_wait(ssem, rsem, dw_thru, land, axis, after, name=f"rs_wait_{key}")
        w_, m_, v_, layer, out_name = wmv[key]
        big[out_name] = _adamw_sharded(w_, m_, v_, layer, dw_done, land_done, axis, me_arr, big.get(out_name),
                                       name=f"adamw_{key}")
        return big[out_name][0]

    after = dX0
    for entry in pending[:-1]:
        after = finish_grad(entry, after)

    misc = jnp.zeros((D,), F32)
    misc = misc.at[0:RH].set(dlg_f[:, 0, 0]).at[RH:2 * RH].set(dlg_b[:, 0, 0])
    misc = misc.at[2 * RH:2 * RH + Hq].set(dsink_acc[:, :G, 0].reshape(Hq)).at[2 * RH + Hq].set(loss_part)
    rows = ([dmod_lat[i][k] for i in range(2) for k in range(6)] + [dmod_ctx[i][k] for i in range(2) for k in range(6)]
            + [d_gmix0, d_gmix1, d_gmlp0, d_gmlp1, d_gf, misc, zeros_d, zeros_d])
    part = jnp.stack(rows, 0)
    part_all = _all_gather_small(part, name="ag_small_grads")
    tot = _sum_devices(part_all, name="sum_small_grads")

    grad_ada_b = (tot[0:12] + tot[12:24]).reshape(2, 6 * D)
    grad_norm_mix_g, grad_norm_mlp_g, grad_final_norm_g = tot[24:26], tot[26:28], tot[28]
    grad_ret_decay_fwd = (tot[29, 0:RH] * jax.nn.sigmoid(-ret_decay_fwd[0]))[None]
    grad_ret_decay_bwd = (tot[29, RH:2 * RH] * jax.nn.sigmoid(-ret_decay_bwd[0]))[None]
    grad_attn_sink = tot[29, 2 * RH:2 * RH + Hq][None]
    loss = tot[29, 2 * RH + Hq]

    dlat_cols = lax.dynamic_slice_in_dim(part_all[:, 0:12].reshape(N_DEV, 2, 6 * D), me * acols, acols, axis=2)
    dctx_cols = lax.dynamic_slice_in_dim(tot[12:24].reshape(2, 6 * D), me * acols, acols, axis=1)
    dmod16 = jnp.concatenate([dlat_cols.transpose(1, 0, 2), dctx_cols[:, None, :], jnp.zeros((2, 7, acols), F32)], 1)
    cond_t = _silu(c16).T
    g_ada, d_ada, nm_ada, nv_ada, dcond_part = _ada_bwd(cond_t, dmod16, ada_w, m_ada_w, v_ada_w, name="ada_bwd")
    dcond = (dcond_part[0, :, 0] + dcond_part[1, :, 0]).reshape(D // LANES, LANES)
    pad_rows = -(D // LANES) % 8
    dcond_pad = jnp.concatenate([dcond, jnp.zeros((pad_rows, LANES), F32)], 0) if pad_rows else dcond
    dcond_all = _all_gather_small(dcond_pad, name="ag_dcond")
    dcond_tot = _sum_devices(dcond_all, name="sum_dcond")[:D // LANES].reshape(D)
    sg = jax.nn.sigmoid(c_ctx)
    grad_c_ctx = dcond_tot * (sg * (1.0 + c_ctx * (1.0 - sg)))

    small_w = [c_ctx, ada_b, norm_mix_g, norm_mlp_g, ret_decay_fwd, ret_decay_bwd, attn_sink, final_norm_g]
    small_g = [grad_c_ctx, grad_ada_b, grad_norm_mix_g, grad_norm_mlp_g, grad_ret_decay_fwd, grad_ret_decay_bwd,
               grad_attn_sink, grad_final_norm_g]
    small_m = [m_c_ctx, m_ada_b, m_norm_mix_g, m_norm_mlp_g, m_ret_decay_fwd, m_ret_decay_bwd, m_attn_sink,
               m_final_norm_g]
    small_v = [v_c_ctx, v_ada_b, v_norm_mix_g, v_norm_mlp_g, v_ret_decay_fwd, v_ret_decay_bwd, v_attn_sink,
               v_final_norm_g]
    sizes = [w_.size for w_ in small_w]
    total = sum(-(-s // LANES) * LANES for s in sizes)
    total_pad = -(-total // (8 * LANES)) * 8 * LANES

    def flat_pack(ts, fill):
        pieces = []
        for t_ in ts:
            f = t_.reshape(-1).astype(F32)
            pad = -f.size % LANES
            pieces.append(jnp.concatenate([f, jnp.full((pad,), fill, F32)]) if pad else f)
        pieces.append(jnp.full((total_pad - total,), fill, F32))
        return jnp.concatenate(pieces).reshape(total_pad // LANES, LANES)

    d_s, nm_s, nv_s = _adamw_flat(flat_pack(small_w, 0.0), flat_pack(small_g, 0.0), flat_pack(small_m, 0.0),
                                  flat_pack(small_v, 1.0), name="adamw_small")

    def unpack(p):
        flat = p.reshape(-1)
        res, off = [], 0
        for w_, s in zip(small_w, sizes):
            res.append(flat[off:off + s].reshape(w_.shape))
            off += -(-s // LANES) * LANES
        return res

    finish_grad(pending[-1], d_s)
    d_small, nm_small, nv_small = unpack(d_s), unpack(nm_s), unpack(nv_s)
    small_names = ["c_ctx", "ada_b", "norm_mix_g", "norm_mlp_g", "ret_decay_fwd", "ret_decay_bwd", "attn_sink",
                   "final_norm_g"]
    sm = {n: (g_, d_, m_, v_) for n, g_, d_, m_, v_ in zip(small_names, small_g, d_small, nm_small, nv_small)}

    def out4(n):
        if n == "ada_w":
            return g_ada, d_ada, nm_ada, nv_ada
        if n in big:
            return tuple(big[n])
        return sm[n]

    order = ["c_ctx", "ada_w", "ada_b", "norm_mix_g", "norm_mlp_g", "mlp_w1", "mlp_w2", "ret_w_in", "ret_w_out",
             "ret_decay_fwd", "ret_decay_bwd", "attn_w_in", "attn_w_out", "attn_sink", "final_norm_g"]
    quads = [out4(n) for n in order]
    return (loss, grad_x, *[q_[0] for q_ in quads], *[q_[1] for q_ in quads], *[q_[2] for q_ in quads],
            *[q_[3] for q_ in quads])
```

```python
import functools

import jax
import jax.numpy as jnp
from jax import lax
from jax.experimental import pallas as pl
from jax.experimental.pallas import tpu as pltpu

F32 = jnp.float32
BF16 = jnp.bfloat16

N_DEV = 8
NORM_EPS = 1e-6
CHUNK = 128
ATT_HEAD_DIM = 128
GRID_W = 64
ROPE_BASE = 10000.0
NEG_INF = -1e30
ADAM_LR, ADAM_B1, ADAM_B2, ADAM_EPS, ADAM_WD, ADAM_STEP = 0.001, 0.9, 0.999, 1e-08, 0.01, 10

V7X_VMEM_LIMIT_BYTES = 56 * 1024 * 1024
LANES = 128
MESH = pl.DeviceIdType.MESH

_NN = (((1,), (0,)), ((), ()))
_NT = (((1,), (1,)), ((), ()))
_TN = (((0,), (0,)), ((), ()))


def _dot(a, b, dn=_NN):
    return lax.dot_general(a, b, dn, preferred_element_type=F32)


def _cparams(*sem):
    return pltpu.CompilerParams(dimension_semantics=sem, vmem_limit_bytes=V7X_VMEM_LIMIT_BYTES)


def _pick(n, pref, mult=LANES):
    if n <= pref:
        return n
    best = None
    for d in range(mult, pref + 1, mult):
        if n % d == 0:
            best = d
    assert best is not None, (n, pref)
    return best


def _silu(x):
    return x * jax.nn.sigmoid(x)


def _mm(a, b, mode, out_dtypes, *, M, N, K, bm, bn, bk, name, b_col0=0, epilogue=None, extras=(), dep=None):
    assert M % bm == 0 and N % bn == 0 and K % bk == 0 and b_col0 % bn == 0, (name, M, N, K, bm, bn, bk, b_col0)
    nk = K // bk
    c0 = b_col0 // bn
    if mode == "nn":
        a_spec = pl.BlockSpec((bm, bk), lambda i, j, k: (i, k))
        b_spec = pl.BlockSpec((bk, bn), lambda i, j, k: (k, j + c0))
    elif mode == "nt":
        a_spec = pl.BlockSpec((bm, bk), lambda i, j, k: (i, k))
        b_spec = pl.BlockSpec((bn, bk), lambda i, j, k: (j + c0, k))
    else:
        a_spec = pl.BlockSpec((bk, bm), lambda i, j, k: (k, i))
        b_spec = pl.BlockSpec((bk, bn), lambda i, j, k: (k, j + c0))
    dn = {"nn": _NN, "nt": _NT, "tn": _TN}[mode]
    e_specs = [pl.BlockSpec(bs, (lambda i, j, k, f=f: f(i, j))) for (_, bs, f) in extras]
    ne, no = len(extras), len(out_dtypes)
    nd = 0 if dep is None else 1

    def body(a_ref, b_ref, *rest):
        e_refs, o_refs = rest[:ne], rest[ne + nd:ne + nd + no]
        i, j, k = pl.program_id(0), pl.program_id(1), pl.program_id(2)

        def finish(acc):
            outs = (acc,) if epilogue is None else epilogue(acc, i, j, *[e[...] for e in e_refs])
            for o_ref, o in zip(o_refs, outs):
                o_ref[...] = o.astype(o_ref.dtype)

        p = _dot(a_ref[...], b_ref[...], dn)
        if nk == 1:
            finish(p)
        else:
            acc_ref = rest[-1]

            @pl.when(k == 0)
            def _():
                acc_ref[...] = p

            @pl.when(k > 0)
            def _():
                acc_ref[...] += p

            @pl.when(k == nk - 1)
            def _():
                finish(acc_ref[...])

    outs = pl.pallas_call(
        body, name=name, grid=(M // bm, N // bn, nk),
        in_specs=[a_spec, b_spec] + e_specs + [pl.BlockSpec(memory_space=pl.ANY)] * nd,
        out_specs=[pl.BlockSpec((bm, bn), lambda i, j, k: (i, j)) for _ in out_dtypes],
        out_shape=[jax.ShapeDtypeStruct((M, N), dt) for dt in out_dtypes],
        scratch_shapes=[pltpu.VMEM((bm, bn), F32)] if nk > 1 else [],
        compiler_params=_cparams("parallel", "parallel", "arbitrary"),
    )(a, b, *[e[0] for e in extras], *([dep] if nd else []))
    return outs


def _rowwise(body, rows, vecs, outs, n_acc, *, R, L, tr, name, acc_width=None):
    assert R % tr == 0 and L % tr == 0, (name, R, L, tr)
    nl = L // tr
    n_regions = 2 if R > L else 1
    n_rows, n_vecs, n_outs = len(rows), len(vecs), len(outs)
    acc_pad = -(-n_acc // 8) * 8 if n_acc else 0

    in_specs = []
    for (_, w, cb, lat_only) in rows:
        if lat_only:
            in_specs.append(pl.BlockSpec((tr, w), lambda i, cb=cb: (jnp.minimum(i, nl - 1), cb)))
        else:
            in_specs.append(pl.BlockSpec((tr, w), lambda i, cb=cb: (i, cb)))
    for v in vecs:
        in_specs.append(pl.BlockSpec(v.shape, lambda i, nd=v.ndim: (0,) * nd))
    out_specs = [pl.BlockSpec((tr, w), lambda i: (i, 0)) for (w, _) in outs]
    out_shape = [jax.ShapeDtypeStruct((R, w), dt) for (w, dt) in outs]
    if n_acc:
        out_specs.append(pl.BlockSpec((None, acc_pad, acc_width), lambda i: (jnp.where(i >= nl, 1, 0), 0, 0)))
        out_shape.append(jax.ShapeDtypeStruct((n_regions, acc_pad, acc_width), F32))

    def kern(*refs):
        i = pl.program_id(0)
        is_ctx = i >= nl
        ins = [r[...] for r in refs[:n_rows + n_vecs]]
        o_refs = refs[n_rows + n_vecs:]
        out_tiles, acc_rows = body(is_ctx, *ins)
        for o_ref, o in zip(o_refs[:n_outs], out_tiles):
            o_ref[...] = o.astype(o_ref.dtype)
        if n_acc:
            acc_ref = o_refs[n_outs]

            @pl.when((i == 0) | (i == nl))
            def _():
                acc_ref[...] = jnp.zeros_like(acc_ref)

            for r, row in enumerate(acc_rows):
                acc_ref[r:r + 1, :] += row

    res = pl.pallas_call(
        kern, name=name, grid=(R // tr,), in_specs=in_specs, out_specs=out_specs, out_shape=out_shape,
        compiler_params=_cparams("arbitrary"),
    )(*[r[0] for r in rows], *vecs)
    return res


def _colsum(x):
    return jnp.sum(x, axis=0, keepdims=True)


def _rms_stats(x):
    r = lax.rsqrt(jnp.mean(x * x, axis=-1, keepdims=True) + NORM_EPS)
    return x * r, r


def _sel(is_ctx, pk, lat_row, ctx_row):
    return jnp.where(is_ctx, pk[ctx_row:ctx_row + 1, :], pk[lat_row:lat_row + 1, :])


def _normmod(x, g, pk, *, R, L, tr, name):
    D = x.shape[-1]

    def body(is_ctx, xt, gv, pkv):
        xh, _ = _rms_stats(xt)
        sh, sc = _sel(is_ctx, pkv, 0, 2), _sel(is_ctx, pkv, 1, 3)
        return ((xh * gv) * (1.0 + sc) + sh,), ()

    return _rowwise(body, [(x, D, 0, False)], [g, pk], [(D, BF16)], 0, R=R, L=L, tr=tr, name=name)[0]


def _normmod_bwd(x_in, da, dx_out, dx_out_lat_only, g, pk, prev, *, R, L, tr, name):
    D = x_in.shape[-1]
    has_prev = prev is not None

    def body(is_ctx, *t):
        if has_prev:
            xt, dat, dxo, mp, gv, pkv, gates = t
        else:
            xt, dat, dxo, gv, pkv = t
        xh, r = _rms_stats(xt)
        sc = _sel(is_ctx, pkv, 1, 3)
        if dx_out_lat_only:
            dxo = jnp.where(is_ctx, 0.0, dxo)
        dn = dat * (1.0 + sc)
        w = dn * gv
        dxi = dxo + r * (w - xh * jnp.mean(w * xh, axis=-1, keepdims=True))
        accs = [_colsum(dat), _colsum(dat * (xh * gv)), _colsum(dn * xh)]
        outs = [dxi]
        if has_prev:
            gate = _sel(is_ctx, gates, 0, 1)
            outs.append(dxi * gate)
            accs.append(_colsum(dxi * mp.astype(F32)))
        return outs, accs

    rows = [(x_in, D, 0, False), (da, D, 0, False), (dx_out, D, 0, dx_out_lat_only)]
    vecs = [g, pk]
    outs = [(D, F32)]
    if has_prev:
        rows.append((prev[0], D, 0, False))
        vecs.append(prev[1])
        outs.append((D, BF16))
    return _rowwise(body, rows, vecs, outs, 4 if has_prev else 3, R=R, L=L, tr=tr, name=name, acc_width=D)


def _loss_head(x4, target, m_prev, gf, gate, *, L, tr, name):
    D = x4.shape[-1]

    def body(is_ctx, xt, tg, mp, gfv, gatev):
        xh, r = _rms_stats(xt)
        e = xh * gfv - tg
        dy = e * (1.0 / D)
        w = dy * gfv
        dx = r * (w - xh * jnp.mean(w * xh, axis=-1, keepdims=True))
        accs = [_colsum(e * e) * (0.5 / D), _colsum(dy * xh), _colsum(dx * mp.astype(F32))]
        return (dx, dx * gatev), accs

    return _rowwise(body, [(x4, D, 0, False), (target, D, 0, False), (m_prev, D, 0, False)], [gf, gate],
                    [(D, F32), (D, BF16)], 3, R=L, L=L, tr=tr, name=name, acc_width=D)


def _decays(lgh, rev):
    ii = lax.broadcasted_iota(jnp.int32, (CHUNK, CHUNK), 0)
    jj = lax.broadcasted_iota(jnp.int32, (CHUNK, CHUNK), 1)
    ri = lax.broadcasted_iota(jnp.int32, (CHUNK, 1), 0).astype(F32)
    diff = (jj - ii if rev else ii - jj)
    amat = jnp.where(diff >= 0, jnp.exp(lgh * jnp.maximum(diff, 0).astype(F32)), 0.0)
    pos = (CHUNK - ri) if rev else (ri + 1.0)
    bq = jnp.exp(lgh * pos)
    bk = jnp.exp(lgh * (CHUNK - pos))
    return amat, bq, bk, pos


def _ret_specs(H, nT, nL, rev, backward):
    def chunk(s):
        s = (nT - 1 - s) if backward else s
        return (nT - 1 - s) if rev else (s + nL) % nT

    def step(s):
        return (nT - 1 - s) if backward else s

    return chunk, step


def _ret_fwd(qk, vg, lg, *, T, L, H, rev, name):
    dk, dv = 2 * LANES, 4 * LANES
    nT, nL = T // CHUNK, L // CHUNK
    chunk, step = _ret_specs(H, nT, nL, rev, False)

    def body(lg_ref, q_ref, k_ref, v_ref, o_ref, st_ref, s_scr):
        h, s = pl.program_id(0), pl.program_id(1)
        lgh = lg_ref[0, h]
        amat, bq, bk, _ = _decays(lgh, rev)

        @pl.when(s == 0)
        def _():
            s_scr[...] = jnp.zeros_like(s_scr)

        q, k, v = q_ref[...], k_ref[...], v_ref[...]
        st = s_scr[...]
        st_ref[...] = st
        scores = _dot(q, k, _NT) * amat
        o_ref[...] = _dot(scores.astype(BF16), v) + _dot(q, st.astype(BF16)) * bq
        kd = (k.astype(F32) * bk).astype(BF16)
        s_scr[...] = st * jnp.exp(lgh * CHUNK) + _dot(kd, v, _TN)

    return pl.pallas_call(
        body, name=name, grid=(H, nT),
        in_specs=[pl.BlockSpec(memory_space=pltpu.SMEM),
                  pl.BlockSpec((CHUNK, dk), lambda h, s: (chunk(s), h)),
                  pl.BlockSpec((CHUNK, dk), lambda h, s: (chunk(s), H + h)),
                  pl.BlockSpec((CHUNK, dv), lambda h, s: (chunk(s), h))],
        out_specs=[pl.BlockSpec((CHUNK, dv), lambda h, s: (chunk(s), h)),
                   pl.BlockSpec((None, None, dk, dv), lambda h, s: (h, s, 0, 0))],
        out_shape=[jax.ShapeDtypeStruct((T, H * dv), F32), jax.ShapeDtypeStruct((H, nT, dk, dv), F32)],
        scratch_shapes=[pltpu.VMEM((dk, dv), F32)],
        compiler_params=_cparams("parallel", "arbitrary"),
    )(lg, qk, qk, vg)


def _ret_bwd(qk, vg, do, states, lg, *, T, L, H, rev, name):
    dk, dv = 2 * LANES, 4 * LANES
    nT, nL = T // CHUNK, L // CHUNK
    chunk, step = _ret_specs(H, nT, nL, rev, True)

    def body(lg_ref, q_ref, k_ref, v_ref, do_ref, st_ref, dq_ref, dk_ref, dv_ref, dlg_ref, ds_scr):
        h, s = pl.program_id(0), pl.program_id(1)
        lgh = lg_ref[0, h]
        amat, bq, bk, pos = _decays(lgh, rev)

        @pl.when(s == 0)
        def _():
            ds_scr[...] = jnp.zeros_like(ds_scr)
            dlg_ref[...] = jnp.zeros_like(dlg_ref)

        q, k, v, dob = q_ref[...], k_ref[...], v_ref[...], do_ref[...]
        st = st_ref[...]
        stb = st.astype(BF16)
        ds_new = ds_scr[...]
        dsb = ds_new.astype(BF16)
        qf, kf = q.astype(F32), k.astype(F32)
        scores = (_dot(q, k, _NT) * amat).astype(BF16)
        dqk = (_dot(dob, v, _NT) * amat).astype(BF16)
        dq = _dot(dqk, k) + _dot(dob, stb, _NT) * bq
        dkk = _dot(dqk, q, _TN) + _dot(v, dsb, _NT) * bk
        kd = (kf * bk).astype(BF16)
        dvv = _dot(scores, dob, _TN) + _dot(kd, dsb)
        dod = (dob.astype(F32) * bq).astype(BF16)
        ds_prev = ds_new * jnp.exp(lgh * CHUNK) + _dot(q, dod, _TN)
        ds_scr[...] = ds_prev
        dq_ref[...] = dq
        dk_ref[...] = dkk
        dv_ref[...] = dvv
        dlg = (jnp.sum(pos * jnp.sum(qf * dq - kf * dkk, axis=-1, keepdims=True))
               + CHUNK * jnp.sum(ds_prev * st))
        dlg_ref[...] += dlg

    return pl.pallas_call(
        body, name=name, grid=(H, nT),
        in_specs=[pl.BlockSpec(memory_space=pltpu.SMEM),
                  pl.BlockSpec((CHUNK, dk), lambda h, s: (chunk(s), h)),
                  pl.BlockSpec((CHUNK, dk), lambda h, s: (chunk(s), H + h)),
                  pl.BlockSpec((CHUNK, dv), lambda h, s: (chunk(s), h)),
                  pl.BlockSpec((CHUNK, dv), lambda h, s: (chunk(s), h)),
                  pl.BlockSpec((None, None, dk, dv), lambda h, s: (h, step(s), 0, 0))],
        out_specs=[pl.BlockSpec((CHUNK, dk), lambda h, s: (chunk(s), h)),
                   pl.BlockSpec((CHUNK, dk), lambda h, s: (chunk(s), h)),
                   pl.BlockSpec((CHUNK, dv), lambda h, s: (chunk(s), h)),
                   pl.BlockSpec((None, 8, LANES), lambda h, s: (h, 0, 0))],
        out_shape=[jax.ShapeDtypeStruct((T, H * dk), F32), jax.ShapeDtypeStruct((T, H * dk), F32),
                   jax.ShapeDtypeStruct((T, H * dv), F32), jax.ShapeDtypeStruct((H, 8, LANES), F32)],
        scratch_shapes=[pltpu.VMEM((dk, dv), F32)],
        compiler_params=_cparams("parallel", "arbitrary"),
    )(lg, qk, qk, vg, do, states)


def _readout(o_f, o_b, vg, *, T, L, H, tr, name):
    dv = 4 * LANES
    W = H * dv

    def body(is_ctx, of, ob, g):
        o = of + ob
        parts = []
        for h in range(H):
            oh = o[:, h * dv:(h + 1) * dv]
            parts.append(oh * lax.rsqrt(jnp.mean(oh * oh, axis=-1, keepdims=True) + NORM_EPS))
        y = jnp.concatenate(parts, axis=1)
        return (_silu(g.astype(F32)) * y,), ()

    return _rowwise(body, [(o_f, W, 0, False), (o_b, W, 0, False), (vg, W, 1, False)], [], [(W, BF16)], 0,
                    R=T, L=L, tr=tr, name=name)[0]


def _readout_bwd(dz, o_f, o_b, vg, *, T, L, H, tr, name):
    dv = 4 * LANES
    W = H * dv

    def body(is_ctx, dzt, of, ob, g):
        o = of + ob
        gf = g.astype(F32)
        sg = jax.nn.sigmoid(gf)
        dzf = dzt.astype(F32)
        dy = dzf * (gf * sg)
        ys, dos = [], []
        for h in range(H):
            sl = slice(h * dv, (h + 1) * dv)
            oh, dyh = o[:, sl], dy[:, sl]
            r = lax.rsqrt(jnp.mean(oh * oh, axis=-1, keepdims=True) + NORM_EPS)
            yh = oh * r
            ys.append(yh)
            dos.append(r * (dyh - yh * jnp.mean(dyh * yh, axis=-1, keepdims=True)))
        y = jnp.concatenate(ys, axis=1)
        dg = dzf * y * (sg * (1.0 + gf * (1.0 - sg)))
        return (jnp.concatenate(dos, axis=1), dg), ()

    return _rowwise(body, [(dz, W, 0, False), (o_f, W, 0, False), (o_b, W, 0, False), (vg, W, 1, False)], [],
                    [(W, BF16), (W, BF16)], 0, R=T, L=L, tr=tr, name=name)


def _ret_assemble(dq_f, dq_b, dk_f, dk_b, dv_f, dv_b, dg, cos, sin, *, T, L, H, tr, name):
    dk, dv = 2 * LANES, 4 * LANES
    Wq, Wv = H * dk, H * dv
    kscale = float(dk) ** -0.5

    def unrope(d, c, s_, scale):
        parts = []
        for h in range(H):
            d1, d2 = d[:, h * dk:h * dk + LANES], d[:, h * dk + LANES:(h + 1) * dk]
            parts += [(d1 * c + d2 * s_) * scale, (d2 * c - d1 * s_) * scale]
        return jnp.concatenate(parts, axis=1)

    def body(is_ctx, qf, qb, kf, kb, vf, vb, g, c, s_):
        dq = unrope(qf + qb, c, s_, 1.0)
        dkk = unrope(kf + kb, c, s_, kscale)
        return (jnp.concatenate([dq.astype(BF16), dkk.astype(BF16), (vf + vb).astype(BF16), g], axis=1),), ()

    rows = [(dq_f, Wq, 0, False), (dq_b, Wq, 0, False), (dk_f, Wq, 0, False), (dk_b, Wq, 0, False),
            (dv_f, Wv, 0, False), (dv_b, Wv, 0, False), (dg, Wv, 0, False),
            (cos, LANES, 0, False), (sin, LANES, 0, False)]
    return _rowwise(body, rows, [], [(2 * Wq + 2 * Wv, BF16)], 0, R=T, L=L, tr=tr, name=name)[0]


def _swap32(x):
    n = x.shape[-1]
    lane = lax.broadcasted_iota(jnp.int32, x.shape, x.ndim - 1)
    return jnp.where(lane % 64 < 32, pltpu.roll(x, n - 32, x.ndim - 1), pltpu.roll(x, 32, x.ndim - 1))


def _band_masks(n, nb):
    ii = lax.broadcasted_iota(jnp.int32, (CHUNK, CHUNK), 0)
    jj = lax.broadcasted_iota(jnp.int32, (CHUNK, CHUNK), 1)
    return (jj >= ii) & (n > 0), (jj <= ii) & (n < nb - 1)


def _attn_kv_specs(L, CTX, nb):
    blk = lambda f: pl.BlockSpec((CHUNK, LANES), lambda h, n: (f(n), h))
    prev_, cur_, next_ = (lambda n: jnp.maximum(n - 1, 0)), (lambda n: n), (lambda n: jnp.minimum(n + 1, nb - 1))
    ctx_spec = pl.BlockSpec((CTX, LANES), lambda h, n: (L // CTX, h))
    return [blk(prev_), blk(cur_), blk(next_), ctx_spec]


def _attn_fwd(q, k, v, sink, *, L, CTX, Hkv, G, name):
    nb = L // CHUNK
    scale = float(ATT_HEAD_DIM) ** -0.5
    kvs = _attn_kv_specs(L, CTX, nb)

    def body(sink_ref, q_ref, kp, kc, kn, kx, vp, vc, vn, vx, o_ref, lse_ref):
        h, n = pl.program_id(0), pl.program_id(1)
        mp, mn = _band_masks(n, nb)
        for g in range(G):
            qg = q_ref[:, g * LANES:(g + 1) * LANES]
            sp = jnp.where(mp, _dot(qg, kp[...], _NT) * scale, NEG_INF)
            sc = _dot(qg, kc[...], _NT) * scale
            sn = jnp.where(mn, _dot(qg, kn[...], _NT) * scale, NEG_INF)
            sx = _dot(qg, kx[...], _NT) * scale
            sk = sink_ref[0, h * G + g]
            rmax = lambda t: jnp.max(t, axis=-1, keepdims=True)
            m = jnp.maximum(jnp.maximum(jnp.maximum(rmax(sp), rmax(sc)), jnp.maximum(rmax(sn), rmax(sx))), sk)
            pp, pc, pn, px = jnp.exp(sp - m), jnp.exp(sc - m), jnp.exp(sn - m), jnp.exp(sx - m)
            rsum = lambda t: jnp.sum(t, axis=-1, keepdims=True)
            den = rsum(pp) + rsum(pc) + rsum(pn) + rsum(px) + jnp.exp(sk - m)
            o = (_dot(pp.astype(BF16), vp[...]) + _dot(pc.astype(BF16), vc[...])
                 + _dot(pn.astype(BF16), vn[...]) + _dot(px.astype(BF16), vx[...]))
            o_ref[:, g * LANES:(g + 1) * LANES] = (o / den).astype(o_ref.dtype)
            lse_ref[:, g:g + 1] = m + jnp.log(den)

    return pl.pallas_call(
        body, name=name, grid=(Hkv, nb),
        in_specs=[pl.BlockSpec(memory_space=pltpu.SMEM),
                  pl.BlockSpec((CHUNK, G * LANES), lambda h, n: (n, h))] + kvs + kvs,
        out_specs=[pl.BlockSpec((CHUNK, G * LANES), lambda h, n: (n, h)),
                   pl.BlockSpec((None, CHUNK, G), lambda h, n: (h, n, 0))],
        out_shape=[jax.ShapeDtypeStruct((L, Hkv * G * LANES), BF16), jax.ShapeDtypeStruct((Hkv, L, G), F32)],
        compiler_params=_cparams("parallel", "parallel"),
    )(sink, q, k, k, k, k, v, v, v, v)


def _attn_bwd_q(q, k, v, do, lse, sink, *, L, CTX, Hkv, G, name):
    nb = L // CHUNK
    scale = float(ATT_HEAD_DIM) ** -0.5
    kvs = _attn_kv_specs(L, CTX, nb)
    qspec = pl.BlockSpec((CHUNK, G * LANES), lambda h, n: (n, h))
    rowspec = pl.BlockSpec((None, CHUNK, G), lambda h, n: (h, n, 0))

    def body(sink_ref, q_ref, do_ref, lse_ref, kp, kc, kn, kx, vp, vc, vn, vx,
             dq_ref, dl_ref, dkx_ref, dvx_ref, dsk_ref):
        h, n = pl.program_id(0), pl.program_id(1)
        mp, mn = _band_masks(n, nb)

        @pl.when(n == 0)
        def _():
            dkx_ref[...] = jnp.zeros_like(dkx_ref)
            dvx_ref[...] = jnp.zeros_like(dvx_ref)
            dsk_ref[...] = jnp.zeros_like(dsk_ref)

        for g in range(G):
            sl = slice(g * LANES, (g + 1) * LANES)
            qg, dog = q_ref[:, sl], do_ref[:, sl]
            lse_g = lse_ref[:, g:g + 1]
            sk = sink_ref[0, h * G + g]
            ks, vs, masks = (kp, kc, kn, kx), (vp, vc, vn, vx), (mp, None, mn, None)
            ps, dps = [], []
            for kr, vr, msk in zip(ks, vs, masks):
                s_ = _dot(qg, kr[...], _NT) * scale
                p = jnp.exp(s_ - lse_g)
                if msk is not None:
                    p = jnp.where(msk, p, 0.0)
                ps.append(p)
                dps.append(_dot(dog, vr[...], _NT))
            delta = sum(jnp.sum(p * dp, axis=-1, keepdims=True) for p, dp in zip(ps, dps))
            dq = jnp.zeros((CHUNK, LANES), F32)
            for idx, (p, dp, kr) in enumerate(zip(ps, dps, ks)):
                ds = (p * (dp - delta) * scale).astype(BF16)
                dq = dq + _dot(ds, kr[...])
                if idx == 3:
                    dkx_ref[...] += _dot(ds, qg, _TN)
                    dvx_ref[...] += _dot(p.astype(BF16), dog, _TN)
            dq_ref[:, sl] = dq
            dl_ref[:, g:g + 1] = delta
            dsk_ref[g:g + 1, :] += -jnp.sum(jnp.exp(sk - lse_g) * delta)

    return pl.pallas_call(
        body, name=name, grid=(Hkv, nb),
        in_specs=[pl.BlockSpec(memory_space=pltpu.SMEM), qspec, qspec, rowspec] + kvs + kvs,
        out_specs=[qspec, rowspec,
                   pl.BlockSpec((CTX, LANES), lambda h, n: (0, h)), pl.BlockSpec((CTX, LANES), lambda h, n: (0, h)),
                   pl.BlockSpec((None, 8, LANES), lambda h, n: (h, 0, 0))],
        out_shape=[jax.ShapeDtypeStruct((L, Hkv * G * LANES), F32), jax.ShapeDtypeStruct((Hkv, L, G), F32),
                   jax.ShapeDtypeStruct((CTX, Hkv * LANES), F32), jax.ShapeDtypeStruct((CTX, Hkv * LANES), F32),
                   jax.ShapeDtypeStruct((Hkv, 8, LANES), F32)],
        compiler_params=_cparams("parallel", "arbitrary"),
    )(sink, q, do, lse, k, k, k, k, v, v, v, v)


def _attn_bwd_kv(q, k, v, do, lse, delta, *, L, Hkv, G, name):
    nb = L // CHUNK
    scale = float(ATT_HEAD_DIM) ** -0.5
    fs = [(lambda n: jnp.maximum(n - 1, 0)), (lambda n: n), (lambda n: jnp.minimum(n + 1, nb - 1))]
    qspecs = [pl.BlockSpec((CHUNK, G * LANES), lambda h, n, f=f: (f(n), h)) for f in fs]
    rspecs = [pl.BlockSpec((None, CHUNK, G), lambda h, n, f=f: (h, f(n), 0)) for f in fs]
    kspec = pl.BlockSpec((CHUNK, LANES), lambda h, n: (n, h))

    def body(k_ref, v_ref, q0, q1, q2, d0, d1, d2, l0, l1, l2, e0, e1, e2, dk_ref, dv_ref):
        n = pl.program_id(1)
        ii = lax.broadcasted_iota(jnp.int32, (CHUNK, CHUNK), 0)
        jj = lax.broadcasted_iota(jnp.int32, (CHUNK, CHUNK), 1)
        masks = ((jj <= ii) & (n > 0), None, (jj >= ii) & (n < nb - 1))
        kb, vb = k_ref[...], v_ref[...]
        dk = jnp.zeros((CHUNK, LANES), F32)
        dv = jnp.zeros((CHUNK, LANES), F32)
        for qr, dr, lr, er, msk in zip((q0, q1, q2), (d0, d1, d2), (l0, l1, l2), (e0, e1, e2), masks):
            for g in range(G):
                sl = slice(g * LANES, (g + 1) * LANES)
                qg, dog = qr[:, sl], dr[:, sl]
                p = jnp.exp(_dot(qg, kb, _NT) * scale - lr[:, g:g + 1])
                if msk is not None:
                    p = jnp.where(msk, p, 0.0)
                ds = (p * (_dot(dog, vb, _NT) - er[:, g:g + 1]) * scale).astype(BF16)
                dk = dk + _dot(ds, qg, _TN)
                dv = dv + _dot(p.astype(BF16), dog, _TN)
        dk_ref[...] = dk
        dv_ref[...] = dv

    return pl.pallas_call(
        body, name=name, grid=(Hkv, nb),
        in_specs=[kspec, kspec] + qspecs + qspecs + rspecs + rspecs,
        out_specs=[kspec, kspec],
        out_shape=[jax.ShapeDtypeStruct((L, Hkv * LANES), F32), jax.ShapeDtypeStruct((L, Hkv * LANES), F32)],
        compiler_params=_cparams("parallel", "parallel"),
    )(k, v, q, q, q, do, do, do, lse, lse, lse, delta, delta, delta)


def _attn_assemble(dq, dk_lat, dv_lat, dk_ctx, dv_ctx, cos, sin, *, T, L, CTX, Hq, Hkv, tr, name):
    Wq, Wk = Hq * LANES, Hkv * LANES
    ctx_blocks = CTX // tr
    nl = L // tr

    def unrope(d, c, s_, heads):
        return d * jnp.tile(c, (1, heads)) + _swap32(d * jnp.tile(s_, (1, heads)))

    def body(is_ctx, dqt, dkl, dvl, dkc, dvc, c, s_):
        dq_ = jnp.where(is_ctx, 0.0, unrope(dqt, c, s_, Hq))
        dk_ = unrope(jnp.where(is_ctx, dkc, dkl), c, s_, Hkv)
        dv_ = jnp.where(is_ctx, dvc, dvl)
        return (jnp.concatenate([dq_, dk_, dv_], axis=1),), ()

    def ctx_map(i):
        return (jnp.clip(i - nl, 0, ctx_blocks - 1), 0)

    assert T % tr == 0 and L % tr == 0 and CTX % tr == 0
    in_specs = [pl.BlockSpec((tr, Wq), lambda i: (jnp.minimum(i, nl - 1), 0)),
                pl.BlockSpec((tr, Wk), lambda i: (jnp.minimum(i, nl - 1), 0)),
                pl.BlockSpec((tr, Wk), lambda i: (jnp.minimum(i, nl - 1), 0)),
                pl.BlockSpec((tr, Wk), ctx_map), pl.BlockSpec((tr, Wk), ctx_map),
                pl.BlockSpec((tr, LANES), lambda i: (i, 0)), pl.BlockSpec((tr, LANES), lambda i: (i, 0))]

    def kern(dq_r, dkl_r, dvl_r, dkc_r, dvc_r, c_r, s_r, o_ref):
        is_ctx = pl.program_id(0) >= nl
        (out,), _ = body(is_ctx, dq_r[...], dkl_r[...], dvl_r[...], dkc_r[...], dvc_r[...], c_r[...], s_r[...])
        o_ref[...] = out.astype(o_ref.dtype)

    return pl.pallas_call(
        kern, name=name, grid=(T // tr,), in_specs=in_specs,
        out_specs=pl.BlockSpec((tr, Wq + 2 * Wk), lambda i: (i, 0)),
        out_shape=jax.ShapeDtypeStruct((T, Wq + 2 * Wk), BF16),
        compiler_params=_cparams("parallel"),
    )(dq, dk_lat, dv_lat, dk_ctx, dv_ctx, cos, sin)


def _my_place():
    x, y, c = lax.axis_index("x"), lax.axis_index("y"), lax.axis_index("c")
    return x, y, c


def _all_gather_small(v, *, name):
    R, C = v.shape

    def body(x_ref, out_ref, send_sems, recv_sems, local_sem):
        x, y, c = _my_place()
        me, sibling = (x, y, c), (x, y, 1 - c)
        chips = [(1 - x, y), (x, 1 - y), (1 - x, 1 - y)]

        def slot(px, py, pc):
            return out_ref.at[4 * px + 2 * py + pc]

        def copy(k, block, to, src=None):
            return pltpu.make_async_remote_copy(
                src_ref=slot(*block) if src is None else src, dst_ref=slot(*block),
                send_sem=send_sems.at[k], recv_sem=recv_sems.at[k], device_id=to, device_id_type=MESH)

        mine = pltpu.make_async_copy(x_ref, slot(*me), local_sem)
        mine.start()
        first = [copy(0, me, sibling, src=x_ref)]
        first += [copy(1 + j, me, (*chip, c), src=x_ref) for j, chip in enumerate(chips)]
        for cp in first:
            cp.start()
        passed = [copy(4 + j, (*chip, c), sibling) for j, chip in enumerate(chips)]
        for j, chip in enumerate(chips):
            copy(1 + j, (*chip, c), me).wait_recv()
            passed[j].start()
        copy(0, sibling, me).wait_recv()
        for j, chip in enumerate(chips):
            copy(4 + j, (*chip, 1 - c), me).wait_recv()
        for cp in first + passed:
            cp.wait_send()
        mine.wait()

    return pl.pallas_call(
        body, name=name, out_shape=jax.ShapeDtypeStruct((N_DEV, R, C), v.dtype),
        in_specs=[pl.BlockSpec(memory_space=pltpu.VMEM)], out_specs=pl.BlockSpec(memory_space=pltpu.VMEM),
        scratch_shapes=[pltpu.SemaphoreType.DMA((7,)), pltpu.SemaphoreType.DMA((7,)), pltpu.SemaphoreType.DMA],
    )(v)


def _shard_slice(ref, axis, idx, size):
    if axis == 1:
        return ref.at[:, pl.ds(idx * size, size), :]
    return ref.at[:, :, pl.ds(idx * size, size)]


def _all_gather_weights(shards, axes, *, name):
    nt = len(shards)
    sizes = [s.shape[a] for s, a in zip(shards, axes)]
    out_shape = []
    for s, a in zip(shards, axes):
        shp = list(s.shape)
        shp[a] *= N_DEV
        out_shape.append(jax.ShapeDtypeStruct(tuple(shp), s.dtype))

    def body(*refs):
        ins, outs = refs[:nt], refs[nt:2 * nt]
        send_sems, recv_sems, local_sems = refs[2 * nt:]
        x, y, c = _my_place()
        me, sibling = (x, y, c), (x, y, 1 - c)
        chips = [(1 - x, y), (x, 1 - y), (1 - x, 1 - y)]
        all_sends = []
        locals_ = []
        for t in range(nt):
            def slot(px, py, pc, t=t):
                return _shard_slice(outs[t], axes[t], 4 * px + 2 * py + pc, sizes[t])

            def copy(k, block, to, src=None, t=t, slot=slot):
                return pltpu.make_async_remote_copy(
                    src_ref=slot(*block) if src is None else src, dst_ref=slot(*block),
                    send_sem=send_sems.at[t, k], recv_sem=recv_sems.at[t, k], device_id=to, device_id_type=MESH)

            mine = pltpu.make_async_copy(ins[t], slot(*me), local_sems.at[t])
            mine.start()
            locals_.append(mine)
            first = [copy(0, me, sibling, src=ins[t])]
            first += [copy(1 + j, me, (*chip, c), src=ins[t]) for j, chip in enumerate(chips)]
            for cp in first:
                cp.start()
            all_sends += first
        for t in range(nt):
            def slot(px, py, pc, t=t):
                return _shard_slice(outs[t], axes[t], 4 * px + 2 * py + pc, sizes[t])

            def copy(k, block, to, t=t, slot=slot):
                return pltpu.make_async_remote_copy(
                    src_ref=slot(*block), dst_ref=slot(*block),
                    send_sem=send_sems.at[t, k], recv_sem=recv_sems.at[t, k], device_id=to, device_id_type=MESH)

            passed = [copy(4 + j, (*chip, c), sibling) for j, chip in enumerate(chips)]
            for j, chip in enumerate(chips):
                copy(1 + j, (*chip, c), me).wait_recv()
                passed[j].start()
            all_sends += passed
        for t in range(nt):
            def slot(px, py, pc, t=t):
                return _shard_slice(outs[t], axes[t], 4 * px + 2 * py + pc, sizes[t])

            def copy(k, block, to, t=t, slot=slot):
                return pltpu.make_async_remote_copy(
                    src_ref=slot(*block), dst_ref=slot(*block),
                    send_sem=send_sems.at[t, k], recv_sem=recv_sems.at[t, k], device_id=to, device_id_type=MESH)

            copy(0, sibling, me).wait_recv()
            for j, chip in enumerate(chips):
                copy(4 + j, (*chip, 1 - c), me).wait_recv()
        for cp in all_sends:
            cp.wait_send()
        for mine in locals_:
            mine.wait()

    return pl.pallas_call(
        body, name=name, out_shape=out_shape,
        in_specs=[pl.BlockSpec(memory_space=pl.ANY)] * nt, out_specs=[pl.BlockSpec(memory_space=pl.ANY)] * nt,
        scratch_shapes=[pltpu.SemaphoreType.DMA((nt, 7)), pltpu.SemaphoreType.DMA((nt, 7)),
                        pltpu.SemaphoreType.DMA((nt,))],
    )(*shards)


def _exchange_grads(grads, axes, *, name):
    nt = len(grads)
    sizes = [g.shape[a] // N_DEV for g, a in zip(grads, axes)]
    out_shape = []
    for g, a, sz in zip(grads, axes, sizes):
        shp = list(g.shape)
        shp[a] = sz
        out_shape.append(jax.ShapeDtypeStruct((N_DEV, *shp), g.dtype))

    def body(*refs):
        ins, outs = refs[:nt], refs[nt:2 * nt]
        send_sems, recv_sems, local_sems = refs[2 * nt:]
        x, y, c = _my_place()
        my_idx = 4 * x + 2 * y + c

        def peer(r):
            px = (1 - x) if r & 4 else x
            py = (1 - y) if r & 2 else y
            pc = (1 - c) if r & 1 else c
            return (px, py, pc)

        copies, locals_ = [], []
        for t in range(nt):
            mine = pltpu.make_async_copy(_shard_slice(ins[t], axes[t], my_idx, sizes[t]), outs[t].at[my_idx],
                                         local_sems.at[t])
            mine.start()
            locals_.append(mine)
            for r in range(1, N_DEV):
                p = peer(r)
                p_idx = 4 * p[0] + 2 * p[1] + p[2]
                cp = pltpu.make_async_remote_copy(
                    src_ref=_shard_slice(ins[t], axes[t], p_idx, sizes[t]), dst_ref=outs[t].at[my_idx],
                    send_sem=send_sems.at[t, r - 1], recv_sem=recv_sems.at[t, r - 1], device_id=p, device_id_type=MESH)
                cp.start()
                copies.append((cp, t, r, p_idx))
        for cp, t, r, p_idx in copies:
            pltpu.make_async_remote_copy(
                src_ref=_shard_slice(ins[t], axes[t], p_idx, sizes[t]), dst_ref=outs[t].at[p_idx],
                send_sem=send_sems.at[t, r - 1], recv_sem=recv_sems.at[t, r - 1], device_id=peer(r),
                device_id_type=MESH).wait_recv()
        for cp, _, _, _ in copies:
            cp.wait_send()
        for mine in locals_:
            mine.wait()

    return pl.pallas_call(
        body, name=name, out_shape=out_shape,
        in_specs=[pl.BlockSpec(memory_space=pl.ANY)] * nt, out_specs=[pl.BlockSpec(memory_space=pl.ANY)] * nt,
        scratch_shapes=[pltpu.SemaphoreType.DMA((nt, 7)), pltpu.SemaphoreType.DMA((nt, 7)),
                        pltpu.SemaphoreType.DMA((nt,))],
    )(*grads)


_HBM_SPEC = pl.BlockSpec(memory_space=pltpu.HBM)
_SEM_SPEC = pl.BlockSpec(memory_space=pltpu.SEMAPHORE)
_ANY_SPEC = pl.BlockSpec(memory_space=pl.ANY)
_DATAFLOW = pltpu.SideEffectType.DATAFLOW_SIDE_EFFECTING
N_PEERS = N_DEV - 1


def _peer(r):
    x, y, c = _my_place()
    return ((1 - x) if r & 4 else x, (1 - y) if r & 2 else y, (1 - c) if r & 1 else c)


def _index_of(place):
    return 4 * place[0] + 2 * place[1] + place[2]


def _slot(ref, axis, idx, size):
    if axis == 0:
        return ref.at[pl.ds(idx * size, size), :]
    return ref.at[:, pl.ds(idx * size, size)]


def _cast_place(w3, layer, axis, me_arr, *, name):
    Ks, Ns = w3.shape[1], w3.shape[2]
    tr = _pick(Ks, 256, 16)
    nblk = Ks // tr
    full = (Ks * N_DEV, Ns) if axis == 0 else (Ks, Ns * N_DEV)
    if axis == 0:
        out_map = lambda i, me: (me[0] * nblk + i, 0)
    else:
        out_map = lambda i, me: (i, me[0])

    def body(me_ref, w_ref, o_ref):
        o_ref[...] = w_ref[...].astype(BF16)

    return pl.pallas_call(
        body, name=name, out_shape=jax.ShapeDtypeStruct(full, BF16),
        grid_spec=pltpu.PrefetchScalarGridSpec(
            num_scalar_prefetch=1, grid=(nblk,),
            in_specs=[pl.BlockSpec((None, tr, Ns), lambda i, me: (layer, i, 0))],
            out_specs=pl.BlockSpec((tr, Ns), out_map)),
        compiler_params=_cparams("parallel"),
    )(me_arr, w3)


def _gather_start(lands, axes, after, *, name):
    nt = len(lands)
    sizes = [l.shape[a] // N_DEV for l, a in zip(lands, axes)]

    def body(*refs):
        ins, send_sems, recv_sems, token = refs[:nt], refs[nt + 1], refs[nt + 2], refs[-1]
        my_idx = _index_of(_my_place())
        for t in range(nt):
            mine = _slot(ins[t], axes[t], my_idx, sizes[t])
            for r in range(1, N_DEV):
                k = t * N_PEERS + r - 1
                pltpu.make_async_remote_copy(src_ref=mine, dst_ref=mine, send_sem=send_sems.at[k],
                                             recv_sem=recv_sems.at[k], device_id=_peer(r), device_id_type=MESH).start()
        token[...] = jnp.zeros_like(token)

    res = pl.pallas_call(
        body, name=name,
        out_shape=(pltpu.SemaphoreType.DMA((nt * N_PEERS,)), pltpu.SemaphoreType.DMA((nt * N_PEERS,)),
                   *[pltpu.HBM(l.shape, l.dtype) for l in lands], jax.ShapeDtypeStruct((8, LANES), F32)),
        in_specs=[_HBM_SPEC] * nt + [_ANY_SPEC],
        out_specs=(_SEM_SPEC, _SEM_SPEC, *[_HBM_SPEC] * nt, pl.BlockSpec(memory_space=pltpu.VMEM)),
        input_output_aliases={t: 2 + t for t in range(nt)},
        compiler_params=pltpu.CompilerParams(has_side_effects=_DATAFLOW),
    )(*[pltpu.with_memory_space_constraint(l, pltpu.HBM) for l in lands], after)
    return res[0], res[1], list(res[2:2 + nt]), res[-1]


def _gather_wait(send_sems, recv_sems, lands, axes, after, *, name):
    nt = len(lands)
    sizes = [l.shape[a] // N_DEV for l, a in zip(lands, axes)]

    def body(*refs):
        ins, send_sems, recv_sems = refs[:nt], refs[nt], refs[nt + 1]
        my_idx = _index_of(_my_place())
        for t in range(nt):
            for r in range(1, N_DEV):
                p = _peer(r)
                k = t * N_PEERS + r - 1
                cp = pltpu.make_async_remote_copy(
                    src_ref=_slot(ins[t], axes[t], my_idx, sizes[t]), dst_ref=_slot(ins[t], axes[t], _index_of(p), sizes[t]),
                    send_sem=send_sems.at[k], recv_sem=recv_sems.at[k], device_id=p, device_id_type=MESH)
                cp.wait_send()
                cp.wait_recv()

    res = pl.pallas_call(
        body, name=name, out_shape=[pltpu.HBM(l.shape, l.dtype) for l in lands],
        in_specs=[_HBM_SPEC] * nt + [_SEM_SPEC, _SEM_SPEC, _ANY_SPEC], out_specs=[_HBM_SPEC] * nt,
        input_output_aliases={t: t for t in range(nt)},
        compiler_params=pltpu.CompilerParams(has_side_effects=_DATAFLOW),
    )(*lands, send_sems, recv_sems, after)
    return list(res)


def _scatter_start(dw, axis, *, name):
    size = dw.shape[axis] // N_DEV
    land_shape = (N_PEERS, size, dw.shape[1]) if axis == 0 else (N_PEERS, dw.shape[0], size)

    def body(dw_ref, land_ref, send_sems, recv_sems, dw_thru, land_thru, token):
        for r in range(1, N_DEV):
            p = _peer(r)
            pltpu.make_async_remote_copy(src_ref=_slot(dw_ref, axis, _index_of(p), size), dst_ref=land_ref.at[r - 1],
                                         send_sem=send_sems.at[r - 1], recv_sem=recv_sems.at[r - 1], device_id=p,
                                         device_id_type=MESH).start()
        token[...] = jnp.zeros_like(token)

    land = pltpu.with_memory_space_constraint(lax.empty(land_shape, dw.dtype), pltpu.HBM)
    return pl.pallas_call(
        body, name=name,
        out_shape=(pltpu.SemaphoreType.DMA((N_PEERS,)), pltpu.SemaphoreType.DMA((N_PEERS,)),
                   pltpu.HBM(dw.shape, dw.dtype), pltpu.HBM(land_shape, dw.dtype), jax.ShapeDtypeStruct((8, LANES), F32)),
        in_specs=[_HBM_SPEC, _HBM_SPEC],
        out_specs=(_SEM_SPEC, _SEM_SPEC, _HBM_SPEC, _HBM_SPEC, pl.BlockSpec(memory_space=pltpu.VMEM)),
        input_output_aliases={0: 2, 1: 3},
        compiler_params=pltpu.CompilerParams(has_side_effects=_DATAFLOW),
    )(pltpu.with_memory_space_constraint(dw, pltpu.HBM), land)


def _scatter_wait(send_sems, recv_sems, dw, land, axis, after, *, name):
    size = dw.shape[axis] // N_DEV

    def body(dw_ref, land_ref, send_sems, recv_sems, after_ref, dw_thru, land_thru):
        for r in range(1, N_DEV):
            p = _peer(r)
            cp = pltpu.make_async_remote_copy(src_ref=_slot(dw_ref, axis, _index_of(p), size), dst_ref=land_ref.at[r - 1],
                                              send_sem=send_sems.at[r - 1], recv_sem=recv_sems.at[r - 1], device_id=p,
                                              device_id_type=MESH)
            cp.wait_send()
            cp.wait_recv()

    return pl.pallas_call(
        body, name=name, out_shape=(pltpu.HBM(dw.shape, dw.dtype), pltpu.HBM(land.shape, land.dtype)),
        in_specs=[_HBM_SPEC, _HBM_SPEC, _SEM_SPEC, _SEM_SPEC, _ANY_SPEC], out_specs=(_HBM_SPEC, _HBM_SPEC),
        input_output_aliases={0: 0, 1: 1},
        compiler_params=pltpu.CompilerParams(has_side_effects=_DATAFLOW),
    )(dw, land, send_sems, recv_sems, after)


def _adamw_math(w, g, m, v):
    m = ADAM_B1 * m + (1.0 - ADAM_B1) * g
    v = ADAM_B2 * v + (1.0 - ADAM_B2) * (g * g)
    m_hat = m / (1.0 - ADAM_B1 ** ADAM_STEP)
    v_hat = v / (1.0 - ADAM_B2 ** ADAM_STEP)
    delta = -ADAM_LR * (m_hat / (jnp.sqrt(v_hat) + ADAM_EPS) + ADAM_WD * w)
    return delta, m, v


def _adamw_sharded(w, m, v, layer, dw, land, axis, me_arr, prev, *, name):
    nl, Ks, Ns = w.shape
    tr = _pick(Ks, 128, 16)
    nblk = Ks // tr
    if axis == 0:
        own_map = lambda i, me: (me[0] * nblk + i, 0)
    else:
        own_map = lambda i, me: (i, me[0])
    wspec = pl.BlockSpec((None, tr, Ns), lambda i, me: (layer, i, 0))
    n_prev = 0 if prev is None else 4

    def body(me_ref, w_ref, m_ref, v_ref, own_ref, r_ref, *rest):
        g_ref, d_ref, nm_ref, nv_ref = rest[n_prev:]
        g = own_ref[...].astype(F32)
        for r in range(N_PEERS):
            g = g + r_ref[r].astype(F32)
        delta, nm, nv = _adamw_math(w_ref[...], g, m_ref[...], v_ref[...])
        g_ref[...], d_ref[...], nm_ref[...], nv_ref[...] = g, delta, nm, nv

    return pl.pallas_call(
        body, name=name, out_shape=[jax.ShapeDtypeStruct((nl, Ks, Ns), F32)] * 4,
        grid_spec=pltpu.PrefetchScalarGridSpec(
            num_scalar_prefetch=1, grid=(nblk,),
            in_specs=[wspec, wspec, wspec, pl.BlockSpec((tr, Ns), own_map),
                      pl.BlockSpec((N_PEERS, tr, Ns), lambda i, me: (0, i, 0))] + [_ANY_SPEC] * n_prev,
            out_specs=[wspec] * 4),
        input_output_aliases={6 + k: k for k in range(n_prev)},
        compiler_params=_cparams("parallel"),
    )(me_arr, w, m, v, dw, land, *(prev or []))


def _adamw_flat(w, g, m, v, *, name):
    def body(w_ref, g_ref, m_ref, v_ref, d_ref, nm_ref, nv_ref):
        d_ref[...], nm_ref[...], nv_ref[...] = _adamw_math(w_ref[...], g_ref[...], m_ref[...], v_ref[...])

    spec = pl.BlockSpec(memory_space=pltpu.VMEM)
    return pl.pallas_call(body, name=name, in_specs=[spec] * 4, out_specs=[spec] * 3,
                          out_shape=[jax.ShapeDtypeStruct(w.shape, F32)] * 3)(w, g, m, v)


def _sum_devices(a, *, name):
    def body(a_ref, o_ref):
        s = a_ref[0]
        for d in range(1, N_DEV):
            s = s + a_ref[d]
        o_ref[...] = s

    spec = pl.BlockSpec(memory_space=pltpu.VMEM)
    return pl.pallas_call(body, name=name, in_specs=[spec], out_specs=spec,
                          out_shape=jax.ShapeDtypeStruct(a.shape[1:], F32))(a)


def _ada_mods(c16, ada_w, ada_b_cols, *, name):
    nl, D, cols = ada_w.shape
    bn = _pick(cols, 512)

    def body(c_ref, w_ref, b_ref, o_ref):
        cond = _silu(c_ref[...]).astype(BF16)
        o_ref[...] = _dot(cond, w_ref[...].astype(BF16)) + b_ref[...]

    return pl.pallas_call(
        body, name=name, grid=(nl, cols // bn),
        in_specs=[pl.BlockSpec((16, D), lambda l, j: (0, 0)), pl.BlockSpec((None, D, bn), lambda l, j: (l, 0, j)),
                  pl.BlockSpec((None, 1, bn), lambda l, j: (l, 0, j))],
        out_specs=pl.BlockSpec((None, 16, bn), lambda l, j: (l, 0, j)),
        out_shape=jax.ShapeDtypeStruct((nl, 16, cols), F32),
        compiler_params=_cparams("parallel", "parallel"),
    )(c16, ada_w, ada_b_cols)


def _ada_bwd(cond_t, dmod, w, m, v, *, name):
    nl, D, cols = w.shape
    tr = _pick(D, 256, 8)

    def body(ct_ref, dm_ref, w_ref, m_ref, v_ref, g_ref, d_ref, nm_ref, nv_ref, dc_ref):
        ct, dm, wt = ct_ref[...], dm_ref[...], w_ref[...]
        g = ct[:, 0:1] * dm[0:1, :]
        for r in range(1, N_DEV + 1):
            g = g + ct[:, r:r + 1] * dm[r:r + 1, :]
        delta, nm, nv = _adamw_math(wt, g, m_ref[...], v_ref[...])
        g_ref[...], d_ref[...], nm_ref[...], nv_ref[...] = g, delta, nm, nv
        dc_ref[...] = jnp.sum(wt * dm[N_DEV:N_DEV + 1, :], axis=-1, keepdims=True)

    wspec = pl.BlockSpec((None, tr, cols), lambda l, i: (l, i, 0))
    return pl.pallas_call(
        body, name=name, grid=(nl, D // tr),
        in_specs=[pl.BlockSpec((tr, 16), lambda l, i: (i, 0)), pl.BlockSpec((None, 16, cols), lambda l, i: (l, 0, 0)),
                  wspec, wspec, wspec],
        out_specs=[wspec] * 4 + [pl.BlockSpec((None, tr, 1), lambda l, i: (l, i, 0))],
        out_shape=[jax.ShapeDtypeStruct((nl, D, cols), F32)] * 4 + [jax.ShapeDtypeStruct((nl, D, 1), F32)],
        compiler_params=_cparams("parallel", "parallel"),
    )(cond_t, dmod, w, m, v)


def _rope_tables(L, CTX):
    def angles(pos, dim):
        inv_freq = ROPE_BASE ** (-jnp.arange(0, dim, 2, dtype=F32) / dim)
        return pos.astype(F32)[:, None] * inv_freq[None, :]

    def pad(cos, sin):
        return (jnp.concatenate([cos, jnp.ones((CTX, LANES), F32)], 0),
                jnp.concatenate([sin, jnp.zeros((CTX, LANES), F32)], 0))

    ret = angles(jnp.arange(L), 2 * LANES)
    ret_cs = pad(jnp.cos(ret), jnp.sin(ret))
    rows = angles(jnp.arange(L) // GRID_W, ATT_HEAD_DIM // 2)
    cols = angles(jnp.arange(L) % GRID_W, ATT_HEAD_DIM // 2)
    cos = jnp.concatenate([jnp.cos(rows)] * 2 + [jnp.cos(cols)] * 2, axis=1)
    sin = jnp.concatenate([-jnp.sin(rows), jnp.sin(rows), -jnp.sin(cols), jnp.sin(cols)], axis=1)
    return ret_cs, pad(cos, sin)


def kernel(x, c, ctx, c_ctx, ada_w, ada_b, norm_mix_g, norm_mlp_g, mlp_w1, mlp_w2, ret_w_in, ret_w_out, ret_decay_fwd, ret_decay_bwd, attn_w_in, attn_w_out, attn_sink, final_norm_g, loss_target, m_c_ctx, m_ada_w, m_ada_b, m_norm_mix_g, m_norm_mlp_g, m_mlp_w1, m_mlp_w2, m_ret_w_in, m_ret_w_out, m_ret_decay_fwd, m_ret_decay_bwd, m_attn_w_in, m_attn_w_out, m_attn_sink, m_final_norm_g, v_c_ctx, v_ada_w, v_ada_b, v_norm_mix_g, v_norm_mlp_g, v_mlp_w1, v_mlp_w2, v_ret_w_in, v_ret_w_out, v_ret_decay_fwd, v_ret_decay_bwd, v_attn_w_in, v_attn_w_out, v_attn_sink, v_final_norm_g):
    L, D = x.shape[1], x.shape[2]
    CTX = ctx.shape[1]
    T = L + CTX
    RH = ret_decay_fwd.shape[-1]
    assert D == RH * 2 * LANES and ada_w.shape[0] == 2 and ret_w_in.shape[0] == 1 and attn_w_in.shape[0] == 1
    Hq = attn_sink.shape[-1]
    Hkv = (attn_w_in.shape[-1] * N_DEV // ATT_HEAD_DIM - Hq) // 2
    G = Hq // Hkv
    FF = mlp_w1.shape[-1] * N_DEV
    Wq_r, Wv_r = RH * 2 * LANES, RH * 4 * LANES
    acols = ada_w.shape[-1]
    tr = _pick(CTX, 256, 8)
    tr_wide = _pick(CTX, 128, 8)
    bmT = T // 4 if (T % 64 == 0) else T
    bmL = L // 4 if (L % 64 == 0) else L
    x_idx, y_idx, c_idx = lax.axis_index("x"), lax.axis_index("y"), lax.axis_index("c")
    me = 4 * x_idx + 2 * y_idx + c_idx
    me_arr = jnp.reshape(me, (1,)).astype(jnp.int32)

    (rcos, rsin), (acos, asin) = _rope_tables(L, CTX)
    lg_f = jax.nn.log_sigmoid(ret_decay_fwd.astype(F32))
    lg_b = jax.nn.log_sigmoid(ret_decay_bwd.astype(F32))

    c_pad = jnp.concatenate([c.astype(F32), jnp.zeros((7, D), F32)], 0)
    c_all = _all_gather_small(c_pad, name="ag_c")[:, 0, :]
    c16 = jnp.concatenate([c_all, c_ctx[None, :], jnp.zeros((7, D), F32)], 0)
    ada_b_cols = lax.dynamic_slice_in_dim(ada_b, me * acols, acols, axis=1)[:, None, :]
    mods_shard = _ada_mods(c16, ada_w, ada_b_cols, name="ada_mods")
    mods_all = _all_gather_small(mods_shard.reshape(32, acols), name="ag_mods")

    wdefs = {"ret_in": (ret_w_in, 0, 1), "ret_out": (ret_w_out, 0, 0), "w1_0": (mlp_w1, 0, 1), "w2_0": (mlp_w2, 0, 0),
             "attn_in": (attn_w_in, 0, 1), "attn_out": (attn_w_out, 0, 0), "w1_1": (mlp_w1, 1, 1), "w2_1": (mlp_w2, 1, 0)}
    groups = [["ret_in"], ["ret_out", "w1_0", "w2_0"], ["attn_in", "attn_out", "w1_1", "w2_1"]]
    gathers, tok = [], mods_all
    for gi, keys in enumerate(groups):
        g_axes = [wdefs[k][2] for k in keys]
        placed = [_cast_place(*wdefs[k], me_arr, name=f"place_{k}") for k in keys]
        ssem, rsem, lands, tok = _gather_start(placed, g_axes, tok, name=f"ag_start{gi}")
        gathers.append((ssem, rsem, lands, g_axes))

    mods_all = (mods_all + tok[0, 0]).reshape(N_DEV, 2, 16, acols).transpose(1, 2, 0, 3).reshape(2, 16, 6, D)
    mod_lat = lax.dynamic_index_in_dim(mods_all, me, axis=1, keepdims=False)
    mod_ctx = mods_all[:, N_DEV]

    def pack(i, ks, kc):
        return jnp.stack([mod_lat[i, ks], mod_lat[i, kc], mod_ctx[i, ks], mod_ctx[i, kc]], 0)

    def gates(i, k):
        return jnp.stack([mod_lat[i, k], mod_ctx[i, k]], 0)

    def gate_epilogue(gl, gc, x_rows_lat_only):
        def epi(acc, i, j, xt, gv):
            if x_rows_lat_only:
                gate = gv[0:1, :]
            else:
                row = i * acc.shape[0] + lax.broadcasted_iota(jnp.int32, (acc.shape[0], 1), 0)
                gate = jnp.where(row >= L, gv[1:2, :], gv[0:1, :])
            return xt + gate * acc, acc
        return epi

    w1, w2 = {}, {}

    mmT = dict(M=T, bm=bmT)
    mmL = dict(M=L, bm=bmL)

    def bn_of(n, off=0):
        b = 4 * LANES
        while n % b or off % b:
            b -= LANES
        return b

    X0 = jnp.concatenate([x[0], ctx[0]], axis=0)
    g_mix0, g_mlp0 = norm_mix_g[0:1], norm_mlp_g[0:1]
    g_mix1, g_mlp1 = norm_mix_g[1:2], norm_mlp_g[1:2]
    a0 = _normmod(X0, g_mix0, pack(0, 0, 1), R=T, L=L, tr=tr, name="normmod_mix0")
    (wr_in,) = _gather_wait(*gathers[0][:3], gathers[0][3], a0, name="ag_wait0")

    bn_qk = _pick(2 * Wq_r, 512, 2 * LANES)
    nq_blocks = Wq_r // bn_qk
    kscale = float(2 * LANES) ** -0.5

    def rope_epi(acc, i, j, cos, sin):
        parts = []
        for h in range(acc.shape[1] // (2 * LANES)):
            x1 = acc[:, h * 2 * LANES:h * 2 * LANES + LANES]
            x2 = acc[:, h * 2 * LANES + LANES:(h + 1) * 2 * LANES]
            parts += [x1 * cos - x2 * sin, x2 * cos + x1 * sin]
        return (jnp.concatenate(parts, axis=1) * jnp.where(j < nq_blocks, 1.0, kscale),)

    def row_tile(arr, bm):
        return (arr, (bm, LANES), lambda i, j: (i, 0))

    (qk0,) = _mm(a0, wr_in, "nn", [BF16], N=2 * Wq_r, K=D, bn=bn_qk, bk=D, name="ret_qk", epilogue=rope_epi,
                 extras=[row_tile(rcos, bmT), row_tile(rsin, bmT)], **mmT)
    bn_vg = bn_of(2 * Wv_r, 2 * Wq_r)
    (vg0,) = _mm(a0, wr_in, "nn", [BF16], N=2 * Wv_r, K=D, bn=bn_vg, bk=D, name="ret_vg", b_col0=2 * Wq_r, **mmT)

    of, st_f = _ret_fwd(qk0, vg0, lg_f, T=T, L=L, H=RH, rev=False, name="ret_scan_f")
    ob, st_b = _ret_fwd(qk0, vg0, lg_b, T=T, L=L, H=RH, rev=True, name="ret_scan_b")
    z0 = _readout(of, ob, vg0, T=T, L=L, H=RH, tr=tr_wide, name="ret_readout")
    wr_out, w1[0], w2[0] = _gather_wait(*gathers[1][:3], gathers[1][3], z0, name="ag_wait1")

    bnD = _pick(D, 512)

    def xtile(arr, bm):
        return (arr, (bm, bnD), lambda i, j: (i, j))

    def gtile(gv):
        return (gv, (2, bnD), lambda i, j: (0, j))

    bk_v = _pick(Wv_r, 2048)
    X1, ro0 = _mm(z0, wr_out, "nn", [F32, BF16], N=D, K=Wv_r, bn=bnD, bk=bk_v, name="ret_out",
                  epilogue=gate_epilogue(None, None, False), extras=[xtile(X0, bmT), gtile(gates(0, 2))], **mmT)

    def mlp_fwd(Xin, i, g_mlp, rows, name):
        a = _normmod(Xin, g_mlp, pack(i, 3, 4), R=rows["M"], L=L, tr=tr, name=f"normmod_mlp{name}")

        def relu2(acc, i_, j_):
            u = jnp.maximum(acc, 0.0)
            return u, u * u

        bnF = _pick(FF, 512)
        u, r = _mm(a, w1[i], "nn", [BF16, BF16], N=FF, K=D, bn=bnF, bk=D, name=f"mlp_up{name}", epilogue=relu2, **rows)
        bkF = _pick(FF, 2048)
        Xout, mo = _mm(r, w2[i], "nn", [F32, BF16], N=D, K=FF, bn=bnD, bk=bkF, name=f"mlp_down{name}",
                       epilogue=gate_epilogue(None, None, rows["M"] == L),
                       extras=[xtile(Xin, rows["bm"]), gtile(gates(i, 5))], **rows)
        return a, u, r, Xout, mo

    a1, u0, r0, X2, mo0 = mlp_fwd(X1, 0, g_mlp0, mmT, "0")

    a2 = _normmod(X2, g_mix1, pack(1, 0, 1), R=T, L=L, tr=tr, name="normmod_mix1")
    wa_in, wa_out, w1[1], w2[1] = _gather_wait(*gathers[2][:3], gathers[2][3], a2, name="ag_wait2")
    Wq_a, Wk_a = Hq * LANES, Hkv * LANES

    def arope_epi(acc, i, j, cos, sin):
        heads = acc.shape[1] // LANES
        return (acc * jnp.tile(cos, (1, heads)) + _swap32(acc) * jnp.tile(sin, (1, heads)),)

    bn_q = _pick(Wq_a, 512)
    (q1,) = _mm(a2, wa_in, "nn", [BF16], N=Wq_a, K=D, bn=bn_q, bk=D, name="attn_q", epilogue=arope_epi,
                extras=[row_tile(acos, bmL), row_tile(asin, bmL)], **mmL)
    bn_k = bn_of(Wk_a, Wq_a)
    (k1,) = _mm(a2, wa_in, "nn", [BF16], N=Wk_a, K=D, bn=bn_k, bk=D, name="attn_k", b_col0=Wq_a, epilogue=arope_epi,
                extras=[row_tile(acos, bmT), row_tile(asin, bmT)], **mmT)
    bn_v = bn_of(Wk_a, Wq_a + Wk_a)
    (v1,) = _mm(a2, wa_in, "nn", [BF16], N=Wk_a, K=D, bn=bn_v, bk=D, name="attn_v", b_col0=Wq_a + Wk_a, **mmT)
    o1, lse = _attn_fwd(q1, k1, v1, attn_sink, L=L, CTX=CTX, Hkv=Hkv, G=G, name="attn_fwd")
    X3, ao = _mm(o1, wa_out, "nn", [F32, BF16], N=D, K=Wq_a, bn=bnD, bk=_pick(Wq_a, 2048), name="attn_out",
                 epilogue=gate_epilogue(None, None, True), extras=[xtile(X2, bmL), gtile(gates(1, 2))], **mmL)
    a3, u1, r1, X4, mo1 = mlp_fwd(X3, 1, g_mlp1, mmL, "1")

    dX4, dmo1, acc_head = _loss_head(X4, loss_target[0], mo1, final_norm_g[None, :], gates(1, 5)[0:1], L=L, tr=tr,
                                     name="loss_head")
    loss_part = jnp.sum(acc_head[0, 0])
    d_gf = acc_head[0, 1]
    zeros_d = jnp.zeros((D,), F32)
    dmod_lat = [[zeros_d] * 6, [zeros_d] * 6]
    dmod_ctx = [[zeros_d] * 6, [zeros_d] * 6]
    dmod_lat[1][5] = acc_head[0, 2]

    bn_dw = 512

    def mlp_bwd(dmo, a, u, r, i, rows, name):
        Mr = rows["M"]
        bkr = rows["bm"]

        def times_2u(acc, i_, j_, ut):
            return (acc * (2.0 * ut.astype(F32)),)

        bnF = _pick(FF, 512)
        (dw2,) = _mm(r, dmo, "tn", [BF16], M=FF, N=D, K=Mr, bm=_pick(FF, 1024), bn=_pick(D, bn_dw), bk=bkr,
                     name=f"mlp_down_dw{name}")
        tok_ = send_grad(f"w2_{i}", dw2, 0)
        (dh,) = _mm(dmo, w2[i], "nt", [BF16], N=FF, K=D, bn=bnF, bk=D, name=f"mlp_down_dx{name}", epilogue=times_2u,
                    extras=[(u, (rows["bm"], bnF), lambda i_, j_: (i_, j_))], dep=tok_, **rows)
        (dw1,) = _mm(a, dh, "tn", [BF16], M=D, N=FF, K=Mr, bm=_pick(D, 1024), bn=_pick(FF, bn_dw), bk=bkr,
                     name=f"mlp_up_dw{name}")
        tok_ = send_grad(f"w1_{i}", dw1, 1)
        (da,) = _mm(dh, w1[i], "nt", [F32], N=D, K=FF, bn=bnD, bk=_pick(FF, 2048), name=f"mlp_up_dx{name}", dep=tok_,
                    **rows)
        return da

    pending = []

    def send_grad(key, dw, axis):
        ssem, rsem, dw_thru, land, tok_ = _scatter_start(dw, axis, name=f"rs_start_{key}")
        pending.append((key, axis, ssem, rsem, dw_thru, land))
        return tok_

    da3 = mlp_bwd(dmo1, a3, u1, r1, 1, mmL, "1")
    dX3, dao, acc = _normmod_bwd(X3, da3, dX4, False, g_mlp1, pack(1, 3, 4), (ao, gates(1, 2)), R=L, L=L, tr=tr,
                                 name="normmod_mlp1_bwd")
    dmod_lat[1][3], dmod_lat[1][4], d_gmlp1, dmod_lat[1][2] = acc[0, 0], acc[0, 1], acc[0, 2], acc[0, 3]

    (dwa_out,) = _mm(o1, dao, "tn", [BF16], M=Wq_a, N=D, K=L, bm=_pick(Wq_a, 1024), bn=_pick(D, bn_dw), bk=bmL,
                     name="attn_out_dw")
    tok = send_grad("attn_out", dwa_out, 0)
    (do1,) = _mm(dao, wa_out, "nt", [BF16], N=Wq_a, K=D, bn=_pick(Wq_a, 512), bk=D, name="attn_out_dx", dep=tok, **mmL)
    dq1, delta1, dkx, dvx, dsink_acc = _attn_bwd_q(q1, k1, v1, do1, lse, attn_sink, L=L, CTX=CTX, Hkv=Hkv, G=G,
                                                   name="attn_bwd_q")
    dk1, dv1 = _attn_bwd_kv(q1, k1, v1, do1, lse, delta1, L=L, Hkv=Hkv, G=G, name="attn_bwd_kv")
    dp1 = _attn_assemble(dq1, dk1, dv1, dkx, dvx, acos, asin, T=T, L=L, CTX=CTX, Hq=Hq, Hkv=Hkv, tr=tr_wide,
                         name="attn_assemble")
    Wa_in = Wq_a + 2 * Wk_a
    (dwa_in,) = _mm(a2, dp1, "tn", [BF16], M=D, N=Wa_in, K=T, bm=_pick(D, 1024), bn=_pick(Wa_in, bn_dw), bk=bmT,
                    name="attn_in_dw")
    tok = send_grad("attn_in", dwa_in, 1)
    (da2,) = _mm(dp1, wa_in, "nt", [F32], N=D, K=Wa_in, bn=bnD, bk=_pick(Wa_in, 2048), name="attn_in_dx", dep=tok,
                 **mmT)
    dX2, dmo0, acc = _normmod_bwd(X2, da2, dX3, True, g_mix1, pack(1, 0, 1), (mo0, gates(0, 5)), R=T, L=L, tr=tr,
                                  name="normmod_mix1_bwd")
    dmod_lat[1][0], dmod_lat[1][1], d_gmix1, dmod_lat[0][5] = acc[0, 0], acc[0, 1], acc[0, 2] + acc[1, 2], acc[0, 3]
    dmod_ctx[1][0], dmod_ctx[1][1], dmod_ctx[0][5] = acc[1, 0], acc[1, 1], acc[1, 3]

    da1 = mlp_bwd(dmo0, a1, u0, r0, 0, mmT, "0")
    dX1, dro0, acc = _normmod_bwd(X1, da1, dX2, False, g_mlp0, pack(0, 3, 4), (ro0, gates(0, 2)), R=T, L=L, tr=tr,
                                  name="normmod_mlp0_bwd")
    dmod_lat[0][3], dmod_lat[0][4], d_gmlp0, dmod_lat[0][2] = acc[0, 0], acc[0, 1], acc[0, 2] + acc[1, 2], acc[0, 3]
    dmod_ctx[0][3], dmod_ctx[0][4], dmod_ctx[0][2] = acc[1, 0], acc[1, 1], acc[1, 3]

    (dwr_out,) = _mm(z0, dro0, "tn", [BF16], M=Wv_r, N=D, K=T, bm=_pick(Wv_r, 1024), bn=_pick(D, bn_dw), bk=bmT,
                     name="ret_out_dw")
    tok = send_grad("ret_out", dwr_out, 0)
    (dz0,) = _mm(dro0, wr_out, "nt", [BF16], N=Wv_r, K=D, bn=_pick(Wv_r, 512), bk=D, name="ret_out_dx", dep=tok, **mmT)
    do0, dg0 = _readout_bwd(dz0, of, ob, vg0, T=T, L=L, H=RH, tr=tr_wide, name="ret_readout_bwd")
    dq_f, dk_f, dv_f, dlg_f = _ret_bwd(qk0, vg0, do0, st_f, lg_f, T=T, L=L, H=RH, rev=False, name="ret_scan_f_bwd")
    dq_b, dk_b, dv_b, dlg_b = _ret_bwd(qk0, vg0, do0, st_b, lg_b, T=T, L=L, H=RH, rev=True, name="ret_scan_b_bwd")
    dp0 = _ret_assemble(dq_f, dq_b, dk_f, dk_b, dv_f, dv_b, dg0, rcos, rsin, T=T, L=L, H=RH, tr=tr_wide,
                        name="ret_assemble")
    Wr_in = 2 * Wq_r + 2 * Wv_r
    (dwr_in,) = _mm(a0, dp0, "tn", [BF16], M=D, N=Wr_in, K=T, bm=_pick(D, 1024), bn=_pick(Wr_in, bn_dw), bk=bmT,
                    name="ret_in_dw")
    tok = send_grad("ret_in", dwr_in, 1)
    (da0,) = _mm(dp0, wr_in, "nt", [F32], N=D, K=Wr_in, bn=bnD, bk=_pick(Wr_in, 2048), name="ret_in_dx", dep=tok, **mmT)
    dX0, acc = _normmod_bwd(X0, da0, dX1, False, g_mix0, pack(0, 0, 1), None, R=T, L=L, tr=tr, name="normmod_mix0_bwd")
    dmod_lat[0][0], dmod_lat[0][1], d_gmix0 = acc[0, 0], acc[0, 1], acc[0, 2] + acc[1, 2]
    dmod_ctx[0][0], dmod_ctx[0][1] = acc[1, 0], acc[1, 1]
    grad_x = dX0[:L][None]

    wmv = {"ret_in": (ret_w_in, m_ret_w_in, v_ret_w_in, 0, "ret_w_in"),
           "ret_out": (ret_w_out, m_ret_w_out, v_ret_w_out, 0, "ret_w_out"),
           "attn_in": (attn_w_in, m_attn_w_in, v_attn_w_in, 0, "attn_w_in"),
           "attn_out": (attn_w_out, m_attn_w_out, v_attn_w_out, 0, "attn_w_out"),
           "w1_0": (mlp_w1, m_mlp_w1, v_mlp_w1, 0, "mlp_w1"), "w1_1": (mlp_w1, m_mlp_w1, v_mlp_w1, 1, "mlp_w1"),
           "w2_0": (mlp_w2, m_mlp_w2, v_mlp_w2, 0, "mlp_w2"), "w2_1": (mlp_w2, m_mlp_w2, v_mlp_w2, 1, "mlp_w2")}
    big = {}

    def finish_grad(entry, after):
        key, axis, ssem, rsem, dw_thru, land = entry
        dw_done, land_done = _scatter_wait(ssem, rsem, dw_thru, land, axis, after, name=f"rs_wait_{key}")
        w_, m_, v_, layer, out_name = wmv[key]
        big[out_name] = _adamw_sharded(w_, m_, v_, layer, dw_done, land_done, axis, me_arr, big.get(out_name),
                                       name=f"adamw_{key}")
        return big[out_name][0]

    after = dX0
    for entry in pending[:-1]:
        after = finish_grad(entry, after)

    misc = jnp.zeros((D,), F32)
    misc = misc.at[0:RH].set(dlg_f[:, 0, 0]).at[RH:2 * RH].set(dlg_b[:, 0, 0])
    misc = misc.at[2 * RH:2 * RH + Hq].set(dsink_acc[:, :G, 0].reshape(Hq)).at[2 * RH + Hq].set(loss_part)
    rows = ([dmod_lat[i][k] for i in range(2) for k in range(6)] + [dmod_ctx[i][k] for i in range(2) for k in range(6)]
            + [d_gmix0, d_gmix1, d_gmlp0, d_gmlp1, d_gf, misc, zeros_d, zeros_d])
    part = jnp.stack(rows, 0)
    part_all = _all_gather_small(part, name="ag_small_grads")
    tot = _sum_devices(part_all, name="sum_small_grads")

    grad_ada_b = (tot[0:12] + tot[12:24]).reshape(2, 6 * D)
    grad_norm_mix_g, grad_norm_mlp_g, grad_final_norm_g = tot[24:26], tot[26:28], tot[28]
    grad_ret_decay_fwd = (tot[29, 0:RH] * jax.nn.sigmoid(-ret_decay_fwd[0]))[None]
    grad_ret_decay_bwd = (tot[29, RH:2 * RH] * jax.nn.sigmoid(-ret_decay_bwd[0]))[None]
    grad_attn_sink = tot[29, 2 * RH:2 * RH + Hq][None]
    loss = tot[29, 2 * RH + Hq]

    dlat_cols = lax.dynamic_slice_in_dim(part_all[:, 0:12].reshape(N_DEV, 2, 6 * D), me * acols, acols, axis=2)
    dctx_cols = lax.dynamic_slice_in_dim(tot[12:24].reshape(2, 6 * D), me * acols, acols, axis=1)
    dmod16 = jnp.concatenate([dlat_cols.transpose(1, 0, 2), dctx_cols[:, None, :], jnp.zeros((2, 7, acols), F32)], 1)
    cond_t = _silu(c16).T
    g_ada, d_ada, nm_ada, nv_ada, dcond_part = _ada_bwd(cond_t, dmod16, ada_w, m_ada_w, v_ada_w, name="ada_bwd")
    dcond = (dcond_part[0, :, 0] + dcond_part[1, :, 0]).reshape(D // LANES, LANES)
    pad_rows = -(D // LANES) % 8
    dcond_pad = jnp.concatenate([dcond, jnp.zeros((pad_rows, LANES), F32)], 0) if pad_rows else dcond
    dcond_all = _all_gather_small(dcond_pad, name="ag_dcond")
    dcond_tot = _sum_devices(dcond_all, name="sum_dcond")[:D // LANES].reshape(D)
    sg = jax.nn.sigmoid(c_ctx)
    grad_c_ctx = dcond_tot * (sg * (1.0 + c_ctx * (1.0 - sg)))

    small_w = [c_ctx, ada_b, norm_mix_g, norm_mlp_g, ret_decay_fwd, ret_decay_bwd, attn_sink, final_norm_g]
    small_g = [grad_c_ctx, grad_ada_b, grad_norm_mix_g, grad_norm_mlp_g, grad_ret_decay_fwd, grad_ret_decay_bwd,
               grad_attn_sink, grad_final_norm_g]
    small_m = [m_c_ctx, m_ada_b, m_norm_mix_g, m_norm_mlp_g, m_ret_decay_fwd, m_ret_decay_bwd, m_attn_sink,
               m_final_norm_g]
    small_v = [v_c_ctx, v_ada_b, v_norm_mix_g, v_norm_mlp_g, v_ret_decay_fwd, v_ret_decay_bwd, v_attn_sink,
               v_final_norm_g]
    sizes = [w_.size for w_ in small_w]
    total = sum(-(-s // LANES) * LANES for s in sizes)
    total_pad = -(-total // (8 * LANES)) * 8 * LANES

    def flat_pack(ts, fill):
        pieces = []
        for t_ in ts:
            f = t_.reshape(-1).astype(F32)
            pad = -f.size % LANES
            pieces.append(jnp.concatenate([f, jnp.full((pad,), fill, F32)]) if pad else f)
        pieces.append(jnp.full((total_pad - total,), fill, F32))
        return jnp.concatenate(pieces).reshape(total_pad // LANES, LANES)

    d_s, nm_s, nv_s = _adamw_flat(flat_pack(small_w, 0.0), flat_pack(small_g, 0.0), flat_pack(small_m, 0.0),
                                  flat_pack(small_v, 1.0), name="adamw_small")

    def unpack(p):
        flat = p.reshape(-1)
        res, off = [], 0
        for w_, s in zip(small_w, sizes):
            res.append(flat[off:off + s].reshape(w_.shape))
            off += -(-s // LANES) * LANES
        return res

    finish_grad(pending[-1], d_s)
    d_small, nm_small, nv_small = unpack(d_s), unpack(nm_s), unpack(nv_s)
    small_names = ["c_ctx", "ada_b", "norm_mix_g", "norm_mlp_g", "ret_decay_fwd", "ret_decay_bwd", "attn_sink",
                   "final_norm_g"]
    sm = {n: (g_, d_, m_, v_) for n, g_, d_, m_, v_ in zip(small_names, small_g, d_small, nm_small, nv_small)}

    def out4(n):
        if n == "ada_w":
            return g_ada, d_ada, nm_ada, nv_ada
        if n in big:
            return tuple(big[n])
        return sm[n]

    order = ["c_ctx", "ada_w", "ada_b", "norm_mix_g", "norm_mlp_g", "mlp_w1", "mlp_w2", "ret_w_in", "ret_w_out",
             "ret_decay_fwd", "ret_decay_bwd", "attn_w_in", "attn_w_out", "attn_sink", "final_norm_g"]
    quads = [out4(n) for n in order]
    return (loss, grad_x, *[q_[0] for q_ in quads], *[q_[1] for q_ in quads], *[q_[2] for q_ in quads],
            *[q_[3] for q_ in quads])
```

```python
import functools

import jax
import jax.numpy as jnp
from jax import lax
from jax.experimental import pallas as pl
from jax.experimental.pallas import tpu as pltpu

F32 = jnp.float32
BF16 = jnp.bfloat16

N_DEV = 8
NORM_EPS = 1e-6
CHUNK = 128
ATT_HEAD_DIM = 128
GRID_W = 64
ROPE_BASE = 10000.0
NEG_INF = -1e30
ADAM_LR, ADAM_B1, ADAM_B2, ADAM_EPS, ADAM_WD, ADAM_STEP = 0.001, 0.9, 0.999, 1e-08, 0.01, 10

V7X_VMEM_LIMIT_BYTES = 56 * 1024 * 1024
LANES = 128
MESH = pl.DeviceIdType.MESH

_NN = (((1,), (0,)), ((), ()))
_NT = (((1,), (1,)), ((), ()))
_TN = (((0,), (0,)), ((), ()))


def _dot(a, b, dn=_NN):
    return lax.dot_general(a, b, dn, preferred_element_type=F32)


def _cparams(*sem):
    return pltpu.CompilerParams(dimension_semantics=sem, vmem_limit_bytes=V7X_VMEM_LIMIT_BYTES)


def _pick(n, pref, mult=LANES):
    if n <= pref:
        return n
    best = None
    for d in range(mult, pref + 1, mult):
        if n % d == 0:
            best = d
    assert best is not None, (n, pref)
    return best


def _silu(x):
    return x * jax.nn.sigmoid(x)


def _mm(a, b, mode, out_dtypes, *, M, N, K, bm, bn, bk, name, b_col0=0, epilogue=None, extras=(), dep=None):
    assert M % bm == 0 and N % bn == 0 and K % bk == 0 and b_col0 % bn == 0, (name, M, N, K, bm, bn, bk, b_col0)
    nk = K // bk
    c0 = b_col0 // bn
    if mode == "nn":
        a_spec = pl.BlockSpec((bm, bk), lambda i, j, k: (i, k))
        b_spec = pl.BlockSpec((bk, bn), lambda i, j, k: (k, j + c0))
    elif mode == "nt":
        a_spec = pl.BlockSpec((bm, bk), lambda i, j, k: (i, k))
        b_spec = pl.BlockSpec((bn, bk), lambda i, j, k: (j + c0, k))
    else:
        a_spec = pl.BlockSpec((bk, bm), lambda i, j, k: (k, i))
        b_spec = pl.BlockSpec((bk, bn), lambda i, j, k: (k, j + c0))
    dn = {"nn": _NN, "nt": _NT, "tn": _TN}[mode]
    e_specs = [pl.BlockSpec(bs, (lambda i, j, k, f=f: f(i, j))) for (_, bs, f) in extras]
    ne, no = len(extras), len(out_dtypes)
    nd = 0 if dep is None else 1

    def body(a_ref, b_ref, *rest):
        e_refs, o_refs = rest[:ne], rest[ne + nd:ne + nd + no]
        i, j, k = pl.program_id(0), pl.program_id(1), pl.program_id(2)

        def finish(acc):
            outs = (acc,) if epilogue is None else epilogue(acc, i, j, *[e[...] for e in e_refs])
            for o_ref, o in zip(o_refs, outs):
                o_ref[...] = o.astype(o_ref.dtype)

        p = _dot(a_ref[...], b_ref[...], dn)
        if nk == 1:
            finish(p)
        else:
            acc_ref = rest[-1]

            @pl.when(k == 0)
            def _():
                acc_ref[...] = p

            @pl.when(k > 0)
            def _():
                acc_ref[...] += p

            @pl.when(k == nk - 1)
            def _():
                finish(acc_ref[...])

    outs = pl.pallas_call(
        body, name=name, grid=(M // bm, N // bn, nk),
        in_specs=[a_spec, b_spec] + e_specs + [pl.BlockSpec(memory_space=pl.ANY)] * nd,
        out_specs=[pl.BlockSpec((bm, bn), lambda i, j, k: (i, j)) for _ in out_dtypes],
        out_shape=[jax.ShapeDtypeStruct((M, N), dt) for dt in out_dtypes],
        scratch_shapes=[pltpu.VMEM((bm, bn), F32)] if nk > 1 else [],
        compiler_params=_cparams("parallel", "parallel", "arbitrary"),
    )(a, b, *[e[0] for e in extras], *([dep] if nd else []))
    return outs


def _rowwise(body, rows, vecs, outs, n_acc, *, R, L, tr, name, acc_width=None):
    assert R % tr == 0 and L % tr == 0, (name, R, L, tr)
    nl = L // tr
    n_regions = 2 if R > L else 1
    n_rows, n_vecs, n_outs = len(rows), len(vecs), len(outs)
    acc_pad = -(-n_acc // 8) * 8 if n_acc else 0

    in_specs = []
    for (_, w, cb, lat_only) in rows:
        if lat_only:
            in_specs.append(pl.BlockSpec((tr, w), lambda i, cb=cb: (jnp.minimum(i, nl - 1), cb)))
        else:
            in_specs.append(pl.BlockSpec((tr, w), lambda i, cb=cb: (i, cb)))
    for v in vecs:
        in_specs.append(pl.BlockSpec(v.shape, lambda i, nd=v.ndim: (0,) * nd))
    out_specs = [pl.BlockSpec((tr, w), lambda i: (i, 0)) for (w, _) in outs]
    out_shape = [jax.ShapeDtypeStruct((R, w), dt) for (w, dt) in outs]
    if n_acc:
        out_specs.append(pl.BlockSpec((None, acc_pad, acc_width), lambda i: (jnp.where(i >= nl, 1, 0), 0, 0)))
        out_shape.append(jax.ShapeDtypeStruct((n_regions, acc_pad, acc_width), F32))

    def kern(*refs):
        i = pl.program_id(0)
        is_ctx = i >= nl
        ins = [r[...] for r in refs[:n_rows + n_vecs]]
        o_refs = refs[n_rows + n_vecs:]
        out_tiles, acc_rows = body(is_ctx, *ins)
        for o_ref, o in zip(o_refs[:n_outs], out_tiles):
            o_ref[...] = o.astype(o_ref.dtype)
        if n_acc:
            acc_ref = o_refs[n_outs]

            @pl.when((i == 0) | (i == nl))
            def _():
                acc_ref[...] = jnp.zeros_like(acc_ref)

            for r, row in enumerate(acc_rows):
                acc_ref[r:r + 1, :] += row

    res = pl.pallas_call(
        kern, name=name, grid=(R // tr,), in_specs=in_specs, out_specs=out_specs, out_shape=out_shape,
        compiler_params=_cparams("arbitrary"),
    )(*[r[0] for r in rows], *vecs)
    return res


def _colsum(x):
    return jnp.sum(x, axis=0, keepdims=True)


def _rms_stats(x):
    r = lax.rsqrt(jnp.mean(x * x, axis=-1, keepdims=True) + NORM_EPS)
    return x * r, r


def _sel(is_ctx, pk, lat_row, ctx_row):
    return jnp.where(is_ctx, pk[ctx_row:ctx_row + 1, :], pk[lat_row:lat_row + 1, :])


def _normmod(x, g, pk, *, R, L, tr, name):
    D = x.shape[-1]

    def body(is_ctx, xt, gv, pkv):
        xh, _ = _rms_stats(xt)
        sh, sc = _sel(is_ctx, pkv, 0, 2), _sel(is_ctx, pkv, 1, 3)
        return ((xh * gv) * (1.0 + sc) + sh,), ()

    return _rowwise(body, [(x, D, 0, False)], [g, pk], [(D, BF16)], 0, R=R, L=L, tr=tr, name=name)[0]


def _normmod_bwd(x_in, da, dx_out, dx_out_lat_only, g, pk, prev, *, R, L, tr, name):
    D = x_in.shape[-1]
    has_prev = prev is not None

    def body(is_ctx, *t):
        if has_prev:
            xt, dat, dxo, mp, gv, pkv, gates = t
        else:
            xt, dat, dxo, gv, pkv = t
        xh, r = _rms_stats(xt)
        sc = _sel(is_ctx, pkv, 1, 3)
        if dx_out_lat_only:
            dxo = jnp.where(is_ctx, 0.0, dxo)
        dn = dat * (1.0 + sc)
        w = dn * gv
        dxi = dxo + r * (w - xh * jnp.mean(w * xh, axis=-1, keepdims=True))
        accs = [_colsum(dat), _colsum(dat * (xh * gv)), _colsum(dn * xh)]
        outs = [dxi]
        if has_prev:
            gate = _sel(is_ctx, gates, 0, 1)
            outs.append(dxi * gate)
            accs.append(_colsum(dxi * mp.astype(F32)))
        return outs, accs

    rows = [(x_in, D, 0, False), (da, D, 0, False), (dx_out, D, 0, dx_out_lat_only)]
    vecs = [g, pk]
    outs = [(D, F32)]
    if has_prev:
        rows.append((prev[0], D, 0, False))
        vecs.append(prev[1])
        outs.append((D, BF16))
    return _rowwise(body, rows, vecs, outs, 4 if has_prev else 3, R=R, L=L, tr=tr, name=name, acc_width=D)


def _loss_head(x4, target, m_prev, gf, gate, *, L, tr, name):
    D = x4.shape[-1]

    def body(is_ctx, xt, tg, mp, gfv, gatev):
        xh, r = _rms_stats(xt)
        e = xh * gfv - tg
        dy = e * (1.0 / D)
        w = dy * gfv
        dx = r * (w - xh * jnp.mean(w * xh, axis=-1, keepdims=True))
        accs = [_colsum(e * e) * (0.5 / D), _colsum(dy * xh), _colsum(dx * mp.astype(F32))]
        return (dx, dx * gatev), accs

    return _rowwise(body, [(x4, D, 0, False), (target, D, 0, False), (m_prev, D, 0, False)], [gf, gate],
                    [(D, F32), (D, BF16)], 3, R=L, L=L, tr=tr, name=name, acc_width=D)


RET_CHUNK = 2 * LANES
RET_HEADS_PER_STEP = 2


def _decays(lgh, rev):
    C = RET_CHUNK
    ii = lax.broadcasted_iota(jnp.int32, (C, C), 0)
    jj = lax.broadcasted_iota(jnp.int32, (C, C), 1)
    ri = lax.broadcasted_iota(jnp.int32, (C, 1), 0).astype(F32)
    diff = (jj - ii if rev else ii - jj)
    amat = jnp.where(diff >= 0, jnp.exp(lgh * jnp.maximum(diff, 0).astype(F32)), 0.0)
    pos = (C - ri) if rev else (ri + 1.0)
    bq = jnp.exp(lgh * pos)
    bk = jnp.exp(lgh * (C - pos))
    return amat, bq, bk, pos


def _ret_geometry(T, L, H, rev, backward):
    C = RET_CHUNK
    assert T % C == 0 and L % C == 0, (T, L)
    nT, nL = T // C, L // C
    hb = RET_HEADS_PER_STEP if H % RET_HEADS_PER_STEP == 0 else 1

    def step(s):
        return (nT - 1 - s) if backward else s

    def chunk(s):
        s = step(s)
        return (nT - 1 - s) if rev else (s + nL) % nT

    return C, nT, hb, chunk, step


def _ret_fwd(qk, vg, lg, *, T, L, H, rev, name):
    dk, dv = 2 * LANES, 4 * LANES
    C, nT, hb, chunk, step = _ret_geometry(T, L, H, rev, False)

    def body(lg_ref, q_ref, k_ref, v_ref, o_ref, st_ref, s_scr):
        hg, s = pl.program_id(0), pl.program_id(1)

        @pl.when(s == 0)
        def _():
            s_scr[...] = jnp.zeros_like(s_scr)

        for hh in range(hb):
            lgh = lg_ref[0, hg * hb + hh]
            amat, bq, bk, _ = _decays(lgh, rev)
            q, k = q_ref[:, hh * dk:(hh + 1) * dk], k_ref[:, hh * dk:(hh + 1) * dk]
            v = v_ref[:, hh * dv:(hh + 1) * dv]
            stb = s_scr[hh].astype(BF16)
            st_ref[hh] = stb
            scores = _dot(q, k, _NT) * amat
            o_ref[:, hh * dv:(hh + 1) * dv] = _dot(scores.astype(BF16), v) + _dot(q, stb) * bq
            kd = (k.astype(F32) * bk).astype(BF16)
            s_scr[hh] = s_scr[hh] * jnp.exp(lgh * C) + _dot(kd, v, _TN)

    return pl.pallas_call(
        body, name=name, grid=(H // hb, nT),
        in_specs=[pl.BlockSpec(memory_space=pltpu.SMEM),
                  pl.BlockSpec((C, hb * dk), lambda h, s: (chunk(s), h)),
                  pl.BlockSpec((C, hb * dk), lambda h, s: (chunk(s), H // hb + h)),
                  pl.BlockSpec((C, hb * dv), lambda h, s: (chunk(s), h))],
        out_specs=[pl.BlockSpec((C, hb * dv), lambda h, s: (chunk(s), h)),
                   pl.BlockSpec((hb, None, dk, dv), lambda h, s: (h, s, 0, 0))],
        out_shape=[jax.ShapeDtypeStruct((T, H * dv), F32), jax.ShapeDtypeStruct((H, nT, dk, dv), BF16)],
        scratch_shapes=[pltpu.VMEM((hb, dk, dv), F32)],
        compiler_params=_cparams("parallel", "arbitrary"),
    )(lg, qk, qk, vg)


def _ret_bwd(qk, vg, do, states, lg, *, T, L, H, rev, name):
    dk, dv = 2 * LANES, 4 * LANES
    C, nT, hb, chunk, step = _ret_geometry(T, L, H, rev, True)

    def body(lg_ref, q_ref, k_ref, v_ref, do_ref, st_ref, dq_ref, dk_ref, dv_ref, dlg_ref, ds_scr):
        hg, s = pl.program_id(0), pl.program_id(1)

        @pl.when(s == 0)
        def _():
            ds_scr[...] = jnp.zeros_like(ds_scr)
            dlg_ref[...] = jnp.zeros_like(dlg_ref)

        for hh in range(hb):
            lgh = lg_ref[0, hg * hb + hh]
            amat, bq, bk, pos = _decays(lgh, rev)
            ksl, vsl = slice(hh * dk, (hh + 1) * dk), slice(hh * dv, (hh + 1) * dv)
            q, k, v, dob = q_ref[:, ksl], k_ref[:, ksl], v_ref[:, vsl], do_ref[:, vsl]
            stb = st_ref[hh]
            ds_new = ds_scr[hh]
            dsb = ds_new.astype(BF16)
            qf, kf = q.astype(F32), k.astype(F32)
            scores = (_dot(q, k, _NT) * amat).astype(BF16)
            dqk = (_dot(dob, v, _NT) * amat).astype(BF16)
            dq = _dot(dqk, k) + _dot(dob, stb, _NT) * bq
            dkk = _dot(dqk, q, _TN) + _dot(v, dsb, _NT) * bk
            kd = (kf * bk).astype(BF16)
            dvv = _dot(scores, dob, _TN) + _dot(kd, dsb)
            dod = (dob.astype(F32) * bq).astype(BF16)
            ds_prev = ds_new * jnp.exp(lgh * C) + _dot(q, dod, _TN)
            ds_scr[hh] = ds_prev
            dq_ref[:, ksl] = dq
            dk_ref[:, ksl] = dkk
            dv_ref[:, vsl] = dvv
            dlg = (jnp.sum(pos * jnp.sum(qf * dq - kf * dkk, axis=-1, keepdims=True))
                   + C * jnp.sum(ds_prev * stb.astype(F32)))
            dlg_ref[hh] += dlg

    qspec = pl.BlockSpec((C, hb * dk), lambda h, s: (chunk(s), h))
    vspec = pl.BlockSpec((C, hb * dv), lambda h, s: (chunk(s), h))
    return pl.pallas_call(
        body, name=name, grid=(H // hb, nT),
        in_specs=[pl.BlockSpec(memory_space=pltpu.SMEM), qspec,
                  pl.BlockSpec((C, hb * dk), lambda h, s: (chunk(s), H // hb + h)), vspec, vspec,
                  pl.BlockSpec((hb, None, dk, dv), lambda h, s: (h, step(s), 0, 0))],
        out_specs=[qspec, qspec, vspec, pl.BlockSpec((hb, 8, LANES), lambda h, s: (h, 0, 0))],
        out_shape=[jax.ShapeDtypeStruct((T, H * dk), F32), jax.ShapeDtypeStruct((T, H * dk), F32),
                   jax.ShapeDtypeStruct((T, H * dv), F32), jax.ShapeDtypeStruct((H, 8, LANES), F32)],
        scratch_shapes=[pltpu.VMEM((hb, dk, dv), F32)],
        compiler_params=_cparams("parallel", "arbitrary"),
    )(lg, qk, qk, vg, do, states)


def _readout(o_f, o_b, vg, *, T, L, H, tr, name):
    dv = 4 * LANES
    W = H * dv

    def body(is_ctx, of, ob, g):
        o = of + ob
        parts = []
        for h in range(H):
            oh = o[:, h * dv:(h + 1) * dv]
            parts.append(oh * lax.rsqrt(jnp.mean(oh * oh, axis=-1, keepdims=True) + NORM_EPS))
        y = jnp.concatenate(parts, axis=1)
        return (_silu(g.astype(F32)) * y,), ()

    return _rowwise(body, [(o_f, W, 0, False), (o_b, W, 0, False), (vg, W, 1, False)], [], [(W, BF16)], 0,
                    R=T, L=L, tr=tr, name=name)[0]


def _readout_bwd(dz, o_f, o_b, vg, *, T, L, H, tr, name):
    dv = 4 * LANES
    W = H * dv

    def body(is_ctx, dzt, of, ob, g):
        o = of + ob
        gf = g.astype(F32)
        sg = jax.nn.sigmoid(gf)
        dzf = dzt.astype(F32)
        dy = dzf * (gf * sg)
        ys, dos = [], []
        for h in range(H):
            sl = slice(h * dv, (h + 1) * dv)
            oh, dyh = o[:, sl], dy[:, sl]
            r = lax.rsqrt(jnp.mean(oh * oh, axis=-1, keepdims=True) + NORM_EPS)
            yh = oh * r
            ys.append(yh)
            dos.append(r * (dyh - yh * jnp.mean(dyh * yh, axis=-1, keepdims=True)))
        y = jnp.concatenate(ys, axis=1)
        dg = dzf * y * (sg * (1.0 + gf * (1.0 - sg)))
        return (jnp.concatenate(dos, axis=1), dg), ()

    return _rowwise(body, [(dz, W, 0, False), (o_f, W, 0, False), (o_b, W, 0, False), (vg, W, 1, False)], [],
                    [(W, BF16), (W, BF16)], 0, R=T, L=L, tr=tr, name=name)


def _ret_assemble(dq_f, dq_b, dk_f, dk_b, dv_f, dv_b, dg, cos, sin, *, T, L, H, tr, name):
    dk, dv = 2 * LANES, 4 * LANES
    Wq, Wv = H * dk, H * dv
    kscale = float(dk) ** -0.5

    def unrope(d, c, s_, scale):
        parts = []
        for h in range(H):
            d1, d2 = d[:, h * dk:h * dk + LANES], d[:, h * dk + LANES:(h + 1) * dk]
            parts += [(d1 * c + d2 * s_) * scale, (d2 * c - d1 * s_) * scale]
        return jnp.concatenate(parts, axis=1)

    def body(is_ctx, qf, qb, kf, kb, vf, vb, g, c, s_):
        dq = unrope(qf + qb, c, s_, 1.0)
        dkk = unrope(kf + kb, c, s_, kscale)
        return (jnp.concatenate([dq.astype(BF16), dkk.astype(BF16), (vf + vb).astype(BF16), g], axis=1),), ()

    rows = [(dq_f, Wq, 0, False), (dq_b, Wq, 0, False), (dk_f, Wq, 0, False), (dk_b, Wq, 0, False),
            (dv_f, Wv, 0, False), (dv_b, Wv, 0, False), (dg, Wv, 0, False),
            (cos, LANES, 0, False), (sin, LANES, 0, False)]
    return _rowwise(body, rows, [], [(2 * Wq + 2 * Wv, BF16)], 0, R=T, L=L, tr=tr, name=name)[0]


def _swap32(x):
    n = x.shape[-1]
    lane = lax.broadcasted_iota(jnp.int32, x.shape, x.ndim - 1)
    return jnp.where(lane % 64 < 32, pltpu.roll(x, n - 32, x.ndim - 1), pltpu.roll(x, 32, x.ndim - 1))


def _band_masks(n, nb):
    ii = lax.broadcasted_iota(jnp.int32, (CHUNK, CHUNK), 0)
    jj = lax.broadcasted_iota(jnp.int32, (CHUNK, CHUNK), 1)
    return (jj >= ii) & (n > 0), (jj <= ii) & (n < nb - 1)


def _attn_kv_specs(L, CTX, nb):
    blk = lambda f: pl.BlockSpec((CHUNK, LANES), lambda h, n: (f(n), h))
    prev_, cur_, next_ = (lambda n: jnp.maximum(n - 1, 0)), (lambda n: n), (lambda n: jnp.minimum(n + 1, nb - 1))
    ctx_spec = pl.BlockSpec((CTX, LANES), lambda h, n: (L // CTX, h))
    return [blk(prev_), blk(cur_), blk(next_), ctx_spec]


def _attn_fwd(q, k, v, sink, *, L, CTX, Hkv, G, name):
    nb = L // CHUNK
    scale = float(ATT_HEAD_DIM) ** -0.5
    kvs = _attn_kv_specs(L, CTX, nb)

    def body(sink_ref, q_ref, kp, kc, kn, kx, vp, vc, vn, vx, o_ref, lse_ref):
        h, n = pl.program_id(0), pl.program_id(1)
        mp, mn = _band_masks(n, nb)
        for g in range(G):
            qg = q_ref[:, g * LANES:(g + 1) * LANES]
            sp = jnp.where(mp, _dot(qg, kp[...], _NT) * scale, NEG_INF)
            sc = _dot(qg, kc[...], _NT) * scale
            sn = jnp.where(mn, _dot(qg, kn[...], _NT) * scale, NEG_INF)
            sx = _dot(qg, kx[...], _NT) * scale
            sk = sink_ref[0, h * G + g]
            rmax = lambda t: jnp.max(t, axis=-1, keepdims=True)
            m = jnp.maximum(jnp.maximum(jnp.maximum(rmax(sp), rmax(sc)), jnp.maximum(rmax(sn), rmax(sx))), sk)
            pp, pc, pn, px = jnp.exp(sp - m), jnp.exp(sc - m), jnp.exp(sn - m), jnp.exp(sx - m)
            rsum = lambda t: jnp.sum(t, axis=-1, keepdims=True)
            den = rsum(pp) + rsum(pc) + rsum(pn) + rsum(px) + jnp.exp(sk - m)
            o = (_dot(pp.astype(BF16), vp[...]) + _dot(pc.astype(BF16), vc[...])
                 + _dot(pn.astype(BF16), vn[...]) + _dot(px.astype(BF16), vx[...]))
            o_ref[:, g * LANES:(g + 1) * LANES] = (o / den).astype(o_ref.dtype)
            lse_ref[:, g:g + 1] = m + jnp.log(den)

    return pl.pallas_call(
        body, name=name, grid=(Hkv, nb),
        in_specs=[pl.BlockSpec(memory_space=pltpu.SMEM),
                  pl.BlockSpec((CHUNK, G * LANES), lambda h, n: (n, h))] + kvs + kvs,
        out_specs=[pl.BlockSpec((CHUNK, G * LANES), lambda h, n: (n, h)),
                   pl.BlockSpec((None, CHUNK, G), lambda h, n: (h, n, 0))],
        out_shape=[jax.ShapeDtypeStruct((L, Hkv * G * LANES), BF16), jax.ShapeDtypeStruct((Hkv, L, G), F32)],
        compiler_params=_cparams("parallel", "parallel"),
    )(sink, q, k, k, k, k, v, v, v, v)


def _attn_bwd_q(q, k, v, do, lse, sink, *, L, CTX, Hkv, G, name):
    nb = L // CHUNK
    scale = float(ATT_HEAD_DIM) ** -0.5
    kvs = _attn_kv_specs(L, CTX, nb)
    qspec = pl.BlockSpec((CHUNK, G * LANES), lambda h, n: (n, h))
    rowspec = pl.BlockSpec((None, CHUNK, G), lambda h, n: (h, n, 0))

    def body(sink_ref, q_ref, do_ref, lse_ref, kp, kc, kn, kx, vp, vc, vn, vx,
             dq_ref, dl_ref, dkx_ref, dvx_ref, dsk_ref):
        h, n = pl.program_id(0), pl.program_id(1)
        mp, mn = _band_masks(n, nb)

        @pl.when(n == 0)
        def _():
            dkx_ref[...] = jnp.zeros_like(dkx_ref)
            dvx_ref[...] = jnp.zeros_like(dvx_ref)
            dsk_ref[...] = jnp.zeros_like(dsk_ref)

        for g in range(G):
            sl = slice(g * LANES, (g + 1) * LANES)
            qg, dog = q_ref[:, sl], do_ref[:, sl]
            lse_g = lse_ref[:, g:g + 1]
            sk = sink_ref[0, h * G + g]
            ks, vs, masks = (kp, kc, kn, kx), (vp, vc, vn, vx), (mp, None, mn, None)
            ps, dps = [], []
            for kr, vr, msk in zip(ks, vs, masks):
                s_ = _dot(qg, kr[...], _NT) * scale
                p = jnp.exp(s_ - lse_g)
                if msk is not None:
                    p = jnp.where(msk, p, 0.0)
                ps.append(p)
                dps.append(_dot(dog, vr[...], _NT))
            delta = sum(jnp.sum(p * dp, axis=-1, keepdims=True) for p, dp in zip(ps, dps))
            dq = jnp.zeros((CHUNK, LANES), F32)
            for idx, (p, dp, kr) in enumerate(zip(ps, dps, ks)):
                ds = (p * (dp - delta) * scale).astype(BF16)
                dq = dq + _dot(ds, kr[...])
                if idx == 3:
                    dkx_ref[...] += _dot(ds, qg, _TN)
                    dvx_ref[...] += _dot(p.astype(BF16), dog, _TN)
            dq_ref[:, sl] = dq
            dl_ref[:, g:g + 1] = delta
            dsk_ref[g:g + 1, :] += -jnp.sum(jnp.exp(sk - lse_g) * delta)

    return pl.pallas_call(
        body, name=name, grid=(Hkv, nb),
        in_specs=[pl.BlockSpec(memory_space=pltpu.SMEM), qspec, qspec, rowspec] + kvs + kvs,
        out_specs=[qspec, rowspec,
                   pl.BlockSpec((CTX, LANES), lambda h, n: (0, h)), pl.BlockSpec((CTX, LANES), lambda h, n: (0, h)),
                   pl.BlockSpec((None, 8, LANES), lambda h, n: (h, 0, 0))],
        out_shape=[jax.ShapeDtypeStruct((L, Hkv * G * LANES), F32), jax.ShapeDtypeStruct((Hkv, L, G), F32),
                   jax.ShapeDtypeStruct((CTX, Hkv * LANES), F32), jax.ShapeDtypeStruct((CTX, Hkv * LANES), F32),
                   jax.ShapeDtypeStruct((Hkv, 8, LANES), F32)],
        compiler_params=_cparams("parallel", "arbitrary"),
    )(sink, q, do, lse, k, k, k, k, v, v, v, v)


def _attn_bwd_kv(q, k, v, do, lse, delta, *, L, Hkv, G, name):
    nb = L // CHUNK
    scale = float(ATT_HEAD_DIM) ** -0.5
    fs = [(lambda n: jnp.maximum(n - 1, 0)), (lambda n: n), (lambda n: jnp.minimum(n + 1, nb - 1))]
    qspecs = [pl.BlockSpec((CHUNK, G * LANES), lambda h, n, f=f: (f(n), h)) for f in fs]
    rspecs = [pl.BlockSpec((None, CHUNK, G), lambda h, n, f=f: (h, f(n), 0)) for f in fs]
    kspec = pl.BlockSpec((CHUNK, LANES), lambda h, n: (n, h))

    def body(k_ref, v_ref, q0, q1, q2, d0, d1, d2, l0, l1, l2, e0, e1, e2, dk_ref, dv_ref):
        n = pl.program_id(1)
        ii = lax.broadcasted_iota(jnp.int32, (CHUNK, CHUNK), 0)
        jj = lax.broadcasted_iota(jnp.int32, (CHUNK, CHUNK), 1)
        masks = ((jj <= ii) & (n > 0), None, (jj >= ii) & (n < nb - 1))
        kb, vb = k_ref[...], v_ref[...]
        dk = jnp.zeros((CHUNK, LANES), F32)
        dv = jnp.zeros((CHUNK, LANES), F32)
        for qr, dr, lr, er, msk in zip((q0, q1, q2), (d0, d1, d2), (l0, l1, l2), (e0, e1, e2), masks):
            for g in range(G):
                sl = slice(g * LANES, (g + 1) * LANES)
                qg, dog = qr[:, sl], dr[:, sl]
                p = jnp.exp(_dot(qg, kb, _NT) * scale - lr[:, g:g + 1])
                if msk is not None:
                    p = jnp.where(msk, p, 0.0)
                ds = (p * (_dot(dog, vb, _NT) - er[:, g:g + 1]) * scale).astype(BF16)
                dk = dk + _dot(ds, qg, _TN)
                dv = dv + _dot(p.astype(BF16), dog, _TN)
        dk_ref[...] = dk
        dv_ref[...] = dv

    return pl.pallas_call(
        body, name=name, grid=(Hkv, nb),
        in_specs=[kspec, kspec] + qspecs + qspecs + rspecs + rspecs,
        out_specs=[kspec, kspec],
        out_shape=[jax.ShapeDtypeStruct((L, Hkv * LANES), F32), jax.ShapeDtypeStruct((L, Hkv * LANES), F32)],
        compiler_params=_cparams("parallel", "parallel"),
    )(k, v, q, q, q, do, do, do, lse, lse, lse, delta, delta, delta)


def _attn_assemble(dq, dk_lat, dv_lat, dk_ctx, dv_ctx, cos, sin, *, T, L, CTX, Hq, Hkv, tr, name):
    Wq, Wk = Hq * LANES, Hkv * LANES
    ctx_blocks = CTX // tr
    nl = L // tr

    def unrope(d, c, s_, heads):
        return d * jnp.tile(c, (1, heads)) + _swap32(d * jnp.tile(s_, (1, heads)))

    def body(is_ctx, dqt, dkl, dvl, dkc, dvc, c, s_):
        dq_ = jnp.where(is_ctx, 0.0, unrope(dqt, c, s_, Hq))
        dk_ = unrope(jnp.where(is_ctx, dkc, dkl), c, s_, Hkv)
        dv_ = jnp.where(is_ctx, dvc, dvl)
        return (jnp.concatenate([dq_, dk_, dv_], axis=1),), ()

    def ctx_map(i):
        return (jnp.clip(i - nl, 0, ctx_blocks - 1), 0)

    assert T % tr == 0 and L % tr == 0 and CTX % tr == 0
    in_specs = [pl.BlockSpec((tr, Wq), lambda i: (jnp.minimum(i, nl - 1), 0)),
                pl.BlockSpec((tr, Wk), lambda i: (jnp.minimum(i, nl - 1), 0)),
                pl.BlockSpec((tr, Wk), lambda i: (jnp.minimum(i, nl - 1), 0)),
                pl.BlockSpec((tr, Wk), ctx_map), pl.BlockSpec((tr, Wk), ctx_map),
                pl.BlockSpec((tr, LANES), lambda i: (i, 0)), pl.BlockSpec((tr, LANES), lambda i: (i, 0))]

    def kern(dq_r, dkl_r, dvl_r, dkc_r, dvc_r, c_r, s_r, o_ref):
        is_ctx = pl.program_id(0) >= nl
        (out,), _ = body(is_ctx, dq_r[...], dkl_r[...], dvl_r[...], dkc_r[...], dvc_r[...], c_r[...], s_r[...])
        o_ref[...] = out.astype(o_ref.dtype)

    return pl.pallas_call(
        kern, name=name, grid=(T // tr,), in_specs=in_specs,
        out_specs=pl.BlockSpec((tr, Wq + 2 * Wk), lambda i: (i, 0)),
        out_shape=jax.ShapeDtypeStruct((T, Wq + 2 * Wk), BF16),
        compiler_params=_cparams("parallel"),
    )(dq, dk_lat, dv_lat, dk_ctx, dv_ctx, cos, sin)


def _my_place():
    x, y, c = lax.axis_index("x"), lax.axis_index("y"), lax.axis_index("c")
    return x, y, c


def _all_gather_small(v, *, name):
    R, C = v.shape

    def body(x_ref, out_ref, send_sems, recv_sems, local_sem):
        x, y, c = _my_place()
        me, sibling = (x, y, c), (x, y, 1 - c)
        chips = [(1 - x, y), (x, 1 - y), (1 - x, 1 - y)]

        def slot(px, py, pc):
            return out_ref.at[4 * px + 2 * py + pc]

        def copy(k, block, to, src=None):
            return pltpu.make_async_remote_copy(
                src_ref=slot(*block) if src is None else src, dst_ref=slot(*block),
                send_sem=send_sems.at[k], recv_sem=recv_sems.at[k], device_id=to, device_id_type=MESH)

        mine = pltpu.make_async_copy(x_ref, slot(*me), local_sem)
        mine.start()
        first = [copy(0, me, sibling, src=x_ref)]
        first += [copy(1 + j, me, (*chip, c), src=x_ref) for j, chip in enumerate(chips)]
        for cp in first:
            cp.start()
        passed = [copy(4 + j, (*chip, c), sibling) for j, chip in enumerate(chips)]
        for j, chip in enumerate(chips):
            copy(1 + j, (*chip, c), me).wait_recv()
            passed[j].start()
        copy(0, sibling, me).wait_recv()
        for j, chip in enumerate(chips):
            copy(4 + j, (*chip, 1 - c), me).wait_recv()
        for cp in first + passed:
            cp.wait_send()
        mine.wait()

    return pl.pallas_call(
        body, name=name, out_shape=jax.ShapeDtypeStruct((N_DEV, R, C), v.dtype),
        in_specs=[pl.BlockSpec(memory_space=pltpu.VMEM)], out_specs=pl.BlockSpec(memory_space=pltpu.VMEM),
        scratch_shapes=[pltpu.SemaphoreType.DMA((7,)), pltpu.SemaphoreType.DMA((7,)), pltpu.SemaphoreType.DMA],
    )(v)


def _shard_slice(ref, axis, idx, size):
    if axis == 1:
        return ref.at[:, pl.ds(idx * size, size), :]
    return ref.at[:, :, pl.ds(idx * size, size)]


def _all_gather_weights(shards, axes, *, name):
    nt = len(shards)
    sizes = [s.shape[a] for s, a in zip(shards, axes)]
    out_shape = []
    for s, a in zip(shards, axes):
        shp = list(s.shape)
        shp[a] *= N_DEV
        out_shape.append(jax.ShapeDtypeStruct(tuple(shp), s.dtype))

    def body(*refs):
        ins, outs = refs[:nt], refs[nt:2 * nt]
        send_sems, recv_sems, local_sems = refs[2 * nt:]
        x, y, c = _my_place()
        me, sibling = (x, y, c), (x, y, 1 - c)
        chips = [(1 - x, y), (x, 1 - y), (1 - x, 1 - y)]
        all_sends = []
        locals_ = []
        for t in range(nt):
            def slot(px, py, pc, t=t):
                return _shard_slice(outs[t], axes[t], 4 * px + 2 * py + pc, sizes[t])

            def copy(k, block, to, src=None, t=t, slot=slot):
                return pltpu.make_async_remote_copy(
                    src_ref=slot(*block) if src is None else src, dst_ref=slot(*block),
                    send_sem=send_sems.at[t, k], recv_sem=recv_sems.at[t, k], device_id=to, device_id_type=MESH)

            mine = pltpu.make_async_copy(ins[t], slot(*me), local_sems.at[t])
            mine.start()
            locals_.append(mine)
            first = [copy(0, me, sibling, src=ins[t])]
            first += [copy(1 + j, me, (*chip, c), src=ins[t]) for j, chip in enumerate(chips)]
            for cp in first:
                cp.start()
            all_sends += first
        for t in range(nt):
            def slot(px, py, pc, t=t):
                return _shard_slice(outs[t], axes[t], 4 * px + 2 * py + pc, sizes[t])

            def copy(k, block, to, t=t, slot=slot):
                return pltpu.make_async_remote_copy(
                    src_ref=slot(*block), dst_ref=slot(*block),
                    send_sem=send_sems.at[t, k], recv_sem=recv_sems.at[t, k], device_id=to, device_id_type=MESH)

            passed = [copy(4 + j, (*chip, c), sibling) for j, chip in enumerate(chips)]
            for j, chip in enumerate(chips):
                copy(1 + j, (*chip, c), me).wait_recv()
                passed[j].start()
            all_sends += passed
        for t in range(nt):
            def slot(px, py, pc, t=t):
                return _shard_slice(outs[t], axes[t], 4 * px + 2 * py + pc, sizes[t])

            def copy(k, block, to, t=t, slot=slot):
                return pltpu.make_async_remote_copy(
                    src_ref=slot(*block), dst_ref=slot(*block),
                    send_sem=send_sems.at[t, k], recv_sem=recv_sems.at[t, k], device_id=to, device_id_type=MESH)

            copy(0, sibling, me).wait_recv()
            for j, chip in enumerate(chips):
                copy(4 + j, (*chip, 1 - c), me).wait_recv()
        for cp in all_sends:
            cp.wait_send()
        for mine in locals_:
            mine.wait()

    return pl.pallas_call(
        body, name=name, out_shape=out_shape,
        in_specs=[pl.BlockSpec(memory_space=pl.ANY)] * nt, out_specs=[pl.BlockSpec(memory_space=pl.ANY)] * nt,
        scratch_shapes=[pltpu.SemaphoreType.DMA((nt, 7)), pltpu.SemaphoreType.DMA((nt, 7)),
                        pltpu.SemaphoreType.DMA((nt,))],
    )(*shards)


def _exchange_grads(grads, axes, *, name):
    nt = len(grads)
    sizes = [g.shape[a] // N_DEV for g, a in zip(grads, axes)]
    out_shape = []
    for g, a, sz in zip(grads, axes, sizes):
        shp = list(g.shape)
        shp[a] = sz
        out_shape.append(jax.ShapeDtypeStruct((N_DEV, *shp), g.dtype))

    def body(*refs):
        ins, outs = refs[:nt], refs[nt:2 * nt]
        send_sems, recv_sems, local_sems = refs[2 * nt:]
        x, y, c = _my_place()
        my_idx = 4 * x + 2 * y + c

        def peer(r):
            px = (1 - x) if r & 4 else x
            py = (1 - y) if r & 2 else y
            pc = (1 - c) if r & 1 else c
            return (px, py, pc)

        copies, locals_ = [], []
        for t in range(nt):
            mine = pltpu.make_async_copy(_shard_slice(ins[t], axes[t], my_idx, sizes[t]), outs[t].at[my_idx],
                                         local_sems.at[t])
            mine.start()
            locals_.append(mine)
            for r in range(1, N_DEV):
                p = peer(r)
                p_idx = 4 * p[0] + 2 * p[1] + p[2]
                cp = pltpu.make_async_remote_copy(
                    src_ref=_shard_slice(ins[t], axes[t], p_idx, sizes[t]), dst_ref=outs[t].at[my_idx],
                    send_sem=send_sems.at[t, r - 1], recv_sem=recv_sems.at[t, r - 1], device_id=p, device_id_type=MESH)
                cp.start()
                copies.append((cp, t, r, p_idx))
        for cp, t, r, p_idx in copies:
            pltpu.make_async_remote_copy(
                src_ref=_shard_slice(ins[t], axes[t], p_idx, sizes[t]), dst_ref=outs[t].at[p_idx],
                send_sem=send_sems.at[t, r - 1], recv_sem=recv_sems.at[t, r - 1], device_id=peer(r),
                device_id_type=MESH).wait_recv()
        for cp, _, _, _ in copies:
            cp.wait_send()
        for mine in locals_:
            mine.wait()

    return pl.pallas_call(
        body, name=name, out_shape=out_shape,
        in_specs=[pl.BlockSpec(memory_space=pl.ANY)] * nt, out_specs=[pl.BlockSpec(memory_space=pl.ANY)] * nt,
        scratch_shapes=[pltpu.SemaphoreType.DMA((nt, 7)), pltpu.SemaphoreType.DMA((nt, 7)),
                        pltpu.SemaphoreType.DMA((nt,))],
    )(*grads)


_HBM_SPEC = pl.BlockSpec(memory_space=pltpu.HBM)
_SEM_SPEC = pl.BlockSpec(memory_space=pltpu.SEMAPHORE)
_ANY_SPEC = pl.BlockSpec(memory_space=pl.ANY)
_DATAFLOW = pltpu.SideEffectType.DATAFLOW_SIDE_EFFECTING
N_PEERS = N_DEV - 1


def _peer(r):
    x, y, c = _my_place()
    return ((1 - x) if r & 4 else x, (1 - y) if r & 2 else y, (1 - c) if r & 1 else c)


def _index_of(place):
    return 4 * place[0] + 2 * place[1] + place[2]


def _slot(ref, axis, idx, size):
    if axis == 0:
        return ref.at[pl.ds(idx * size, size), :]
    return ref.at[:, pl.ds(idx * size, size)]


def _cast_place(w3, layer, axis, me_arr, *, name):
    Ks, Ns = w3.shape[1], w3.shape[2]
    tr = _pick(Ks, 256, 16)
    nblk = Ks // tr
    full = (Ks * N_DEV, Ns) if axis == 0 else (Ks, Ns * N_DEV)
    if axis == 0:
        out_map = lambda i, me: (me[0] * nblk + i, 0)
    else:
        out_map = lambda i, me: (i, me[0])

    def body(me_ref, w_ref, o_ref):
        o_ref[...] = w_ref[...].astype(BF16)

    return pl.pallas_call(
        body, name=name, out_shape=jax.ShapeDtypeStruct(full, BF16),
        grid_spec=pltpu.PrefetchScalarGridSpec(
            num_scalar_prefetch=1, grid=(nblk,),
            in_specs=[pl.BlockSpec((None, tr, Ns), lambda i, me: (layer, i, 0))],
            out_specs=pl.BlockSpec((tr, Ns), out_map)),
        compiler_params=_cparams("parallel"),
    )(me_arr, w3)


def _gather_start(lands, axes, after, *, name):
    nt = len(lands)
    sizes = [l.shape[a] // N_DEV for l, a in zip(lands, axes)]

    def body(*refs):
        ins, send_sems, recv_sems, token = refs[:nt], refs[nt + 1], refs[nt + 2], refs[-1]
        my_idx = _index_of(_my_place())
        for t in range(nt):
            mine = _slot(ins[t], axes[t], my_idx, sizes[t])
            for r in range(1, N_DEV):
                k = t * N_PEERS + r - 1
                pltpu.make_async_remote_copy(src_ref=mine, dst_ref=mine, send_sem=send_sems.at[k],
                                             recv_sem=recv_sems.at[k], device_id=_peer(r), device_id_type=MESH).start()
        token[...] = jnp.zeros_like(token)

    res = pl.pallas_call(
        body, name=name,
        out_shape=(pltpu.SemaphoreType.DMA((nt * N_PEERS,)), pltpu.SemaphoreType.DMA((nt * N_PEERS,)),
                   *[pltpu.HBM(l.shape, l.dtype) for l in lands], jax.ShapeDtypeStruct((8, LANES), F32)),
        in_specs=[_HBM_SPEC] * nt + [_ANY_SPEC],
        out_specs=(_SEM_SPEC, _SEM_SPEC, *[_HBM_SPEC] * nt, pl.BlockSpec(memory_space=pltpu.VMEM)),
        input_output_aliases={t: 2 + t for t in range(nt)},
        compiler_params=pltpu.CompilerParams(has_side_effects=_DATAFLOW),
    )(*[pltpu.with_memory_space_constraint(l, pltpu.HBM) for l in lands], after)
    return res[0], res[1], list(res[2:2 + nt]), res[-1]


def _gather_wait(send_sems, recv_sems, lands, axes, after, *, name):
    nt = len(lands)
    sizes = [l.shape[a] // N_DEV for l, a in zip(lands, axes)]

    def body(*refs):
        ins, send_sems, recv_sems = refs[:nt], refs[nt], refs[nt + 1]
        my_idx = _index_of(_my_place())
        for t in range(nt):
            for r in range(1, N_DEV):
                p = _peer(r)
                k = t * N_PEERS + r - 1
                cp = pltpu.make_async_remote_copy(
                    src_ref=_slot(ins[t], axes[t], my_idx, sizes[t]), dst_ref=_slot(ins[t], axes[t], _index_of(p), sizes[t]),
                    send_sem=send_sems.at[k], recv_sem=recv_sems.at[k], device_id=p, device_id_type=MESH)
                cp.wait_send()
                cp.wait_recv()

    res = pl.pallas_call(
        body, name=name, out_shape=[pltpu.HBM(l.shape, l.dtype) for l in lands],
        in_specs=[_HBM_SPEC] * nt + [_SEM_SPEC, _SEM_SPEC, _ANY_SPEC], out_specs=[_HBM_SPEC] * nt,
        input_output_aliases={t: t for t in range(nt)},
        compiler_params=pltpu.CompilerParams(has_side_effects=_DATAFLOW),
    )(*lands, send_sems, recv_sems, after)
    return list(res)


def _scatter_start(dw, axis, *, name):
    size = dw.shape[axis] // N_DEV
    land_shape = (N_PEERS, size, dw.shape[1]) if axis == 0 else (N_PEERS, dw.shape[0], size)

    def body(dw_ref, land_ref, send_sems, recv_sems, dw_thru, land_thru, token):
        for r in range(1, N_DEV):
            p = _peer(r)
            pltpu.make_async_remote_copy(src_ref=_slot(dw_ref, axis, _index_of(p), size), dst_ref=land_ref.at[r - 1],
                                         send_sem=send_sems.at[r - 1], recv_sem=recv_sems.at[r - 1], device_id=p,
                                         device_id_type=MESH).start()
        token[...] = jnp.zeros_like(token)

    land = pltpu.with_memory_space_constraint(lax.empty(land_shape, dw.dtype), pltpu.HBM)
    return pl.pallas_call(
        body, name=name,
        out_shape=(pltpu.SemaphoreType.DMA((N_PEERS,)), pltpu.SemaphoreType.DMA((N_PEERS,)),
                   pltpu.HBM(dw.shape, dw.dtype), pltpu.HBM(land_shape, dw.dtype), jax.ShapeDtypeStruct((8, LANES), F32)),
        in_specs=[_HBM_SPEC, _HBM_SPEC],
        out_specs=(_SEM_SPEC, _SEM_SPEC, _HBM_SPEC, _HBM_SPEC, pl.BlockSpec(memory_space=pltpu.VMEM)),
        input_output_aliases={0: 2, 1: 3},
        compiler_params=pltpu.CompilerParams(has_side_effects=_DATAFLOW),
    )(pltpu.with_memory_space_constraint(dw, pltpu.HBM), land)


def _scatter_wait(send_sems, recv_sems, dw, land, axis, after, *, name):
    size = dw.shape[axis] // N_DEV

    def body(dw_ref, land_ref, send_sems, recv_sems, after_ref, dw_thru, land_thru):
        for r in range(1, N_DEV):
            p = _peer(r)
            cp = pltpu.make_async_remote_copy(src_ref=_slot(dw_ref, axis, _index_of(p), size), dst_ref=land_ref.at[r - 1],
                                              send_sem=send_sems.at[r - 1], recv_sem=recv_sems.at[r - 1], device_id=p,
                                              device_id_type=MESH)
            cp.wait_send()
            cp.wait_recv()

    return pl.pallas_call(
        body, name=name, out_shape=(pltpu.HBM(dw.shape, dw.dtype), pltpu.HBM(land.shape, land.dtype)),
        in_specs=[_HBM_SPEC, _HBM_SPEC, _SEM_SPEC, _SEM_SPEC, _ANY_SPEC], out_specs=(_HBM_SPEC, _HBM_SPEC),
        input_output_aliases={0: 0, 1: 1},
        compiler_params=pltpu.CompilerParams(has_side_effects=_DATAFLOW),
    )(dw, land, send_sems, recv_sems, after)


def _adamw_math(w, g, m, v):
    m = ADAM_B1 * m + (1.0 - ADAM_B1) * g
    v = ADAM_B2 * v + (1.0 - ADAM_B2) * (g * g)
    m_hat = m / (1.0 - ADAM_B1 ** ADAM_STEP)
    v_hat = v / (1.0 - ADAM_B2 ** ADAM_STEP)
    delta = -ADAM_LR * (m_hat / (jnp.sqrt(v_hat) + ADAM_EPS) + ADAM_WD * w)
    return delta, m, v


def _adamw_sharded(w, m, v, layer, dw, land, axis, me_arr, prev, *, name):
    nl, Ks, Ns = w.shape
    tr = _pick(Ks, 128, 16)
    nblk = Ks // tr
    if axis == 0:
        own_map = lambda i, me: (me[0] * nblk + i, 0)
    else:
        own_map = lambda i, me: (i, me[0])
    wspec = pl.BlockSpec((None, tr, Ns), lambda i, me: (layer, i, 0))
    n_prev = 0 if prev is None else 4

    def body(me_ref, w_ref, m_ref, v_ref, own_ref, r_ref, *rest):
        g_ref, d_ref, nm_ref, nv_ref = rest[n_prev:]
        g = own_ref[...].astype(F32)
        for r in range(N_PEERS):
            g = g + r_ref[r].astype(F32)
        delta, nm, nv = _adamw_math(w_ref[...], g, m_ref[...], v_ref[...])
        g_ref[...], d_ref[...], nm_ref[...], nv_ref[...] = g, delta, nm, nv

    return pl.pallas_call(
        body, name=name, out_shape=[jax.ShapeDtypeStruct((nl, Ks, Ns), F32)] * 4,
        grid_spec=pltpu.PrefetchScalarGridSpec(
            num_scalar_prefetch=1, grid=(nblk,),
            in_specs=[wspec, wspec, wspec, pl.BlockSpec((tr, Ns), own_map),
                      pl.BlockSpec((N_PEERS, tr, Ns), lambda i, me: (0, i, 0))] + [_ANY_SPEC] * n_prev,
            out_specs=[wspec] * 4),
        input_output_aliases={6 + k: k for k in range(n_prev)},
        compiler_params=_cparams("parallel"),
    )(me_arr, w, m, v, dw, land, *(prev or []))


def _adamw_flat(w, g, m, v, *, name):
    def body(w_ref, g_ref, m_ref, v_ref, d_ref, nm_ref, nv_ref):
        d_ref[...], nm_ref[...], nv_ref[...] = _adamw_math(w_ref[...], g_ref[...], m_ref[...], v_ref[...])

    spec = pl.BlockSpec(memory_space=pltpu.VMEM)
    return pl.pallas_call(body, name=name, in_specs=[spec] * 4, out_specs=[spec] * 3,
                          out_shape=[jax.ShapeDtypeStruct(w.shape, F32)] * 3)(w, g, m, v)


def _sum_devices(a, *, name):
    def body(a_ref, o_ref):
        s = a_ref[0]
        for d in range(1, N_DEV):
            s = s + a_ref[d]
        o_ref[...] = s

    spec = pl.BlockSpec(memory_space=pltpu.VMEM)
    return pl.pallas_call(body, name=name, in_specs=[spec], out_specs=spec,
                          out_shape=jax.ShapeDtypeStruct(a.shape[1:], F32))(a)


def _ada_mods(c16, ada_w, ada_b_cols, *, name):
    nl, D, cols = ada_w.shape
    bn = _pick(cols, 512)

    def body(c_ref, w_ref, b_ref, o_ref):
        cond = _silu(c_ref[...]).astype(BF16)
        o_ref[...] = _dot(cond, w_ref[...].astype(BF16)) + b_ref[...]

    return pl.pallas_call(
        body, name=name, grid=(nl, cols // bn),
        in_specs=[pl.BlockSpec((16, D), lambda l, j: (0, 0)), pl.BlockSpec((None, D, bn), lambda l, j: (l, 0, j)),
                  pl.BlockSpec((None, 1, bn), lambda l, j: (l, 0, j))],
        out_specs=pl.BlockSpec((None, 16, bn), lambda l, j: (l, 0, j)),
        out_shape=jax.ShapeDtypeStruct((nl, 16, cols), F32),
        compiler_params=_cparams("parallel", "parallel"),
    )(c16, ada_w, ada_b_cols)


def _ada_bwd(cond_t, dmod, w, m, v, *, name):
    nl, D, cols = w.shape
    tr = _pick(D, 256, 8)

    def body(ct_ref, dm_ref, w_ref, m_ref, v_ref, g_ref, d_ref, nm_ref, nv_ref, dc_ref):
        ct, dm, wt = ct_ref[...], dm_ref[...], w_ref[...]
        g = ct[:, 0:1] * dm[0:1, :]
        for r in range(1, N_DEV + 1):
            g = g + ct[:, r:r + 1] * dm[r:r + 1, :]
        delta, nm, nv = _adamw_math(wt, g, m_ref[...], v_ref[...])
        g_ref[...], d_ref[...], nm_ref[...], nv_ref[...] = g, delta, nm, nv
        dc_ref[...] = jnp.sum(wt * dm[N_DEV:N_DEV + 1, :], axis=-1, keepdims=True)

    wspec = pl.BlockSpec((None, tr, cols), lambda l, i: (l, i, 0))
    return pl.pallas_call(
        body, name=name, grid=(nl, D // tr),
        in_specs=[pl.BlockSpec((tr, 16), lambda l, i: (i, 0)), pl.BlockSpec((None, 16, cols), lambda l, i: (l, 0, 0)),
                  wspec, wspec, wspec],
        out_specs=[wspec] * 4 + [pl.BlockSpec((None, tr, 1), lambda l, i: (l, i, 0))],
        out_shape=[jax.ShapeDtypeStruct((nl, D, cols), F32)] * 4 + [jax.ShapeDtypeStruct((nl, D, 1), F32)],
        compiler_params=_cparams("parallel", "parallel"),
    )(cond_t, dmod, w, m, v)


def _rope_tables(L, CTX):
    def angles(pos, dim):
        inv_freq = ROPE_BASE ** (-jnp.arange(0, dim, 2, dtype=F32) / dim)
        return pos.astype(F32)[:, None] * inv_freq[None, :]

    def pad(cos, sin):
        return (jnp.concatenate([cos, jnp.ones((CTX, LANES), F32)], 0),
                jnp.concatenate([sin, jnp.zeros((CTX, LANES), F32)], 0))

    ret = angles(jnp.arange(L), 2 * LANES)
    ret_cs = pad(jnp.cos(ret), jnp.sin(ret))
    rows = angles(jnp.arange(L) // GRID_W, ATT_HEAD_DIM // 2)
    cols = angles(jnp.arange(L) % GRID_W, ATT_HEAD_DIM // 2)
    cos = jnp.concatenate([jnp.cos(rows)] * 2 + [jnp.cos(cols)] * 2, axis=1)
    sin = jnp.concatenate([-jnp.sin(rows), jnp.sin(rows), -jnp.sin(cols), jnp.sin(cols)], axis=1)
    return ret_cs, pad(cos, sin)


def kernel(x, c, ctx, c_ctx, ada_w, ada_b, norm_mix_g, norm_mlp_g, mlp_w1, mlp_w2, ret_w_in, ret_w_out, ret_decay_fwd, ret_decay_bwd, attn_w_in, attn_w_out, attn_sink, final_norm_g, loss_target, m_c_ctx, m_ada_w, m_ada_b, m_norm_mix_g, m_norm_mlp_g, m_mlp_w1, m_mlp_w2, m_ret_w_in, m_ret_w_out, m_ret_decay_fwd, m_ret_decay_bwd, m_attn_w_in, m_attn_w_out, m_attn_sink, m_final_norm_g, v_c_ctx, v_ada_w, v_ada_b, v_norm_mix_g, v_norm_mlp_g, v_mlp_w1, v_mlp_w2, v_ret_w_in, v_ret_w_out, v_ret_decay_fwd, v_ret_decay_bwd, v_attn_w_in, v_attn_w_out, v_attn_sink, v_final_norm_g):
    L, D = x.shape[1], x.shape[2]
    CTX = ctx.shape[1]
    T = L + CTX
    RH = ret_decay_fwd.shape[-1]
    assert D == RH * 2 * LANES and ada_w.shape[0] == 2 and ret_w_in.shape[0] == 1 and attn_w_in.shape[0] == 1
    Hq = attn_sink.shape[-1]
    Hkv = (attn_w_in.shape[-1] * N_DEV // ATT_HEAD_DIM - Hq) // 2
    G = Hq // Hkv
    FF = mlp_w1.shape[-1] * N_DEV
    Wq_r, Wv_r = RH * 2 * LANES, RH * 4 * LANES
    acols = ada_w.shape[-1]
    tr = _pick(CTX, 256, 8)
    tr_wide = _pick(CTX, 128, 8)
    bmT = T // 4 if (T % 64 == 0) else T
    bmL = L // 4 if (L % 64 == 0) else L
    x_idx, y_idx, c_idx = lax.axis_index("x"), lax.axis_index("y"), lax.axis_index("c")
    me = 4 * x_idx + 2 * y_idx + c_idx
    me_arr = jnp.reshape(me, (1,)).astype(jnp.int32)

    (rcos, rsin), (acos, asin) = _rope_tables(L, CTX)
    lg_f = jax.nn.log_sigmoid(ret_decay_fwd.astype(F32))
    lg_b = jax.nn.log_sigmoid(ret_decay_bwd.astype(F32))

    c_pad = jnp.concatenate([c.astype(F32), jnp.zeros((7, D), F32)], 0)
    c_all = _all_gather_small(c_pad, name="ag_c")[:, 0, :]
    c16 = jnp.concatenate([c_all, c_ctx[None, :], jnp.zeros((7, D), F32)], 0)
    ada_b_cols = lax.dynamic_slice_in_dim(ada_b, me * acols, acols, axis=1)[:, None, :]
    mods_shard = _ada_mods(c16, ada_w, ada_b_cols, name="ada_mods")
    mods_all = _all_gather_small(mods_shard.reshape(32, acols), name="ag_mods")

    wdefs = {"ret_in": (ret_w_in, 0, 1), "ret_out": (ret_w_out, 0, 0), "w1_0": (mlp_w1, 0, 1), "w2_0": (mlp_w2, 0, 0),
             "attn_in": (attn_w_in, 0, 1), "attn_out": (attn_w_out, 0, 0), "w1_1": (mlp_w1, 1, 1), "w2_1": (mlp_w2, 1, 0)}
    groups = [["ret_in"], ["ret_out", "w1_0", "w2_0"], ["attn_in", "attn_out", "w1_1", "w2_1"]]
    gathers, tok = [], mods_all
    for gi, keys in enumerate(groups):
        g_axes = [wdefs[k][2] for k in keys]
        placed = [_cast_place(*wdefs[k], me_arr, name=f"place_{k}") for k in keys]
        ssem, rsem, lands, tok = _gather_start(placed, g_axes, tok, name=f"ag_start{gi}")
        gathers.append((ssem, rsem, lands, g_axes))

    mods_all = (mods_all + tok[0, 0]).reshape(N_DEV, 2, 16, acols).transpose(1, 2, 0, 3).reshape(2, 16, 6, D)
    mod_lat = lax.dynamic_index_in_dim(mods_all, me, axis=1, keepdims=False)
    mod_ctx = mods_all[:, N_DEV]

    def pack(i, ks, kc):
        return jnp.stack([mod_lat[i, ks], mod_lat[i, kc], mod_ctx[i, ks], mod_ctx[i, kc]], 0)

    def gates(i, k):
        return jnp.stack([mod_lat[i, k], mod_ctx[i, k]], 0)

    def gate_epilogue(gl, gc, x_rows_lat_only):
        def epi(acc, i, j, xt, gv):
            if x_rows_lat_only:
                gate = gv[0:1, :]
            else:
                row = i * acc.shape[0] + lax.broadcasted_iota(jnp.int32, (acc.shape[0], 1), 0)
                gate = jnp.where(row >= L, gv[1:2, :], gv[0:1, :])
            return xt + gate * acc, acc
        return epi

    w1, w2 = {}, {}

    mmT = dict(M=T, bm=bmT)
    mmL = dict(M=L, bm=bmL)

    def bn_of(n, off=0):
        b = 4 * LANES
        while n % b or off % b:
            b -= LANES
        return b

    X0 = jnp.concatenate([x[0], ctx[0]], axis=0)
    g_mix0, g_mlp0 = norm_mix_g[0:1], norm_mlp_g[0:1]
    g_mix1, g_mlp1 = norm_mix_g[1:2], norm_mlp_g[1:2]
    a0 = _normmod(X0, g_mix0, pack(0, 0, 1), R=T, L=L, tr=tr, name="normmod_mix0")
    (wr_in,) = _gather_wait(*gathers[0][:3], gathers[0][3], a0, name="ag_wait0")

    bn_qk = _pick(2 * Wq_r, 512, 2 * LANES)
    nq_blocks = Wq_r // bn_qk
    kscale = float(2 * LANES) ** -0.5

    def rope_epi(acc, i, j, cos, sin):
        parts = []
        for h in range(acc.shape[1] // (2 * LANES)):
            x1 = acc[:, h * 2 * LANES:h * 2 * LANES + LANES]
            x2 = acc[:, h * 2 * LANES + LANES:(h + 1) * 2 * LANES]
            parts += [x1 * cos - x2 * sin, x2 * cos + x1 * sin]
        return (jnp.concatenate(parts, axis=1) * jnp.where(j < nq_blocks, 1.0, kscale),)

    def row_tile(arr, bm):
        return (arr, (bm, LANES), lambda i, j: (i, 0))

    (qk0,) = _mm(a0, wr_in, "nn", [BF16], N=2 * Wq_r, K=D, bn=bn_qk, bk=D, name="ret_qk", epilogue=rope_epi,
                 extras=[row_tile(rcos, bmT), row_tile(rsin, bmT)], **mmT)
    bn_vg = bn_of(2 * Wv_r, 2 * Wq_r)
    (vg0,) = _mm(a0, wr_in, "nn", [BF16], N=2 * Wv_r, K=D, bn=bn_vg, bk=D, name="ret_vg", b_col0=2 * Wq_r, **mmT)

    of, st_f = _ret_fwd(qk0, vg0, lg_f, T=T, L=L, H=RH, rev=False, name="ret_scan_f")
    ob, st_b = _ret_fwd(qk0, vg0, lg_b, T=T, L=L, H=RH, rev=True, name="ret_scan_b")
    z0 = _readout(of, ob, vg0, T=T, L=L, H=RH, tr=tr_wide, name="ret_readout")
    wr_out, w1[0], w2[0] = _gather_wait(*gathers[1][:3], gathers[1][3], z0, name="ag_wait1")

    bnD = _pick(D, 512)

    def xtile(arr, bm):
        return (arr, (bm, bnD), lambda i, j: (i, j))

    def gtile(gv):
        return (gv, (2, bnD), lambda i, j: (0, j))

    bk_v = _pick(Wv_r, 2048)
    X1, ro0 = _mm(z0, wr_out, "nn", [F32, BF16], N=D, K=Wv_r, bn=bnD, bk=bk_v, name="ret_out",
                  epilogue=gate_epilogue(None, None, False), extras=[xtile(X0, bmT), gtile(gates(0, 2))], **mmT)

    def mlp_fwd(Xin, i, g_mlp, rows, name):
        a = _normmod(Xin, g_mlp, pack(i, 3, 4), R=rows["M"], L=L, tr=tr, name=f"normmod_mlp{name}")

        def relu2(acc, i_, j_):
            u = jnp.maximum(acc, 0.0)
            return u, u * u

        bnF = _pick(FF, 512)
        u, r = _mm(a, w1[i], "nn", [BF16, BF16], N=FF, K=D, bn=bnF, bk=D, name=f"mlp_up{name}", epilogue=relu2, **rows)
        bkF = _pick(FF, 2048)
        Xout, mo = _mm(r, w2[i], "nn", [F32, BF16], N=D, K=FF, bn=bnD, bk=bkF, name=f"mlp_down{name}",
                       epilogue=gate_epilogue(None, None, rows["M"] == L),
                       extras=[xtile(Xin, rows["bm"]), gtile(gates(i, 5))], **rows)
        return a, u, r, Xout, mo

    a1, u0, r0, X2, mo0 = mlp_fwd(X1, 0, g_mlp0, mmT, "0")

    a2 = _normmod(X2, g_mix1, pack(1, 0, 1), R=T, L=L, tr=tr, name="normmod_mix1")
    wa_in, wa_out, w1[1], w2[1] = _gather_wait(*gathers[2][:3], gathers[2][3], a2, name="ag_wait2")
    Wq_a, Wk_a = Hq * LANES, Hkv * LANES

    def arope_epi(acc, i, j, cos, sin):
        heads = acc.shape[1] // LANES
        return (acc * jnp.tile(cos, (1, heads)) + _swap32(acc) * jnp.tile(sin, (1, heads)),)

    bn_q = _pick(Wq_a, 512)
    (q1,) = _mm(a2, wa_in, "nn", [BF16], N=Wq_a, K=D, bn=bn_q, bk=D, name="attn_q", epilogue=arope_epi,
                extras=[row_tile(acos, bmL), row_tile(asin, bmL)], **mmL)
    bn_k = bn_of(Wk_a, Wq_a)
    (k1,) = _mm(a2, wa_in, "nn", [BF16], N=Wk_a, K=D, bn=bn_k, bk=D, name="attn_k", b_col0=Wq_a, epilogue=arope_epi,
                extras=[row_tile(acos, bmT), row_tile(asin, bmT)], **mmT)
    bn_v = bn_of(Wk_a, Wq_a + Wk_a)
    (v1,) = _mm(a2, wa_in, "nn", [BF16], N=Wk_a, K=D, bn=bn_v, bk=D, name="attn_v", b_col0=Wq_a + Wk_a, **mmT)
    o1, lse = _attn_fwd(q1, k1, v1, attn_sink, L=L, CTX=CTX, Hkv=Hkv, G=G, name="attn_fwd")
    X3, ao = _mm(o1, wa_out, "nn", [F32, BF16], N=D, K=Wq_a, bn=bnD, bk=_pick(Wq_a, 2048), name="attn_out",
                 epilogue=gate_epilogue(None, None, True), extras=[xtile(X2, bmL), gtile(gates(1, 2))], **mmL)
    a3, u1, r1, X4, mo1 = mlp_fwd(X3, 1, g_mlp1, mmL, "1")

    dX4, dmo1, acc_head = _loss_head(X4, loss_target[0], mo1, final_norm_g[None, :], gates(1, 5)[0:1], L=L, tr=tr,
                                     name="loss_head")
    loss_part = jnp.sum(acc_head[0, 0])
    d_gf = acc_head[0, 1]
    zeros_d = jnp.zeros((D,), F32)
    dmod_lat = [[zeros_d] * 6, [zeros_d] * 6]
    dmod_ctx = [[zeros_d] * 6, [zeros_d] * 6]
    dmod_lat[1][5] = acc_head[0, 2]

    bn_dw = 2048

    def mlp_bwd(dmo, a, u, r, i, rows, name):
        Mr = rows["M"]
        bkr = rows["bm"]

        def times_2u(acc, i_, j_, ut):
            return (acc * (2.0 * ut.astype(F32)),)

        bnF = _pick(FF, 512)
        (dw2,) = _mm(r, dmo, "tn", [BF16], M=FF, N=D, K=Mr, bm=_pick(FF, 1024), bn=_pick(D, bn_dw), bk=bkr,
                     name=f"mlp_down_dw{name}")
        tok_ = send_grad(f"w2_{i}", dw2, 0)
        (dh,) = _mm(dmo, w2[i], "nt", [BF16], N=FF, K=D, bn=bnF, bk=D, name=f"mlp_down_dx{name}", epilogue=times_2u,
                    extras=[(u, (rows["bm"], bnF), lambda i_, j_: (i_, j_))], dep=tok_, **rows)
        (dw1,) = _mm(a, dh, "tn", [BF16], M=D, N=FF, K=Mr, bm=_pick(D, 1024), bn=_pick(FF, bn_dw), bk=bkr,
                     name=f"mlp_up_dw{name}")
        tok_ = send_grad(f"w1_{i}", dw1, 1)
        (da,) = _mm(dh, w1[i], "nt", [F32], N=D, K=FF, bn=bnD, bk=_pick(FF, 2048), name=f"mlp_up_dx{name}", dep=tok_,
                    **rows)
        return da

    pending = []

    def send_grad(key, dw, axis):
        ssem, rsem, dw_thru, land, tok_ = _scatter_start(dw, axis, name=f"rs_start_{key}")
        pending.append((key, axis, ssem, rsem, dw_thru, land))
        return tok_

    da3 = mlp_bwd(dmo1, a3, u1, r1, 1, mmL, "1")
    dX3, dao, acc = _normmod_bwd(X3, da3, dX4, False, g_mlp1, pack(1, 3, 4), (ao, gates(1, 2)), R=L, L=L, tr=tr,
                                 name="normmod_mlp1_bwd")
    dmod_lat[1][3], dmod_lat[1][4], d_gmlp1, dmod_lat[1][2] = acc[0, 0], acc[0, 1], acc[0, 2], acc[0, 3]

    (dwa_out,) = _mm(o1, dao, "tn", [BF16], M=Wq_a, N=D, K=L, bm=_pick(Wq_a, 1024), bn=_pick(D, bn_dw), bk=bmL,
                     name="attn_out_dw")
    tok = send_grad("attn_out", dwa_out, 0)
    (do1,) = _mm(dao, wa_out, "nt", [BF16], N=Wq_a, K=D, bn=_pick(Wq_a, 512), bk=D, name="attn_out_dx", dep=tok, **mmL)
    dq1, delta1, dkx, dvx, dsink_acc = _attn_bwd_q(q1, k1, v1, do1, lse, attn_sink, L=L, CTX=CTX, Hkv=Hkv, G=G,
                                                   name="attn_bwd_q")
    dk1, dv1 = _attn_bwd_kv(q1, k1, v1, do1, lse, delta1, L=L, Hkv=Hkv, G=G, name="attn_bwd_kv")
    dp1 = _attn_assemble(dq1, dk1, dv1, dkx, dvx, acos, asin, T=T, L=L, CTX=CTX, Hq=Hq, Hkv=Hkv, tr=tr_wide,
                         name="attn_assemble")
    Wa_in = Wq_a + 2 * Wk_a
    (dwa_in,) = _mm(a2, dp1, "tn", [BF16], M=D, N=Wa_in, K=T, bm=_pick(D, 1024), bn=_pick(Wa_in, bn_dw), bk=bmT,
                    name="attn_in_dw")
    tok = send_grad("attn_in", dwa_in, 1)
    (da2,) = _mm(dp1, wa_in, "nt", [F32], N=D, K=Wa_in, bn=bnD, bk=_pick(Wa_in, 2048), name="attn_in_dx", dep=tok,
                 **mmT)
    dX2, dmo0, acc = _normmod_bwd(X2, da2, dX3, True, g_mix1, pack(1, 0, 1), (mo0, gates(0, 5)), R=T, L=L, tr=tr,
                                  name="normmod_mix1_bwd")
    dmod_lat[1][0], dmod_lat[1][1], d_gmix1, dmod_lat[0][5] = acc[0, 0], acc[0, 1], acc[0, 2] + acc[1, 2], acc[0, 3]
    dmod_ctx[1][0], dmod_ctx[1][1], dmod_ctx[0][5] = acc[1, 0], acc[1, 1], acc[1, 3]

    da1 = mlp_bwd(dmo0, a1, u0, r0, 0, mmT, "0")
    dX1, dro0, acc = _normmod_bwd(X1, da1, dX2, False, g_mlp0, pack(0, 3, 4), (ro0, gates(0, 2)), R=T, L=L, tr=tr,
                                  name="normmod_mlp0_bwd")
    dmod_lat[0][3], dmod_lat[0][4], d_gmlp0, dmod_lat[0][2] = acc[0, 0], acc[0, 1], acc[0, 2] + acc[1, 2], acc[0, 3]
    dmod_ctx[0][3], dmod_ctx[0][4], dmod_ctx[0][2] = acc[1, 0], acc[1, 1], acc[1, 3]

    (dwr_out,) = _mm(z0, dro0, "tn", [BF16], M=Wv_r, N=D, K=T, bm=_pick(Wv_r, 1024), bn=_pick(D, bn_dw), bk=bmT,
                     name="ret_out_dw")
    tok = send_grad("ret_out", dwr_out, 0)
    (dz0,) = _mm(dro0, wr_out, "nt", [BF16], N=Wv_r, K=D, bn=_pick(Wv_r, 512), bk=D, name="ret_out_dx", dep=tok, **mmT)
    do0, dg0 = _readout_bwd(dz0, of, ob, vg0, T=T, L=L, H=RH, tr=tr_wide, name="ret_readout_bwd")
    dq_f, dk_f, dv_f, dlg_f = _ret_bwd(qk0, vg0, do0, st_f, lg_f, T=T, L=L, H=RH, rev=False, name="ret_scan_f_bwd")
    dq_b, dk_b, dv_b, dlg_b = _ret_bwd(qk0, vg0, do0, st_b, lg_b, T=T, L=L, H=RH, rev=True, name="ret_scan_b_bwd")
    dp0 = _ret_assemble(dq_f, dq_b, dk_f, dk_b, dv_f, dv_b, dg0, rcos, rsin, T=T, L=L, H=RH, tr=tr_wide,
                        name="ret_assemble")
    Wr_in = 2 * Wq_r + 2 * Wv_r
    (dwr_in,) = _mm(a0, dp0, "tn", [BF16], M=D, N=Wr_in, K=T, bm=_pick(D, 1024), bn=_pick(Wr_in, bn_dw), bk=bmT,
                    name="ret_in_dw")
    tok = send_grad("ret_in", dwr_in, 1)
    (da0,) = _mm(dp0, wr_in, "nt", [F32], N=D, K=Wr_in, bn=bnD, bk=_pick(Wr_in, 2048), name="ret_in_dx", dep=tok, **mmT)
    dX0, acc = _normmod_bwd(X0, da0, dX1, False, g_mix0, pack(0, 0, 1), None, R=T, L=L, tr=tr, name="normmod_mix0_bwd")
    dmod_lat[0][0], dmod_lat[0][1], d_gmix0 = acc[0, 0], acc[0, 1], acc[0, 2] + acc[1, 2]
    dmod_ctx[0][0], dmod_ctx[0][1] = acc[1, 0], acc[1, 1]
    grad_x = dX0[:L][None]

    wmv = {"ret_in": (ret_w_in, m_ret_w_in, v_ret_w_in, 0, "ret_w_in"),
           "ret_out": (ret_w_out, m_ret_w_out, v_ret_w_out, 0, "ret_w_out"),
           "attn_in": (attn_w_in, m_attn_w_in, v_attn_w_in, 0, "attn_w_in"),
           "attn_out": (attn_w_out, m_attn_w_out, v_attn_w_out, 0, "attn_w_out"),
           "w1_0": (mlp_w1, m_mlp_w1, v_mlp_w1, 0, "mlp_w1"), "w1_1": (mlp_w1, m_mlp_w1, v_mlp_w1, 1, "mlp_w1"),
           "w2_0": (mlp_w2, m_mlp_w2, v_mlp_w2, 0, "mlp_w2"), "w2_1": (mlp_w2, m_mlp_w2, v_mlp_w2, 1, "mlp_w2")}
    big = {}

    def finish_grad(entry, after):
        key, axis, ssem, rsem, dw_thru, land = entry
        dw_done, land_done = _scatter_wait(ssem, rsem, dw_thru, land, axis, after, name=f"rs_wait_{key}")
        w_, m_, v_, layer, out_name = wmv[key]
        big[out_name] = _adamw_sharded(w_, m_, v_, layer, dw_done, land_done, axis, me_arr, big.get(out_name),
                                       name=f"adamw_{key}")
        return big[out_name][0]

    after = dX0
    for entry in pending[:-1]:
        after = finish_grad(entry, after)

    misc = jnp.zeros((D,), F32)
    misc = misc.at[0:RH].set(dlg_f[:, 0, 0]).at[RH:2 * RH].set(dlg_b[:, 0, 0])
    misc = misc.at[2 * RH:2 * RH + Hq].set(dsink_acc[:, :G, 0].reshape(Hq)).at[2 * RH + Hq].set(loss_part)
    rows = ([dmod_lat[i][k] for i in range(2) for k in range(6)] + [dmod_ctx[i][k] for i in range(2) for k in range(6)]
            + [d_gmix0, d_gmix1, d_gmlp0, d_gmlp1, d_gf, misc, zeros_d, zeros_d])
    part = jnp.stack(rows, 0)
    part_all = _all_gather_small(part, name="ag_small_grads")
    tot = _sum_devices(part_all, name="sum_small_grads")

    grad_ada_b = (tot[0:12] + tot[12:24]).reshape(2, 6 * D)
    grad_norm_mix_g, grad_norm_mlp_g, grad_final_norm_g = tot[24:26], tot[26:28], tot[28]
    grad_ret_decay_fwd = (tot[29, 0:RH] * jax.nn.sigmoid(-ret_decay_fwd[0]))[None]
    grad_ret_decay_bwd = (tot[29, RH:2 * RH] * jax.nn.sigmoid(-ret_decay_bwd[0]))[None]
    grad_attn_sink = tot[29, 2 * RH:2 * RH + Hq][None]
    loss = tot[29, 2 * RH + Hq]

    dlat_cols = lax.dynamic_slice_in_dim(part_all[:, 0:12].reshape(N_DEV, 2, 6 * D), me * acols, acols, axis=2)
    dctx_cols = lax.dynamic_slice_in_dim(tot[12:24].reshape(2, 6 * D), me * acols, acols, axis=1)
    dmod16 = jnp.concatenate([dlat_cols.transpose(1, 0, 2), dctx_cols[:, None, :], jnp.zeros((2, 7, acols), F32)], 1)
    cond_t = _silu(c16).T
    g_ada, d_ada, nm_ada, nv_ada, dcond_part = _ada_bwd(cond_t, dmod16, ada_w, m_ada_w, v_ada_w, name="ada_bwd")
    dcond = (dcond_part[0, :, 0] + dcond_part[1, :, 0]).reshape(D // LANES, LANES)
    pad_rows = -(D // LANES) % 8
    dcond_pad = jnp.concatenate([dcond, jnp.zeros((pad_rows, LANES), F32)], 0) if pad_rows else dcond
    dcond_all = _all_gather_small(dcond_pad, name="ag_dcond")
    dcond_tot = _sum_devices(dcond_all, name="sum_dcond")[:D // LANES].reshape(D)
    sg = jax.nn.sigmoid(c_ctx)
    grad_c_ctx = dcond_tot * (sg * (1.0 + c_ctx * (1.0 - sg)))

    small_w = [c_ctx, ada_b, norm_mix_g, norm_mlp_g, ret_decay_fwd, ret_decay_bwd, attn_sink, final_norm_g]
    small_g = [grad_c_ctx, grad_ada_b, grad_norm_mix_g, grad_norm_mlp_g, grad_ret_decay_fwd, grad_ret_decay_bwd,
               grad_attn_sink, grad_final_norm_g]
    small_m = [m_c_ctx, m_ada_b, m_norm_mix_g, m_norm_mlp_g, m_ret_decay_fwd, m_ret_decay_bwd, m_attn_sink,
               m_final_norm_g]
    small_v = [v_c_ctx, v_ada_b, v_norm_mix_g, v_norm_mlp_g, v_ret_decay_fwd, v_ret_decay_bwd, v_attn_sink,
               v_final_norm_g]
    sizes = [w_.size for w_ in small_w]
    total = sum(-(-s // LANES) * LANES for s in sizes)
    total_pad = -(-total // (8 * LANES)) * 8 * LANES

    def flat_pack(ts, fill):
        pieces = []
        for t_ in ts:
            f = t_.reshape(-1).astype(F32)
            pad = -f.size % LANES
            pieces.append(jnp.concatenate([f, jnp.full((pad,), fill, F32)]) if pad else f)
        pieces.append(jnp.full((total_pad - total,), fill, F32))
        return jnp.concatenate(pieces).reshape(total_pad // LANES, LANES)

    d_s, nm_s, nv_s = _adamw_flat(flat_pack(small_w, 0.0), flat_pack(small_g, 0.0), flat_pack(small_m, 0.0),
                                  flat_pack(small_v, 1.0), name="adamw_small")

    def unpack(p):
        flat = p.reshape(-1)
        res, off = [], 0
        for w_, s in zip(small_w, sizes):
            res.append(flat[off:off + s].reshape(w_.shape))
            off += -(-s // LANES) * LANES
        return res

    finish_grad(pending[-1], d_s)
    d_small, nm_small, nv_small = unpack(d_s), unpack(nm_s), unpack(nv_s)
    small_names = ["c_ctx", "ada_b", "norm_mix_g", "norm_mlp_g", "ret_decay_fwd", "ret_decay_bwd", "attn_sink",
                   "final_norm_g"]
    sm = {n: (g_, d_, m_, v_) for n, g_, d_, m_, v_ in zip(small_names, small_g, d_small, nm_small, nv_small)}

    def out4(n):
        if n == "ada_w":
            return g_ada, d_ada, nm_ada, nv_ada
        if n in big:
            return tuple(big[n])
        return sm[n]

    order = ["c_ctx", "ada_w", "ada_b", "norm_mix_g", "norm_mlp_g", "mlp_w1", "mlp_w2", "ret_w_in", "ret_w_out",
             "ret_decay_fwd", "ret_decay_bwd", "attn_w_in", "attn_w_out", "attn_sink", "final_norm_g"]
    quads = [out4(n) for n in order]
    return (loss, grad_x, *[q_[0] for q_ in quads], *[q_[1] for q_ in quads], *[q_[2] for q_ in quads],
            *[q_[3] for q_ in quads])
```

```python
import functools

import jax
import jax.numpy as jnp
from jax import lax
from jax.experimental import pallas as pl
from jax.experimental.pallas import tpu as pltpu

F32 = jnp.float32
BF16 = jnp.bfloat16

N_DEV = 8
NORM_EPS = 1e-6
CHUNK = 128
ATT_HEAD_DIM = 128
GRID_W = 64
ROPE_BASE = 10000.0
NEG_INF = -1e30
ADAM_LR, ADAM_B1, ADAM_B2, ADAM_EPS, ADAM_WD, ADAM_STEP = 0.001, 0.9, 0.999, 1e-08, 0.01, 10

V7X_VMEM_LIMIT_BYTES = 56 * 1024 * 1024
LANES = 128
MESH = pl.DeviceIdType.MESH

_NN = (((1,), (0,)), ((), ()))
_NT = (((1,), (1,)), ((), ()))
_TN = (((0,), (0,)), ((), ()))


def _dot(a, b, dn=_NN):
    return lax.dot_general(a, b, dn, preferred_element_type=F32)


def _cparams(*sem):
    return pltpu.CompilerParams(dimension_semantics=sem, vmem_limit_bytes=V7X_VMEM_LIMIT_BYTES)


def _pick(n, pref, mult=LANES):
    if n <= pref:
        return n
    best = None
    for d in range(mult, pref + 1, mult):
        if n % d == 0:
            best = d
    assert best is not None, (n, pref)
    return best


def _silu(x):
    return x * jax.nn.sigmoid(x)


def _mm(a, b, mode, out_dtypes, *, M, N, K, bm, bn, bk, name, b_col0=0, epilogue=None, extras=(), dep=None):
    assert M % bm == 0 and N % bn == 0 and K % bk == 0 and b_col0 % bn == 0, (name, M, N, K, bm, bn, bk, b_col0)
    nk = K // bk
    c0 = b_col0 // bn
    if mode == "nn":
        a_spec = pl.BlockSpec((bm, bk), lambda i, j, k: (i, k))
        b_spec = pl.BlockSpec((bk, bn), lambda i, j, k: (k, j + c0))
    elif mode == "nt":
        a_spec = pl.BlockSpec((bm, bk), lambda i, j, k: (i, k))
        b_spec = pl.BlockSpec((bn, bk), lambda i, j, k: (j + c0, k))
    else:
        a_spec = pl.BlockSpec((bk, bm), lambda i, j, k: (k, i))
        b_spec = pl.BlockSpec((bk, bn), lambda i, j, k: (k, j + c0))
    dn = {"nn": _NN, "nt": _NT, "tn": _TN}[mode]
    e_specs = [pl.BlockSpec(bs, (lambda i, j, k, f=f: f(i, j))) for (_, bs, f) in extras]
    ne, no = len(extras), len(out_dtypes)
    nd = 0 if dep is None else 1

    def body(a_ref, b_ref, *rest):
        e_refs, o_refs = rest[:ne], rest[ne + nd:ne + nd + no]
        i, j, k = pl.program_id(0), pl.program_id(1), pl.program_id(2)

        def finish(acc):
            outs = (acc,) if epilogue is None else epilogue(acc, i, j, *[e[...] for e in e_refs])
            for o_ref, o in zip(o_refs, outs):
                o_ref[...] = o.astype(o_ref.dtype)

        p = _dot(a_ref[...], b_ref[...], dn)
        if nk == 1:
            finish(p)
        else:
            acc_ref = rest[-1]

            @pl.when(k == 0)
            def _():
                acc_ref[...] = p

            @pl.when(k > 0)
            def _():
                acc_ref[...] += p

            @pl.when(k == nk - 1)
            def _():
                finish(acc_ref[...])

    outs = pl.pallas_call(
        body, name=name, grid=(M // bm, N // bn, nk),
        in_specs=[a_spec, b_spec] + e_specs + [pl.BlockSpec(memory_space=pl.ANY)] * nd,
        out_specs=[pl.BlockSpec((bm, bn), lambda i, j, k: (i, j)) for _ in out_dtypes],
        out_shape=[jax.ShapeDtypeStruct((M, N), dt) for dt in out_dtypes],
        scratch_shapes=[pltpu.VMEM((bm, bn), F32)] if nk > 1 else [],
        compiler_params=_cparams("parallel", "parallel", "arbitrary"),
    )(a, b, *[e[0] for e in extras], *([dep] if nd else []))
    return outs


def _rowwise(body, rows, vecs, outs, n_acc, *, R, L, tr, name, acc_width=None):
    assert R % tr == 0 and L % tr == 0, (name, R, L, tr)
    nl = L // tr
    n_regions = 2 if R > L else 1
    n_rows, n_vecs, n_outs = len(rows), len(vecs), len(outs)
    acc_pad = -(-n_acc // 8) * 8 if n_acc else 0

    in_specs = []
    for (_, w, cb, lat_only) in rows:
        if lat_only:
            in_specs.append(pl.BlockSpec((tr, w), lambda i, cb=cb: (jnp.minimum(i, nl - 1), cb)))
        else:
            in_specs.append(pl.BlockSpec((tr, w), lambda i, cb=cb: (i, cb)))
    for v in vecs:
        in_specs.append(pl.BlockSpec(v.shape, lambda i, nd=v.ndim: (0,) * nd))
    out_specs = [pl.BlockSpec((tr, w), lambda i: (i, 0)) for (w, _) in outs]
    out_shape = [jax.ShapeDtypeStruct((R, w), dt) for (w, dt) in outs]
    if n_acc:
        out_specs.append(pl.BlockSpec((None, acc_pad, acc_width), lambda i: (jnp.where(i >= nl, 1, 0), 0, 0)))
        out_shape.append(jax.ShapeDtypeStruct((n_regions, acc_pad, acc_width), F32))

    def kern(*refs):
        i = pl.program_id(0)
        is_ctx = i >= nl
        ins = [r[...] for r in refs[:n_rows + n_vecs]]
        o_refs = refs[n_rows + n_vecs:]
        out_tiles, acc_rows = body(is_ctx, *ins)
        for o_ref, o in zip(o_refs[:n_outs], out_tiles):
            o_ref[...] = o.astype(o_ref.dtype)
        if n_acc:
            acc_ref = o_refs[n_outs]

            @pl.when((i == 0) | (i == nl))
            def _():
                acc_ref[...] = jnp.zeros_like(acc_ref)

            for r, row in enumerate(acc_rows):
                acc_ref[r:r + 1, :] += row

    res = pl.pallas_call(
        kern, name=name, grid=(R // tr,), in_specs=in_specs, out_specs=out_specs, out_shape=out_shape,
        compiler_params=_cparams("arbitrary"),
    )(*[r[0] for r in rows], *vecs)
    return res


def _colsum(x):
    return jnp.sum(x, axis=0, keepdims=True)


def _rms_stats(x):
    r = lax.rsqrt(jnp.mean(x * x, axis=-1, keepdims=True) + NORM_EPS)
    return x * r, r


def _sel(is_ctx, pk, lat_row, ctx_row):
    return jnp.where(is_ctx, pk[ctx_row:ctx_row + 1, :], pk[lat_row:lat_row + 1, :])


def _normmod(x, g, pk, *, R, L, tr, name):
    D = x.shape[-1]

    def body(is_ctx, xt, gv, pkv):
        xh, _ = _rms_stats(xt)
        sh, sc = _sel(is_ctx, pkv, 0, 2), _sel(is_ctx, pkv, 1, 3)
        return ((xh * gv) * (1.0 + sc) + sh,), ()

    return _rowwise(body, [(x, D, 0, False)], [g, pk], [(D, BF16)], 0, R=R, L=L, tr=tr, name=name)[0]


def _normmod_bwd(x_in, da, dx_out, dx_out_lat_only, g, pk, prev, *, R, L, tr, name):
    D = x_in.shape[-1]
    has_prev = prev is not None

    def body(is_ctx, *t):
        if has_prev:
            xt, dat, dxo, mp, gv, pkv, gates = t
        else:
            xt, dat, dxo, gv, pkv = t
        xh, r = _rms_stats(xt)
        sc = _sel(is_ctx, pkv, 1, 3)
        if dx_out_lat_only:
            dxo = jnp.where(is_ctx, 0.0, dxo)
        dn = dat * (1.0 + sc)
        w = dn * gv
        dxi = dxo + r * (w - xh * jnp.mean(w * xh, axis=-1, keepdims=True))
        accs = [_colsum(dat), _colsum(dat * (xh * gv)), _colsum(dn * xh)]
        outs = [dxi]
        if has_prev:
            gate = _sel(is_ctx, gates, 0, 1)
            outs.append(dxi * gate)
            accs.append(_colsum(dxi * mp.astype(F32)))
        return outs, accs

    rows = [(x_in, D, 0, False), (da, D, 0, False), (dx_out, D, 0, dx_out_lat_only)]
    vecs = [g, pk]
    outs = [(D, F32)]
    if has_prev:
        rows.append((prev[0], D, 0, False))
        vecs.append(prev[1])
        outs.append((D, BF16))
    return _rowwise(body, rows, vecs, outs, 4 if has_prev else 3, R=R, L=L, tr=tr, name=name, acc_width=D)


def _loss_head(x4, target, m_prev, gf, gate, *, L, tr, name):
    D = x4.shape[-1]

    def body(is_ctx, xt, tg, mp, gfv, gatev):
        xh, r = _rms_stats(xt)
        e = xh * gfv - tg
        dy = e * (1.0 / D)
        w = dy * gfv
        dx = r * (w - xh * jnp.mean(w * xh, axis=-1, keepdims=True))
        accs = [_colsum(e * e) * (0.5 / D), _colsum(dy * xh), _colsum(dx * mp.astype(F32))]
        return (dx, dx * gatev), accs

    return _rowwise(body, [(x4, D, 0, False), (target, D, 0, False), (m_prev, D, 0, False)], [gf, gate],
                    [(D, F32), (D, BF16)], 3, R=L, L=L, tr=tr, name=name, acc_width=D)


RET_CHUNK = 2 * LANES
RET_HEADS_PER_STEP = 2


def _decays(lgh, rev):
    C = RET_CHUNK
    ii = lax.broadcasted_iota(jnp.int32, (C, C), 0)
    jj = lax.broadcasted_iota(jnp.int32, (C, C), 1)
    ri = lax.broadcasted_iota(jnp.int32, (C, 1), 0).astype(F32)
    diff = (jj - ii if rev else ii - jj)
    amat = jnp.where(diff >= 0, jnp.exp(lgh * jnp.maximum(diff, 0).astype(F32)), 0.0)
    pos = (C - ri) if rev else (ri + 1.0)
    bq = jnp.exp(lgh * pos)
    bk = jnp.exp(lgh * (C - pos))
    return amat, bq, bk, pos


def _ret_geometry(T, L, H, rev, backward):
    C = RET_CHUNK
    assert T % C == 0 and L % C == 0, (T, L)
    nT, nL = T // C, L // C
    hb = RET_HEADS_PER_STEP if H % RET_HEADS_PER_STEP == 0 else 1

    def step(s):
        return (nT - 1 - s) if backward else s

    def chunk(s):
        s = step(s)
        return (nT - 1 - s) if rev else (s + nL) % nT

    return C, nT, hb, chunk, step


def _ret_fwd(qk, vg, lg, *, T, L, H, rev, name):
    dk, dv = 2 * LANES, 4 * LANES
    C, nT, hb, chunk, step = _ret_geometry(T, L, H, rev, False)

    def body(lg_ref, q_ref, k_ref, v_ref, o_ref, st_ref, s_scr):
        hg, s = pl.program_id(0), pl.program_id(1)

        @pl.when(s == 0)
        def _():
            s_scr[...] = jnp.zeros_like(s_scr)

        for hh in range(hb):
            lgh = lg_ref[0, hg * hb + hh]
            amat, bq, bk, _ = _decays(lgh, rev)
            q, k = q_ref[:, hh * dk:(hh + 1) * dk], k_ref[:, hh * dk:(hh + 1) * dk]
            v = v_ref[:, hh * dv:(hh + 1) * dv]
            stb = s_scr[hh].astype(BF16)
            st_ref[hh] = stb
            scores = _dot(q, k, _NT) * amat
            o_ref[:, hh * dv:(hh + 1) * dv] = _dot(scores.astype(BF16), v) + _dot(q, stb) * bq
            kd = (k.astype(F32) * bk).astype(BF16)
            s_scr[hh] = s_scr[hh] * jnp.exp(lgh * C) + _dot(kd, v, _TN)

    return pl.pallas_call(
        body, name=name, grid=(H // hb, nT),
        in_specs=[pl.BlockSpec(memory_space=pltpu.SMEM),
                  pl.BlockSpec((C, hb * dk), lambda h, s: (chunk(s), h)),
                  pl.BlockSpec((C, hb * dk), lambda h, s: (chunk(s), H // hb + h)),
                  pl.BlockSpec((C, hb * dv), lambda h, s: (chunk(s), h))],
        out_specs=[pl.BlockSpec((C, hb * dv), lambda h, s: (chunk(s), h)),
                   pl.BlockSpec((hb, None, dk, dv), lambda h, s: (h, s, 0, 0))],
        out_shape=[jax.ShapeDtypeStruct((T, H * dv), F32), jax.ShapeDtypeStruct((H, nT, dk, dv), BF16)],
        scratch_shapes=[pltpu.VMEM((hb, dk, dv), F32)],
        compiler_params=_cparams("parallel", "arbitrary"),
    )(lg, qk, qk, vg)


def _ret_bwd(qk, vg, do, states, lg, *, T, L, H, rev, name):
    dk, dv = 2 * LANES, 4 * LANES
    C, nT, hb, chunk, step = _ret_geometry(T, L, H, rev, True)

    def body(lg_ref, q_ref, k_ref, v_ref, do_ref, st_ref, dq_ref, dk_ref, dv_ref, dlg_ref, ds_scr):
        hg, s = pl.program_id(0), pl.program_id(1)

        @pl.when(s == 0)
        def _():
            ds_scr[...] = jnp.zeros_like(ds_scr)
            dlg_ref[...] = jnp.zeros_like(dlg_ref)

        for hh in range(hb):
            lgh = lg_ref[0, hg * hb + hh]
            amat, bq, bk, pos = _decays(lgh, rev)
            ksl, vsl = slice(hh * dk, (hh + 1) * dk), slice(hh * dv, (hh + 1) * dv)
            q, k, v, dob = q_ref[:, ksl], k_ref[:, ksl], v_ref[:, vsl], do_ref[:, vsl]
            stb = st_ref[hh]
            ds_new = ds_scr[hh]
            dsb = ds_new.astype(BF16)
            qf, kf = q.astype(F32), k.astype(F32)
            scores = (_dot(q, k, _NT) * amat).astype(BF16)
            dqk = (_dot(dob, v, _NT) * amat).astype(BF16)
            dq = _dot(dqk, k) + _dot(dob, stb, _NT) * bq
            dkk = _dot(dqk, q, _TN) + _dot(v, dsb, _NT) * bk
            kd = (kf * bk).astype(BF16)
            dvv = _dot(scores, dob, _TN) + _dot(kd, dsb)
            dod = (dob.astype(F32) * bq).astype(BF16)
            ds_prev = ds_new * jnp.exp(lgh * C) + _dot(q, dod, _TN)
            ds_scr[hh] = ds_prev
            dq_ref[:, ksl] = dq
            dk_ref[:, ksl] = dkk
            dv_ref[:, vsl] = dvv
            dlg = (jnp.sum(pos * jnp.sum(qf * dq - kf * dkk, axis=-1, keepdims=True))
                   + C * jnp.sum(ds_prev * stb.astype(F32)))
            dlg_ref[hh] += dlg

    qspec = pl.BlockSpec((C, hb * dk), lambda h, s: (chunk(s), h))
    vspec = pl.BlockSpec((C, hb * dv), lambda h, s: (chunk(s), h))
    return pl.pallas_call(
        body, name=name, grid=(H // hb, nT),
        in_specs=[pl.BlockSpec(memory_space=pltpu.SMEM), qspec,
                  pl.BlockSpec((C, hb * dk), lambda h, s: (chunk(s), H // hb + h)), vspec, vspec,
                  pl.BlockSpec((hb, None, dk, dv), lambda h, s: (h, step(s), 0, 0))],
        out_specs=[qspec, qspec, vspec, pl.BlockSpec((hb, 8, LANES), lambda h, s: (h, 0, 0))],
        out_shape=[jax.ShapeDtypeStruct((T, H * dk), F32), jax.ShapeDtypeStruct((T, H * dk), F32),
                   jax.ShapeDtypeStruct((T, H * dv), F32), jax.ShapeDtypeStruct((H, 8, LANES), F32)],
        scratch_shapes=[pltpu.VMEM((hb, dk, dv), F32)],
        compiler_params=_cparams("parallel", "arbitrary"),
    )(lg, qk, qk, vg, do, states)


def _readout(o_f, o_b, vg, *, T, L, H, tr, name):
    dv = 4 * LANES
    W = H * dv

    def body(is_ctx, of, ob, g):
        o = of + ob
        parts = []
        for h in range(H):
            oh = o[:, h * dv:(h + 1) * dv]
            parts.append(oh * lax.rsqrt(jnp.mean(oh * oh, axis=-1, keepdims=True) + NORM_EPS))
        y = jnp.concatenate(parts, axis=1)
        return (_silu(g.astype(F32)) * y,), ()

    return _rowwise(body, [(o_f, W, 0, False), (o_b, W, 0, False), (vg, W, 1, False)], [], [(W, BF16)], 0,
                    R=T, L=L, tr=tr, name=name)[0]


def _readout_bwd(dz, o_f, o_b, vg, *, T, L, H, tr, name):
    dv = 4 * LANES
    W = H * dv

    def body(is_ctx, dzt, of, ob, g):
        o = of + ob
        gf = g.astype(F32)
        sg = jax.nn.sigmoid(gf)
        dzf = dzt.astype(F32)
        dy = dzf * (gf * sg)
        ys, dos = [], []
        for h in range(H):
            sl = slice(h * dv, (h + 1) * dv)
            oh, dyh = o[:, sl], dy[:, sl]
            r = lax.rsqrt(jnp.mean(oh * oh, axis=-1, keepdims=True) + NORM_EPS)
            yh = oh * r
            ys.append(yh)
            dos.append(r * (dyh - yh * jnp.mean(dyh * yh, axis=-1, keepdims=True)))
        y = jnp.concatenate(ys, axis=1)
        dg = dzf * y * (sg * (1.0 + gf * (1.0 - sg)))
        return (jnp.concatenate(dos, axis=1), dg), ()

    return _rowwise(body, [(dz, W, 0, False), (o_f, W, 0, False), (o_b, W, 0, False), (vg, W, 1, False)], [],
                    [(W, BF16), (W, BF16)], 0, R=T, L=L, tr=tr, name=name)


def _ret_assemble(dq_f, dq_b, dk_f, dk_b, dv_f, dv_b, dg, cos, sin, *, T, L, H, tr, name):
    dk, dv = 2 * LANES, 4 * LANES
    Wq, Wv = H * dk, H * dv
    kscale = float(dk) ** -0.5

    def unrope(d, c, s_, scale):
        parts = []
        for h in range(H):
            d1, d2 = d[:, h * dk:h * dk + LANES], d[:, h * dk + LANES:(h + 1) * dk]
            parts += [(d1 * c + d2 * s_) * scale, (d2 * c - d1 * s_) * scale]
        return jnp.concatenate(parts, axis=1)

    def body(is_ctx, qf, qb, kf, kb, vf, vb, g, c, s_):
        dq = unrope(qf + qb, c, s_, 1.0)
        dkk = unrope(kf + kb, c, s_, kscale)
        return (jnp.concatenate([dq.astype(BF16), dkk.astype(BF16), (vf + vb).astype(BF16), g], axis=1),), ()

    rows = [(dq_f, Wq, 0, False), (dq_b, Wq, 0, False), (dk_f, Wq, 0, False), (dk_b, Wq, 0, False),
            (dv_f, Wv, 0, False), (dv_b, Wv, 0, False), (dg, Wv, 0, False),
            (cos, LANES, 0, False), (sin, LANES, 0, False)]
    return _rowwise(body, rows, [], [(2 * Wq + 2 * Wv, BF16)], 0, R=T, L=L, tr=tr, name=name)[0]


def _swap32(x):
    n = x.shape[-1]
    lane = lax.broadcasted_iota(jnp.int32, x.shape, x.ndim - 1)
    return jnp.where(lane % 64 < 32, pltpu.roll(x, n - 32, x.ndim - 1), pltpu.roll(x, 32, x.ndim - 1))


def _band_masks(n, nb):
    ii = lax.broadcasted_iota(jnp.int32, (CHUNK, CHUNK), 0)
    jj = lax.broadcasted_iota(jnp.int32, (CHUNK, CHUNK), 1)
    return (jj >= ii) & (n > 0), (jj <= ii) & (n < nb - 1)


def _attn_kv_specs(L, CTX, nb):
    blk = lambda f: pl.BlockSpec((CHUNK, LANES), lambda h, n: (f(n), h))
    prev_, cur_, next_ = (lambda n: jnp.maximum(n - 1, 0)), (lambda n: n), (lambda n: jnp.minimum(n + 1, nb - 1))
    ctx_spec = pl.BlockSpec((CTX, LANES), lambda h, n: (L // CTX, h))
    return [blk(prev_), blk(cur_), blk(next_), ctx_spec]


def _attn_fwd(q, k, v, sink, *, L, CTX, Hkv, G, name):
    nb = L // CHUNK
    scale = float(ATT_HEAD_DIM) ** -0.5
    kvs = _attn_kv_specs(L, CTX, nb)

    def body(sink_ref, q_ref, kp, kc, kn, kx, vp, vc, vn, vx, o_ref, lse_ref):
        h, n = pl.program_id(0), pl.program_id(1)
        mp, mn = _band_masks(n, nb)
        for g in range(G):
            qg = q_ref[:, g * LANES:(g + 1) * LANES]
            sp = jnp.where(mp, _dot(qg, kp[...], _NT) * scale, NEG_INF)
            sc = _dot(qg, kc[...], _NT) * scale
            sn = jnp.where(mn, _dot(qg, kn[...], _NT) * scale, NEG_INF)
            sx = _dot(qg, kx[...], _NT) * scale
            sk = sink_ref[0, h * G + g]
            rmax = lambda t: jnp.max(t, axis=-1, keepdims=True)
            m = jnp.maximum(jnp.maximum(jnp.maximum(rmax(sp), rmax(sc)), jnp.maximum(rmax(sn), rmax(sx))), sk)
            pp, pc, pn, px = jnp.exp(sp - m), jnp.exp(sc - m), jnp.exp(sn - m), jnp.exp(sx - m)
            rsum = lambda t: jnp.sum(t, axis=-1, keepdims=True)
            den = rsum(pp) + rsum(pc) + rsum(pn) + rsum(px) + jnp.exp(sk - m)
            o = (_dot(pp.astype(BF16), vp[...]) + _dot(pc.astype(BF16), vc[...])
                 + _dot(pn.astype(BF16), vn[...]) + _dot(px.astype(BF16), vx[...]))
            o_ref[:, g * LANES:(g + 1) * LANES] = (o / den).astype(o_ref.dtype)
            lse_ref[:, g:g + 1] = m + jnp.log(den)

    return pl.pallas_call(
        body, name=name, grid=(Hkv, nb),
        in_specs=[pl.BlockSpec(memory_space=pltpu.SMEM),
                  pl.BlockSpec((CHUNK, G * LANES), lambda h, n: (n, h))] + kvs + kvs,
        out_specs=[pl.BlockSpec((CHUNK, G * LANES), lambda h, n: (n, h)),
                   pl.BlockSpec((None, CHUNK, G), lambda h, n: (h, n, 0))],
        out_shape=[jax.ShapeDtypeStruct((L, Hkv * G * LANES), BF16), jax.ShapeDtypeStruct((Hkv, L, G), F32)],
        compiler_params=_cparams("parallel", "parallel"),
    )(sink, q, k, k, k, k, v, v, v, v)


def _attn_bwd_q(q, k, v, do, lse, sink, *, L, CTX, Hkv, G, name):
    nb = L // CHUNK
    scale = float(ATT_HEAD_DIM) ** -0.5
    kvs = _attn_kv_specs(L, CTX, nb)
    qspec = pl.BlockSpec((CHUNK, G * LANES), lambda h, n: (n, h))
    rowspec = pl.BlockSpec((None, CHUNK, G), lambda h, n: (h, n, 0))

    def body(sink_ref, q_ref, do_ref, lse_ref, kp, kc, kn, kx, vp, vc, vn, vx,
             dq_ref, dl_ref, dkx_ref, dvx_ref, dsk_ref):
        h, n = pl.program_id(0), pl.program_id(1)
        mp, mn = _band_masks(n, nb)

        @pl.when(n == 0)
        def _():
            dkx_ref[...] = jnp.zeros_like(dkx_ref)
            dvx_ref[...] = jnp.zeros_like(dvx_ref)
            dsk_ref[...] = jnp.zeros_like(dsk_ref)

        for g in range(G):
            sl = slice(g * LANES, (g + 1) * LANES)
            qg, dog = q_ref[:, sl], do_ref[:, sl]
            lse_g = lse_ref[:, g:g + 1]
            sk = sink_ref[0, h * G + g]
            ks, vs, masks = (kp, kc, kn, kx), (vp, vc, vn, vx), (mp, None, mn, None)
            ps, dps = [], []
            for kr, vr, msk in zip(ks, vs, masks):
                s_ = _dot(qg, kr[...], _NT) * scale
                p = jnp.exp(s_ - lse_g)
                if msk is not None:
                    p = jnp.where(msk, p, 0.0)
                ps.append(p)
                dps.append(_dot(dog, vr[...], _NT))
            delta = sum(jnp.sum(p * dp, axis=-1, keepdims=True) for p, dp in zip(ps, dps))
            dq = jnp.zeros((CHUNK, LANES), F32)
            for idx, (p, dp, kr) in enumerate(zip(ps, dps, ks)):
                ds = (p * (dp - delta) * scale).astype(BF16)
                dq = dq + _dot(ds, kr[...])
                if idx == 3:
                    dkx_ref[...] += _dot(ds, qg, _TN)
                    dvx_ref[...] += _dot(p.astype(BF16), dog, _TN)
            dq_ref[:, sl] = dq
            dl_ref[:, g:g + 1] = delta
            dsk_ref[g:g + 1, :] += -jnp.sum(jnp.exp(sk - lse_g) * delta)

    return pl.pallas_call(
        body, name=name, grid=(Hkv, nb),
        in_specs=[pl.BlockSpec(memory_space=pltpu.SMEM), qspec, qspec, rowspec] + kvs + kvs,
        out_specs=[qspec, rowspec,
                   pl.BlockSpec((CTX, LANES), lambda h, n: (0, h)), pl.BlockSpec((CTX, LANES), lambda h, n: (0, h)),
                   pl.BlockSpec((None, 8, LANES), lambda h, n: (h, 0, 0))],
        out_shape=[jax.ShapeDtypeStruct((L, Hkv * G * LANES), F32), jax.ShapeDtypeStruct((Hkv, L, G), F32),
                   jax.ShapeDtypeStruct((CTX, Hkv * LANES), F32), jax.ShapeDtypeStruct((CTX, Hkv * LANES), F32),
                   jax.ShapeDtypeStruct((Hkv, 8, LANES), F32)],
        compiler_params=_cparams("parallel", "arbitrary"),
    )(sink, q, do, lse, k, k, k, k, v, v, v, v)


def _attn_bwd_kv(q, k, v, do, lse, delta, *, L, Hkv, G, name):
    nb = L // CHUNK
    scale = float(ATT_HEAD_DIM) ** -0.5
    fs = [(lambda n: jnp.maximum(n - 1, 0)), (lambda n: n), (lambda n: jnp.minimum(n + 1, nb - 1))]
    qspecs = [pl.BlockSpec((CHUNK, G * LANES), lambda h, n, f=f: (f(n), h)) for f in fs]
    rspecs = [pl.BlockSpec((None, CHUNK, G), lambda h, n, f=f: (h, f(n), 0)) for f in fs]
    kspec = pl.BlockSpec((CHUNK, LANES), lambda h, n: (n, h))

    def body(k_ref, v_ref, q0, q1, q2, d0, d1, d2, l0, l1, l2, e0, e1, e2, dk_ref, dv_ref):
        n = pl.program_id(1)
        ii = lax.broadcasted_iota(jnp.int32, (CHUNK, CHUNK), 0)
        jj = lax.broadcasted_iota(jnp.int32, (CHUNK, CHUNK), 1)
        masks = ((jj <= ii) & (n > 0), None, (jj >= ii) & (n < nb - 1))
        kb, vb = k_ref[...], v_ref[...]
        dk = jnp.zeros((CHUNK, LANES), F32)
        dv = jnp.zeros((CHUNK, LANES), F32)
        for qr, dr, lr, er, msk in zip((q0, q1, q2), (d0, d1, d2), (l0, l1, l2), (e0, e1, e2), masks):
            for g in range(G):
                sl = slice(g * LANES, (g + 1) * LANES)
                qg, dog = qr[:, sl], dr[:, sl]
                p = jnp.exp(_dot(qg, kb, _NT) * scale - lr[:, g:g + 1])
                if msk is not None:
                    p = jnp.where(msk, p, 0.0)
                ds = (p * (_dot(dog, vb, _NT) - er[:, g:g + 1]) * scale).astype(BF16)
                dk = dk + _dot(ds, qg, _TN)
                dv = dv + _dot(p.astype(BF16), dog, _TN)
        dk_ref[...] = dk
        dv_ref[...] = dv

    return pl.pallas_call(
        body, name=name, grid=(Hkv, nb),
        in_specs=[kspec, kspec] + qspecs + qspecs + rspecs + rspecs,
        out_specs=[kspec, kspec],
        out_shape=[jax.ShapeDtypeStruct((L, Hkv * LANES), F32), jax.ShapeDtypeStruct((L, Hkv * LANES), F32)],
        compiler_params=_cparams("parallel", "parallel"),
    )(k, v, q, q, q, do, do, do, lse, lse, lse, delta, delta, delta)


def _attn_assemble(dq, dk_lat, dv_lat, dk_ctx, dv_ctx, cos, sin, *, T, L, CTX, Hq, Hkv, tr, name):
    Wq, Wk = Hq * LANES, Hkv * LANES
    ctx_blocks = CTX // tr
    nl = L // tr

    def unrope(d, c, s_, heads):
        return d * jnp.tile(c, (1, heads)) + _swap32(d * jnp.tile(s_, (1, heads)))

    def body(is_ctx, dqt, dkl, dvl, dkc, dvc, c, s_):
        dq_ = jnp.where(is_ctx, 0.0, unrope(dqt, c, s_, Hq))
        dk_ = unrope(jnp.where(is_ctx, dkc, dkl), c, s_, Hkv)
        dv_ = jnp.where(is_ctx, dvc, dvl)
        return (jnp.concatenate([dq_, dk_, dv_], axis=1),), ()

    def ctx_map(i):
        return (jnp.clip(i - nl, 0, ctx_blocks - 1), 0)

    assert T % tr == 0 and L % tr == 0 and CTX % tr == 0
    in_specs = [pl.BlockSpec((tr, Wq), lambda i: (jnp.minimum(i, nl - 1), 0)),
                pl.BlockSpec((tr, Wk), lambda i: (jnp.minimum(i, nl - 1), 0)),
                pl.BlockSpec((tr, Wk), lambda i: (jnp.minimum(i, nl - 1), 0)),
                pl.BlockSpec((tr, Wk), ctx_map), pl.BlockSpec((tr, Wk), ctx_map),
                pl.BlockSpec((tr, LANES), lambda i: (i, 0)), pl.BlockSpec((tr, LANES), lambda i: (i, 0))]

    def kern(dq_r, dkl_r, dvl_r, dkc_r, dvc_r, c_r, s_r, o_ref):
        is_ctx = pl.program_id(0) >= nl
        (out,), _ = body(is_ctx, dq_r[...], dkl_r[...], dvl_r[...], dkc_r[...], dvc_r[...], c_r[...], s_r[...])
        o_ref[...] = out.astype(o_ref.dtype)

    return pl.pallas_call(
        kern, name=name, grid=(T // tr,), in_specs=in_specs,
        out_specs=pl.BlockSpec((tr, Wq + 2 * Wk), lambda i: (i, 0)),
        out_shape=jax.ShapeDtypeStruct((T, Wq + 2 * Wk), BF16),
        compiler_params=_cparams("parallel"),
    )(dq, dk_lat, dv_lat, dk_ctx, dv_ctx, cos, sin)


def _my_place():
    x, y, c = lax.axis_index("x"), lax.axis_index("y"), lax.axis_index("c")
    return x, y, c


def _all_gather_small(v, *, name):
    R, C = v.shape

    def body(x_ref, out_ref, send_sems, recv_sems, local_sem):
        x, y, c = _my_place()
        me, sibling = (x, y, c), (x, y, 1 - c)
        chips = [(1 - x, y), (x, 1 - y), (1 - x, 1 - y)]

        def slot(px, py, pc):
            return out_ref.at[4 * px + 2 * py + pc]

        def copy(k, block, to, src=None):
            return pltpu.make_async_remote_copy(
                src_ref=slot(*block) if src is None else src, dst_ref=slot(*block),
                send_sem=send_sems.at[k], recv_sem=recv_sems.at[k], device_id=to, device_id_type=MESH)

        mine = pltpu.make_async_copy(x_ref, slot(*me), local_sem)
        mine.start()
        first = [copy(0, me, sibling, src=x_ref)]
        first += [copy(1 + j, me, (*chip, c), src=x_ref) for j, chip in enumerate(chips)]
        for cp in first:
            cp.start()
        passed = [copy(4 + j, (*chip, c), sibling) for j, chip in enumerate(chips)]
        for j, chip in enumerate(chips):
            copy(1 + j, (*chip, c), me).wait_recv()
            passed[j].start()
        copy(0, sibling, me).wait_recv()
        for j, chip in enumerate(chips):
            copy(4 + j, (*chip, 1 - c), me).wait_recv()
        for cp in first + passed:
            cp.wait_send()
        mine.wait()

    return pl.pallas_call(
        body, name=name, out_shape=jax.ShapeDtypeStruct((N_DEV, R, C), v.dtype),
        in_specs=[pl.BlockSpec(memory_space=pltpu.VMEM)], out_specs=pl.BlockSpec(memory_space=pltpu.VMEM),
        scratch_shapes=[pltpu.SemaphoreType.DMA((7,)), pltpu.SemaphoreType.DMA((7,)), pltpu.SemaphoreType.DMA],
    )(v)


def _shard_slice(ref, axis, idx, size):
    if axis == 1:
        return ref.at[:, pl.ds(idx * size, size), :]
    return ref.at[:, :, pl.ds(idx * size, size)]


def _all_gather_weights(shards, axes, *, name):
    nt = len(shards)
    sizes = [s.shape[a] for s, a in zip(shards, axes)]
    out_shape = []
    for s, a in zip(shards, axes):
        shp = list(s.shape)
        shp[a] *= N_DEV
        out_shape.append(jax.ShapeDtypeStruct(tuple(shp), s.dtype))

    def body(*refs):
        ins, outs = refs[:nt], refs[nt:2 * nt]
        send_sems, recv_sems, local_sems = refs[2 * nt:]
        x, y, c = _my_place()
        me, sibling = (x, y, c), (x, y, 1 - c)
        chips = [(1 - x, y), (x, 1 - y), (1 - x, 1 - y)]
        all_sends = []
        locals_ = []
        for t in range(nt):
            def slot(px, py, pc, t=t):
                return _shard_slice(outs[t], axes[t], 4 * px + 2 * py + pc, sizes[t])

            def copy(k, block, to, src=None, t=t, slot=slot):
                return pltpu.make_async_remote_copy(
                    src_ref=slot(*block) if src is None else src, dst_ref=slot(*block),
                    send_sem=send_sems.at[t, k], recv_sem=recv_sems.at[t, k], device_id=to, device_id_type=MESH)

            mine = pltpu.make_async_copy(ins[t], slot(*me), local_sems.at[t])
            mine.start()
            locals_.append(mine)
            first = [copy(0, me, sibling, src=ins[t])]
            first += [copy(1 + j, me, (*chip, c), src=ins[t]) for j, chip in enumerate(chips)]
            for cp in first:
                cp.start()
            all_sends += first
        for t in range(nt):
            def slot(px, py, pc, t=t):
                return _shard_slice(outs[t], axes[t], 4 * px + 2 * py + pc, sizes[t])

            def copy(k, block, to, t=t, slot=slot):
                return pltpu.make_async_remote_copy(
                    src_ref=slot(*block), dst_ref=slot(*block),
                    send_sem=send_sems.at[t, k], recv_sem=recv_sems.at[t, k], device_id=to, device_id_type=MESH)

            passed = [copy(4 + j, (*chip, c), sibling) for j, chip in enumerate(chips)]
            for j, chip in enumerate(chips):
                copy(1 + j, (*chip, c), me).wait_recv()
                passed[j].start()
            all_sends += passed
        for t in range(nt):
            def slot(px, py, pc, t=t):
                return _shard_slice(outs[t], axes[t], 4 * px + 2 * py + pc, sizes[t])

            def copy(k, block, to, t=t, slot=slot):
                return pltpu.make_async_remote_copy(
                    src_ref=slot(*block), dst_ref=slot(*block),
                    send_sem=send_sems.at[t, k], recv_sem=recv_sems.at[t, k], device_id=to, device_id_type=MESH)

            copy(0, sibling, me).wait_recv()
            for j, chip in enumerate(chips):
                copy(4 + j, (*chip, 1 - c), me).wait_recv()
        for cp in all_sends:
            cp.wait_send()
        for mine in locals_:
            mine.wait()

    return pl.pallas_call(
        body, name=name, out_shape=out_shape,
        in_specs=[pl.BlockSpec(memory_space=pl.ANY)] * nt, out_specs=[pl.BlockSpec(memory_space=pl.ANY)] * nt,
        scratch_shapes=[pltpu.SemaphoreType.DMA((nt, 7)), pltpu.SemaphoreType.DMA((nt, 7)),
                        pltpu.SemaphoreType.DMA((nt,))],
    )(*shards)


def _exchange_grads(grads, axes, *, name):
    nt = len(grads)
    sizes = [g.shape[a] // N_DEV for g, a in zip(grads, axes)]
    out_shape = []
    for g, a, sz in zip(grads, axes, sizes):
        shp = list(g.shape)
        shp[a] = sz
        out_shape.append(jax.ShapeDtypeStruct((N_DEV, *shp), g.dtype))

    def body(*refs):
        ins, outs = refs[:nt], refs[nt:2 * nt]
        send_sems, recv_sems, local_sems = refs[2 * nt:]
        x, y, c = _my_place()
        my_idx = 4 * x + 2 * y + c

        def peer(r):
            px = (1 - x) if r & 4 else x
            py = (1 - y) if r & 2 else y
            pc = (1 - c) if r & 1 else c
            return (px, py, pc)

        copies, locals_ = [], []
        for t in range(nt):
            mine = pltpu.make_async_copy(_shard_slice(ins[t], axes[t], my_idx, sizes[t]), outs[t].at[my_idx],
                                         local_sems.at[t])
            mine.start()
            locals_.append(mine)
            for r in range(1, N_DEV):
                p = peer(r)
                p_idx = 4 * p[0] + 2 * p[1] + p[2]
                cp = pltpu.make_async_remote_copy(
                    src_ref=_shard_slice(ins[t], axes[t], p_idx, sizes[t]), dst_ref=outs[t].at[my_idx],
                    send_sem=send_sems.at[t, r - 1], recv_sem=recv_sems.at[t, r - 1], device_id=p, device_id_type=MESH)
                cp.start()
                copies.append((cp, t, r, p_idx))
        for cp, t, r, p_idx in copies:
            pltpu.make_async_remote_copy(
                src_ref=_shard_slice(ins[t], axes[t], p_idx, sizes[t]), dst_ref=outs[t].at[p_idx],
                send_sem=send_sems.at[t, r - 1], recv_sem=recv_sems.at[t, r - 1], device_id=peer(r),
                device_id_type=MESH).wait_recv()
        for cp, _, _, _ in copies:
            cp.wait_send()
        for mine in locals_:
            mine.wait()

    return pl.pallas_call(
        body, name=name, out_shape=out_shape,
        in_specs=[pl.BlockSpec(memory_space=pl.ANY)] * nt, out_specs=[pl.BlockSpec(memory_space=pl.ANY)] * nt,
        scratch_shapes=[pltpu.SemaphoreType.DMA((nt, 7)), pltpu.SemaphoreType.DMA((nt, 7)),
                        pltpu.SemaphoreType.DMA((nt,))],
    )(*grads)


_HBM_SPEC = pl.BlockSpec(memory_space=pltpu.HBM)
_SEM_SPEC = pl.BlockSpec(memory_space=pltpu.SEMAPHORE)
_ANY_SPEC = pl.BlockSpec(memory_space=pl.ANY)
_DATAFLOW = pltpu.SideEffectType.DATAFLOW_SIDE_EFFECTING
N_PEERS = N_DEV - 1


def _peer(r):
    x, y, c = _my_place()
    return ((1 - x) if r & 4 else x, (1 - y) if r & 2 else y, (1 - c) if r & 1 else c)


def _index_of(place):
    return 4 * place[0] + 2 * place[1] + place[2]


def _slot(ref, axis, idx, size):
    if axis == 0:
        return ref.at[pl.ds(idx * size, size), :]
    return ref.at[:, pl.ds(idx * size, size)]


def _cast_place(w3, layer, axis, me_arr, *, name):
    Ks, Ns = w3.shape[1], w3.shape[2]
    tr = _pick(Ks, 256, 16)
    nblk = Ks // tr
    full = (Ks * N_DEV, Ns) if axis == 0 else (Ks, Ns * N_DEV)
    if axis == 0:
        out_map = lambda i, me: (me[0] * nblk + i, 0)
    else:
        out_map = lambda i, me: (i, me[0])

    def body(me_ref, w_ref, o_ref):
        o_ref[...] = w_ref[...].astype(BF16)

    return pl.pallas_call(
        body, name=name, out_shape=jax.ShapeDtypeStruct(full, BF16),
        grid_spec=pltpu.PrefetchScalarGridSpec(
            num_scalar_prefetch=1, grid=(nblk,),
            in_specs=[pl.BlockSpec((None, tr, Ns), lambda i, me: (layer, i, 0))],
            out_specs=pl.BlockSpec((tr, Ns), out_map)),
        compiler_params=_cparams("parallel"),
    )(me_arr, w3)


AG_FIRST = 4
AG_CHIPS = 3


def _sibling():
    x, y, c = _my_place()
    return (x, y, 1 - c)


def _chip_peer(j, same_core=True):
    x, y, c = _my_place()
    px = (1 - x) if j in (0, 2) else x
    py = (1 - y) if j in (1, 2) else y
    return (px, py, c if same_core else 1 - c)


def _gather_start(lands, axes, after, *, name):
    nt = len(lands)
    sizes = [l.shape[a] // N_DEV for l, a in zip(lands, axes)]

    def body(*refs):
        ins, send_sems, recv_sems, token = refs[:nt], refs[nt + 1], refs[nt + 2], refs[-1]
        my_idx = _index_of(_my_place())
        for t in range(nt):
            mine = _slot(ins[t], axes[t], my_idx, sizes[t])
            for k, to in enumerate([_sibling()] + [_chip_peer(j) for j in range(AG_CHIPS)]):
                pltpu.make_async_remote_copy(src_ref=mine, dst_ref=mine, send_sem=send_sems.at[t * AG_FIRST + k],
                                             recv_sem=recv_sems.at[t * AG_FIRST + k], device_id=to,
                                             device_id_type=MESH).start()
        token[...] = jnp.zeros_like(token)

    res = pl.pallas_call(
        body, name=name,
        out_shape=(pltpu.SemaphoreType.DMA((nt * AG_FIRST,)), pltpu.SemaphoreType.DMA((nt * AG_FIRST,)),
                   *[pltpu.HBM(l.shape, l.dtype) for l in lands], jax.ShapeDtypeStruct((8, LANES), F32)),
        in_specs=[_HBM_SPEC] * nt + [_ANY_SPEC],
        out_specs=(_SEM_SPEC, _SEM_SPEC, *[_HBM_SPEC] * nt, pl.BlockSpec(memory_space=pltpu.VMEM)),
        input_output_aliases={t: 2 + t for t in range(nt)},
        compiler_params=pltpu.CompilerParams(has_side_effects=_DATAFLOW),
    )(*[pltpu.with_memory_space_constraint(l, pltpu.HBM) for l in lands], after)
    return res[0], res[1], list(res[2:2 + nt]), res[-1]


def _gather_forward(send_a, recv_a, lands, axes, after, *, name):
    nt = len(lands)
    sizes = [l.shape[a] // N_DEV for l, a in zip(lands, axes)]

    def body(*refs):
        ins, send_a, recv_a = refs[:nt], refs[nt], refs[nt + 1]
        send_f, recv_f, token = refs[nt + 3], refs[nt + 4], refs[-1]
        my_idx = _index_of(_my_place())
        for t in range(nt):
            for j in range(AG_CHIPS):
                src_dev = _chip_peer(j)
                arrived = _slot(ins[t], axes[t], _index_of(src_dev), sizes[t])
                pltpu.make_async_remote_copy(
                    src_ref=_slot(ins[t], axes[t], my_idx, sizes[t]), dst_ref=arrived,
                    send_sem=send_a.at[t * AG_FIRST + 1 + j], recv_sem=recv_a.at[t * AG_FIRST + 1 + j],
                    device_id=src_dev, device_id_type=MESH).wait_recv()
                pltpu.make_async_remote_copy(src_ref=arrived, dst_ref=arrived, send_sem=send_f.at[t * AG_CHIPS + j],
                                             recv_sem=recv_f.at[t * AG_CHIPS + j], device_id=_sibling(),
                                             device_id_type=MESH).start()
        token[...] = jnp.zeros_like(token)

    res = pl.pallas_call(
        body, name=name,
        out_shape=(pltpu.SemaphoreType.DMA((nt * AG_CHIPS,)), pltpu.SemaphoreType.DMA((nt * AG_CHIPS,)),
                   *[pltpu.HBM(l.shape, l.dtype) for l in lands], jax.ShapeDtypeStruct((8, LANES), F32)),
        in_specs=[_HBM_SPEC] * nt + [_SEM_SPEC, _SEM_SPEC, _ANY_SPEC],
        out_specs=(_SEM_SPEC, _SEM_SPEC, *[_HBM_SPEC] * nt, pl.BlockSpec(memory_space=pltpu.VMEM)),
        input_output_aliases={t: 2 + t for t in range(nt)},
        compiler_params=pltpu.CompilerParams(has_side_effects=_DATAFLOW),
    )(*lands, send_a, recv_a, after)
    return res[0], res[1], list(res[2:2 + nt]), res[-1]


def _gather_wait(send_a, recv_a, send_f, recv_f, lands, axes, after, *, name):
    nt = len(lands)
    sizes = [l.shape[a] // N_DEV for l, a in zip(lands, axes)]

    def body(*refs):
        ins, send_a, recv_a, send_f, recv_f = refs[:nt], refs[nt], refs[nt + 1], refs[nt + 2], refs[nt + 3]
        my_idx = _index_of(_my_place())
        sib = _sibling()
        for t in range(nt):
            mine = _slot(ins[t], axes[t], my_idx, sizes[t])
            for k, to in enumerate([sib] + [_chip_peer(j) for j in range(AG_CHIPS)]):
                pltpu.make_async_remote_copy(src_ref=mine, dst_ref=mine, send_sem=send_a.at[t * AG_FIRST + k],
                                             recv_sem=recv_a.at[t * AG_FIRST + k], device_id=to,
                                             device_id_type=MESH).wait_send()
            pltpu.make_async_remote_copy(src_ref=mine, dst_ref=_slot(ins[t], axes[t], _index_of(sib), sizes[t]),
                                         send_sem=send_a.at[t * AG_FIRST], recv_sem=recv_a.at[t * AG_FIRST],
                                         device_id=sib, device_id_type=MESH).wait_recv()
            for j in range(AG_CHIPS):
                sent = _slot(ins[t], axes[t], _index_of(_chip_peer(j)), sizes[t])
                got = _slot(ins[t], axes[t], _index_of(_chip_peer(j, same_core=False)), sizes[t])
                cp = pltpu.make_async_remote_copy(src_ref=sent, dst_ref=got, send_sem=send_f.at[t * AG_CHIPS + j],
                                                  recv_sem=recv_f.at[t * AG_CHIPS + j], device_id=sib,
                                                  device_id_type=MESH)
                cp.wait_send()
                cp.wait_recv()

    res = pl.pallas_call(
        body, name=name, out_shape=[pltpu.HBM(l.shape, l.dtype) for l in lands],
        in_specs=[_HBM_SPEC] * nt + [_SEM_SPEC] * 4 + [_ANY_SPEC], out_specs=[_HBM_SPEC] * nt,
        input_output_aliases={t: t for t in range(nt)},
        compiler_params=pltpu.CompilerParams(has_side_effects=_DATAFLOW),
    )(*lands, send_a, recv_a, send_f, recv_f, after)
    return list(res)


def _scatter_start(dw, axis, *, name):
    size = dw.shape[axis] // N_DEV
    land_shape = (N_PEERS, size, dw.shape[1]) if axis == 0 else (N_PEERS, dw.shape[0], size)

    def body(dw_ref, land_ref, send_sems, recv_sems, dw_thru, land_thru, token):
        for r in range(1, N_DEV):
            p = _peer(r)
            pltpu.make_async_remote_copy(src_ref=_slot(dw_ref, axis, _index_of(p), size), dst_ref=land_ref.at[r - 1],
                                         send_sem=send_sems.at[r - 1], recv_sem=recv_sems.at[r - 1], device_id=p,
                                         device_id_type=MESH).start()
        token[...] = jnp.zeros_like(token)

    land = pltpu.with_memory_space_constraint(lax.empty(land_shape, dw.dtype), pltpu.HBM)
    return pl.pallas_call(
        body, name=name,
        out_shape=(pltpu.SemaphoreType.DMA((N_PEERS,)), pltpu.SemaphoreType.DMA((N_PEERS,)),
                   pltpu.HBM(dw.shape, dw.dtype), pltpu.HBM(land_shape, dw.dtype), jax.ShapeDtypeStruct((8, LANES), F32)),
        in_specs=[_HBM_SPEC, _HBM_SPEC],
        out_specs=(_SEM_SPEC, _SEM_SPEC, _HBM_SPEC, _HBM_SPEC, pl.BlockSpec(memory_space=pltpu.VMEM)),
        input_output_aliases={0: 2, 1: 3},
        compiler_params=pltpu.CompilerParams(has_side_effects=_DATAFLOW),
    )(pltpu.with_memory_space_constraint(dw, pltpu.HBM), land)


def _scatter_wait(send_sems, recv_sems, dw, land, axis, after, *, name):
    size = dw.shape[axis] // N_DEV

    def body(dw_ref, land_ref, send_sems, recv_sems, after_ref, dw_thru, land_thru):
        for r in range(1, N_DEV):
            p = _peer(r)
            cp = pltpu.make_async_remote_copy(src_ref=_slot(dw_ref, axis, _index_of(p), size), dst_ref=land_ref.at[r - 1],
                                              send_sem=send_sems.at[r - 1], recv_sem=recv_sems.at[r - 1], device_id=p,
                                              device_id_type=MESH)
            cp.wait_send()
            cp.wait_recv()

    return pl.pallas_call(
        body, name=name, out_shape=(pltpu.HBM(dw.shape, dw.dtype), pltpu.HBM(land.shape, land.dtype)),
        in_specs=[_HBM_SPEC, _HBM_SPEC, _SEM_SPEC, _SEM_SPEC, _ANY_SPEC], out_specs=(_HBM_SPEC, _HBM_SPEC),
        input_output_aliases={0: 0, 1: 1},
        compiler_params=pltpu.CompilerParams(has_side_effects=_DATAFLOW),
    )(dw, land, send_sems, recv_sems, after)


def _adamw_math(w, g, m, v):
    m = ADAM_B1 * m + (1.0 - ADAM_B1) * g
    v = ADAM_B2 * v + (1.0 - ADAM_B2) * (g * g)
    m_hat = m / (1.0 - ADAM_B1 ** ADAM_STEP)
    v_hat = v / (1.0 - ADAM_B2 ** ADAM_STEP)
    delta = -ADAM_LR * (m_hat / (jnp.sqrt(v_hat) + ADAM_EPS) + ADAM_WD * w)
    return delta, m, v


def _adamw_sharded(w, m, v, layer, dw, land, axis, me_arr, prev, *, name):
    nl, Ks, Ns = w.shape
    tr = _pick(Ks, 128, 16)
    nblk = Ks // tr
    if axis == 0:
        own_map = lambda i, me: (me[0] * nblk + i, 0)
    else:
        own_map = lambda i, me: (i, me[0])
    wspec = pl.BlockSpec((None, tr, Ns), lambda i, me: (layer, i, 0))
    n_prev = 0 if prev is None else 4

    def body(me_ref, w_ref, m_ref, v_ref, own_ref, r_ref, *rest):
        g_ref, d_ref, nm_ref, nv_ref = rest[n_prev:]
        g = own_ref[...].astype(F32)
        for r in range(N_PEERS):
            g = g + r_ref[r].astype(F32)
        delta, nm, nv = _adamw_math(w_ref[...], g, m_ref[...], v_ref[...])
        g_ref[...], d_ref[...], nm_ref[...], nv_ref[...] = g, delta, nm, nv

    return pl.pallas_call(
        body, name=name, out_shape=[jax.ShapeDtypeStruct((nl, Ks, Ns), F32)] * 4,
        grid_spec=pltpu.PrefetchScalarGridSpec(
            num_scalar_prefetch=1, grid=(nblk,),
            in_specs=[wspec, wspec, wspec, pl.BlockSpec((tr, Ns), own_map),
                      pl.BlockSpec((N_PEERS, tr, Ns), lambda i, me: (0, i, 0))] + [_ANY_SPEC] * n_prev,
            out_specs=[wspec] * 4),
        input_output_aliases={6 + k: k for k in range(n_prev)},
        compiler_params=_cparams("parallel"),
    )(me_arr, w, m, v, dw, land, *(prev or []))


def _adamw_flat(w, g, m, v, *, name):
    def body(w_ref, g_ref, m_ref, v_ref, d_ref, nm_ref, nv_ref):
        d_ref[...], nm_ref[...], nv_ref[...] = _adamw_math(w_ref[...], g_ref[...], m_ref[...], v_ref[...])

    spec = pl.BlockSpec(memory_space=pltpu.VMEM)
    return pl.pallas_call(body, name=name, in_specs=[spec] * 4, out_specs=[spec] * 3,
                          out_shape=[jax.ShapeDtypeStruct(w.shape, F32)] * 3)(w, g, m, v)


def _sum_devices(a, *, name):
    def body(a_ref, o_ref):
        s = a_ref[0]
        for d in range(1, N_DEV):
            s = s + a_ref[d]
        o_ref[...] = s

    spec = pl.BlockSpec(memory_space=pltpu.VMEM)
    return pl.pallas_call(body, name=name, in_specs=[spec], out_specs=spec,
                          out_shape=jax.ShapeDtypeStruct(a.shape[1:], F32))(a)


def _ada_mods(c16, ada_w, ada_b_cols, *, name):
    nl, D, cols = ada_w.shape
    bn = _pick(cols, 512)

    def body(c_ref, w_ref, b_ref, o_ref):
        cond = _silu(c_ref[...]).astype(BF16)
        o_ref[...] = _dot(cond, w_ref[...].astype(BF16)) + b_ref[...]

    return pl.pallas_call(
        body, name=name, grid=(nl, cols // bn),
        in_specs=[pl.BlockSpec((16, D), lambda l, j: (0, 0)), pl.BlockSpec((None, D, bn), lambda l, j: (l, 0, j)),
                  pl.BlockSpec((None, 1, bn), lambda l, j: (l, 0, j))],
        out_specs=pl.BlockSpec((None, 16, bn), lambda l, j: (l, 0, j)),
        out_shape=jax.ShapeDtypeStruct((nl, 16, cols), F32),
        compiler_params=_cparams("parallel", "parallel"),
    )(c16, ada_w, ada_b_cols)


def _ada_bwd(cond_t, dmod, w, m, v, *, name):
    nl, D, cols = w.shape
    tr = _pick(D, 256, 8)

    def body(ct_ref, dm_ref, w_ref, m_ref, v_ref, g_ref, d_ref, nm_ref, nv_ref, dc_ref):
        ct, dm, wt = ct_ref[...], dm_ref[...], w_ref[...]
        g = ct[:, 0:1] * dm[0:1, :]
        for r in range(1, N_DEV + 1):
            g = g + ct[:, r:r + 1] * dm[r:r + 1, :]
        delta, nm, nv = _adamw_math(wt, g, m_ref[...], v_ref[...])
        g_ref[...], d_ref[...], nm_ref[...], nv_ref[...] = g, delta, nm, nv
        dc_ref[...] = jnp.sum(wt * dm[N_DEV:N_DEV + 1, :], axis=-1, keepdims=True)

    wspec = pl.BlockSpec((None, tr, cols), lambda l, i: (l, i, 0))
    return pl.pallas_call(
        body, name=name, grid=(nl, D // tr),
        in_specs=[pl.BlockSpec((tr, 16), lambda l, i: (i, 0)), pl.BlockSpec((None, 16, cols), lambda l, i: (l, 0, 0)),
                  wspec, wspec, wspec],
        out_specs=[wspec] * 4 + [pl.BlockSpec((None, tr, 1), lambda l, i: (l, i, 0))],
        out_shape=[jax.ShapeDtypeStruct((nl, D, cols), F32)] * 4 + [jax.ShapeDtypeStruct((nl, D, 1), F32)],
        compiler_params=_cparams("parallel", "parallel"),
    )(cond_t, dmod, w, m, v)


def _rope_tables(L, CTX):
    def angles(pos, dim):
        inv_freq = ROPE_BASE ** (-jnp.arange(0, dim, 2, dtype=F32) / dim)
        return pos.astype(F32)[:, None] * inv_freq[None, :]

    def pad(cos, sin):
        return (jnp.concatenate([cos, jnp.ones((CTX, LANES), F32)], 0),
                jnp.concatenate([sin, jnp.zeros((CTX, LANES), F32)], 0))

    ret = angles(jnp.arange(L), 2 * LANES)
    ret_cs = pad(jnp.cos(ret), jnp.sin(ret))
    rows = angles(jnp.arange(L) // GRID_W, ATT_HEAD_DIM // 2)
    cols = angles(jnp.arange(L) % GRID_W, ATT_HEAD_DIM // 2)
    cos = jnp.concatenate([jnp.cos(rows)] * 2 + [jnp.cos(cols)] * 2, axis=1)
    sin = jnp.concatenate([-jnp.sin(rows), jnp.sin(rows), -jnp.sin(cols), jnp.sin(cols)], axis=1)
    return ret_cs, pad(cos, sin)


def kernel(x, c, ctx, c_ctx, ada_w, ada_b, norm_mix_g, norm_mlp_g, mlp_w1, mlp_w2, ret_w_in, ret_w_out, ret_decay_fwd, ret_decay_bwd, attn_w_in, attn_w_out, attn_sink, final_norm_g, loss_target, m_c_ctx, m_ada_w, m_ada_b, m_norm_mix_g, m_norm_mlp_g, m_mlp_w1, m_mlp_w2, m_ret_w_in, m_ret_w_out, m_ret_decay_fwd, m_ret_decay_bwd, m_attn_w_in, m_attn_w_out, m_attn_sink, m_final_norm_g, v_c_ctx, v_ada_w, v_ada_b, v_norm_mix_g, v_norm_mlp_g, v_mlp_w1, v_mlp_w2, v_ret_w_in, v_ret_w_out, v_ret_decay_fwd, v_ret_decay_bwd, v_attn_w_in, v_attn_w_out, v_attn_sink, v_final_norm_g):
    L, D = x.shape[1], x.shape[2]
    CTX = ctx.shape[1]
    T = L + CTX
    RH = ret_decay_fwd.shape[-1]
    assert D == RH * 2 * LANES and ada_w.shape[0] == 2 and ret_w_in.shape[0] == 1 and attn_w_in.shape[0] == 1
    Hq = attn_sink.shape[-1]
    Hkv = (attn_w_in.shape[-1] * N_DEV // ATT_HEAD_DIM - Hq) // 2
    G = Hq // Hkv
    FF = mlp_w1.shape[-1] * N_DEV
    Wq_r, Wv_r = RH * 2 * LANES, RH * 4 * LANES
    acols = ada_w.shape[-1]
    tr = _pick(CTX, 256, 8)
    tr_wide = _pick(CTX, 128, 8)
    bmT = T // 4 if (T % 64 == 0) else T
    bmL = L // 4 if (L % 64 == 0) else L
    x_idx, y_idx, c_idx = lax.axis_index("x"), lax.axis_index("y"), lax.axis_index("c")
    me = 4 * x_idx + 2 * y_idx + c_idx
    me_arr = jnp.reshape(me, (1,)).astype(jnp.int32)

    (rcos, rsin), (acos, asin) = _rope_tables(L, CTX)
    lg_f = jax.nn.log_sigmoid(ret_decay_fwd.astype(F32))
    lg_b = jax.nn.log_sigmoid(ret_decay_bwd.astype(F32))

    c_pad = jnp.concatenate([c.astype(F32), jnp.zeros((7, D), F32)], 0)
    c_all = _all_gather_small(c_pad, name="ag_c")[:, 0, :]
    c16 = jnp.concatenate([c_all, c_ctx[None, :], jnp.zeros((7, D), F32)], 0)
    ada_b_cols = lax.dynamic_slice_in_dim(ada_b, me * acols, acols, axis=1)[:, None, :]
    mods_shard = _ada_mods(c16, ada_w, ada_b_cols, name="ada_mods")
    mods_all = _all_gather_small(mods_shard.reshape(32, acols), name="ag_mods")

    wdefs = {"ret_in": (ret_w_in, 0, 1), "ret_out": (ret_w_out, 0, 0), "w1_0": (mlp_w1, 0, 1), "w2_0": (mlp_w2, 0, 0),
             "attn_in": (attn_w_in, 0, 1), "attn_out": (attn_w_out, 0, 0), "w1_1": (mlp_w1, 1, 1), "w2_1": (mlp_w2, 1, 0)}
    groups = [["ret_in"], ["ret_out", "w1_0", "w2_0"], ["attn_in", "attn_out", "w1_1", "w2_1"]]

    def ag_start(gi, after):
        g_axes = [wdefs[k][2] for k in groups[gi]]
        placed = [_cast_place(*wdefs[k], me_arr, name=f"place_{k}") for k in groups[gi]]
        ssem, rsem, lands, tok_ = _gather_start(placed, g_axes, after, name=f"ag_start{gi}")
        return dict(a=(ssem, rsem), lands=lands, axes=g_axes, gi=gi), tok_

    def ag_forward(g, after):
        fs, fr, g["lands"], tok_ = _gather_forward(*g["a"], g["lands"], g["axes"], after, name=f"ag_forward{g['gi']}")
        g["f"] = (fs, fr)
        return tok_

    def ag_wait(g, after):
        return _gather_wait(*g["a"], *g["f"], g["lands"], g["axes"], after, name=f"ag_wait{g['gi']}")

    g0, tok = ag_start(0, mods_all)
    mods_all = (mods_all + tok[0, 0]).reshape(N_DEV, 2, 16, acols).transpose(1, 2, 0, 3).reshape(2, 16, 6, D)
    mod_lat = lax.dynamic_index_in_dim(mods_all, me, axis=1, keepdims=False)
    mod_ctx = mods_all[:, N_DEV]

    def pack(i, ks, kc):
        return jnp.stack([mod_lat[i, ks], mod_lat[i, kc], mod_ctx[i, ks], mod_ctx[i, kc]], 0)

    def gates(i, k):
        return jnp.stack([mod_lat[i, k], mod_ctx[i, k]], 0)

    def gate_epilogue(gl, gc, x_rows_lat_only):
        def epi(acc, i, j, xt, gv):
            if x_rows_lat_only:
                gate = gv[0:1, :]
            else:
                row = i * acc.shape[0] + lax.broadcasted_iota(jnp.int32, (acc.shape[0], 1), 0)
                gate = jnp.where(row >= L, gv[1:2, :], gv[0:1, :])
            return xt + gate * acc, acc
        return epi

    w1, w2 = {}, {}

    mmT = dict(M=T, bm=bmT)
    mmL = dict(M=L, bm=bmL)

    def bn_of(n, off=0):
        b = 4 * LANES
        while n % b or off % b:
            b -= LANES
        return b

    X0 = jnp.concatenate([x[0], ctx[0]], axis=0)
    g_mix0, g_mlp0 = norm_mix_g[0:1], norm_mlp_g[0:1]
    g_mix1, g_mlp1 = norm_mix_g[1:2], norm_mlp_g[1:2]
    a0 = _normmod(X0, g_mix0, pack(0, 0, 1), R=T, L=L, tr=tr, name="normmod_mix0")
    tok = ag_forward(g0, a0)
    (wr_in,) = ag_wait(g0, tok)
    g1, tok = ag_start(1, wr_in)

    bn_qk = _pick(2 * Wq_r, 512, 2 * LANES)
    nq_blocks = Wq_r // bn_qk
    kscale = float(2 * LANES) ** -0.5

    def rope_epi(acc, i, j, cos, sin):
        parts = []
        for h in range(acc.shape[1] // (2 * LANES)):
            x1 = acc[:, h * 2 * LANES:h * 2 * LANES + LANES]
            x2 = acc[:, h * 2 * LANES + LANES:(h + 1) * 2 * LANES]
            parts += [x1 * cos - x2 * sin, x2 * cos + x1 * sin]
        return (jnp.concatenate(parts, axis=1) * jnp.where(j < nq_blocks, 1.0, kscale),)

    def row_tile(arr, bm):
        return (arr, (bm, LANES), lambda i, j: (i, 0))

    (qk0,) = _mm(a0, wr_in, "nn", [BF16], N=2 * Wq_r, K=D, bn=bn_qk, bk=D, name="ret_qk", epilogue=rope_epi,
                 extras=[row_tile(rcos, bmT), row_tile(rsin, bmT)], dep=tok, **mmT)
    bn_vg = bn_of(2 * Wv_r, 2 * Wq_r)
    (vg0,) = _mm(a0, wr_in, "nn", [BF16], N=2 * Wv_r, K=D, bn=bn_vg, bk=D, name="ret_vg", b_col0=2 * Wq_r, **mmT)

    of, st_f = _ret_fwd(qk0, vg0, lg_f, T=T, L=L, H=RH, rev=False, name="ret_scan_f")
    tok = ag_forward(g1, of)
    ob, st_b = _ret_fwd(qk0, vg0, lg_b + tok[0, 0], T=T, L=L, H=RH, rev=True, name="ret_scan_b")
    z0 = _readout(of, ob, vg0, T=T, L=L, H=RH, tr=tr_wide, name="ret_readout")
    wr_out, w1[0], w2[0] = ag_wait(g1, z0)
    g2, tok = ag_start(2, wr_out)

    bnD = _pick(D, 512)

    def xtile(arr, bm):
        return (arr, (bm, bnD), lambda i, j: (i, j))

    def gtile(gv):
        return (gv, (2, bnD), lambda i, j: (0, j))

    bk_v = _pick(Wv_r, 2048)
    X1, ro0 = _mm(z0, wr_out, "nn", [F32, BF16], N=D, K=Wv_r, bn=bnD, bk=bk_v, name="ret_out",
                  epilogue=gate_epilogue(None, None, False), extras=[xtile(X0, bmT), gtile(gates(0, 2))], dep=tok, **mmT)

    def mlp_fwd(Xin, i, g_mlp, rows, name, between=None):
        a = _normmod(Xin, g_mlp, pack(i, 3, 4), R=rows["M"], L=L, tr=tr, name=f"normmod_mlp{name}")

        def relu2(acc, i_, j_):
            u = jnp.maximum(acc, 0.0)
            return u, u * u

        bnF = _pick(FF, 512)
        u, r = _mm(a, w1[i], "nn", [BF16, BF16], N=FF, K=D, bn=bnF, bk=D, name=f"mlp_up{name}", epilogue=relu2, **rows)
        dep = None if between is None else between(u)
        bkF = _pick(FF, 2048)
        Xout, mo = _mm(r, w2[i], "nn", [F32, BF16], N=D, K=FF, bn=bnD, bk=bkF, name=f"mlp_down{name}",
                       epilogue=gate_epilogue(None, None, rows["M"] == L),
                       extras=[xtile(Xin, rows["bm"]), gtile(gates(i, 5))], dep=dep, **rows)
        return a, u, r, Xout, mo

    a1, u0, r0, X2, mo0 = mlp_fwd(X1, 0, g_mlp0, mmT, "0", between=lambda u: ag_forward(g2, u))

    a2 = _normmod(X2, g_mix1, pack(1, 0, 1), R=T, L=L, tr=tr, name="normmod_mix1")
    wa_in, wa_out, w1[1], w2[1] = ag_wait(g2, a2)
    Wq_a, Wk_a = Hq * LANES, Hkv * LANES

    def arope_epi(acc, i, j, cos, sin):
        heads = acc.shape[1] // LANES
        return (acc * jnp.tile(cos, (1, heads)) + _swap32(acc) * jnp.tile(sin, (1, heads)),)

    bn_q = _pick(Wq_a, 512)
    (q1,) = _mm(a2, wa_in, "nn", [BF16], N=Wq_a, K=D, bn=bn_q, bk=D, name="attn_q", epilogue=arope_epi,
                extras=[row_tile(acos, bmL), row_tile(asin, bmL)], **mmL)
    bn_k = bn_of(Wk_a, Wq_a)
    (k1,) = _mm(a2, wa_in, "nn", [BF16], N=Wk_a, K=D, bn=bn_k, bk=D, name="attn_k", b_col0=Wq_a, epilogue=arope_epi,
                extras=[row_tile(acos, bmT), row_tile(asin, bmT)], **mmT)
    bn_v = bn_of(Wk_a, Wq_a + Wk_a)
    (v1,) = _mm(a2, wa_in, "nn", [BF16], N=Wk_a, K=D, bn=bn_v, bk=D, name="attn_v", b_col0=Wq_a + Wk_a, **mmT)
    o1, lse = _attn_fwd(q1, k1, v1, attn_sink, L=L, CTX=CTX, Hkv=Hkv, G=G, name="attn_fwd")
    X3, ao = _mm(o1, wa_out, "nn", [F32, BF16], N=D, K=Wq_a, bn=bnD, bk=_pick(Wq_a, 2048), name="attn_out",
                 epilogue=gate_epilogue(None, None, True), extras=[xtile(X2, bmL), gtile(gates(1, 2))], **mmL)
    a3, u1, r1, X4, mo1 = mlp_fwd(X3, 1, g_mlp1, mmL, "1")

    dX4, dmo1, acc_head = _loss_head(X4, loss_target[0], mo1, final_norm_g[None, :], gates(1, 5)[0:1], L=L, tr=tr,
                                     name="loss_head")
    loss_part = jnp.sum(acc_head[0, 0])
    d_gf = acc_head[0, 1]
    zeros_d = jnp.zeros((D,), F32)
    dmod_lat = [[zeros_d] * 6, [zeros_d] * 6]
    dmod_ctx = [[zeros_d] * 6, [zeros_d] * 6]
    dmod_lat[1][5] = acc_head[0, 2]

    bn_dw = 2048

    def mlp_bwd(dmo, a, u, r, i, rows, name):
        Mr = rows["M"]
        bkr = rows["bm"]

        def times_2u(acc, i_, j_, ut):
            return (acc * (2.0 * ut.astype(F32)),)

        bnF = _pick(FF, 512)
        (dw2,) = _mm(r, dmo, "tn", [BF16], M=FF, N=D, K=Mr, bm=_pick(FF, 1024), bn=_pick(D, bn_dw), bk=bkr,
                     name=f"mlp_down_dw{name}")
        tok_ = send_grad(f"w2_{i}", dw2, 0)
        (dh,) = _mm(dmo, w2[i], "nt", [BF16], N=FF, K=D, bn=bnF, bk=D, name=f"mlp_down_dx{name}", epilogue=times_2u,
                    extras=[(u, (rows["bm"], bnF), lambda i_, j_: (i_, j_))], dep=tok_, **rows)
        (dw1,) = _mm(a, dh, "tn", [BF16], M=D, N=FF, K=Mr, bm=_pick(D, 1024), bn=_pick(FF, bn_dw), bk=bkr,
                     name=f"mlp_up_dw{name}")
        tok_ = send_grad(f"w1_{i}", dw1, 1)
        (da,) = _mm(dh, w1[i], "nt", [F32], N=D, K=FF, bn=bnD, bk=_pick(FF, 2048), name=f"mlp_up_dx{name}", dep=tok_,
                    **rows)
        return da

    pending = []

    def send_grad(key, dw, axis):
        ssem, rsem, dw_thru, land, tok_ = _scatter_start(dw, axis, name=f"rs_start_{key}")
        pending.append((key, axis, ssem, rsem, dw_thru, land))
        return tok_

    da3 = mlp_bwd(dmo1, a3, u1, r1, 1, mmL, "1")
    dX3, dao, acc = _normmod_bwd(X3, da3, dX4, False, g_mlp1, pack(1, 3, 4), (ao, gates(1, 2)), R=L, L=L, tr=tr,
                                 name="normmod_mlp1_bwd")
    dmod_lat[1][3], dmod_lat[1][4], d_gmlp1, dmod_lat[1][2] = acc[0, 0], acc[0, 1], acc[0, 2], acc[0, 3]

    (dwa_out,) = _mm(o1, dao, "tn", [BF16], M=Wq_a, N=D, K=L, bm=_pick(Wq_a, 1024), bn=_pick(D, bn_dw), bk=bmL,
                     name="attn_out_dw")
    tok = send_grad("attn_out", dwa_out, 0)
    (do1,) = _mm(dao, wa_out, "nt", [BF16], N=Wq_a, K=D, bn=_pick(Wq_a, 512), bk=D, name="attn_out_dx", dep=tok, **mmL)
    dq1, delta1, dkx, dvx, dsink_acc = _attn_bwd_q(q1, k1, v1, do1, lse, attn_sink, L=L, CTX=CTX, Hkv=Hkv, G=G,
                                                   name="attn_bwd_q")
    dk1, dv1 = _attn_bwd_kv(q1, k1, v1, do1, lse, delta1, L=L, Hkv=Hkv, G=G, name="attn_bwd_kv")
    dp1 = _attn_assemble(dq1, dk1, dv1, dkx, dvx, acos, asin, T=T, L=L, CTX=CTX, Hq=Hq, Hkv=Hkv, tr=tr_wide,
                         name="attn_assemble")
    Wa_in = Wq_a + 2 * Wk_a
    (dwa_in,) = _mm(a2, dp1, "tn", [BF16], M=D, N=Wa_in, K=T, bm=_pick(D, 1024), bn=_pick(Wa_in, bn_dw), bk=bmT,
                    name="attn_in_dw")
    tok = send_grad("attn_in", dwa_in, 1)
    (da2,) = _mm(dp1, wa_in, "nt", [F32], N=D, K=Wa_in, bn=bnD, bk=_pick(Wa_in, 2048), name="attn_in_dx", dep=tok,
                 **mmT)
    dX2, dmo0, acc = _normmod_bwd(X2, da2, dX3, True, g_mix1, pack(1, 0, 1), (mo0, gates(0, 5)), R=T, L=L, tr=tr,
                                  name="normmod_mix1_bwd")
    dmod_lat[1][0], dmod_lat[1][1], d_gmix1, dmod_lat[0][5] = acc[0, 0], acc[0, 1], acc[0, 2] + acc[1, 2], acc[0, 3]
    dmod_ctx[1][0], dmod_ctx[1][1], dmod_ctx[0][5] = acc[1, 0], acc[1, 1], acc[1, 3]

    da1 = mlp_bwd(dmo0, a1, u0, r0, 0, mmT, "0")
    dX1, dro0, acc = _normmod_bwd(X1, da1, dX2, False, g_mlp0, pack(0, 3, 4), (ro0, gates(0, 2)), R=T, L=L, tr=tr,
                                  name="normmod_mlp0_bwd")
    dmod_lat[0][3], dmod_lat[0][4], d_gmlp0, dmod_lat[0][2] = acc[0, 0], acc[0, 1], acc[0, 2] + acc[1, 2], acc[0, 3]
    dmod_ctx[0][3], dmod_ctx[0][4], dmod_ctx[0][2] = acc[1, 0], acc[1, 1], acc[1, 3]

    (dwr_out,) = _mm(z0, dro0, "tn", [BF16], M=Wv_r, N=D, K=T, bm=_pick(Wv_r, 1024), bn=_pick(D, bn_dw), bk=bmT,
                     name="ret_out_dw")
    tok = send_grad("ret_out", dwr_out, 0)
    (dz0,) = _mm(dro0, wr_out, "nt", [BF16], N=Wv_r, K=D, bn=_pick(Wv_r, 512), bk=D, name="ret_out_dx", dep=tok, **mmT)
    do0, dg0 = _readout_bwd(dz0, of, ob, vg0, T=T, L=L, H=RH, tr=tr_wide, name="ret_readout_bwd")
    dq_f, dk_f, dv_f, dlg_f = _ret_bwd(qk0, vg0, do0, st_f, lg_f, T=T, L=L, H=RH, rev=False, name="ret_scan_f_bwd")
    dq_b, dk_b, dv_b, dlg_b = _ret_bwd(qk0, vg0, do0, st_b, lg_b, T=T, L=L, H=RH, rev=True, name="ret_scan_b_bwd")
    dp0 = _ret_assemble(dq_f, dq_b, dk_f, dk_b, dv_f, dv_b, dg0, rcos, rsin, T=T, L=L, H=RH, tr=tr_wide,
                        name="ret_assemble")
    Wr_in = 2 * Wq_r + 2 * Wv_r
    (dwr_in,) = _mm(a0, dp0, "tn", [BF16], M=D, N=Wr_in, K=T, bm=_pick(D, 1024), bn=_pick(Wr_in, bn_dw), bk=bmT,
                    name="ret_in_dw")
    tok = send_grad("ret_in", dwr_in, 1)
    (da0,) = _mm(dp0, wr_in, "nt", [F32], N=D, K=Wr_in, bn=bnD, bk=_pick(Wr_in, 2048), name="ret_in_dx", dep=tok, **mmT)
    dX0, acc = _normmod_bwd(X0, da0, dX1, False, g_mix0, pack(0, 0, 1), None, R=T, L=L, tr=tr, name="normmod_mix0_bwd")
    dmod_lat[0][0], dmod_lat[0][1], d_gmix0 = acc[0, 0], acc[0, 1], acc[0, 2] + acc[1, 2]
    dmod_ctx[0][0], dmod_ctx[0][1] = acc[1, 0], acc[1, 1]
    grad_x = dX0[:L][None]

    wmv = {"ret_in": (ret_w_in, m_ret_w_in, v_ret_w_in, 0, "ret_w_in"),
           "ret_out": (ret_w_out, m_ret_w_out, v_ret_w_out, 0, "ret_w_out"),
           "attn_in": (attn_w_in, m_attn_w_in, v_attn_w_in, 0, "attn_w_in"),
           "attn_out": (attn_w_out, m_attn_w_out, v_attn_w_out, 0, "attn_w_out"),
           "w1_0": (mlp_w1, m_mlp_w1, v_mlp_w1, 0, "mlp_w1"), "w1_1": (mlp_w1, m_mlp_w1, v_mlp_w1, 1, "mlp_w1"),
           "w2_0": (mlp_w2, m_mlp_w2, v_mlp_w2, 0, "mlp_w2"), "w2_1": (mlp_w2, m_mlp_w2, v_mlp_w2, 1, "mlp_w2")}
    big = {}

    def finish_grad(entry, after):
        key, axis, ssem, rsem, dw_thru, land = entry
        dw_done, land_done = _scatter_wait(ssem, rsem, dw_thru, land, axis, after, name=f"rs_wait_{key}")
        w_, m_, v_, layer, out_name = wmv[key]
        big[out_name] = _adamw_sharded(w_, m_, v_, layer, dw_done, land_done, axis, me_arr, big.get(out_name),
                                       name=f"adamw_{key}")
        return big[out_name][0]

    after = dX0
    for entry in pending[:-1]:
        after = finish_grad(entry, after)

    misc = jnp.zeros((D,), F32)
    misc = misc.at[0:RH].set(dlg_f[:, 0, 0]).at[RH:2 * RH].set(dlg_b[:, 0, 0])
    misc = misc.at[2 * RH:2 * RH + Hq].set(dsink_acc[:, :G, 0].reshape(Hq)).at[2 * RH + Hq].set(loss_part)
    rows = ([dmod_lat[i][k] for i in range(2) for k in range(6)] + [dmod_ctx[i][k] for i in range(2) for k in range(6)]
            + [d_gmix0, d_gmix1, d_gmlp0, d_gmlp1, d_gf, misc, zeros_d, zeros_d])
    part = jnp.stack(rows, 0)
    part_all = _all_gather_small(part, name="ag_small_grads")
    tot = _sum_devices(part_all, name="sum_small_grads")

    grad_ada_b = (tot[0:12] + tot[12:24]).reshape(2, 6 * D)
    grad_norm_mix_g, grad_norm_mlp_g, grad_final_norm_g = tot[24:26], tot[26:28], tot[28]
    grad_ret_decay_fwd = (tot[29, 0:RH] * jax.nn.sigmoid(-ret_decay_fwd[0]))[None]
    grad_ret_decay_bwd = (tot[29, RH:2 * RH] * jax.nn.sigmoid(-ret_decay_bwd[0]))[None]
    grad_attn_sink = tot[29, 2 * RH:2 * RH + Hq][None]
    loss = tot[29, 2 * RH + Hq]

    dlat_cols = lax.dynamic_slice_in_dim(part_all[:, 0:12].reshape(N_DEV, 2, 6 * D), me * acols, acols, axis=2)
    dctx_cols = lax.dynamic_slice_in_dim(tot[12:24].reshape(2, 6 * D), me * acols, acols, axis=1)
    dmod16 = jnp.concatenate([dlat_cols.transpose(1, 0, 2), dctx_cols[:, None, :], jnp.zeros((2, 7, acols), F32)], 1)
    cond_t = _silu(c16).T
    g_ada, d_ada, nm_ada, nv_ada, dcond_part = _ada_bwd(cond_t, dmod16, ada_w, m_ada_w, v_ada_w, name="ada_bwd")
    dcond = (dcond_part[0, :, 0] + dcond_part[1, :, 0]).reshape(D // LANES, LANES)
    pad_rows = -(D // LANES) % 8
    dcond_pad = jnp.concatenate([dcond, jnp.zeros((pad_rows, LANES), F32)], 0) if pad_rows else dcond
    dcond_all = _all_gather_small(dcond_pad, name="ag_dcond")
    dcond_tot = _sum_devices(dcond_all, name="sum_dcond")[:D // LANES].reshape(D)
    sg = jax.nn.sigmoid(c_ctx)
    grad_c_ctx = dcond_tot * (sg * (1.0 + c_ctx * (1.0 - sg)))

    small_w = [c_ctx, ada_b, norm_mix_g, norm_mlp_g, ret_decay_fwd, ret_decay_bwd, attn_sink, final_norm_g]
    small_g = [grad_c_ctx, grad_ada_b, grad_norm_mix_g, grad_norm_mlp_g, grad_ret_decay_fwd, grad_ret_decay_bwd,
               grad_attn_sink, grad_final_norm_g]
    small_m = [m_c_ctx, m_ada_b, m_norm_mix_g, m_norm_mlp_g, m_ret_decay_fwd, m_ret_decay_bwd, m_attn_sink,
               m_final_norm_g]
    small_v = [v_c_ctx, v_ada_b, v_norm_mix_g, v_norm_mlp_g, v_ret_decay_fwd, v_ret_decay_bwd, v_attn_sink,
               v_final_norm_g]
    sizes = [w_.size for w_ in small_w]
    total = sum(-(-s // LANES) * LANES for s in sizes)
    total_pad = -(-total // (8 * LANES)) * 8 * LANES

    def flat_pack(ts, fill):
        pieces = []
        for t_ in ts:
            f = t_.reshape(-1).astype(F32)
            pad = -f.size % LANES
            pieces.append(jnp.concatenate([f, jnp.full((pad,), fill, F32)]) if pad else f)
        pieces.append(jnp.full((total_pad - total,), fill, F32))
        return jnp.concatenate(pieces).reshape(total_pad // LANES, LANES)

    d_s, nm_s, nv_s = _adamw_flat(flat_pack(small_w, 0.0), flat_pack(small_g, 0.0), flat_pack(small_m, 0.0),
                                  flat_pack(small_v, 1.0), name="adamw_small")

    def unpack(p):
        flat = p.reshape(-1)
        res, off = [], 0
        for w_, s in zip(small_w, sizes):
            res.append(flat[off:off + s].reshape(w_.shape))
            off += -(-s // LANES) * LANES
        return res

    finish_grad(pending[-1], d_s)
    d_small, nm_small, nv_small = unpack(d_s), unpack(nm_s), unpack(nv_s)
    small_names = ["c_ctx", "ada_b", "norm_mix_g", "norm_mlp_g", "ret_decay_fwd", "ret_decay_bwd", "attn_sink",
                   "final_norm_g"]
    sm = {n: (g_, d_, m_, v_) for n, g_, d_, m_, v_ in zip(small_names, small_g, d_small, nm_small, nv_small)}

    def out4(n):
        if n == "ada_w":
            return g_ada, d_ada, nm_ada, nv_ada
        if n in big:
            return tuple(big[n])
        return sm[n]

    order = ["c_ctx", "ada_w", "ada_b", "norm_mix_g", "norm_mlp_g", "mlp_w1", "mlp_w2", "ret_w_in", "ret_w_out",
             "ret_decay_fwd", "ret_decay_bwd", "attn_w_in", "attn_w_out", "attn_sink", "final_norm_g"]
    quads = [out4(n) for n in order]
    return (loss, grad_x, *[q_[0] for q_ in quads], *[q_[1] for q_ in quads], *[q_[2] for q_ in quads],
            *[q_[3] for q_ in quads])
```

```python
import functools

import jax
import jax.numpy as jnp
from jax import lax
from jax.experimental import pallas as pl
from jax.experimental.pallas import tpu as pltpu

F32 = jnp.float32
BF16 = jnp.bfloat16

N_DEV = 8
NORM_EPS = 1e-6
CHUNK = 128
ATT_HEAD_DIM = 128
GRID_W = 64
ROPE_BASE = 10000.0
NEG_INF = -1e30
ADAM_LR, ADAM_B1, ADAM_B2, ADAM_EPS, ADAM_WD, ADAM_STEP = 0.001, 0.9, 0.999, 1e-08, 0.01, 10

V7X_VMEM_LIMIT_BYTES = 56 * 1024 * 1024
LANES = 128
MESH = pl.DeviceIdType.MESH

_NN = (((1,), (0,)), ((), ()))
_NT = (((1,), (1,)), ((), ()))
_TN = (((0,), (0,)), ((), ()))


def _dot(a, b, dn=_NN):
    return lax.dot_general(a, b, dn, preferred_element_type=F32)


def _cparams(*sem):
    return pltpu.CompilerParams(dimension_semantics=sem, vmem_limit_bytes=V7X_VMEM_LIMIT_BYTES)


def _pick(n, pref, mult=LANES):
    if n <= pref:
        return n
    best = None
    for d in range(mult, pref + 1, mult):
        if n % d == 0:
            best = d
    assert best is not None, (n, pref)
    return best


def _silu(x):
    return x * jax.nn.sigmoid(x)


def _mm(a, b, mode, out_dtypes, *, M, N, K, bm, bn, bk, name, b_col0=0, epilogue=None, extras=(), dep=None):
    assert M % bm == 0 and N % bn == 0 and K % bk == 0 and b_col0 % bn == 0, (name, M, N, K, bm, bn, bk, b_col0)
    nk = K // bk
    c0 = b_col0 // bn
    if mode == "nn":
        a_spec = pl.BlockSpec((bm, bk), lambda i, j, k: (i, k))
        b_spec = pl.BlockSpec((bk, bn), lambda i, j, k: (k, j + c0))
    elif mode == "nt":
        a_spec = pl.BlockSpec((bm, bk), lambda i, j, k: (i, k))
        b_spec = pl.BlockSpec((bn, bk), lambda i, j, k: (j + c0, k))
    else:
        a_spec = pl.BlockSpec((bk, bm), lambda i, j, k: (k, i))
        b_spec = pl.BlockSpec((bk, bn), lambda i, j, k: (k, j + c0))
    dn = {"nn": _NN, "nt": _NT, "tn": _TN}[mode]
    e_specs = [pl.BlockSpec(bs, (lambda i, j, k, f=f: f(i, j))) for (_, bs, f) in extras]
    ne, no = len(extras), len(out_dtypes)
    nd = 0 if dep is None else 1

    def body(a_ref, b_ref, *rest):
        e_refs, o_refs = rest[:ne], rest[ne + nd:ne + nd + no]
        i, j, k = pl.program_id(0), pl.program_id(1), pl.program_id(2)

        def finish(acc):
            outs = (acc,) if epilogue is None else epilogue(acc, i, j, *[e[...] for e in e_refs])
            for o_ref, o in zip(o_refs, outs):
                o_ref[...] = o.astype(o_ref.dtype)

        p = _dot(a_ref[...], b_ref[...], dn)
        if nk == 1:
            finish(p)
        else:
            acc_ref = rest[-1]

            @pl.when(k == 0)
            def _():
                acc_ref[...] = p

            @pl.when(k > 0)
            def _():
                acc_ref[...] += p

            @pl.when(k == nk - 1)
            def _():
                finish(acc_ref[...])

    outs = pl.pallas_call(
        body, name=name, grid=(M // bm, N // bn, nk),
        in_specs=[a_spec, b_spec] + e_specs + [pl.BlockSpec(memory_space=pl.ANY)] * nd,
        out_specs=[pl.BlockSpec((bm, bn), lambda i, j, k: (i, j)) for _ in out_dtypes],
        out_shape=[jax.ShapeDtypeStruct((M, N), dt) for dt in out_dtypes],
        scratch_shapes=[pltpu.VMEM((bm, bn), F32)] if nk > 1 else [],
        compiler_params=_cparams("parallel", "parallel", "arbitrary"),
    )(a, b, *[e[0] for e in extras], *([dep] if nd else []))
    return outs


def _rowwise(body, rows, vecs, outs, n_acc, *, R, L, tr, name, acc_width=None, dep=None):
    assert R % tr == 0 and L % tr == 0, (name, R, L, tr)
    nl = L // tr
    n_regions = 2 if R > L else 1
    n_rows, n_vecs, n_outs = len(rows), len(vecs), len(outs)
    n_dep = 0 if dep is None else 1
    acc_pad = -(-n_acc // 8) * 8 if n_acc else 0

    in_specs = []
    for (_, w, cb, lat_only) in rows:
        if lat_only:
            in_specs.append(pl.BlockSpec((tr, w), lambda i, cb=cb: (jnp.minimum(i, nl - 1), cb)))
        else:
            in_specs.append(pl.BlockSpec((tr, w), lambda i, cb=cb: (i, cb)))
    for v in vecs:
        in_specs.append(pl.BlockSpec(v.shape, lambda i, nd=v.ndim: (0,) * nd))
    in_specs += [pl.BlockSpec(memory_space=pl.ANY)] * n_dep
    out_specs = [pl.BlockSpec((tr, w), lambda i: (i, 0)) for (w, _) in outs]
    out_shape = [jax.ShapeDtypeStruct((R, w), dt) for (w, dt) in outs]
    if n_acc:
        out_specs.append(pl.BlockSpec((None, acc_pad, acc_width), lambda i: (jnp.where(i >= nl, 1, 0), 0, 0)))
        out_shape.append(jax.ShapeDtypeStruct((n_regions, acc_pad, acc_width), F32))

    def kern(*refs):
        i = pl.program_id(0)
        is_ctx = i >= nl
        ins = [r[...] for r in refs[:n_rows + n_vecs]]
        o_refs = refs[n_rows + n_vecs + n_dep:]
        out_tiles, acc_rows = body(is_ctx, *ins)
        for o_ref, o in zip(o_refs[:n_outs], out_tiles):
            o_ref[...] = o.astype(o_ref.dtype)
        if n_acc:
            acc_ref = o_refs[n_outs]

            @pl.when((i == 0) | (i == nl))
            def _():
                acc_ref[...] = jnp.zeros_like(acc_ref)

            for r, row in enumerate(acc_rows):
                acc_ref[r:r + 1, :] += row

    res = pl.pallas_call(
        kern, name=name, grid=(R // tr,), in_specs=in_specs, out_specs=out_specs, out_shape=out_shape,
        compiler_params=_cparams("arbitrary"),
    )(*[r[0] for r in rows], *vecs, *([dep] if n_dep else []))
    return res


def _colsum(x):
    return jnp.sum(x, axis=0, keepdims=True)


def _rms_stats(x):
    r = lax.rsqrt(jnp.mean(x * x, axis=-1, keepdims=True) + NORM_EPS)
    return x * r, r


def _sel(is_ctx, pk, lat_row, ctx_row):
    return jnp.where(is_ctx, pk[ctx_row:ctx_row + 1, :], pk[lat_row:lat_row + 1, :])


def _normmod(x, g, pk, *, R, L, tr, name, dep=None):
    D = x.shape[-1]

    def body(is_ctx, xt, gv, pkv):
        xh, _ = _rms_stats(xt)
        sh, sc = _sel(is_ctx, pkv, 0, 2), _sel(is_ctx, pkv, 1, 3)
        return ((xh * gv) * (1.0 + sc) + sh,), ()

    return _rowwise(body, [(x, D, 0, False)], [g, pk], [(D, BF16)], 0, R=R, L=L, tr=tr, name=name, dep=dep)[0]


def _normmod_bwd(x_in, da, dx_out, dx_out_lat_only, g, pk, prev, *, R, L, tr, name):
    D = x_in.shape[-1]
    has_prev = prev is not None

    def body(is_ctx, *t):
        if has_prev:
            xt, dat, dxo, mp, gv, pkv, gates = t
        else:
            xt, dat, dxo, gv, pkv = t
        xh, r = _rms_stats(xt)
        sc = _sel(is_ctx, pkv, 1, 3)
        if dx_out_lat_only:
            dxo = jnp.where(is_ctx, 0.0, dxo)
        dn = dat * (1.0 + sc)
        w = dn * gv
        dxi = dxo + r * (w - xh * jnp.mean(w * xh, axis=-1, keepdims=True))
        accs = [_colsum(dat), _colsum(dat * (xh * gv)), _colsum(dn * xh)]
        outs = [dxi]
        if has_prev:
            gate = _sel(is_ctx, gates, 0, 1)
            outs.append(dxi * gate)
            accs.append(_colsum(dxi * mp.astype(F32)))
        return outs, accs

    rows = [(x_in, D, 0, False), (da, D, 0, False), (dx_out, D, 0, dx_out_lat_only)]
    vecs = [g, pk]
    outs = [(D, F32)]
    if has_prev:
        rows.append((prev[0], D, 0, False))
        vecs.append(prev[1])
        outs.append((D, BF16))
    return _rowwise(body, rows, vecs, outs, 4 if has_prev else 3, R=R, L=L, tr=tr, name=name, acc_width=D)


def _loss_head(x4, target, m_prev, gf, gate, *, L, tr, name):
    D = x4.shape[-1]

    def body(is_ctx, xt, tg, mp, gfv, gatev):
        xh, r = _rms_stats(xt)
        e = xh * gfv - tg
        dy = e * (1.0 / D)
        w = dy * gfv
        dx = r * (w - xh * jnp.mean(w * xh, axis=-1, keepdims=True))
        accs = [_colsum(e * e) * (0.5 / D), _colsum(dy * xh), _colsum(dx * mp.astype(F32))]
        return (dx, dx * gatev), accs

    return _rowwise(body, [(x4, D, 0, False), (target, D, 0, False), (m_prev, D, 0, False)], [gf, gate],
                    [(D, F32), (D, BF16)], 3, R=L, L=L, tr=tr, name=name, acc_width=D)


RET_CHUNK = 2 * LANES
RET_HEADS_PER_STEP = 2


def _decays(lgh, rev):
    C = RET_CHUNK
    ii = lax.broadcasted_iota(jnp.int32, (C, C), 0)
    jj = lax.broadcasted_iota(jnp.int32, (C, C), 1)
    ri = lax.broadcasted_iota(jnp.int32, (C, 1), 0).astype(F32)
    diff = (jj - ii if rev else ii - jj)
    amat = jnp.where(diff >= 0, jnp.exp(lgh * jnp.maximum(diff, 0).astype(F32)), 0.0)
    pos = (C - ri) if rev else (ri + 1.0)
    bq = jnp.exp(lgh * pos)
    bk = jnp.exp(lgh * (C - pos))
    return amat, bq, bk, pos


def _ret_geometry(T, L, H, rev, backward):
    C = RET_CHUNK
    assert T % C == 0 and L % C == 0, (T, L)
    nT, nL = T // C, L // C
    hb = RET_HEADS_PER_STEP if H % RET_HEADS_PER_STEP == 0 else 1

    def step(s):
        return (nT - 1 - s) if backward else s

    def chunk(s):
        s = step(s)
        return (nT - 1 - s) if rev else (s + nL) % nT

    return C, nT, hb, chunk, step


def _ret_fwd(qk, vg, lg, *, T, L, H, rev, name):
    dk, dv = 2 * LANES, 4 * LANES
    C, nT, hb, chunk, step = _ret_geometry(T, L, H, rev, False)

    def body(lg_ref, q_ref, k_ref, v_ref, o_ref, st_ref, s_scr):
        hg, s = pl.program_id(0), pl.program_id(1)

        @pl.when(s == 0)
        def _():
            s_scr[...] = jnp.zeros_like(s_scr)

        for hh in range(hb):
            lgh = lg_ref[0, hg * hb + hh]
            amat, bq, bk, _ = _decays(lgh, rev)
            q, k = q_ref[:, hh * dk:(hh + 1) * dk], k_ref[:, hh * dk:(hh + 1) * dk]
            v = v_ref[:, hh * dv:(hh + 1) * dv]
            stb = s_scr[hh].astype(BF16)
            st_ref[hh] = stb
            scores = _dot(q, k, _NT) * amat
            o_ref[:, hh * dv:(hh + 1) * dv] = _dot(scores.astype(BF16), v) + _dot(q, stb) * bq
            kd = (k.astype(F32) * bk).astype(BF16)
            s_scr[hh] = s_scr[hh] * jnp.exp(lgh * C) + _dot(kd, v, _TN)

    return pl.pallas_call(
        body, name=name, grid=(H // hb, nT),
        in_specs=[pl.BlockSpec(memory_space=pltpu.SMEM),
                  pl.BlockSpec((C, hb * dk), lambda h, s: (chunk(s), h)),
                  pl.BlockSpec((C, hb * dk), lambda h, s: (chunk(s), H // hb + h)),
                  pl.BlockSpec((C, hb * dv), lambda h, s: (chunk(s), h))],
        out_specs=[pl.BlockSpec((C, hb * dv), lambda h, s: (chunk(s), h)),
                   pl.BlockSpec((hb, None, dk, dv), lambda h, s: (h, s, 0, 0))],
        out_shape=[jax.ShapeDtypeStruct((T, H * dv), F32), jax.ShapeDtypeStruct((H, nT, dk, dv), BF16)],
        scratch_shapes=[pltpu.VMEM((hb, dk, dv), F32)],
        compiler_params=_cparams("parallel", "arbitrary"),
    )(lg, qk, qk, vg)


def _ret_bwd(qk, vg, do, states, lg, *, T, L, H, rev, name):
    dk, dv = 2 * LANES, 4 * LANES
    C, nT, hb, chunk, step = _ret_geometry(T, L, H, rev, True)

    def body(lg_ref, q_ref, k_ref, v_ref, do_ref, st_ref, dq_ref, dk_ref, dv_ref, dlg_ref, ds_scr):
        hg, s = pl.program_id(0), pl.program_id(1)

        @pl.when(s == 0)
        def _():
            ds_scr[...] = jnp.zeros_like(ds_scr)
            dlg_ref[...] = jnp.zeros_like(dlg_ref)

        for hh in range(hb):
            lgh = lg_ref[0, hg * hb + hh]
            amat, bq, bk, pos = _decays(lgh, rev)
            ksl, vsl = slice(hh * dk, (hh + 1) * dk), slice(hh * dv, (hh + 1) * dv)
            q, k, v, dob = q_ref[:, ksl], k_ref[:, ksl], v_ref[:, vsl], do_ref[:, vsl]
            stb = st_ref[hh]
            ds_new = ds_scr[hh]
            dsb = ds_new.astype(BF16)
            qf, kf = q.astype(F32), k.astype(F32)
            scores = (_dot(q, k, _NT) * amat).astype(BF16)
            dqk = (_dot(dob, v, _NT) * amat).astype(BF16)
            dq = _dot(dqk, k) + _dot(dob, stb, _NT) * bq
            dkk = _dot(dqk, q, _TN) + _dot(v, dsb, _NT) * bk
            kd = (kf * bk).astype(BF16)
            dvv = _dot(scores, dob, _TN) + _dot(kd, dsb)
            dod = (dob.astype(F32) * bq).astype(BF16)
            ds_prev = ds_new * jnp.exp(lgh * C) + _dot(q, dod, _TN)
            ds_scr[hh] = ds_prev
            dq_ref[:, ksl] = dq.astype(dq_ref.dtype)
            dk_ref[:, ksl] = dkk.astype(dk_ref.dtype)
            dv_ref[:, vsl] = dvv.astype(dv_ref.dtype)
            dlg = (jnp.sum(pos * jnp.sum(qf * dq - kf * dkk, axis=-1, keepdims=True))
                   + C * jnp.sum(ds_prev * stb.astype(F32)))
            dlg_ref[hh] += dlg

    qspec = pl.BlockSpec((C, hb * dk), lambda h, s: (chunk(s), h))
    vspec = pl.BlockSpec((C, hb * dv), lambda h, s: (chunk(s), h))
    return pl.pallas_call(
        body, name=name, grid=(H // hb, nT),
        in_specs=[pl.BlockSpec(memory_space=pltpu.SMEM), qspec,
                  pl.BlockSpec((C, hb * dk), lambda h, s: (chunk(s), H // hb + h)), vspec, vspec,
                  pl.BlockSpec((hb, None, dk, dv), lambda h, s: (h, step(s), 0, 0))],
        out_specs=[qspec, qspec, vspec, pl.BlockSpec((hb, 8, LANES), lambda h, s: (h, 0, 0))],
        out_shape=[jax.ShapeDtypeStruct((T, H * dk), BF16), jax.ShapeDtypeStruct((T, H * dk), BF16),
                   jax.ShapeDtypeStruct((T, H * dv), BF16), jax.ShapeDtypeStruct((H, 8, LANES), F32)],
        scratch_shapes=[pltpu.VMEM((hb, dk, dv), F32)],
        compiler_params=_cparams("parallel", "arbitrary"),
    )(lg, qk, qk, vg, do, states)


def _readout(o_f, o_b, vg, *, T, L, H, tr, name, dep=None):
    dv = 4 * LANES
    W = H * dv

    def body(is_ctx, of, ob, g):
        o = of + ob
        parts = []
        for h in range(H):
            oh = o[:, h * dv:(h + 1) * dv]
            parts.append(oh * lax.rsqrt(jnp.mean(oh * oh, axis=-1, keepdims=True) + NORM_EPS))
        y = jnp.concatenate(parts, axis=1)
        return (_silu(g.astype(F32)) * y,), ()

    return _rowwise(body, [(o_f, W, 0, False), (o_b, W, 0, False), (vg, W, 1, False)], [], [(W, BF16)], 0,
                    R=T, L=L, tr=tr, name=name, dep=dep)[0]


def _readout_bwd(dz, o_f, o_b, vg, *, T, L, H, tr, name):
    dv = 4 * LANES
    W = H * dv

    def body(is_ctx, dzt, of, ob, g):
        o = of + ob
        gf = g.astype(F32)
        sg = jax.nn.sigmoid(gf)
        dzf = dzt.astype(F32)
        dy = dzf * (gf * sg)
        ys, dos = [], []
        for h in range(H):
            sl = slice(h * dv, (h + 1) * dv)
            oh, dyh = o[:, sl], dy[:, sl]
            r = lax.rsqrt(jnp.mean(oh * oh, axis=-1, keepdims=True) + NORM_EPS)
            yh = oh * r
            ys.append(yh)
            dos.append(r * (dyh - yh * jnp.mean(dyh * yh, axis=-1, keepdims=True)))
        y = jnp.concatenate(ys, axis=1)
        dg = dzf * y * (sg * (1.0 + gf * (1.0 - sg)))
        return (jnp.concatenate(dos, axis=1), dg), ()

    return _rowwise(body, [(dz, W, 0, False), (o_f, W, 0, False), (o_b, W, 0, False), (vg, W, 1, False)], [],
                    [(W, BF16), (W, BF16)], 0, R=T, L=L, tr=tr, name=name)


def _ret_assemble(dq_f, dq_b, dk_f, dk_b, dv_f, dv_b, dg, cos, sin, *, T, L, H, tr, name):
    dk, dv = 2 * LANES, 4 * LANES
    Wq, Wv = H * dk, H * dv
    kscale = float(dk) ** -0.5

    def unrope(d, c, s_, scale):
        parts = []
        for h in range(H):
            d1, d2 = d[:, h * dk:h * dk + LANES], d[:, h * dk + LANES:(h + 1) * dk]
            parts += [(d1 * c + d2 * s_) * scale, (d2 * c - d1 * s_) * scale]
        return jnp.concatenate(parts, axis=1)

    def body(is_ctx, qf, qb, kf, kb, vf, vb, g, c, s_):
        add = lambda a, b: a.astype(F32) + b.astype(F32)
        dq = unrope(add(qf, qb), c, s_, 1.0)
        dkk = unrope(add(kf, kb), c, s_, kscale)
        return (jnp.concatenate([dq.astype(BF16), dkk.astype(BF16), add(vf, vb).astype(BF16), g], axis=1),), ()

    rows = [(dq_f, Wq, 0, False), (dq_b, Wq, 0, False), (dk_f, Wq, 0, False), (dk_b, Wq, 0, False),
            (dv_f, Wv, 0, False), (dv_b, Wv, 0, False), (dg, Wv, 0, False),
            (cos, LANES, 0, False), (sin, LANES, 0, False)]
    return _rowwise(body, rows, [], [(2 * Wq + 2 * Wv, BF16)], 0, R=T, L=L, tr=tr, name=name)[0]


def _swap32(x):
    n = x.shape[-1]
    lane = lax.broadcasted_iota(jnp.int32, x.shape, x.ndim - 1)
    return jnp.where(lane % 64 < 32, pltpu.roll(x, n - 32, x.ndim - 1), pltpu.roll(x, 32, x.ndim - 1))


def _stack_heads(ref, G):
    return jnp.concatenate([ref[:, g * LANES:(g + 1) * LANES] for g in range(G)], axis=0)


def _stack_columns(ref, G):
    return jnp.concatenate([ref[:, g:g + 1] for g in range(G)], axis=0)


def _sink_column(sink_ref, h, G):
    return jnp.concatenate([jnp.full((CHUNK, 1), sink_ref[0, h * G + g], F32) for g in range(G)], axis=0)


def _key_mask(n, nb, CTX, G):
    W = 3 * CHUNK + CTX
    ii = lax.broadcasted_iota(jnp.int32, (G * CHUNK, W), 0) & (CHUNK - 1)
    col = lax.broadcasted_iota(jnp.int32, (G * CHUNK, W), 1)
    is_prev = col < CHUNK
    is_next = (col >= 2 * CHUNK) & (col < 3 * CHUNK)
    prev_ok = is_prev & (col >= ii) & (n > 0)
    next_ok = is_next & ((col - 2 * CHUNK) <= ii) & (n < nb - 1)
    return prev_ok | next_ok | jnp.logical_not(is_prev | is_next)


def _attn_kv_specs(L, CTX, nb):
    blk = lambda f: pl.BlockSpec((CHUNK, LANES), lambda h, n: (f(n), h))
    prev_, cur_, next_ = (lambda n: jnp.maximum(n - 1, 0)), (lambda n: n), (lambda n: jnp.minimum(n + 1, nb - 1))
    ctx_spec = pl.BlockSpec((CTX, LANES), lambda h, n: (L // CTX, h))
    return [blk(prev_), blk(cur_), blk(next_), ctx_spec]


def _attn_fwd(q, k, v, sink, *, L, CTX, Hkv, G, name):
    nb = L // CHUNK
    scale = float(ATT_HEAD_DIM) ** -0.5
    kvs = _attn_kv_specs(L, CTX, nb)

    def body(sink_ref, q_ref, kp, kc, kn, kx, vp, vc, vn, vx, o_ref, lse_ref):
        h, n = pl.program_id(0), pl.program_id(1)
        qs = _stack_heads(q_ref, G)
        kall = jnp.concatenate([kp[...], kc[...], kn[...], kx[...]], axis=0)
        vall = jnp.concatenate([vp[...], vc[...], vn[...], vx[...]], axis=0)
        s_ = jnp.where(_key_mask(n, nb, CTX, G), _dot(qs, kall, _NT) * scale, NEG_INF)
        sk = _sink_column(sink_ref, h, G)
        m = jnp.maximum(jnp.max(s_, axis=-1, keepdims=True), sk)
        p = jnp.exp(s_ - m)
        den = jnp.sum(p, axis=-1, keepdims=True) + jnp.exp(sk - m)
        o = _dot(p.astype(BF16), vall) / den
        lse = m + jnp.log(den)
        for g in range(G):
            o_ref[:, g * LANES:(g + 1) * LANES] = o[g * CHUNK:(g + 1) * CHUNK].astype(o_ref.dtype)
            lse_ref[:, g:g + 1] = lse[g * CHUNK:(g + 1) * CHUNK]

    return pl.pallas_call(
        body, name=name, grid=(Hkv, nb),
        in_specs=[pl.BlockSpec(memory_space=pltpu.SMEM),
                  pl.BlockSpec((CHUNK, G * LANES), lambda h, n: (n, h))] + kvs + kvs,
        out_specs=[pl.BlockSpec((CHUNK, G * LANES), lambda h, n: (n, h)),
                   pl.BlockSpec((None, CHUNK, G), lambda h, n: (h, n, 0))],
        out_shape=[jax.ShapeDtypeStruct((L, Hkv * G * LANES), BF16), jax.ShapeDtypeStruct((Hkv, L, G), F32)],
        compiler_params=_cparams("parallel", "parallel"),
    )(sink, q, k, k, k, k, v, v, v, v)


def _attn_bwd(q, k, v, do, lse, sink, *, L, CTX, Hkv, G, name):
    nb = L // CHUNK
    scale = float(ATT_HEAD_DIM) ** -0.5
    kvs = _attn_kv_specs(L, CTX, nb)
    qspec = pl.BlockSpec((CHUNK, G * LANES), lambda h, n: (n, h))
    rowspec = pl.BlockSpec((None, CHUNK, G), lambda h, n: (h, n, 0))
    colspec = lambda rows: pl.BlockSpec((rows, LANES), lambda h, n: (0, h))

    def body(sink_ref, q_ref, do_ref, lse_ref, kp, kc, kn, kx, vp, vc, vn, vx,
             dq_ref, dk_ref, dv_ref, dkx_ref, dvx_ref, dsk_ref):
        h, n = pl.program_id(0), pl.program_id(1)

        @pl.when(n == 0)
        def _():
            for r in (dk_ref, dv_ref, dkx_ref, dvx_ref, dsk_ref):
                r[...] = jnp.zeros_like(r)

        qs, dos = _stack_heads(q_ref, G), _stack_heads(do_ref, G)
        kall = jnp.concatenate([kp[...], kc[...], kn[...], kx[...]], axis=0)
        vall = jnp.concatenate([vp[...], vc[...], vn[...], vx[...]], axis=0)
        lse_c = _stack_columns(lse_ref, G)
        p = jnp.where(_key_mask(n, nb, CTX, G), jnp.exp(_dot(qs, kall, _NT) * scale - lse_c), 0.0)
        dp = _dot(dos, vall, _NT)
        delta = jnp.sum(p * dp, axis=-1, keepdims=True)
        ds = (p * (dp - delta) * scale).astype(BF16)
        dq = _dot(ds, kall)
        dk_all = _dot(ds, qs, _TN)
        dv_all = _dot(p.astype(BF16), dos, _TN)
        for g in range(G):
            dq_ref[:, g * LANES:(g + 1) * LANES] = dq[g * CHUNK:(g + 1) * CHUNK]
        for part, blk in enumerate((jnp.maximum(n - 1, 0), n, jnp.minimum(n + 1, nb - 1))):
            rows = pl.ds(pl.multiple_of(blk * CHUNK, CHUNK), CHUNK)
            dk_ref[rows, :] += dk_all[part * CHUNK:(part + 1) * CHUNK]
            dv_ref[rows, :] += dv_all[part * CHUNK:(part + 1) * CHUNK]
        dkx_ref[...] += dk_all[3 * CHUNK:]
        dvx_ref[...] += dv_all[3 * CHUNK:]
        dsink = -jnp.exp(_sink_column(sink_ref, h, G) - lse_c) * delta
        for g in range(G):
            dsk_ref[g:g + 1, :] += jnp.sum(dsink[g * CHUNK:(g + 1) * CHUNK])

    return pl.pallas_call(
        body, name=name, grid=(Hkv, nb),
        in_specs=[pl.BlockSpec(memory_space=pltpu.SMEM), qspec, qspec, rowspec] + kvs + kvs,
        out_specs=[qspec, colspec(L), colspec(L), colspec(CTX), colspec(CTX),
                   pl.BlockSpec((None, 8, LANES), lambda h, n: (h, 0, 0))],
        out_shape=[jax.ShapeDtypeStruct((L, Hkv * G * LANES), F32),
                   jax.ShapeDtypeStruct((L, Hkv * LANES), F32), jax.ShapeDtypeStruct((L, Hkv * LANES), F32),
                   jax.ShapeDtypeStruct((CTX, Hkv * LANES), F32), jax.ShapeDtypeStruct((CTX, Hkv * LANES), F32),
                   jax.ShapeDtypeStruct((Hkv, 8, LANES), F32)],
        compiler_params=_cparams("parallel", "arbitrary"),
    )(sink, q, do, lse, k, k, k, k, v, v, v, v)


def _attn_assemble(dq, dk_lat, dv_lat, dk_ctx, dv_ctx, cos, sin, *, T, L, CTX, Hq, Hkv, tr, name):
    Wq, Wk = Hq * LANES, Hkv * LANES
    ctx_blocks = CTX // tr
    nl = L // tr

    def unrope(d, c, s_, heads):
        return d * jnp.tile(c, (1, heads)) + _swap32(d * jnp.tile(s_, (1, heads)))

    def body(is_ctx, dqt, dkl, dvl, dkc, dvc, c, s_):
        dq_ = jnp.where(is_ctx, 0.0, unrope(dqt, c, s_, Hq))
        dk_ = unrope(jnp.where(is_ctx, dkc, dkl), c, s_, Hkv)
        dv_ = jnp.where(is_ctx, dvc, dvl)
        return (jnp.concatenate([dq_, dk_, dv_], axis=1),), ()

    def ctx_map(i):
        return (jnp.clip(i - nl, 0, ctx_blocks - 1), 0)

    assert T % tr == 0 and L % tr == 0 and CTX % tr == 0
    in_specs = [pl.BlockSpec((tr, Wq), lambda i: (jnp.minimum(i, nl - 1), 0)),
                pl.BlockSpec((tr, Wk), lambda i: (jnp.minimum(i, nl - 1), 0)),
                pl.BlockSpec((tr, Wk), lambda i: (jnp.minimum(i, nl - 1), 0)),
                pl.BlockSpec((tr, Wk), ctx_map), pl.BlockSpec((tr, Wk), ctx_map),
                pl.BlockSpec((tr, LANES), lambda i: (i, 0)), pl.BlockSpec((tr, LANES), lambda i: (i, 0))]

    def kern(dq_r, dkl_r, dvl_r, dkc_r, dvc_r, c_r, s_r, o_ref):
        is_ctx = pl.program_id(0) >= nl
        (out,), _ = body(is_ctx, dq_r[...], dkl_r[...], dvl_r[...], dkc_r[...], dvc_r[...], c_r[...], s_r[...])
        o_ref[...] = out.astype(o_ref.dtype)

    return pl.pallas_call(
        kern, name=name, grid=(T // tr,), in_specs=in_specs,
        out_specs=pl.BlockSpec((tr, Wq + 2 * Wk), lambda i: (i, 0)),
        out_shape=jax.ShapeDtypeStruct((T, Wq + 2 * Wk), BF16),
        compiler_params=_cparams("parallel"),
    )(dq, dk_lat, dv_lat, dk_ctx, dv_ctx, cos, sin)


def _my_place():
    x, y, c = lax.axis_index("x"), lax.axis_index("y"), lax.axis_index("c")
    return x, y, c


def _all_gather_small(v, *, name):
    R, C = v.shape

    def body(x_ref, out_ref, send_sems, recv_sems, local_sem):
        x, y, c = _my_place()
        me, sibling = (x, y, c), (x, y, 1 - c)
        chips = [(1 - x, y), (x, 1 - y), (1 - x, 1 - y)]

        def slot(px, py, pc):
            return out_ref.at[4 * px + 2 * py + pc]

        def copy(k, block, to, src=None):
            return pltpu.make_async_remote_copy(
                src_ref=slot(*block) if src is None else src, dst_ref=slot(*block),
                send_sem=send_sems.at[k], recv_sem=recv_sems.at[k], device_id=to, device_id_type=MESH)

        mine = pltpu.make_async_copy(x_ref, slot(*me), local_sem)
        mine.start()
        first = [copy(0, me, sibling, src=x_ref)]
        first += [copy(1 + j, me, (*chip, c), src=x_ref) for j, chip in enumerate(chips)]
        for cp in first:
            cp.start()
        passed = [copy(4 + j, (*chip, c), sibling) for j, chip in enumerate(chips)]
        for j, chip in enumerate(chips):
            copy(1 + j, (*chip, c), me).wait_recv()
            passed[j].start()
        copy(0, sibling, me).wait_recv()
        for j, chip in enumerate(chips):
            copy(4 + j, (*chip, 1 - c), me).wait_recv()
        for cp in first + passed:
            cp.wait_send()
        mine.wait()

    return pl.pallas_call(
        body, name=name, out_shape=jax.ShapeDtypeStruct((N_DEV, R, C), v.dtype),
        in_specs=[pl.BlockSpec(memory_space=pltpu.VMEM)], out_specs=pl.BlockSpec(memory_space=pltpu.VMEM),
        scratch_shapes=[pltpu.SemaphoreType.DMA((7,)), pltpu.SemaphoreType.DMA((7,)), pltpu.SemaphoreType.DMA],
    )(v)


def _shard_slice(ref, axis, idx, size):
    if axis == 1:
        return ref.at[:, pl.ds(idx * size, size), :]
    return ref.at[:, :, pl.ds(idx * size, size)]


def _all_gather_weights(shards, axes, *, name):
    nt = len(shards)
    sizes = [s.shape[a] for s, a in zip(shards, axes)]
    out_shape = []
    for s, a in zip(shards, axes):
        shp = list(s.shape)
        shp[a] *= N_DEV
        out_shape.append(jax.ShapeDtypeStruct(tuple(shp), s.dtype))

    def body(*refs):
        ins, outs = refs[:nt], refs[nt:2 * nt]
        send_sems, recv_sems, local_sems = refs[2 * nt:]
        x, y, c = _my_place()
        me, sibling = (x, y, c), (x, y, 1 - c)
        chips = [(1 - x, y), (x, 1 - y), (1 - x, 1 - y)]
        all_sends = []
        locals_ = []
        for t in range(nt):
            def slot(px, py, pc, t=t):
                return _shard_slice(outs[t], axes[t], 4 * px + 2 * py + pc, sizes[t])

            def copy(k, block, to, src=None, t=t, slot=slot):
                return pltpu.make_async_remote_copy(
                    src_ref=slot(*block) if src is None else src, dst_ref=slot(*block),
                    send_sem=send_sems.at[t, k], recv_sem=recv_sems.at[t, k], device_id=to, device_id_type=MESH)

            mine = pltpu.make_async_copy(ins[t], slot(*me), local_sems.at[t])
            mine.start()
            locals_.append(mine)
            first = [copy(0, me, sibling, src=ins[t])]
            first += [copy(1 + j, me, (*chip, c), src=ins[t]) for j, chip in enumerate(chips)]
            for cp in first:
                cp.start()
            all_sends += first
        for t in range(nt):
            def slot(px, py, pc, t=t):
                return _shard_slice(outs[t], axes[t], 4 * px + 2 * py + pc, sizes[t])

            def copy(k, block, to, t=t, slot=slot):
                return pltpu.make_async_remote_copy(
                    src_ref=slot(*block), dst_ref=slot(*block),
                    send_sem=send_sems.at[t, k], recv_sem=recv_sems.at[t, k], device_id=to, device_id_type=MESH)

            passed = [copy(4 + j, (*chip, c), sibling) for j, chip in enumerate(chips)]
            for j, chip in enumerate(chips):
                copy(1 + j, (*chip, c), me).wait_recv()
                passed[j].start()
            all_sends += passed
        for t in range(nt):
            def slot(px, py, pc, t=t):
                return _shard_slice(outs[t], axes[t], 4 * px + 2 * py + pc, sizes[t])

            def copy(k, block, to, t=t, slot=slot):
                return pltpu.make_async_remote_copy(
                    src_ref=slot(*block), dst_ref=slot(*block),
                    send_sem=send_sems.at[t, k], recv_sem=recv_sems.at[t, k], device_id=to, device_id_type=MESH)

            copy(0, sibling, me).wait_recv()
            for j, chip in enumerate(chips):
                copy(4 + j, (*chip, 1 - c), me).wait_recv()
        for cp in all_sends:
            cp.wait_send()
        for mine in locals_:
            mine.wait()

    return pl.pallas_call(
        body, name=name, out_shape=out_shape,
        in_specs=[pl.BlockSpec(memory_space=pl.ANY)] * nt, out_specs=[pl.BlockSpec(memory_space=pl.ANY)] * nt,
        scratch_shapes=[pltpu.SemaphoreType.DMA((nt, 7)), pltpu.SemaphoreType.DMA((nt, 7)),
                        pltpu.SemaphoreType.DMA((nt,))],
    )(*shards)


def _exchange_grads(grads, axes, *, name):
    nt = len(grads)
    sizes = [g.shape[a] // N_DEV for g, a in zip(grads, axes)]
    out_shape = []
    for g, a, sz in zip(grads, axes, sizes):
        shp = list(g.shape)
        shp[a] = sz
        out_shape.append(jax.ShapeDtypeStruct((N_DEV, *shp), g.dtype))

    def body(*refs):
        ins, outs = refs[:nt], refs[nt:2 * nt]
        send_sems, recv_sems, local_sems = refs[2 * nt:]
        x, y, c = _my_place()
        my_idx = 4 * x + 2 * y + c

        def peer(r):
            px = (1 - x) if r & 4 else x
            py = (1 - y) if r & 2 else y
            pc = (1 - c) if r & 1 else c
            return (px, py, pc)

        copies, locals_ = [], []
        for t in range(nt):
            mine = pltpu.make_async_copy(_shard_slice(ins[t], axes[t], my_idx, sizes[t]), outs[t].at[my_idx],
                                         local_sems.at[t])
            mine.start()
            locals_.append(mine)
            for r in range(1, N_DEV):
                p = peer(r)
                p_idx = 4 * p[0] + 2 * p[1] + p[2]
                cp = pltpu.make_async_remote_copy(
                    src_ref=_shard_slice(ins[t], axes[t], p_idx, sizes[t]), dst_ref=outs[t].at[my_idx],
                    send_sem=send_sems.at[t, r - 1], recv_sem=recv_sems.at[t, r - 1], device_id=p, device_id_type=MESH)
                cp.start()
                copies.append((cp, t, r, p_idx))
        for cp, t, r, p_idx in copies:
            pltpu.make_async_remote_copy(
                src_ref=_shard_slice(ins[t], axes[t], p_idx, sizes[t]), dst_ref=outs[t].at[p_idx],
                send_sem=send_sems.at[t, r - 1], recv_sem=recv_sems.at[t, r - 1], device_id=peer(r),
                device_id_type=MESH).wait_recv()
        for cp, _, _, _ in copies:
            cp.wait_send()
        for mine in locals_:
            mine.wait()

    return pl.pallas_call(
        body, name=name, out_shape=out_shape,
        in_specs=[pl.BlockSpec(memory_space=pl.ANY)] * nt, out_specs=[pl.BlockSpec(memory_space=pl.ANY)] * nt,
        scratch_shapes=[pltpu.SemaphoreType.DMA((nt, 7)), pltpu.SemaphoreType.DMA((nt, 7)),
                        pltpu.SemaphoreType.DMA((nt,))],
    )(*grads)


_HBM_SPEC = pl.BlockSpec(memory_space=pltpu.HBM)
_SEM_SPEC = pl.BlockSpec(memory_space=pltpu.SEMAPHORE)
_ANY_SPEC = pl.BlockSpec(memory_space=pl.ANY)
_DATAFLOW = pltpu.SideEffectType.DATAFLOW_SIDE_EFFECTING
N_PEERS = N_DEV - 1


def _peer(r):
    x, y, c = _my_place()
    return ((1 - x) if r & 4 else x, (1 - y) if r & 2 else y, (1 - c) if r & 1 else c)


def _index_of(place):
    return 4 * place[0] + 2 * place[1] + place[2]


def _slot(ref, axis, idx, size):
    if axis == 0:
        return ref.at[pl.ds(idx * size, size), :]
    return ref.at[:, pl.ds(idx * size, size)]


def _cast_place(w3, layer, axis, me_arr, *, name):
    Ks, Ns = w3.shape[1], w3.shape[2]
    tr = _pick(Ks, 256, 16)
    nblk = Ks // tr
    full = (Ks * N_DEV, Ns) if axis == 0 else (Ks, Ns * N_DEV)
    if axis == 0:
        out_map = lambda i, me: (me[0] * nblk + i, 0)
    else:
        out_map = lambda i, me: (i, me[0])

    def body(me_ref, w_ref, o_ref):
        o_ref[...] = w_ref[...].astype(BF16)

    return pl.pallas_call(
        body, name=name, out_shape=jax.ShapeDtypeStruct(full, BF16),
        grid_spec=pltpu.PrefetchScalarGridSpec(
            num_scalar_prefetch=1, grid=(nblk,),
            in_specs=[pl.BlockSpec((None, tr, Ns), lambda i, me: (layer, i, 0))],
            out_specs=pl.BlockSpec((tr, Ns), out_map)),
        compiler_params=_cparams("parallel"),
    )(me_arr, w3)


AG_FIRST = 4
AG_CHIPS = 3


def _sibling():
    x, y, c = _my_place()
    return (x, y, 1 - c)


def _chip_peer(j, same_core=True):
    x, y, c = _my_place()
    px = (1 - x) if j in (0, 2) else x
    py = (1 - y) if j in (1, 2) else y
    return (px, py, c if same_core else 1 - c)


def _gather_start(lands, axes, after, *, name):
    nt = len(lands)
    sizes = [l.shape[a] // N_DEV for l, a in zip(lands, axes)]

    def body(*refs):
        ins, send_sems, recv_sems, token = refs[:nt], refs[nt + 1], refs[nt + 2], refs[-1]
        my_idx = _index_of(_my_place())
        for t in range(nt):
            mine = _slot(ins[t], axes[t], my_idx, sizes[t])
            for k, to in enumerate([_sibling()] + [_chip_peer(j) for j in range(AG_CHIPS)]):
                pltpu.make_async_remote_copy(src_ref=mine, dst_ref=mine, send_sem=send_sems.at[t * AG_FIRST + k],
                                             recv_sem=recv_sems.at[t * AG_FIRST + k], device_id=to,
                                             device_id_type=MESH).start()
        token[...] = jnp.zeros_like(token)

    res = pl.pallas_call(
        body, name=name,
        out_shape=(pltpu.SemaphoreType.DMA((nt * AG_FIRST,)), pltpu.SemaphoreType.DMA((nt * AG_FIRST,)),
                   *[pltpu.HBM(l.shape, l.dtype) for l in lands], jax.ShapeDtypeStruct((8, LANES), F32)),
        in_specs=[_HBM_SPEC] * nt + [_ANY_SPEC],
        out_specs=(_SEM_SPEC, _SEM_SPEC, *[_HBM_SPEC] * nt, pl.BlockSpec(memory_space=pltpu.VMEM)),
        input_output_aliases={t: 2 + t for t in range(nt)},
        compiler_params=pltpu.CompilerParams(has_side_effects=_DATAFLOW),
    )(*[pltpu.with_memory_space_constraint(l, pltpu.HBM) for l in lands], after)
    return res[0], res[1], list(res[2:2 + nt]), res[-1]


def _gather_forward(send_a, recv_a, lands, axes, after, *, name):
    nt = len(lands)
    sizes = [l.shape[a] // N_DEV for l, a in zip(lands, axes)]

    def body(*refs):
        ins, send_a, recv_a = refs[:nt], refs[nt], refs[nt + 1]
        send_f, recv_f, token = refs[nt + 3], refs[nt + 4], refs[-1]
        my_idx = _index_of(_my_place())
        for t in range(nt):
            for j in range(AG_CHIPS):
                src_dev = _chip_peer(j)
                arrived = _slot(ins[t], axes[t], _index_of(src_dev), sizes[t])
                pltpu.make_async_remote_copy(
                    src_ref=_slot(ins[t], axes[t], my_idx, sizes[t]), dst_ref=arrived,
                    send_sem=send_a.at[t * AG_FIRST + 1 + j], recv_sem=recv_a.at[t * AG_FIRST + 1 + j],
                    device_id=src_dev, device_id_type=MESH).wait_recv()
                pltpu.make_async_remote_copy(src_ref=arrived, dst_ref=arrived, send_sem=send_f.at[t * AG_CHIPS + j],
                                             recv_sem=recv_f.at[t * AG_CHIPS + j], device_id=_sibling(),
                                             device_id_type=MESH).start()
        token[...] = jnp.zeros_like(token)

    res = pl.pallas_call(
        body, name=name,
        out_shape=(pltpu.SemaphoreType.DMA((nt * AG_CHIPS,)), pltpu.SemaphoreType.DMA((nt * AG_CHIPS,)),
                   *[pltpu.HBM(l.shape, l.dtype) for l in lands], jax.ShapeDtypeStruct((8, LANES), F32)),
        in_specs=[_HBM_SPEC] * nt + [_SEM_SPEC, _SEM_SPEC, _ANY_SPEC],
        out_specs=(_SEM_SPEC, _SEM_SPEC, *[_HBM_SPEC] * nt, pl.BlockSpec(memory_space=pltpu.VMEM)),
        input_output_aliases={t: 2 + t for t in range(nt)},
        compiler_params=pltpu.CompilerParams(has_side_effects=_DATAFLOW),
    )(*lands, send_a, recv_a, after)
    return res[0], res[1], list(res[2:2 + nt]), res[-1]


def _gather_wait(send_a, recv_a, send_f, recv_f, lands, axes, after, *, name):
    nt = len(lands)
    sizes = [l.shape[a] // N_DEV for l, a in zip(lands, axes)]

    def body(*refs):
        ins, send_a, recv_a, send_f, recv_f = refs[:nt], refs[nt], refs[nt + 1], refs[nt + 2], refs[nt + 3]
        my_idx = _index_of(_my_place())
        sib = _sibling()
        for t in range(nt):
            mine = _slot(ins[t], axes[t], my_idx, sizes[t])
            for k, to in enumerate([sib] + [_chip_peer(j) for j in range(AG_CHIPS)]):
                pltpu.make_async_remote_copy(src_ref=mine, dst_ref=mine, send_sem=send_a.at[t * AG_FIRST + k],
                                             recv_sem=recv_a.at[t * AG_FIRST + k], device_id=to,
                                             device_id_type=MESH).wait_send()
            pltpu.make_async_remote_copy(src_ref=mine, dst_ref=_slot(ins[t], axes[t], _index_of(sib), sizes[t]),
                                         send_sem=send_a.at[t * AG_FIRST], recv_sem=recv_a.at[t * AG_FIRST],
                                         device_id=sib, device_id_type=MESH).wait_recv()
            for j in range(AG_CHIPS):
                sent = _slot(ins[t], axes[t], _index_of(_chip_peer(j)), sizes[t])
                got = _slot(ins[t], axes[t], _index_of(_chip_peer(j, same_core=False)), sizes[t])
                cp = pltpu.make_async_remote_copy(src_ref=sent, dst_ref=got, send_sem=send_f.at[t * AG_CHIPS + j],
                                                  recv_sem=recv_f.at[t * AG_CHIPS + j], device_id=sib,
                                                  device_id_type=MESH)
                cp.wait_send()
                cp.wait_recv()

    res = pl.pallas_call(
        body, name=name, out_shape=[pltpu.HBM(l.shape, l.dtype) for l in lands],
        in_specs=[_HBM_SPEC] * nt + [_SEM_SPEC] * 4 + [_ANY_SPEC], out_specs=[_HBM_SPEC] * nt,
        input_output_aliases={t: t for t in range(nt)},
        compiler_params=pltpu.CompilerParams(has_side_effects=_DATAFLOW),
    )(*lands, send_a, recv_a, send_f, recv_f, after)
    return list(res)


def _scatter_start(dw, axis, *, name):
    size = dw.shape[axis] // N_DEV
    land_shape = (N_PEERS, size, dw.shape[1]) if axis == 0 else (N_PEERS, dw.shape[0], size)

    def body(dw_ref, land_ref, send_sems, recv_sems, dw_thru, land_thru, token):
        for r in range(1, N_DEV):
            p = _peer(r)
            pltpu.make_async_remote_copy(src_ref=_slot(dw_ref, axis, _index_of(p), size), dst_ref=land_ref.at[r - 1],
                                         send_sem=send_sems.at[r - 1], recv_sem=recv_sems.at[r - 1], device_id=p,
                                         device_id_type=MESH).start()
        token[...] = jnp.zeros_like(token)

    land = pltpu.with_memory_space_constraint(lax.empty(land_shape, dw.dtype), pltpu.HBM)
    return pl.pallas_call(
        body, name=name,
        out_shape=(pltpu.SemaphoreType.DMA((N_PEERS,)), pltpu.SemaphoreType.DMA((N_PEERS,)),
                   pltpu.HBM(dw.shape, dw.dtype), pltpu.HBM(land_shape, dw.dtype), jax.ShapeDtypeStruct((8, LANES), F32)),
        in_specs=[_HBM_SPEC, _HBM_SPEC],
        out_specs=(_SEM_SPEC, _SEM_SPEC, _HBM_SPEC, _HBM_SPEC, pl.BlockSpec(memory_space=pltpu.VMEM)),
        input_output_aliases={0: 2, 1: 3},
        compiler_params=pltpu.CompilerParams(has_side_effects=_DATAFLOW),
    )(pltpu.with_memory_space_constraint(dw, pltpu.HBM), land)


def _scatter_wait(send_sems, recv_sems, dw, land, axis, after, *, name):
    size = dw.shape[axis] // N_DEV

    def body(dw_ref, land_ref, send_sems, recv_sems, after_ref, dw_thru, land_thru):
        for r in range(1, N_DEV):
            p = _peer(r)
            cp = pltpu.make_async_remote_copy(src_ref=_slot(dw_ref, axis, _index_of(p), size), dst_ref=land_ref.at[r - 1],
                                              send_sem=send_sems.at[r - 1], recv_sem=recv_sems.at[r - 1], device_id=p,
                                              device_id_type=MESH)
            cp.wait_send()
            cp.wait_recv()

    return pl.pallas_call(
        body, name=name, out_shape=(pltpu.HBM(dw.shape, dw.dtype), pltpu.HBM(land.shape, land.dtype)),
        in_specs=[_HBM_SPEC, _HBM_SPEC, _SEM_SPEC, _SEM_SPEC, _ANY_SPEC], out_specs=(_HBM_SPEC, _HBM_SPEC),
        input_output_aliases={0: 0, 1: 1},
        compiler_params=pltpu.CompilerParams(has_side_effects=_DATAFLOW),
    )(dw, land, send_sems, recv_sems, after)


def _adamw_math(w, g, m, v):
    m = ADAM_B1 * m + (1.0 - ADAM_B1) * g
    v = ADAM_B2 * v + (1.0 - ADAM_B2) * (g * g)
    m_hat = m / (1.0 - ADAM_B1 ** ADAM_STEP)
    v_hat = v / (1.0 - ADAM_B2 ** ADAM_STEP)
    delta = -ADAM_LR * (m_hat / (jnp.sqrt(v_hat) + ADAM_EPS) + ADAM_WD * w)
    return delta, m, v


def _adamw_sharded(w, m, v, layer, dw, land, axis, me_arr, prev, *, name):
    nl, Ks, Ns = w.shape
    tr = _pick(Ks, 128, 16)
    nblk = Ks // tr
    if axis == 0:
        own_map = lambda i, me: (me[0] * nblk + i, 0)
    else:
        own_map = lambda i, me: (i, me[0])
    wspec = pl.BlockSpec((None, tr, Ns), lambda i, me: (layer, i, 0))
    n_prev = 0 if prev is None else 4

    def body(me_ref, w_ref, m_ref, v_ref, own_ref, r_ref, *rest):
        g_ref, d_ref, nm_ref, nv_ref = rest[n_prev:]
        g = own_ref[...].astype(F32)
        for r in range(N_PEERS):
            g = g + r_ref[r].astype(F32)
        delta, nm, nv = _adamw_math(w_ref[...], g, m_ref[...], v_ref[...])
        g_ref[...], d_ref[...], nm_ref[...], nv_ref[...] = g, delta, nm, nv

    return pl.pallas_call(
        body, name=name, out_shape=[jax.ShapeDtypeStruct((nl, Ks, Ns), F32)] * 4,
        grid_spec=pltpu.PrefetchScalarGridSpec(
            num_scalar_prefetch=1, grid=(nblk,),
            in_specs=[wspec, wspec, wspec, pl.BlockSpec((tr, Ns), own_map),
                      pl.BlockSpec((N_PEERS, tr, Ns), lambda i, me: (0, i, 0))] + [_ANY_SPEC] * n_prev,
            out_specs=[wspec] * 4),
        input_output_aliases={6 + k: k for k in range(n_prev)},
        compiler_params=_cparams("parallel"),
    )(me_arr, w, m, v, dw, land, *(prev or []))


def _adamw_flat(w, g, m, v, *, name):
    def body(w_ref, g_ref, m_ref, v_ref, d_ref, nm_ref, nv_ref):
        d_ref[...], nm_ref[...], nv_ref[...] = _adamw_math(w_ref[...], g_ref[...], m_ref[...], v_ref[...])

    spec = pl.BlockSpec(memory_space=pltpu.VMEM)
    return pl.pallas_call(body, name=name, in_specs=[spec] * 4, out_specs=[spec] * 3,
                          out_shape=[jax.ShapeDtypeStruct(w.shape, F32)] * 3)(w, g, m, v)


def _sum_devices(a, *, name):
    def body(a_ref, o_ref):
        s = a_ref[0]
        for d in range(1, N_DEV):
            s = s + a_ref[d]
        o_ref[...] = s

    spec = pl.BlockSpec(memory_space=pltpu.VMEM)
    return pl.pallas_call(body, name=name, in_specs=[spec], out_specs=spec,
                          out_shape=jax.ShapeDtypeStruct(a.shape[1:], F32))(a)


def _ada_mods(c16, ada_w, ada_b_cols, *, name):
    nl, D, cols = ada_w.shape
    bn = _pick(cols, 512)

    def body(c_ref, w_ref, b_ref, o_ref):
        cond = _silu(c_ref[...]).astype(BF16)
        o_ref[...] = _dot(cond, w_ref[...].astype(BF16)) + b_ref[...]

    return pl.pallas_call(
        body, name=name, grid=(nl, cols // bn),
        in_specs=[pl.BlockSpec((16, D), lambda l, j: (0, 0)), pl.BlockSpec((None, D, bn), lambda l, j: (l, 0, j)),
                  pl.BlockSpec((None, 1, bn), lambda l, j: (l, 0, j))],
        out_specs=pl.BlockSpec((None, 16, bn), lambda l, j: (l, 0, j)),
        out_shape=jax.ShapeDtypeStruct((nl, 16, cols), F32),
        compiler_params=_cparams("parallel", "parallel"),
    )(c16, ada_w, ada_b_cols)


def _ada_bwd(cond_t, dmod, w, m, v, *, name):
    nl, D, cols = w.shape
    tr = _pick(D, 256, 8)

    def body(ct_ref, dm_ref, w_ref, m_ref, v_ref, g_ref, d_ref, nm_ref, nv_ref, dc_ref):
        ct, dm, wt = ct_ref[...], dm_ref[...], w_ref[...]
        g = ct[:, 0:1] * dm[0:1, :]
        for r in range(1, N_DEV + 1):
            g = g + ct[:, r:r + 1] * dm[r:r + 1, :]
        delta, nm, nv = _adamw_math(wt, g, m_ref[...], v_ref[...])
        g_ref[...], d_ref[...], nm_ref[...], nv_ref[...] = g, delta, nm, nv
        dc_ref[...] = jnp.sum(wt * dm[N_DEV:N_DEV + 1, :], axis=-1, keepdims=True)

    wspec = pl.BlockSpec((None, tr, cols), lambda l, i: (l, i, 0))
    return pl.pallas_call(
        body, name=name, grid=(nl, D // tr),
        in_specs=[pl.BlockSpec((tr, 16), lambda l, i: (i, 0)), pl.BlockSpec((None, 16, cols), lambda l, i: (l, 0, 0)),
                  wspec, wspec, wspec],
        out_specs=[wspec] * 4 + [pl.BlockSpec((None, tr, 1), lambda l, i: (l, i, 0))],
        out_shape=[jax.ShapeDtypeStruct((nl, D, cols), F32)] * 4 + [jax.ShapeDtypeStruct((nl, D, 1), F32)],
        compiler_params=_cparams("parallel", "parallel"),
    )(cond_t, dmod, w, m, v)


def _rope_tables(L, CTX):
    def angles(pos, dim):
        inv_freq = ROPE_BASE ** (-jnp.arange(0, dim, 2, dtype=F32) / dim)
        return pos.astype(F32)[:, None] * inv_freq[None, :]

    def pad(cos, sin):
        return (jnp.concatenate([cos, jnp.ones((CTX, LANES), F32)], 0),
                jnp.concatenate([sin, jnp.zeros((CTX, LANES), F32)], 0))

    ret = angles(jnp.arange(L), 2 * LANES)
    ret_cs = pad(jnp.cos(ret), jnp.sin(ret))
    rows = angles(jnp.arange(L) // GRID_W, ATT_HEAD_DIM // 2)
    cols = angles(jnp.arange(L) % GRID_W, ATT_HEAD_DIM // 2)
    cos = jnp.concatenate([jnp.cos(rows)] * 2 + [jnp.cos(cols)] * 2, axis=1)
    sin = jnp.concatenate([-jnp.sin(rows), jnp.sin(rows), -jnp.sin(cols), jnp.sin(cols)], axis=1)
    return ret_cs, pad(cos, sin)


def kernel(x, c, ctx, c_ctx, ada_w, ada_b, norm_mix_g, norm_mlp_g, mlp_w1, mlp_w2, ret_w_in, ret_w_out, ret_decay_fwd, ret_decay_bwd, attn_w_in, attn_w_out, attn_sink, final_norm_g, loss_target, m_c_ctx, m_ada_w, m_ada_b, m_norm_mix_g, m_norm_mlp_g, m_mlp_w1, m_mlp_w2, m_ret_w_in, m_ret_w_out, m_ret_decay_fwd, m_ret_decay_bwd, m_attn_w_in, m_attn_w_out, m_attn_sink, m_final_norm_g, v_c_ctx, v_ada_w, v_ada_b, v_norm_mix_g, v_norm_mlp_g, v_mlp_w1, v_mlp_w2, v_ret_w_in, v_ret_w_out, v_ret_decay_fwd, v_ret_decay_bwd, v_attn_w_in, v_attn_w_out, v_attn_sink, v_final_norm_g):
    L, D = x.shape[1], x.shape[2]
    CTX = ctx.shape[1]
    T = L + CTX
    RH = ret_decay_fwd.shape[-1]
    assert D == RH * 2 * LANES and ada_w.shape[0] == 2 and ret_w_in.shape[0] == 1 and attn_w_in.shape[0] == 1
    Hq = attn_sink.shape[-1]
    Hkv = (attn_w_in.shape[-1] * N_DEV // ATT_HEAD_DIM - Hq) // 2
    G = Hq // Hkv
    FF = mlp_w1.shape[-1] * N_DEV
    Wq_r, Wv_r = RH * 2 * LANES, RH * 4 * LANES
    acols = ada_w.shape[-1]
    tr = _pick(CTX, 256, 8)
    tr_wide = _pick(CTX, 128, 8)
    bmT = T // 4 if (T % 64 == 0) else T
    bmL = L // 4 if (L % 64 == 0) else L
    x_idx, y_idx, c_idx = lax.axis_index("x"), lax.axis_index("y"), lax.axis_index("c")
    me = 4 * x_idx + 2 * y_idx + c_idx
    me_arr = jnp.reshape(me, (1,)).astype(jnp.int32)

    (rcos, rsin), (acos, asin) = _rope_tables(L, CTX)
    lg_f = jax.nn.log_sigmoid(ret_decay_fwd.astype(F32))
    lg_b = jax.nn.log_sigmoid(ret_decay_bwd.astype(F32))

    c_pad = jnp.concatenate([c.astype(F32), jnp.zeros((7, D), F32)], 0)
    c_all = _all_gather_small(c_pad, name="ag_c")[:, 0, :]
    c16 = jnp.concatenate([c_all, c_ctx[None, :], jnp.zeros((7, D), F32)], 0)
    ada_b_cols = lax.dynamic_slice_in_dim(ada_b, me * acols, acols, axis=1)[:, None, :]
    mods_shard = _ada_mods(c16, ada_w, ada_b_cols, name="ada_mods")
    mods_all = _all_gather_small(mods_shard.reshape(32, acols), name="ag_mods")

    wdefs = {"ret_in": (ret_w_in, 0, 1), "ret_out": (ret_w_out, 0, 0), "w1_0": (mlp_w1, 0, 1), "w2_0": (mlp_w2, 0, 0),
             "attn_in": (attn_w_in, 0, 1), "attn_out": (attn_w_out, 0, 0), "w1_1": (mlp_w1, 1, 1), "w2_1": (mlp_w2, 1, 0)}
    groups = [["ret_in"], ["ret_out", "w1_0", "w2_0"], ["attn_in", "attn_out"], ["w1_1", "w2_1"]]

    def ag_start(gi, after):
        g_axes = [wdefs[k][2] for k in groups[gi]]
        placed = [_cast_place(*wdefs[k], me_arr, name=f"place_{k}") for k in groups[gi]]
        ssem, rsem, lands, tok_ = _gather_start(placed, g_axes, after, name=f"ag_start{gi}")
        return dict(a=(ssem, rsem), lands=lands, axes=g_axes, gi=gi), tok_

    def ag_forward(g, after):
        fs, fr, g["lands"], tok_ = _gather_forward(*g["a"], g["lands"], g["axes"], after, name=f"ag_forward{g['gi']}")
        g["f"] = (fs, fr)
        return tok_

    def ag_wait(g, after):
        return _gather_wait(*g["a"], *g["f"], g["lands"], g["axes"], after, name=f"ag_wait{g['gi']}")

    g0, tok = ag_start(0, mods_all)
    mods_all = (mods_all + tok[0, 0]).reshape(N_DEV, 2, 16, acols).transpose(1, 2, 0, 3).reshape(2, 16, 6, D)
    mod_lat = lax.dynamic_index_in_dim(mods_all, me, axis=1, keepdims=False)
    mod_ctx = mods_all[:, N_DEV]

    def pack(i, ks, kc):
        return jnp.stack([mod_lat[i, ks], mod_lat[i, kc], mod_ctx[i, ks], mod_ctx[i, kc]], 0)

    def gates(i, k):
        return jnp.stack([mod_lat[i, k], mod_ctx[i, k]], 0)

    def gate_epilogue(gl, gc, x_rows_lat_only):
        def epi(acc, i, j, xt, gv):
            if x_rows_lat_only:
                gate = gv[0:1, :]
            else:
                row = i * acc.shape[0] + lax.broadcasted_iota(jnp.int32, (acc.shape[0], 1), 0)
                gate = jnp.where(row >= L, gv[1:2, :], gv[0:1, :])
            return xt + gate * acc, acc
        return epi

    w1, w2 = {}, {}

    mmT = dict(M=T, bm=bmT)
    mmL = dict(M=L, bm=bmL)

    def bn_of(n, off=0):
        b = 4 * LANES
        while n % b or off % b:
            b -= LANES
        return b

    X0 = jnp.concatenate([x[0], ctx[0]], axis=0)
    g_mix0, g_mlp0 = norm_mix_g[0:1], norm_mlp_g[0:1]
    g_mix1, g_mlp1 = norm_mix_g[1:2], norm_mlp_g[1:2]
    a0 = _normmod(X0, g_mix0, pack(0, 0, 1), R=T, L=L, tr=tr, name="normmod_mix0")
    tok = ag_forward(g0, a0)
    (wr_in,) = ag_wait(g0, tok)
    g1, tok = ag_start(1, wr_in)

    bn_qk = _pick(2 * Wq_r, 512, 2 * LANES)
    nq_blocks = Wq_r // bn_qk
    kscale = float(2 * LANES) ** -0.5

    def rope_epi(acc, i, j, cos, sin):
        parts = []
        for h in range(acc.shape[1] // (2 * LANES)):
            x1 = acc[:, h * 2 * LANES:h * 2 * LANES + LANES]
            x2 = acc[:, h * 2 * LANES + LANES:(h + 1) * 2 * LANES]
            parts += [x1 * cos - x2 * sin, x2 * cos + x1 * sin]
        return (jnp.concatenate(parts, axis=1) * jnp.where(j < nq_blocks, 1.0, kscale),)

    def row_tile(arr, bm):
        return (arr, (bm, LANES), lambda i, j: (i, 0))

    (qk0,) = _mm(a0, wr_in, "nn", [BF16], N=2 * Wq_r, K=D, bn=bn_qk, bk=D, name="ret_qk", epilogue=rope_epi,
                 extras=[row_tile(rcos, bmT), row_tile(rsin, bmT)], dep=tok, **mmT)
    bn_vg = bn_of(2 * Wv_r, 2 * Wq_r)
    (vg0,) = _mm(a0, wr_in, "nn", [BF16], N=2 * Wv_r, K=D, bn=bn_vg, bk=D, name="ret_vg", b_col0=2 * Wq_r, dep=tok,
                 **mmT)

    of, st_f = _ret_fwd(qk0, vg0, lg_f, T=T, L=L, H=RH, rev=False, name="ret_scan_f")
    ob, st_b = _ret_fwd(qk0, vg0, lg_b, T=T, L=L, H=RH, rev=True, name="ret_scan_b")
    tok = ag_forward(g1, ob)
    z0 = _readout(of, ob, vg0, T=T, L=L, H=RH, tr=tr_wide, name="ret_readout", dep=tok)
    wr_out, w1[0], w2[0] = ag_wait(g1, z0)
    g2, tok = ag_start(2, wr_out)
    g3, tok = ag_start(3, tok)

    bnD = _pick(D, 512)

    def xtile(arr, bm):
        return (arr, (bm, bnD), lambda i, j: (i, j))

    def gtile(gv):
        return (gv, (2, bnD), lambda i, j: (0, j))

    bk_v = _pick(Wv_r, 2048)
    X1, ro0 = _mm(z0, wr_out, "nn", [F32, BF16], N=D, K=Wv_r, bn=bnD, bk=bk_v, name="ret_out",
                  epilogue=gate_epilogue(None, None, False), extras=[xtile(X0, bmT), gtile(gates(0, 2))], dep=tok, **mmT)

    def mlp_fwd(Xin, i, g_mlp, rows, name, between=None):
        a = _normmod(Xin, g_mlp, pack(i, 3, 4), R=rows["M"], L=L, tr=tr, name=f"normmod_mlp{name}")

        def relu2(acc, i_, j_):
            u = jnp.maximum(acc, 0.0)
            return u, u * u

        bnF = _pick(FF, 512)
        u, r = _mm(a, w1[i], "nn", [BF16, BF16], N=FF, K=D, bn=bnF, bk=D, name=f"mlp_up{name}", epilogue=relu2, **rows)
        dep = None if between is None else between(u)
        bkF = _pick(FF, 2048)
        Xout, mo = _mm(r, w2[i], "nn", [F32, BF16], N=D, K=FF, bn=bnD, bk=bkF, name=f"mlp_down{name}",
                       epilogue=gate_epilogue(None, None, rows["M"] == L),
                       extras=[xtile(Xin, rows["bm"]), gtile(gates(i, 5))], dep=dep, **rows)
        return a, u, r, Xout, mo

    a1, u0, r0, X2, mo0 = mlp_fwd(X1, 0, g_mlp0, mmT, "0")

    tok = ag_forward(g2, X2)
    a2 = _normmod(X2, g_mix1, pack(1, 0, 1), R=T, L=L, tr=tr, name="normmod_mix1", dep=tok)
    wa_in, wa_out = ag_wait(g2, a2)
    Wq_a, Wk_a = Hq * LANES, Hkv * LANES

    def arope_epi(acc, i, j, cos, sin):
        heads = acc.shape[1] // LANES
        return (acc * jnp.tile(cos, (1, heads)) + _swap32(acc) * jnp.tile(sin, (1, heads)),)

    bn_q = _pick(Wq_a, 512)
    (q1,) = _mm(a2, wa_in, "nn", [BF16], N=Wq_a, K=D, bn=bn_q, bk=D, name="attn_q", epilogue=arope_epi,
                extras=[row_tile(acos, bmL), row_tile(asin, bmL)], **mmL)
    bn_k = bn_of(Wk_a, Wq_a)
    (k1,) = _mm(a2, wa_in, "nn", [BF16], N=Wk_a, K=D, bn=bn_k, bk=D, name="attn_k", b_col0=Wq_a, epilogue=arope_epi,
                extras=[row_tile(acos, bmT), row_tile(asin, bmT)], **mmT)
    bn_v = bn_of(Wk_a, Wq_a + Wk_a)
    (v1,) = _mm(a2, wa_in, "nn", [BF16], N=Wk_a, K=D, bn=bn_v, bk=D, name="attn_v", b_col0=Wq_a + Wk_a, **mmT)
    tok = ag_forward(g3, q1)
    o1, lse = _attn_fwd(q1, k1, v1, attn_sink + tok[0, 0], L=L, CTX=CTX, Hkv=Hkv, G=G, name="attn_fwd")
    w1[1], w2[1] = ag_wait(g3, o1)
    X3, ao = _mm(o1, wa_out, "nn", [F32, BF16], N=D, K=Wq_a, bn=bnD, bk=_pick(Wq_a, 2048), name="attn_out",
                 epilogue=gate_epilogue(None, None, True), extras=[xtile(X2, bmL), gtile(gates(1, 2))], **mmL)
    a3, u1, r1, X4, mo1 = mlp_fwd(X3, 1, g_mlp1, mmL, "1")

    dX4, dmo1, acc_head = _loss_head(X4, loss_target[0], mo1, final_norm_g[None, :], gates(1, 5)[0:1], L=L, tr=tr,
                                     name="loss_head")
    loss_part = jnp.sum(acc_head[0, 0])
    d_gf = acc_head[0, 1]
    zeros_d = jnp.zeros((D,), F32)
    dmod_lat = [[zeros_d] * 6, [zeros_d] * 6]
    dmod_ctx = [[zeros_d] * 6, [zeros_d] * 6]
    dmod_lat[1][5] = acc_head[0, 2]

    bn_dw = 2048

    def mlp_bwd(dmo, a, u, r, i, rows, name):
        Mr = rows["M"]
        bkr = rows["bm"]

        def times_2u(acc, i_, j_, ut):
            return (acc * (2.0 * ut.astype(F32)),)

        bnF = _pick(FF, 512)
        (dw2,) = _mm(r, dmo, "tn", [BF16], M=FF, N=D, K=Mr, bm=_pick(FF, 1024), bn=_pick(D, bn_dw), bk=bkr,
                     name=f"mlp_down_dw{name}")
        tok_ = send_grad(f"w2_{i}", dw2, 0)
        (dh,) = _mm(dmo, w2[i], "nt", [BF16], N=FF, K=D, bn=bnF, bk=D, name=f"mlp_down_dx{name}", epilogue=times_2u,
                    extras=[(u, (rows["bm"], bnF), lambda i_, j_: (i_, j_))], dep=tok_, **rows)
        (dw1,) = _mm(a, dh, "tn", [BF16], M=D, N=FF, K=Mr, bm=_pick(D, 1024), bn=_pick(FF, bn_dw), bk=bkr,
                     name=f"mlp_up_dw{name}")
        tok_ = send_grad(f"w1_{i}", dw1, 1)
        (da,) = _mm(dh, w1[i], "nt", [F32], N=D, K=FF, bn=bnD, bk=_pick(FF, 2048), name=f"mlp_up_dx{name}", dep=tok_,
                    **rows)
        return da

    pending = []

    def send_grad(key, dw, axis):
        ssem, rsem, dw_thru, land, tok_ = _scatter_start(dw, axis, name=f"rs_start_{key}")
        pending.append((key, axis, ssem, rsem, dw_thru, land))
        return tok_

    da3 = mlp_bwd(dmo1, a3, u1, r1, 1, mmL, "1")
    dX3, dao, acc = _normmod_bwd(X3, da3, dX4, False, g_mlp1, pack(1, 3, 4), (ao, gates(1, 2)), R=L, L=L, tr=tr,
                                 name="normmod_mlp1_bwd")
    dmod_lat[1][3], dmod_lat[1][4], d_gmlp1, dmod_lat[1][2] = acc[0, 0], acc[0, 1], acc[0, 2], acc[0, 3]

    (dwa_out,) = _mm(o1, dao, "tn", [BF16], M=Wq_a, N=D, K=L, bm=_pick(Wq_a, 1024), bn=_pick(D, bn_dw), bk=bmL,
                     name="attn_out_dw")
    tok = send_grad("attn_out", dwa_out, 0)
    (do1,) = _mm(dao, wa_out, "nt", [BF16], N=Wq_a, K=D, bn=_pick(Wq_a, 512), bk=D, name="attn_out_dx", dep=tok, **mmL)
    dq1, dk1, dv1, dkx, dvx, dsink_acc = _attn_bwd(q1, k1, v1, do1, lse, attn_sink, L=L, CTX=CTX, Hkv=Hkv, G=G,
                                                   name="attn_bwd")
    dp1 = _attn_assemble(dq1, dk1, dv1, dkx, dvx, acos, asin, T=T, L=L, CTX=CTX, Hq=Hq, Hkv=Hkv, tr=tr_wide,
                         name="attn_assemble")
    Wa_in = Wq_a + 2 * Wk_a
    (dwa_in,) = _mm(a2, dp1, "tn", [BF16], M=D, N=Wa_in, K=T, bm=_pick(D, 1024), bn=_pick(Wa_in, bn_dw), bk=bmT,
                    name="attn_in_dw")
    tok = send_grad("attn_in", dwa_in, 1)
    (da2,) = _mm(dp1, wa_in, "nt", [F32], N=D, K=Wa_in, bn=bnD, bk=_pick(Wa_in, 2048), name="attn_in_dx", dep=tok,
                 **mmT)
    dX2, dmo0, acc = _normmod_bwd(X2, da2, dX3, True, g_mix1, pack(1, 0, 1), (mo0, gates(0, 5)), R=T, L=L, tr=tr,
                                  name="normmod_mix1_bwd")
    dmod_lat[1][0], dmod_lat[1][1], d_gmix1, dmod_lat[0][5] = acc[0, 0], acc[0, 1], acc[0, 2] + acc[1, 2], acc[0, 3]
    dmod_ctx[1][0], dmod_ctx[1][1], dmod_ctx[0][5] = acc[1, 0], acc[1, 1], acc[1, 3]

    da1 = mlp_bwd(dmo0, a1, u0, r0, 0, mmT, "0")
    dX1, dro0, acc = _normmod_bwd(X1, da1, dX2, False, g_mlp0, pack(0, 3, 4), (ro0, gates(0, 2)), R=T, L=L, tr=tr,
                                  name="normmod_mlp0_bwd")
    dmod_lat[0][3], dmod_lat[0][4], d_gmlp0, dmod_lat[0][2] = acc[0, 0], acc[0, 1], acc[0, 2] + acc[1, 2], acc[0, 3]
    dmod_ctx[0][3], dmod_ctx[0][4], dmod_ctx[0][2] = acc[1, 0], acc[1, 1], acc[1, 3]

    (dwr_out,) = _mm(z0, dro0, "tn", [BF16], M=Wv_r, N=D, K=T, bm=_pick(Wv_r, 1024), bn=_pick(D, bn_dw), bk=bmT,
                     name="ret_out_dw")
    tok = send_grad("ret_out", dwr_out, 0)
    (dz0,) = _mm(dro0, wr_out, "nt", [BF16], N=Wv_r, K=D, bn=_pick(Wv_r, 512), bk=D, name="ret_out_dx", dep=tok, **mmT)
    do0, dg0 = _readout_bwd(dz0, of, ob, vg0, T=T, L=L, H=RH, tr=tr_wide, name="ret_readout_bwd")
    dq_f, dk_f, dv_f, dlg_f = _ret_bwd(qk0, vg0, do0, st_f, lg_f, T=T, L=L, H=RH, rev=False, name="ret_scan_f_bwd")
    dq_b, dk_b, dv_b, dlg_b = _ret_bwd(qk0, vg0, do0, st_b, lg_b, T=T, L=L, H=RH, rev=True, name="ret_scan_b_bwd")
    dp0 = _ret_assemble(dq_f, dq_b, dk_f, dk_b, dv_f, dv_b, dg0, rcos, rsin, T=T, L=L, H=RH, tr=tr_wide,
                        name="ret_assemble")
    Wr_in = 2 * Wq_r + 2 * Wv_r
    (dwr_in,) = _mm(a0, dp0, "tn", [BF16], M=D, N=Wr_in, K=T, bm=_pick(D, 1024), bn=_pick(Wr_in, bn_dw), bk=bmT,
                    name="ret_in_dw")
    tok = send_grad("ret_in", dwr_in, 1)
    (da0,) = _mm(dp0, wr_in, "nt", [F32], N=D, K=Wr_in, bn=bnD, bk=_pick(Wr_in, 2048), name="ret_in_dx", dep=tok, **mmT)
    dX0, acc = _normmod_bwd(X0, da0, dX1, False, g_mix0, pack(0, 0, 1), None, R=T, L=L, tr=tr, name="normmod_mix0_bwd")
    dmod_lat[0][0], dmod_lat[0][1], d_gmix0 = acc[0, 0], acc[0, 1], acc[0, 2] + acc[1, 2]
    dmod_ctx[0][0], dmod_ctx[0][1] = acc[1, 0], acc[1, 1]
    grad_x = dX0[:L][None]

    wmv = {"ret_in": (ret_w_in, m_ret_w_in, v_ret_w_in, 0, "ret_w_in"),
           "ret_out": (ret_w_out, m_ret_w_out, v_ret_w_out, 0, "ret_w_out"),
           "attn_in": (attn_w_in, m_attn_w_in, v_attn_w_in, 0, "attn_w_in"),
           "attn_out": (attn_w_out, m_attn_w_out, v_attn_w_out, 0, "attn_w_out"),
           "w1_0": (mlp_w1, m_mlp_w1, v_mlp_w1, 0, "mlp_w1"), "w1_1": (mlp_w1, m_mlp_w1, v_mlp_w1, 1, "mlp_w1"),
           "w2_0": (mlp_w2, m_mlp_w2, v_mlp_w2, 0, "mlp_w2"), "w2_1": (mlp_w2, m_mlp_w2, v_mlp_w2, 1, "mlp_w2")}
    big = {}

    def finish_grad(entry, after):
        key, axis, ssem, rsem, dw_thru, land = entry
        dw_done, land_done = _scatter_wait(ssem, rsem, dw_thru, land, axis, after, name=f"rs_wait_{key}")
        w_, m_, v_, layer, out_name = wmv[key]
        big[out_name] = _adamw_sharded(w_, m_, v_, layer, dw_done, land_done, axis, me_arr, big.get(out_name),
                                       name=f"adamw_{key}")
        return big[out_name][0]

    after = dX0
    for entry in pending[:-1]:
        after = finish_grad(entry, after)

    misc = jnp.zeros((D,), F32)
    misc = misc.at[0:RH].set(dlg_f[:, 0, 0]).at[RH:2 * RH].set(dlg_b[:, 0, 0])
    misc = misc.at[2 * RH:2 * RH + Hq].set(dsink_acc[:, :G, 0].reshape(Hq)).at[2 * RH + Hq].set(loss_part)
    rows = ([dmod_lat[i][k] for i in range(2) for k in range(6)] + [dmod_ctx[i][k] for i in range(2) for k in range(6)]
            + [d_gmix0, d_gmix1, d_gmlp0, d_gmlp1, d_gf, misc, zeros_d, zeros_d])
    part = jnp.stack(rows, 0)
    part_all = _all_gather_small(part, name="ag_small_grads")
    tot = _sum_devices(part_all, name="sum_small_grads")

    grad_ada_b = (tot[0:12] + tot[12:24]).reshape(2, 6 * D)
    grad_norm_mix_g, grad_norm_mlp_g, grad_final_norm_g = tot[24:26], tot[26:28], tot[28]
    grad_ret_decay_fwd = (tot[29, 0:RH] * jax.nn.sigmoid(-ret_decay_fwd[0]))[None]
    grad_ret_decay_bwd = (tot[29, RH:2 * RH] * jax.nn.sigmoid(-ret_decay_bwd[0]))[None]
    grad_attn_sink = tot[29, 2 * RH:2 * RH + Hq][None]
    loss = tot[29, 2 * RH + Hq]

    dlat_cols = lax.dynamic_slice_in_dim(part_all[:, 0:12].reshape(N_DEV, 2, 6 * D), me * acols, acols, axis=2)
    dctx_cols = lax.dynamic_slice_in_dim(tot[12:24].reshape(2, 6 * D), me * acols, acols, axis=1)
    dmod16 = jnp.concatenate([dlat_cols.transpose(1, 0, 2), dctx_cols[:, None, :], jnp.zeros((2, 7, acols), F32)], 1)
    cond_t = _silu(c16).T
    g_ada, d_ada, nm_ada, nv_ada, dcond_part = _ada_bwd(cond_t, dmod16, ada_w, m_ada_w, v_ada_w, name="ada_bwd")
    dcond = (dcond_part[0, :, 0] + dcond_part[1, :, 0]).reshape(D // LANES, LANES)
    pad_rows = -(D // LANES) % 8
    dcond_pad = jnp.concatenate([dcond, jnp.zeros((pad_rows, LANES), F32)], 0) if pad_rows else dcond
    dcond_all = _all_gather_small(dcond_pad, name="ag_dcond")
    dcond_tot = _sum_devices(dcond_all, name="sum_dcond")[:D // LANES].reshape(D)
    sg = jax.nn.sigmoid(c_ctx)
    grad_c_ctx = dcond_tot * (sg * (1.0 + c_ctx * (1.0 - sg)))

    small_w = [c_ctx, ada_b, norm_mix_g, norm_mlp_g, ret_decay_fwd, ret_decay_bwd, attn_sink, final_norm_g]
    small_g = [grad_c_ctx, grad_ada_b, grad_norm_mix_g, grad_norm_mlp_g, grad_ret_decay_fwd, grad_ret_decay_bwd,
               grad_attn_sink, grad_final_norm_g]
    small_m = [m_c_ctx, m_ada_b, m_norm_mix_g, m_norm_mlp_g, m_ret_decay_fwd, m_ret_decay_bwd, m_attn_sink,
               m_final_norm_g]
    small_v = [v_c_ctx, v_ada_b, v_norm_mix_g, v_norm_mlp_g, v_ret_decay_fwd, v_ret_decay_bwd, v_attn_sink,
               v_final_norm_g]
    sizes = [w_.size for w_ in small_w]
    total = sum(-(-s // LANES) * LANES for s in sizes)
    total_pad = -(-total // (8 * LANES)) * 8 * LANES

    def flat_pack(ts, fill):
        pieces = []
        for t_ in ts:
            f = t_.reshape(-1).astype(F32)
            pad = -f.size % LANES
            pieces.append(jnp.concatenate([f, jnp.full((pad,), fill, F32)]) if pad else f)
        pieces.append(jnp.full((total_pad - total,), fill, F32))
        return jnp.concatenate(pieces).reshape(total_pad // LANES, LANES)

    d_s, nm_s, nv_s = _adamw_flat(flat_pack(small_w, 0.0), flat_pack(small_g, 0.0), flat_pack(small_m, 0.0),
                                  flat_pack(small_v, 1.0), name="adamw_small")

    def unpack(p):
        flat = p.reshape(-1)
        res, off = [], 0
        for w_, s in zip(small_w, sizes):
            res.append(flat[off:off + s].reshape(w_.shape))
            off += -(-s // LANES) * LANES
        return res

    finish_grad(pending[-1], d_s)
    d_small, nm_small, nv_small = unpack(d_s), unpack(nm_s), unpack(nv_s)
    small_names = ["c_ctx", "ada_b", "norm_mix_g", "norm_mlp_g", "ret_decay_fwd", "ret_decay_bwd", "attn_sink",
                   "final_norm_g"]
    sm = {n: (g_, d_, m_, v_) for n, g_, d_, m_, v_ in zip(small_names, small_g, d_small, nm_small, nv_small)}

    def out4(n):
        if n == "ada_w":
            return g_ada, d_ada, nm_ada, nv_ada
        if n in big:
            return tuple(big[n])
        return sm[n]

    order = ["c_ctx", "ada_w", "ada_b", "norm_mix_g", "norm_mlp_g", "mlp_w1", "mlp_w2", "ret_w_in", "ret_w_out",
             "ret_decay_fwd", "ret_decay_bwd", "attn_w_in", "attn_w_out", "attn_sink", "final_norm_g"]
    quads = [out4(n) for n in order]
    return (loss, grad_x, *[q_[0] for q_ in quads], *[q_[1] for q_ in quads], *[q_[2] for q_ in quads],
            *[q_[3] for q_ in quads])
```

```python
import functools

import jax
import jax.numpy as jnp
from jax import lax
from jax.experimental import pallas as pl
from jax.experimental.pallas import tpu as pltpu

F32 = jnp.float32
BF16 = jnp.bfloat16

N_DEV = 8
NORM_EPS = 1e-6
CHUNK = 128
ATT_HEAD_DIM = 128
GRID_W = 64
ROPE_BASE = 10000.0
NEG_INF = -1e30
ADAM_LR, ADAM_B1, ADAM_B2, ADAM_EPS, ADAM_WD, ADAM_STEP = 0.001, 0.9, 0.999, 1e-08, 0.01, 10

V7X_VMEM_LIMIT_BYTES = 56 * 1024 * 1024
MM_BN = 1024
LANES = 128
MESH = pl.DeviceIdType.MESH

_NN = (((1,), (0,)), ((), ()))
_NT = (((1,), (1,)), ((), ()))
_TN = (((0,), (0,)), ((), ()))


def _dot(a, b, dn=_NN):
    return lax.dot_general(a, b, dn, preferred_element_type=F32)


def _cparams(*sem):
    return pltpu.CompilerParams(dimension_semantics=sem, vmem_limit_bytes=V7X_VMEM_LIMIT_BYTES)


def _pick(n, pref, mult=LANES):
    if n <= pref:
        return n
    best = None
    for d in range(mult, pref + 1, mult):
        if n % d == 0:
            best = d
    assert best is not None, (n, pref)
    return best


def _silu(x):
    return x * jax.nn.sigmoid(x)


def _mm(a, b, mode, out_dtypes, *, M, N, K, bm, bn, bk, name, b_col0=0, epilogue=None, extras=(), dep=None):
    assert M % bm == 0 and N % bn == 0 and K % bk == 0 and b_col0 % bn == 0, (name, M, N, K, bm, bn, bk, b_col0)
    nk = K // bk
    c0 = b_col0 // bn
    if mode == "nn":
        a_spec = pl.BlockSpec((bm, bk), lambda i, j, k: (i, k))
        b_spec = pl.BlockSpec((bk, bn), lambda i, j, k: (k, j + c0))
    elif mode == "nt":
        a_spec = pl.BlockSpec((bm, bk), lambda i, j, k: (i, k))
        b_spec = pl.BlockSpec((bn, bk), lambda i, j, k: (j + c0, k))
    else:
        a_spec = pl.BlockSpec((bk, bm), lambda i, j, k: (k, i))
        b_spec = pl.BlockSpec((bk, bn), lambda i, j, k: (k, j + c0))
    dn = {"nn": _NN, "nt": _NT, "tn": _TN}[mode]
    e_specs = [pl.BlockSpec(bs, (lambda i, j, k, f=f: f(i, j))) for (_, bs, f) in extras]
    ne, no = len(extras), len(out_dtypes)
    nd = 0 if dep is None else 1

    def body(a_ref, b_ref, *rest):
        e_refs, o_refs = rest[:ne], rest[ne + nd:ne + nd + no]
        i, j, k = pl.program_id(0), pl.program_id(1), pl.program_id(2)

        def finish(acc):
            outs = (acc,) if epilogue is None else epilogue(acc, i, j, *[e[...] for e in e_refs])
            for o_ref, o in zip(o_refs, outs):
                o_ref[...] = o.astype(o_ref.dtype)

        p = _dot(a_ref[...], b_ref[...], dn)
        if nk == 1:
            finish(p)
        else:
            acc_ref = rest[-1]

            @pl.when(k == 0)
            def _():
                acc_ref[...] = p

            @pl.when(k > 0)
            def _():
                acc_ref[...] += p

            @pl.when(k == nk - 1)
            def _():
                finish(acc_ref[...])

    outs = pl.pallas_call(
        body, name=name, grid=(M // bm, N // bn, nk),
        in_specs=[a_spec, b_spec] + e_specs + [pl.BlockSpec(memory_space=pl.ANY)] * nd,
        out_specs=[pl.BlockSpec((bm, bn), lambda i, j, k: (i, j)) for _ in out_dtypes],
        out_shape=[jax.ShapeDtypeStruct((M, N), dt) for dt in out_dtypes],
        scratch_shapes=[pltpu.VMEM((bm, bn), F32)] if nk > 1 else [],
        compiler_params=_cparams("parallel", "parallel", "arbitrary"),
    )(a, b, *[e[0] for e in extras], *([dep] if nd else []))
    return outs


def _rowwise(body, rows, vecs, outs, n_acc, *, R, L, tr, name, acc_width=None, dep=None):
    assert R % tr == 0 and L % tr == 0, (name, R, L, tr)
    nl = L // tr
    n_regions = 2 if R > L else 1
    n_rows, n_vecs, n_outs = len(rows), len(vecs), len(outs)
    n_dep = 0 if dep is None else 1
    acc_pad = -(-n_acc // 8) * 8 if n_acc else 0

    in_specs = []
    for (_, w, cb, lat_only) in rows:
        if lat_only:
            in_specs.append(pl.BlockSpec((tr, w), lambda i, cb=cb: (jnp.minimum(i, nl - 1), cb)))
        else:
            in_specs.append(pl.BlockSpec((tr, w), lambda i, cb=cb: (i, cb)))
    for v in vecs:
        in_specs.append(pl.BlockSpec(v.shape, lambda i, nd=v.ndim: (0,) * nd))
    in_specs += [pl.BlockSpec(memory_space=pl.ANY)] * n_dep
    out_specs = [pl.BlockSpec((tr, w), lambda i: (i, 0)) for (w, _) in outs]
    out_shape = [jax.ShapeDtypeStruct((R, w), dt) for (w, dt) in outs]
    if n_acc:
        out_specs.append(pl.BlockSpec((None, acc_pad, acc_width), lambda i: (jnp.where(i >= nl, 1, 0), 0, 0)))
        out_shape.append(jax.ShapeDtypeStruct((n_regions, acc_pad, acc_width), F32))

    def kern(*refs):
        i = pl.program_id(0)
        is_ctx = i >= nl
        ins = [r[...] for r in refs[:n_rows + n_vecs]]
        o_refs = refs[n_rows + n_vecs + n_dep:]
        out_tiles, acc_rows = body(is_ctx, *ins)
        for o_ref, o in zip(o_refs[:n_outs], out_tiles):
            o_ref[...] = o.astype(o_ref.dtype)
        if n_acc:
            acc_ref = o_refs[n_outs]

            @pl.when((i == 0) | (i == nl))
            def _():
                acc_ref[...] = jnp.zeros_like(acc_ref)

            for r, row in enumerate(acc_rows):
                acc_ref[r:r + 1, :] += row

    res = pl.pallas_call(
        kern, name=name, grid=(R // tr,), in_specs=in_specs, out_specs=out_specs, out_shape=out_shape,
        compiler_params=_cparams("arbitrary"),
    )(*[r[0] for r in rows], *vecs, *([dep] if n_dep else []))
    return res


def _colsum(x):
    return jnp.sum(x, axis=0, keepdims=True)


def _rms_stats(x):
    r = lax.rsqrt(jnp.mean(x * x, axis=-1, keepdims=True) + NORM_EPS)
    return x * r, r


def _sel(is_ctx, pk, lat_row, ctx_row):
    return jnp.where(is_ctx, pk[ctx_row:ctx_row + 1, :], pk[lat_row:lat_row + 1, :])


def _normmod(x, g, pk, *, R, L, tr, name, dep=None):
    D = x.shape[-1]

    def body(is_ctx, xt, gv, pkv):
        xh, _ = _rms_stats(xt)
        sh, sc = _sel(is_ctx, pkv, 0, 2), _sel(is_ctx, pkv, 1, 3)
        return ((xh * gv) * (1.0 + sc) + sh,), ()

    return _rowwise(body, [(x, D, 0, False)], [g, pk], [(D, BF16)], 0, R=R, L=L, tr=tr, name=name, dep=dep)[0]


def _normmod_bwd(x_in, da, dx_out, dx_out_lat_only, g, pk, prev, *, R, L, tr, name):
    D = x_in.shape[-1]
    has_prev = prev is not None

    def body(is_ctx, *t):
        if has_prev:
            xt, dat, dxo, mp, gv, pkv, gates = t
        else:
            xt, dat, dxo, gv, pkv = t
        xh, r = _rms_stats(xt)
        sc = _sel(is_ctx, pkv, 1, 3)
        if dx_out_lat_only:
            dxo = jnp.where(is_ctx, 0.0, dxo)
        dn = dat * (1.0 + sc)
        w = dn * gv
        dxi = dxo + r * (w - xh * jnp.mean(w * xh, axis=-1, keepdims=True))
        accs = [_colsum(dat), _colsum(dat * (xh * gv)), _colsum(dn * xh)]
        outs = [dxi]
        if has_prev:
            gate = _sel(is_ctx, gates, 0, 1)
            outs.append(dxi * gate)
            accs.append(_colsum(dxi * mp.astype(F32)))
        return outs, accs

    rows = [(x_in, D, 0, False), (da, D, 0, False), (dx_out, D, 0, dx_out_lat_only)]
    vecs = [g, pk]
    outs = [(D, F32)]
    if has_prev:
        rows.append((prev[0], D, 0, False))
        vecs.append(prev[1])
        outs.append((D, BF16))
    return _rowwise(body, rows, vecs, outs, 4 if has_prev else 3, R=R, L=L, tr=tr, name=name, acc_width=D)


def _loss_head(x4, target, m_prev, gf, gate, *, L, tr, name):
    D = x4.shape[-1]

    def body(is_ctx, xt, tg, mp, gfv, gatev):
        xh, r = _rms_stats(xt)
        e = xh * gfv - tg
        dy = e * (1.0 / D)
        w = dy * gfv
        dx = r * (w - xh * jnp.mean(w * xh, axis=-1, keepdims=True))
        accs = [_colsum(e * e) * (0.5 / D), _colsum(dy * xh), _colsum(dx * mp.astype(F32))]
        return (dx, dx * gatev), accs

    return _rowwise(body, [(x4, D, 0, False), (target, D, 0, False), (m_prev, D, 0, False)], [gf, gate],
                    [(D, F32), (D, BF16)], 3, R=L, L=L, tr=tr, name=name, acc_width=D)


RET_CHUNK = 2 * LANES
RET_HEADS_PER_STEP = 4


def _decays(lgh, rev):
    C = RET_CHUNK
    ii = lax.broadcasted_iota(jnp.int32, (C, C), 0)
    jj = lax.broadcasted_iota(jnp.int32, (C, C), 1)
    ri = lax.broadcasted_iota(jnp.int32, (C, 1), 0).astype(F32)
    diff = (jj - ii if rev else ii - jj)
    amat = jnp.where(diff >= 0, jnp.exp(lgh * jnp.maximum(diff, 0).astype(F32)), 0.0)
    pos = (C - ri) if rev else (ri + 1.0)
    bq = jnp.exp(lgh * pos)
    bk = jnp.exp(lgh * (C - pos))
    return amat, bq, bk, pos


def _ret_geometry(T, L, H, rev, backward):
    C = RET_CHUNK
    assert T % C == 0 and L % C == 0, (T, L)
    nT, nL = T // C, L // C
    hb = RET_HEADS_PER_STEP if H % RET_HEADS_PER_STEP == 0 else 1

    def step(s):
        return (nT - 1 - s) if backward else s

    def chunk(s):
        s = step(s)
        return (nT - 1 - s) if rev else (s + nL) % nT

    return C, nT, hb, chunk, step


def _ret_fwd(qk, vg, lg, *, T, L, H, rev, name):
    dk, dv = 2 * LANES, 4 * LANES
    C, nT, hb, chunk, step = _ret_geometry(T, L, H, rev, False)

    def body(lg_ref, q_ref, k_ref, v_ref, o_ref, st_ref, s_scr):
        hg, s = pl.program_id(0), pl.program_id(1)

        @pl.when(s == 0)
        def _():
            s_scr[...] = jnp.zeros_like(s_scr)

        for hh in range(hb):
            lgh = lg_ref[0, hg * hb + hh]
            amat, bq, bk, _ = _decays(lgh, rev)
            q, k = q_ref[:, hh * dk:(hh + 1) * dk], k_ref[:, hh * dk:(hh + 1) * dk]
            v = v_ref[:, hh * dv:(hh + 1) * dv]
            stb = s_scr[hh].astype(BF16)
            st_ref[hh] = stb
            scores = _dot(q, k, _NT) * amat
            o_ref[:, hh * dv:(hh + 1) * dv] = _dot(scores.astype(BF16), v) + _dot(q, stb) * bq
            kd = (k.astype(F32) * bk).astype(BF16)
            s_scr[hh] = s_scr[hh] * jnp.exp(lgh * C) + _dot(kd, v, _TN)

    return pl.pallas_call(
        body, name=name, grid=(H // hb, nT),
        in_specs=[pl.BlockSpec(memory_space=pltpu.SMEM),
                  pl.BlockSpec((C, hb * dk), lambda h, s: (chunk(s), h)),
                  pl.BlockSpec((C, hb * dk), lambda h, s: (chunk(s), H // hb + h)),
                  pl.BlockSpec((C, hb * dv), lambda h, s: (chunk(s), h))],
        out_specs=[pl.BlockSpec((C, hb * dv), lambda h, s: (chunk(s), h)),
                   pl.BlockSpec((hb, None, dk, dv), lambda h, s: (h, s, 0, 0))],
        out_shape=[jax.ShapeDtypeStruct((T, H * dv), F32), jax.ShapeDtypeStruct((H, nT, dk, dv), BF16)],
        scratch_shapes=[pltpu.VMEM((hb, dk, dv), F32)],
        compiler_params=_cparams("parallel", "arbitrary"),
    )(lg, qk, qk, vg)


def _ret_bwd(qk, vg, do, states, lg, *, T, L, H, rev, name):
    dk, dv = 2 * LANES, 4 * LANES
    C, nT, hb, chunk, step = _ret_geometry(T, L, H, rev, True)

    def body(lg_ref, q_ref, k_ref, v_ref, do_ref, st_ref, dq_ref, dk_ref, dv_ref, dlg_ref, ds_scr):
        hg, s = pl.program_id(0), pl.program_id(1)

        @pl.when(s == 0)
        def _():
            ds_scr[...] = jnp.zeros_like(ds_scr)
            dlg_ref[...] = jnp.zeros_like(dlg_ref)

        for hh in range(hb):
            lgh = lg_ref[0, hg * hb + hh]
            amat, bq, bk, pos = _decays(lgh, rev)
            ksl, vsl = slice(hh * dk, (hh + 1) * dk), slice(hh * dv, (hh + 1) * dv)
            q, k, v, dob = q_ref[:, ksl], k_ref[:, ksl], v_ref[:, vsl], do_ref[:, vsl]
            stb = st_ref[hh]
            ds_new = ds_scr[hh]
            dsb = ds_new.astype(BF16)
            qf, kf = q.astype(F32), k.astype(F32)
            scores = (_dot(q, k, _NT) * amat).astype(BF16)
            dqk = (_dot(dob, v, _NT) * amat).astype(BF16)
            dq = _dot(dqk, k) + _dot(dob, stb, _NT) * bq
            dkk = _dot(dqk, q, _TN) + _dot(v, dsb, _NT) * bk
            kd = (kf * bk).astype(BF16)
            dvv = _dot(scores, dob, _TN) + _dot(kd, dsb)
            dod = (dob.astype(F32) * bq).astype(BF16)
            ds_prev = ds_new * jnp.exp(lgh * C) + _dot(q, dod, _TN)
            ds_scr[hh] = ds_prev
            dq_ref[:, ksl] = dq.astype(dq_ref.dtype)
            dk_ref[:, ksl] = dkk.astype(dk_ref.dtype)
            dv_ref[:, vsl] = dvv.astype(dv_ref.dtype)
            dlg = (jnp.sum(pos * jnp.sum(qf * dq - kf * dkk, axis=-1, keepdims=True))
                   + C * jnp.sum(ds_prev * stb.astype(F32)))
            dlg_ref[hh] += dlg

    qspec = pl.BlockSpec((C, hb * dk), lambda h, s: (chunk(s), h))
    vspec = pl.BlockSpec((C, hb * dv), lambda h, s: (chunk(s), h))
    return pl.pallas_call(
        body, name=name, grid=(H // hb, nT),
        in_specs=[pl.BlockSpec(memory_space=pltpu.SMEM), qspec,
                  pl.BlockSpec((C, hb * dk), lambda h, s: (chunk(s), H // hb + h)), vspec, vspec,
                  pl.BlockSpec((hb, None, dk, dv), lambda h, s: (h, step(s), 0, 0))],
        out_specs=[qspec, qspec, vspec, pl.BlockSpec((hb, 8, LANES), lambda h, s: (h, 0, 0))],
        out_shape=[jax.ShapeDtypeStruct((T, H * dk), BF16), jax.ShapeDtypeStruct((T, H * dk), BF16),
                   jax.ShapeDtypeStruct((T, H * dv), BF16), jax.ShapeDtypeStruct((H, 8, LANES), F32)],
        scratch_shapes=[pltpu.VMEM((hb, dk, dv), F32)],
        compiler_params=_cparams("parallel", "arbitrary"),
    )(lg, qk, qk, vg, do, states)


def _readout(o_f, o_b, vg, *, T, L, H, tr, name, dep=None):
    dv = 4 * LANES
    W = H * dv

    def body(is_ctx, of, ob, g):
        o = of + ob
        parts = []
        for h in range(H):
            oh = o[:, h * dv:(h + 1) * dv]
            parts.append(oh * lax.rsqrt(jnp.mean(oh * oh, axis=-1, keepdims=True) + NORM_EPS))
        y = jnp.concatenate(parts, axis=1)
        return (_silu(g.astype(F32)) * y,), ()

    return _rowwise(body, [(o_f, W, 0, False), (o_b, W, 0, False), (vg, W, 1, False)], [], [(W, BF16)], 0,
                    R=T, L=L, tr=tr, name=name, dep=dep)[0]


def _readout_bwd(dz, o_f, o_b, vg, *, T, L, H, tr, name):
    dv = 4 * LANES
    W = H * dv

    def body(is_ctx, dzt, of, ob, g):
        o = of + ob
        gf = g.astype(F32)
        sg = jax.nn.sigmoid(gf)
        dzf = dzt.astype(F32)
        dy = dzf * (gf * sg)
        ys, dos = [], []
        for h in range(H):
            sl = slice(h * dv, (h + 1) * dv)
            oh, dyh = o[:, sl], dy[:, sl]
            r = lax.rsqrt(jnp.mean(oh * oh, axis=-1, keepdims=True) + NORM_EPS)
            yh = oh * r
            ys.append(yh)
            dos.append(r * (dyh - yh * jnp.mean(dyh * yh, axis=-1, keepdims=True)))
        y = jnp.concatenate(ys, axis=1)
        dg = dzf * y * (sg * (1.0 + gf * (1.0 - sg)))
        return (jnp.concatenate(dos, axis=1), dg), ()

    return _rowwise(body, [(dz, W, 0, False), (o_f, W, 0, False), (o_b, W, 0, False), (vg, W, 1, False)], [],
                    [(W, BF16), (W, BF16)], 0, R=T, L=L, tr=tr, name=name)


def _ret_assemble(dq_f, dq_b, dk_f, dk_b, dv_f, dv_b, dg, cos, sin, *, T, L, H, tr, name):
    dk, dv = 2 * LANES, 4 * LANES
    Wq, Wv = H * dk, H * dv
    kscale = float(dk) ** -0.5

    def unrope(d, c, s_, scale):
        parts = []
        for h in range(H):
            d1, d2 = d[:, h * dk:h * dk + LANES], d[:, h * dk + LANES:(h + 1) * dk]
            parts += [(d1 * c + d2 * s_) * scale, (d2 * c - d1 * s_) * scale]
        return jnp.concatenate(parts, axis=1)

    def body(is_ctx, qf, qb, kf, kb, vf, vb, g, c, s_):
        add = lambda a, b: a.astype(F32) + b.astype(F32)
        dq = unrope(add(qf, qb), c, s_, 1.0)
        dkk = unrope(add(kf, kb), c, s_, kscale)
        return (jnp.concatenate([dq.astype(BF16), dkk.astype(BF16), add(vf, vb).astype(BF16), g], axis=1),), ()

    rows = [(dq_f, Wq, 0, False), (dq_b, Wq, 0, False), (dk_f, Wq, 0, False), (dk_b, Wq, 0, False),
            (dv_f, Wv, 0, False), (dv_b, Wv, 0, False), (dg, Wv, 0, False),
            (cos, LANES, 0, False), (sin, LANES, 0, False)]
    return _rowwise(body, rows, [], [(2 * Wq + 2 * Wv, BF16)], 0, R=T, L=L, tr=tr, name=name)[0]


def _swap32(x):
    n = x.shape[-1]
    lane = lax.broadcasted_iota(jnp.int32, x.shape, x.ndim - 1)
    return jnp.where(lane % 64 < 32, pltpu.roll(x, n - 32, x.ndim - 1), pltpu.roll(x, 32, x.ndim - 1))


def _stack_heads(ref, G):
    return jnp.concatenate([ref[:, g * LANES:(g + 1) * LANES] for g in range(G)], axis=0)


def _stack_columns(ref, G):
    return jnp.concatenate([ref[:, g:g + 1] for g in range(G)], axis=0)


def _sink_column(sink_ref, h, G):
    return jnp.concatenate([jnp.full((CHUNK, 1), sink_ref[0, h * G + g], F32) for g in range(G)], axis=0)


def _key_mask(n, nb, CTX, G):
    W = 3 * CHUNK + CTX
    ii = lax.broadcasted_iota(jnp.int32, (G * CHUNK, W), 0) & (CHUNK - 1)
    col = lax.broadcasted_iota(jnp.int32, (G * CHUNK, W), 1)
    is_prev = col < CHUNK
    is_next = (col >= 2 * CHUNK) & (col < 3 * CHUNK)
    prev_ok = is_prev & (col >= ii) & (n > 0)
    next_ok = is_next & ((col - 2 * CHUNK) <= ii) & (n < nb - 1)
    return prev_ok | next_ok | jnp.logical_not(is_prev | is_next)


def _attn_kv_specs(L, CTX, nb):
    blk = lambda f: pl.BlockSpec((CHUNK, LANES), lambda h, n: (f(n), h))
    prev_, cur_, next_ = (lambda n: jnp.maximum(n - 1, 0)), (lambda n: n), (lambda n: jnp.minimum(n + 1, nb - 1))
    ctx_spec = pl.BlockSpec((CTX, LANES), lambda h, n: (L // CTX, h))
    return [blk(prev_), blk(cur_), blk(next_), ctx_spec]


def _attn_fwd(q, k, v, sink, *, L, CTX, Hkv, G, name):
    nb = L // CHUNK
    scale = float(ATT_HEAD_DIM) ** -0.5
    kvs = _attn_kv_specs(L, CTX, nb)

    def body(sink_ref, q_ref, kp, kc, kn, kx, vp, vc, vn, vx, o_ref, lse_ref):
        h, n = pl.program_id(0), pl.program_id(1)
        qs = _stack_heads(q_ref, G)
        kall = jnp.concatenate([kp[...], kc[...], kn[...], kx[...]], axis=0)
        vall = jnp.concatenate([vp[...], vc[...], vn[...], vx[...]], axis=0)
        s_ = jnp.where(_key_mask(n, nb, CTX, G), _dot(qs, kall, _NT) * scale, NEG_INF)
        sk = _sink_column(sink_ref, h, G)
        m = jnp.maximum(jnp.max(s_, axis=-1, keepdims=True), sk)
        p = jnp.exp(s_ - m)
        den = jnp.sum(p, axis=-1, keepdims=True) + jnp.exp(sk - m)
        o = _dot(p.astype(BF16), vall) / den
        lse = m + jnp.log(den)
        for g in range(G):
            o_ref[:, g * LANES:(g + 1) * LANES] = o[g * CHUNK:(g + 1) * CHUNK].astype(o_ref.dtype)
            lse_ref[:, g:g + 1] = lse[g * CHUNK:(g + 1) * CHUNK]

    return pl.pallas_call(
        body, name=name, grid=(Hkv, nb),
        in_specs=[pl.BlockSpec(memory_space=pltpu.SMEM),
                  pl.BlockSpec((CHUNK, G * LANES), lambda h, n: (n, h))] + kvs + kvs,
        out_specs=[pl.BlockSpec((CHUNK, G * LANES), lambda h, n: (n, h)),
                   pl.BlockSpec((None, CHUNK, G), lambda h, n: (h, n, 0))],
        out_shape=[jax.ShapeDtypeStruct((L, Hkv * G * LANES), BF16), jax.ShapeDtypeStruct((Hkv, L, G), F32)],
        compiler_params=_cparams("parallel", "parallel"),
    )(sink, q, k, k, k, k, v, v, v, v)


def _attn_bwd(q, k, v, do, lse, sink, *, L, CTX, Hkv, G, name):
    nb = L // CHUNK
    scale = float(ATT_HEAD_DIM) ** -0.5
    kvs = _attn_kv_specs(L, CTX, nb)
    qspec = pl.BlockSpec((CHUNK, G * LANES), lambda h, n: (n, h))
    rowspec = pl.BlockSpec((None, CHUNK, G), lambda h, n: (h, n, 0))
    colspec = lambda rows: pl.BlockSpec((rows, LANES), lambda h, n: (0, h))

    def body(sink_ref, q_ref, do_ref, lse_ref, kp, kc, kn, kx, vp, vc, vn, vx,
             dq_ref, dk_ref, dv_ref, dkx_ref, dvx_ref, dsk_ref):
        h, n = pl.program_id(0), pl.program_id(1)

        @pl.when(n == 0)
        def _():
            for r in (dk_ref, dv_ref, dkx_ref, dvx_ref, dsk_ref):
                r[...] = jnp.zeros_like(r)

        qs, dos = _stack_heads(q_ref, G), _stack_heads(do_ref, G)
        kall = jnp.concatenate([kp[...], kc[...], kn[...], kx[...]], axis=0)
        vall = jnp.concatenate([vp[...], vc[...], vn[...], vx[...]], axis=0)
        lse_c = _stack_columns(lse_ref, G)
        p = jnp.where(_key_mask(n, nb, CTX, G), jnp.exp(_dot(qs, kall, _NT) * scale - lse_c), 0.0)
        dp = _dot(dos, vall, _NT)
        delta = jnp.sum(p * dp, axis=-1, keepdims=True)
        ds = (p * (dp - delta) * scale).astype(BF16)
        dq = _dot(ds, kall)
        dk_all = _dot(ds, qs, _TN)
        dv_all = _dot(p.astype(BF16), dos, _TN)
        for g in range(G):
            dq_ref[:, g * LANES:(g + 1) * LANES] = dq[g * CHUNK:(g + 1) * CHUNK]
        for part, blk in enumerate((jnp.maximum(n - 1, 0), n, jnp.minimum(n + 1, nb - 1))):
            rows = pl.ds(pl.multiple_of(blk * CHUNK, CHUNK), CHUNK)
            dk_ref[rows, :] += dk_all[part * CHUNK:(part + 1) * CHUNK]
            dv_ref[rows, :] += dv_all[part * CHUNK:(part + 1) * CHUNK]
        dkx_ref[...] += dk_all[3 * CHUNK:]
        dvx_ref[...] += dv_all[3 * CHUNK:]
        dsink = -jnp.exp(_sink_column(sink_ref, h, G) - lse_c) * delta
        for g in range(G):
            dsk_ref[g:g + 1, :] += jnp.sum(dsink[g * CHUNK:(g + 1) * CHUNK])

    return pl.pallas_call(
        body, name=name, grid=(Hkv, nb),
        in_specs=[pl.BlockSpec(memory_space=pltpu.SMEM), qspec, qspec, rowspec] + kvs + kvs,
        out_specs=[qspec, colspec(L), colspec(L), colspec(CTX), colspec(CTX),
                   pl.BlockSpec((None, 8, LANES), lambda h, n: (h, 0, 0))],
        out_shape=[jax.ShapeDtypeStruct((L, Hkv * G * LANES), F32),
                   jax.ShapeDtypeStruct((L, Hkv * LANES), F32), jax.ShapeDtypeStruct((L, Hkv * LANES), F32),
                   jax.ShapeDtypeStruct((CTX, Hkv * LANES), F32), jax.ShapeDtypeStruct((CTX, Hkv * LANES), F32),
                   jax.ShapeDtypeStruct((Hkv, 8, LANES), F32)],
        compiler_params=_cparams("parallel", "arbitrary"),
    )(sink, q, do, lse, k, k, k, k, v, v, v, v)


def _attn_assemble(dq, dk_lat, dv_lat, dk_ctx, dv_ctx, cos, sin, *, T, L, CTX, Hq, Hkv, tr, name):
    Wq, Wk = Hq * LANES, Hkv * LANES
    ctx_blocks = CTX // tr
    nl = L // tr

    def unrope(d, c, s_, heads):
        return d * jnp.tile(c, (1, heads)) + _swap32(d * jnp.tile(s_, (1, heads)))

    def body(is_ctx, dqt, dkl, dvl, dkc, dvc, c, s_):
        dq_ = jnp.where(is_ctx, 0.0, unrope(dqt, c, s_, Hq))
        dk_ = unrope(jnp.where(is_ctx, dkc, dkl), c, s_, Hkv)
        dv_ = jnp.where(is_ctx, dvc, dvl)
        return (jnp.concatenate([dq_, dk_, dv_], axis=1),), ()

    def ctx_map(i):
        return (jnp.clip(i - nl, 0, ctx_blocks - 1), 0)

    assert T % tr == 0 and L % tr == 0 and CTX % tr == 0
    in_specs = [pl.BlockSpec((tr, Wq), lambda i: (jnp.minimum(i, nl - 1), 0)),
                pl.BlockSpec((tr, Wk), lambda i: (jnp.minimum(i, nl - 1), 0)),
                pl.BlockSpec((tr, Wk), lambda i: (jnp.minimum(i, nl - 1), 0)),
                pl.BlockSpec((tr, Wk), ctx_map), pl.BlockSpec((tr, Wk), ctx_map),
                pl.BlockSpec((tr, LANES), lambda i: (i, 0)), pl.BlockSpec((tr, LANES), lambda i: (i, 0))]

    def kern(dq_r, dkl_r, dvl_r, dkc_r, dvc_r, c_r, s_r, o_ref):
        is_ctx = pl.program_id(0) >= nl
        (out,), _ = body(is_ctx, dq_r[...], dkl_r[...], dvl_r[...], dkc_r[...], dvc_r[...], c_r[...], s_r[...])
        o_ref[...] = out.astype(o_ref.dtype)

    return pl.pallas_call(
        kern, name=name, grid=(T // tr,), in_specs=in_specs,
        out_specs=pl.BlockSpec((tr, Wq + 2 * Wk), lambda i: (i, 0)),
        out_shape=jax.ShapeDtypeStruct((T, Wq + 2 * Wk), BF16),
        compiler_params=_cparams("parallel"),
    )(dq, dk_lat, dv_lat, dk_ctx, dv_ctx, cos, sin)


def _my_place():
    x, y, c = lax.axis_index("x"), lax.axis_index("y"), lax.axis_index("c")
    return x, y, c


def _all_gather_small(v, *, name):
    R, C = v.shape

    def body(x_ref, out_ref, send_sems, recv_sems, local_sem):
        x, y, c = _my_place()
        me, sibling = (x, y, c), (x, y, 1 - c)
        chips = [(1 - x, y), (x, 1 - y), (1 - x, 1 - y)]

        def slot(px, py, pc):
            return out_ref.at[4 * px + 2 * py + pc]

        def copy(k, block, to, src=None):
            return pltpu.make_async_remote_copy(
                src_ref=slot(*block) if src is None else src, dst_ref=slot(*block),
                send_sem=send_sems.at[k], recv_sem=recv_sems.at[k], device_id=to, device_id_type=MESH)

        mine = pltpu.make_async_copy(x_ref, slot(*me), local_sem)
        mine.start()
        first = [copy(0, me, sibling, src=x_ref)]
        first += [copy(1 + j, me, (*chip, c), src=x_ref) for j, chip in enumerate(chips)]
        for cp in first:
            cp.start()
        passed = [copy(4 + j, (*chip, c), sibling) for j, chip in enumerate(chips)]
        for j, chip in enumerate(chips):
            copy(1 + j, (*chip, c), me).wait_recv()
            passed[j].start()
        copy(0, sibling, me).wait_recv()
        for j, chip in enumerate(chips):
            copy(4 + j, (*chip, 1 - c), me).wait_recv()
        for cp in first + passed:
            cp.wait_send()
        mine.wait()

    return pl.pallas_call(
        body, name=name, out_shape=jax.ShapeDtypeStruct((N_DEV, R, C), v.dtype),
        in_specs=[pl.BlockSpec(memory_space=pltpu.VMEM)], out_specs=pl.BlockSpec(memory_space=pltpu.VMEM),
        scratch_shapes=[pltpu.SemaphoreType.DMA((7,)), pltpu.SemaphoreType.DMA((7,)), pltpu.SemaphoreType.DMA],
    )(v)


def _shard_slice(ref, axis, idx, size):
    if axis == 1:
        return ref.at[:, pl.ds(idx * size, size), :]
    return ref.at[:, :, pl.ds(idx * size, size)]


def _all_gather_weights(shards, axes, *, name):
    nt = len(shards)
    sizes = [s.shape[a] for s, a in zip(shards, axes)]
    out_shape = []
    for s, a in zip(shards, axes):
        shp = list(s.shape)
        shp[a] *= N_DEV
        out_shape.append(jax.ShapeDtypeStruct(tuple(shp), s.dtype))

    def body(*refs):
        ins, outs = refs[:nt], refs[nt:2 * nt]
        send_sems, recv_sems, local_sems = refs[2 * nt:]
        x, y, c = _my_place()
        me, sibling = (x, y, c), (x, y, 1 - c)
        chips = [(1 - x, y), (x, 1 - y), (1 - x, 1 - y)]
        all_sends = []
        locals_ = []
        for t in range(nt):
            def slot(px, py, pc, t=t):
                return _shard_slice(outs[t], axes[t], 4 * px + 2 * py + pc, sizes[t])

            def copy(k, block, to, src=None, t=t, slot=slot):
                return pltpu.make_async_remote_copy(
                    src_ref=slot(*block) if src is None else src, dst_ref=slot(*block),
                    send_sem=send_sems.at[t, k], recv_sem=recv_sems.at[t, k], device_id=to, device_id_type=MESH)

            mine = pltpu.make_async_copy(ins[t], slot(*me), local_sems.at[t])
            mine.start()
            locals_.append(mine)
            first = [copy(0, me, sibling, src=ins[t])]
            first += [copy(1 + j, me, (*chip, c), src=ins[t]) for j, chip in enumerate(chips)]
            for cp in first:
                cp.start()
            all_sends += first
        for t in range(nt):
            def slot(px, py, pc, t=t):
                return _shard_slice(outs[t], axes[t], 4 * px + 2 * py + pc, sizes[t])

            def copy(k, block, to, t=t, slot=slot):
                return pltpu.make_async_remote_copy(
                    src_ref=slot(*block), dst_ref=slot(*block),
                    send_sem=send_sems.at[t, k], recv_sem=recv_sems.at[t, k], device_id=to, device_id_type=MESH)

            passed = [copy(4 + j, (*chip, c), sibling) for j, chip in enumerate(chips)]
            for j, chip in enumerate(chips):
                copy(1 + j, (*chip, c), me).wait_recv()
                passed[j].start()
            all_sends += passed
        for t in range(nt):
            def slot(px, py, pc, t=t):
                return _shard_slice(outs[t], axes[t], 4 * px + 2 * py + pc, sizes[t])

            def copy(k, block, to, t=t, slot=slot):
                return pltpu.make_async_remote_copy(
                    src_ref=slot(*block), dst_ref=slot(*block),
                    send_sem=send_sems.at[t, k], recv_sem=recv_sems.at[t, k], device_id=to, device_id_type=MESH)

            copy(0, sibling, me).wait_recv()
            for j, chip in enumerate(chips):
                copy(4 + j, (*chip, 1 - c), me).wait_recv()
        for cp in all_sends:
            cp.wait_send()
        for mine in locals_:
            mine.wait()

    return pl.pallas_call(
        body, name=name, out_shape=out_shape,
        in_specs=[pl.BlockSpec(memory_space=pl.ANY)] * nt, out_specs=[pl.BlockSpec(memory_space=pl.ANY)] * nt,
        scratch_shapes=[pltpu.SemaphoreType.DMA((nt, 7)), pltpu.SemaphoreType.DMA((nt, 7)),
                        pltpu.SemaphoreType.DMA((nt,))],
    )(*shards)


def _exchange_grads(grads, axes, *, name):
    nt = len(grads)
    sizes = [g.shape[a] // N_DEV for g, a in zip(grads, axes)]
    out_shape = []
    for g, a, sz in zip(grads, axes, sizes):
        shp = list(g.shape)
        shp[a] = sz
        out_shape.append(jax.ShapeDtypeStruct((N_DEV, *shp), g.dtype))

    def body(*refs):
        ins, outs = refs[:nt], refs[nt:2 * nt]
        send_sems, recv_sems, local_sems = refs[2 * nt:]
        x, y, c = _my_place()
        my_idx = 4 * x + 2 * y + c

        def peer(r):
            px = (1 - x) if r & 4 else x
            py = (1 - y) if r & 2 else y
            pc = (1 - c) if r & 1 else c
            return (px, py, pc)

        copies, locals_ = [], []
        for t in range(nt):
            mine = pltpu.make_async_copy(_shard_slice(ins[t], axes[t], my_idx, sizes[t]), outs[t].at[my_idx],
                                         local_sems.at[t])
            mine.start()
            locals_.append(mine)
            for r in range(1, N_DEV):
                p = peer(r)
                p_idx = 4 * p[0] + 2 * p[1] + p[2]
                cp = pltpu.make_async_remote_copy(
                    src_ref=_shard_slice(ins[t], axes[t], p_idx, sizes[t]), dst_ref=outs[t].at[my_idx],
                    send_sem=send_sems.at[t, r - 1], recv_sem=recv_sems.at[t, r - 1], device_id=p, device_id_type=MESH)
                cp.start()
                copies.append((cp, t, r, p_idx))
        for cp, t, r, p_idx in copies:
            pltpu.make_async_remote_copy(
                src_ref=_shard_slice(ins[t], axes[t], p_idx, sizes[t]), dst_ref=outs[t].at[p_idx],
                send_sem=send_sems.at[t, r - 1], recv_sem=recv_sems.at[t, r - 1], device_id=peer(r),
                device_id_type=MESH).wait_recv()
        for cp, _, _, _ in copies:
            cp.wait_send()
        for mine in locals_:
            mine.wait()

    return pl.pallas_call(
        body, name=name, out_shape=out_shape,
        in_specs=[pl.BlockSpec(memory_space=pl.ANY)] * nt, out_specs=[pl.BlockSpec(memory_space=pl.ANY)] * nt,
        scratch_shapes=[pltpu.SemaphoreType.DMA((nt, 7)), pltpu.SemaphoreType.DMA((nt, 7)),
                        pltpu.SemaphoreType.DMA((nt,))],
    )(*grads)


_HBM_SPEC = pl.BlockSpec(memory_space=pltpu.HBM)
_SEM_SPEC = pl.BlockSpec(memory_space=pltpu.SEMAPHORE)
_ANY_SPEC = pl.BlockSpec(memory_space=pl.ANY)
_DATAFLOW = pltpu.SideEffectType.DATAFLOW_SIDE_EFFECTING
N_PEERS = N_DEV - 1


def _peer(r):
    x, y, c = _my_place()
    return ((1 - x) if r & 4 else x, (1 - y) if r & 2 else y, (1 - c) if r & 1 else c)


def _index_of(place):
    return 4 * place[0] + 2 * place[1] + place[2]


def _slot(ref, axis, idx, size):
    if axis == 0:
        return ref.at[pl.ds(idx * size, size), :]
    return ref.at[:, pl.ds(idx * size, size)]


def _cast_place(w3, layer, axis, me_arr, dep, *, name):
    Ks, Ns = w3.shape[1], w3.shape[2]
    tr = _pick(Ks, 256, 16)
    nblk = Ks // tr
    full = (Ks * N_DEV, Ns) if axis == 0 else (Ks, Ns * N_DEV)
    if axis == 0:
        out_map = lambda i, me: (me[0] * nblk + i, 0)
    else:
        out_map = lambda i, me: (i, me[0])

    def body(me_ref, w_ref, dep_ref, o_ref):
        o_ref[...] = w_ref[...].astype(BF16)

    return pl.pallas_call(
        body, name=name, out_shape=jax.ShapeDtypeStruct(full, BF16),
        grid_spec=pltpu.PrefetchScalarGridSpec(
            num_scalar_prefetch=1, grid=(nblk,),
            in_specs=[pl.BlockSpec((None, tr, Ns), lambda i, me: (layer, i, 0)), pl.BlockSpec(memory_space=pl.ANY)],
            out_specs=pl.BlockSpec((tr, Ns), out_map)),
        compiler_params=_cparams("parallel"),
    )(me_arr, w3, dep)


AG_FIRST = 4
AG_CHIPS = 3


def _sibling():
    x, y, c = _my_place()
    return (x, y, 1 - c)


def _chip_peer(j, same_core=True):
    x, y, c = _my_place()
    px = (1 - x) if j in (0, 2) else x
    py = (1 - y) if j in (1, 2) else y
    return (px, py, c if same_core else 1 - c)


def _gather_start(lands, axes, after, *, name):
    nt = len(lands)
    sizes = [l.shape[a] // N_DEV for l, a in zip(lands, axes)]

    def body(*refs):
        ins, send_sems, recv_sems, token = refs[:nt], refs[nt + 1], refs[nt + 2], refs[-1]
        my_idx = _index_of(_my_place())
        for t in range(nt):
            mine = _slot(ins[t], axes[t], my_idx, sizes[t])
            for k, to in enumerate([_sibling()] + [_chip_peer(j) for j in range(AG_CHIPS)]):
                pltpu.make_async_remote_copy(src_ref=mine, dst_ref=mine, send_sem=send_sems.at[t * AG_FIRST + k],
                                             recv_sem=recv_sems.at[t * AG_FIRST + k], device_id=to,
                                             device_id_type=MESH).start()
        token[...] = jnp.zeros_like(token)

    res = pl.pallas_call(
        body, name=name,
        out_shape=(pltpu.SemaphoreType.DMA((nt * AG_FIRST,)), pltpu.SemaphoreType.DMA((nt * AG_FIRST,)),
                   *[pltpu.HBM(l.shape, l.dtype) for l in lands], jax.ShapeDtypeStruct((8, LANES), F32)),
        in_specs=[_HBM_SPEC] * nt + [_ANY_SPEC],
        out_specs=(_SEM_SPEC, _SEM_SPEC, *[_HBM_SPEC] * nt, pl.BlockSpec(memory_space=pltpu.VMEM)),
        input_output_aliases={t: 2 + t for t in range(nt)},
        compiler_params=pltpu.CompilerParams(has_side_effects=_DATAFLOW),
    )(*[pltpu.with_memory_space_constraint(l, pltpu.HBM) for l in lands], after)
    return res[0], res[1], list(res[2:2 + nt]), res[-1]


def _gather_forward(send_a, recv_a, lands, axes, after, *, name):
    nt = len(lands)
    sizes = [l.shape[a] // N_DEV for l, a in zip(lands, axes)]

    def body(*refs):
        ins, send_a, recv_a = refs[:nt], refs[nt], refs[nt + 1]
        send_f, recv_f, token = refs[nt + 3], refs[nt + 4], refs[-1]
        my_idx = _index_of(_my_place())
        for t in range(nt):
            for j in range(AG_CHIPS):
                src_dev = _chip_peer(j)
                arrived = _slot(ins[t], axes[t], _index_of(src_dev), sizes[t])
                pltpu.make_async_remote_copy(
                    src_ref=_slot(ins[t], axes[t], my_idx, sizes[t]), dst_ref=arrived,
                    send_sem=send_a.at[t * AG_FIRST + 1 + j], recv_sem=recv_a.at[t * AG_FIRST + 1 + j],
                    device_id=src_dev, device_id_type=MESH).wait_recv()
                pltpu.make_async_remote_copy(src_ref=arrived, dst_ref=arrived, send_sem=send_f.at[t * AG_CHIPS + j],
                                             recv_sem=recv_f.at[t * AG_CHIPS + j], device_id=_sibling(),
                                             device_id_type=MESH).start()
        token[...] = jnp.zeros_like(token)

    res = pl.pallas_call(
        body, name=name,
        out_shape=(pltpu.SemaphoreType.DMA((nt * AG_CHIPS,)), pltpu.SemaphoreType.DMA((nt * AG_CHIPS,)),
                   *[pltpu.HBM(l.shape, l.dtype) for l in lands], jax.ShapeDtypeStruct((8, LANES), F32)),
        in_specs=[_HBM_SPEC] * nt + [_SEM_SPEC, _SEM_SPEC, _ANY_SPEC],
        out_specs=(_SEM_SPEC, _SEM_SPEC, *[_HBM_SPEC] * nt, pl.BlockSpec(memory_space=pltpu.VMEM)),
        input_output_aliases={t: 2 + t for t in range(nt)},
        compiler_params=pltpu.CompilerParams(has_side_effects=_DATAFLOW),
    )(*lands, send_a, recv_a, after)
    return res[0], res[1], list(res[2:2 + nt]), res[-1]


def _gather_wait(send_a, recv_a, send_f, recv_f, lands, axes, after, *, name):
    nt = len(lands)
    sizes = [l.shape[a] // N_DEV for l, a in zip(lands, axes)]

    def body(*refs):
        ins, send_a, recv_a, send_f, recv_f = refs[:nt], refs[nt], refs[nt + 1], refs[nt + 2], refs[nt + 3]
        my_idx = _index_of(_my_place())
        sib = _sibling()
        for t in range(nt):
            mine = _slot(ins[t], axes[t], my_idx, sizes[t])
            for k, to in enumerate([sib] + [_chip_peer(j) for j in range(AG_CHIPS)]):
                pltpu.make_async_remote_copy(src_ref=mine, dst_ref=mine, send_sem=send_a.at[t * AG_FIRST + k],
                                             recv_sem=recv_a.at[t * AG_FIRST + k], device_id=to,
                                             device_id_type=MESH).wait_send()
            pltpu.make_async_remote_copy(src_ref=mine, dst_ref=_slot(ins[t], axes[t], _index_of(sib), sizes[t]),
                                         send_sem=send_a.at[t * AG_FIRST], recv_sem=recv_a.at[t * AG_FIRST],
                                         device_id=sib, device_id_type=MESH).wait_recv()
            for j in range(AG_CHIPS):
                sent = _slot(ins[t], axes[t], _index_of(_chip_peer(j)), sizes[t])
                got = _slot(ins[t], axes[t], _index_of(_chip_peer(j, same_core=False)), sizes[t])
                cp = pltpu.make_async_remote_copy(src_ref=sent, dst_ref=got, send_sem=send_f.at[t * AG_CHIPS + j],
                                                  recv_sem=recv_f.at[t * AG_CHIPS + j], device_id=sib,
                                                  device_id_type=MESH)
                cp.wait_send()
                cp.wait_recv()

    res = pl.pallas_call(
        body, name=name, out_shape=[pltpu.HBM(l.shape, l.dtype) for l in lands],
        in_specs=[_HBM_SPEC] * nt + [_SEM_SPEC] * 4 + [_ANY_SPEC], out_specs=[_HBM_SPEC] * nt,
        input_output_aliases={t: t for t in range(nt)},
        compiler_params=pltpu.CompilerParams(has_side_effects=_DATAFLOW),
    )(*lands, send_a, recv_a, send_f, recv_f, after)
    return list(res)


def _scatter_start(dw, axis, *, name):
    size = dw.shape[axis] // N_DEV
    land_shape = (N_PEERS, size, dw.shape[1]) if axis == 0 else (N_PEERS, dw.shape[0], size)

    def body(dw_ref, land_ref, send_sems, recv_sems, dw_thru, land_thru, token):
        for r in range(1, N_DEV):
            p = _peer(r)
            pltpu.make_async_remote_copy(src_ref=_slot(dw_ref, axis, _index_of(p), size), dst_ref=land_ref.at[r - 1],
                                         send_sem=send_sems.at[r - 1], recv_sem=recv_sems.at[r - 1], device_id=p,
                                         device_id_type=MESH).start()
        token[...] = jnp.zeros_like(token)

    land = pltpu.with_memory_space_constraint(lax.empty(land_shape, dw.dtype), pltpu.HBM)
    return pl.pallas_call(
        body, name=name,
        out_shape=(pltpu.SemaphoreType.DMA((N_PEERS,)), pltpu.SemaphoreType.DMA((N_PEERS,)),
                   pltpu.HBM(dw.shape, dw.dtype), pltpu.HBM(land_shape, dw.dtype), jax.ShapeDtypeStruct((8, LANES), F32)),
        in_specs=[_HBM_SPEC, _HBM_SPEC],
        out_specs=(_SEM_SPEC, _SEM_SPEC, _HBM_SPEC, _HBM_SPEC, pl.BlockSpec(memory_space=pltpu.VMEM)),
        input_output_aliases={0: 2, 1: 3},
        compiler_params=pltpu.CompilerParams(has_side_effects=_DATAFLOW),
    )(pltpu.with_memory_space_constraint(dw, pltpu.HBM), land)


def _scatter_wait(send_sems, recv_sems, dw, land, axis, after, *, name):
    size = dw.shape[axis] // N_DEV

    def body(dw_ref, land_ref, send_sems, recv_sems, after_ref, dw_thru, land_thru):
        for r in range(1, N_DEV):
            p = _peer(r)
            cp = pltpu.make_async_remote_copy(src_ref=_slot(dw_ref, axis, _index_of(p), size), dst_ref=land_ref.at[r - 1],
                                              send_sem=send_sems.at[r - 1], recv_sem=recv_sems.at[r - 1], device_id=p,
                                              device_id_type=MESH)
            cp.wait_send()
            cp.wait_recv()

    return pl.pallas_call(
        body, name=name, out_shape=(pltpu.HBM(dw.shape, dw.dtype), pltpu.HBM(land.shape, land.dtype)),
        in_specs=[_HBM_SPEC, _HBM_SPEC, _SEM_SPEC, _SEM_SPEC, _ANY_SPEC], out_specs=(_HBM_SPEC, _HBM_SPEC),
        input_output_aliases={0: 0, 1: 1},
        compiler_params=pltpu.CompilerParams(has_side_effects=_DATAFLOW),
    )(dw, land, send_sems, recv_sems, after)


def _adamw_math(w, g, m, v):
    m = ADAM_B1 * m + (1.0 - ADAM_B1) * g
    v = ADAM_B2 * v + (1.0 - ADAM_B2) * (g * g)
    m_hat = m / (1.0 - ADAM_B1 ** ADAM_STEP)
    v_hat = v / (1.0 - ADAM_B2 ** ADAM_STEP)
    delta = -ADAM_LR * (m_hat / (jnp.sqrt(v_hat) + ADAM_EPS) + ADAM_WD * w)
    return delta, m, v


def _adamw_sharded(w, m, v, layer, dw, land, axis, me_arr, prev, *, name):
    nl, Ks, Ns = w.shape
    tr = _pick(Ks, 128, 16)
    nblk = Ks // tr
    if axis == 0:
        own_map = lambda i, me: (me[0] * nblk + i, 0)
    else:
        own_map = lambda i, me: (i, me[0])
    wspec = pl.BlockSpec((None, tr, Ns), lambda i, me: (layer, i, 0))
    n_prev = 0 if prev is None else 4

    def body(me_ref, w_ref, m_ref, v_ref, own_ref, r_ref, *rest):
        g_ref, d_ref, nm_ref, nv_ref = rest[n_prev:]
        g = own_ref[...].astype(F32)
        for r in range(N_PEERS):
            g = g + r_ref[r].astype(F32)
        delta, nm, nv = _adamw_math(w_ref[...], g, m_ref[...], v_ref[...])
        g_ref[...], d_ref[...], nm_ref[...], nv_ref[...] = g, delta, nm, nv

    return pl.pallas_call(
        body, name=name, out_shape=[jax.ShapeDtypeStruct((nl, Ks, Ns), F32)] * 4,
        grid_spec=pltpu.PrefetchScalarGridSpec(
            num_scalar_prefetch=1, grid=(nblk,),
            in_specs=[wspec, wspec, wspec, pl.BlockSpec((tr, Ns), own_map),
                      pl.BlockSpec((N_PEERS, tr, Ns), lambda i, me: (0, i, 0))] + [_ANY_SPEC] * n_prev,
            out_specs=[wspec] * 4),
        input_output_aliases={6 + k: k for k in range(n_prev)},
        compiler_params=_cparams("parallel"),
    )(me_arr, w, m, v, dw, land, *(prev or []))


def _adamw_flat(w, g, m, v, *, name):
    def body(w_ref, g_ref, m_ref, v_ref, d_ref, nm_ref, nv_ref):
        d_ref[...], nm_ref[...], nv_ref[...] = _adamw_math(w_ref[...], g_ref[...], m_ref[...], v_ref[...])

    spec = pl.BlockSpec(memory_space=pltpu.VMEM)
    return pl.pallas_call(body, name=name, in_specs=[spec] * 4, out_specs=[spec] * 3,
                          out_shape=[jax.ShapeDtypeStruct(w.shape, F32)] * 3)(w, g, m, v)


def _sum_devices(a, *, name):
    def body(a_ref, o_ref):
        s = a_ref[0]
        for d in range(1, N_DEV):
            s = s + a_ref[d]
        o_ref[...] = s

    spec = pl.BlockSpec(memory_space=pltpu.VMEM)
    return pl.pallas_call(body, name=name, in_specs=[spec], out_specs=spec,
                          out_shape=jax.ShapeDtypeStruct(a.shape[1:], F32))(a)


def _ada_mods(c16, ada_w, ada_b_cols, *, name):
    nl, D, cols = ada_w.shape
    bn = _pick(cols, 512)

    def body(c_ref, w_ref, b_ref, o_ref):
        cond = _silu(c_ref[...]).astype(BF16)
        o_ref[...] = _dot(cond, w_ref[...].astype(BF16)) + b_ref[...]

    return pl.pallas_call(
        body, name=name, grid=(nl, cols // bn),
        in_specs=[pl.BlockSpec((16, D), lambda l, j: (0, 0)), pl.BlockSpec((None, D, bn), lambda l, j: (l, 0, j)),
                  pl.BlockSpec((None, 1, bn), lambda l, j: (l, 0, j))],
        out_specs=pl.BlockSpec((None, 16, bn), lambda l, j: (l, 0, j)),
        out_shape=jax.ShapeDtypeStruct((nl, 16, cols), F32),
        compiler_params=_cparams("parallel", "parallel"),
    )(c16, ada_w, ada_b_cols)


def _ada_bwd(cond_t, dmod, w, m, v, *, name):
    nl, D, cols = w.shape
    tr = _pick(D, 256, 8)

    def body(ct_ref, dm_ref, w_ref, m_ref, v_ref, g_ref, d_ref, nm_ref, nv_ref, dc_ref):
        ct, dm, wt = ct_ref[...], dm_ref[...], w_ref[...]
        g = ct[:, 0:1] * dm[0:1, :]
        for r in range(1, N_DEV + 1):
            g = g + ct[:, r:r + 1] * dm[r:r + 1, :]
        delta, nm, nv = _adamw_math(wt, g, m_ref[...], v_ref[...])
        g_ref[...], d_ref[...], nm_ref[...], nv_ref[...] = g, delta, nm, nv
        dc_ref[...] = jnp.sum(wt * dm[N_DEV:N_DEV + 1, :], axis=-1, keepdims=True)

    wspec = pl.BlockSpec((None, tr, cols), lambda l, i: (l, i, 0))
    return pl.pallas_call(
        body, name=name, grid=(nl, D // tr),
        in_specs=[pl.BlockSpec((tr, 16), lambda l, i: (i, 0)), pl.BlockSpec((None, 16, cols), lambda l, i: (l, 0, 0)),
                  wspec, wspec, wspec],
        out_specs=[wspec] * 4 + [pl.BlockSpec((None, tr, 1), lambda l, i: (l, i, 0))],
        out_shape=[jax.ShapeDtypeStruct((nl, D, cols), F32)] * 4 + [jax.ShapeDtypeStruct((nl, D, 1), F32)],
        compiler_params=_cparams("parallel", "parallel"),
    )(cond_t, dmod, w, m, v)


def _rope_tables(L, CTX):
    def angles(pos, dim):
        inv_freq = ROPE_BASE ** (-jnp.arange(0, dim, 2, dtype=F32) / dim)
        return pos.astype(F32)[:, None] * inv_freq[None, :]

    def pad(cos, sin):
        return (jnp.concatenate([cos, jnp.ones((CTX, LANES), F32)], 0),
                jnp.concatenate([sin, jnp.zeros((CTX, LANES), F32)], 0))

    ret = angles(jnp.arange(L), 2 * LANES)
    ret_cs = pad(jnp.cos(ret), jnp.sin(ret))
    rows = angles(jnp.arange(L) // GRID_W, ATT_HEAD_DIM // 2)
    cols = angles(jnp.arange(L) % GRID_W, ATT_HEAD_DIM // 2)
    cos = jnp.concatenate([jnp.cos(rows)] * 2 + [jnp.cos(cols)] * 2, axis=1)
    sin = jnp.concatenate([-jnp.sin(rows), jnp.sin(rows), -jnp.sin(cols), jnp.sin(cols)], axis=1)
    return ret_cs, pad(cos, sin)


def kernel(x, c, ctx, c_ctx, ada_w, ada_b, norm_mix_g, norm_mlp_g, mlp_w1, mlp_w2, ret_w_in, ret_w_out, ret_decay_fwd, ret_decay_bwd, attn_w_in, attn_w_out, attn_sink, final_norm_g, loss_target, m_c_ctx, m_ada_w, m_ada_b, m_norm_mix_g, m_norm_mlp_g, m_mlp_w1, m_mlp_w2, m_ret_w_in, m_ret_w_out, m_ret_decay_fwd, m_ret_decay_bwd, m_attn_w_in, m_attn_w_out, m_attn_sink, m_final_norm_g, v_c_ctx, v_ada_w, v_ada_b, v_norm_mix_g, v_norm_mlp_g, v_mlp_w1, v_mlp_w2, v_ret_w_in, v_ret_w_out, v_ret_decay_fwd, v_ret_decay_bwd, v_attn_w_in, v_attn_w_out, v_attn_sink, v_final_norm_g):
    L, D = x.shape[1], x.shape[2]
    CTX = ctx.shape[1]
    T = L + CTX
    RH = ret_decay_fwd.shape[-1]
    assert D == RH * 2 * LANES and ada_w.shape[0] == 2 and ret_w_in.shape[0] == 1 and attn_w_in.shape[0] == 1
    Hq = attn_sink.shape[-1]
    Hkv = (attn_w_in.shape[-1] * N_DEV // ATT_HEAD_DIM - Hq) // 2
    G = Hq // Hkv
    FF = mlp_w1.shape[-1] * N_DEV
    Wq_r, Wv_r = RH * 2 * LANES, RH * 4 * LANES
    acols = ada_w.shape[-1]
    tr = _pick(CTX, 256, 8)
    tr_wide = _pick(CTX, 128, 8)
    bmT = T // 4 if (T % 64 == 0) else T
    bmL = L // 4 if (L % 64 == 0) else L
    x_idx, y_idx, c_idx = lax.axis_index("x"), lax.axis_index("y"), lax.axis_index("c")
    me = 4 * x_idx + 2 * y_idx + c_idx
    me_arr = jnp.reshape(me, (1,)).astype(jnp.int32)

    (rcos, rsin), (acos, asin) = _rope_tables(L, CTX)
    lg_f = jax.nn.log_sigmoid(ret_decay_fwd.astype(F32))
    lg_b = jax.nn.log_sigmoid(ret_decay_bwd.astype(F32))

    c_pad = jnp.concatenate([c.astype(F32), jnp.zeros((7, D), F32)], 0)
    c_all = _all_gather_small(c_pad, name="ag_c")[:, 0, :]
    c16 = jnp.concatenate([c_all, c_ctx[None, :], jnp.zeros((7, D), F32)], 0)
    ada_b_cols = lax.dynamic_slice_in_dim(ada_b, me * acols, acols, axis=1)[:, None, :]
    mods_shard = _ada_mods(c16, ada_w, ada_b_cols, name="ada_mods")
    mods_all = _all_gather_small(mods_shard.reshape(32, acols), name="ag_mods")

    wdefs = {"ret_in": (ret_w_in, 0, 1), "ret_out": (ret_w_out, 0, 0), "w1_0": (mlp_w1, 0, 1), "w2_0": (mlp_w2, 0, 0),
             "attn_in": (attn_w_in, 0, 1), "attn_out": (attn_w_out, 0, 0), "w1_1": (mlp_w1, 1, 1), "w2_1": (mlp_w2, 1, 0)}
    groups = [["ret_in"], ["ret_out", "w1_0", "w2_0"], ["attn_in", "attn_out"], ["w1_1", "w2_1"]]

    placed = {}

    def ag_start(gi, after):
        g_axes = [wdefs[k][2] for k in groups[gi]]
        ssem, rsem, lands, tok_ = _gather_start([placed[k] for k in groups[gi]], g_axes, after, name=f"ag_start{gi}")
        return dict(a=(ssem, rsem), lands=lands, axes=g_axes, gi=gi), tok_

    def ag_forward(g, after):
        fs, fr, g["lands"], tok_ = _gather_forward(*g["a"], g["lands"], g["axes"], after, name=f"ag_forward{g['gi']}")
        g["f"] = (fs, fr)
        return tok_

    def ag_wait(g, after):
        return _gather_wait(*g["a"], *g["f"], g["lands"], g["axes"], after, name=f"ag_wait{g['gi']}")

    placed["ret_in"] = _cast_place(*wdefs["ret_in"], me_arr, mods_all, name="place_ret_in")
    g0, tok = ag_start(0, mods_all)
    last_cast = tok
    for keys in groups[1:]:
        for k in keys:
            last_cast = placed[k] = _cast_place(*wdefs[k], me_arr, last_cast, name=f"place_{k}")
    mods_all = (mods_all + tok[0, 0]).reshape(N_DEV, 2, 16, acols).transpose(1, 2, 0, 3).reshape(2, 16, 6, D)
    mod_lat = lax.dynamic_index_in_dim(mods_all, me, axis=1, keepdims=False)
    mod_ctx = mods_all[:, N_DEV]

    def pack(i, ks, kc):
        return jnp.stack([mod_lat[i, ks], mod_lat[i, kc], mod_ctx[i, ks], mod_ctx[i, kc]], 0)

    def gates(i, k):
        return jnp.stack([mod_lat[i, k], mod_ctx[i, k]], 0)

    def gate_epilogue(gl, gc, x_rows_lat_only):
        def epi(acc, i, j, xt, gv):
            if x_rows_lat_only:
                gate = gv[0:1, :]
            else:
                row = i * acc.shape[0] + lax.broadcasted_iota(jnp.int32, (acc.shape[0], 1), 0)
                gate = jnp.where(row >= L, gv[1:2, :], gv[0:1, :])
            return xt + gate * acc, acc
        return epi

    w1, w2 = {}, {}

    mmT = dict(M=T, bm=bmT)
    mmL = dict(M=L, bm=bmL)

    def bn_of(n, off=0):
        b = MM_BN
        while n % b or off % b:
            b -= LANES
        return b

    X0 = jnp.concatenate([x[0], ctx[0]], axis=0)
    g_mix0, g_mlp0 = norm_mix_g[0:1], norm_mlp_g[0:1]
    g_mix1, g_mlp1 = norm_mix_g[1:2], norm_mlp_g[1:2]
    a0 = _normmod(X0, g_mix0, pack(0, 0, 1), R=T, L=L, tr=tr, name="normmod_mix0", dep=last_cast)
    tok = ag_forward(g0, a0)
    (wr_in,) = ag_wait(g0, tok)
    g1, tok = ag_start(1, wr_in)

    bn_qk = _pick(Wq_r, MM_BN, 2 * LANES)
    nq_blocks = Wq_r // bn_qk
    kscale = float(2 * LANES) ** -0.5

    def rope_epi(acc, i, j, cos, sin):
        parts = []
        for h in range(acc.shape[1] // (2 * LANES)):
            x1 = acc[:, h * 2 * LANES:h * 2 * LANES + LANES]
            x2 = acc[:, h * 2 * LANES + LANES:(h + 1) * 2 * LANES]
            parts += [x1 * cos - x2 * sin, x2 * cos + x1 * sin]
        return (jnp.concatenate(parts, axis=1) * jnp.where(j < nq_blocks, 1.0, kscale),)

    def row_tile(arr, bm):
        return (arr, (bm, LANES), lambda i, j: (i, 0))

    (qk0,) = _mm(a0, wr_in, "nn", [BF16], N=2 * Wq_r, K=D, bn=bn_qk, bk=D, name="ret_qk", epilogue=rope_epi,
                 extras=[row_tile(rcos, bmT), row_tile(rsin, bmT)], dep=tok, **mmT)
    bn_vg = bn_of(2 * Wv_r, 2 * Wq_r)
    (vg0,) = _mm(a0, wr_in, "nn", [BF16], N=2 * Wv_r, K=D, bn=bn_vg, bk=D, name="ret_vg", b_col0=2 * Wq_r, dep=tok,
                 **mmT)

    of, st_f = _ret_fwd(qk0, vg0, lg_f, T=T, L=L, H=RH, rev=False, name="ret_scan_f")
    ob, st_b = _ret_fwd(qk0, vg0, lg_b, T=T, L=L, H=RH, rev=True, name="ret_scan_b")
    tok = ag_forward(g1, of[:8, :LANES] + ob[:8, :LANES])
    z0 = _readout(of, ob, vg0, T=T, L=L, H=RH, tr=tr_wide, name="ret_readout", dep=tok)
    wr_out, w1[0], w2[0] = ag_wait(g1, z0)
    g2, tok = ag_start(2, wr_out)
    g3, tok = ag_start(3, tok)

    bnD = _pick(D, MM_BN)

    def xtile(arr, bm):
        return (arr, (bm, bnD), lambda i, j: (i, j))

    def gtile(gv):
        return (gv, (2, bnD), lambda i, j: (0, j))

    bk_v = _pick(Wv_r, 2048)
    X1, ro0 = _mm(z0, wr_out, "nn", [F32, BF16], N=D, K=Wv_r, bn=bnD, bk=bk_v, name="ret_out",
                  epilogue=gate_epilogue(None, None, False), extras=[xtile(X0, bmT), gtile(gates(0, 2))], dep=tok, **mmT)

    def mlp_fwd(Xin, i, g_mlp, rows, name, between=None):
        a = _normmod(Xin, g_mlp, pack(i, 3, 4), R=rows["M"], L=L, tr=tr, name=f"normmod_mlp{name}")

        def relu2(acc, i_, j_):
            u = jnp.maximum(acc, 0.0)
            return u, u * u

        bnF = _pick(FF, MM_BN)
        u, r = _mm(a, w1[i], "nn", [BF16, BF16], N=FF, K=D, bn=bnF, bk=D, name=f"mlp_up{name}", epilogue=relu2, **rows)
        dep = None if between is None else between(u)
        bkF = _pick(FF, 2048)
        Xout, mo = _mm(r, w2[i], "nn", [F32, BF16], N=D, K=FF, bn=bnD, bk=bkF, name=f"mlp_down{name}",
                       epilogue=gate_epilogue(None, None, rows["M"] == L),
                       extras=[xtile(Xin, rows["bm"]), gtile(gates(i, 5))], dep=dep, **rows)
        return a, u, r, Xout, mo

    a1, u0, r0, X2, mo0 = mlp_fwd(X1, 0, g_mlp0, mmT, "0")

    tok = ag_forward(g2, X2)
    a2 = _normmod(X2, g_mix1, pack(1, 0, 1), R=T, L=L, tr=tr, name="normmod_mix1", dep=tok)
    wa_in, wa_out = ag_wait(g2, a2)
    Wq_a, Wk_a = Hq * LANES, Hkv * LANES

    def arope_epi(acc, i, j, cos, sin):
        heads = acc.shape[1] // LANES
        return (acc * jnp.tile(cos, (1, heads)) + _swap32(acc) * jnp.tile(sin, (1, heads)),)

    bn_q = _pick(Wq_a, MM_BN)
    (q1,) = _mm(a2, wa_in, "nn", [BF16], N=Wq_a, K=D, bn=bn_q, bk=D, name="attn_q", epilogue=arope_epi,
                extras=[row_tile(acos, bmL), row_tile(asin, bmL)], **mmL)
    bn_k = bn_of(Wk_a, Wq_a)
    (k1,) = _mm(a2, wa_in, "nn", [BF16], N=Wk_a, K=D, bn=bn_k, bk=D, name="attn_k", b_col0=Wq_a, epilogue=arope_epi,
                extras=[row_tile(acos, bmT), row_tile(asin, bmT)], **mmT)
    bn_v = bn_of(Wk_a, Wq_a + Wk_a)
    (v1,) = _mm(a2, wa_in, "nn", [BF16], N=Wk_a, K=D, bn=bn_v, bk=D, name="attn_v", b_col0=Wq_a + Wk_a, **mmT)
    tok = ag_forward(g3, q1)
    o1, lse = _attn_fwd(q1, k1, v1, attn_sink + tok[0, 0], L=L, CTX=CTX, Hkv=Hkv, G=G, name="attn_fwd")
    w1[1], w2[1] = ag_wait(g3, o1)
    X3, ao = _mm(o1, wa_out, "nn", [F32, BF16], N=D, K=Wq_a, bn=bnD, bk=_pick(Wq_a, 2048), name="attn_out",
                 epilogue=gate_epilogue(None, None, True), extras=[xtile(X2, bmL), gtile(gates(1, 2))], **mmL)
    a3, u1, r1, X4, mo1 = mlp_fwd(X3, 1, g_mlp1, mmL, "1")

    dX4, dmo1, acc_head = _loss_head(X4, loss_target[0], mo1, final_norm_g[None, :], gates(1, 5)[0:1], L=L, tr=tr,
                                     name="loss_head")
    loss_part = jnp.sum(acc_head[0, 0])
    d_gf = acc_head[0, 1]
    zeros_d = jnp.zeros((D,), F32)
    dmod_lat = [[zeros_d] * 6, [zeros_d] * 6]
    dmod_ctx = [[zeros_d] * 6, [zeros_d] * 6]
    dmod_lat[1][5] = acc_head[0, 2]

    bn_dw = 2048

    def mlp_bwd(dmo, a, u, r, i, rows, name):
        Mr = rows["M"]
        bkr = rows["bm"]

        def times_2u(acc, i_, j_, ut):
            return (acc * (2.0 * ut.astype(F32)),)

        bnF = _pick(FF, MM_BN)
        (dw2,) = _mm(r, dmo, "tn", [BF16], M=FF, N=D, K=Mr, bm=_pick(FF, 1024), bn=_pick(D, bn_dw), bk=bkr,
                     name=f"mlp_down_dw{name}")
        tok_ = send_grad(f"w2_{i}", dw2, 0)
        (dh,) = _mm(dmo, w2[i], "nt", [BF16], N=FF, K=D, bn=bnF, bk=D, name=f"mlp_down_dx{name}", epilogue=times_2u,
                    extras=[(u, (rows["bm"], bnF), lambda i_, j_: (i_, j_))], dep=tok_, **rows)
        (dw1,) = _mm(a, dh, "tn", [BF16], M=D, N=FF, K=Mr, bm=_pick(D, 1024), bn=_pick(FF, bn_dw), bk=bkr,
                     name=f"mlp_up_dw{name}")
        tok_ = send_grad(f"w1_{i}", dw1, 1)
        (da,) = _mm(dh, w1[i], "nt", [F32], N=D, K=FF, bn=bnD, bk=_pick(FF, 2048), name=f"mlp_up_dx{name}", dep=tok_,
                    **rows)
        return da

    pending = []

    def send_grad(key, dw, axis):
        ssem, rsem, dw_thru, land, tok_ = _scatter_start(dw, axis, name=f"rs_start_{key}")
        pending.append((key, axis, ssem, rsem, dw_thru, land))
        return tok_

    da3 = mlp_bwd(dmo1, a3, u1, r1, 1, mmL, "1")
    dX3, dao, acc = _normmod_bwd(X3, da3, dX4, False, g_mlp1, pack(1, 3, 4), (ao, gates(1, 2)), R=L, L=L, tr=tr,
                                 name="normmod_mlp1_bwd")
    dmod_lat[1][3], dmod_lat[1][4], d_gmlp1, dmod_lat[1][2] = acc[0, 0], acc[0, 1], acc[0, 2], acc[0, 3]

    (dwa_out,) = _mm(o1, dao, "tn", [BF16], M=Wq_a, N=D, K=L, bm=_pick(Wq_a, 1024), bn=_pick(D, bn_dw), bk=bmL,
                     name="attn_out_dw")
    tok = send_grad("attn_out", dwa_out, 0)
    (do1,) = _mm(dao, wa_out, "nt", [BF16], N=Wq_a, K=D, bn=_pick(Wq_a, MM_BN), bk=D, name="attn_out_dx", dep=tok, **mmL)
    dq1, dk1, dv1, dkx, dvx, dsink_acc = _attn_bwd(q1, k1, v1, do1, lse, attn_sink, L=L, CTX=CTX, Hkv=Hkv, G=G,
                                                   name="attn_bwd")
    dp1 = _attn_assemble(dq1, dk1, dv1, dkx, dvx, acos, asin, T=T, L=L, CTX=CTX, Hq=Hq, Hkv=Hkv, tr=tr_wide,
                         name="attn_assemble")
    Wa_in = Wq_a + 2 * Wk_a
    (dwa_in,) = _mm(a2, dp1, "tn", [BF16], M=D, N=Wa_in, K=T, bm=_pick(D, 1024), bn=_pick(Wa_in, bn_dw), bk=bmT,
                    name="attn_in_dw")
    tok = send_grad("attn_in", dwa_in, 1)
    (da2,) = _mm(dp1, wa_in, "nt", [F32], N=D, K=Wa_in, bn=bnD, bk=_pick(Wa_in, 2048), name="attn_in_dx", dep=tok,
                 **mmT)
    dX2, dmo0, acc = _normmod_bwd(X2, da2, dX3, True, g_mix1, pack(1, 0, 1), (mo0, gates(0, 5)), R=T, L=L, tr=tr,
                                  name="normmod_mix1_bwd")
    dmod_lat[1][0], dmod_lat[1][1], d_gmix1, dmod_lat[0][5] = acc[0, 0], acc[0, 1], acc[0, 2] + acc[1, 2], acc[0, 3]
    dmod_ctx[1][0], dmod_ctx[1][1], dmod_ctx[0][5] = acc[1, 0], acc[1, 1], acc[1, 3]

    da1 = mlp_bwd(dmo0, a1, u0, r0, 0, mmT, "0")
    dX1, dro0, acc = _normmod_bwd(X1, da1, dX2, False, g_mlp0, pack(0, 3, 4), (ro0, gates(0, 2)), R=T, L=L, tr=tr,
                                  name="normmod_mlp0_bwd")
    dmod_lat[0][3], dmod_lat[0][4], d_gmlp0, dmod_lat[0][2] = acc[0, 0], acc[0, 1], acc[0, 2] + acc[1, 2], acc[0, 3]
    dmod_ctx[0][3], dmod_ctx[0][4], dmod_ctx[0][2] = acc[1, 0], acc[1, 1], acc[1, 3]

    (dwr_out,) = _mm(z0, dro0, "tn", [BF16], M=Wv_r, N=D, K=T, bm=_pick(Wv_r, 1024), bn=_pick(D, bn_dw), bk=bmT,
                     name="ret_out_dw")
    tok = send_grad("ret_out", dwr_out, 0)
    (dz0,) = _mm(dro0, wr_out, "nt", [BF16], N=Wv_r, K=D, bn=_pick(Wv_r, MM_BN), bk=D, name="ret_out_dx", dep=tok, **mmT)
    do0, dg0 = _readout_bwd(dz0, of, ob, vg0, T=T, L=L, H=RH, tr=tr_wide, name="ret_readout_bwd")
    dq_f, dk_f, dv_f, dlg_f = _ret_bwd(qk0, vg0, do0, st_f, lg_f, T=T, L=L, H=RH, rev=False, name="ret_scan_f_bwd")
    dq_b, dk_b, dv_b, dlg_b = _ret_bwd(qk0, vg0, do0, st_b, lg_b, T=T, L=L, H=RH, rev=True, name="ret_scan_b_bwd")
    dp0 = _ret_assemble(dq_f, dq_b, dk_f, dk_b, dv_f, dv_b, dg0, rcos, rsin, T=T, L=L, H=RH, tr=tr_wide,
                        name="ret_assemble")
    Wr_in = 2 * Wq_r + 2 * Wv_r
    (dwr_in,) = _mm(a0, dp0, "tn", [BF16], M=D, N=Wr_in, K=T, bm=_pick(D, 1024), bn=_pick(Wr_in, bn_dw), bk=bmT,
                    name="ret_in_dw")
    tok = send_grad("ret_in", dwr_in, 1)
    (da0,) = _mm(dp0, wr_in, "nt", [F32], N=D, K=Wr_in, bn=bnD, bk=_pick(Wr_in, 2048), name="ret_in_dx", dep=tok, **mmT)
    dX0, acc = _normmod_bwd(X0, da0, dX1, False, g_mix0, pack(0, 0, 1), None, R=T, L=L, tr=tr, name="normmod_mix0_bwd")
    dmod_lat[0][0], dmod_lat[0][1], d_gmix0 = acc[0, 0], acc[0, 1], acc[0, 2] + acc[1, 2]
    dmod_ctx[0][0], dmod_ctx[0][1] = acc[1, 0], acc[1, 1]
    grad_x = dX0[:L][None]

    wmv = {"ret_in": (ret_w_in, m_ret_w_in, v_ret_w_in, 0, "ret_w_in"),
           "ret_out": (ret_w_out, m_ret_w_out, v_ret_w_out, 0, "ret_w_out"),
           "attn_in": (attn_w_in, m_attn_w_in, v_attn_w_in, 0, "attn_w_in"),
           "attn_out": (attn_w_out, m_attn_w_out, v_attn_w_out, 0, "attn_w_out"),
           "w1_0": (mlp_w1, m_mlp_w1, v_mlp_w1, 0, "mlp_w1"), "w1_1": (mlp_w1, m_mlp_w1, v_mlp_w1, 1, "mlp_w1"),
           "w2_0": (mlp_w2, m_mlp_w2, v_mlp_w2, 0, "mlp_w2"), "w2_1": (mlp_w2, m_mlp_w2, v_mlp_w2, 1, "mlp_w2")}
    big = {}

    def finish_grad(entry, after):
        key, axis, ssem, rsem, dw_thru, land = entry
        dw_done, land_done = _scatter_wait(ssem, rsem, dw_thru, land, axis, after, name=f"rs_wait_{key}")
        w_, m_, v_, layer, out_name = wmv[key]
        big[out_name] = _adamw_sharded(w_, m_, v_, layer, dw_done, land_done, axis, me_arr, big.get(out_name),
                                       name=f"adamw_{key}")
        return big[out_name][0]

    after = dX0
    for entry in pending[:-1]:
        after = finish_grad(entry, after)

    misc = jnp.zeros((D,), F32)
    misc = misc.at[0:RH].set(dlg_f[:, 0, 0]).at[RH:2 * RH].set(dlg_b[:, 0, 0])
    misc = misc.at[2 * RH:2 * RH + Hq].set(dsink_acc[:, :G, 0].reshape(Hq)).at[2 * RH + Hq].set(loss_part)
    rows = ([dmod_lat[i][k] for i in range(2) for k in range(6)] + [dmod_ctx[i][k] for i in range(2) for k in range(6)]
            + [d_gmix0, d_gmix1, d_gmlp0, d_gmlp1, d_gf, misc, zeros_d, zeros_d])
    part = jnp.stack(rows, 0)
    part_all = _all_gather_small(part, name="ag_small_grads")
    tot = _sum_devices(part_all, name="sum_small_grads")

    grad_ada_b = (tot[0:12] + tot[12:24]).reshape(2, 6 * D)
    grad_norm_mix_g, grad_norm_mlp_g, grad_final_norm_g = tot[24:26], tot[26:28], tot[28]
    grad_ret_decay_fwd = (tot[29, 0:RH] * jax.nn.sigmoid(-ret_decay_fwd[0]))[None]
    grad_ret_decay_bwd = (tot[29, RH:2 * RH] * jax.nn.sigmoid(-ret_decay_bwd[0]))[None]
    grad_attn_sink = tot[29, 2 * RH:2 * RH + Hq][None]
    loss = tot[29, 2 * RH + Hq]

    dlat_cols = lax.dynamic_slice_in_dim(part_all[:, 0:12].reshape(N_DEV, 2, 6 * D), me * acols, acols, axis=2)
    dctx_cols = lax.dynamic_slice_in_dim(tot[12:24].reshape(2, 6 * D), me * acols, acols, axis=1)
    dmod16 = jnp.concatenate([dlat_cols.transpose(1, 0, 2), dctx_cols[:, None, :], jnp.zeros((2, 7, acols), F32)], 1)
    cond_t = _silu(c16).T
    g_ada, d_ada, nm_ada, nv_ada, dcond_part = _ada_bwd(cond_t, dmod16, ada_w, m_ada_w, v_ada_w, name="ada_bwd")
    dcond = (dcond_part[0, :, 0] + dcond_part[1, :, 0]).reshape(D // LANES, LANES)
    pad_rows = -(D // LANES) % 8
    dcond_pad = jnp.concatenate([dcond, jnp.zeros((pad_rows, LANES), F32)], 0) if pad_rows else dcond
    dcond_all = _all_gather_small(dcond_pad, name="ag_dcond")
    dcond_tot = _sum_devices(dcond_all, name="sum_dcond")[:D // LANES].reshape(D)
    sg = jax.nn.sigmoid(c_ctx)
    grad_c_ctx = dcond_tot * (sg * (1.0 + c_ctx * (1.0 - sg)))

    small_w = [c_ctx, ada_b, norm_mix_g, norm_mlp_g, ret_decay_fwd, ret_decay_bwd, attn_sink, final_norm_g]
    small_g = [grad_c_ctx, grad_ada_b, grad_norm_mix_g, grad_norm_mlp_g, grad_ret_decay_fwd, grad_ret_decay_bwd,
               grad_attn_sink, grad_final_norm_g]
    small_m = [m_c_ctx, m_ada_b, m_norm_mix_g, m_norm_mlp_g, m_ret_decay_fwd, m_ret_decay_bwd, m_attn_sink,
               m_final_norm_g]
    small_v = [v_c_ctx, v_ada_b, v_norm_mix_g, v_norm_mlp_g, v_ret_decay_fwd, v_ret_decay_bwd, v_attn_sink,
               v_final_norm_g]
    sizes = [w_.size for w_ in small_w]
    total = sum(-(-s // LANES) * LANES for s in sizes)
    total_pad = -(-total // (8 * LANES)) * 8 * LANES

    def flat_pack(ts, fill):
        pieces = []
        for t_ in ts:
            f = t_.reshape(-1).astype(F32)
            pad = -f.size % LANES
            pieces.append(jnp.concatenate([f, jnp.full((pad,), fill, F32)]) if pad else f)
        pieces.append(jnp.full((total_pad - total,), fill, F32))
        return jnp.concatenate(pieces).reshape(total_pad // LANES, LANES)

    d_s, nm_s, nv_s = _adamw_flat(flat_pack(small_w, 0.0), flat_pack(small_g, 0.0), flat_pack(small_m, 0.0),
                                  flat_pack(small_v, 1.0), name="adamw_small")

    def unpack(p):
        flat = p.reshape(-1)
        res, off = [], 0
        for w_, s in zip(small_w, sizes):
            res.append(flat[off:off + s].reshape(w_.shape))
            off += -(-s // LANES) * LANES
        return res

    finish_grad(pending[-1], d_s)
    d_small, nm_small, nv_small = unpack(d_s), unpack(nm_s), unpack(nv_s)
    small_names = ["c_ctx", "ada_b", "norm_mix_g", "norm_mlp_g", "ret_decay_fwd", "ret_decay_bwd", "attn_sink",
                   "final_norm_g"]
    sm = {n: (g_, d_, m_, v_) for n, g_, d_, m_, v_ in zip(small_names, small_g, d_small, nm_small, nv_small)}

    def out4(n):
        if n == "ada_w":
            return g_ada, d_ada, nm_ada, nv_ada
        if n in big:
            return tuple(big[n])
        return sm[n]

    order = ["c_ctx", "ada_w", "ada_b", "norm_mix_g", "norm_mlp_g", "mlp_w1", "mlp_w2", "ret_w_in", "ret_w_out",
             "ret_decay_fwd", "ret_decay_bwd", "attn_w_in", "attn_w_out", "attn_sink", "final_norm_g"]
    quads = [out4(n) for n in order]
    return (loss, grad_x, *[q_[0] for q_ in quads], *[q_[1] for q_ in quads], *[q_[2] for q_ in quads],
            *[q_[3] for q_ in quads])
```

```python
import functools

import jax
import jax.numpy as jnp
from jax import lax
from jax.experimental import pallas as pl
from jax.experimental.pallas import tpu as pltpu

F32 = jnp.float32
BF16 = jnp.bfloat16

N_DEV = 8
NORM_EPS = 1e-6
CHUNK = 128
ATT_HEAD_DIM = 128
GRID_W = 64
ROPE_BASE = 10000.0
NEG_INF = -1e30
ADAM_LR, ADAM_B1, ADAM_B2, ADAM_EPS, ADAM_WD, ADAM_STEP = 0.001, 0.9, 0.999, 1e-08, 0.01, 10

V7X_VMEM_LIMIT_BYTES = 56 * 1024 * 1024
MM_BN = 1024
LANES = 128
MESH = pl.DeviceIdType.MESH

_NN = (((1,), (0,)), ((), ()))
_NT = (((1,), (1,)), ((), ()))
_TN = (((0,), (0,)), ((), ()))


def _dot(a, b, dn=_NN):
    return lax.dot_general(a, b, dn, preferred_element_type=F32)


def _cparams(*sem):
    return pltpu.CompilerParams(dimension_semantics=sem, vmem_limit_bytes=V7X_VMEM_LIMIT_BYTES)


def _pick(n, pref, mult=LANES):
    if n <= pref:
        return n
    best = None
    for d in range(mult, pref + 1, mult):
        if n % d == 0:
            best = d
    assert best is not None, (n, pref)
    return best


def _silu(x):
    return x * jax.nn.sigmoid(x)


def _mm(a, b, mode, out_dtypes, *, M, N, K, bm, bn, bk, name, b_col0=0, epilogue=None, extras=(), dep=None):
    assert M % bm == 0 and N % bn == 0 and K % bk == 0 and b_col0 % bn == 0, (name, M, N, K, bm, bn, bk, b_col0)
    nk = K // bk
    c0 = b_col0 // bn
    if mode == "nn":
        a_spec = pl.BlockSpec((bm, bk), lambda i, j, k: (i, k))
        b_spec = pl.BlockSpec((bk, bn), lambda i, j, k: (k, j + c0))
    elif mode == "nt":
        a_spec = pl.BlockSpec((bm, bk), lambda i, j, k: (i, k))
        b_spec = pl.BlockSpec((bn, bk), lambda i, j, k: (j + c0, k))
    else:
        a_spec = pl.BlockSpec((bk, bm), lambda i, j, k: (k, i))
        b_spec = pl.BlockSpec((bk, bn), lambda i, j, k: (k, j + c0))
    dn = {"nn": _NN, "nt": _NT, "tn": _TN}[mode]
    e_specs = [pl.BlockSpec(bs, (lambda i, j, k, f=f: f(i, j))) for (_, bs, f) in extras]
    ne, no = len(extras), len(out_dtypes)
    nd = 0 if dep is None else 1

    def body(a_ref, b_ref, *rest):
        e_refs, o_refs = rest[:ne], rest[ne + nd:ne + nd + no]
        i, j, k = pl.program_id(0), pl.program_id(1), pl.program_id(2)

        def finish(acc):
            outs = (acc,) if epilogue is None else epilogue(acc, i, j, *[e[...] for e in e_refs])
            for o_ref, o in zip(o_refs, outs):
                o_ref[...] = o.astype(o_ref.dtype)

        p = _dot(a_ref[...], b_ref[...], dn)
        if nk == 1:
            finish(p)
        else:
            acc_ref = rest[-1]

            @pl.when(k == 0)
            def _():
                acc_ref[...] = p

            @pl.when(k > 0)
            def _():
                acc_ref[...] += p

            @pl.when(k == nk - 1)
            def _():
                finish(acc_ref[...])

    outs = pl.pallas_call(
        body, name=name, grid=(M // bm, N // bn, nk),
        in_specs=[a_spec, b_spec] + e_specs + [pl.BlockSpec(memory_space=pl.ANY)] * nd,
        out_specs=[pl.BlockSpec((bm, bn), lambda i, j, k: (i, j)) for _ in out_dtypes],
        out_shape=[jax.ShapeDtypeStruct((M, N), dt) for dt in out_dtypes],
        scratch_shapes=[pltpu.VMEM((bm, bn), F32)] if nk > 1 else [],
        compiler_params=_cparams("parallel", "parallel", "arbitrary"),
    )(a, b, *[e[0] for e in extras], *([dep] if nd else []))
    return outs


def _rowwise(body, rows, vecs, outs, n_acc, *, R, L, tr, name, acc_width=None, dep=None):
    assert R % tr == 0 and L % tr == 0, (name, R, L, tr)
    nl = L // tr
    n_regions = 2 if R > L else 1
    n_rows, n_vecs, n_outs = len(rows), len(vecs), len(outs)
    n_dep = 0 if dep is None else 1
    acc_pad = -(-n_acc // 8) * 8 if n_acc else 0

    in_specs = []
    for (_, w, cb, lat_only) in rows:
        if lat_only:
            in_specs.append(pl.BlockSpec((tr, w), lambda i, cb=cb: (jnp.minimum(i, nl - 1), cb)))
        else:
            in_specs.append(pl.BlockSpec((tr, w), lambda i, cb=cb: (i, cb)))
    for v in vecs:
        in_specs.append(pl.BlockSpec(v.shape, lambda i, nd=v.ndim: (0,) * nd))
    in_specs += [pl.BlockSpec(memory_space=pl.ANY)] * n_dep
    out_specs = [pl.BlockSpec((tr, w), lambda i: (i, 0)) for (w, _) in outs]
    out_shape = [jax.ShapeDtypeStruct((R, w), dt) for (w, dt) in outs]
    if n_acc:
        out_specs.append(pl.BlockSpec((None, acc_pad, acc_width), lambda i: (jnp.where(i >= nl, 1, 0), 0, 0)))
        out_shape.append(jax.ShapeDtypeStruct((n_regions, acc_pad, acc_width), F32))

    def kern(*refs):
        i = pl.program_id(0)
        is_ctx = i >= nl
        ins = [r[...] for r in refs[:n_rows + n_vecs]]
        o_refs = refs[n_rows + n_vecs + n_dep:]
        out_tiles, acc_rows = body(is_ctx, *ins)
        for o_ref, o in zip(o_refs[:n_outs], out_tiles):
            o_ref[...] = o.astype(o_ref.dtype)
        if n_acc:
            acc_ref = o_refs[n_outs]

            @pl.when((i == 0) | (i == nl))
            def _():
                acc_ref[...] = jnp.zeros_like(acc_ref)

            for r, row in enumerate(acc_rows):
                acc_ref[r:r + 1, :] += row

    res = pl.pallas_call(
        kern, name=name, grid=(R // tr,), in_specs=in_specs, out_specs=out_specs, out_shape=out_shape,
        compiler_params=_cparams("arbitrary"),
    )(*[r[0] for r in rows], *vecs, *([dep] if n_dep else []))
    return res


def _colsum(x):
    return jnp.sum(x, axis=0, keepdims=True)


def _rms_stats(x):
    r = lax.rsqrt(jnp.mean(x * x, axis=-1, keepdims=True) + NORM_EPS)
    return x * r, r


def _sel(is_ctx, pk, lat_row, ctx_row):
    return jnp.where(is_ctx, pk[ctx_row:ctx_row + 1, :], pk[lat_row:lat_row + 1, :])


def _normmod(x, g, pk, *, R, L, tr, name, dep=None):
    D = x.shape[-1]

    def body(is_ctx, xt, gv, pkv):
        xh, _ = _rms_stats(xt)
        sh, sc = _sel(is_ctx, pkv, 0, 2), _sel(is_ctx, pkv, 1, 3)
        return ((xh * gv) * (1.0 + sc) + sh,), ()

    return _rowwise(body, [(x, D, 0, False)], [g, pk], [(D, BF16)], 0, R=R, L=L, tr=tr, name=name, dep=dep)[0]


def _normmod_bwd(x_in, da, dx_out, dx_out_lat_only, g, pk, prev, *, R, L, tr, name):
    D = x_in.shape[-1]
    has_prev = prev is not None

    def body(is_ctx, *t):
        if has_prev:
            xt, dat, dxo, mp, gv, pkv, gates = t
        else:
            xt, dat, dxo, gv, pkv = t
        xh, r = _rms_stats(xt)
        sc = _sel(is_ctx, pkv, 1, 3)
        if dx_out_lat_only:
            dxo = jnp.where(is_ctx, 0.0, dxo)
        dn = dat * (1.0 + sc)
        w = dn * gv
        dxi = dxo + r * (w - xh * jnp.mean(w * xh, axis=-1, keepdims=True))
        accs = [_colsum(dat), _colsum(dat * (xh * gv)), _colsum(dn * xh)]
        outs = [dxi]
        if has_prev:
            gate = _sel(is_ctx, gates, 0, 1)
            outs.append(dxi * gate)
            accs.append(_colsum(dxi * mp.astype(F32)))
        return outs, accs

    rows = [(x_in, D, 0, False), (da, D, 0, False), (dx_out, D, 0, dx_out_lat_only)]
    vecs = [g, pk]
    outs = [(D, F32)]
    if has_prev:
        rows.append((prev[0], D, 0, False))
        vecs.append(prev[1])
        outs.append((D, BF16))
    return _rowwise(body, rows, vecs, outs, 4 if has_prev else 3, R=R, L=L, tr=tr, name=name, acc_width=D)


def _loss_head(x4, target, m_prev, gf, gate, *, L, tr, name):
    D = x4.shape[-1]

    def body(is_ctx, xt, tg, mp, gfv, gatev):
        xh, r = _rms_stats(xt)
        e = xh * gfv - tg
        dy = e * (1.0 / D)
        w = dy * gfv
        dx = r * (w - xh * jnp.mean(w * xh, axis=-1, keepdims=True))
        accs = [_colsum(e * e) * (0.5 / D), _colsum(dy * xh), _colsum(dx * mp.astype(F32))]
        return (dx, dx * gatev), accs

    return _rowwise(body, [(x4, D, 0, False), (target, D, 0, False), (m_prev, D, 0, False)], [gf, gate],
                    [(D, F32), (D, BF16)], 3, R=L, L=L, tr=tr, name=name, acc_width=D)


RET_CHUNK = 2 * LANES
RET_HEADS_PER_STEP = 4


def _decays(lgh, rev):
    C = RET_CHUNK
    ii = lax.broadcasted_iota(jnp.int32, (C, C), 0)
    jj = lax.broadcasted_iota(jnp.int32, (C, C), 1)
    ri = lax.broadcasted_iota(jnp.int32, (C, 1), 0).astype(F32)
    diff = (jj - ii if rev else ii - jj)
    amat = jnp.where(diff >= 0, jnp.exp(lgh * jnp.maximum(diff, 0).astype(F32)), 0.0)
    pos = (C - ri) if rev else (ri + 1.0)
    bq = jnp.exp(lgh * pos)
    bk = jnp.exp(lgh * (C - pos))
    return amat, bq, bk, pos


def _ret_geometry(T, L, H, rev, backward):
    C = RET_CHUNK
    assert T % C == 0 and L % C == 0, (T, L)
    nT, nL = T // C, L // C
    hb = RET_HEADS_PER_STEP if H % RET_HEADS_PER_STEP == 0 else 1

    def step(s):
        return (nT - 1 - s) if backward else s

    def chunk(s):
        s = step(s)
        return (nT - 1 - s) if rev else (s + nL) % nT

    return C, nT, hb, chunk, step


def _ret_fwd(qk, vg, lg, *, T, L, H, rev, name):
    dk, dv = 2 * LANES, 4 * LANES
    C, nT, hb, chunk, step = _ret_geometry(T, L, H, rev, False)

    def body(lg_ref, q_ref, k_ref, v_ref, o_ref, st_ref, s_scr):
        hg, s = pl.program_id(0), pl.program_id(1)

        @pl.when(s == 0)
        def _():
            s_scr[...] = jnp.zeros_like(s_scr)

        for hh in range(hb):
            lgh = lg_ref[0, hg * hb + hh]
            amat, bq, bk, _ = _decays(lgh, rev)
            q, k = q_ref[:, hh * dk:(hh + 1) * dk], k_ref[:, hh * dk:(hh + 1) * dk]
            v = v_ref[:, hh * dv:(hh + 1) * dv]
            stb = s_scr[hh].astype(BF16)
            st_ref[hh] = stb
            scores = _dot(q, k, _NT) * amat
            o_ref[:, hh * dv:(hh + 1) * dv] = _dot(scores.astype(BF16), v) + _dot(q, stb) * bq
            kd = (k.astype(F32) * bk).astype(BF16)
            s_scr[hh] = s_scr[hh] * jnp.exp(lgh * C) + _dot(kd, v, _TN)

    return pl.pallas_call(
        body, name=name, grid=(H // hb, nT),
        in_specs=[pl.BlockSpec(memory_space=pltpu.SMEM),
                  pl.BlockSpec((C, hb * dk), lambda h, s: (chunk(s), h)),
                  pl.BlockSpec((C, hb * dk), lambda h, s: (chunk(s), H // hb + h)),
                  pl.BlockSpec((C, hb * dv), lambda h, s: (chunk(s), h))],
        out_specs=[pl.BlockSpec((C, hb * dv), lambda h, s: (chunk(s), h)),
                   pl.BlockSpec((hb, None, dk, dv), lambda h, s: (h, s, 0, 0))],
        out_shape=[jax.ShapeDtypeStruct((T, H * dv), F32), jax.ShapeDtypeStruct((H, nT, dk, dv), BF16)],
        scratch_shapes=[pltpu.VMEM((hb, dk, dv), F32)],
        compiler_params=_cparams("parallel", "arbitrary"),
    )(lg, qk, qk, vg)


def _ret_bwd(qk, vg, do, states, lg, *, T, L, H, rev, name):
    dk, dv = 2 * LANES, 4 * LANES
    C, nT, hb, chunk, step = _ret_geometry(T, L, H, rev, True)

    def body(lg_ref, q_ref, k_ref, v_ref, do_ref, st_ref, dq_ref, dk_ref, dv_ref, dlg_ref, ds_scr):
        hg, s = pl.program_id(0), pl.program_id(1)

        @pl.when(s == 0)
        def _():
            ds_scr[...] = jnp.zeros_like(ds_scr)
            dlg_ref[...] = jnp.zeros_like(dlg_ref)

        for hh in range(hb):
            lgh = lg_ref[0, hg * hb + hh]
            amat, bq, bk, pos = _decays(lgh, rev)
            ksl, vsl = slice(hh * dk, (hh + 1) * dk), slice(hh * dv, (hh + 1) * dv)
            q, k, v, dob = q_ref[:, ksl], k_ref[:, ksl], v_ref[:, vsl], do_ref[:, vsl]
            stb = st_ref[hh]
            ds_new = ds_scr[hh]
            dsb = ds_new.astype(BF16)
            qf, kf = q.astype(F32), k.astype(F32)
            scores = (_dot(q, k, _NT) * amat).astype(BF16)
            dqk = (_dot(dob, v, _NT) * amat).astype(BF16)
            dq = _dot(dqk, k) + _dot(dob, stb, _NT) * bq
            dkk = _dot(dqk, q, _TN) + _dot(v, dsb, _NT) * bk
            kd = (kf * bk).astype(BF16)
            dvv = _dot(scores, dob, _TN) + _dot(kd, dsb)
            dod = (dob.astype(F32) * bq).astype(BF16)
            ds_prev = ds_new * jnp.exp(lgh * C) + _dot(q, dod, _TN)
            ds_scr[hh] = ds_prev
            dq_ref[:, ksl] = dq.astype(dq_ref.dtype)
            dk_ref[:, ksl] = dkk.astype(dk_ref.dtype)
            dv_ref[:, vsl] = dvv.astype(dv_ref.dtype)
            dlg = (jnp.sum(pos * jnp.sum(qf * dq - kf * dkk, axis=-1, keepdims=True))
                   + C * jnp.sum(ds_prev * stb.astype(F32)))
            dlg_ref[hh] += dlg

    qspec = pl.BlockSpec((C, hb * dk), lambda h, s: (chunk(s), h))
    vspec = pl.BlockSpec((C, hb * dv), lambda h, s: (chunk(s), h))
    return pl.pallas_call(
        body, name=name, grid=(H // hb, nT),
        in_specs=[pl.BlockSpec(memory_space=pltpu.SMEM), qspec,
                  pl.BlockSpec((C, hb * dk), lambda h, s: (chunk(s), H // hb + h)), vspec, vspec,
                  pl.BlockSpec((hb, None, dk, dv), lambda h, s: (h, step(s), 0, 0))],
        out_specs=[qspec, qspec, vspec, pl.BlockSpec((hb, 8, LANES), lambda h, s: (h, 0, 0))],
        out_shape=[jax.ShapeDtypeStruct((T, H * dk), BF16), jax.ShapeDtypeStruct((T, H * dk), BF16),
                   jax.ShapeDtypeStruct((T, H * dv), BF16), jax.ShapeDtypeStruct((H, 8, LANES), F32)],
        scratch_shapes=[pltpu.VMEM((hb, dk, dv), F32)],
        compiler_params=_cparams("parallel", "arbitrary"),
    )(lg, qk, qk, vg, do, states)


def _readout(o_f, o_b, vg, *, T, L, H, tr, name, dep=None):
    dv = 4 * LANES
    W = H * dv

    def body(is_ctx, of, ob, g):
        o = of + ob
        parts = []
        for h in range(H):
            oh = o[:, h * dv:(h + 1) * dv]
            parts.append(oh * lax.rsqrt(jnp.mean(oh * oh, axis=-1, keepdims=True) + NORM_EPS))
        y = jnp.concatenate(parts, axis=1)
        return (_silu(g.astype(F32)) * y,), ()

    return _rowwise(body, [(o_f, W, 0, False), (o_b, W, 0, False), (vg, W, 1, False)], [], [(W, BF16)], 0,
                    R=T, L=L, tr=tr, name=name, dep=dep)[0]


def _readout_bwd(dz, o_f, o_b, vg, *, T, L, H, tr, name):
    dv = 4 * LANES
    W = H * dv

    def body(is_ctx, dzt, of, ob, g):
        o = of + ob
        gf = g.astype(F32)
        sg = jax.nn.sigmoid(gf)
        dzf = dzt.astype(F32)
        dy = dzf * (gf * sg)
        ys, dos = [], []
        for h in range(H):
            sl = slice(h * dv, (h + 1) * dv)
            oh, dyh = o[:, sl], dy[:, sl]
            r = lax.rsqrt(jnp.mean(oh * oh, axis=-1, keepdims=True) + NORM_EPS)
            yh = oh * r
            ys.append(yh)
            dos.append(r * (dyh - yh * jnp.mean(dyh * yh, axis=-1, keepdims=True)))
        y = jnp.concatenate(ys, axis=1)
        dg = dzf * y * (sg * (1.0 + gf * (1.0 - sg)))
        return (jnp.concatenate(dos, axis=1), dg), ()

    return _rowwise(body, [(dz, W, 0, False), (o_f, W, 0, False), (o_b, W, 0, False), (vg, W, 1, False)], [],
                    [(W, BF16), (W, BF16)], 0, R=T, L=L, tr=tr, name=name)


def _ret_assemble(dq_f, dq_b, dk_f, dk_b, dv_f, dv_b, dg, cos, sin, *, T, L, H, tr, name):
    dk, dv = 2 * LANES, 4 * LANES
    Wq, Wv = H * dk, H * dv
    kscale = float(dk) ** -0.5

    def unrope(d, c, s_, scale):
        parts = []
        for h in range(H):
            d1, d2 = d[:, h * dk:h * dk + LANES], d[:, h * dk + LANES:(h + 1) * dk]
            parts += [(d1 * c + d2 * s_) * scale, (d2 * c - d1 * s_) * scale]
        return jnp.concatenate(parts, axis=1)

    def body(is_ctx, qf, qb, kf, kb, vf, vb, g, c, s_):
        add = lambda a, b: a.astype(F32) + b.astype(F32)
        dq = unrope(add(qf, qb), c, s_, 1.0)
        dkk = unrope(add(kf, kb), c, s_, kscale)
        return (jnp.concatenate([dq.astype(BF16), dkk.astype(BF16), add(vf, vb).astype(BF16), g], axis=1),), ()

    rows = [(dq_f, Wq, 0, False), (dq_b, Wq, 0, False), (dk_f, Wq, 0, False), (dk_b, Wq, 0, False),
            (dv_f, Wv, 0, False), (dv_b, Wv, 0, False), (dg, Wv, 0, False),
            (cos, LANES, 0, False), (sin, LANES, 0, False)]
    return _rowwise(body, rows, [], [(2 * Wq + 2 * Wv, BF16)], 0, R=T, L=L, tr=tr, name=name)[0]


def _swap32(x):
    n = x.shape[-1]
    lane = lax.broadcasted_iota(jnp.int32, x.shape, x.ndim - 1)
    return jnp.where(lane % 64 < 32, pltpu.roll(x, n - 32, x.ndim - 1), pltpu.roll(x, 32, x.ndim - 1))


def _stack_heads(ref, G):
    return jnp.concatenate([ref[:, g * LANES:(g + 1) * LANES] for g in range(G)], axis=0)


def _stack_columns(ref, G):
    return jnp.concatenate([ref[:, g:g + 1] for g in range(G)], axis=0)


def _sink_column(sink_ref, h, G):
    return jnp.concatenate([jnp.full((CHUNK, 1), sink_ref[0, h * G + g], F32) for g in range(G)], axis=0)


def _key_mask(n, nb, CTX, G):
    W = 3 * CHUNK + CTX
    ii = lax.broadcasted_iota(jnp.int32, (G * CHUNK, W), 0) & (CHUNK - 1)
    col = lax.broadcasted_iota(jnp.int32, (G * CHUNK, W), 1)
    is_prev = col < CHUNK
    is_next = (col >= 2 * CHUNK) & (col < 3 * CHUNK)
    prev_ok = is_prev & (col >= ii) & (n > 0)
    next_ok = is_next & ((col - 2 * CHUNK) <= ii) & (n < nb - 1)
    return prev_ok | next_ok | jnp.logical_not(is_prev | is_next)


def _attn_kv_specs(L, CTX, nb):
    blk = lambda f: pl.BlockSpec((CHUNK, LANES), lambda h, n: (f(n), h))
    prev_, cur_, next_ = (lambda n: jnp.maximum(n - 1, 0)), (lambda n: n), (lambda n: jnp.minimum(n + 1, nb - 1))
    ctx_spec = pl.BlockSpec((CTX, LANES), lambda h, n: (L // CTX, h))
    return [blk(prev_), blk(cur_), blk(next_), ctx_spec]


def _attn_fwd(q, k, v, sink, *, L, CTX, Hkv, G, name):
    nb = L // CHUNK
    scale = float(ATT_HEAD_DIM) ** -0.5
    kvs = _attn_kv_specs(L, CTX, nb)

    def body(sink_ref, q_ref, kp, kc, kn, kx, vp, vc, vn, vx, o_ref, lse_ref):
        h, n = pl.program_id(0), pl.program_id(1)
        qs = _stack_heads(q_ref, G)
        kall = jnp.concatenate([kp[...], kc[...], kn[...], kx[...]], axis=0)
        vall = jnp.concatenate([vp[...], vc[...], vn[...], vx[...]], axis=0)
        s_ = jnp.where(_key_mask(n, nb, CTX, G), _dot(qs, kall, _NT) * scale, NEG_INF)
        sk = _sink_column(sink_ref, h, G)
        m = jnp.maximum(jnp.max(s_, axis=-1, keepdims=True), sk)
        p = jnp.exp(s_ - m)
        den = jnp.sum(p, axis=-1, keepdims=True) + jnp.exp(sk - m)
        o = _dot(p.astype(BF16), vall) / den
        lse = m + jnp.log(den)
        for g in range(G):
            o_ref[:, g * LANES:(g + 1) * LANES] = o[g * CHUNK:(g + 1) * CHUNK].astype(o_ref.dtype)
            lse_ref[:, g:g + 1] = lse[g * CHUNK:(g + 1) * CHUNK]

    return pl.pallas_call(
        body, name=name, grid=(Hkv, nb),
        in_specs=[pl.BlockSpec(memory_space=pltpu.SMEM),
                  pl.BlockSpec((CHUNK, G * LANES), lambda h, n: (n, h))] + kvs + kvs,
        out_specs=[pl.BlockSpec((CHUNK, G * LANES), lambda h, n: (n, h)),
                   pl.BlockSpec((None, CHUNK, G), lambda h, n: (h, n, 0))],
        out_shape=[jax.ShapeDtypeStruct((L, Hkv * G * LANES), BF16), jax.ShapeDtypeStruct((Hkv, L, G), F32)],
        compiler_params=_cparams("parallel", "parallel"),
    )(sink, q, k, k, k, k, v, v, v, v)


def _attn_bwd(q, k, v, do, lse, sink, *, L, CTX, Hkv, G, name):
    nb = L // CHUNK
    scale = float(ATT_HEAD_DIM) ** -0.5
    kvs = _attn_kv_specs(L, CTX, nb)
    qspec = pl.BlockSpec((CHUNK, G * LANES), lambda h, n: (n, h))
    rowspec = pl.BlockSpec((None, CHUNK, G), lambda h, n: (h, n, 0))
    colspec = lambda rows: pl.BlockSpec((rows, LANES), lambda h, n: (0, h))

    def body(sink_ref, q_ref, do_ref, lse_ref, kp, kc, kn, kx, vp, vc, vn, vx,
             dq_ref, dk_ref, dv_ref, dkx_ref, dvx_ref, dsk_ref):
        h, n = pl.program_id(0), pl.program_id(1)

        @pl.when(n == 0)
        def _():
            for r in (dk_ref, dv_ref, dkx_ref, dvx_ref, dsk_ref):
                r[...] = jnp.zeros_like(r)

        qs, dos = _stack_heads(q_ref, G), _stack_heads(do_ref, G)
        kall = jnp.concatenate([kp[...], kc[...], kn[...], kx[...]], axis=0)
        vall = jnp.concatenate([vp[...], vc[...], vn[...], vx[...]], axis=0)
        lse_c = _stack_columns(lse_ref, G)
        p = jnp.where(_key_mask(n, nb, CTX, G), jnp.exp(_dot(qs, kall, _NT) * scale - lse_c), 0.0)
        dp = _dot(dos, vall, _NT)
        delta = jnp.sum(p * dp, axis=-1, keepdims=True)
        ds = (p * (dp - delta) * scale).astype(BF16)
        dq = _dot(ds, kall)
        dk_all = _dot(ds, qs, _TN)
        dv_all = _dot(p.astype(BF16), dos, _TN)
        for g in range(G):
            dq_ref[:, g * LANES:(g + 1) * LANES] = dq[g * CHUNK:(g + 1) * CHUNK]
        for part, blk in enumerate((jnp.maximum(n - 1, 0), n, jnp.minimum(n + 1, nb - 1))):
            rows = pl.ds(pl.multiple_of(blk * CHUNK, CHUNK), CHUNK)
            dk_ref[rows, :] += dk_all[part * CHUNK:(part + 1) * CHUNK]
            dv_ref[rows, :] += dv_all[part * CHUNK:(part + 1) * CHUNK]
        dkx_ref[...] += dk_all[3 * CHUNK:]
        dvx_ref[...] += dv_all[3 * CHUNK:]
        dsink = -jnp.exp(_sink_column(sink_ref, h, G) - lse_c) * delta
        for g in range(G):
            dsk_ref[g:g + 1, :] += jnp.sum(dsink[g * CHUNK:(g + 1) * CHUNK])

    return pl.pallas_call(
        body, name=name, grid=(Hkv, nb),
        in_specs=[pl.BlockSpec(memory_space=pltpu.SMEM), qspec, qspec, rowspec] + kvs + kvs,
        out_specs=[qspec, colspec(L), colspec(L), colspec(CTX), colspec(CTX),
                   pl.BlockSpec((None, 8, LANES), lambda h, n: (h, 0, 0))],
        out_shape=[jax.ShapeDtypeStruct((L, Hkv * G * LANES), F32),
                   jax.ShapeDtypeStruct((L, Hkv * LANES), F32), jax.ShapeDtypeStruct((L, Hkv * LANES), F32),
                   jax.ShapeDtypeStruct((CTX, Hkv * LANES), F32), jax.ShapeDtypeStruct((CTX, Hkv * LANES), F32),
                   jax.ShapeDtypeStruct((Hkv, 8, LANES), F32)],
        compiler_params=_cparams("parallel", "arbitrary"),
    )(sink, q, do, lse, k, k, k, k, v, v, v, v)


def _attn_assemble(dq, dk_lat, dv_lat, dk_ctx, dv_ctx, cos, sin, *, T, L, CTX, Hq, Hkv, tr, name):
    Wq, Wk = Hq * LANES, Hkv * LANES
    ctx_blocks = CTX // tr
    nl = L // tr

    def unrope(d, c, s_, heads):
        return d * jnp.tile(c, (1, heads)) + _swap32(d * jnp.tile(s_, (1, heads)))

    def body(is_ctx, dqt, dkl, dvl, dkc, dvc, c, s_):
        dq_ = jnp.where(is_ctx, 0.0, unrope(dqt, c, s_, Hq))
        dk_ = unrope(jnp.where(is_ctx, dkc, dkl), c, s_, Hkv)
        dv_ = jnp.where(is_ctx, dvc, dvl)
        return (jnp.concatenate([dq_, dk_, dv_], axis=1),), ()

    def ctx_map(i):
        return (jnp.clip(i - nl, 0, ctx_blocks - 1), 0)

    assert T % tr == 0 and L % tr == 0 and CTX % tr == 0
    in_specs = [pl.BlockSpec((tr, Wq), lambda i: (jnp.minimum(i, nl - 1), 0)),
                pl.BlockSpec((tr, Wk), lambda i: (jnp.minimum(i, nl - 1), 0)),
                pl.BlockSpec((tr, Wk), lambda i: (jnp.minimum(i, nl - 1), 0)),
                pl.BlockSpec((tr, Wk), ctx_map), pl.BlockSpec((tr, Wk), ctx_map),
                pl.BlockSpec((tr, LANES), lambda i: (i, 0)), pl.BlockSpec((tr, LANES), lambda i: (i, 0))]

    def kern(dq_r, dkl_r, dvl_r, dkc_r, dvc_r, c_r, s_r, o_ref):
        is_ctx = pl.program_id(0) >= nl
        (out,), _ = body(is_ctx, dq_r[...], dkl_r[...], dvl_r[...], dkc_r[...], dvc_r[...], c_r[...], s_r[...])
        o_ref[...] = out.astype(o_ref.dtype)

    return pl.pallas_call(
        kern, name=name, grid=(T // tr,), in_specs=in_specs,
        out_specs=pl.BlockSpec((tr, Wq + 2 * Wk), lambda i: (i, 0)),
        out_shape=jax.ShapeDtypeStruct((T, Wq + 2 * Wk), BF16),
        compiler_params=_cparams("parallel"),
    )(dq, dk_lat, dv_lat, dk_ctx, dv_ctx, cos, sin)


def _my_place():
    x, y, c = lax.axis_index("x"), lax.axis_index("y"), lax.axis_index("c")
    return x, y, c


def _all_gather_small(v, *, name):
    R, C = v.shape

    def body(x_ref, out_ref, send_sems, recv_sems, local_sem):
        x, y, c = _my_place()
        me, sibling = (x, y, c), (x, y, 1 - c)
        chips = [(1 - x, y), (x, 1 - y), (1 - x, 1 - y)]

        def slot(px, py, pc):
            return out_ref.at[4 * px + 2 * py + pc]

        def copy(k, block, to, src=None):
            return pltpu.make_async_remote_copy(
                src_ref=slot(*block) if src is None else src, dst_ref=slot(*block),
                send_sem=send_sems.at[k], recv_sem=recv_sems.at[k], device_id=to, device_id_type=MESH)

        mine = pltpu.make_async_copy(x_ref, slot(*me), local_sem)
        mine.start()
        first = [copy(0, me, sibling, src=x_ref)]
        first += [copy(1 + j, me, (*chip, c), src=x_ref) for j, chip in enumerate(chips)]
        for cp in first:
            cp.start()
        passed = [copy(4 + j, (*chip, c), sibling) for j, chip in enumerate(chips)]
        for j, chip in enumerate(chips):
            copy(1 + j, (*chip, c), me).wait_recv()
            passed[j].start()
        copy(0, sibling, me).wait_recv()
        for j, chip in enumerate(chips):
            copy(4 + j, (*chip, 1 - c), me).wait_recv()
        for cp in first + passed:
            cp.wait_send()
        mine.wait()

    return pl.pallas_call(
        body, name=name, out_shape=jax.ShapeDtypeStruct((N_DEV, R, C), v.dtype),
        in_specs=[pl.BlockSpec(memory_space=pltpu.VMEM)], out_specs=pl.BlockSpec(memory_space=pltpu.VMEM),
        scratch_shapes=[pltpu.SemaphoreType.DMA((7,)), pltpu.SemaphoreType.DMA((7,)), pltpu.SemaphoreType.DMA],
    )(v)


def _shard_slice(ref, axis, idx, size):
    if axis == 1:
        return ref.at[:, pl.ds(idx * size, size), :]
    return ref.at[:, :, pl.ds(idx * size, size)]


def _all_gather_weights(shards, axes, *, name):
    nt = len(shards)
    sizes = [s.shape[a] for s, a in zip(shards, axes)]
    out_shape = []
    for s, a in zip(shards, axes):
        shp = list(s.shape)
        shp[a] *= N_DEV
        out_shape.append(jax.ShapeDtypeStruct(tuple(shp), s.dtype))

    def body(*refs):
        ins, outs = refs[:nt], refs[nt:2 * nt]
        send_sems, recv_sems, local_sems = refs[2 * nt:]
        x, y, c = _my_place()
        me, sibling = (x, y, c), (x, y, 1 - c)
        chips = [(1 - x, y), (x, 1 - y), (1 - x, 1 - y)]
        all_sends = []
        locals_ = []
        for t in range(nt):
            def slot(px, py, pc, t=t):
                return _shard_slice(outs[t], axes[t], 4 * px + 2 * py + pc, sizes[t])

            def copy(k, block, to, src=None, t=t, slot=slot):
                return pltpu.make_async_remote_copy(
                    src_ref=slot(*block) if src is None else src, dst_ref=slot(*block),
                    send_sem=send_sems.at[t, k], recv_sem=recv_sems.at[t, k], device_id=to, device_id_type=MESH)

            mine = pltpu.make_async_copy(ins[t], slot(*me), local_sems.at[t])
            mine.start()
            locals_.append(mine)
            first = [copy(0, me, sibling, src=ins[t])]
            first += [copy(1 + j, me, (*chip, c), src=ins[t]) for j, chip in enumerate(chips)]
            for cp in first:
                cp.start()
            all_sends += first
        for t in range(nt):
            def slot(px, py, pc, t=t):
                return _shard_slice(outs[t], axes[t], 4 * px + 2 * py + pc, sizes[t])

            def copy(k, block, to, t=t, slot=slot):
                return pltpu.make_async_remote_copy(
                    src_ref=slot(*block), dst_ref=slot(*block),
                    send_sem=send_sems.at[t, k], recv_sem=recv_sems.at[t, k], device_id=to, device_id_type=MESH)

            passed = [copy(4 + j, (*chip, c), sibling) for j, chip in enumerate(chips)]
            for j, chip in enumerate(chips):
                copy(1 + j, (*chip, c), me).wait_recv()
                passed[j].start()
            all_sends += passed
        for t in range(nt):
            def slot(px, py, pc, t=t):
                return _shard_slice(outs[t], axes[t], 4 * px + 2 * py + pc, sizes[t])

            def copy(k, block, to, t=t, slot=slot):
                return pltpu.make_async_remote_copy(
                    src_ref=slot(*block), dst_ref=slot(*block),
                    send_sem=send_sems.at[t, k], recv_sem=recv_sems.at[t, k], device_id=to, device_id_type=MESH)

            copy(0, sibling, me).wait_recv()
            for j, chip in enumerate(chips):
                copy(4 + j, (*chip, 1 - c), me).wait_recv()
        for cp in all_sends:
            cp.wait_send()
        for mine in locals_:
            mine.wait()

    return pl.pallas_call(
        body, name=name, out_shape=out_shape,
        in_specs=[pl.BlockSpec(memory_space=pl.ANY)] * nt, out_specs=[pl.BlockSpec(memory_space=pl.ANY)] * nt,
        scratch_shapes=[pltpu.SemaphoreType.DMA((nt, 7)), pltpu.SemaphoreType.DMA((nt, 7)),
                        pltpu.SemaphoreType.DMA((nt,))],
    )(*shards)


def _exchange_grads(grads, axes, *, name):
    nt = len(grads)
    sizes = [g.shape[a] // N_DEV for g, a in zip(grads, axes)]
    out_shape = []
    for g, a, sz in zip(grads, axes, sizes):
        shp = list(g.shape)
        shp[a] = sz
        out_shape.append(jax.ShapeDtypeStruct((N_DEV, *shp), g.dtype))

    def body(*refs):
        ins, outs = refs[:nt], refs[nt:2 * nt]
        send_sems, recv_sems, local_sems = refs[2 * nt:]
        x, y, c = _my_place()
        my_idx = 4 * x + 2 * y + c

        def peer(r):
            px = (1 - x) if r & 4 else x
            py = (1 - y) if r & 2 else y
            pc = (1 - c) if r & 1 else c
            return (px, py, pc)

        copies, locals_ = [], []
        for t in range(nt):
            mine = pltpu.make_async_copy(_shard_slice(ins[t], axes[t], my_idx, sizes[t]), outs[t].at[my_idx],
                                         local_sems.at[t])
            mine.start()
            locals_.append(mine)
            for r in range(1, N_DEV):
                p = peer(r)
                p_idx = 4 * p[0] + 2 * p[1] + p[2]
                cp = pltpu.make_async_remote_copy(
                    src_ref=_shard_slice(ins[t], axes[t], p_idx, sizes[t]), dst_ref=outs[t].at[my_idx],
                    send_sem=send_sems.at[t, r - 1], recv_sem=recv_sems.at[t, r - 1], device_id=p, device_id_type=MESH)
                cp.start()
                copies.append((cp, t, r, p_idx))
        for cp, t, r, p_idx in copies:
            pltpu.make_async_remote_copy(
                src_ref=_shard_slice(ins[t], axes[t], p_idx, sizes[t]), dst_ref=outs[t].at[p_idx],
                send_sem=send_sems.at[t, r - 1], recv_sem=recv_sems.at[t, r - 1], device_id=peer(r),
                device_id_type=MESH).wait_recv()
        for cp, _, _, _ in copies:
            cp.wait_send()
        for mine in locals_:
            mine.wait()

    return pl.pallas_call(
        body, name=name, out_shape=out_shape,
        in_specs=[pl.BlockSpec(memory_space=pl.ANY)] * nt, out_specs=[pl.BlockSpec(memory_space=pl.ANY)] * nt,
        scratch_shapes=[pltpu.SemaphoreType.DMA((nt, 7)), pltpu.SemaphoreType.DMA((nt, 7)),
                        pltpu.SemaphoreType.DMA((nt,))],
    )(*grads)


_HBM_SPEC = pl.BlockSpec(memory_space=pltpu.HBM)
_SEM_SPEC = pl.BlockSpec(memory_space=pltpu.SEMAPHORE)
_ANY_SPEC = pl.BlockSpec(memory_space=pl.ANY)
_DATAFLOW = pltpu.SideEffectType.DATAFLOW_SIDE_EFFECTING
N_PEERS = N_DEV - 1


def _peer(r):
    x, y, c = _my_place()
    return ((1 - x) if r & 4 else x, (1 - y) if r & 2 else y, (1 - c) if r & 1 else c)


def _index_of(place):
    return 4 * place[0] + 2 * place[1] + place[2]


def _slot(ref, axis, idx, size):
    if axis == 0:
        return ref.at[pl.ds(idx * size, size), :]
    return ref.at[:, pl.ds(idx * size, size)]


def _cast_place(w3, layer, axis, me_arr, dep, *, name):
    Ks, Ns = w3.shape[1], w3.shape[2]
    tr = _pick(Ks, 256, 16)
    nblk = Ks // tr
    full = (Ks * N_DEV, Ns) if axis == 0 else (Ks, Ns * N_DEV)
    if axis == 0:
        out_map = lambda i, me: (me[0] * nblk + i, 0)
    else:
        out_map = lambda i, me: (i, me[0])

    def body(me_ref, w_ref, dep_ref, o_ref):
        o_ref[...] = w_ref[...].astype(BF16)

    return pl.pallas_call(
        body, name=name, out_shape=jax.ShapeDtypeStruct(full, BF16),
        grid_spec=pltpu.PrefetchScalarGridSpec(
            num_scalar_prefetch=1, grid=(nblk,),
            in_specs=[pl.BlockSpec((None, tr, Ns), lambda i, me: (layer, i, 0)), pl.BlockSpec(memory_space=pl.ANY)],
            out_specs=pl.BlockSpec((tr, Ns), out_map)),
        compiler_params=_cparams("parallel"),
    )(me_arr, w3, dep)


AG_FIRST = 4
AG_CHIPS = 3


def _sibling():
    x, y, c = _my_place()
    return (x, y, 1 - c)


def _chip_peer(j, same_core=True):
    x, y, c = _my_place()
    px = (1 - x) if j in (0, 2) else x
    py = (1 - y) if j in (1, 2) else y
    return (px, py, c if same_core else 1 - c)


def _gather_start(lands, axes, after, *, name):
    nt = len(lands)
    sizes = [l.shape[a] // N_DEV for l, a in zip(lands, axes)]

    def body(*refs):
        ins, send_sems, recv_sems, token = refs[:nt], refs[nt + 1], refs[nt + 2], refs[-1]
        my_idx = _index_of(_my_place())
        for t in range(nt):
            mine = _slot(ins[t], axes[t], my_idx, sizes[t])
            for k, to in enumerate([_sibling()] + [_chip_peer(j) for j in range(AG_CHIPS)]):
                pltpu.make_async_remote_copy(src_ref=mine, dst_ref=mine, send_sem=send_sems.at[t * AG_FIRST + k],
                                             recv_sem=recv_sems.at[t * AG_FIRST + k], device_id=to,
                                             device_id_type=MESH).start()
        token[...] = jnp.zeros_like(token)

    res = pl.pallas_call(
        body, name=name,
        out_shape=(pltpu.SemaphoreType.DMA((nt * AG_FIRST,)), pltpu.SemaphoreType.DMA((nt * AG_FIRST,)),
                   *[pltpu.HBM(l.shape, l.dtype) for l in lands], jax.ShapeDtypeStruct((8, LANES), F32)),
        in_specs=[_HBM_SPEC] * nt + [_ANY_SPEC],
        out_specs=(_SEM_SPEC, _SEM_SPEC, *[_HBM_SPEC] * nt, pl.BlockSpec(memory_space=pltpu.VMEM)),
        input_output_aliases={t: 2 + t for t in range(nt)},
        compiler_params=pltpu.CompilerParams(has_side_effects=_DATAFLOW),
    )(*[pltpu.with_memory_space_constraint(l, pltpu.HBM) for l in lands], after)
    return res[0], res[1], list(res[2:2 + nt]), res[-1]


def _gather_forward(send_a, recv_a, lands, axes, after, *, name):
    nt = len(lands)
    sizes = [l.shape[a] // N_DEV for l, a in zip(lands, axes)]

    def body(*refs):
        ins, send_a, recv_a = refs[:nt], refs[nt], refs[nt + 1]
        send_f, recv_f, token = refs[nt + 3], refs[nt + 4], refs[-1]
        my_idx = _index_of(_my_place())
        for t in range(nt):
            for j in range(AG_CHIPS):
                src_dev = _chip_peer(j)
                arrived = _slot(ins[t], axes[t], _index_of(src_dev), sizes[t])
                pltpu.make_async_remote_copy(
                    src_ref=_slot(ins[t], axes[t], my_idx, sizes[t]), dst_ref=arrived,
                    send_sem=send_a.at[t * AG_FIRST + 1 + j], recv_sem=recv_a.at[t * AG_FIRST + 1 + j],
                    device_id=src_dev, device_id_type=MESH).wait_recv()
                pltpu.make_async_remote_copy(src_ref=arrived, dst_ref=arrived, send_sem=send_f.at[t * AG_CHIPS + j],
                                             recv_sem=recv_f.at[t * AG_CHIPS + j], device_id=_sibling(),
                                             device_id_type=MESH).start()
        token[...] = jnp.zeros_like(token)

    res = pl.pallas_call(
        body, name=name,
        out_shape=(pltpu.SemaphoreType.DMA((nt * AG_CHIPS,)), pltpu.SemaphoreType.DMA((nt * AG_CHIPS,)),
                   *[pltpu.HBM(l.shape, l.dtype) for l in lands], jax.ShapeDtypeStruct((8, LANES), F32)),
        in_specs=[_HBM_SPEC] * nt + [_SEM_SPEC, _SEM_SPEC, _ANY_SPEC],
        out_specs=(_SEM_SPEC, _SEM_SPEC, *[_HBM_SPEC] * nt, pl.BlockSpec(memory_space=pltpu.VMEM)),
        input_output_aliases={t: 2 + t for t in range(nt)},
        compiler_params=pltpu.CompilerParams(has_side_effects=_DATAFLOW),
    )(*lands, send_a, recv_a, after)
    return res[0], res[1], list(res[2:2 + nt]), res[-1]


def _gather_wait(send_a, recv_a, send_f, recv_f, lands, axes, after, *, name):
    nt = len(lands)
    sizes = [l.shape[a] // N_DEV for l, a in zip(lands, axes)]

    def body(*refs):
        ins, send_a, recv_a, send_f, recv_f = refs[:nt], refs[nt], refs[nt + 1], refs[nt + 2], refs[nt + 3]
        my_idx = _index_of(_my_place())
        sib = _sibling()
        for t in range(nt):
            mine = _slot(ins[t], axes[t], my_idx, sizes[t])
            for k, to in enumerate([sib] + [_chip_peer(j) for j in range(AG_CHIPS)]):
                pltpu.make_async_remote_copy(src_ref=mine, dst_ref=mine, send_sem=send_a.at[t * AG_FIRST + k],
                                             recv_sem=recv_a.at[t * AG_FIRST + k], device_id=to,
                                             device_id_type=MESH).wait_send()
            pltpu.make_async_remote_copy(src_ref=mine, dst_ref=_slot(ins[t], axes[t], _index_of(sib), sizes[t]),
                                         send_sem=send_a.at[t * AG_FIRST], recv_sem=recv_a.at[t * AG_FIRST],
                                         device_id=sib, device_id_type=MESH).wait_recv()
            for j in range(AG_CHIPS):
                sent = _slot(ins[t], axes[t], _index_of(_chip_peer(j)), sizes[t])
                got = _slot(ins[t], axes[t], _index_of(_chip_peer(j, same_core=False)), sizes[t])
                cp = pltpu.make_async_remote_copy(src_ref=sent, dst_ref=got, send_sem=send_f.at[t * AG_CHIPS + j],
                                                  recv_sem=recv_f.at[t * AG_CHIPS + j], device_id=sib,
                                                  device_id_type=MESH)
                cp.wait_send()
                cp.wait_recv()

    res = pl.pallas_call(
        body, name=name, out_shape=[pltpu.HBM(l.shape, l.dtype) for l in lands],
        in_specs=[_HBM_SPEC] * nt + [_SEM_SPEC] * 4 + [_ANY_SPEC], out_specs=[_HBM_SPEC] * nt,
        input_output_aliases={t: t for t in range(nt)},
        compiler_params=pltpu.CompilerParams(has_side_effects=_DATAFLOW),
    )(*lands, send_a, recv_a, send_f, recv_f, after)
    return list(res)


def _scatter_start(dw, axis, *, name):
    size = dw.shape[axis] // N_DEV
    land_shape = (N_PEERS, size, dw.shape[1]) if axis == 0 else (N_PEERS, dw.shape[0], size)

    def body(dw_ref, land_ref, send_sems, recv_sems, dw_thru, land_thru, token):
        for r in range(1, N_DEV):
            p = _peer(r)
            pltpu.make_async_remote_copy(src_ref=_slot(dw_ref, axis, _index_of(p), size), dst_ref=land_ref.at[r - 1],
                                         send_sem=send_sems.at[r - 1], recv_sem=recv_sems.at[r - 1], device_id=p,
                                         device_id_type=MESH).start()
        token[...] = jnp.zeros_like(token)

    land = pltpu.with_memory_space_constraint(lax.empty(land_shape, dw.dtype), pltpu.HBM)
    return pl.pallas_call(
        body, name=name,
        out_shape=(pltpu.SemaphoreType.DMA((N_PEERS,)), pltpu.SemaphoreType.DMA((N_PEERS,)),
                   pltpu.HBM(dw.shape, dw.dtype), pltpu.HBM(land_shape, dw.dtype), jax.ShapeDtypeStruct((8, LANES), F32)),
        in_specs=[_HBM_SPEC, _HBM_SPEC],
        out_specs=(_SEM_SPEC, _SEM_SPEC, _HBM_SPEC, _HBM_SPEC, pl.BlockSpec(memory_space=pltpu.VMEM)),
        input_output_aliases={0: 2, 1: 3},
        compiler_params=pltpu.CompilerParams(has_side_effects=_DATAFLOW),
    )(pltpu.with_memory_space_constraint(dw, pltpu.HBM), land)


def _scatter_wait(send_sems, recv_sems, dw, land, axis, after, *, name):
    size = dw.shape[axis] // N_DEV

    def body(dw_ref, land_ref, send_sems, recv_sems, after_ref, dw_thru, land_thru):
        for r in range(1, N_DEV):
            p = _peer(r)
            cp = pltpu.make_async_remote_copy(src_ref=_slot(dw_ref, axis, _index_of(p), size), dst_ref=land_ref.at[r - 1],
                                              send_sem=send_sems.at[r - 1], recv_sem=recv_sems.at[r - 1], device_id=p,
                                              device_id_type=MESH)
            cp.wait_send()
            cp.wait_recv()

    return pl.pallas_call(
        body, name=name, out_shape=(pltpu.HBM(dw.shape, dw.dtype), pltpu.HBM(land.shape, land.dtype)),
        in_specs=[_HBM_SPEC, _HBM_SPEC, _SEM_SPEC, _SEM_SPEC, _ANY_SPEC], out_specs=(_HBM_SPEC, _HBM_SPEC),
        input_output_aliases={0: 0, 1: 1},
        compiler_params=pltpu.CompilerParams(has_side_effects=_DATAFLOW),
    )(dw, land, send_sems, recv_sems, after)


def _adamw_math(w, g, m, v):
    m = ADAM_B1 * m + (1.0 - ADAM_B1) * g
    v = ADAM_B2 * v + (1.0 - ADAM_B2) * (g * g)
    m_hat = m / (1.0 - ADAM_B1 ** ADAM_STEP)
    v_hat = v / (1.0 - ADAM_B2 ** ADAM_STEP)
    delta = -ADAM_LR * (m_hat / (jnp.sqrt(v_hat) + ADAM_EPS) + ADAM_WD * w)
    return delta, m, v


def _adamw_sharded(w, m, v, layer, dw, land, axis, me_arr, prev, *, name):
    nl, Ks, Ns = w.shape
    tr = _pick(Ks, 128, 16)
    nblk = Ks // tr
    if axis == 0:
        own_map = lambda i, me: (me[0] * nblk + i, 0)
    else:
        own_map = lambda i, me: (i, me[0])
    wspec = pl.BlockSpec((None, tr, Ns), lambda i, me: (layer, i, 0))
    n_prev = 0 if prev is None else 4

    def body(me_ref, w_ref, m_ref, v_ref, own_ref, r_ref, *rest):
        g_ref, d_ref, nm_ref, nv_ref = rest[n_prev:]
        g = own_ref[...].astype(F32)
        for r in range(N_PEERS):
            g = g + r_ref[r].astype(F32)
        delta, nm, nv = _adamw_math(w_ref[...], g, m_ref[...], v_ref[...])
        g_ref[...], d_ref[...], nm_ref[...], nv_ref[...] = g, delta, nm, nv

    return pl.pallas_call(
        body, name=name, out_shape=[jax.ShapeDtypeStruct((nl, Ks, Ns), F32)] * 4,
        grid_spec=pltpu.PrefetchScalarGridSpec(
            num_scalar_prefetch=1, grid=(nblk,),
            in_specs=[wspec, wspec, wspec, pl.BlockSpec((tr, Ns), own_map),
                      pl.BlockSpec((N_PEERS, tr, Ns), lambda i, me: (0, i, 0))] + [_ANY_SPEC] * n_prev,
            out_specs=[wspec] * 4),
        input_output_aliases={6 + k: k for k in range(n_prev)},
        compiler_params=_cparams("parallel"),
    )(me_arr, w, m, v, dw, land, *(prev or []))


def _adamw_flat(w, g, m, v, *, name):
    def body(w_ref, g_ref, m_ref, v_ref, d_ref, nm_ref, nv_ref):
        d_ref[...], nm_ref[...], nv_ref[...] = _adamw_math(w_ref[...], g_ref[...], m_ref[...], v_ref[...])

    spec = pl.BlockSpec(memory_space=pltpu.VMEM)
    return pl.pallas_call(body, name=name, in_specs=[spec] * 4, out_specs=[spec] * 3,
                          out_shape=[jax.ShapeDtypeStruct(w.shape, F32)] * 3)(w, g, m, v)


def _sum_devices(a, *, name):
    def body(a_ref, o_ref):
        s = a_ref[0]
        for d in range(1, N_DEV):
            s = s + a_ref[d]
        o_ref[...] = s

    spec = pl.BlockSpec(memory_space=pltpu.VMEM)
    return pl.pallas_call(body, name=name, in_specs=[spec], out_specs=spec,
                          out_shape=jax.ShapeDtypeStruct(a.shape[1:], F32))(a)


def _ada_mods(c16, ada_w, ada_b_cols, *, name):
    nl, D, cols = ada_w.shape
    bn = _pick(cols, 512)

    def body(c_ref, w_ref, b_ref, o_ref):
        cond = _silu(c_ref[...]).astype(BF16)
        o_ref[...] = _dot(cond, w_ref[...].astype(BF16)) + b_ref[...]

    return pl.pallas_call(
        body, name=name, grid=(nl, cols // bn),
        in_specs=[pl.BlockSpec((16, D), lambda l, j: (0, 0)), pl.BlockSpec((None, D, bn), lambda l, j: (l, 0, j)),
                  pl.BlockSpec((None, 1, bn), lambda l, j: (l, 0, j))],
        out_specs=pl.BlockSpec((None, 16, bn), lambda l, j: (l, 0, j)),
        out_shape=jax.ShapeDtypeStruct((nl, 16, cols), F32),
        compiler_params=_cparams("parallel", "parallel"),
    )(c16, ada_w, ada_b_cols)


def _ada_bwd(cond_t, dmod, w, m, v, *, name):
    nl, D, cols = w.shape
    tr = _pick(D, 256, 8)

    def body(ct_ref, dm_ref, w_ref, m_ref, v_ref, g_ref, d_ref, nm_ref, nv_ref, dc_ref):
        ct, dm, wt = ct_ref[...], dm_ref[...], w_ref[...]
        g = ct[:, 0:1] * dm[0:1, :]
        for r in range(1, N_DEV + 1):
            g = g + ct[:, r:r + 1] * dm[r:r + 1, :]
        delta, nm, nv = _adamw_math(wt, g, m_ref[...], v_ref[...])
        g_ref[...], d_ref[...], nm_ref[...], nv_ref[...] = g, delta, nm, nv
        dc_ref[...] = jnp.sum(wt * dm[N_DEV:N_DEV + 1, :], axis=-1, keepdims=True)

    wspec = pl.BlockSpec((None, tr, cols), lambda l, i: (l, i, 0))
    return pl.pallas_call(
        body, name=name, grid=(nl, D // tr),
        in_specs=[pl.BlockSpec((tr, 16), lambda l, i: (i, 0)), pl.BlockSpec((None, 16, cols), lambda l, i: (l, 0, 0)),
                  wspec, wspec, wspec],
        out_specs=[wspec] * 4 + [pl.BlockSpec((None, tr, 1), lambda l, i: (l, i, 0))],
        out_shape=[jax.ShapeDtypeStruct((nl, D, cols), F32)] * 4 + [jax.ShapeDtypeStruct((nl, D, 1), F32)],
        compiler_params=_cparams("parallel", "parallel"),
    )(cond_t, dmod, w, m, v)


def _rope_tables(L, CTX):
    def angles(pos, dim):
        inv_freq = ROPE_BASE ** (-jnp.arange(0, dim, 2, dtype=F32) / dim)
        return pos.astype(F32)[:, None] * inv_freq[None, :]

    def pad(cos, sin):
        return (jnp.concatenate([cos, jnp.ones((CTX, LANES), F32)], 0),
                jnp.concatenate([sin, jnp.zeros((CTX, LANES), F32)], 0))

    ret = angles(jnp.arange(L), 2 * LANES)
    ret_cs = pad(jnp.cos(ret), jnp.sin(ret))
    rows = angles(jnp.arange(L) // GRID_W, ATT_HEAD_DIM // 2)
    cols = angles(jnp.arange(L) % GRID_W, ATT_HEAD_DIM // 2)
    cos = jnp.concatenate([jnp.cos(rows)] * 2 + [jnp.cos(cols)] * 2, axis=1)
    sin = jnp.concatenate([-jnp.sin(rows), jnp.sin(rows), -jnp.sin(cols), jnp.sin(cols)], axis=1)
    return ret_cs, pad(cos, sin)


def kernel(x, c, ctx, c_ctx, ada_w, ada_b, norm_mix_g, norm_mlp_g, mlp_w1, mlp_w2, ret_w_in, ret_w_out, ret_decay_fwd, ret_decay_bwd, attn_w_in, attn_w_out, attn_sink, final_norm_g, loss_target, m_c_ctx, m_ada_w, m_ada_b, m_norm_mix_g, m_norm_mlp_g, m_mlp_w1, m_mlp_w2, m_ret_w_in, m_ret_w_out, m_ret_decay_fwd, m_ret_decay_bwd, m_attn_w_in, m_attn_w_out, m_attn_sink, m_final_norm_g, v_c_ctx, v_ada_w, v_ada_b, v_norm_mix_g, v_norm_mlp_g, v_mlp_w1, v_mlp_w2, v_ret_w_in, v_ret_w_out, v_ret_decay_fwd, v_ret_decay_bwd, v_attn_w_in, v_attn_w_out, v_attn_sink, v_final_norm_g):
    L, D = x.shape[1], x.shape[2]
    CTX = ctx.shape[1]
    T = L + CTX
    RH = ret_decay_fwd.shape[-1]
    assert D == RH * 2 * LANES and ada_w.shape[0] == 2 and ret_w_in.shape[0] == 1 and attn_w_in.shape[0] == 1
    Hq = attn_sink.shape[-1]
    Hkv = (attn_w_in.shape[-1] * N_DEV // ATT_HEAD_DIM - Hq) // 2
    G = Hq // Hkv
    FF = mlp_w1.shape[-1] * N_DEV
    Wq_r, Wv_r = RH * 2 * LANES, RH * 4 * LANES
    acols = ada_w.shape[-1]
    tr = _pick(CTX, 256, 8)
    tr_wide = _pick(CTX, 128, 8)
    bmT = T // 4 if (T % 64 == 0) else T
    bmL = L // 4 if (L % 64 == 0) else L
    x_idx, y_idx, c_idx = lax.axis_index("x"), lax.axis_index("y"), lax.axis_index("c")
    me = 4 * x_idx + 2 * y_idx + c_idx
    me_arr = jnp.reshape(me, (1,)).astype(jnp.int32)

    (rcos, rsin), (acos, asin) = _rope_tables(L, CTX)
    lg_f = jax.nn.log_sigmoid(ret_decay_fwd.astype(F32))
    lg_b = jax.nn.log_sigmoid(ret_decay_bwd.astype(F32))

    c_pad = jnp.concatenate([c.astype(F32), jnp.zeros((7, D), F32)], 0)
    c_all = _all_gather_small(c_pad, name="ag_c")[:, 0, :]
    c16 = jnp.concatenate([c_all, c_ctx[None, :], jnp.zeros((7, D), F32)], 0)
    ada_b_cols = lax.dynamic_slice_in_dim(ada_b, me * acols, acols, axis=1)[:, None, :]
    mods_shard = _ada_mods(c16, ada_w, ada_b_cols, name="ada_mods")
    mods_all = _all_gather_small(mods_shard.reshape(32, acols), name="ag_mods")

    wdefs = {"ret_in": (ret_w_in, 0, 1), "ret_out": (ret_w_out, 0, 0), "w1_0": (mlp_w1, 0, 1), "w2_0": (mlp_w2, 0, 0),
             "attn_in": (attn_w_in, 0, 1), "attn_out": (attn_w_out, 0, 0), "w1_1": (mlp_w1, 1, 1), "w2_1": (mlp_w2, 1, 0)}
    groups = [["ret_in"], ["ret_out", "w1_0", "w2_0"], ["attn_in", "attn_out"], ["w1_1", "w2_1"]]

    placed = {}

    def ag_start(gi, after):
        g_axes = [wdefs[k][2] for k in groups[gi]]
        ssem, rsem, lands, tok_ = _gather_start([placed[k] for k in groups[gi]], g_axes, after, name=f"ag_start{gi}")
        return dict(a=(ssem, rsem), lands=lands, axes=g_axes, gi=gi), tok_

    def ag_forward(g, after):
        fs, fr, g["lands"], tok_ = _gather_forward(*g["a"], g["lands"], g["axes"], after, name=f"ag_forward{g['gi']}")
        g["f"] = (fs, fr)
        return tok_

    def ag_wait(g, after):
        return _gather_wait(*g["a"], *g["f"], g["lands"], g["axes"], after, name=f"ag_wait{g['gi']}")

    placed["ret_in"] = _cast_place(*wdefs["ret_in"], me_arr, mods_all, name="place_ret_in")
    g0, tok = ag_start(0, mods_all)
    last_cast = tok
    for keys in groups[1:]:
        for k in keys:
            last_cast = placed[k] = _cast_place(*wdefs[k], me_arr, last_cast, name=f"place_{k}")
    mods_all = (mods_all + tok[0, 0]).reshape(N_DEV, 2, 16, acols).transpose(1, 2, 0, 3).reshape(2, 16, 6, D)
    mod_lat = lax.dynamic_index_in_dim(mods_all, me, axis=1, keepdims=False)
    mod_ctx = mods_all[:, N_DEV]

    def pack(i, ks, kc):
        return jnp.stack([mod_lat[i, ks], mod_lat[i, kc], mod_ctx[i, ks], mod_ctx[i, kc]], 0)

    def gates(i, k):
        return jnp.stack([mod_lat[i, k], mod_ctx[i, k]], 0)

    def gate_epilogue(gl, gc, x_rows_lat_only):
        def epi(acc, i, j, xt, gv):
            if x_rows_lat_only:
                gate = gv[0:1, :]
            else:
                row = i * acc.shape[0] + lax.broadcasted_iota(jnp.int32, (acc.shape[0], 1), 0)
                gate = jnp.where(row >= L, gv[1:2, :], gv[0:1, :])
            return xt + gate * acc, acc
        return epi

    w1, w2 = {}, {}

    mmT = dict(M=T, bm=bmT)
    mmL = dict(M=L, bm=bmL)

    def bn_of(n, off=0):
        b = MM_BN
        while n % b or off % b:
            b -= LANES
        return b

    X0 = jnp.concatenate([x[0], ctx[0]], axis=0)
    g_mix0, g_mlp0 = norm_mix_g[0:1], norm_mlp_g[0:1]
    g_mix1, g_mlp1 = norm_mix_g[1:2], norm_mlp_g[1:2]
    a0 = _normmod(X0, g_mix0, pack(0, 0, 1), R=T, L=L, tr=tr, name="normmod_mix0", dep=last_cast)
    tok = ag_forward(g0, a0)
    (wr_in,) = ag_wait(g0, tok)
    g1, tok = ag_start(1, wr_in)

    bn_qk = _pick(Wq_r, MM_BN, 2 * LANES)
    nq_blocks = Wq_r // bn_qk
    kscale = float(2 * LANES) ** -0.5

    def rope_epi(acc, i, j, cos, sin):
        parts = []
        for h in range(acc.shape[1] // (2 * LANES)):
            x1 = acc[:, h * 2 * LANES:h * 2 * LANES + LANES]
            x2 = acc[:, h * 2 * LANES + LANES:(h + 1) * 2 * LANES]
            parts += [x1 * cos - x2 * sin, x2 * cos + x1 * sin]
        return (jnp.concatenate(parts, axis=1) * jnp.where(j < nq_blocks, 1.0, kscale),)

    def row_tile(arr, bm):
        return (arr, (bm, LANES), lambda i, j: (i, 0))

    (qk0,) = _mm(a0, wr_in, "nn", [BF16], N=2 * Wq_r, K=D, bn=bn_qk, bk=D, name="ret_qk", epilogue=rope_epi,
                 extras=[row_tile(rcos, bmT), row_tile(rsin, bmT)], dep=tok, **mmT)
    bn_vg = bn_of(2 * Wv_r, 2 * Wq_r)
    (vg0,) = _mm(a0, wr_in, "nn", [BF16], N=2 * Wv_r, K=D, bn=bn_vg, bk=D, name="ret_vg", b_col0=2 * Wq_r, dep=tok,
                 **mmT)

    of, st_f = _ret_fwd(qk0, vg0, lg_f, T=T, L=L, H=RH, rev=False, name="ret_scan_f")
    ob, st_b = _ret_fwd(qk0, vg0, lg_b, T=T, L=L, H=RH, rev=True, name="ret_scan_b")
    tok = ag_forward(g1, of[:8, :LANES] + ob[:8, :LANES])
    z0 = _readout(of, ob, vg0, T=T, L=L, H=RH, tr=tr_wide, name="ret_readout", dep=tok)
    wr_out, w1[0], w2[0] = ag_wait(g1, z0)
    g2, tok = ag_start(2, wr_out)
    g3, tok = ag_start(3, tok)

    bnD = _pick(D, MM_BN)

    def xtile(arr, bm):
        return (arr, (bm, bnD), lambda i, j: (i, j))

    def gtile(gv):
        return (gv, (2, bnD), lambda i, j: (0, j))

    bk_v = _pick(Wv_r, 2048)
    X1, ro0 = _mm(z0, wr_out, "nn", [F32, BF16], N=D, K=Wv_r, bn=bnD, bk=bk_v, name="ret_out",
                  epilogue=gate_epilogue(None, None, False), extras=[xtile(X0, bmT), gtile(gates(0, 2))], dep=tok, **mmT)

    def mlp_fwd(Xin, i, g_mlp, rows, name, between=None):
        a = _normmod(Xin, g_mlp, pack(i, 3, 4), R=rows["M"], L=L, tr=tr, name=f"normmod_mlp{name}")

        def relu2(acc, i_, j_):
            u = jnp.maximum(acc, 0.0)
            return u, u * u

        bnF = _pick(FF, MM_BN)
        u, r = _mm(a, w1[i], "nn", [BF16, BF16], N=FF, K=D, bn=bnF, bk=D, name=f"mlp_up{name}", epilogue=relu2, **rows)
        dep = None if between is None else between(u)
        bkF = _pick(FF, 2048)
        Xout, mo = _mm(r, w2[i], "nn", [F32, BF16], N=D, K=FF, bn=bnD, bk=bkF, name=f"mlp_down{name}",
                       epilogue=gate_epilogue(None, None, rows["M"] == L),
                       extras=[xtile(Xin, rows["bm"]), gtile(gates(i, 5))], dep=dep, **rows)
        return a, u, r, Xout, mo

    a1, u0, r0, X2, mo0 = mlp_fwd(X1, 0, g_mlp0, mmT, "0")

    tok = ag_forward(g2, X2)
    a2 = _normmod(X2, g_mix1, pack(1, 0, 1), R=T, L=L, tr=tr, name="normmod_mix1", dep=tok)
    wa_in, wa_out = ag_wait(g2, a2)
    Wq_a, Wk_a = Hq * LANES, Hkv * LANES

    def arope_epi(acc, i, j, cos, sin):
        heads = acc.shape[1] // LANES
        return (acc * jnp.tile(cos, (1, heads)) + _swap32(acc) * jnp.tile(sin, (1, heads)),)

    bn_q = _pick(Wq_a, MM_BN)
    (q1,) = _mm(a2, wa_in, "nn", [BF16], N=Wq_a, K=D, bn=bn_q, bk=D, name="attn_q", epilogue=arope_epi,
                extras=[row_tile(acos, bmL), row_tile(asin, bmL)], **mmL)
    bn_k = bn_of(Wk_a, Wq_a)
    (k1,) = _mm(a2, wa_in, "nn", [BF16], N=Wk_a, K=D, bn=bn_k, bk=D, name="attn_k", b_col0=Wq_a, epilogue=arope_epi,
                extras=[row_tile(acos, bmT), row_tile(asin, bmT)], **mmT)
    bn_v = bn_of(Wk_a, Wq_a + Wk_a)
    (v1,) = _mm(a2, wa_in, "nn", [BF16], N=Wk_a, K=D, bn=bn_v, bk=D, name="attn_v", b_col0=Wq_a + Wk_a, **mmT)
    tok = ag_forward(g3, q1)
    o1, lse = _attn_fwd(q1, k1, v1, attn_sink + tok[0, 0], L=L, CTX=CTX, Hkv=Hkv, G=G, name="attn_fwd")
    w1[1], w2[1] = ag_wait(g3, o1)
    X3, ao = _mm(o1, wa_out, "nn", [F32, BF16], N=D, K=Wq_a, bn=bnD, bk=_pick(Wq_a, 2048), name="attn_out",
                 epilogue=gate_epilogue(None, None, True), extras=[xtile(X2, bmL), gtile(gates(1, 2))], **mmL)
    a3, u1, r1, X4, mo1 = mlp_fwd(X3, 1, g_mlp1, mmL, "1")

    dX4, dmo1, acc_head = _loss_head(X4, loss_target[0], mo1, final_norm_g[None, :], gates(1, 5)[0:1], L=L, tr=tr,
                                     name="loss_head")
    loss_part = jnp.sum(acc_head[0, 0])
    d_gf = acc_head[0, 1]
    zeros_d = jnp.zeros((D,), F32)
    dmod_lat = [[zeros_d] * 6, [zeros_d] * 6]
    dmod_ctx = [[zeros_d] * 6, [zeros_d] * 6]
    dmod_lat[1][5] = acc_head[0, 2]

    def dw_mm(a, b, M, N, K, name):
        return _mm(a, b, "tn", [BF16], M=M, N=N, K=K, bm=_pick(M, 512), bn=_pick(N, MM_BN), bk=K, name=name)[0]

    def mlp_bwd(dmo, a, u, r, i, rows, name):
        Mr = rows["M"]

        def times_2u(acc, i_, j_, ut):
            return (acc * (2.0 * ut.astype(F32)),)

        bnF = _pick(FF, MM_BN)
        dw2 = dw_mm(r, dmo, FF, D, Mr, f"mlp_down_dw{name}")
        tok_ = send_grad(f"w2_{i}", dw2, 0)
        (dh,) = _mm(dmo, w2[i], "nt", [BF16], N=FF, K=D, bn=bnF, bk=D, name=f"mlp_down_dx{name}", epilogue=times_2u,
                    extras=[(u, (rows["bm"], bnF), lambda i_, j_: (i_, j_))], dep=tok_, **rows)
        dw1 = dw_mm(a, dh, D, FF, Mr, f"mlp_up_dw{name}")
        tok_ = send_grad(f"w1_{i}", dw1, 1)
        (da,) = _mm(dh, w1[i], "nt", [F32], N=D, K=FF, bn=bnD, bk=_pick(FF, 2048), name=f"mlp_up_dx{name}", dep=tok_,
                    **rows)
        return da

    pending = []

    def send_grad(key, dw, axis):
        ssem, rsem, dw_thru, land, tok_ = _scatter_start(dw, axis, name=f"rs_start_{key}")
        pending.append((key, axis, ssem, rsem, dw_thru, land))
        return tok_

    da3 = mlp_bwd(dmo1, a3, u1, r1, 1, mmL, "1")
    dX3, dao, acc = _normmod_bwd(X3, da3, dX4, False, g_mlp1, pack(1, 3, 4), (ao, gates(1, 2)), R=L, L=L, tr=tr,
                                 name="normmod_mlp1_bwd")
    dmod_lat[1][3], dmod_lat[1][4], d_gmlp1, dmod_lat[1][2] = acc[0, 0], acc[0, 1], acc[0, 2], acc[0, 3]

    dwa_out = dw_mm(o1, dao, Wq_a, D, L, "attn_out_dw")
    tok = send_grad("attn_out", dwa_out, 0)
    (do1,) = _mm(dao, wa_out, "nt", [BF16], N=Wq_a, K=D, bn=_pick(Wq_a, MM_BN), bk=D, name="attn_out_dx", dep=tok, **mmL)
    dq1, dk1, dv1, dkx, dvx, dsink_acc = _attn_bwd(q1, k1, v1, do1, lse, attn_sink, L=L, CTX=CTX, Hkv=Hkv, G=G,
                                                   name="attn_bwd")
    dp1 = _attn_assemble(dq1, dk1, dv1, dkx, dvx, acos, asin, T=T, L=L, CTX=CTX, Hq=Hq, Hkv=Hkv, tr=tr_wide,
                         name="attn_assemble")
    Wa_in = Wq_a + 2 * Wk_a
    dwa_in = dw_mm(a2, dp1, D, Wa_in, T, "attn_in_dw")
    tok = send_grad("attn_in", dwa_in, 1)
    (da2,) = _mm(dp1, wa_in, "nt", [F32], N=D, K=Wa_in, bn=bnD, bk=_pick(Wa_in, 2048), name="attn_in_dx", dep=tok,
                 **mmT)
    dX2, dmo0, acc = _normmod_bwd(X2, da2, dX3, True, g_mix1, pack(1, 0, 1), (mo0, gates(0, 5)), R=T, L=L, tr=tr,
                                  name="normmod_mix1_bwd")
    dmod_lat[1][0], dmod_lat[1][1], d_gmix1, dmod_lat[0][5] = acc[0, 0], acc[0, 1], acc[0, 2] + acc[1, 2], acc[0, 3]
    dmod_ctx[1][0], dmod_ctx[1][1], dmod_ctx[0][5] = acc[1, 0], acc[1, 1], acc[1, 3]

    da1 = mlp_bwd(dmo0, a1, u0, r0, 0, mmT, "0")
    dX1, dro0, acc = _normmod_bwd(X1, da1, dX2, False, g_mlp0, pack(0, 3, 4), (ro0, gates(0, 2)), R=T, L=L, tr=tr,
                                  name="normmod_mlp0_bwd")
    dmod_lat[0][3], dmod_lat[0][4], d_gmlp0, dmod_lat[0][2] = acc[0, 0], acc[0, 1], acc[0, 2] + acc[1, 2], acc[0, 3]
    dmod_ctx[0][3], dmod_ctx[0][4], dmod_ctx[0][2] = acc[1, 0], acc[1, 1], acc[1, 3]

    dwr_out = dw_mm(z0, dro0, Wv_r, D, T, "ret_out_dw")
    tok = send_grad("ret_out", dwr_out, 0)
    (dz0,) = _mm(dro0, wr_out, "nt", [BF16], N=Wv_r, K=D, bn=_pick(Wv_r, MM_BN), bk=D, name="ret_out_dx", dep=tok, **mmT)
    do0, dg0 = _readout_bwd(dz0, of, ob, vg0, T=T, L=L, H=RH, tr=tr_wide, name="ret_readout_bwd")
    dq_f, dk_f, dv_f, dlg_f = _ret_bwd(qk0, vg0, do0, st_f, lg_f, T=T, L=L, H=RH, rev=False, name="ret_scan_f_bwd")
    dq_b, dk_b, dv_b, dlg_b = _ret_bwd(qk0, vg0, do0, st_b, lg_b, T=T, L=L, H=RH, rev=True, name="ret_scan_b_bwd")
    dp0 = _ret_assemble(dq_f, dq_b, dk_f, dk_b, dv_f, dv_b, dg0, rcos, rsin, T=T, L=L, H=RH, tr=tr_wide,
                        name="ret_assemble")
    Wr_in = 2 * Wq_r + 2 * Wv_r
    dwr_in = dw_mm(a0, dp0, D, Wr_in, T, "ret_in_dw")
    tok = send_grad("ret_in", dwr_in, 1)
    (da0,) = _mm(dp0, wr_in, "nt", [F32], N=D, K=Wr_in, bn=bnD, bk=_pick(Wr_in, 2048), name="ret_in_dx", dep=tok, **mmT)
    dX0, acc = _normmod_bwd(X0, da0, dX1, False, g_mix0, pack(0, 0, 1), None, R=T, L=L, tr=tr, name="normmod_mix0_bwd")
    dmod_lat[0][0], dmod_lat[0][1], d_gmix0 = acc[0, 0], acc[0, 1], acc[0, 2] + acc[1, 2]
    dmod_ctx[0][0], dmod_ctx[0][1] = acc[1, 0], acc[1, 1]
    grad_x = dX0[:L][None]

    wmv = {"ret_in": (ret_w_in, m_ret_w_in, v_ret_w_in, 0, "ret_w_in"),
           "ret_out": (ret_w_out, m_ret_w_out, v_ret_w_out, 0, "ret_w_out"),
           "attn_in": (attn_w_in, m_attn_w_in, v_attn_w_in, 0, "attn_w_in"),
           "attn_out": (attn_w_out, m_attn_w_out, v_attn_w_out, 0, "attn_w_out"),
           "w1_0": (mlp_w1, m_mlp_w1, v_mlp_w1, 0, "mlp_w1"), "w1_1": (mlp_w1, m_mlp_w1, v_mlp_w1, 1, "mlp_w1"),
           "w2_0": (mlp_w2, m_mlp_w2, v_mlp_w2, 0, "mlp_w2"), "w2_1": (mlp_w2, m_mlp_w2, v_mlp_w2, 1, "mlp_w2")}
    big = {}

    def finish_grad(entry, after):
        key, axis, ssem, rsem, dw_thru, land = entry
        dw_done, land_done = _scatter_wait(ssem, rsem, dw_thru, land, axis, after, name=f"rs_wait_{key}")
        w_, m_, v_, layer, out_name = wmv[key]
        big[out_name] = _adamw_sharded(w_, m_, v_, layer, dw_done, land_done, axis, me_arr, big.get(out_name),
                                       name=f"adamw_{key}")
        return big[out_name][0]

    after = dX0
    for entry in pending[:-1]:
        after = finish_grad(entry, after)

    misc = jnp.zeros((D,), F32)
    misc = misc.at[0:RH].set(dlg_f[:, 0, 0]).at[RH:2 * RH].set(dlg_b[:, 0, 0])
    misc = misc.at[2 * RH:2 * RH + Hq].set(dsink_acc[:, :G, 0].reshape(Hq)).at[2 * RH + Hq].set(loss_part)
    rows = ([dmod_lat[i][k] for i in range(2) for k in range(6)] + [dmod_ctx[i][k] for i in range(2) for k in range(6)]
            + [d_gmix0, d_gmix1, d_gmlp0, d_gmlp1, d_gf, misc, zeros_d, zeros_d])
    part = jnp.stack(rows, 0)
    part_all = _all_gather_small(part, name="ag_small_grads")
    tot = _sum_devices(part_all, name="sum_small_grads")

    grad_ada_b = (tot[0:12] + tot[12:24]).reshape(2, 6 * D)
    grad_norm_mix_g, grad_norm_mlp_g, grad_final_norm_g = tot[24:26], tot[26:28], tot[28]
    grad_ret_decay_fwd = (tot[29, 0:RH] * jax.nn.sigmoid(-ret_decay_fwd[0]))[None]
    grad_ret_decay_bwd = (tot[29, RH:2 * RH] * jax.nn.sigmoid(-ret_decay_bwd[0]))[None]
    grad_attn_sink = tot[29, 2 * RH:2 * RH + Hq][None]
    loss = tot[29, 2 * RH + Hq]

    dlat_cols = lax.dynamic_slice_in_dim(part_all[:, 0:12].reshape(N_DEV, 2, 6 * D), me * acols, acols, axis=2)
    dctx_cols = lax.dynamic_slice_in_dim(tot[12:24].reshape(2, 6 * D), me * acols, acols, axis=1)
    dmod16 = jnp.concatenate([dlat_cols.transpose(1, 0, 2), dctx_cols[:, None, :], jnp.zeros((2, 7, acols), F32)], 1)
    cond_t = _silu(c16).T
    g_ada, d_ada, nm_ada, nv_ada, dcond_part = _ada_bwd(cond_t, dmod16, ada_w, m_ada_w, v_ada_w, name="ada_bwd")
    dcond = (dcond_part[0, :, 0] + dcond_part[1, :, 0]).reshape(D // LANES, LANES)
    pad_rows = -(D // LANES) % 8
    dcond_pad = jnp.concatenate([dcond, jnp.zeros((pad_rows, LANES), F32)], 0) if pad_rows else dcond
    dcond_all = _all_gather_small(dcond_pad, name="ag_dcond")
    dcond_tot = _sum_devices(dcond_all, name="sum_dcond")[:D // LANES].reshape(D)
    sg = jax.nn.sigmoid(c_ctx)
    grad_c_ctx = dcond_tot * (sg * (1.0 + c_ctx * (1.0 - sg)))

    small_w = [c_ctx, ada_b, norm_mix_g, norm_mlp_g, ret_decay_fwd, ret_decay_bwd, attn_sink, final_norm_g]
    small_g = [grad_c_ctx, grad_ada_b, grad_norm_mix_g, grad_norm_mlp_g, grad_ret_decay_fwd, grad_ret_decay_bwd,
               grad_attn_sink, grad_final_norm_g]
    small_m = [m_c_ctx, m_ada_b, m_norm_mix_g, m_norm_mlp_g, m_ret_decay_fwd, m_ret_decay_bwd, m_attn_sink,
               m_final_norm_g]
    small_v = [v_c_ctx, v_ada_b, v_norm_mix_g, v_norm_mlp_g, v_ret_decay_fwd, v_ret_decay_bwd, v_attn_sink,
               v_final_norm_g]
    sizes = [w_.size for w_ in small_w]
    total = sum(-(-s // LANES) * LANES for s in sizes)
    total_pad = -(-total // (8 * LANES)) * 8 * LANES

    def flat_pack(ts, fill):
        pieces = []
        for t_ in ts:
            f = t_.reshape(-1).astype(F32)
            pad = -f.size % LANES
            pieces.append(jnp.concatenate([f, jnp.full((pad,), fill, F32)]) if pad else f)
        pieces.append(jnp.full((total_pad - total,), fill, F32))
        return jnp.concatenate(pieces).reshape(total_pad // LANES, LANES)

    d_s, nm_s, nv_s = _adamw_flat(flat_pack(small_w, 0.0), flat_pack(small_g, 0.0), flat_pack(small_m, 0.0),
                                  flat_pack(small_v, 1.0), name="adamw_small")

    def unpack(p):
        flat = p.reshape(-1)
        res, off = [], 0
        for w_, s in zip(small_w, sizes):
            res.append(flat[off:off + s].reshape(w_.shape))
            off += -(-s // LANES) * LANES
        return res

    finish_grad(pending[-1], d_s)
    d_small, nm_small, nv_small = unpack(d_s), unpack(nm_s), unpack(nv_s)
    small_names = ["c_ctx", "ada_b", "norm_mix_g", "norm_mlp_g", "ret_decay_fwd", "ret_decay_bwd", "attn_sink",
                   "final_norm_g"]
    sm = {n: (g_, d_, m_, v_) for n, g_, d_, m_, v_ in zip(small_names, small_g, d_small, nm_small, nv_small)}

    def out4(n):
        if n == "ada_w":
            return g_ada, d_ada, nm_ada, nv_ada
        if n in big:
            return tuple(big[n])
        return sm[n]

    order = ["c_ctx", "ada_w", "ada_b", "norm_mix_g", "norm_mlp_g", "mlp_w1", "mlp_w2", "ret_w_in", "ret_w_out",
             "ret_decay_fwd", "ret_decay_bwd", "attn_w_in", "attn_w_out", "attn_sink", "final_norm_g"]
    quads = [out4(n) for n in order]
    return (loss, grad_x, *[q_[0] for q_ in quads], *[q_[1] for q_ in quads], *[q_[2] for q_ in quads],
            *[q_[3] for q_ in quads])
```

```python
import functools

import jax
import jax.numpy as jnp
from jax import lax
from jax.experimental import pallas as pl
from jax.experimental.pallas import tpu as pltpu

F32 = jnp.float32
BF16 = jnp.bfloat16

N_DEV = 8
NORM_EPS = 1e-6
CHUNK = 128
ATT_HEAD_DIM = 128
GRID_W = 64
ROPE_BASE = 10000.0
NEG_INF = -1e30
ADAM_LR, ADAM_B1, ADAM_B2, ADAM_EPS, ADAM_WD, ADAM_STEP = 0.001, 0.9, 0.999, 1e-08, 0.01, 10

V7X_VMEM_LIMIT_BYTES = 56 * 1024 * 1024
MM_BN = 1024
LANES = 128
MESH = pl.DeviceIdType.MESH

_NN = (((1,), (0,)), ((), ()))
_NT = (((1,), (1,)), ((), ()))
_TN = (((0,), (0,)), ((), ()))


def _dot(a, b, dn=_NN):
    return lax.dot_general(a, b, dn, preferred_element_type=F32)


def _cparams(*sem):
    return pltpu.CompilerParams(dimension_semantics=sem, vmem_limit_bytes=V7X_VMEM_LIMIT_BYTES)


def _pick(n, pref, mult=LANES):
    if n <= pref:
        return n
    best = None
    for d in range(mult, pref + 1, mult):
        if n % d == 0:
            best = d
    assert best is not None, (n, pref)
    return best


def _silu(x):
    return x * jax.nn.sigmoid(x)


def _mm(a, b, mode, out_dtypes, *, M, N, K, bm, bn, bk, name, b_col0=0, epilogue=None, extras=(), dep=None):
    assert M % bm == 0 and N % bn == 0 and K % bk == 0 and b_col0 % bn == 0, (name, M, N, K, bm, bn, bk, b_col0)
    nk = K // bk
    c0 = b_col0 // bn
    if mode == "nn":
        a_spec = pl.BlockSpec((bm, bk), lambda i, j, k: (i, k))
        b_spec = pl.BlockSpec((bk, bn), lambda i, j, k: (k, j + c0))
    elif mode == "nt":
        a_spec = pl.BlockSpec((bm, bk), lambda i, j, k: (i, k))
        b_spec = pl.BlockSpec((bn, bk), lambda i, j, k: (j + c0, k))
    else:
        a_spec = pl.BlockSpec((bk, bm), lambda i, j, k: (k, i))
        b_spec = pl.BlockSpec((bk, bn), lambda i, j, k: (k, j + c0))
    dn = {"nn": _NN, "nt": _NT, "tn": _TN}[mode]
    e_specs = [pl.BlockSpec(bs, (lambda i, j, k, f=f: f(i, j))) for (_, bs, f) in extras]
    ne, no = len(extras), len(out_dtypes)
    nd = 0 if dep is None else 1

    def body(a_ref, b_ref, *rest):
        e_refs, o_refs = rest[:ne], rest[ne + nd:ne + nd + no]
        i, j, k = pl.program_id(0), pl.program_id(1), pl.program_id(2)

        def finish(acc):
            outs = (acc,) if epilogue is None else epilogue(acc, i, j, *[e[...] for e in e_refs])
            for o_ref, o in zip(o_refs, outs):
                o_ref[...] = o.astype(o_ref.dtype)

        p = _dot(a_ref[...], b_ref[...], dn)
        if nk == 1:
            finish(p)
        else:
            acc_ref = rest[-1]

            @pl.when(k == 0)
            def _():
                acc_ref[...] = p

            @pl.when(k > 0)
            def _():
                acc_ref[...] += p

            @pl.when(k == nk - 1)
            def _():
                finish(acc_ref[...])

    outs = pl.pallas_call(
        body, name=name, grid=(M // bm, N // bn, nk),
        in_specs=[a_spec, b_spec] + e_specs + [pl.BlockSpec(memory_space=pl.ANY)] * nd,
        out_specs=[pl.BlockSpec((bm, bn), lambda i, j, k: (i, j)) for _ in out_dtypes],
        out_shape=[jax.ShapeDtypeStruct((M, N), dt) for dt in out_dtypes],
        scratch_shapes=[pltpu.VMEM((bm, bn), F32)] if nk > 1 else [],
        compiler_params=_cparams("parallel", "parallel", "arbitrary"),
    )(a, b, *[e[0] for e in extras], *([dep] if nd else []))
    return outs


def _rowwise(body, rows, vecs, outs, n_acc, *, R, L, tr, name, acc_width=None, dep=None):
    assert R % tr == 0 and L % tr == 0, (name, R, L, tr)
    nl = L // tr
    n_regions = 2 if R > L else 1
    n_rows, n_vecs, n_outs = len(rows), len(vecs), len(outs)
    n_dep = 0 if dep is None else 1
    acc_pad = -(-n_acc // 8) * 8 if n_acc else 0

    in_specs = []
    for (_, w, cb, lat_only) in rows:
        if lat_only:
            in_specs.append(pl.BlockSpec((tr, w), lambda i, cb=cb: (jnp.minimum(i, nl - 1), cb)))
        else:
            in_specs.append(pl.BlockSpec((tr, w), lambda i, cb=cb: (i, cb)))
    for v in vecs:
        in_specs.append(pl.BlockSpec(v.shape, lambda i, nd=v.ndim: (0,) * nd))
    in_specs += [pl.BlockSpec(memory_space=pl.ANY)] * n_dep
    out_specs = [pl.BlockSpec((tr, w), lambda i: (i, 0)) for (w, _) in outs]
    out_shape = [jax.ShapeDtypeStruct((R, w), dt) for (w, dt) in outs]
    if n_acc:
        out_specs.append(pl.BlockSpec((None, acc_pad, acc_width), lambda i: (jnp.where(i >= nl, 1, 0), 0, 0)))
        out_shape.append(jax.ShapeDtypeStruct((n_regions, acc_pad, acc_width), F32))

    def kern(*refs):
        i = pl.program_id(0)
        is_ctx = i >= nl
        ins = [r[...] for r in refs[:n_rows + n_vecs]]
        o_refs = refs[n_rows + n_vecs + n_dep:]
        out_tiles, acc_rows = body(is_ctx, *ins)
        for o_ref, o in zip(o_refs[:n_outs], out_tiles):
            o_ref[...] = o.astype(o_ref.dtype)
        if n_acc:
            acc_ref = o_refs[n_outs]

            @pl.when((i == 0) | (i == nl))
            def _():
                acc_ref[...] = jnp.zeros_like(acc_ref)

            for r, row in enumerate(acc_rows):
                acc_ref[r:r + 1, :] += row

    res = pl.pallas_call(
        kern, name=name, grid=(R // tr,), in_specs=in_specs, out_specs=out_specs, out_shape=out_shape,
        compiler_params=_cparams("arbitrary"),
    )(*[r[0] for r in rows], *vecs, *([dep] if n_dep else []))
    return res


def _colsum(x):
    return jnp.sum(x, axis=0, keepdims=True)


def _rms_stats(x):
    r = lax.rsqrt(jnp.mean(x * x, axis=-1, keepdims=True) + NORM_EPS)
    return x * r, r


def _sel(is_ctx, pk, lat_row, ctx_row):
    return jnp.where(is_ctx, pk[ctx_row:ctx_row + 1, :], pk[lat_row:lat_row + 1, :])


def _normmod(x, g, pk, *, R, L, tr, name, dep=None):
    D = x.shape[-1]

    def body(is_ctx, xt, gv, pkv):
        xh, _ = _rms_stats(xt)
        sh, sc = _sel(is_ctx, pkv, 0, 2), _sel(is_ctx, pkv, 1, 3)
        return ((xh * gv) * (1.0 + sc) + sh,), ()

    return _rowwise(body, [(x, D, 0, False)], [g, pk], [(D, BF16)], 0, R=R, L=L, tr=tr, name=name, dep=dep)[0]


def _normmod_bwd(x_in, da, dx_out, dx_out_lat_only, g, pk, prev, *, R, L, tr, name):
    D = x_in.shape[-1]
    has_prev = prev is not None

    def body(is_ctx, *t):
        if has_prev:
            xt, dat, dxo, mp, gv, pkv, gates = t
        else:
            xt, dat, dxo, gv, pkv = t
        xh, r = _rms_stats(xt)
        sc = _sel(is_ctx, pkv, 1, 3)
        if dx_out_lat_only:
            dxo = jnp.where(is_ctx, 0.0, dxo)
        dn = dat * (1.0 + sc)
        w = dn * gv
        dxi = dxo + r * (w - xh * jnp.mean(w * xh, axis=-1, keepdims=True))
        accs = [_colsum(dat), _colsum(dat * (xh * gv)), _colsum(dn * xh)]
        outs = [dxi]
        if has_prev:
            gate = _sel(is_ctx, gates, 0, 1)
            outs.append(dxi * gate)
            accs.append(_colsum(dxi * mp.astype(F32)))
        return outs, accs

    rows = [(x_in, D, 0, False), (da, D, 0, False), (dx_out, D, 0, dx_out_lat_only)]
    vecs = [g, pk]
    outs = [(D, F32)]
    if has_prev:
        rows.append((prev[0], D, 0, False))
        vecs.append(prev[1])
        outs.append((D, BF16))
    return _rowwise(body, rows, vecs, outs, 4 if has_prev else 3, R=R, L=L, tr=tr, name=name, acc_width=D)


def _loss_head(x4, target, m_prev, gf, gate, *, L, tr, name):
    D = x4.shape[-1]

    def body(is_ctx, xt, tg, mp, gfv, gatev):
        xh, r = _rms_stats(xt)
        e = xh * gfv - tg
        dy = e * (1.0 / D)
        w = dy * gfv
        dx = r * (w - xh * jnp.mean(w * xh, axis=-1, keepdims=True))
        accs = [_colsum(e * e) * (0.5 / D), _colsum(dy * xh), _colsum(dx * mp.astype(F32))]
        return (dx, dx * gatev), accs

    return _rowwise(body, [(x4, D, 0, False), (target, D, 0, False), (m_prev, D, 0, False)], [gf, gate],
                    [(D, F32), (D, BF16)], 3, R=L, L=L, tr=tr, name=name, acc_width=D)


RET_CHUNK = 2 * LANES
RET_HEADS_PER_STEP = 4


def _decays(lgh, rev):
    C = RET_CHUNK
    ii = lax.broadcasted_iota(jnp.int32, (C, C), 0)
    jj = lax.broadcasted_iota(jnp.int32, (C, C), 1)
    ri = lax.broadcasted_iota(jnp.int32, (C, 1), 0).astype(F32)
    diff = (jj - ii if rev else ii - jj)
    amat = jnp.where(diff >= 0, jnp.exp(lgh * jnp.maximum(diff, 0).astype(F32)), 0.0)
    pos = (C - ri) if rev else (ri + 1.0)
    bq = jnp.exp(lgh * pos)
    bk = jnp.exp(lgh * (C - pos))
    return amat, bq, bk, pos


def _ret_geometry(T, L, H, rev, backward):
    C = RET_CHUNK
    assert T % C == 0 and L % C == 0, (T, L)
    nT, nL = T // C, L // C
    hb = RET_HEADS_PER_STEP if H % RET_HEADS_PER_STEP == 0 else 1

    def step(s):
        return (nT - 1 - s) if backward else s

    def chunk(s):
        s = step(s)
        return (nT - 1 - s) if rev else (s + nL) % nT

    return C, nT, hb, chunk, step


def _ret_fwd(qk, vg, lg, other, *, T, L, H, rev, name):
    dk, dv = 2 * LANES, 4 * LANES
    C, nT, hb, chunk, step = _ret_geometry(T, L, H, rev, False)
    n_other = 0 if other is None else 1

    def body(lg_ref, q_ref, k_ref, v_ref, *rest):
        o_ref, st_ref, s_scr = rest[n_other:]
        hg, s = pl.program_id(0), pl.program_id(1)

        @pl.when(s == 0)
        def _():
            s_scr[...] = jnp.zeros_like(s_scr)

        for hh in range(hb):
            lgh = lg_ref[0, hg * hb + hh]
            amat, bq, bk, _ = _decays(lgh, rev)
            q, k = q_ref[:, hh * dk:(hh + 1) * dk], k_ref[:, hh * dk:(hh + 1) * dk]
            v = v_ref[:, hh * dv:(hh + 1) * dv]
            stb = s_scr[hh].astype(BF16)
            st_ref[hh] = stb
            scores = _dot(q, k, _NT) * amat
            o = _dot(scores.astype(BF16), v) + _dot(q, stb) * bq
            if n_other:
                o = rest[0][:, hh * dv:(hh + 1) * dv] + o
            o_ref[:, hh * dv:(hh + 1) * dv] = o
            kd = (k.astype(F32) * bk).astype(BF16)
            s_scr[hh] = s_scr[hh] * jnp.exp(lgh * C) + _dot(kd, v, _TN)

    vspec = pl.BlockSpec((C, hb * dv), lambda h, s: (chunk(s), h))
    return pl.pallas_call(
        body, name=name, grid=(H // hb, nT),
        in_specs=[pl.BlockSpec(memory_space=pltpu.SMEM),
                  pl.BlockSpec((C, hb * dk), lambda h, s: (chunk(s), h)),
                  pl.BlockSpec((C, hb * dk), lambda h, s: (chunk(s), H // hb + h)), vspec] + [vspec] * n_other,
        out_specs=[vspec, pl.BlockSpec((hb, None, dk, dv), lambda h, s: (h, s, 0, 0))],
        out_shape=[jax.ShapeDtypeStruct((T, H * dv), F32), jax.ShapeDtypeStruct((H, nT, dk, dv), BF16)],
        scratch_shapes=[pltpu.VMEM((hb, dk, dv), F32)],
        compiler_params=_cparams("parallel", "arbitrary"),
    )(lg, qk, qk, vg, *([other] if n_other else []))


def _ret_bwd(qk, vg, do, states, lg, *, T, L, H, rev, name):
    dk, dv = 2 * LANES, 4 * LANES
    C, nT, hb, chunk, step = _ret_geometry(T, L, H, rev, True)

    def body(lg_ref, q_ref, k_ref, v_ref, do_ref, st_ref, dq_ref, dk_ref, dv_ref, dlg_ref, ds_scr):
        hg, s = pl.program_id(0), pl.program_id(1)

        @pl.when(s == 0)
        def _():
            ds_scr[...] = jnp.zeros_like(ds_scr)
            dlg_ref[...] = jnp.zeros_like(dlg_ref)

        for hh in range(hb):
            lgh = lg_ref[0, hg * hb + hh]
            amat, bq, bk, pos = _decays(lgh, rev)
            ksl, vsl = slice(hh * dk, (hh + 1) * dk), slice(hh * dv, (hh + 1) * dv)
            q, k, v, dob = q_ref[:, ksl], k_ref[:, ksl], v_ref[:, vsl], do_ref[:, vsl]
            stb = st_ref[hh]
            ds_new = ds_scr[hh]
            dsb = ds_new.astype(BF16)
            qf, kf = q.astype(F32), k.astype(F32)
            scores = (_dot(q, k, _NT) * amat).astype(BF16)
            dqk = (_dot(dob, v, _NT) * amat).astype(BF16)
            dq = _dot(dqk, k) + _dot(dob, stb, _NT) * bq
            dkk = _dot(dqk, q, _TN) + _dot(v, dsb, _NT) * bk
            kd = (kf * bk).astype(BF16)
            dvv = _dot(scores, dob, _TN) + _dot(kd, dsb)
            dod = (dob.astype(F32) * bq).astype(BF16)
            ds_prev = ds_new * jnp.exp(lgh * C) + _dot(q, dod, _TN)
            ds_scr[hh] = ds_prev
            dq_ref[:, ksl] = dq.astype(dq_ref.dtype)
            dk_ref[:, ksl] = dkk.astype(dk_ref.dtype)
            dv_ref[:, vsl] = dvv.astype(dv_ref.dtype)
            dlg = (jnp.sum(pos * jnp.sum(qf * dq - kf * dkk, axis=-1, keepdims=True))
                   + C * jnp.sum(ds_prev * stb.astype(F32)))
            dlg_ref[hh] += dlg

    qspec = pl.BlockSpec((C, hb * dk), lambda h, s: (chunk(s), h))
    vspec = pl.BlockSpec((C, hb * dv), lambda h, s: (chunk(s), h))
    return pl.pallas_call(
        body, name=name, grid=(H // hb, nT),
        in_specs=[pl.BlockSpec(memory_space=pltpu.SMEM), qspec,
                  pl.BlockSpec((C, hb * dk), lambda h, s: (chunk(s), H // hb + h)), vspec, vspec,
                  pl.BlockSpec((hb, None, dk, dv), lambda h, s: (h, step(s), 0, 0))],
        out_specs=[qspec, qspec, vspec, pl.BlockSpec((hb, 8, LANES), lambda h, s: (h, 0, 0))],
        out_shape=[jax.ShapeDtypeStruct((T, H * dk), BF16), jax.ShapeDtypeStruct((T, H * dk), BF16),
                   jax.ShapeDtypeStruct((T, H * dv), BF16), jax.ShapeDtypeStruct((H, 8, LANES), F32)],
        scratch_shapes=[pltpu.VMEM((hb, dk, dv), F32)],
        compiler_params=_cparams("parallel", "arbitrary"),
    )(lg, qk, qk, vg, do, states)


def _readout(o, vg, *, T, L, H, tr, name, dep=None):
    dv = 4 * LANES
    W = H * dv

    def body(is_ctx, o, g):
        parts = []
        for h in range(H):
            oh = o[:, h * dv:(h + 1) * dv]
            parts.append(oh * lax.rsqrt(jnp.mean(oh * oh, axis=-1, keepdims=True) + NORM_EPS))
        y = jnp.concatenate(parts, axis=1)
        return (_silu(g.astype(F32)) * y,), ()

    return _rowwise(body, [(o, W, 0, False), (vg, W, 1, False)], [], [(W, BF16)], 0,
                    R=T, L=L, tr=tr, name=name, dep=dep)[0]


def _readout_bwd(dz, o, vg, *, T, L, H, tr, name):
    dv = 4 * LANES
    W = H * dv

    def body(is_ctx, dzt, o, g):
        gf = g.astype(F32)
        sg = jax.nn.sigmoid(gf)
        dzf = dzt.astype(F32)
        dy = dzf * (gf * sg)
        ys, dos = [], []
        for h in range(H):
            sl = slice(h * dv, (h + 1) * dv)
            oh, dyh = o[:, sl], dy[:, sl]
            r = lax.rsqrt(jnp.mean(oh * oh, axis=-1, keepdims=True) + NORM_EPS)
            yh = oh * r
            ys.append(yh)
            dos.append(r * (dyh - yh * jnp.mean(dyh * yh, axis=-1, keepdims=True)))
        y = jnp.concatenate(ys, axis=1)
        dg = dzf * y * (sg * (1.0 + gf * (1.0 - sg)))
        return (jnp.concatenate(dos, axis=1), dg), ()

    return _rowwise(body, [(dz, W, 0, False), (o, W, 0, False), (vg, W, 1, False)], [],
                    [(W, BF16), (W, BF16)], 0, R=T, L=L, tr=tr, name=name)


def _ret_assemble(dq_f, dq_b, dk_f, dk_b, dv_f, dv_b, dg, cos, sin, *, T, L, H, tr, name):
    dk, dv = 2 * LANES, 4 * LANES
    Wq, Wv = H * dk, H * dv
    kscale = float(dk) ** -0.5

    def unrope(d, c, s_, scale):
        parts = []
        for h in range(H):
            d1, d2 = d[:, h * dk:h * dk + LANES], d[:, h * dk + LANES:(h + 1) * dk]
            parts += [(d1 * c + d2 * s_) * scale, (d2 * c - d1 * s_) * scale]
        return jnp.concatenate(parts, axis=1)

    def body(is_ctx, qf, qb, kf, kb, vf, vb, g, c, s_):
        add = lambda a, b: a.astype(F32) + b.astype(F32)
        dq = unrope(add(qf, qb), c, s_, 1.0)
        dkk = unrope(add(kf, kb), c, s_, kscale)
        return (jnp.concatenate([dq.astype(BF16), dkk.astype(BF16), add(vf, vb).astype(BF16), g], axis=1),), ()

    rows = [(dq_f, Wq, 0, False), (dq_b, Wq, 0, False), (dk_f, Wq, 0, False), (dk_b, Wq, 0, False),
            (dv_f, Wv, 0, False), (dv_b, Wv, 0, False), (dg, Wv, 0, False),
            (cos, LANES, 0, False), (sin, LANES, 0, False)]
    return _rowwise(body, rows, [], [(2 * Wq + 2 * Wv, BF16)], 0, R=T, L=L, tr=tr, name=name)[0]


def _swap32(x):
    n = x.shape[-1]
    lane = lax.broadcasted_iota(jnp.int32, x.shape, x.ndim - 1)
    return jnp.where(lane % 64 < 32, pltpu.roll(x, n - 32, x.ndim - 1), pltpu.roll(x, 32, x.ndim - 1))


ATT_Q_BLOCKS = 2


def _stack_heads_at(ref, rows, G):
    return jnp.concatenate([ref[rows, g * LANES:(g + 1) * LANES] for g in range(G)], axis=0)


def _stack_columns_at(ref, rows, G):
    return jnp.concatenate([ref[rows, g:g + 1] for g in range(G)], axis=0)


def _sink_column(sink_ref, h, G):
    return jnp.concatenate([jnp.full((CHUNK, 1), sink_ref[0, h * G + g], F32) for g in range(G)], axis=0)


def _key_mask(n, nb, CTX, G):
    W = 3 * CHUNK + CTX
    ii = lax.broadcasted_iota(jnp.int32, (G * CHUNK, W), 0) & (CHUNK - 1)
    col = lax.broadcasted_iota(jnp.int32, (G * CHUNK, W), 1)
    is_prev = col < CHUNK
    is_next = (col >= 2 * CHUNK) & (col < 3 * CHUNK)
    prev_ok = is_prev & (col >= ii) & (n > 0)
    next_ok = is_next & ((col - 2 * CHUNK) <= ii) & (n < nb - 1)
    return prev_ok | next_ok | jnp.logical_not(is_prev | is_next)


def _attn_geometry(L, CTX):
    nb = L // CHUNK
    QB = ATT_Q_BLOCKS if nb % ATT_Q_BLOCKS == 0 else 1

    def blk(j):
        return pl.BlockSpec((CHUNK, LANES), lambda h, m: (jnp.clip(m * QB + j - 1, 0, nb - 1), h))

    kvs = [blk(j) for j in range(QB + 2)] + [pl.BlockSpec((CTX, LANES), lambda h, m: (L // CTX, h))]
    return nb, QB, kvs


def _attn_fwd(q, k, v, sink, *, L, CTX, Hkv, G, name):
    scale = float(ATT_HEAD_DIM) ** -0.5
    nb, QB, kvs = _attn_geometry(L, CTX)
    nkv = QB + 3

    def body(sink_ref, q_ref, *rest):
        kb, vb, (o_ref, lse_ref) = rest[:nkv], rest[nkv:2 * nkv], rest[2 * nkv:]
        h, m_ = pl.program_id(0), pl.program_id(1)
        sk = _sink_column(sink_ref, h, G)
        for sub in range(QB):
            rows = slice(sub * CHUNK, (sub + 1) * CHUNK)
            qs = _stack_heads_at(q_ref, rows, G)
            kall = jnp.concatenate([kb[sub + j][...] for j in range(3)] + [kb[-1][...]], axis=0)
            vall = jnp.concatenate([vb[sub + j][...] for j in range(3)] + [vb[-1][...]], axis=0)
            s_ = jnp.where(_key_mask(m_ * QB + sub, nb, CTX, G), _dot(qs, kall, _NT) * scale, NEG_INF)
            m = jnp.maximum(jnp.max(s_, axis=-1, keepdims=True), sk)
            p = jnp.exp(s_ - m)
            den = jnp.sum(p, axis=-1, keepdims=True) + jnp.exp(sk - m)
            o = _dot(p.astype(BF16), vall) / den
            lse = m + jnp.log(den)
            for g in range(G):
                o_ref[rows, g * LANES:(g + 1) * LANES] = o[g * CHUNK:(g + 1) * CHUNK].astype(o_ref.dtype)
                lse_ref[rows, g:g + 1] = lse[g * CHUNK:(g + 1) * CHUNK]

    qspec = pl.BlockSpec((QB * CHUNK, G * LANES), lambda h, m: (m, h))
    return pl.pallas_call(
        body, name=name, grid=(Hkv, nb // QB),
        in_specs=[pl.BlockSpec(memory_space=pltpu.SMEM), qspec] + kvs + kvs,
        out_specs=[qspec, pl.BlockSpec((None, QB * CHUNK, G), lambda h, m: (h, m, 0))],
        out_shape=[jax.ShapeDtypeStruct((L, Hkv * G * LANES), BF16), jax.ShapeDtypeStruct((Hkv, L, G), F32)],
        compiler_params=_cparams("parallel", "parallel"),
    )(sink, q, *([k] * nkv), *([v] * nkv))


def _attn_bwd(q, k, v, do, lse, sink, *, L, CTX, Hkv, G, name):
    scale = float(ATT_HEAD_DIM) ** -0.5
    nb, QB, kvs = _attn_geometry(L, CTX)
    nkv = QB + 3
    qspec = pl.BlockSpec((QB * CHUNK, G * LANES), lambda h, m: (m, h))
    rowspec = pl.BlockSpec((None, QB * CHUNK, G), lambda h, m: (h, m, 0))
    colspec = lambda rows: pl.BlockSpec((rows, LANES), lambda h, m: (0, h))

    def body(sink_ref, q_ref, do_ref, lse_ref, *rest):
        kb, vb = rest[:nkv], rest[nkv:2 * nkv]
        dq_ref, dk_ref, dv_ref, dkx_ref, dvx_ref, dsk_ref = rest[2 * nkv:]
        h, m_ = pl.program_id(0), pl.program_id(1)

        @pl.when(m_ == 0)
        def _():
            for r in (dk_ref, dv_ref, dkx_ref, dvx_ref, dsk_ref):
                r[...] = jnp.zeros_like(r)

        sk = _sink_column(sink_ref, h, G)
        for sub in range(QB):
            n = m_ * QB + sub
            rows = slice(sub * CHUNK, (sub + 1) * CHUNK)
            qs, dos = _stack_heads_at(q_ref, rows, G), _stack_heads_at(do_ref, rows, G)
            kall = jnp.concatenate([kb[sub + j][...] for j in range(3)] + [kb[-1][...]], axis=0)
            vall = jnp.concatenate([vb[sub + j][...] for j in range(3)] + [vb[-1][...]], axis=0)
            lse_c = _stack_columns_at(lse_ref, rows, G)
            p = jnp.where(_key_mask(n, nb, CTX, G), jnp.exp(_dot(qs, kall, _NT) * scale - lse_c), 0.0)
            dp = _dot(dos, vall, _NT)
            delta = jnp.sum(p * dp, axis=-1, keepdims=True)
            ds = (p * (dp - delta) * scale).astype(BF16)
            dq = _dot(ds, kall)
            dk_all = _dot(ds, qs, _TN)
            dv_all = _dot(p.astype(BF16), dos, _TN)
            for g in range(G):
                dq_ref[rows, g * LANES:(g + 1) * LANES] = dq[g * CHUNK:(g + 1) * CHUNK]
            for part, blk in enumerate((jnp.maximum(n - 1, 0), n, jnp.minimum(n + 1, nb - 1))):
                krows = pl.ds(pl.multiple_of(blk * CHUNK, CHUNK), CHUNK)
                dk_ref[krows, :] += dk_all[part * CHUNK:(part + 1) * CHUNK]
                dv_ref[krows, :] += dv_all[part * CHUNK:(part + 1) * CHUNK]
            dkx_ref[...] += dk_all[3 * CHUNK:]
            dvx_ref[...] += dv_all[3 * CHUNK:]
            dsink = -jnp.exp(sk - lse_c) * delta
            for g in range(G):
                dsk_ref[g:g + 1, :] += jnp.sum(dsink[g * CHUNK:(g + 1) * CHUNK])

    return pl.pallas_call(
        body, name=name, grid=(Hkv, nb // QB),
        in_specs=[pl.BlockSpec(memory_space=pltpu.SMEM), qspec, qspec, rowspec] + kvs + kvs,
        out_specs=[qspec, colspec(L), colspec(L), colspec(CTX), colspec(CTX),
                   pl.BlockSpec((None, 8, LANES), lambda h, m: (h, 0, 0))],
        out_shape=[jax.ShapeDtypeStruct((L, Hkv * G * LANES), F32),
                   jax.ShapeDtypeStruct((L, Hkv * LANES), F32), jax.ShapeDtypeStruct((L, Hkv * LANES), F32),
                   jax.ShapeDtypeStruct((CTX, Hkv * LANES), F32), jax.ShapeDtypeStruct((CTX, Hkv * LANES), F32),
                   jax.ShapeDtypeStruct((Hkv, 8, LANES), F32)],
        compiler_params=_cparams("parallel", "arbitrary"),
    )(sink, q, do, lse, *([k] * nkv), *([v] * nkv))


def _attn_assemble(dq, dk_lat, dv_lat, dk_ctx, dv_ctx, cos, sin, *, T, L, CTX, Hq, Hkv, tr, name):
    Wq, Wk = Hq * LANES, Hkv * LANES
    ctx_blocks = CTX // tr
    nl = L // tr

    def unrope(d, c, s_, heads):
        return d * jnp.tile(c, (1, heads)) + _swap32(d * jnp.tile(s_, (1, heads)))

    def body(is_ctx, dqt, dkl, dvl, dkc, dvc, c, s_):
        dq_ = jnp.where(is_ctx, 0.0, unrope(dqt, c, s_, Hq))
        dk_ = unrope(jnp.where(is_ctx, dkc, dkl), c, s_, Hkv)
        dv_ = jnp.where(is_ctx, dvc, dvl)
        return (jnp.concatenate([dq_, dk_, dv_], axis=1),), ()

    def ctx_map(i):
        return (jnp.clip(i - nl, 0, ctx_blocks - 1), 0)

    assert T % tr == 0 and L % tr == 0 and CTX % tr == 0
    in_specs = [pl.BlockSpec((tr, Wq), lambda i: (jnp.minimum(i, nl - 1), 0)),
                pl.BlockSpec((tr, Wk), lambda i: (jnp.minimum(i, nl - 1), 0)),
                pl.BlockSpec((tr, Wk), lambda i: (jnp.minimum(i, nl - 1), 0)),
                pl.BlockSpec((tr, Wk), ctx_map), pl.BlockSpec((tr, Wk), ctx_map),
                pl.BlockSpec((tr, LANES), lambda i: (i, 0)), pl.BlockSpec((tr, LANES), lambda i: (i, 0))]

    def kern(dq_r, dkl_r, dvl_r, dkc_r, dvc_r, c_r, s_r, o_ref):
        is_ctx = pl.program_id(0) >= nl
        (out,), _ = body(is_ctx, dq_r[...], dkl_r[...], dvl_r[...], dkc_r[...], dvc_r[...], c_r[...], s_r[...])
        o_ref[...] = out.astype(o_ref.dtype)

    return pl.pallas_call(
        kern, name=name, grid=(T // tr,), in_specs=in_specs,
        out_specs=pl.BlockSpec((tr, Wq + 2 * Wk), lambda i: (i, 0)),
        out_shape=jax.ShapeDtypeStruct((T, Wq + 2 * Wk), BF16),
        compiler_params=_cparams("parallel"),
    )(dq, dk_lat, dv_lat, dk_ctx, dv_ctx, cos, sin)


def _my_place():
    x, y, c = lax.axis_index("x"), lax.axis_index("y"), lax.axis_index("c")
    return x, y, c


def _all_gather_small(v, *, name):
    R, C = v.shape

    def body(x_ref, out_ref, send_sems, recv_sems, local_sem):
        x, y, c = _my_place()
        me, sibling = (x, y, c), (x, y, 1 - c)
        chips = [(1 - x, y), (x, 1 - y), (1 - x, 1 - y)]

        def slot(px, py, pc):
            return out_ref.at[4 * px + 2 * py + pc]

        def copy(k, block, to, src=None):
            return pltpu.make_async_remote_copy(
                src_ref=slot(*block) if src is None else src, dst_ref=slot(*block),
                send_sem=send_sems.at[k], recv_sem=recv_sems.at[k], device_id=to, device_id_type=MESH)

        mine = pltpu.make_async_copy(x_ref, slot(*me), local_sem)
        mine.start()
        first = [copy(0, me, sibling, src=x_ref)]
        first += [copy(1 + j, me, (*chip, c), src=x_ref) for j, chip in enumerate(chips)]
        for cp in first:
            cp.start()
        passed = [copy(4 + j, (*chip, c), sibling) for j, chip in enumerate(chips)]
        for j, chip in enumerate(chips):
            copy(1 + j, (*chip, c), me).wait_recv()
            passed[j].start()
        copy(0, sibling, me).wait_recv()
        for j, chip in enumerate(chips):
            copy(4 + j, (*chip, 1 - c), me).wait_recv()
        for cp in first + passed:
            cp.wait_send()
        mine.wait()

    return pl.pallas_call(
        body, name=name, out_shape=jax.ShapeDtypeStruct((N_DEV, R, C), v.dtype),
        in_specs=[pl.BlockSpec(memory_space=pltpu.VMEM)], out_specs=pl.BlockSpec(memory_space=pltpu.VMEM),
        scratch_shapes=[pltpu.SemaphoreType.DMA((7,)), pltpu.SemaphoreType.DMA((7,)), pltpu.SemaphoreType.DMA],
    )(v)


_HBM_SPEC = pl.BlockSpec(memory_space=pltpu.HBM)
_SEM_SPEC = pl.BlockSpec(memory_space=pltpu.SEMAPHORE)
_ANY_SPEC = pl.BlockSpec(memory_space=pl.ANY)
_DATAFLOW = pltpu.SideEffectType.DATAFLOW_SIDE_EFFECTING
N_PEERS = N_DEV - 1


def _peer(r):
    x, y, c = _my_place()
    return ((1 - x) if r & 4 else x, (1 - y) if r & 2 else y, (1 - c) if r & 1 else c)


def _index_of(place):
    return 4 * place[0] + 2 * place[1] + place[2]


def _slot(ref, axis, idx, size):
    if axis == 0:
        return ref.at[pl.ds(idx * size, size), :]
    return ref.at[:, pl.ds(idx * size, size)]


def _cast_place(w3, layer, axis, me_arr, dep, *, name):
    Ks, Ns = w3.shape[1], w3.shape[2]
    tr = _pick(Ks, 256, 16)
    nblk = Ks // tr
    full = (Ks * N_DEV, Ns) if axis == 0 else (Ks, Ns * N_DEV)
    if axis == 0:
        out_map = lambda i, me: (me[0] * nblk + i, 0)
    else:
        out_map = lambda i, me: (i, me[0])

    def body(me_ref, w_ref, dep_ref, o_ref):
        o_ref[...] = w_ref[...].astype(BF16)

    return pl.pallas_call(
        body, name=name, out_shape=jax.ShapeDtypeStruct(full, BF16),
        grid_spec=pltpu.PrefetchScalarGridSpec(
            num_scalar_prefetch=1, grid=(nblk,),
            in_specs=[pl.BlockSpec((None, tr, Ns), lambda i, me: (layer, i, 0)), pl.BlockSpec(memory_space=pl.ANY)],
            out_specs=pl.BlockSpec((tr, Ns), out_map)),
        compiler_params=_cparams("parallel"),
    )(me_arr, w3, dep)


AG_FIRST = 4
AG_CHIPS = 3


def _sibling():
    x, y, c = _my_place()
    return (x, y, 1 - c)


def _chip_peer(j, same_core=True):
    x, y, c = _my_place()
    px = (1 - x) if j in (0, 2) else x
    py = (1 - y) if j in (1, 2) else y
    return (px, py, c if same_core else 1 - c)


def _gather_start(lands, axes, after, *, name):
    nt = len(lands)
    sizes = [l.shape[a] // N_DEV for l, a in zip(lands, axes)]

    def body(*refs):
        ins, send_sems, recv_sems, token = refs[:nt], refs[nt + 1], refs[nt + 2], refs[-1]
        my_idx = _index_of(_my_place())
        for t in range(nt):
            mine = _slot(ins[t], axes[t], my_idx, sizes[t])
            for k, to in enumerate([_sibling()] + [_chip_peer(j) for j in range(AG_CHIPS)]):
                pltpu.make_async_remote_copy(src_ref=mine, dst_ref=mine, send_sem=send_sems.at[t * AG_FIRST + k],
                                             recv_sem=recv_sems.at[t * AG_FIRST + k], device_id=to,
                                             device_id_type=MESH).start()
        token[...] = jnp.zeros_like(token)

    res = pl.pallas_call(
        body, name=name,
        out_shape=(pltpu.SemaphoreType.DMA((nt * AG_FIRST,)), pltpu.SemaphoreType.DMA((nt * AG_FIRST,)),
                   *[pltpu.HBM(l.shape, l.dtype) for l in lands], jax.ShapeDtypeStruct((8, LANES), F32)),
        in_specs=[_HBM_SPEC] * nt + [_ANY_SPEC],
        out_specs=(_SEM_SPEC, _SEM_SPEC, *[_HBM_SPEC] * nt, pl.BlockSpec(memory_space=pltpu.VMEM)),
        input_output_aliases={t: 2 + t for t in range(nt)},
        compiler_params=pltpu.CompilerParams(has_side_effects=_DATAFLOW),
    )(*[pltpu.with_memory_space_constraint(l, pltpu.HBM) for l in lands], after)
    return res[0], res[1], list(res[2:2 + nt]), res[-1]


def _gather_forward(send_a, recv_a, lands, axes, after, *, name):
    nt = len(lands)
    sizes = [l.shape[a] // N_DEV for l, a in zip(lands, axes)]

    def body(*refs):
        ins, send_a, recv_a = refs[:nt], refs[nt], refs[nt + 1]
        send_f, recv_f, token = refs[nt + 3], refs[nt + 4], refs[-1]
        my_idx = _index_of(_my_place())
        for t in range(nt):
            for j in range(AG_CHIPS):
                src_dev = _chip_peer(j)
                arrived = _slot(ins[t], axes[t], _index_of(src_dev), sizes[t])
                pltpu.make_async_remote_copy(
                    src_ref=_slot(ins[t], axes[t], my_idx, sizes[t]), dst_ref=arrived,
                    send_sem=send_a.at[t * AG_FIRST + 1 + j], recv_sem=recv_a.at[t * AG_FIRST + 1 + j],
                    device_id=src_dev, device_id_type=MESH).wait_recv()
                pltpu.make_async_remote_copy(src_ref=arrived, dst_ref=arrived, send_sem=send_f.at[t * AG_CHIPS + j],
                                             recv_sem=recv_f.at[t * AG_CHIPS + j], device_id=_sibling(),
                                             device_id_type=MESH).start()
        token[...] = jnp.zeros_like(token)

    res = pl.pallas_call(
        body, name=name,
        out_shape=(pltpu.SemaphoreType.DMA((nt * AG_CHIPS,)), pltpu.SemaphoreType.DMA((nt * AG_CHIPS,)),
                   *[pltpu.HBM(l.shape, l.dtype) for l in lands], jax.ShapeDtypeStruct((8, LANES), F32)),
        in_specs=[_HBM_SPEC] * nt + [_SEM_SPEC, _SEM_SPEC, _ANY_SPEC],
        out_specs=(_SEM_SPEC, _SEM_SPEC, *[_HBM_SPEC] * nt, pl.BlockSpec(memory_space=pltpu.VMEM)),
        input_output_aliases={t: 2 + t for t in range(nt)},
        compiler_params=pltpu.CompilerParams(has_side_effects=_DATAFLOW),
    )(*lands, send_a, recv_a, after)
    return res[0], res[1], list(res[2:2 + nt]), res[-1]


def _gather_wait(send_a, recv_a, send_f, recv_f, lands, axes, after, *, name):
    nt = len(lands)
    sizes = [l.shape[a] // N_DEV for l, a in zip(lands, axes)]

    def body(*refs):
        ins, send_a, recv_a, send_f, recv_f = refs[:nt], refs[nt], refs[nt + 1], refs[nt + 2], refs[nt + 3]
        my_idx = _index_of(_my_place())
        sib = _sibling()
        for t in range(nt):
            mine = _slot(ins[t], axes[t], my_idx, sizes[t])
            for k, to in enumerate([sib] + [_chip_peer(j) for j in range(AG_CHIPS)]):
                pltpu.make_async_remote_copy(src_ref=mine, dst_ref=mine, send_sem=send_a.at[t * AG_FIRST + k],
                                             recv_sem=recv_a.at[t * AG_FIRST + k], device_id=to,
                                             device_id_type=MESH).wait_send()
            pltpu.make_async_remote_copy(src_ref=mine, dst_ref=_slot(ins[t], axes[t], _index_of(sib), sizes[t]),
                                         send_sem=send_a.at[t * AG_FIRST], recv_sem=recv_a.at[t * AG_FIRST],
                                         device_id=sib, device_id_type=MESH).wait_recv()
            for j in range(AG_CHIPS):
                sent = _slot(ins[t], axes[t], _index_of(_chip_peer(j)), sizes[t])
                got = _slot(ins[t], axes[t], _index_of(_chip_peer(j, same_core=False)), sizes[t])
                cp = pltpu.make_async_remote_copy(src_ref=sent, dst_ref=got, send_sem=send_f.at[t * AG_CHIPS + j],
                                                  recv_sem=recv_f.at[t * AG_CHIPS + j], device_id=sib,
                                                  device_id_type=MESH)
                cp.wait_send()
                cp.wait_recv()

    res = pl.pallas_call(
        body, name=name, out_shape=[pltpu.HBM(l.shape, l.dtype) for l in lands],
        in_specs=[_HBM_SPEC] * nt + [_SEM_SPEC] * 4 + [_ANY_SPEC], out_specs=[_HBM_SPEC] * nt,
        input_output_aliases={t: t for t in range(nt)},
        compiler_params=pltpu.CompilerParams(has_side_effects=_DATAFLOW),
    )(*lands, send_a, recv_a, send_f, recv_f, after)
    return list(res)


def _scatter_start(dw, axis, *, name):
    size = dw.shape[axis] // N_DEV
    land_shape = (N_PEERS, size, dw.shape[1]) if axis == 0 else (N_PEERS, dw.shape[0], size)

    def body(dw_ref, land_ref, send_sems, recv_sems, dw_thru, land_thru, token):
        for r in range(1, N_DEV):
            p = _peer(r)
            pltpu.make_async_remote_copy(src_ref=_slot(dw_ref, axis, _index_of(p), size), dst_ref=land_ref.at[r - 1],
                                         send_sem=send_sems.at[r - 1], recv_sem=recv_sems.at[r - 1], device_id=p,
                                         device_id_type=MESH).start()
        token[...] = jnp.zeros_like(token)

    land = pltpu.with_memory_space_constraint(lax.empty(land_shape, dw.dtype), pltpu.HBM)
    return pl.pallas_call(
        body, name=name,
        out_shape=(pltpu.SemaphoreType.DMA((N_PEERS,)), pltpu.SemaphoreType.DMA((N_PEERS,)),
                   pltpu.HBM(dw.shape, dw.dtype), pltpu.HBM(land_shape, dw.dtype), jax.ShapeDtypeStruct((8, LANES), F32)),
        in_specs=[_HBM_SPEC, _HBM_SPEC],
        out_specs=(_SEM_SPEC, _SEM_SPEC, _HBM_SPEC, _HBM_SPEC, pl.BlockSpec(memory_space=pltpu.VMEM)),
        input_output_aliases={0: 2, 1: 3},
        compiler_params=pltpu.CompilerParams(has_side_effects=_DATAFLOW),
    )(pltpu.with_memory_space_constraint(dw, pltpu.HBM), land)


def _scatter_wait(send_sems, recv_sems, dw, land, axis, after, *, name):
    size = dw.shape[axis] // N_DEV

    def body(dw_ref, land_ref, send_sems, recv_sems, after_ref, dw_thru, land_thru):
        for r in range(1, N_DEV):
            p = _peer(r)
            cp = pltpu.make_async_remote_copy(src_ref=_slot(dw_ref, axis, _index_of(p), size), dst_ref=land_ref.at[r - 1],
                                              send_sem=send_sems.at[r - 1], recv_sem=recv_sems.at[r - 1], device_id=p,
                                              device_id_type=MESH)
            cp.wait_send()
            cp.wait_recv()

    return pl.pallas_call(
        body, name=name, out_shape=(pltpu.HBM(dw.shape, dw.dtype), pltpu.HBM(land.shape, land.dtype)),
        in_specs=[_HBM_SPEC, _HBM_SPEC, _SEM_SPEC, _SEM_SPEC, _ANY_SPEC], out_specs=(_HBM_SPEC, _HBM_SPEC),
        input_output_aliases={0: 0, 1: 1},
        compiler_params=pltpu.CompilerParams(has_side_effects=_DATAFLOW),
    )(dw, land, send_sems, recv_sems, after)


def _adamw_math(w, g, m, v):
    m = ADAM_B1 * m + (1.0 - ADAM_B1) * g
    v = ADAM_B2 * v + (1.0 - ADAM_B2) * (g * g)
    m_hat = m / (1.0 - ADAM_B1 ** ADAM_STEP)
    v_hat = v / (1.0 - ADAM_B2 ** ADAM_STEP)
    delta = -ADAM_LR * (m_hat / (jnp.sqrt(v_hat) + ADAM_EPS) + ADAM_WD * w)
    return delta, m, v


def _adamw_sharded(w, m, v, layer, dw, land, axis, me_arr, prev, *, name):
    nl, Ks, Ns = w.shape
    tr = _pick(Ks, 128, 16)
    nblk = Ks // tr
    if axis == 0:
        own_map = lambda i, me: (me[0] * nblk + i, 0)
    else:
        own_map = lambda i, me: (i, me[0])
    wspec = pl.BlockSpec((None, tr, Ns), lambda i, me: (layer, i, 0))
    n_prev = 0 if prev is None else 4

    def body(me_ref, w_ref, m_ref, v_ref, own_ref, r_ref, *rest):
        g_ref, d_ref, nm_ref, nv_ref = rest[n_prev:]
        g = own_ref[...].astype(F32)
        for r in range(N_PEERS):
            g = g + r_ref[r].astype(F32)
        delta, nm, nv = _adamw_math(w_ref[...], g, m_ref[...], v_ref[...])
        g_ref[...], d_ref[...], nm_ref[...], nv_ref[...] = g, delta, nm, nv

    return pl.pallas_call(
        body, name=name, out_shape=[jax.ShapeDtypeStruct((nl, Ks, Ns), F32)] * 4,
        grid_spec=pltpu.PrefetchScalarGridSpec(
            num_scalar_prefetch=1, grid=(nblk,),
            in_specs=[wspec, wspec, wspec, pl.BlockSpec((tr, Ns), own_map),
                      pl.BlockSpec((N_PEERS, tr, Ns), lambda i, me: (0, i, 0))] + [_ANY_SPEC] * n_prev,
            out_specs=[wspec] * 4),
        input_output_aliases={6 + k: k for k in range(n_prev)},
        compiler_params=_cparams("parallel"),
    )(me_arr, w, m, v, dw, land, *(prev or []))


def _adamw_flat(w, g, m, v, *, name):
    def body(w_ref, g_ref, m_ref, v_ref, d_ref, nm_ref, nv_ref):
        d_ref[...], nm_ref[...], nv_ref[...] = _adamw_math(w_ref[...], g_ref[...], m_ref[...], v_ref[...])

    spec = pl.BlockSpec(memory_space=pltpu.VMEM)
    return pl.pallas_call(body, name=name, in_specs=[spec] * 4, out_specs=[spec] * 3,
                          out_shape=[jax.ShapeDtypeStruct(w.shape, F32)] * 3)(w, g, m, v)


def _sum_devices(a, *, name):
    def body(a_ref, o_ref):
        s = a_ref[0]
        for d in range(1, N_DEV):
            s = s + a_ref[d]
        o_ref[...] = s

    spec = pl.BlockSpec(memory_space=pltpu.VMEM)
    return pl.pallas_call(body, name=name, in_specs=[spec], out_specs=spec,
                          out_shape=jax.ShapeDtypeStruct(a.shape[1:], F32))(a)


def _ada_mods(c16, ada_w, ada_b_cols, *, name):
    nl, D, cols = ada_w.shape
    bn = _pick(cols, 512)

    def body(c_ref, w_ref, b_ref, o_ref):
        cond = _silu(c_ref[...]).astype(BF16)
        o_ref[...] = _dot(cond, w_ref[...].astype(BF16)) + b_ref[...]

    return pl.pallas_call(
        body, name=name, grid=(nl, cols // bn),
        in_specs=[pl.BlockSpec((16, D), lambda l, j: (0, 0)), pl.BlockSpec((None, D, bn), lambda l, j: (l, 0, j)),
                  pl.BlockSpec((None, 1, bn), lambda l, j: (l, 0, j))],
        out_specs=pl.BlockSpec((None, 16, bn), lambda l, j: (l, 0, j)),
        out_shape=jax.ShapeDtypeStruct((nl, 16, cols), F32),
        compiler_params=_cparams("parallel", "parallel"),
    )(c16, ada_w, ada_b_cols)


def _ada_bwd(cond_t, dmod, w, m, v, *, name):
    nl, D, cols = w.shape
    tr = _pick(D, 256, 8)

    def body(ct_ref, dm_ref, w_ref, m_ref, v_ref, g_ref, d_ref, nm_ref, nv_ref, dc_ref):
        ct, dm, wt = ct_ref[...], dm_ref[...], w_ref[...]
        g = ct[:, 0:1] * dm[0:1, :]
        for r in range(1, N_DEV + 1):
            g = g + ct[:, r:r + 1] * dm[r:r + 1, :]
        delta, nm, nv = _adamw_math(wt, g, m_ref[...], v_ref[...])
        g_ref[...], d_ref[...], nm_ref[...], nv_ref[...] = g, delta, nm, nv
        dc_ref[...] = jnp.sum(wt * dm[N_DEV:N_DEV + 1, :], axis=-1, keepdims=True)

    wspec = pl.BlockSpec((None, tr, cols), lambda l, i: (l, i, 0))
    return pl.pallas_call(
        body, name=name, grid=(nl, D // tr),
        in_specs=[pl.BlockSpec((tr, 16), lambda l, i: (i, 0)), pl.BlockSpec((None, 16, cols), lambda l, i: (l, 0, 0)),
                  wspec, wspec, wspec],
        out_specs=[wspec] * 4 + [pl.BlockSpec((None, tr, 1), lambda l, i: (l, i, 0))],
        out_shape=[jax.ShapeDtypeStruct((nl, D, cols), F32)] * 4 + [jax.ShapeDtypeStruct((nl, D, 1), F32)],
        compiler_params=_cparams("parallel", "parallel"),
    )(cond_t, dmod, w, m, v)


def _rope_tables(L, CTX):
    def angles(pos, dim):
        inv_freq = ROPE_BASE ** (-jnp.arange(0, dim, 2, dtype=F32) / dim)
        return pos.astype(F32)[:, None] * inv_freq[None, :]

    def pad(cos, sin):
        return (jnp.concatenate([cos, jnp.ones((CTX, LANES), F32)], 0),
                jnp.concatenate([sin, jnp.zeros((CTX, LANES), F32)], 0))

    ret = angles(jnp.arange(L), 2 * LANES)
    ret_cs = pad(jnp.cos(ret), jnp.sin(ret))
    rows = angles(jnp.arange(L) // GRID_W, ATT_HEAD_DIM // 2)
    cols = angles(jnp.arange(L) % GRID_W, ATT_HEAD_DIM // 2)
    cos = jnp.concatenate([jnp.cos(rows)] * 2 + [jnp.cos(cols)] * 2, axis=1)
    sin = jnp.concatenate([-jnp.sin(rows), jnp.sin(rows), -jnp.sin(cols), jnp.sin(cols)], axis=1)
    return ret_cs, pad(cos, sin)


def kernel(x, c, ctx, c_ctx, ada_w, ada_b, norm_mix_g, norm_mlp_g, mlp_w1, mlp_w2, ret_w_in, ret_w_out, ret_decay_fwd, ret_decay_bwd, attn_w_in, attn_w_out, attn_sink, final_norm_g, loss_target, m_c_ctx, m_ada_w, m_ada_b, m_norm_mix_g, m_norm_mlp_g, m_mlp_w1, m_mlp_w2, m_ret_w_in, m_ret_w_out, m_ret_decay_fwd, m_ret_decay_bwd, m_attn_w_in, m_attn_w_out, m_attn_sink, m_final_norm_g, v_c_ctx, v_ada_w, v_ada_b, v_norm_mix_g, v_norm_mlp_g, v_mlp_w1, v_mlp_w2, v_ret_w_in, v_ret_w_out, v_ret_decay_fwd, v_ret_decay_bwd, v_attn_w_in, v_attn_w_out, v_attn_sink, v_final_norm_g):
    L, D = x.shape[1], x.shape[2]
    CTX = ctx.shape[1]
    T = L + CTX
    RH = ret_decay_fwd.shape[-1]
    assert D == RH * 2 * LANES and ada_w.shape[0] == 2 and ret_w_in.shape[0] == 1 and attn_w_in.shape[0] == 1
    Hq = attn_sink.shape[-1]
    Hkv = (attn_w_in.shape[-1] * N_DEV // ATT_HEAD_DIM - Hq) // 2
    G = Hq // Hkv
    FF = mlp_w1.shape[-1] * N_DEV
    Wq_r, Wv_r = RH * 2 * LANES, RH * 4 * LANES
    acols = ada_w.shape[-1]
    tr = _pick(CTX, 256, 8)
    tr_wide = _pick(CTX, 128, 8)
    bmT = T // 4 if (T % 64 == 0) else T
    bmL = L // 4 if (L % 64 == 0) else L
    x_idx, y_idx, c_idx = lax.axis_index("x"), lax.axis_index("y"), lax.axis_index("c")
    me = 4 * x_idx + 2 * y_idx + c_idx
    me_arr = jnp.reshape(me, (1,)).astype(jnp.int32)

    (rcos, rsin), (acos, asin) = _rope_tables(L, CTX)
    lg_f = jax.nn.log_sigmoid(ret_decay_fwd.astype(F32))
    lg_b = jax.nn.log_sigmoid(ret_decay_bwd.astype(F32))

    c_pad = jnp.concatenate([c.astype(F32), jnp.zeros((7, D), F32)], 0)
    c_all = _all_gather_small(c_pad, name="ag_c")[:, 0, :]
    c16 = jnp.concatenate([c_all, c_ctx[None, :], jnp.zeros((7, D), F32)], 0)
    ada_b_cols = lax.dynamic_slice_in_dim(ada_b, me * acols, acols, axis=1)[:, None, :]
    mods_shard = _ada_mods(c16, ada_w, ada_b_cols, name="ada_mods")
    mods_all = _all_gather_small(mods_shard.reshape(32, acols), name="ag_mods")

    wdefs = {"ret_in": (ret_w_in, 0, 1), "ret_out": (ret_w_out, 0, 0), "w1_0": (mlp_w1, 0, 1), "w2_0": (mlp_w2, 0, 0),
             "attn_in": (attn_w_in, 0, 1), "attn_out": (attn_w_out, 0, 0), "w1_1": (mlp_w1, 1, 1), "w2_1": (mlp_w2, 1, 0)}
    groups = [["ret_in"], ["ret_out", "w1_0", "w2_0"], ["attn_in", "attn_out"], ["w1_1", "w2_1"]]

    placed = {}

    def ag_start(gi, after):
        g_axes = [wdefs[k][2] for k in groups[gi]]
        ssem, rsem, lands, tok_ = _gather_start([placed[k] for k in groups[gi]], g_axes, after, name=f"ag_start{gi}")
        return dict(a=(ssem, rsem), lands=lands, axes=g_axes, gi=gi), tok_

    def ag_forward(g, after):
        fs, fr, g["lands"], tok_ = _gather_forward(*g["a"], g["lands"], g["axes"], after, name=f"ag_forward{g['gi']}")
        g["f"] = (fs, fr)
        return tok_

    def ag_wait(g, after):
        return _gather_wait(*g["a"], *g["f"], g["lands"], g["axes"], after, name=f"ag_wait{g['gi']}")

    placed["ret_in"] = _cast_place(*wdefs["ret_in"], me_arr, mods_all, name="place_ret_in")
    g0, tok = ag_start(0, mods_all)
    last_cast = tok
    for keys in groups[1:]:
        for k in keys:
            last_cast = placed[k] = _cast_place(*wdefs[k], me_arr, last_cast, name=f"place_{k}")
    mods_all = (mods_all + tok[0, 0]).reshape(N_DEV, 2, 16, acols).transpose(1, 2, 0, 3).reshape(2, 16, 6, D)
    mod_lat = lax.dynamic_index_in_dim(mods_all, me, axis=1, keepdims=False)
    mod_ctx = mods_all[:, N_DEV]

    def pack(i, ks, kc):
        return jnp.stack([mod_lat[i, ks], mod_lat[i, kc], mod_ctx[i, ks], mod_ctx[i, kc]], 0)

    def gates(i, k):
        return jnp.stack([mod_lat[i, k], mod_ctx[i, k]], 0)

    def gate_epilogue(gl, gc, x_rows_lat_only):
        def epi(acc, i, j, xt, gv):
            if x_rows_lat_only:
                gate = gv[0:1, :]
            else:
                row = i * acc.shape[0] + lax.broadcasted_iota(jnp.int32, (acc.shape[0], 1), 0)
                gate = jnp.where(row >= L, gv[1:2, :], gv[0:1, :])
            return xt + gate * acc, acc
        return epi

    w1, w2 = {}, {}

    mmT = dict(M=T, bm=bmT)
    mmL = dict(M=L, bm=bmL)

    def bn_of(n, off=0):
        b = MM_BN
        while n % b or off % b:
            b -= LANES
        return b

    X0 = jnp.concatenate([x[0], ctx[0]], axis=0)
    g_mix0, g_mlp0 = norm_mix_g[0:1], norm_mlp_g[0:1]
    g_mix1, g_mlp1 = norm_mix_g[1:2], norm_mlp_g[1:2]
    a0 = _normmod(X0, g_mix0, pack(0, 0, 1), R=T, L=L, tr=tr, name="normmod_mix0", dep=last_cast)
    tok = ag_forward(g0, a0)
    (wr_in,) = ag_wait(g0, tok)
    g1, tok = ag_start(1, wr_in)

    bn_qk = _pick(Wq_r, MM_BN, 2 * LANES)
    nq_blocks = Wq_r // bn_qk
    kscale = float(2 * LANES) ** -0.5

    def rope_epi(acc, i, j, cos, sin):
        parts = []
        for h in range(acc.shape[1] // (2 * LANES)):
            x1 = acc[:, h * 2 * LANES:h * 2 * LANES + LANES]
            x2 = acc[:, h * 2 * LANES + LANES:(h + 1) * 2 * LANES]
            parts += [x1 * cos - x2 * sin, x2 * cos + x1 * sin]
        return (jnp.concatenate(parts, axis=1) * jnp.where(j < nq_blocks, 1.0, kscale),)

    def row_tile(arr, bm):
        return (arr, (bm, LANES), lambda i, j: (i, 0))

    (qk0,) = _mm(a0, wr_in, "nn", [BF16], N=2 * Wq_r, K=D, bn=bn_qk, bk=D, name="ret_qk", epilogue=rope_epi,
                 extras=[row_tile(rcos, bmT), row_tile(rsin, bmT)], dep=tok, **mmT)
    bn_vg = bn_of(2 * Wv_r, 2 * Wq_r)
    (vg0,) = _mm(a0, wr_in, "nn", [BF16], N=2 * Wv_r, K=D, bn=bn_vg, bk=D, name="ret_vg", b_col0=2 * Wq_r, dep=tok,
                 **mmT)

    of, st_f = _ret_fwd(qk0, vg0, lg_f, None, T=T, L=L, H=RH, rev=False, name="ret_scan_f")
    o0, st_b = _ret_fwd(qk0, vg0, lg_b, of, T=T, L=L, H=RH, rev=True, name="ret_scan_b")
    tok = ag_forward(g1, o0)
    z0 = _readout(o0, vg0, T=T, L=L, H=RH, tr=tr_wide, name="ret_readout", dep=tok)
    wr_out, w1[0], w2[0] = ag_wait(g1, z0)
    g2, tok = ag_start(2, wr_out)
    g3, tok = ag_start(3, tok)

    bnD = _pick(D, MM_BN)

    def xtile(arr, bm):
        return (arr, (bm, bnD), lambda i, j: (i, j))

    def gtile(gv):
        return (gv, (2, bnD), lambda i, j: (0, j))

    bk_max = 2048
    X1, ro0 = _mm(z0, wr_out, "nn", [F32, BF16], N=D, K=Wv_r, bn=bnD, bk=_pick(Wv_r, bk_max), name="ret_out",
                  epilogue=gate_epilogue(None, None, False), extras=[xtile(X0, bmT), gtile(gates(0, 2))], dep=tok, **mmT)

    def mlp_fwd(Xin, i, g_mlp, rows, name):
        a = _normmod(Xin, g_mlp, pack(i, 3, 4), R=rows["M"], L=L, tr=tr, name=f"normmod_mlp{name}")

        def relu2(acc, i_, j_):
            u = jnp.maximum(acc, 0.0)
            return u, u * u

        bnF = _pick(FF, MM_BN)
        u, r = _mm(a, w1[i], "nn", [BF16, BF16], N=FF, K=D, bn=bnF, bk=D, name=f"mlp_up{name}", epilogue=relu2, **rows)
        Xout, mo = _mm(r, w2[i], "nn", [F32, BF16], N=D, K=FF, bn=bnD, bk=_pick(FF, bk_max), name=f"mlp_down{name}",
                       epilogue=gate_epilogue(None, None, rows["M"] == L),
                       extras=[xtile(Xin, rows["bm"]), gtile(gates(i, 5))], **rows)
        return a, u, r, Xout, mo

    a1, u0, r0, X2, mo0 = mlp_fwd(X1, 0, g_mlp0, mmT, "0")

    tok = ag_forward(g2, X2)
    a2 = _normmod(X2, g_mix1, pack(1, 0, 1), R=T, L=L, tr=tr, name="normmod_mix1", dep=tok)
    wa_in, wa_out = ag_wait(g2, a2)
    Wq_a, Wk_a = Hq * LANES, Hkv * LANES

    def arope_epi(acc, i, j, cos, sin):
        heads = acc.shape[1] // LANES
        return (acc * jnp.tile(cos, (1, heads)) + _swap32(acc) * jnp.tile(sin, (1, heads)),)

    bn_q = _pick(Wq_a, MM_BN)
    (q1,) = _mm(a2, wa_in, "nn", [BF16], N=Wq_a, K=D, bn=bn_q, bk=D, name="attn_q", epilogue=arope_epi,
                extras=[row_tile(acos, bmL), row_tile(asin, bmL)], **mmL)
    bn_k = bn_of(Wk_a, Wq_a)
    (k1,) = _mm(a2, wa_in, "nn", [BF16], N=Wk_a, K=D, bn=bn_k, bk=D, name="attn_k", b_col0=Wq_a, epilogue=arope_epi,
                extras=[row_tile(acos, bmT), row_tile(asin, bmT)], **mmT)
    bn_v = bn_of(Wk_a, Wq_a + Wk_a)
    (v1,) = _mm(a2, wa_in, "nn", [BF16], N=Wk_a, K=D, bn=bn_v, bk=D, name="attn_v", b_col0=Wq_a + Wk_a, **mmT)
    tok = ag_forward(g3, q1)
    o1, lse = _attn_fwd(q1, k1, v1, attn_sink + tok[0, 0], L=L, CTX=CTX, Hkv=Hkv, G=G, name="attn_fwd")
    w1[1], w2[1] = ag_wait(g3, o1)
    X3, ao = _mm(o1, wa_out, "nn", [F32, BF16], N=D, K=Wq_a, bn=bnD, bk=_pick(Wq_a, 2048), name="attn_out",
                 epilogue=gate_epilogue(None, None, True), extras=[xtile(X2, bmL), gtile(gates(1, 2))], **mmL)
    a3, u1, r1, X4, mo1 = mlp_fwd(X3, 1, g_mlp1, mmL, "1")

    dX4, dmo1, acc_head = _loss_head(X4, loss_target[0], mo1, final_norm_g[None, :], gates(1, 5)[0:1], L=L, tr=tr,
                                     name="loss_head")
    loss_part = jnp.sum(acc_head[0, 0])
    d_gf = acc_head[0, 1]
    zeros_d = jnp.zeros((D,), F32)
    dmod_lat = [[zeros_d] * 6, [zeros_d] * 6]
    dmod_ctx = [[zeros_d] * 6, [zeros_d] * 6]
    dmod_lat[1][5] = acc_head[0, 2]

    def dw_mm(a, b, M, N, K, name):
        return _mm(a, b, "tn", [BF16], M=M, N=N, K=K, bm=_pick(M, 512), bn=_pick(N, MM_BN), bk=K, name=name)[0]

    def mlp_bwd(dmo, a, u, r, i, rows, name):
        Mr = rows["M"]

        def times_2u(acc, i_, j_, ut):
            return (acc * (2.0 * ut.astype(F32)),)

        bnF = _pick(FF, MM_BN)
        dw2 = dw_mm(r, dmo, FF, D, Mr, f"mlp_down_dw{name}")
        tok_ = send_grad(f"w2_{i}", dw2, 0)
        (dh,) = _mm(dmo, w2[i], "nt", [BF16], N=FF, K=D, bn=bnF, bk=D, name=f"mlp_down_dx{name}", epilogue=times_2u,
                    extras=[(u, (rows["bm"], bnF), lambda i_, j_: (i_, j_))], dep=tok_, **rows)
        dw1 = dw_mm(a, dh, D, FF, Mr, f"mlp_up_dw{name}")
        tok_ = send_grad(f"w1_{i}", dw1, 1)
        (da,) = _mm(dh, w1[i], "nt", [F32], N=D, K=FF, bn=bnD, bk=_pick(FF, bk_max), name=f"mlp_up_dx{name}", dep=tok_,
                    **rows)
        return da

    pending = []

    def send_grad(key, dw, axis):
        ssem, rsem, dw_thru, land, tok_ = _scatter_start(dw, axis, name=f"rs_start_{key}")
        pending.append((key, axis, ssem, rsem, dw_thru, land))
        return tok_

    da3 = mlp_bwd(dmo1, a3, u1, r1, 1, mmL, "1")
    dX3, dao, acc = _normmod_bwd(X3, da3, dX4, False, g_mlp1, pack(1, 3, 4), (ao, gates(1, 2)), R=L, L=L, tr=tr,
                                 name="normmod_mlp1_bwd")
    dmod_lat[1][3], dmod_lat[1][4], d_gmlp1, dmod_lat[1][2] = acc[0, 0], acc[0, 1], acc[0, 2], acc[0, 3]

    dwa_out = dw_mm(o1, dao, Wq_a, D, L, "attn_out_dw")
    tok = send_grad("attn_out", dwa_out, 0)
    (do1,) = _mm(dao, wa_out, "nt", [BF16], N=Wq_a, K=D, bn=_pick(Wq_a, MM_BN), bk=D, name="attn_out_dx", dep=tok, **mmL)
    dq1, dk1, dv1, dkx, dvx, dsink_acc = _attn_bwd(q1, k1, v1, do1, lse, attn_sink, L=L, CTX=CTX, Hkv=Hkv, G=G,
                                                   name="attn_bwd")
    dp1 = _attn_assemble(dq1, dk1, dv1, dkx, dvx, acos, asin, T=T, L=L, CTX=CTX, Hq=Hq, Hkv=Hkv, tr=tr_wide,
                         name="attn_assemble")
    Wa_in = Wq_a + 2 * Wk_a
    dwa_in = dw_mm(a2, dp1, D, Wa_in, T, "attn_in_dw")
    tok = send_grad("attn_in", dwa_in, 1)
    (da2,) = _mm(dp1, wa_in, "nt", [F32], N=D, K=Wa_in, bn=bnD, bk=_pick(Wa_in, 2 * bk_max), name="attn_in_dx",
                 dep=tok, **mmT)
    dX2, dmo0, acc = _normmod_bwd(X2, da2, dX3, True, g_mix1, pack(1, 0, 1), (mo0, gates(0, 5)), R=T, L=L, tr=tr,
                                  name="normmod_mix1_bwd")
    dmod_lat[1][0], dmod_lat[1][1], d_gmix1, dmod_lat[0][5] = acc[0, 0], acc[0, 1], acc[0, 2] + acc[1, 2], acc[0, 3]
    dmod_ctx[1][0], dmod_ctx[1][1], dmod_ctx[0][5] = acc[1, 0], acc[1, 1], acc[1, 3]

    da1 = mlp_bwd(dmo0, a1, u0, r0, 0, mmT, "0")
    dX1, dro0, acc = _normmod_bwd(X1, da1, dX2, False, g_mlp0, pack(0, 3, 4), (ro0, gates(0, 2)), R=T, L=L, tr=tr,
                                  name="normmod_mlp0_bwd")
    dmod_lat[0][3], dmod_lat[0][4], d_gmlp0, dmod_lat[0][2] = acc[0, 0], acc[0, 1], acc[0, 2] + acc[1, 2], acc[0, 3]
    dmod_ctx[0][3], dmod_ctx[0][4], dmod_ctx[0][2] = acc[1, 0], acc[1, 1], acc[1, 3]

    dwr_out = dw_mm(z0, dro0, Wv_r, D, T, "ret_out_dw")
    tok = send_grad("ret_out", dwr_out, 0)
    (dz0,) = _mm(dro0, wr_out, "nt", [BF16], N=Wv_r, K=D, bn=_pick(Wv_r, MM_BN), bk=D, name="ret_out_dx", dep=tok, **mmT)
    do0, dg0 = _readout_bwd(dz0, o0, vg0, T=T, L=L, H=RH, tr=tr_wide, name="ret_readout_bwd")
    dq_f, dk_f, dv_f, dlg_f = _ret_bwd(qk0, vg0, do0, st_f, lg_f, T=T, L=L, H=RH, rev=False, name="ret_scan_f_bwd")
    dq_b, dk_b, dv_b, dlg_b = _ret_bwd(qk0, vg0, do0, st_b, lg_b, T=T, L=L, H=RH, rev=True, name="ret_scan_b_bwd")
    dp0 = _ret_assemble(dq_f, dq_b, dk_f, dk_b, dv_f, dv_b, dg0, rcos, rsin, T=T, L=L, H=RH, tr=tr_wide,
                        name="ret_assemble")
    Wr_in = 2 * Wq_r + 2 * Wv_r
    dwr_in = dw_mm(a0, dp0, D, Wr_in, T, "ret_in_dw")
    tok = send_grad("ret_in", dwr_in, 1)
    (da0,) = _mm(dp0, wr_in, "nt", [F32], N=D, K=Wr_in, bn=bnD, bk=_pick(Wr_in, bk_max), name="ret_in_dx", dep=tok,
                 **mmT)
    dX0, acc = _normmod_bwd(X0, da0, dX1, False, g_mix0, pack(0, 0, 1), None, R=T, L=L, tr=tr, name="normmod_mix0_bwd")
    dmod_lat[0][0], dmod_lat[0][1], d_gmix0 = acc[0, 0], acc[0, 1], acc[0, 2] + acc[1, 2]
    dmod_ctx[0][0], dmod_ctx[0][1] = acc[1, 0], acc[1, 1]
    grad_x = dX0[:L][None]

    wmv = {"ret_in": (ret_w_in, m_ret_w_in, v_ret_w_in, 0, "ret_w_in"),
           "ret_out": (ret_w_out, m_ret_w_out, v_ret_w_out, 0, "ret_w_out"),
           "attn_in": (attn_w_in, m_attn_w_in, v_attn_w_in, 0, "attn_w_in"),
           "attn_out": (attn_w_out, m_attn_w_out, v_attn_w_out, 0, "attn_w_out"),
           "w1_0": (mlp_w1, m_mlp_w1, v_mlp_w1, 0, "mlp_w1"), "w1_1": (mlp_w1, m_mlp_w1, v_mlp_w1, 1, "mlp_w1"),
           "w2_0": (mlp_w2, m_mlp_w2, v_mlp_w2, 0, "mlp_w2"), "w2_1": (mlp_w2, m_mlp_w2, v_mlp_w2, 1, "mlp_w2")}
    big = {}

    def finish_grad(entry, after):
        key, axis, ssem, rsem, dw_thru, land = entry
        dw_done, land_done = _scatter_wait(ssem, rsem, dw_thru, land, axis, after, name=f"rs_wait_{key}")
        w_, m_, v_, layer, out_name = wmv[key]
        big[out_name] = _adamw_sharded(w_, m_, v_, layer, dw_done, land_done, axis, me_arr, big.get(out_name),
                                       name=f"adamw_{key}")
        return big[out_name][0]

    after = dX0
    for entry in pending[:-1]:
        after = finish_grad(entry, after)

    misc = jnp.zeros((D,), F32)
    misc = misc.at[0:RH].set(dlg_f[:, 0, 0]).at[RH:2 * RH].set(dlg_b[:, 0, 0])
    misc = misc.at[2 * RH:2 * RH + Hq].set(dsink_acc[:, :G, 0].reshape(Hq)).at[2 * RH + Hq].set(loss_part)
    rows = ([dmod_lat[i][k] for i in range(2) for k in range(6)] + [dmod_ctx[i][k] for i in range(2) for k in range(6)]
            + [d_gmix0, d_gmix1, d_gmlp0, d_gmlp1, d_gf, misc, zeros_d, zeros_d])
    part = jnp.stack(rows, 0)
    part_all = _all_gather_small(part, name="ag_small_grads")
    tot = _sum_devices(part_all, name="sum_small_grads")

    grad_ada_b = (tot[0:12] + tot[12:24]).reshape(2, 6 * D)
    grad_norm_mix_g, grad_norm_mlp_g, grad_final_norm_g = tot[24:26], tot[26:28], tot[28]
    grad_ret_decay_fwd = (tot[29, 0:RH] * jax.nn.sigmoid(-ret_decay_fwd[0]))[None]
    grad_ret_decay_bwd = (tot[29, RH:2 * RH] * jax.nn.sigmoid(-ret_decay_bwd[0]))[None]
    grad_attn_sink = tot[29, 2 * RH:2 * RH + Hq][None]
    loss = tot[29, 2 * RH + Hq]

    dlat_cols = lax.dynamic_slice_in_dim(part_all[:, 0:12].reshape(N_DEV, 2, 6 * D), me * acols, acols, axis=2)
    dctx_cols = lax.dynamic_slice_in_dim(tot[12:24].reshape(2, 6 * D), me * acols, acols, axis=1)
    dmod16 = jnp.concatenate([dlat_cols.transpose(1, 0, 2), dctx_cols[:, None, :], jnp.zeros((2, 7, acols), F32)], 1)
    cond_t = _silu(c16).T
    g_ada, d_ada, nm_ada, nv_ada, dcond_part = _ada_bwd(cond_t, dmod16, ada_w, m_ada_w, v_ada_w, name="ada_bwd")
    dcond = (dcond_part[0, :, 0] + dcond_part[1, :, 0]).reshape(D // LANES, LANES)
    pad_rows = -(D // LANES) % 8
    dcond_pad = jnp.concatenate([dcond, jnp.zeros((pad_rows, LANES), F32)], 0) if pad_rows else dcond
    dcond_all = _all_gather_small(dcond_pad, name="ag_dcond")
    dcond_tot = _sum_devices(dcond_all, name="sum_dcond")[:D // LANES].reshape(D)
    sg = jax.nn.sigmoid(c_ctx)
    grad_c_ctx = dcond_tot * (sg * (1.0 + c_ctx * (1.0 - sg)))

    small_w = [c_ctx, ada_b, norm_mix_g, norm_mlp_g, ret_decay_fwd, ret_decay_bwd, attn_sink, final_norm_g]
    small_g = [grad_c_ctx, grad_ada_b, grad_norm_mix_g, grad_norm_mlp_g, grad_ret_decay_fwd, grad_ret_decay_bwd,
               grad_attn_sink, grad_final_norm_g]
    small_m = [m_c_ctx, m_ada_b, m_norm_mix_g, m_norm_mlp_g, m_ret_decay_fwd, m_ret_decay_bwd, m_attn_sink,
               m_final_norm_g]
    small_v = [v_c_ctx, v_ada_b, v_norm_mix_g, v_norm_mlp_g, v_ret_decay_fwd, v_ret_decay_bwd, v_attn_sink,
               v_final_norm_g]
    sizes = [w_.size for w_ in small_w]
    total = sum(-(-s // LANES) * LANES for s in sizes)
    total_pad = -(-total // (8 * LANES)) * 8 * LANES

    def flat_pack(ts, fill):
        pieces = []
        for t_ in ts:
            f = t_.reshape(-1).astype(F32)
            pad = -f.size % LANES
            pieces.append(jnp.concatenate([f, jnp.full((pad,), fill, F32)]) if pad else f)
        pieces.append(jnp.full((total_pad - total,), fill, F32))
        return jnp.concatenate(pieces).reshape(total_pad // LANES, LANES)

    d_s, nm_s, nv_s = _adamw_flat(flat_pack(small_w, 0.0), flat_pack(small_g, 0.0), flat_pack(small_m, 0.0),
                                  flat_pack(small_v, 1.0), name="adamw_small")

    def unpack(p):
        flat = p.reshape(-1)
        res, off = [], 0
        for w_, s in zip(small_w, sizes):
            res.append(flat[off:off + s].reshape(w_.shape))
            off += -(-s // LANES) * LANES
        return res

    finish_grad(pending[-1], d_s)
    d_small, nm_small, nv_small = unpack(d_s), unpack(nm_s), unpack(nv_s)
    small_names = ["c_ctx", "ada_b", "norm_mix_g", "norm_mlp_g", "ret_decay_fwd", "ret_decay_bwd", "attn_sink",
                   "final_norm_g"]
    sm = {n: (g_, d_, m_, v_) for n, g_, d_, m_, v_ in zip(small_names, small_g, d_small, nm_small, nv_small)}

    def out4(n):
        if n == "ada_w":
            return g_ada, d_ada, nm_ada, nv_ada
        if n in big:
            return tuple(big[n])
        return sm[n]

    order = ["c_ctx", "ada_w", "ada_b", "norm_mix_g", "norm_mlp_g", "mlp_w1", "mlp_w2", "ret_w_in", "ret_w_out",
             "ret_decay_fwd", "ret_decay_bwd", "attn_w_in", "attn_w_out", "attn_sink", "final_norm_g"]
    quads = [out4(n) for n in order]
    return (loss, grad_x, *[q_[0] for q_ in quads], *[q_[1] for q_ in quads], *[q_[2] for q_ in quads],
            *[q_[3] for q_ in quads])
```

```python
import functools

import jax
import jax.numpy as jnp
from jax import lax
from jax.experimental import pallas as pl
from jax.experimental.pallas import tpu as pltpu

F32 = jnp.float32
BF16 = jnp.bfloat16

N_DEV = 8
NORM_EPS = 1e-6
CHUNK = 128
ATT_HEAD_DIM = 128
GRID_W = 64
ROPE_BASE = 10000.0
NEG_INF = -1e30
ADAM_LR, ADAM_B1, ADAM_B2, ADAM_EPS, ADAM_WD, ADAM_STEP = 0.001, 0.9, 0.999, 1e-08, 0.01, 10

V7X_VMEM_LIMIT_BYTES = 56 * 1024 * 1024
MM_BN = 1024
LANES = 128
MESH = pl.DeviceIdType.MESH

_NN = (((1,), (0,)), ((), ()))
_NT = (((1,), (1,)), ((), ()))
_TN = (((0,), (0,)), ((), ()))


def _dot(a, b, dn=_NN):
    return lax.dot_general(a, b, dn, preferred_element_type=F32)


def _cparams(*sem):
    return pltpu.CompilerParams(dimension_semantics=sem, vmem_limit_bytes=V7X_VMEM_LIMIT_BYTES)


def _pick(n, pref, mult=LANES):
    if n <= pref:
        return n
    best = None
    for d in range(mult, pref + 1, mult):
        if n % d == 0:
            best = d
    assert best is not None, (n, pref)
    return best


def _silu(x):
    return x * jax.nn.sigmoid(x)


def _mm(a, b, mode, out_dtypes, *, M, N, K, bm, bn, bk, name, b_col0=0, epilogue=None, extras=(), dep=None):
    assert M % bm == 0 and N % bn == 0 and K % bk == 0 and b_col0 % bn == 0, (name, M, N, K, bm, bn, bk, b_col0)
    nk = K // bk
    c0 = b_col0 // bn
    if mode == "nn":
        a_spec = pl.BlockSpec((bm, bk), lambda i, j, k: (i, k))
        b_spec = pl.BlockSpec((bk, bn), lambda i, j, k: (k, j + c0))
    elif mode == "nt":
        a_spec = pl.BlockSpec((bm, bk), lambda i, j, k: (i, k))
        b_spec = pl.BlockSpec((bn, bk), lambda i, j, k: (j + c0, k))
    else:
        a_spec = pl.BlockSpec((bk, bm), lambda i, j, k: (k, i))
        b_spec = pl.BlockSpec((bk, bn), lambda i, j, k: (k, j + c0))
    dn = {"nn": _NN, "nt": _NT, "tn": _TN}[mode]
    e_specs = [pl.BlockSpec(bs, (lambda i, j, k, f=f: f(i, j))) for (_, bs, f) in extras]
    ne, no = len(extras), len(out_dtypes)
    nd = 0 if dep is None else 1

    def body(a_ref, b_ref, *rest):
        e_refs, o_refs = rest[:ne], rest[ne + nd:ne + nd + no]
        i, j, k = pl.program_id(0), pl.program_id(1), pl.program_id(2)

        def finish(acc):
            outs = (acc,) if epilogue is None else epilogue(acc, i, j, *[e[...] for e in e_refs])
            for o_ref, o in zip(o_refs, outs):
                o_ref[...] = o.astype(o_ref.dtype)

        p = _dot(a_ref[...], b_ref[...], dn)
        if nk == 1:
            finish(p)
        else:
            acc_ref = rest[-1]

            @pl.when(k == 0)
            def _():
                acc_ref[...] = p

            @pl.when(k > 0)
            def _():
                acc_ref[...] += p

            @pl.when(k == nk - 1)
            def _():
                finish(acc_ref[...])

    outs = pl.pallas_call(
        body, name=name, grid=(M // bm, N // bn, nk),
        in_specs=[a_spec, b_spec] + e_specs + [pl.BlockSpec(memory_space=pl.ANY)] * nd,
        out_specs=[pl.BlockSpec((bm, bn), lambda i, j, k: (i, j)) for _ in out_dtypes],
        out_shape=[jax.ShapeDtypeStruct((M, N), dt) for dt in out_dtypes],
        scratch_shapes=[pltpu.VMEM((bm, bn), F32)] if nk > 1 else [],
        compiler_params=_cparams("parallel", "parallel", "arbitrary"),
    )(a, b, *[e[0] for e in extras], *([dep] if nd else []))
    return outs


def _rowwise(body, rows, vecs, outs, n_acc, *, R, L, tr, name, acc_width=None, dep=None, row0=0):
    assert row0 % tr == 0
    b0 = row0 // tr
    assert R % tr == 0 and L % tr == 0, (name, R, L, tr)
    nl = L // tr
    n_regions = 2 if R > L else 1
    n_rows, n_vecs, n_outs = len(rows), len(vecs), len(outs)
    n_dep = 0 if dep is None else 1
    acc_pad = -(-n_acc // 8) * 8 if n_acc else 0

    in_specs = []
    for (_, w, cb, lat_only) in rows:
        if lat_only:
            in_specs.append(pl.BlockSpec((tr, w), lambda i, cb=cb: (jnp.minimum(i, nl - 1), cb)))
        else:
            in_specs.append(pl.BlockSpec((tr, w), lambda i, cb=cb: (i + b0, cb)))
    for v in vecs:
        in_specs.append(pl.BlockSpec(v.shape, lambda i, nd=v.ndim: (0,) * nd))
    in_specs += [pl.BlockSpec(memory_space=pl.ANY)] * n_dep
    out_specs = [pl.BlockSpec((tr, w), lambda i: (i, 0)) for (w, _) in outs]
    out_shape = [jax.ShapeDtypeStruct((R, w), dt) for (w, dt) in outs]
    if n_acc:
        out_specs.append(pl.BlockSpec((None, acc_pad, acc_width), lambda i: (jnp.where(i >= nl, 1, 0), 0, 0)))
        out_shape.append(jax.ShapeDtypeStruct((n_regions, acc_pad, acc_width), F32))

    def kern(*refs):
        i = pl.program_id(0)
        is_ctx = i >= nl
        ins = [r[...] for r in refs[:n_rows + n_vecs]]
        o_refs = refs[n_rows + n_vecs + n_dep:]
        out_tiles, acc_rows = body(is_ctx, *ins)
        for o_ref, o in zip(o_refs[:n_outs], out_tiles):
            o_ref[...] = o.astype(o_ref.dtype)
        if n_acc:
            acc_ref = o_refs[n_outs]

            @pl.when((i == 0) | (i == nl))
            def _():
                acc_ref[...] = jnp.zeros_like(acc_ref)

            for r, row in enumerate(acc_rows):
                acc_ref[r:r + 1, :] += row

    res = pl.pallas_call(
        kern, name=name, grid=(R // tr,), in_specs=in_specs, out_specs=out_specs, out_shape=out_shape,
        compiler_params=_cparams("arbitrary"),
    )(*[r[0] for r in rows], *vecs, *([dep] if n_dep else []))
    return res


def _colsum(x):
    return jnp.sum(x, axis=0, keepdims=True)


def _rms_stats(x):
    r = lax.rsqrt(jnp.mean(x * x, axis=-1, keepdims=True) + NORM_EPS)
    return x * r, r


def _sel(is_ctx, pk, lat_row, ctx_row):
    return jnp.where(is_ctx, pk[ctx_row:ctx_row + 1, :], pk[lat_row:lat_row + 1, :])


def _normmod(x, g, pk, *, R, L, tr, name, dep=None):
    D = x.shape[-1]

    def body(is_ctx, xt, gv, pkv):
        xh, _ = _rms_stats(xt)
        sh, sc = _sel(is_ctx, pkv, 0, 2), _sel(is_ctx, pkv, 1, 3)
        return ((xh * gv) * (1.0 + sc) + sh,), ()

    return _rowwise(body, [(x, D, 0, False)], [g, pk], [(D, BF16)], 0, R=R, L=L, tr=tr, name=name, dep=dep)[0]


def _normmod_bwd(x_in, da, dx_out, dx_out_lat_only, g, pk, prev, *, R, L, tr, name, row0=0):
    D = x_in.shape[-1]
    has_prev = prev is not None

    def body(is_ctx, *t):
        if has_prev:
            xt, dat, dxo, mp, gv, pkv, gates = t
        else:
            xt, dat, dxo, gv, pkv = t
        xh, r = _rms_stats(xt)
        dat = dat.astype(F32)
        sc = _sel(is_ctx, pkv, 1, 3)
        if dx_out_lat_only:
            dxo = jnp.where(is_ctx, 0.0, dxo)
        dn = dat * (1.0 + sc)
        w = dn * gv
        dxi = dxo + r * (w - xh * jnp.mean(w * xh, axis=-1, keepdims=True))
        accs = [_colsum(dat), _colsum(dat * (xh * gv)), _colsum(dn * xh)]
        outs = [dxi]
        if has_prev:
            gate = _sel(is_ctx, gates, 0, 1)
            outs.append(dxi * gate)
            accs.append(_colsum(dxi * mp.astype(F32)))
        return outs, accs

    rows = [(x_in, D, 0, False), (da, D, 0, False), (dx_out, D, 0, dx_out_lat_only)]
    vecs = [g, pk]
    outs = [(D, F32)]
    if has_prev:
        rows.append((prev[0], D, 0, False))
        vecs.append(prev[1])
        outs.append((D, BF16))
    return _rowwise(body, rows, vecs, outs, 4 if has_prev else 3, R=R, L=L, tr=tr, name=name, acc_width=D, row0=row0)


def _loss_head(x4, target, m_prev, gf, gate, *, L, tr, name):
    D = x4.shape[-1]

    def body(is_ctx, xt, tg, mp, gfv, gatev):
        xh, r = _rms_stats(xt)
        e = xh * gfv - tg
        dy = e * (1.0 / D)
        w = dy * gfv
        dx = r * (w - xh * jnp.mean(w * xh, axis=-1, keepdims=True))
        accs = [_colsum(e * e) * (0.5 / D), _colsum(dy * xh), _colsum(dx * mp.astype(F32))]
        return (dx, dx * gatev), accs

    return _rowwise(body, [(x4, D, 0, False), (target, D, 0, False), (m_prev, D, 0, False)], [gf, gate],
                    [(D, F32), (D, BF16)], 3, R=L, L=L, tr=tr, name=name, acc_width=D)


RET_CHUNK = 2 * LANES
RET_HEADS_PER_STEP = 4


def _decays(lgh, rev):
    C = RET_CHUNK
    ii = lax.broadcasted_iota(jnp.int32, (C, C), 0)
    jj = lax.broadcasted_iota(jnp.int32, (C, C), 1)
    ri = lax.broadcasted_iota(jnp.int32, (C, 1), 0).astype(F32)
    diff = (jj - ii if rev else ii - jj)
    amat = jnp.where(diff >= 0, jnp.exp(lgh * jnp.maximum(diff, 0).astype(F32)), 0.0)
    pos = (C - ri) if rev else (ri + 1.0)
    bq = jnp.exp(lgh * pos)
    bk = jnp.exp(lgh * (C - pos))
    return amat, bq, bk, pos


def _ret_geometry(T, L, H, rev, backward):
    C = RET_CHUNK
    assert T % C == 0 and L % C == 0, (T, L)
    nT, nL = T // C, L // C
    hb = RET_HEADS_PER_STEP if H % RET_HEADS_PER_STEP == 0 else 1

    def step(s):
        return (nT - 1 - s) if backward else s

    def chunk(s):
        s = step(s)
        return (nT - 1 - s) if rev else (s + nL) % nT

    return C, nT, hb, chunk, step


def _ret_fwd(qk, vg, lg, other, *, T, L, H, rev, name):
    dk, dv = 2 * LANES, 4 * LANES
    C, nT, hb, chunk, step = _ret_geometry(T, L, H, rev, False)
    n_other = 0 if other is None else 1

    def body(lg_ref, q_ref, k_ref, v_ref, *rest):
        o_ref, st_ref, s_scr = rest[n_other:]
        hg, s = pl.program_id(0), pl.program_id(1)

        @pl.when(s == 0)
        def _():
            s_scr[...] = jnp.zeros_like(s_scr)

        for hh in range(hb):
            lgh = lg_ref[0, hg * hb + hh]
            amat, bq, bk, _ = _decays(lgh, rev)
            q, k = q_ref[:, hh * dk:(hh + 1) * dk], k_ref[:, hh * dk:(hh + 1) * dk]
            v = v_ref[:, hh * dv:(hh + 1) * dv]
            stb = s_scr[hh].astype(BF16)
            st_ref[hh] = stb
            scores = _dot(q, k, _NT) * amat
            o = _dot(scores.astype(BF16), v) + _dot(q, stb) * bq
            if n_other:
                o = rest[0][:, hh * dv:(hh + 1) * dv] + o
            o_ref[:, hh * dv:(hh + 1) * dv] = o
            kd = (k.astype(F32) * bk).astype(BF16)
            s_scr[hh] = s_scr[hh] * jnp.exp(lgh * C) + _dot(kd, v, _TN)

    vspec = pl.BlockSpec((C, hb * dv), lambda h, s: (chunk(s), h))
    return pl.pallas_call(
        body, name=name, grid=(H // hb, nT),
        in_specs=[pl.BlockSpec(memory_space=pltpu.SMEM),
                  pl.BlockSpec((C, hb * dk), lambda h, s: (chunk(s), h)),
                  pl.BlockSpec((C, hb * dk), lambda h, s: (chunk(s), H // hb + h)), vspec] + [vspec] * n_other,
        out_specs=[vspec, pl.BlockSpec((hb, None, dk, dv), lambda h, s: (h, s, 0, 0))],
        out_shape=[jax.ShapeDtypeStruct((T, H * dv), F32), jax.ShapeDtypeStruct((H, nT, dk, dv), BF16)],
        scratch_shapes=[pltpu.VMEM((hb, dk, dv), F32)],
        compiler_params=_cparams("parallel", "arbitrary"),
    )(lg, qk, qk, vg, *([other] if n_other else []))


def _ret_bwd(qk, vg, do, states, lg, *, T, L, H, rev, name):
    dk, dv = 2 * LANES, 4 * LANES
    C, nT, hb, chunk, step = _ret_geometry(T, L, H, rev, True)

    def body(lg_ref, q_ref, k_ref, v_ref, do_ref, st_ref, dq_ref, dk_ref, dv_ref, dlg_ref, ds_scr):
        hg, s = pl.program_id(0), pl.program_id(1)

        @pl.when(s == 0)
        def _():
            ds_scr[...] = jnp.zeros_like(ds_scr)
            dlg_ref[...] = jnp.zeros_like(dlg_ref)

        for hh in range(hb):
            lgh = lg_ref[0, hg * hb + hh]
            amat, bq, bk, pos = _decays(lgh, rev)
            ksl, vsl = slice(hh * dk, (hh + 1) * dk), slice(hh * dv, (hh + 1) * dv)
            q, k, v, dob = q_ref[:, ksl], k_ref[:, ksl], v_ref[:, vsl], do_ref[:, vsl]
            stb = st_ref[hh]
            ds_new = ds_scr[hh]
            dsb = ds_new.astype(BF16)
            qf, kf = q.astype(F32), k.astype(F32)
            scores = (_dot(q, k, _NT) * amat).astype(BF16)
            dqk = (_dot(dob, v, _NT) * amat).astype(BF16)
            dq = _dot(dqk, k) + _dot(dob, stb, _NT) * bq
            dkk = _dot(dqk, q, _TN) + _dot(v, dsb, _NT) * bk
            kd = (kf * bk).astype(BF16)
            dvv = _dot(scores, dob, _TN) + _dot(kd, dsb)
            dod = (dob.astype(F32) * bq).astype(BF16)
            ds_prev = ds_new * jnp.exp(lgh * C) + _dot(q, dod, _TN)
            ds_scr[hh] = ds_prev
            dq_ref[:, ksl] = dq.astype(dq_ref.dtype)
            dk_ref[:, ksl] = dkk.astype(dk_ref.dtype)
            dv_ref[:, vsl] = dvv.astype(dv_ref.dtype)
            dlg = (jnp.sum(pos * jnp.sum(qf * dq - kf * dkk, axis=-1, keepdims=True))
                   + C * jnp.sum(ds_prev * stb.astype(F32)))
            dlg_ref[hh] += dlg

    qspec = pl.BlockSpec((C, hb * dk), lambda h, s: (chunk(s), h))
    vspec = pl.BlockSpec((C, hb * dv), lambda h, s: (chunk(s), h))
    return pl.pallas_call(
        body, name=name, grid=(H // hb, nT),
        in_specs=[pl.BlockSpec(memory_space=pltpu.SMEM), qspec,
                  pl.BlockSpec((C, hb * dk), lambda h, s: (chunk(s), H // hb + h)), vspec, vspec,
                  pl.BlockSpec((hb, None, dk, dv), lambda h, s: (h, step(s), 0, 0))],
        out_specs=[qspec, qspec, vspec, pl.BlockSpec((hb, 8, LANES), lambda h, s: (h, 0, 0))],
        out_shape=[jax.ShapeDtypeStruct((T, H * dk), BF16), jax.ShapeDtypeStruct((T, H * dk), BF16),
                   jax.ShapeDtypeStruct((T, H * dv), BF16), jax.ShapeDtypeStruct((H, 8, LANES), F32)],
        scratch_shapes=[pltpu.VMEM((hb, dk, dv), F32)],
        compiler_params=_cparams("parallel", "arbitrary"),
    )(lg, qk, qk, vg, do, states)


def _readout(o, vg, *, T, L, H, tr, name, dep=None):
    dv = 4 * LANES
    W = H * dv

    def body(is_ctx, o, g):
        parts = []
        for h in range(H):
            oh = o[:, h * dv:(h + 1) * dv]
            parts.append(oh * lax.rsqrt(jnp.mean(oh * oh, axis=-1, keepdims=True) + NORM_EPS))
        y = jnp.concatenate(parts, axis=1)
        return (_silu(g.astype(F32)) * y,), ()

    return _rowwise(body, [(o, W, 0, False), (vg, W, 1, False)], [], [(W, BF16)], 0,
                    R=T, L=L, tr=tr, name=name, dep=dep)[0]


def _readout_bwd(dz, o, vg, *, T, L, H, tr, name):
    dv = 4 * LANES
    W = H * dv

    def body(is_ctx, dzt, o, g):
        gf = g.astype(F32)
        sg = jax.nn.sigmoid(gf)
        dzf = dzt.astype(F32)
        dy = dzf * (gf * sg)
        ys, dos = [], []
        for h in range(H):
            sl = slice(h * dv, (h + 1) * dv)
            oh, dyh = o[:, sl], dy[:, sl]
            r = lax.rsqrt(jnp.mean(oh * oh, axis=-1, keepdims=True) + NORM_EPS)
            yh = oh * r
            ys.append(yh)
            dos.append(r * (dyh - yh * jnp.mean(dyh * yh, axis=-1, keepdims=True)))
        y = jnp.concatenate(ys, axis=1)
        dg = dzf * y * (sg * (1.0 + gf * (1.0 - sg)))
        return (jnp.concatenate(dos, axis=1), dg), ()

    return _rowwise(body, [(dz, W, 0, False), (o, W, 0, False), (vg, W, 1, False)], [],
                    [(W, BF16), (W, BF16)], 0, R=T, L=L, tr=tr, name=name)


def _ret_assemble(dq_f, dq_b, dk_f, dk_b, dv_f, dv_b, dg, cos, sin, *, T, L, H, tr, name):
    dk, dv = 2 * LANES, 4 * LANES
    Wq, Wv = H * dk, H * dv
    kscale = float(dk) ** -0.5

    def unrope(d, c, s_, scale):
        parts = []
        for h in range(H):
            d1, d2 = d[:, h * dk:h * dk + LANES], d[:, h * dk + LANES:(h + 1) * dk]
            parts += [(d1 * c + d2 * s_) * scale, (d2 * c - d1 * s_) * scale]
        return jnp.concatenate(parts, axis=1)

    def body(is_ctx, qf, qb, kf, kb, vf, vb, g, c, s_):
        add = lambda a, b: a.astype(F32) + b.astype(F32)
        dq = unrope(add(qf, qb), c, s_, 1.0)
        dkk = unrope(add(kf, kb), c, s_, kscale)
        return (jnp.concatenate([dq.astype(BF16), dkk.astype(BF16), add(vf, vb).astype(BF16), g], axis=1),), ()

    rows = [(dq_f, Wq, 0, False), (dq_b, Wq, 0, False), (dk_f, Wq, 0, False), (dk_b, Wq, 0, False),
            (dv_f, Wv, 0, False), (dv_b, Wv, 0, False), (dg, Wv, 0, False),
            (cos, LANES, 0, False), (sin, LANES, 0, False)]
    return _rowwise(body, rows, [], [(2 * Wq + 2 * Wv, BF16)], 0, R=T, L=L, tr=tr, name=name)[0]


def _swap32(x):
    n = x.shape[-1]
    lane = lax.broadcasted_iota(jnp.int32, x.shape, x.ndim - 1)
    return jnp.where(lane % 64 < 32, pltpu.roll(x, n - 32, x.ndim - 1), pltpu.roll(x, 32, x.ndim - 1))


ATT_Q_BLOCKS = 4


def _stack_heads_at(ref, rows, G):
    return jnp.concatenate([ref[rows, g * LANES:(g + 1) * LANES] for g in range(G)], axis=0)


def _stack_columns_at(ref, rows, G):
    return jnp.concatenate([ref[rows, g:g + 1] for g in range(G)], axis=0)


def _sink_column(sink_ref, h, G):
    return jnp.concatenate([jnp.full((CHUNK, 1), sink_ref[0, h * G + g], F32) for g in range(G)], axis=0)


def _key_mask(n, nb, CTX, G):
    W = 3 * CHUNK + CTX
    ii = lax.broadcasted_iota(jnp.int32, (G * CHUNK, W), 0) & (CHUNK - 1)
    col = lax.broadcasted_iota(jnp.int32, (G * CHUNK, W), 1)
    is_prev = col < CHUNK
    is_next = (col >= 2 * CHUNK) & (col < 3 * CHUNK)
    prev_ok = is_prev & (col >= ii) & (n > 0)
    next_ok = is_next & ((col - 2 * CHUNK) <= ii) & (n < nb - 1)
    return prev_ok | next_ok | jnp.logical_not(is_prev | is_next)


def _attn_geometry(L, CTX):
    nb = L // CHUNK
    QB = ATT_Q_BLOCKS if nb % ATT_Q_BLOCKS == 0 else 1

    def blk(j):
        return pl.BlockSpec((CHUNK, LANES), lambda h, m: (jnp.clip(m * QB + j - 1, 0, nb - 1), h))

    kvs = [blk(j) for j in range(QB + 2)] + [pl.BlockSpec((CTX, LANES), lambda h, m: (L // CTX, h))]
    return nb, QB, kvs


def _attn_fwd(q, k, v, sink, *, L, CTX, Hkv, G, name):
    scale = float(ATT_HEAD_DIM) ** -0.5
    nb, QB, kvs = _attn_geometry(L, CTX)
    nkv = QB + 3

    def body(sink_ref, q_ref, *rest):
        kb, vb, (o_ref, lse_ref) = rest[:nkv], rest[nkv:2 * nkv], rest[2 * nkv:]
        h, m_ = pl.program_id(0), pl.program_id(1)
        sk = _sink_column(sink_ref, h, G)
        for sub in range(QB):
            rows = slice(sub * CHUNK, (sub + 1) * CHUNK)
            qs = _stack_heads_at(q_ref, rows, G)
            kall = jnp.concatenate([kb[sub + j][...] for j in range(3)] + [kb[-1][...]], axis=0)
            vall = jnp.concatenate([vb[sub + j][...] for j in range(3)] + [vb[-1][...]], axis=0)
            s_ = jnp.where(_key_mask(m_ * QB + sub, nb, CTX, G), _dot(qs, kall, _NT) * scale, NEG_INF)
            m = jnp.maximum(jnp.max(s_, axis=-1, keepdims=True), sk)
            p = jnp.exp(s_ - m)
            den = jnp.sum(p, axis=-1, keepdims=True) + jnp.exp(sk - m)
            o = _dot(p.astype(BF16), vall) / den
            lse = m + jnp.log(den)
            for g in range(G):
                o_ref[rows, g * LANES:(g + 1) * LANES] = o[g * CHUNK:(g + 1) * CHUNK].astype(o_ref.dtype)
                lse_ref[rows, g:g + 1] = lse[g * CHUNK:(g + 1) * CHUNK]

    qspec = pl.BlockSpec((QB * CHUNK, G * LANES), lambda h, m: (m, h))
    return pl.pallas_call(
        body, name=name, grid=(Hkv, nb // QB),
        in_specs=[pl.BlockSpec(memory_space=pltpu.SMEM), qspec] + kvs + kvs,
        out_specs=[qspec, pl.BlockSpec((None, QB * CHUNK, G), lambda h, m: (h, m, 0))],
        out_shape=[jax.ShapeDtypeStruct((L, Hkv * G * LANES), BF16), jax.ShapeDtypeStruct((Hkv, L, G), F32)],
        compiler_params=_cparams("parallel", "parallel"),
    )(sink, q, *([k] * nkv), *([v] * nkv))


def _attn_bwd(q, k, v, do, lse, sink, *, L, CTX, Hkv, G, name):
    scale = float(ATT_HEAD_DIM) ** -0.5
    nb, QB, kvs = _attn_geometry(L, CTX)
    nkv = QB + 3
    qspec = pl.BlockSpec((QB * CHUNK, G * LANES), lambda h, m: (m, h))
    rowspec = pl.BlockSpec((None, QB * CHUNK, G), lambda h, m: (h, m, 0))
    colspec = lambda rows: pl.BlockSpec((rows, LANES), lambda h, m: (0, h))

    def body(sink_ref, q_ref, do_ref, lse_ref, *rest):
        kb, vb = rest[:nkv], rest[nkv:2 * nkv]
        dq_ref, dk_ref, dv_ref, dkx_ref, dvx_ref, dsk_ref = rest[2 * nkv:]
        h, m_ = pl.program_id(0), pl.program_id(1)

        @pl.when(m_ == 0)
        def _():
            for r in (dk_ref, dv_ref, dkx_ref, dvx_ref, dsk_ref):
                r[...] = jnp.zeros_like(r)

        sk = _sink_column(sink_ref, h, G)
        for sub in range(QB):
            n = m_ * QB + sub
            rows = slice(sub * CHUNK, (sub + 1) * CHUNK)
            qs, dos = _stack_heads_at(q_ref, rows, G), _stack_heads_at(do_ref, rows, G)
            kall = jnp.concatenate([kb[sub + j][...] for j in range(3)] + [kb[-1][...]], axis=0)
            vall = jnp.concatenate([vb[sub + j][...] for j in range(3)] + [vb[-1][...]], axis=0)
            lse_c = _stack_columns_at(lse_ref, rows, G)
            p = jnp.where(_key_mask(n, nb, CTX, G), jnp.exp(_dot(qs, kall, _NT) * scale - lse_c), 0.0)
            dp = _dot(dos, vall, _NT)
            delta = jnp.sum(p * dp, axis=-1, keepdims=True)
            ds = (p * (dp - delta) * scale).astype(BF16)
            dq = _dot(ds, kall)
            dk_all = _dot(ds, qs, _TN)
            dv_all = _dot(p.astype(BF16), dos, _TN)
            for g in range(G):
                dq_ref[rows, g * LANES:(g + 1) * LANES] = dq[g * CHUNK:(g + 1) * CHUNK]
            for part, blk in enumerate((jnp.maximum(n - 1, 0), n, jnp.minimum(n + 1, nb - 1))):
                krows = pl.ds(pl.multiple_of(blk * CHUNK, CHUNK), CHUNK)
                dk_ref[krows, :] += dk_all[part * CHUNK:(part + 1) * CHUNK]
                dv_ref[krows, :] += dv_all[part * CHUNK:(part + 1) * CHUNK]
            dkx_ref[...] += dk_all[3 * CHUNK:]
            dvx_ref[...] += dv_all[3 * CHUNK:]
            dsink = -jnp.exp(sk - lse_c) * delta
            for g in range(G):
                dsk_ref[g:g + 1, :] += jnp.sum(dsink[g * CHUNK:(g + 1) * CHUNK])

    return pl.pallas_call(
        body, name=name, grid=(Hkv, nb // QB),
        in_specs=[pl.BlockSpec(memory_space=pltpu.SMEM), qspec, qspec, rowspec] + kvs + kvs,
        out_specs=[qspec, colspec(L), colspec(L), colspec(CTX), colspec(CTX),
                   pl.BlockSpec((None, 8, LANES), lambda h, m: (h, 0, 0))],
        out_shape=[jax.ShapeDtypeStruct((L, Hkv * G * LANES), F32),
                   jax.ShapeDtypeStruct((L, Hkv * LANES), F32), jax.ShapeDtypeStruct((L, Hkv * LANES), F32),
                   jax.ShapeDtypeStruct((CTX, Hkv * LANES), F32), jax.ShapeDtypeStruct((CTX, Hkv * LANES), F32),
                   jax.ShapeDtypeStruct((Hkv, 8, LANES), F32)],
        compiler_params=_cparams("parallel", "arbitrary"),
    )(sink, q, do, lse, *([k] * nkv), *([v] * nkv))


def _attn_assemble(dq, dk_lat, dv_lat, dk_ctx, dv_ctx, cos, sin, *, T, L, CTX, Hq, Hkv, tr, name):
    Wq, Wk = Hq * LANES, Hkv * LANES
    ctx_blocks = CTX // tr
    nl = L // tr

    def unrope(d, c, s_, heads):
        return d * jnp.tile(c, (1, heads)) + _swap32(d * jnp.tile(s_, (1, heads)))

    def body(is_ctx, dqt, dkl, dvl, dkc, dvc, c, s_):
        dq_ = jnp.where(is_ctx, 0.0, unrope(dqt, c, s_, Hq))
        dk_ = unrope(jnp.where(is_ctx, dkc, dkl), c, s_, Hkv)
        dv_ = jnp.where(is_ctx, dvc, dvl)
        return (jnp.concatenate([dq_, dk_, dv_], axis=1),), ()

    def ctx_map(i):
        return (jnp.clip(i - nl, 0, ctx_blocks - 1), 0)

    assert T % tr == 0 and L % tr == 0 and CTX % tr == 0
    in_specs = [pl.BlockSpec((tr, Wq), lambda i: (jnp.minimum(i, nl - 1), 0)),
                pl.BlockSpec((tr, Wk), lambda i: (jnp.minimum(i, nl - 1), 0)),
                pl.BlockSpec((tr, Wk), lambda i: (jnp.minimum(i, nl - 1), 0)),
                pl.BlockSpec((tr, Wk), ctx_map), pl.BlockSpec((tr, Wk), ctx_map),
                pl.BlockSpec((tr, LANES), lambda i: (i, 0)), pl.BlockSpec((tr, LANES), lambda i: (i, 0))]

    def kern(dq_r, dkl_r, dvl_r, dkc_r, dvc_r, c_r, s_r, o_ref):
        is_ctx = pl.program_id(0) >= nl
        (out,), _ = body(is_ctx, dq_r[...], dkl_r[...], dvl_r[...], dkc_r[...], dvc_r[...], c_r[...], s_r[...])
        o_ref[...] = out.astype(o_ref.dtype)

    return pl.pallas_call(
        kern, name=name, grid=(T // tr,), in_specs=in_specs,
        out_specs=pl.BlockSpec((tr, Wq + 2 * Wk), lambda i: (i, 0)),
        out_shape=jax.ShapeDtypeStruct((T, Wq + 2 * Wk), BF16),
        compiler_params=_cparams("parallel"),
    )(dq, dk_lat, dv_lat, dk_ctx, dv_ctx, cos, sin)


def _my_place():
    x, y, c = lax.axis_index("x"), lax.axis_index("y"), lax.axis_index("c")
    return x, y, c


def _all_gather_small(v, *, name):
    R, C = v.shape

    def body(x_ref, out_ref, send_sems, recv_sems, local_sem):
        x, y, c = _my_place()
        me, sibling = (x, y, c), (x, y, 1 - c)
        chips = [(1 - x, y), (x, 1 - y), (1 - x, 1 - y)]

        def slot(px, py, pc):
            return out_ref.at[4 * px + 2 * py + pc]

        def copy(k, block, to, src=None):
            return pltpu.make_async_remote_copy(
                src_ref=slot(*block) if src is None else src, dst_ref=slot(*block),
                send_sem=send_sems.at[k], recv_sem=recv_sems.at[k], device_id=to, device_id_type=MESH)

        mine = pltpu.make_async_copy(x_ref, slot(*me), local_sem)
        mine.start()
        first = [copy(0, me, sibling, src=x_ref)]
        first += [copy(1 + j, me, (*chip, c), src=x_ref) for j, chip in enumerate(chips)]
        for cp in first:
            cp.start()
        passed = [copy(4 + j, (*chip, c), sibling) for j, chip in enumerate(chips)]
        for j, chip in enumerate(chips):
            copy(1 + j, (*chip, c), me).wait_recv()
            passed[j].start()
        copy(0, sibling, me).wait_recv()
        for j, chip in enumerate(chips):
            copy(4 + j, (*chip, 1 - c), me).wait_recv()
        for cp in first + passed:
            cp.wait_send()
        mine.wait()

    return pl.pallas_call(
        body, name=name, out_shape=jax.ShapeDtypeStruct((N_DEV, R, C), v.dtype),
        in_specs=[pl.BlockSpec(memory_space=pltpu.VMEM)], out_specs=pl.BlockSpec(memory_space=pltpu.VMEM),
        scratch_shapes=[pltpu.SemaphoreType.DMA((7,)), pltpu.SemaphoreType.DMA((7,)), pltpu.SemaphoreType.DMA],
    )(v)


_HBM_SPEC = pl.BlockSpec(memory_space=pltpu.HBM)
_SEM_SPEC = pl.BlockSpec(memory_space=pltpu.SEMAPHORE)
_ANY_SPEC = pl.BlockSpec(memory_space=pl.ANY)
_DATAFLOW = pltpu.SideEffectType.DATAFLOW_SIDE_EFFECTING
N_PEERS = N_DEV - 1


def _peer(r):
    x, y, c = _my_place()
    return ((1 - x) if r & 4 else x, (1 - y) if r & 2 else y, (1 - c) if r & 1 else c)


def _index_of(place):
    return 4 * place[0] + 2 * place[1] + place[2]


def _slot(ref, axis, idx, size):
    if axis == 0:
        return ref.at[pl.ds(idx * size, size), :]
    return ref.at[:, pl.ds(idx * size, size)]


def _cast_place(w3, layer, axis, me_arr, dep, *, name):
    Ks, Ns = w3.shape[1], w3.shape[2]
    tr = _pick(Ks, 256, 16)
    nblk = Ks // tr
    full = (Ks * N_DEV, Ns) if axis == 0 else (Ks, Ns * N_DEV)
    if axis == 0:
        out_map = lambda i, me: (me[0] * nblk + i, 0)
    else:
        out_map = lambda i, me: (i, me[0])

    def body(me_ref, w_ref, dep_ref, o_ref):
        o_ref[...] = w_ref[...].astype(BF16)

    return pl.pallas_call(
        body, name=name, out_shape=jax.ShapeDtypeStruct(full, BF16),
        grid_spec=pltpu.PrefetchScalarGridSpec(
            num_scalar_prefetch=1, grid=(nblk,),
            in_specs=[pl.BlockSpec((None, tr, Ns), lambda i, me: (layer, i, 0)), pl.BlockSpec(memory_space=pl.ANY)],
            out_specs=pl.BlockSpec((tr, Ns), out_map)),
        compiler_params=_cparams("parallel"),
    )(me_arr, w3, dep)


AG_FIRST = 4
AG_CHIPS = 3


def _sibling():
    x, y, c = _my_place()
    return (x, y, 1 - c)


def _chip_peer(j, same_core=True):
    x, y, c = _my_place()
    px = (1 - x) if j in (0, 2) else x
    py = (1 - y) if j in (1, 2) else y
    return (px, py, c if same_core else 1 - c)


def _gather_start(lands, axes, after, *, name):
    nt = len(lands)
    sizes = [l.shape[a] // N_DEV for l, a in zip(lands, axes)]

    def body(*refs):
        ins, send_sems, recv_sems, token = refs[:nt], refs[nt + 1], refs[nt + 2], refs[-1]
        my_idx = _index_of(_my_place())
        for t in range(nt):
            mine = _slot(ins[t], axes[t], my_idx, sizes[t])
            for k, to in enumerate([_sibling()] + [_chip_peer(j) for j in range(AG_CHIPS)]):
                pltpu.make_async_remote_copy(src_ref=mine, dst_ref=mine, send_sem=send_sems.at[t * AG_FIRST + k],
                                             recv_sem=recv_sems.at[t * AG_FIRST + k], device_id=to,
                                             device_id_type=MESH).start()
        token[...] = jnp.zeros_like(token)

    res = pl.pallas_call(
        body, name=name,
        out_shape=(pltpu.SemaphoreType.DMA((nt * AG_FIRST,)), pltpu.SemaphoreType.DMA((nt * AG_FIRST,)),
                   *[pltpu.HBM(l.shape, l.dtype) for l in lands], jax.ShapeDtypeStruct((8, LANES), F32)),
        in_specs=[_HBM_SPEC] * nt + [_ANY_SPEC],
        out_specs=(_SEM_SPEC, _SEM_SPEC, *[_HBM_SPEC] * nt, pl.BlockSpec(memory_space=pltpu.VMEM)),
        input_output_aliases={t: 2 + t for t in range(nt)},
        compiler_params=pltpu.CompilerParams(has_side_effects=_DATAFLOW),
    )(*[pltpu.with_memory_space_constraint(l, pltpu.HBM) for l in lands], after)
    return res[0], res[1], list(res[2:2 + nt]), res[-1]


def _gather_forward(send_a, recv_a, lands, axes, after, *, name):
    nt = len(lands)
    sizes = [l.shape[a] // N_DEV for l, a in zip(lands, axes)]

    def body(*refs):
        ins, send_a, recv_a = refs[:nt], refs[nt], refs[nt + 1]
        send_f, recv_f, token = refs[nt + 3], refs[nt + 4], refs[-1]
        my_idx = _index_of(_my_place())
        for t in range(nt):
            for j in range(AG_CHIPS):
                src_dev = _chip_peer(j)
                arrived = _slot(ins[t], axes[t], _index_of(src_dev), sizes[t])
                pltpu.make_async_remote_copy(
                    src_ref=_slot(ins[t], axes[t], my_idx, sizes[t]), dst_ref=arrived,
                    send_sem=send_a.at[t * AG_FIRST + 1 + j], recv_sem=recv_a.at[t * AG_FIRST + 1 + j],
                    device_id=src_dev, device_id_type=MESH).wait_recv()
                pltpu.make_async_remote_copy(src_ref=arrived, dst_ref=arrived, send_sem=send_f.at[t * AG_CHIPS + j],
                                             recv_sem=recv_f.at[t * AG_CHIPS + j], device_id=_sibling(),
                                             device_id_type=MESH).start()
        token[...] = jnp.zeros_like(token)

    res = pl.pallas_call(
        body, name=name,
        out_shape=(pltpu.SemaphoreType.DMA((nt * AG_CHIPS,)), pltpu.SemaphoreType.DMA((nt * AG_CHIPS,)),
                   *[pltpu.HBM(l.shape, l.dtype) for l in lands], jax.ShapeDtypeStruct((8, LANES), F32)),
        in_specs=[_HBM_SPEC] * nt + [_SEM_SPEC, _SEM_SPEC, _ANY_SPEC],
        out_specs=(_SEM_SPEC, _SEM_SPEC, *[_HBM_SPEC] * nt, pl.BlockSpec(memory_space=pltpu.VMEM)),
        input_output_aliases={t: 2 + t for t in range(nt)},
        compiler_params=pltpu.CompilerParams(has_side_effects=_DATAFLOW),
    )(*lands, send_a, recv_a, after)
    return res[0], res[1], list(res[2:2 + nt]), res[-1]


def _gather_wait(send_a, recv_a, send_f, recv_f, lands, axes, after, *, name):
    nt = len(lands)
    sizes = [l.shape[a] // N_DEV for l, a in zip(lands, axes)]

    def body(*refs):
        ins, send_a, recv_a, send_f, recv_f = refs[:nt], refs[nt], refs[nt + 1], refs[nt + 2], refs[nt + 3]
        my_idx = _index_of(_my_place())
        sib = _sibling()
        for t in range(nt):
            mine = _slot(ins[t], axes[t], my_idx, sizes[t])
            for k, to in enumerate([sib] + [_chip_peer(j) for j in range(AG_CHIPS)]):
                pltpu.make_async_remote_copy(src_ref=mine, dst_ref=mine, send_sem=send_a.at[t * AG_FIRST + k],
                                             recv_sem=recv_a.at[t * AG_FIRST + k], device_id=to,
                                             device_id_type=MESH).wait_send()
            pltpu.make_async_remote_copy(src_ref=mine, dst_ref=_slot(ins[t], axes[t], _index_of(sib), sizes[t]),
                                         send_sem=send_a.at[t * AG_FIRST], recv_sem=recv_a.at[t * AG_FIRST],
                                         device_id=sib, device_id_type=MESH).wait_recv()
            for j in range(AG_CHIPS):
                sent = _slot(ins[t], axes[t], _index_of(_chip_peer(j)), sizes[t])
                got = _slot(ins[t], axes[t], _index_of(_chip_peer(j, same_core=False)), sizes[t])
                cp = pltpu.make_async_remote_copy(src_ref=sent, dst_ref=got, send_sem=send_f.at[t * AG_CHIPS + j],
                                                  recv_sem=recv_f.at[t * AG_CHIPS + j], device_id=sib,
                                                  device_id_type=MESH)
                cp.wait_send()
                cp.wait_recv()

    res = pl.pallas_call(
        body, name=name, out_shape=[pltpu.HBM(l.shape, l.dtype) for l in lands],
        in_specs=[_HBM_SPEC] * nt + [_SEM_SPEC] * 4 + [_ANY_SPEC], out_specs=[_HBM_SPEC] * nt,
        input_output_aliases={t: t for t in range(nt)},
        compiler_params=pltpu.CompilerParams(has_side_effects=_DATAFLOW),
    )(*lands, send_a, recv_a, send_f, recv_f, after)
    return list(res)


def _scatter_start(dw, axis, *, name):
    size = dw.shape[axis] // N_DEV
    land_shape = (N_PEERS, size, dw.shape[1]) if axis == 0 else (N_PEERS, dw.shape[0], size)

    def body(dw_ref, land_ref, send_sems, recv_sems, dw_thru, land_thru, token):
        for r in range(1, N_DEV):
            p = _peer(r)
            pltpu.make_async_remote_copy(src_ref=_slot(dw_ref, axis, _index_of(p), size), dst_ref=land_ref.at[r - 1],
                                         send_sem=send_sems.at[r - 1], recv_sem=recv_sems.at[r - 1], device_id=p,
                                         device_id_type=MESH).start()
        token[...] = jnp.zeros_like(token)

    land = pltpu.with_memory_space_constraint(lax.empty(land_shape, dw.dtype), pltpu.HBM)
    return pl.pallas_call(
        body, name=name,
        out_shape=(pltpu.SemaphoreType.DMA((N_PEERS,)), pltpu.SemaphoreType.DMA((N_PEERS,)),
                   pltpu.HBM(dw.shape, dw.dtype), pltpu.HBM(land_shape, dw.dtype), jax.ShapeDtypeStruct((8, LANES), F32)),
        in_specs=[_HBM_SPEC, _HBM_SPEC],
        out_specs=(_SEM_SPEC, _SEM_SPEC, _HBM_SPEC, _HBM_SPEC, pl.BlockSpec(memory_space=pltpu.VMEM)),
        input_output_aliases={0: 2, 1: 3},
        compiler_params=pltpu.CompilerParams(has_side_effects=_DATAFLOW),
    )(pltpu.with_memory_space_constraint(dw, pltpu.HBM), land)


def _scatter_wait(send_sems, recv_sems, dw, land, axis, after, *, name):
    size = dw.shape[axis] // N_DEV

    def body(dw_ref, land_ref, send_sems, recv_sems, after_ref, dw_thru, land_thru):
        for r in range(1, N_DEV):
            p = _peer(r)
            cp = pltpu.make_async_remote_copy(src_ref=_slot(dw_ref, axis, _index_of(p), size), dst_ref=land_ref.at[r - 1],
                                              send_sem=send_sems.at[r - 1], recv_sem=recv_sems.at[r - 1], device_id=p,
                                              device_id_type=MESH)
            cp.wait_send()
            cp.wait_recv()

    return pl.pallas_call(
        body, name=name, out_shape=(pltpu.HBM(dw.shape, dw.dtype), pltpu.HBM(land.shape, land.dtype)),
        in_specs=[_HBM_SPEC, _HBM_SPEC, _SEM_SPEC, _SEM_SPEC, _ANY_SPEC], out_specs=(_HBM_SPEC, _HBM_SPEC),
        input_output_aliases={0: 0, 1: 1},
        compiler_params=pltpu.CompilerParams(has_side_effects=_DATAFLOW),
    )(dw, land, send_sems, recv_sems, after)


def _adamw_math(w, g, m, v):
    m = ADAM_B1 * m + (1.0 - ADAM_B1) * g
    v = ADAM_B2 * v + (1.0 - ADAM_B2) * (g * g)
    m_hat = m / (1.0 - ADAM_B1 ** ADAM_STEP)
    v_hat = v / (1.0 - ADAM_B2 ** ADAM_STEP)
    delta = -ADAM_LR * (m_hat / (jnp.sqrt(v_hat) + ADAM_EPS) + ADAM_WD * w)
    return delta, m, v


def _adamw_sharded(w, m, v, layer, dw, land, axis, me_arr, prev, *, name):
    nl, Ks, Ns = w.shape
    tr = _pick(Ks, 128, 16)
    nblk = Ks // tr
    if axis == 0:
        own_map = lambda i, me: (me[0] * nblk + i, 0)
    else:
        own_map = lambda i, me: (i, me[0])
    wspec = pl.BlockSpec((None, tr, Ns), lambda i, me: (layer, i, 0))
    n_prev = 0 if prev is None else 4

    def body(me_ref, w_ref, m_ref, v_ref, own_ref, r_ref, *rest):
        g_ref, d_ref, nm_ref, nv_ref = rest[n_prev:]
        g = own_ref[...].astype(F32)
        for r in range(N_PEERS):
            g = g + r_ref[r].astype(F32)
        delta, nm, nv = _adamw_math(w_ref[...], g, m_ref[...], v_ref[...])
        g_ref[...], d_ref[...], nm_ref[...], nv_ref[...] = g, delta, nm, nv

    return pl.pallas_call(
        body, name=name, out_shape=[jax.ShapeDtypeStruct((nl, Ks, Ns), F32)] * 4,
        grid_spec=pltpu.PrefetchScalarGridSpec(
            num_scalar_prefetch=1, grid=(nblk,),
            in_specs=[wspec, wspec, wspec, pl.BlockSpec((tr, Ns), own_map),
                      pl.BlockSpec((N_PEERS, tr, Ns), lambda i, me: (0, i, 0))] + [_ANY_SPEC] * n_prev,
            out_specs=[wspec] * 4),
        input_output_aliases={6 + k: k for k in range(n_prev)},
        compiler_params=_cparams("parallel"),
    )(me_arr, w, m, v, dw, land, *(prev or []))


def _adamw_flat(w, g, m, v, *, name):
    def body(w_ref, g_ref, m_ref, v_ref, d_ref, nm_ref, nv_ref):
        d_ref[...], nm_ref[...], nv_ref[...] = _adamw_math(w_ref[...], g_ref[...], m_ref[...], v_ref[...])

    spec = pl.BlockSpec(memory_space=pltpu.VMEM)
    return pl.pallas_call(body, name=name, in_specs=[spec] * 4, out_specs=[spec] * 3,
                          out_shape=[jax.ShapeDtypeStruct(w.shape, F32)] * 3)(w, g, m, v)


def _sum_devices(a, *, name):
    def body(a_ref, o_ref):
        s = a_ref[0]
        for d in range(1, N_DEV):
            s = s + a_ref[d]
        o_ref[...] = s

    spec = pl.BlockSpec(memory_space=pltpu.VMEM)
    return pl.pallas_call(body, name=name, in_specs=[spec], out_specs=spec,
                          out_shape=jax.ShapeDtypeStruct(a.shape[1:], F32))(a)


def _ada_mods(c16, ada_w, ada_b_cols, *, name):
    nl, D, cols = ada_w.shape
    bn = _pick(cols, 512)

    def body(c_ref, w_ref, b_ref, o_ref):
        cond = _silu(c_ref[...]).astype(BF16)
        o_ref[...] = _dot(cond, w_ref[...].astype(BF16)) + b_ref[...]

    return pl.pallas_call(
        body, name=name, grid=(nl, cols // bn),
        in_specs=[pl.BlockSpec((16, D), lambda l, j: (0, 0)), pl.BlockSpec((None, D, bn), lambda l, j: (l, 0, j)),
                  pl.BlockSpec((None, 1, bn), lambda l, j: (l, 0, j))],
        out_specs=pl.BlockSpec((None, 16, bn), lambda l, j: (l, 0, j)),
        out_shape=jax.ShapeDtypeStruct((nl, 16, cols), F32),
        compiler_params=_cparams("parallel", "parallel"),
    )(c16, ada_w, ada_b_cols)


def _ada_bwd(cond_t, dmod, w, m, v, *, name):
    nl, D, cols = w.shape
    tr = _pick(D, 256, 8)

    def body(ct_ref, dm_ref, w_ref, m_ref, v_ref, g_ref, d_ref, nm_ref, nv_ref, dc_ref):
        ct, dm, wt = ct_ref[...], dm_ref[...], w_ref[...]
        g = ct[:, 0:1] * dm[0:1, :]
        for r in range(1, N_DEV + 1):
            g = g + ct[:, r:r + 1] * dm[r:r + 1, :]
        delta, nm, nv = _adamw_math(wt, g, m_ref[...], v_ref[...])
        g_ref[...], d_ref[...], nm_ref[...], nv_ref[...] = g, delta, nm, nv
        dc_ref[...] = jnp.sum(wt * dm[N_DEV:N_DEV + 1, :], axis=-1, keepdims=True)

    wspec = pl.BlockSpec((None, tr, cols), lambda l, i: (l, i, 0))
    return pl.pallas_call(
        body, name=name, grid=(nl, D // tr),
        in_specs=[pl.BlockSpec((tr, 16), lambda l, i: (i, 0)), pl.BlockSpec((None, 16, cols), lambda l, i: (l, 0, 0)),
                  wspec, wspec, wspec],
        out_specs=[wspec] * 4 + [pl.BlockSpec((None, tr, 1), lambda l, i: (l, i, 0))],
        out_shape=[jax.ShapeDtypeStruct((nl, D, cols), F32)] * 4 + [jax.ShapeDtypeStruct((nl, D, 1), F32)],
        compiler_params=_cparams("parallel", "parallel"),
    )(cond_t, dmod, w, m, v)


def _rope_tables(L, CTX):
    def angles(pos, dim):
        inv_freq = ROPE_BASE ** (-jnp.arange(0, dim, 2, dtype=F32) / dim)
        return pos.astype(F32)[:, None] * inv_freq[None, :]

    def pad(cos, sin):
        return (jnp.concatenate([cos, jnp.ones((CTX, LANES), F32)], 0),
                jnp.concatenate([sin, jnp.zeros((CTX, LANES), F32)], 0))

    ret = angles(jnp.arange(L), 2 * LANES)
    ret_cs = pad(jnp.cos(ret), jnp.sin(ret))
    rows = angles(jnp.arange(L) // GRID_W, ATT_HEAD_DIM // 2)
    cols = angles(jnp.arange(L) % GRID_W, ATT_HEAD_DIM // 2)
    cos = jnp.concatenate([jnp.cos(rows)] * 2 + [jnp.cos(cols)] * 2, axis=1)
    sin = jnp.concatenate([-jnp.sin(rows), jnp.sin(rows), -jnp.sin(cols), jnp.sin(cols)], axis=1)
    return ret_cs, pad(cos, sin)


def kernel(x, c, ctx, c_ctx, ada_w, ada_b, norm_mix_g, norm_mlp_g, mlp_w1, mlp_w2, ret_w_in, ret_w_out, ret_decay_fwd, ret_decay_bwd, attn_w_in, attn_w_out, attn_sink, final_norm_g, loss_target, m_c_ctx, m_ada_w, m_ada_b, m_norm_mix_g, m_norm_mlp_g, m_mlp_w1, m_mlp_w2, m_ret_w_in, m_ret_w_out, m_ret_decay_fwd, m_ret_decay_bwd, m_attn_w_in, m_attn_w_out, m_attn_sink, m_final_norm_g, v_c_ctx, v_ada_w, v_ada_b, v_norm_mix_g, v_norm_mlp_g, v_mlp_w1, v_mlp_w2, v_ret_w_in, v_ret_w_out, v_ret_decay_fwd, v_ret_decay_bwd, v_attn_w_in, v_attn_w_out, v_attn_sink, v_final_norm_g):
    L, D = x.shape[1], x.shape[2]
    CTX = ctx.shape[1]
    T = L + CTX
    RH = ret_decay_fwd.shape[-1]
    assert D == RH * 2 * LANES and ada_w.shape[0] == 2 and ret_w_in.shape[0] == 1 and attn_w_in.shape[0] == 1
    Hq = attn_sink.shape[-1]
    Hkv = (attn_w_in.shape[-1] * N_DEV // ATT_HEAD_DIM - Hq) // 2
    G = Hq // Hkv
    FF = mlp_w1.shape[-1] * N_DEV
    Wq_r, Wv_r = RH * 2 * LANES, RH * 4 * LANES
    acols = ada_w.shape[-1]
    tr = _pick(CTX, 256, 8)
    tr_wide = _pick(CTX, 128, 8)
    bmT = T // 4 if (T % 64 == 0) else T
    bmL = L // 4 if (L % 64 == 0) else L
    x_idx, y_idx, c_idx = lax.axis_index("x"), lax.axis_index("y"), lax.axis_index("c")
    me = 4 * x_idx + 2 * y_idx + c_idx
    me_arr = jnp.reshape(me, (1,)).astype(jnp.int32)

    (rcos, rsin), (acos, asin) = _rope_tables(L, CTX)
    lg_f = jax.nn.log_sigmoid(ret_decay_fwd.astype(F32))
    lg_b = jax.nn.log_sigmoid(ret_decay_bwd.astype(F32))

    c_pad = jnp.concatenate([c.astype(F32), jnp.zeros((7, D), F32)], 0)
    c_all = _all_gather_small(c_pad, name="ag_c")[:, 0, :]
    c16 = jnp.concatenate([c_all, c_ctx[None, :], jnp.zeros((7, D), F32)], 0)
    ada_b_cols = lax.dynamic_slice_in_dim(ada_b, me * acols, acols, axis=1)[:, None, :]
    mods_shard = _ada_mods(c16, ada_w, ada_b_cols, name="ada_mods")
    mods_all = _all_gather_small(mods_shard.reshape(32, acols), name="ag_mods")

    wdefs = {"ret_in": (ret_w_in, 0, 1), "ret_out": (ret_w_out, 0, 0), "w1_0": (mlp_w1, 0, 1), "w2_0": (mlp_w2, 0, 0),
             "attn_in": (attn_w_in, 0, 1), "attn_out": (attn_w_out, 0, 0), "w1_1": (mlp_w1, 1, 1), "w2_1": (mlp_w2, 1, 0)}
    groups = [["ret_in"], ["ret_out", "w1_0", "w2_0"], ["attn_in", "attn_out"], ["w1_1", "w2_1"]]

    placed = {}

    def ag_start(gi, after):
        g_axes = [wdefs[k][2] for k in groups[gi]]
        ssem, rsem, lands, tok_ = _gather_start([placed[k] for k in groups[gi]], g_axes, after, name=f"ag_start{gi}")
        return dict(a=(ssem, rsem), lands=lands, axes=g_axes, gi=gi), tok_

    def ag_forward(g, after):
        fs, fr, g["lands"], tok_ = _gather_forward(*g["a"], g["lands"], g["axes"], after, name=f"ag_forward{g['gi']}")
        g["f"] = (fs, fr)
        return tok_

    def ag_wait(g, after):
        return _gather_wait(*g["a"], *g["f"], g["lands"], g["axes"], after, name=f"ag_wait{g['gi']}")

    placed["ret_in"] = _cast_place(*wdefs["ret_in"], me_arr, mods_all, name="place_ret_in")
    g0, tok = ag_start(0, mods_all)
    last_cast = tok
    for keys in groups[1:]:
        for k in keys:
            last_cast = placed[k] = _cast_place(*wdefs[k], me_arr, last_cast, name=f"place_{k}")
    mods_all = (mods_all + tok[0, 0]).reshape(N_DEV, 2, 16, acols).transpose(1, 2, 0, 3).reshape(2, 16, 6, D)
    mod_lat = lax.dynamic_index_in_dim(mods_all, me, axis=1, keepdims=False)
    mod_ctx = mods_all[:, N_DEV]

    def pack(i, ks, kc):
        return jnp.stack([mod_lat[i, ks], mod_lat[i, kc], mod_ctx[i, ks], mod_ctx[i, kc]], 0)

    def gates(i, k):
        return jnp.stack([mod_lat[i, k], mod_ctx[i, k]], 0)

    def gate_epilogue(gl, gc, x_rows_lat_only):
        def epi(acc, i, j, xt, gv):
            if x_rows_lat_only:
                gate = gv[0:1, :]
            else:
                row = i * acc.shape[0] + lax.broadcasted_iota(jnp.int32, (acc.shape[0], 1), 0)
                gate = jnp.where(row >= L, gv[1:2, :], gv[0:1, :])
            return xt + gate * acc, acc
        return epi

    w1, w2 = {}, {}

    mmT = dict(M=T, bm=bmT)
    mmL = dict(M=L, bm=bmL)

    def bn_of(n, off=0):
        b = MM_BN
        while n % b or off % b:
            b -= LANES
        return b

    X0 = jnp.concatenate([x[0], ctx[0]], axis=0)
    g_mix0, g_mlp0 = norm_mix_g[0:1], norm_mlp_g[0:1]
    g_mix1, g_mlp1 = norm_mix_g[1:2], norm_mlp_g[1:2]
    a0 = _normmod(X0, g_mix0, pack(0, 0, 1), R=T, L=L, tr=tr, name="normmod_mix0", dep=last_cast)
    tok = ag_forward(g0, a0)
    (wr_in,) = ag_wait(g0, tok)
    g1, tok = ag_start(1, wr_in)

    bn_qk = _pick(Wq_r, MM_BN, 2 * LANES)
    nq_blocks = Wq_r // bn_qk
    kscale = float(2 * LANES) ** -0.5

    def rope_epi(acc, i, j, cos, sin):
        parts = []
        for h in range(acc.shape[1] // (2 * LANES)):
            x1 = acc[:, h * 2 * LANES:h * 2 * LANES + LANES]
            x2 = acc[:, h * 2 * LANES + LANES:(h + 1) * 2 * LANES]
            parts += [x1 * cos - x2 * sin, x2 * cos + x1 * sin]
        return (jnp.concatenate(parts, axis=1) * jnp.where(j < nq_blocks, 1.0, kscale),)

    def row_tile(arr, bm):
        return (arr, (bm, LANES), lambda i, j: (i, 0))

    (qk0,) = _mm(a0, wr_in, "nn", [BF16], N=2 * Wq_r, K=D, bn=bn_qk, bk=D, name="ret_qk", epilogue=rope_epi,
                 extras=[row_tile(rcos, bmT), row_tile(rsin, bmT)], dep=tok, **mmT)
    bn_vg = bn_of(2 * Wv_r, 2 * Wq_r)
    (vg0,) = _mm(a0, wr_in, "nn", [BF16], N=2 * Wv_r, K=D, bn=bn_vg, bk=D, name="ret_vg", b_col0=2 * Wq_r, dep=tok,
                 **mmT)

    of, st_f = _ret_fwd(qk0, vg0, lg_f, None, T=T, L=L, H=RH, rev=False, name="ret_scan_f")
    o0, st_b = _ret_fwd(qk0, vg0, lg_b, of, T=T, L=L, H=RH, rev=True, name="ret_scan_b")
    tok = ag_forward(g1, o0)
    z0 = _readout(o0, vg0, T=T, L=L, H=RH, tr=tr_wide, name="ret_readout", dep=tok)
    wr_out, w1[0], w2[0] = ag_wait(g1, z0)
    g2, tok = ag_start(2, wr_out)
    g3, tok = ag_start(3, tok)

    bnD = _pick(D, MM_BN)

    def xtile(arr, bm):
        return (arr, (bm, bnD), lambda i, j: (i, j))

    def gtile(gv):
        return (gv, (2, bnD), lambda i, j: (0, j))

    bk_max = 2048
    X1, ro0 = _mm(z0, wr_out, "nn", [F32, BF16], N=D, K=Wv_r, bn=bnD, bk=_pick(Wv_r, bk_max), name="ret_out",
                  epilogue=gate_epilogue(None, None, False), extras=[xtile(X0, bmT), gtile(gates(0, 2))], dep=tok, **mmT)

    def mlp_fwd(Xin, i, g_mlp, rows, name):
        a = _normmod(Xin, g_mlp, pack(i, 3, 4), R=rows["M"], L=L, tr=tr, name=f"normmod_mlp{name}")

        def relu2(acc, i_, j_):
            u = jnp.maximum(acc, 0.0)
            return u, u * u

        bnF = _pick(FF, MM_BN)
        u, r = _mm(a, w1[i], "nn", [BF16, BF16], N=FF, K=D, bn=bnF, bk=D, name=f"mlp_up{name}", epilogue=relu2, **rows)
        Xout, mo = _mm(r, w2[i], "nn", [F32, BF16], N=D, K=FF, bn=bnD, bk=_pick(FF, bk_max), name=f"mlp_down{name}",
                       epilogue=gate_epilogue(None, None, rows["M"] == L),
                       extras=[xtile(Xin, rows["bm"]), gtile(gates(i, 5))], **rows)
        return a, u, r, Xout, mo

    a1, u0, r0, X2, mo0 = mlp_fwd(X1, 0, g_mlp0, mmT, "0")

    tok = ag_forward(g2, X2)
    a2 = _normmod(X2, g_mix1, pack(1, 0, 1), R=T, L=L, tr=tr, name="normmod_mix1", dep=tok)
    wa_in, wa_out = ag_wait(g2, a2)
    Wq_a, Wk_a = Hq * LANES, Hkv * LANES

    def arope_epi(acc, i, j, cos, sin):
        heads = acc.shape[1] // LANES
        return (acc * jnp.tile(cos, (1, heads)) + _swap32(acc) * jnp.tile(sin, (1, heads)),)

    bn_q = _pick(Wq_a, MM_BN)
    (q1,) = _mm(a2, wa_in, "nn", [BF16], N=Wq_a, K=D, bn=bn_q, bk=D, name="attn_q", epilogue=arope_epi,
                extras=[row_tile(acos, bmL), row_tile(asin, bmL)], **mmL)
    bn_k = bn_of(Wk_a, Wq_a)
    (k1,) = _mm(a2, wa_in, "nn", [BF16], N=Wk_a, K=D, bn=bn_k, bk=D, name="attn_k", b_col0=Wq_a, epilogue=arope_epi,
                extras=[row_tile(acos, bmT), row_tile(asin, bmT)], **mmT)
    bn_v = bn_of(Wk_a, Wq_a + Wk_a)
    (v1,) = _mm(a2, wa_in, "nn", [BF16], N=Wk_a, K=D, bn=bn_v, bk=D, name="attn_v", b_col0=Wq_a + Wk_a, **mmT)
    tok = ag_forward(g3, q1)
    o1, lse = _attn_fwd(q1, k1, v1, attn_sink + tok[0, 0], L=L, CTX=CTX, Hkv=Hkv, G=G, name="attn_fwd")
    w1[1], w2[1] = ag_wait(g3, o1)
    X3, ao = _mm(o1, wa_out, "nn", [F32, BF16], N=D, K=Wq_a, bn=bnD, bk=_pick(Wq_a, 2048), name="attn_out",
                 epilogue=gate_epilogue(None, None, True), extras=[xtile(X2, bmL), gtile(gates(1, 2))], **mmL)
    a3, u1, r1, X4, mo1 = mlp_fwd(X3, 1, g_mlp1, mmL, "1")

    dX4, dmo1, acc_head = _loss_head(X4, loss_target[0], mo1, final_norm_g[None, :], gates(1, 5)[0:1], L=L, tr=tr,
                                     name="loss_head")
    loss_part = jnp.sum(acc_head[0, 0])
    d_gf = acc_head[0, 1]
    zeros_d = jnp.zeros((D,), F32)
    dmod_lat = [[zeros_d] * 6, [zeros_d] * 6]
    dmod_ctx = [[zeros_d] * 6, [zeros_d] * 6]
    dmod_lat[1][5] = acc_head[0, 2]

    def dw_mm(a, b, M, N, K, name):
        return _mm(a, b, "tn", [BF16], M=M, N=N, K=K, bm=_pick(M, 512), bn=_pick(N, MM_BN), bk=K, name=name)[0]

    def mlp_bwd(dmo, a, u, r, i, rows, name):
        Mr = rows["M"]

        def times_2u(acc, i_, j_, ut):
            return (acc * (2.0 * ut.astype(F32)),)

        bnF = _pick(FF, MM_BN)
        dw2 = dw_mm(r, dmo, FF, D, Mr, f"mlp_down_dw{name}")
        tok_ = send_grad(f"w2_{i}", dw2, 0)
        (dh,) = _mm(dmo, w2[i], "nt", [BF16], N=FF, K=D, bn=bnF, bk=D, name=f"mlp_down_dx{name}", epilogue=times_2u,
                    extras=[(u, (rows["bm"], bnF), lambda i_, j_: (i_, j_))], dep=tok_, **rows)
        dw1 = dw_mm(a, dh, D, FF, Mr, f"mlp_up_dw{name}")
        tok_ = send_grad(f"w1_{i}", dw1, 1)
        (da,) = _mm(dh, w1[i], "nt", [BF16], N=D, K=FF, bn=bnD, bk=_pick(FF, bk_max), name=f"mlp_up_dx{name}", dep=tok_,
                    **rows)
        return da

    pending = []

    def send_grad(key, dw, axis):
        ssem, rsem, dw_thru, land, tok_ = _scatter_start(dw, axis, name=f"rs_start_{key}")
        pending.append((key, axis, ssem, rsem, dw_thru, land))
        return tok_

    da3 = mlp_bwd(dmo1, a3, u1, r1, 1, mmL, "1")
    dX3, dao, acc = _normmod_bwd(X3, da3, dX4, False, g_mlp1, pack(1, 3, 4), (ao, gates(1, 2)), R=L, L=L, tr=tr,
                                 name="normmod_mlp1_bwd")
    dmod_lat[1][3], dmod_lat[1][4], d_gmlp1, dmod_lat[1][2] = acc[0, 0], acc[0, 1], acc[0, 2], acc[0, 3]

    dwa_out = dw_mm(o1, dao, Wq_a, D, L, "attn_out_dw")
    tok = send_grad("attn_out", dwa_out, 0)
    (do1,) = _mm(dao, wa_out, "nt", [BF16], N=Wq_a, K=D, bn=_pick(Wq_a, MM_BN), bk=D, name="attn_out_dx", dep=tok, **mmL)
    dq1, dk1, dv1, dkx, dvx, dsink_acc = _attn_bwd(q1, k1, v1, do1, lse, attn_sink, L=L, CTX=CTX, Hkv=Hkv, G=G,
                                                   name="attn_bwd")
    dp1 = _attn_assemble(dq1, dk1, dv1, dkx, dvx, acos, asin, T=T, L=L, CTX=CTX, Hq=Hq, Hkv=Hkv, tr=tr_wide,
                         name="attn_assemble")
    Wa_in = Wq_a + 2 * Wk_a
    dwa_in = dw_mm(a2, dp1, D, Wa_in, T, "attn_in_dw")
    tok = send_grad("attn_in", dwa_in, 1)
    (da2,) = _mm(dp1, wa_in, "nt", [BF16], N=D, K=Wa_in, bn=bnD, bk=_pick(Wa_in, 2 * bk_max), name="attn_in_dx",
                 dep=tok, **mmT)
    dX2, dmo0, acc = _normmod_bwd(X2, da2, dX3, True, g_mix1, pack(1, 0, 1), (mo0, gates(0, 5)), R=T, L=L, tr=tr,
                                  name="normmod_mix1_bwd")
    dmod_lat[1][0], dmod_lat[1][1], d_gmix1, dmod_lat[0][5] = acc[0, 0], acc[0, 1], acc[0, 2] + acc[1, 2], acc[0, 3]
    dmod_ctx[1][0], dmod_ctx[1][1], dmod_ctx[0][5] = acc[1, 0], acc[1, 1], acc[1, 3]

    da1 = mlp_bwd(dmo0, a1, u0, r0, 0, mmT, "0")
    dX1, dro0, acc = _normmod_bwd(X1, da1, dX2, False, g_mlp0, pack(0, 3, 4), (ro0, gates(0, 2)), R=T, L=L, tr=tr,
                                  name="normmod_mlp0_bwd")
    dmod_lat[0][3], dmod_lat[0][4], d_gmlp0, dmod_lat[0][2] = acc[0, 0], acc[0, 1], acc[0, 2] + acc[1, 2], acc[0, 3]
    dmod_ctx[0][3], dmod_ctx[0][4], dmod_ctx[0][2] = acc[1, 0], acc[1, 1], acc[1, 3]

    dwr_out = dw_mm(z0, dro0, Wv_r, D, T, "ret_out_dw")
    tok = send_grad("ret_out", dwr_out, 0)
    (dz0,) = _mm(dro0, wr_out, "nt", [BF16], N=Wv_r, K=D, bn=_pick(Wv_r, MM_BN), bk=D, name="ret_out_dx", dep=tok, **mmT)
    do0, dg0 = _readout_bwd(dz0, o0, vg0, T=T, L=L, H=RH, tr=tr_wide, name="ret_readout_bwd")
    dq_f, dk_f, dv_f, dlg_f = _ret_bwd(qk0, vg0, do0, st_f, lg_f, T=T, L=L, H=RH, rev=False, name="ret_scan_f_bwd")
    dq_b, dk_b, dv_b, dlg_b = _ret_bwd(qk0, vg0, do0, st_b, lg_b, T=T, L=L, H=RH, rev=True, name="ret_scan_b_bwd")
    dp0 = _ret_assemble(dq_f, dq_b, dk_f, dk_b, dv_f, dv_b, dg0, rcos, rsin, T=T, L=L, H=RH, tr=tr_wide,
                        name="ret_assemble")
    Wr_in = 2 * Wq_r + 2 * Wv_r
    dwr_in = dw_mm(a0, dp0, D, Wr_in, T, "ret_in_dw")
    tok = send_grad("ret_in", dwr_in, 1)
    (da0,) = _mm(dp0, wr_in, "nt", [BF16], N=D, K=Wr_in, bn=bnD, bk=_pick(Wr_in, bk_max), name="ret_in_dx", dep=tok,
                 **mmT)
    dX0, acc = _normmod_bwd(X0, da0, dX1, False, g_mix0, pack(0, 0, 1), None, R=L, L=L, tr=tr, name="normmod_mix0_bwd")
    _, acc_c = _normmod_bwd(X0, da0, dX1, False, g_mix0, pack(0, 0, 1), None, R=CTX, L=0, tr=tr, row0=L,
                            name="normmod_mix0_bwd_ctx")
    dmod_lat[0][0], dmod_lat[0][1], d_gmix0 = acc[0, 0], acc[0, 1], acc[0, 2] + acc_c[1, 2]
    dmod_ctx[0][0], dmod_ctx[0][1] = acc_c[1, 0], acc_c[1, 1]
    grad_x = dX0[None]

    wmv = {"ret_in": (ret_w_in, m_ret_w_in, v_ret_w_in, 0, "ret_w_in"),
           "ret_out": (ret_w_out, m_ret_w_out, v_ret_w_out, 0, "ret_w_out"),
           "attn_in": (attn_w_in, m_attn_w_in, v_attn_w_in, 0, "attn_w_in"),
           "attn_out": (attn_w_out, m_attn_w_out, v_attn_w_out, 0, "attn_w_out"),
           "w1_0": (mlp_w1, m_mlp_w1, v_mlp_w1, 0, "mlp_w1"), "w1_1": (mlp_w1, m_mlp_w1, v_mlp_w1, 1, "mlp_w1"),
           "w2_0": (mlp_w2, m_mlp_w2, v_mlp_w2, 0, "mlp_w2"), "w2_1": (mlp_w2, m_mlp_w2, v_mlp_w2, 1, "mlp_w2")}
    big = {}

    def finish_grad(entry, after):
        key, axis, ssem, rsem, dw_thru, land = entry
        dw_done, land_done = _scatter_wait(ssem, rsem, dw_thru, land, axis, after, name=f"rs_wait_{key}")
        w_, m_, v_, layer, out_name = wmv[key]
        big[out_name] = _adamw_sharded(w_, m_, v_, layer, dw_done, land_done, axis, me_arr, big.get(out_name),
                                       name=f"adamw_{key}")
        return big[out_name][0]

    after = dX0
    for entry in pending[:-1]:
        after = finish_grad(entry, after)

    misc = jnp.zeros((D,), F32)
    misc = misc.at[0:RH].set(dlg_f[:, 0, 0]).at[RH:2 * RH].set(dlg_b[:, 0, 0])
    misc = misc.at[2 * RH:2 * RH + Hq].set(dsink_acc[:, :G, 0].reshape(Hq)).at[2 * RH + Hq].set(loss_part)
    rows = ([dmod_lat[i][k] for i in range(2) for k in range(6)] + [dmod_ctx[i][k] for i in range(2) for k in range(6)]
            + [d_gmix0, d_gmix1, d_gmlp0, d_gmlp1, d_gf, misc, zeros_d, zeros_d])
    part = jnp.stack(rows, 0)
    part_all = _all_gather_small(part, name="ag_small_grads")
    tot = _sum_devices(part_all, name="sum_small_grads")

    grad_ada_b = (tot[0:12] + tot[12:24]).reshape(2, 6 * D)
    grad_norm_mix_g, grad_norm_mlp_g, grad_final_norm_g = tot[24:26], tot[26:28], tot[28]
    grad_ret_decay_fwd = (tot[29, 0:RH] * jax.nn.sigmoid(-ret_decay_fwd[0]))[None]
    grad_ret_decay_bwd = (tot[29, RH:2 * RH] * jax.nn.sigmoid(-ret_decay_bwd[0]))[None]
    grad_attn_sink = tot[29, 2 * RH:2 * RH + Hq][None]
    loss = tot[29, 2 * RH + Hq]

    dlat_cols = lax.dynamic_slice_in_dim(part_all[:, 0:12].reshape(N_DEV, 2, 6 * D), me * acols, acols, axis=2)
    dctx_cols = lax.dynamic_slice_in_dim(tot[12:24].reshape(2, 6 * D), me * acols, acols, axis=1)
    dmod16 = jnp.concatenate([dlat_cols.transpose(1, 0, 2), dctx_cols[:, None, :], jnp.zeros((2, 7, acols), F32)], 1)
    cond_t = _silu(c16).T
    g_ada, d_ada, nm_ada, nv_ada, dcond_part = _ada_bwd(cond_t, dmod16, ada_w, m_ada_w, v_ada_w, name="ada_bwd")
    dcond = (dcond_part[0, :, 0] + dcond_part[1, :, 0]).reshape(D // LANES, LANES)
    pad_rows = -(D // LANES) % 8
    dcond_pad = jnp.concatenate([dcond, jnp.zeros((pad_rows, LANES), F32)], 0) if pad_rows else dcond
    dcond_all = _all_gather_small(dcond_pad, name="ag_dcond")
    dcond_tot = _sum_devices(dcond_all, name="sum_dcond")[:D // LANES].reshape(D)
    sg = jax.nn.sigmoid(c_ctx)
    grad_c_ctx = dcond_tot * (sg * (1.0 + c_ctx * (1.0 - sg)))

    small_w = [c_ctx, ada_b, norm_mix_g, norm_mlp_g, ret_decay_fwd, ret_decay_bwd, attn_sink, final_norm_g]
    small_g = [grad_c_ctx, grad_ada_b, grad_norm_mix_g, grad_norm_mlp_g, grad_ret_decay_fwd, grad_ret_decay_bwd,
               grad_attn_sink, grad_final_norm_g]
    small_m = [m_c_ctx, m_ada_b, m_norm_mix_g, m_norm_mlp_g, m_ret_decay_fwd, m_ret_decay_bwd, m_attn_sink,
               m_final_norm_g]
    small_v = [v_c_ctx, v_ada_b, v_norm_mix_g, v_norm_mlp_g, v_ret_decay_fwd, v_ret_decay_bwd, v_attn_sink,
               v_final_norm_g]
    sizes = [w_.size for w_ in small_w]
    total = sum(-(-s // LANES) * LANES for s in sizes)
    total_pad = -(-total // (8 * LANES)) * 8 * LANES

    def flat_pack(ts, fill):
        pieces = []
        for t_ in ts:
            f = t_.reshape(-1).astype(F32)
            pad = -f.size % LANES
            pieces.append(jnp.concatenate([f, jnp.full((pad,), fill, F32)]) if pad else f)
        pieces.append(jnp.full((total_pad - total,), fill, F32))
        return jnp.concatenate(pieces).reshape(total_pad // LANES, LANES)

    d_s, nm_s, nv_s = _adamw_flat(flat_pack(small_w, 0.0), flat_pack(small_g, 0.0), flat_pack(small_m, 0.0),
                                  flat_pack(small_v, 1.0), name="adamw_small")

    def unpack(p):
        flat = p.reshape(-1)
        res, off = [], 0
        for w_, s in zip(small_w, sizes):
            res.append(flat[off:off + s].reshape(w_.shape))
            off += -(-s // LANES) * LANES
        return res

    finish_grad(pending[-1], d_s)
    d_small, nm_small, nv_small = unpack(d_s), unpack(nm_s), unpack(nv_s)
    small_names = ["c_ctx", "ada_b", "norm_mix_g", "norm_mlp_g", "ret_decay_fwd", "ret_decay_bwd", "attn_sink",
                   "final_norm_g"]
    sm = {n: (g_, d_, m_, v_) for n, g_, d_, m_, v_ in zip(small_names, small_g, d_small, nm_small, nv_small)}

    def out4(n):
        if n == "ada_w":
            return g_ada, d_ada, nm_ada, nv_ada
        if n in big:
            return tuple(big[n])
        return sm[n]

    order = ["c_ctx", "ada_w", "ada_b", "norm_mix_g", "norm_mlp_g", "mlp_w1", "mlp_w2", "ret_w_in", "ret_w_out",
             "ret_decay_fwd", "ret_decay_bwd", "attn_w_in", "attn_w_out", "attn_sink", "final_norm_g"]
    quads = [out4(n) for n in order]
    return (loss, grad_x, *[q_[0] for q_ in quads], *[q_[1] for q_ in quads], *[q_[2] for q_ in quads],
            *[q_[3] for q_ in quads])
```

```python
import functools

import jax
import jax.numpy as jnp
from jax import lax
from jax.experimental import pallas as pl
from jax.experimental.pallas import tpu as pltpu

F32 = jnp.float32
BF16 = jnp.bfloat16

N_DEV = 8
NORM_EPS = 1e-6
CHUNK = 128
ATT_HEAD_DIM = 128
GRID_W = 64
ROPE_BASE = 10000.0
NEG_INF = -1e30
ADAM_LR, ADAM_B1, ADAM_B2, ADAM_EPS, ADAM_WD, ADAM_STEP = 0.001, 0.9, 0.999, 1e-08, 0.01, 10

V7X_VMEM_LIMIT_BYTES = 56 * 1024 * 1024
MM_BN = 1024
LANES = 128
MESH = pl.DeviceIdType.MESH

_NN = (((1,), (0,)), ((), ()))
_NT = (((1,), (1,)), ((), ()))
_TN = (((0,), (0,)), ((), ()))


def _dot(a, b, dn=_NN):
    return lax.dot_general(a, b, dn, preferred_element_type=F32)


def _cparams(*sem):
    return pltpu.CompilerParams(dimension_semantics=sem, vmem_limit_bytes=V7X_VMEM_LIMIT_BYTES)


def _pick(n, pref, mult=LANES):
    if n <= pref:
        return n
    best = None
    for d in range(mult, pref + 1, mult):
        if n % d == 0:
            best = d
    assert best is not None, (n, pref)
    return best


def _silu(x):
    return x * jax.nn.sigmoid(x)


def _mm(a, b, mode, out_dtypes, *, M, N, K, bm, bn, bk, name, b_col0=0, epilogue=None, extras=(), dep=None,
        cols_outer=False, b_buffers=None):
    assert M % bm == 0 and N % bn == 0 and K % bk == 0 and b_col0 % bn == 0, (name, M, N, K, bm, bn, bk, b_col0)
    nk = K // bk
    c0 = b_col0 // bn
    ax_i, ax_j = (1, 0) if cols_outer else (0, 1)

    def spec(block, f, **kw):
        if cols_outer:
            return pl.BlockSpec(block, lambda j, i, k: f(i, j, k), **kw)
        return pl.BlockSpec(block, f, **kw)

    b_kw = {} if b_buffers is None else dict(pipeline_mode=pl.Buffered(b_buffers))
    if mode == "nn":
        a_spec = spec((bm, bk), lambda i, j, k: (i, k))
        b_spec = spec((bk, bn), lambda i, j, k: (k, j + c0), **b_kw)
    elif mode == "nt":
        a_spec = spec((bm, bk), lambda i, j, k: (i, k))
        b_spec = spec((bn, bk), lambda i, j, k: (j + c0, k), **b_kw)
    else:
        a_spec = spec((bk, bm), lambda i, j, k: (k, i))
        b_spec = spec((bk, bn), lambda i, j, k: (k, j + c0), **b_kw)
    dn = {"nn": _NN, "nt": _NT, "tn": _TN}[mode]
    e_specs = [spec(bs, (lambda i, j, k, f=f: f(i, j))) for (_, bs, f) in extras]
    ne, no = len(extras), len(out_dtypes)
    nd = 0 if dep is None else 1

    def body(a_ref, b_ref, *rest):
        e_refs, o_refs = rest[:ne], rest[ne + nd:ne + nd + no]
        i, j, k = pl.program_id(ax_i), pl.program_id(ax_j), pl.program_id(2)

        def finish(acc):
            outs = (acc,) if epilogue is None else epilogue(acc, i, j, *[e[...] for e in e_refs])
            for o_ref, o in zip(o_refs, outs):
                o_ref[...] = o.astype(o_ref.dtype)

        p = _dot(a_ref[...], b_ref[...], dn)
        if nk == 1:
            finish(p)
        else:
            acc_ref = rest[-1]

            @pl.when(k == 0)
            def _():
                acc_ref[...] = p

            @pl.when(k > 0)
            def _():
                acc_ref[...] += p

            @pl.when(k == nk - 1)
            def _():
                finish(acc_ref[...])

    outs = pl.pallas_call(
        body, name=name, grid=(N // bn, M // bm, nk) if cols_outer else (M // bm, N // bn, nk),
        in_specs=[a_spec, b_spec] + e_specs + [pl.BlockSpec(memory_space=pl.ANY)] * nd,
        out_specs=[spec((bm, bn), lambda i, j, k: (i, j)) for _ in out_dtypes],
        out_shape=[jax.ShapeDtypeStruct((M, N), dt) for dt in out_dtypes],
        scratch_shapes=[pltpu.VMEM((bm, bn), F32)] if nk > 1 else [],
        compiler_params=_cparams("parallel", "parallel", "arbitrary"),
    )(a, b, *[e[0] for e in extras], *([dep] if nd else []))
    return outs


def _rowwise(body, rows, vecs, outs, n_acc, *, R, L, tr, name, acc_width=None, dep=None, row0=0):
    assert row0 % tr == 0
    b0 = row0 // tr
    assert R % tr == 0 and L % tr == 0, (name, R, L, tr)
    nl = L // tr
    n_regions = 2 if R > L else 1
    n_rows, n_vecs, n_outs = len(rows), len(vecs), len(outs)
    n_dep = 0 if dep is None else 1
    acc_pad = -(-n_acc // 8) * 8 if n_acc else 0

    in_specs = []
    for (_, w, cb, lat_only) in rows:
        if lat_only:
            in_specs.append(pl.BlockSpec((tr, w), lambda i, cb=cb: (jnp.minimum(i, nl - 1), cb)))
        else:
            in_specs.append(pl.BlockSpec((tr, w), lambda i, cb=cb: (i + b0, cb)))
    for v in vecs:
        in_specs.append(pl.BlockSpec(v.shape, lambda i, nd=v.ndim: (0,) * nd))
    in_specs += [pl.BlockSpec(memory_space=pl.ANY)] * n_dep
    out_specs = [pl.BlockSpec((tr, w), lambda i: (i, 0)) for (w, _) in outs]
    out_shape = [jax.ShapeDtypeStruct((R, w), dt) for (w, dt) in outs]
    if n_acc:
        out_specs.append(pl.BlockSpec((None, acc_pad, acc_width), lambda i: (jnp.where(i >= nl, 1, 0), 0, 0)))
        out_shape.append(jax.ShapeDtypeStruct((n_regions, acc_pad, acc_width), F32))

    def kern(*refs):
        i = pl.program_id(0)
        is_ctx = i >= nl
        ins = [r[...] for r in refs[:n_rows + n_vecs]]
        o_refs = refs[n_rows + n_vecs + n_dep:]
        out_tiles, acc_rows = body(is_ctx, *ins)
        for o_ref, o in zip(o_refs[:n_outs], out_tiles):
            o_ref[...] = o.astype(o_ref.dtype)
        if n_acc:
            acc_ref = o_refs[n_outs]

            @pl.when((i == 0) | (i == nl))
            def _():
                acc_ref[...] = jnp.zeros_like(acc_ref)

            for r, row in enumerate(acc_rows):
                acc_ref[r:r + 1, :] += row

    res = pl.pallas_call(
        kern, name=name, grid=(R // tr,), in_specs=in_specs, out_specs=out_specs, out_shape=out_shape,
        compiler_params=_cparams("arbitrary"),
    )(*[r[0] for r in rows], *vecs, *([dep] if n_dep else []))
    return res


def _colsum(x):
    return jnp.sum(x, axis=0, keepdims=True)


def _rms_stats(x):
    r = lax.rsqrt(jnp.mean(x * x, axis=-1, keepdims=True) + NORM_EPS)
    return x * r, r


def _sel(is_ctx, pk, lat_row, ctx_row):
    return jnp.where(is_ctx, pk[ctx_row:ctx_row + 1, :], pk[lat_row:lat_row + 1, :])


def _normmod(x, g, pk, *, R, L, tr, name, dep=None):
    D = x.shape[-1]

    def body(is_ctx, xt, gv, pkv):
        xh, _ = _rms_stats(xt)
        sh, sc = _sel(is_ctx, pkv, 0, 2), _sel(is_ctx, pkv, 1, 3)
        return ((xh * gv) * (1.0 + sc) + sh,), ()

    return _rowwise(body, [(x, D, 0, False)], [g, pk], [(D, BF16)], 0, R=R, L=L, tr=tr, name=name, dep=dep)[0]


def _normmod_bwd(x_in, da, dx_out, dx_out_lat_only, g, pk, prev, *, R, L, tr, name, row0=0):
    D = x_in.shape[-1]
    has_prev = prev is not None

    def body(is_ctx, *t):
        if has_prev:
            xt, dat, dxo, mp, gv, pkv, gates = t
        else:
            xt, dat, dxo, gv, pkv = t
        xh, r = _rms_stats(xt)
        dat = dat.astype(F32)
        sc = _sel(is_ctx, pkv, 1, 3)
        if dx_out_lat_only:
            dxo = jnp.where(is_ctx, 0.0, dxo)
        dn = dat * (1.0 + sc)
        w = dn * gv
        dxi = dxo + r * (w - xh * jnp.mean(w * xh, axis=-1, keepdims=True))
        accs = [_colsum(dat), _colsum(dat * (xh * gv)), _colsum(dn * xh)]
        outs = [dxi]
        if has_prev:
            gate = _sel(is_ctx, gates, 0, 1)
            outs.append(dxi * gate)
            accs.append(_colsum(dxi * mp.astype(F32)))
        return outs, accs

    rows = [(x_in, D, 0, False), (da, D, 0, False), (dx_out, D, 0, dx_out_lat_only)]
    vecs = [g, pk]
    outs = [(D, F32)]
    if has_prev:
        rows.append((prev[0], D, 0, False))
        vecs.append(prev[1])
        outs.append((D, BF16))
    return _rowwise(body, rows, vecs, outs, 4 if has_prev else 3, R=R, L=L, tr=tr, name=name, acc_width=D, row0=row0)


def _loss_head(x4, target, m_prev, gf, gate, *, L, tr, name):
    D = x4.shape[-1]

    def body(is_ctx, xt, tg, mp, gfv, gatev):
        xh, r = _rms_stats(xt)
        e = xh * gfv - tg
        dy = e * (1.0 / D)
        w = dy * gfv
        dx = r * (w - xh * jnp.mean(w * xh, axis=-1, keepdims=True))
        accs = [_colsum(e * e) * (0.5 / D), _colsum(dy * xh), _colsum(dx * mp.astype(F32))]
        return (dx, dx * gatev), accs

    return _rowwise(body, [(x4, D, 0, False), (target, D, 0, False), (m_prev, D, 0, False)], [gf, gate],
                    [(D, F32), (D, BF16)], 3, R=L, L=L, tr=tr, name=name, acc_width=D)


RET_CHUNK = 2 * LANES
RET_HEADS_PER_STEP = 4


def _decays(lgh, rev):
    C = RET_CHUNK
    ii = lax.broadcasted_iota(jnp.int32, (C, C), 0)
    jj = lax.broadcasted_iota(jnp.int32, (C, C), 1)
    ri = lax.broadcasted_iota(jnp.int32, (C, 1), 0).astype(F32)
    diff = (jj - ii if rev else ii - jj)
    amat = jnp.where(diff >= 0, jnp.exp(lgh * jnp.maximum(diff, 0).astype(F32)), 0.0)
    pos = (C - ri) if rev else (ri + 1.0)
    bq = jnp.exp(lgh * pos)
    bk = jnp.exp(lgh * (C - pos))
    return amat, bq, bk, pos


def _ret_geometry(T, L, H, rev, backward):
    C = RET_CHUNK
    assert T % C == 0 and L % C == 0, (T, L)
    nT, nL = T // C, L // C
    hb = RET_HEADS_PER_STEP if H % RET_HEADS_PER_STEP == 0 else 1

    def step(s):
        return (nT - 1 - s) if backward else s

    def chunk(s):
        s = step(s)
        return (nT - 1 - s) if rev else (s + nL) % nT

    return C, nT, hb, chunk, step


def _ret_fwd(qk, vg, lg, other, *, T, L, H, rev, name):
    dk, dv = 2 * LANES, 4 * LANES
    C, nT, hb, chunk, step = _ret_geometry(T, L, H, rev, False)
    n_other = 0 if other is None else 1

    def body(lg_ref, q_ref, k_ref, v_ref, *rest):
        o_ref, st_ref, s_scr = rest[n_other:]
        hg, s = pl.program_id(0), pl.program_id(1)

        @pl.when(s == 0)
        def _():
            s_scr[...] = jnp.zeros_like(s_scr)

        for hh in range(hb):
            lgh = lg_ref[0, hg * hb + hh]
            amat, bq, bk, _ = _decays(lgh, rev)
            q, k = q_ref[:, hh * dk:(hh + 1) * dk], k_ref[:, hh * dk:(hh + 1) * dk]
            v = v_ref[:, hh * dv:(hh + 1) * dv]
            stb = s_scr[hh].astype(BF16)
            st_ref[hh] = stb
            scores = _dot(q, k, _NT) * amat
            o = _dot(scores.astype(BF16), v) + _dot(q, stb) * bq
            if n_other:
                o = rest[0][:, hh * dv:(hh + 1) * dv] + o
            o_ref[:, hh * dv:(hh + 1) * dv] = o
            kd = (k.astype(F32) * bk).astype(BF16)
            s_scr[hh] = s_scr[hh] * jnp.exp(lgh * C) + _dot(kd, v, _TN)

    vspec = pl.BlockSpec((C, hb * dv), lambda h, s: (chunk(s), h))
    return pl.pallas_call(
        body, name=name, grid=(H // hb, nT),
        in_specs=[pl.BlockSpec(memory_space=pltpu.SMEM),
                  pl.BlockSpec((C, hb * dk), lambda h, s: (chunk(s), h)),
                  pl.BlockSpec((C, hb * dk), lambda h, s: (chunk(s), H // hb + h)), vspec] + [vspec] * n_other,
        out_specs=[vspec, pl.BlockSpec((hb, None, dk, dv), lambda h, s: (h, s, 0, 0))],
        out_shape=[jax.ShapeDtypeStruct((T, H * dv), F32), jax.ShapeDtypeStruct((H, nT, dk, dv), BF16)],
        scratch_shapes=[pltpu.VMEM((hb, dk, dv), F32)],
        compiler_params=_cparams("parallel", "arbitrary"),
    )(lg, qk, qk, vg, *([other] if n_other else []))


def _ret_bwd(qk, vg, do, states, lg, *, T, L, H, rev, name):
    dk, dv = 2 * LANES, 4 * LANES
    C, nT, hb, chunk, step = _ret_geometry(T, L, H, rev, True)

    def body(lg_ref, q_ref, k_ref, v_ref, do_ref, st_ref, dq_ref, dk_ref, dv_ref, dlg_ref, ds_scr):
        hg, s = pl.program_id(0), pl.program_id(1)

        @pl.when(s == 0)
        def _():
            ds_scr[...] = jnp.zeros_like(ds_scr)
            dlg_ref[...] = jnp.zeros_like(dlg_ref)

        for hh in range(hb):
            lgh = lg_ref[0, hg * hb + hh]
            amat, bq, bk, pos = _decays(lgh, rev)
            ksl, vsl = slice(hh * dk, (hh + 1) * dk), slice(hh * dv, (hh + 1) * dv)
            q, k, v, dob = q_ref[:, ksl], k_ref[:, ksl], v_ref[:, vsl], do_ref[:, vsl]
            stb = st_ref[hh]
            ds_new = ds_scr[hh]
            dsb = ds_new.astype(BF16)
            qf, kf = q.astype(F32), k.astype(F32)
            scores = (_dot(q, k, _NT) * amat).astype(BF16)
            dqk = (_dot(dob, v, _NT) * amat).astype(BF16)
            dq = _dot(dqk, k) + _dot(dob, stb, _NT) * bq
            dkk = _dot(dqk, q, _TN) + _dot(v, dsb, _NT) * bk
            kd = (kf * bk).astype(BF16)
            dvv = _dot(scores, dob, _TN) + _dot(kd, dsb)
            dod = (dob.astype(F32) * bq).astype(BF16)
            ds_prev = ds_new * jnp.exp(lgh * C) + _dot(q, dod, _TN)
            ds_scr[hh] = ds_prev
            dq_ref[:, ksl] = dq.astype(dq_ref.dtype)
            dk_ref[:, ksl] = dkk.astype(dk_ref.dtype)
            dv_ref[:, vsl] = dvv.astype(dv_ref.dtype)
            dlg = (jnp.sum(pos * jnp.sum(qf * dq - kf * dkk, axis=-1, keepdims=True))
                   + C * jnp.sum(ds_prev * stb.astype(F32)))
            dlg_ref[hh] += dlg

    qspec = pl.BlockSpec((C, hb * dk), lambda h, s: (chunk(s), h))
    vspec = pl.BlockSpec((C, hb * dv), lambda h, s: (chunk(s), h))
    return pl.pallas_call(
        body, name=name, grid=(H // hb, nT),
        in_specs=[pl.BlockSpec(memory_space=pltpu.SMEM), qspec,
                  pl.BlockSpec((C, hb * dk), lambda h, s: (chunk(s), H // hb + h)), vspec, vspec,
                  pl.BlockSpec((hb, None, dk, dv), lambda h, s: (h, step(s), 0, 0))],
        out_specs=[qspec, qspec, vspec, pl.BlockSpec((hb, 8, LANES), lambda h, s: (h, 0, 0))],
        out_shape=[jax.ShapeDtypeStruct((T, H * dk), BF16), jax.ShapeDtypeStruct((T, H * dk), BF16),
                   jax.ShapeDtypeStruct((T, H * dv), BF16), jax.ShapeDtypeStruct((H, 8, LANES), F32)],
        scratch_shapes=[pltpu.VMEM((hb, dk, dv), F32)],
        compiler_params=_cparams("parallel", "arbitrary"),
    )(lg, qk, qk, vg, do, states)


def _readout(o, vg, *, T, L, H, tr, name, dep=None):
    dv = 4 * LANES
    W = H * dv

    def body(is_ctx, o, g):
        parts = []
        for h in range(H):
            oh = o[:, h * dv:(h + 1) * dv]
            parts.append(oh * lax.rsqrt(jnp.mean(oh * oh, axis=-1, keepdims=True) + NORM_EPS))
        y = jnp.concatenate(parts, axis=1)
        return (_silu(g.astype(F32)) * y,), ()

    return _rowwise(body, [(o, W, 0, False), (vg, W, 1, False)], [], [(W, BF16)], 0,
                    R=T, L=L, tr=tr, name=name, dep=dep)[0]


def _readout_bwd(dz, o, vg, *, T, L, H, tr, name):
    dv = 4 * LANES
    W = H * dv

    def body(is_ctx, dzt, o, g):
        gf = g.astype(F32)
        sg = jax.nn.sigmoid(gf)
        dzf = dzt.astype(F32)
        dy = dzf * (gf * sg)
        ys, dos = [], []
        for h in range(H):
            sl = slice(h * dv, (h + 1) * dv)
            oh, dyh = o[:, sl], dy[:, sl]
            r = lax.rsqrt(jnp.mean(oh * oh, axis=-1, keepdims=True) + NORM_EPS)
            yh = oh * r
            ys.append(yh)
            dos.append(r * (dyh - yh * jnp.mean(dyh * yh, axis=-1, keepdims=True)))
        y = jnp.concatenate(ys, axis=1)
        dg = dzf * y * (sg * (1.0 + gf * (1.0 - sg)))
        return (jnp.concatenate(dos, axis=1), dg), ()

    return _rowwise(body, [(dz, W, 0, False), (o, W, 0, False), (vg, W, 1, False)], [],
                    [(W, BF16), (W, BF16)], 0, R=T, L=L, tr=tr, name=name)


def _ret_assemble(dq_f, dq_b, dk_f, dk_b, dv_f, dv_b, dg, cos, sin, *, T, L, H, tr, name):
    dk, dv = 2 * LANES, 4 * LANES
    Wq, Wv = H * dk, H * dv
    kscale = float(dk) ** -0.5

    def unrope(d, c, s_, scale):
        parts = []
        for h in range(H):
            d1, d2 = d[:, h * dk:h * dk + LANES], d[:, h * dk + LANES:(h + 1) * dk]
            parts += [(d1 * c + d2 * s_) * scale, (d2 * c - d1 * s_) * scale]
        return jnp.concatenate(parts, axis=1)

    def body(is_ctx, qf, qb, kf, kb, vf, vb, g, c, s_):
        add = lambda a, b: a.astype(F32) + b.astype(F32)
        dq = unrope(add(qf, qb), c, s_, 1.0)
        dkk = unrope(add(kf, kb), c, s_, kscale)
        return (jnp.concatenate([dq.astype(BF16), dkk.astype(BF16), add(vf, vb).astype(BF16), g], axis=1),), ()

    rows = [(dq_f, Wq, 0, False), (dq_b, Wq, 0, False), (dk_f, Wq, 0, False), (dk_b, Wq, 0, False),
            (dv_f, Wv, 0, False), (dv_b, Wv, 0, False), (dg, Wv, 0, False),
            (cos, LANES, 0, False), (sin, LANES, 0, False)]
    return _rowwise(body, rows, [], [(2 * Wq + 2 * Wv, BF16)], 0, R=T, L=L, tr=tr, name=name)[0]


def _swap32(x):
    n = x.shape[-1]
    lane = lax.broadcasted_iota(jnp.int32, x.shape, x.ndim - 1)
    return jnp.where(lane % 64 < 32, pltpu.roll(x, n - 32, x.ndim - 1), pltpu.roll(x, 32, x.ndim - 1))


ATT_Q_BLOCKS = 4


def _stack_heads_at(ref, rows, G):
    return jnp.concatenate([ref[rows, g * LANES:(g + 1) * LANES] for g in range(G)], axis=0)


def _stack_columns_at(ref, rows, G):
    return jnp.concatenate([ref[rows, g:g + 1] for g in range(G)], axis=0)


def _sink_column(sink_ref, h, G):
    return jnp.concatenate([jnp.full((CHUNK, 1), sink_ref[0, h * G + g], F32) for g in range(G)], axis=0)


def _key_mask(n, nb, CTX, G):
    W = 3 * CHUNK + CTX
    ii = lax.broadcasted_iota(jnp.int32, (G * CHUNK, W), 0) & (CHUNK - 1)
    col = lax.broadcasted_iota(jnp.int32, (G * CHUNK, W), 1)
    is_prev = col < CHUNK
    is_next = (col >= 2 * CHUNK) & (col < 3 * CHUNK)
    prev_ok = is_prev & (col >= ii) & (n > 0)
    next_ok = is_next & ((col - 2 * CHUNK) <= ii) & (n < nb - 1)
    return prev_ok | next_ok | jnp.logical_not(is_prev | is_next)


def _attn_geometry(L, CTX):
    nb = L // CHUNK
    QB = ATT_Q_BLOCKS if nb % ATT_Q_BLOCKS == 0 else 1

    def blk(j):
        return pl.BlockSpec((CHUNK, LANES), lambda h, m: (jnp.clip(m * QB + j - 1, 0, nb - 1), h))

    kvs = [blk(j) for j in range(QB + 2)] + [pl.BlockSpec((CTX, LANES), lambda h, m: (L // CTX, h))]
    return nb, QB, kvs


def _attn_fwd(q, k, v, sink, *, L, CTX, Hkv, G, name):
    scale = float(ATT_HEAD_DIM) ** -0.5
    nb, QB, kvs = _attn_geometry(L, CTX)
    nkv = QB + 3

    def body(sink_ref, q_ref, *rest):
        kb, vb, (o_ref, lse_ref) = rest[:nkv], rest[nkv:2 * nkv], rest[2 * nkv:]
        h, m_ = pl.program_id(0), pl.program_id(1)
        sk = _sink_column(sink_ref, h, G)
        for sub in range(QB):
            rows = slice(sub * CHUNK, (sub + 1) * CHUNK)
            qs = _stack_heads_at(q_ref, rows, G)
            kall = jnp.concatenate([kb[sub + j][...] for j in range(3)] + [kb[-1][...]], axis=0)
            vall = jnp.concatenate([vb[sub + j][...] for j in range(3)] + [vb[-1][...]], axis=0)
            s_ = jnp.where(_key_mask(m_ * QB + sub, nb, CTX, G), _dot(qs, kall, _NT) * scale, NEG_INF)
            m = jnp.maximum(jnp.max(s_, axis=-1, keepdims=True), sk)
            p = jnp.exp(s_ - m)
            den = jnp.sum(p, axis=-1, keepdims=True) + jnp.exp(sk - m)
            o = _dot(p.astype(BF16), vall) / den
            lse = m + jnp.log(den)
            for g in range(G):
                o_ref[rows, g * LANES:(g + 1) * LANES] = o[g * CHUNK:(g + 1) * CHUNK].astype(o_ref.dtype)
                lse_ref[rows, g:g + 1] = lse[g * CHUNK:(g + 1) * CHUNK]

    qspec = pl.BlockSpec((QB * CHUNK, G * LANES), lambda h, m: (m, h))
    return pl.pallas_call(
        body, name=name, grid=(Hkv, nb // QB),
        in_specs=[pl.BlockSpec(memory_space=pltpu.SMEM), qspec] + kvs + kvs,
        out_specs=[qspec, pl.BlockSpec((None, QB * CHUNK, G), lambda h, m: (h, m, 0))],
        out_shape=[jax.ShapeDtypeStruct((L, Hkv * G * LANES), BF16), jax.ShapeDtypeStruct((Hkv, L, G), F32)],
        compiler_params=_cparams("parallel", "parallel"),
    )(sink, q, *([k] * nkv), *([v] * nkv))


def _attn_bwd(q, k, v, do, lse, sink, *, L, CTX, Hkv, G, name):
    scale = float(ATT_HEAD_DIM) ** -0.5
    nb, QB, kvs = _attn_geometry(L, CTX)
    nkv = QB + 3
    qspec = pl.BlockSpec((QB * CHUNK, G * LANES), lambda h, m: (m, h))
    rowspec = pl.BlockSpec((None, QB * CHUNK, G), lambda h, m: (h, m, 0))
    colspec = lambda rows: pl.BlockSpec((rows, LANES), lambda h, m: (0, h))

    def body(sink_ref, q_ref, do_ref, lse_ref, *rest):
        kb, vb = rest[:nkv], rest[nkv:2 * nkv]
        dq_ref, dk_ref, dv_ref, dkx_ref, dvx_ref, dsk_ref = rest[2 * nkv:]
        h, m_ = pl.program_id(0), pl.program_id(1)

        @pl.when(m_ == 0)
        def _():
            for r in (dk_ref, dv_ref, dkx_ref, dvx_ref, dsk_ref):
                r[...] = jnp.zeros_like(r)

        sk = _sink_column(sink_ref, h, G)
        for sub in range(QB):
            n = m_ * QB + sub
            rows = slice(sub * CHUNK, (sub + 1) * CHUNK)
            qs, dos = _stack_heads_at(q_ref, rows, G), _stack_heads_at(do_ref, rows, G)
            kall = jnp.concatenate([kb[sub + j][...] for j in range(3)] + [kb[-1][...]], axis=0)
            vall = jnp.concatenate([vb[sub + j][...] for j in range(3)] + [vb[-1][...]], axis=0)
            lse_c = _stack_columns_at(lse_ref, rows, G)
            p = jnp.where(_key_mask(n, nb, CTX, G), jnp.exp(_dot(qs, kall, _NT) * scale - lse_c), 0.0)
            dp = _dot(dos, vall, _NT)
            delta = jnp.sum(p * dp, axis=-1, keepdims=True)
            ds = (p * (dp - delta) * scale).astype(BF16)
            dq = _dot(ds, kall)
            dk_all = _dot(ds, qs, _TN)
            dv_all = _dot(p.astype(BF16), dos, _TN)
            for g in range(G):
                dq_ref[rows, g * LANES:(g + 1) * LANES] = dq[g * CHUNK:(g + 1) * CHUNK]
            for part, blk in enumerate((jnp.maximum(n - 1, 0), n, jnp.minimum(n + 1, nb - 1))):
                krows = pl.ds(pl.multiple_of(blk * CHUNK, CHUNK), CHUNK)
                dk_ref[krows, :] += dk_all[part * CHUNK:(part + 1) * CHUNK]
                dv_ref[krows, :] += dv_all[part * CHUNK:(part + 1) * CHUNK]
            dkx_ref[...] += dk_all[3 * CHUNK:]
            dvx_ref[...] += dv_all[3 * CHUNK:]
            dsink = -jnp.exp(sk - lse_c) * delta
            for g in range(G):
                dsk_ref[g:g + 1, :] += jnp.sum(dsink[g * CHUNK:(g + 1) * CHUNK])

    return pl.pallas_call(
        body, name=name, grid=(Hkv, nb // QB),
        in_specs=[pl.BlockSpec(memory_space=pltpu.SMEM), qspec, qspec, rowspec] + kvs + kvs,
        out_specs=[qspec, colspec(L), colspec(L), colspec(CTX), colspec(CTX),
                   pl.BlockSpec((None, 8, LANES), lambda h, m: (h, 0, 0))],
        out_shape=[jax.ShapeDtypeStruct((L, Hkv * G * LANES), F32),
                   jax.ShapeDtypeStruct((L, Hkv * LANES), F32), jax.ShapeDtypeStruct((L, Hkv * LANES), F32),
                   jax.ShapeDtypeStruct((CTX, Hkv * LANES), F32), jax.ShapeDtypeStruct((CTX, Hkv * LANES), F32),
                   jax.ShapeDtypeStruct((Hkv, 8, LANES), F32)],
        compiler_params=_cparams("parallel", "arbitrary"),
    )(sink, q, do, lse, *([k] * nkv), *([v] * nkv))


def _attn_assemble(dq, dk_lat, dv_lat, dk_ctx, dv_ctx, cos, sin, *, T, L, CTX, Hq, Hkv, tr, name):
    Wq, Wk = Hq * LANES, Hkv * LANES
    ctx_blocks = CTX // tr
    nl = L // tr

    def unrope(d, c, s_, heads):
        return d * jnp.tile(c, (1, heads)) + _swap32(d * jnp.tile(s_, (1, heads)))

    def body(is_ctx, dqt, dkl, dvl, dkc, dvc, c, s_):
        dq_ = jnp.where(is_ctx, 0.0, unrope(dqt, c, s_, Hq))
        dk_ = unrope(jnp.where(is_ctx, dkc, dkl), c, s_, Hkv)
        dv_ = jnp.where(is_ctx, dvc, dvl)
        return (jnp.concatenate([dq_, dk_, dv_], axis=1),), ()

    def ctx_map(i):
        return (jnp.clip(i - nl, 0, ctx_blocks - 1), 0)

    assert T % tr == 0 and L % tr == 0 and CTX % tr == 0
    in_specs = [pl.BlockSpec((tr, Wq), lambda i: (jnp.minimum(i, nl - 1), 0)),
                pl.BlockSpec((tr, Wk), lambda i: (jnp.minimum(i, nl - 1), 0)),
                pl.BlockSpec((tr, Wk), lambda i: (jnp.minimum(i, nl - 1), 0)),
                pl.BlockSpec((tr, Wk), ctx_map), pl.BlockSpec((tr, Wk), ctx_map),
                pl.BlockSpec((tr, LANES), lambda i: (i, 0)), pl.BlockSpec((tr, LANES), lambda i: (i, 0))]

    def kern(dq_r, dkl_r, dvl_r, dkc_r, dvc_r, c_r, s_r, o_ref):
        is_ctx = pl.program_id(0) >= nl
        (out,), _ = body(is_ctx, dq_r[...], dkl_r[...], dvl_r[...], dkc_r[...], dvc_r[...], c_r[...], s_r[...])
        o_ref[...] = out.astype(o_ref.dtype)

    return pl.pallas_call(
        kern, name=name, grid=(T // tr,), in_specs=in_specs,
        out_specs=pl.BlockSpec((tr, Wq + 2 * Wk), lambda i: (i, 0)),
        out_shape=jax.ShapeDtypeStruct((T, Wq + 2 * Wk), BF16),
        compiler_params=_cparams("parallel"),
    )(dq, dk_lat, dv_lat, dk_ctx, dv_ctx, cos, sin)


def _my_place():
    x, y, c = lax.axis_index("x"), lax.axis_index("y"), lax.axis_index("c")
    return x, y, c


def _all_gather_small(v, *, name):
    R, C = v.shape

    def body(x_ref, out_ref, send_sems, recv_sems, local_sem):
        x, y, c = _my_place()
        me, sibling = (x, y, c), (x, y, 1 - c)
        chips = [(1 - x, y), (x, 1 - y), (1 - x, 1 - y)]

        def slot(px, py, pc):
            return out_ref.at[4 * px + 2 * py + pc]

        def copy(k, block, to, src=None):
            return pltpu.make_async_remote_copy(
                src_ref=slot(*block) if src is None else src, dst_ref=slot(*block),
                send_sem=send_sems.at[k], recv_sem=recv_sems.at[k], device_id=to, device_id_type=MESH)

        mine = pltpu.make_async_copy(x_ref, slot(*me), local_sem)
        mine.start()
        first = [copy(0, me, sibling, src=x_ref)]
        first += [copy(1 + j, me, (*chip, c), src=x_ref) for j, chip in enumerate(chips)]
        for cp in first:
            cp.start()
        passed = [copy(4 + j, (*chip, c), sibling) for j, chip in enumerate(chips)]
        for j, chip in enumerate(chips):
            copy(1 + j, (*chip, c), me).wait_recv()
            passed[j].start()
        copy(0, sibling, me).wait_recv()
        for j, chip in enumerate(chips):
            copy(4 + j, (*chip, 1 - c), me).wait_recv()
        for cp in first + passed:
            cp.wait_send()
        mine.wait()

    return pl.pallas_call(
        body, name=name, out_shape=jax.ShapeDtypeStruct((N_DEV, R, C), v.dtype),
        in_specs=[pl.BlockSpec(memory_space=pltpu.VMEM)], out_specs=pl.BlockSpec(memory_space=pltpu.VMEM),
        scratch_shapes=[pltpu.SemaphoreType.DMA((7,)), pltpu.SemaphoreType.DMA((7,)), pltpu.SemaphoreType.DMA],
    )(v)


_HBM_SPEC = pl.BlockSpec(memory_space=pltpu.HBM)
_SEM_SPEC = pl.BlockSpec(memory_space=pltpu.SEMAPHORE)
_ANY_SPEC = pl.BlockSpec(memory_space=pl.ANY)
_DATAFLOW = pltpu.SideEffectType.DATAFLOW_SIDE_EFFECTING
N_PEERS = N_DEV - 1


def _peer(r):
    x, y, c = _my_place()
    return ((1 - x) if r & 4 else x, (1 - y) if r & 2 else y, (1 - c) if r & 1 else c)


def _index_of(place):
    return 4 * place[0] + 2 * place[1] + place[2]


def _slot(ref, axis, idx, size):
    if axis == 0:
        return ref.at[pl.ds(idx * size, size), :]
    return ref.at[:, pl.ds(idx * size, size)]


def _cast_place(w3, layer, axis, me_arr, dep, *, name):
    Ks, Ns = w3.shape[1], w3.shape[2]
    tr = _pick(Ks, 256, 16)
    nblk = Ks // tr
    full = (Ks * N_DEV, Ns) if axis == 0 else (Ks, Ns * N_DEV)
    if axis == 0:
        out_map = lambda i, me: (me[0] * nblk + i, 0)
    else:
        out_map = lambda i, me: (i, me[0])

    def body(me_ref, w_ref, dep_ref, o_ref):
        o_ref[...] = w_ref[...].astype(BF16)

    return pl.pallas_call(
        body, name=name, out_shape=jax.ShapeDtypeStruct(full, BF16),
        grid_spec=pltpu.PrefetchScalarGridSpec(
            num_scalar_prefetch=1, grid=(nblk,),
            in_specs=[pl.BlockSpec((None, tr, Ns), lambda i, me: (layer, i, 0)), pl.BlockSpec(memory_space=pl.ANY)],
            out_specs=pl.BlockSpec((tr, Ns), out_map)),
        compiler_params=_cparams("parallel"),
    )(me_arr, w3, dep)


AG_FIRST = 4
AG_CHIPS = 3


def _sibling():
    x, y, c = _my_place()
    return (x, y, 1 - c)


def _chip_peer(j, same_core=True):
    x, y, c = _my_place()
    px = (1 - x) if j in (0, 2) else x
    py = (1 - y) if j in (1, 2) else y
    return (px, py, c if same_core else 1 - c)


def _gather_start(lands, axes, after, *, name):
    nt = len(lands)
    sizes = [l.shape[a] // N_DEV for l, a in zip(lands, axes)]

    def body(*refs):
        ins, send_sems, recv_sems, token = refs[:nt], refs[nt + 1], refs[nt + 2], refs[-1]
        my_idx = _index_of(_my_place())
        for t in range(nt):
            mine = _slot(ins[t], axes[t], my_idx, sizes[t])
            for k, to in enumerate([_sibling()] + [_chip_peer(j) for j in range(AG_CHIPS)]):
                pltpu.make_async_remote_copy(src_ref=mine, dst_ref=mine, send_sem=send_sems.at[t * AG_FIRST + k],
                                             recv_sem=recv_sems.at[t * AG_FIRST + k], device_id=to,
                                             device_id_type=MESH).start()
        token[...] = jnp.zeros_like(token)

    res = pl.pallas_call(
        body, name=name,
        out_shape=(pltpu.SemaphoreType.DMA((nt * AG_FIRST,)), pltpu.SemaphoreType.DMA((nt * AG_FIRST,)),
                   *[pltpu.HBM(l.shape, l.dtype) for l in lands], jax.ShapeDtypeStruct((8, LANES), F32)),
        in_specs=[_HBM_SPEC] * nt + [_ANY_SPEC],
        out_specs=(_SEM_SPEC, _SEM_SPEC, *[_HBM_SPEC] * nt, pl.BlockSpec(memory_space=pltpu.VMEM)),
        input_output_aliases={t: 2 + t for t in range(nt)},
        compiler_params=pltpu.CompilerParams(has_side_effects=_DATAFLOW),
    )(*[pltpu.with_memory_space_constraint(l, pltpu.HBM) for l in lands], after)
    return res[0], res[1], list(res[2:2 + nt]), res[-1]


def _gather_forward(send_a, recv_a, lands, axes, after, *, name):
    nt = len(lands)
    sizes = [l.shape[a] // N_DEV for l, a in zip(lands, axes)]

    def body(*refs):
        ins, send_a, recv_a = refs[:nt], refs[nt], refs[nt + 1]
        send_f, recv_f, token = refs[nt + 3], refs[nt + 4], refs[-1]
        my_idx = _index_of(_my_place())
        for t in range(nt):
            for j in range(AG_CHIPS):
                src_dev = _chip_peer(j)
                arrived = _slot(ins[t], axes[t], _index_of(src_dev), sizes[t])
                pltpu.make_async_remote_copy(
                    src_ref=_slot(ins[t], axes[t], my_idx, sizes[t]), dst_ref=arrived,
                    send_sem=send_a.at[t * AG_FIRST + 1 + j], recv_sem=recv_a.at[t * AG_FIRST + 1 + j],
                    device_id=src_dev, device_id_type=MESH).wait_recv()
                pltpu.make_async_remote_copy(src_ref=arrived, dst_ref=arrived, send_sem=send_f.at[t * AG_CHIPS + j],
                                             recv_sem=recv_f.at[t * AG_CHIPS + j], device_id=_sibling(),
                                             device_id_type=MESH).start()
        token[...] = jnp.zeros_like(token)

    res = pl.pallas_call(
        body, name=name,
        out_shape=(pltpu.SemaphoreType.DMA((nt * AG_CHIPS,)), pltpu.SemaphoreType.DMA((nt * AG_CHIPS,)),
                   *[pltpu.HBM(l.shape, l.dtype) for l in lands], jax.ShapeDtypeStruct((8, LANES), F32)),
        in_specs=[_HBM_SPEC] * nt + [_SEM_SPEC, _SEM_SPEC, _ANY_SPEC],
        out_specs=(_SEM_SPEC, _SEM_SPEC, *[_HBM_SPEC] * nt, pl.BlockSpec(memory_space=pltpu.VMEM)),
        input_output_aliases={t: 2 + t for t in range(nt)},
        compiler_params=pltpu.CompilerParams(has_side_effects=_DATAFLOW),
    )(*lands, send_a, recv_a, after)
    return res[0], res[1], list(res[2:2 + nt]), res[-1]


def _gather_wait(send_a, recv_a, send_f, recv_f, lands, axes, after, *, name):
    nt = len(lands)
    sizes = [l.shape[a] // N_DEV for l, a in zip(lands, axes)]

    def body(*refs):
        ins, send_a, recv_a, send_f, recv_f = refs[:nt], refs[nt], refs[nt + 1], refs[nt + 2], refs[nt + 3]
        my_idx = _index_of(_my_place())
        sib = _sibling()
        for t in range(nt):
            mine = _slot(ins[t], axes[t], my_idx, sizes[t])
            for k, to in enumerate([sib] + [_chip_peer(j) for j in range(AG_CHIPS)]):
                pltpu.make_async_remote_copy(src_ref=mine, dst_ref=mine, send_sem=send_a.at[t * AG_FIRST + k],
                                             recv_sem=recv_a.at[t * AG_FIRST + k], device_id=to,
                                             device_id_type=MESH).wait_send()
            pltpu.make_async_remote_copy(src_ref=mine, dst_ref=_slot(ins[t], axes[t], _index_of(sib), sizes[t]),
                                         send_sem=send_a.at[t * AG_FIRST], recv_sem=recv_a.at[t * AG_FIRST],
                                         device_id=sib, device_id_type=MESH).wait_recv()
            for j in range(AG_CHIPS):
                sent = _slot(ins[t], axes[t], _index_of(_chip_peer(j)), sizes[t])
                got = _slot(ins[t], axes[t], _index_of(_chip_peer(j, same_core=False)), sizes[t])
                cp = pltpu.make_async_remote_copy(src_ref=sent, dst_ref=got, send_sem=send_f.at[t * AG_CHIPS + j],
                                                  recv_sem=recv_f.at[t * AG_CHIPS + j], device_id=sib,
                                                  device_id_type=MESH)
                cp.wait_send()
                cp.wait_recv()

    res = pl.pallas_call(
        body, name=name, out_shape=[pltpu.HBM(l.shape, l.dtype) for l in lands],
        in_specs=[_HBM_SPEC] * nt + [_SEM_SPEC] * 4 + [_ANY_SPEC], out_specs=[_HBM_SPEC] * nt,
        input_output_aliases={t: t for t in range(nt)},
        compiler_params=pltpu.CompilerParams(has_side_effects=_DATAFLOW),
    )(*lands, send_a, recv_a, send_f, recv_f, after)
    return list(res)


def _scatter_start(dw, axis, *, name):
    size = dw.shape[axis] // N_DEV
    land_shape = (N_PEERS, size, dw.shape[1]) if axis == 0 else (N_PEERS, dw.shape[0], size)

    def body(dw_ref, land_ref, send_sems, recv_sems, dw_thru, land_thru, token):
        for r in range(1, N_DEV):
            p = _peer(r)
            pltpu.make_async_remote_copy(src_ref=_slot(dw_ref, axis, _index_of(p), size), dst_ref=land_ref.at[r - 1],
                                         send_sem=send_sems.at[r - 1], recv_sem=recv_sems.at[r - 1], device_id=p,
                                         device_id_type=MESH).start()
        token[...] = jnp.zeros_like(token)

    land = pltpu.with_memory_space_constraint(lax.empty(land_shape, dw.dtype), pltpu.HBM)
    return pl.pallas_call(
        body, name=name,
        out_shape=(pltpu.SemaphoreType.DMA((N_PEERS,)), pltpu.SemaphoreType.DMA((N_PEERS,)),
                   pltpu.HBM(dw.shape, dw.dtype), pltpu.HBM(land_shape, dw.dtype), jax.ShapeDtypeStruct((8, LANES), F32)),
        in_specs=[_HBM_SPEC, _HBM_SPEC],
        out_specs=(_SEM_SPEC, _SEM_SPEC, _HBM_SPEC, _HBM_SPEC, pl.BlockSpec(memory_space=pltpu.VMEM)),
        input_output_aliases={0: 2, 1: 3},
        compiler_params=pltpu.CompilerParams(has_side_effects=_DATAFLOW),
    )(pltpu.with_memory_space_constraint(dw, pltpu.HBM), land)


def _scatter_wait(send_sems, recv_sems, dw, land, axis, after, *, name):
    size = dw.shape[axis] // N_DEV

    def body(dw_ref, land_ref, send_sems, recv_sems, after_ref, dw_thru, land_thru):
        for r in range(1, N_DEV):
            p = _peer(r)
            cp = pltpu.make_async_remote_copy(src_ref=_slot(dw_ref, axis, _index_of(p), size), dst_ref=land_ref.at[r - 1],
                                              send_sem=send_sems.at[r - 1], recv_sem=recv_sems.at[r - 1], device_id=p,
                                              device_id_type=MESH)
            cp.wait_send()
            cp.wait_recv()

    return pl.pallas_call(
        body, name=name, out_shape=(pltpu.HBM(dw.shape, dw.dtype), pltpu.HBM(land.shape, land.dtype)),
        in_specs=[_HBM_SPEC, _HBM_SPEC, _SEM_SPEC, _SEM_SPEC, _ANY_SPEC], out_specs=(_HBM_SPEC, _HBM_SPEC),
        input_output_aliases={0: 0, 1: 1},
        compiler_params=pltpu.CompilerParams(has_side_effects=_DATAFLOW),
    )(dw, land, send_sems, recv_sems, after)


def _adamw_math(w, g, m, v):
    m = ADAM_B1 * m + (1.0 - ADAM_B1) * g
    v = ADAM_B2 * v + (1.0 - ADAM_B2) * (g * g)
    m_hat = m / (1.0 - ADAM_B1 ** ADAM_STEP)
    v_hat = v / (1.0 - ADAM_B2 ** ADAM_STEP)
    delta = -ADAM_LR * (m_hat / (jnp.sqrt(v_hat) + ADAM_EPS) + ADAM_WD * w)
    return delta, m, v


def _adamw_sharded(w, m, v, layer, dw, land, axis, me_arr, prev, *, name):
    nl, Ks, Ns = w.shape
    tr = _pick(Ks, 128, 16)
    nblk = Ks // tr
    if axis == 0:
        own_map = lambda i, me: (me[0] * nblk + i, 0)
    else:
        own_map = lambda i, me: (i, me[0])
    wspec = pl.BlockSpec((None, tr, Ns), lambda i, me: (layer, i, 0))
    n_prev = 0 if prev is None else 4

    def body(me_ref, w_ref, m_ref, v_ref, own_ref, r_ref, *rest):
        g_ref, d_ref, nm_ref, nv_ref = rest[n_prev:]
        g = own_ref[...].astype(F32)
        for r in range(N_PEERS):
            g = g + r_ref[r].astype(F32)
        delta, nm, nv = _adamw_math(w_ref[...], g, m_ref[...], v_ref[...])
        g_ref[...], d_ref[...], nm_ref[...], nv_ref[...] = g, delta, nm, nv

    return pl.pallas_call(
        body, name=name, out_shape=[jax.ShapeDtypeStruct((nl, Ks, Ns), F32)] * 4,
        grid_spec=pltpu.PrefetchScalarGridSpec(
            num_scalar_prefetch=1, grid=(nblk,),
            in_specs=[wspec, wspec, wspec, pl.BlockSpec((tr, Ns), own_map),
                      pl.BlockSpec((N_PEERS, tr, Ns), lambda i, me: (0, i, 0))] + [_ANY_SPEC] * n_prev,
            out_specs=[wspec] * 4),
        input_output_aliases={6 + k: k for k in range(n_prev)},
        compiler_params=_cparams("parallel"),
    )(me_arr, w, m, v, dw, land, *(prev or []))


def _adamw_flat(w, g, m, v, *, name):
    def body(w_ref, g_ref, m_ref, v_ref, d_ref, nm_ref, nv_ref):
        d_ref[...], nm_ref[...], nv_ref[...] = _adamw_math(w_ref[...], g_ref[...], m_ref[...], v_ref[...])

    spec = pl.BlockSpec(memory_space=pltpu.VMEM)
    return pl.pallas_call(body, name=name, in_specs=[spec] * 4, out_specs=[spec] * 3,
                          out_shape=[jax.ShapeDtypeStruct(w.shape, F32)] * 3)(w, g, m, v)


def _sum_devices(a, *, name):
    def body(a_ref, o_ref):
        s = a_ref[0]
        for d in range(1, N_DEV):
            s = s + a_ref[d]
        o_ref[...] = s

    spec = pl.BlockSpec(memory_space=pltpu.VMEM)
    return pl.pallas_call(body, name=name, in_specs=[spec], out_specs=spec,
                          out_shape=jax.ShapeDtypeStruct(a.shape[1:], F32))(a)


def _ada_mods(c16, ada_w, ada_b_cols, *, name):
    nl, D, cols = ada_w.shape
    bn = _pick(cols, 512)

    def body(c_ref, w_ref, b_ref, o_ref):
        cond = _silu(c_ref[...]).astype(BF16)
        o_ref[...] = _dot(cond, w_ref[...].astype(BF16)) + b_ref[...]

    return pl.pallas_call(
        body, name=name, grid=(nl, cols // bn),
        in_specs=[pl.BlockSpec((16, D), lambda l, j: (0, 0)), pl.BlockSpec((None, D, bn), lambda l, j: (l, 0, j)),
                  pl.BlockSpec((None, 1, bn), lambda l, j: (l, 0, j))],
        out_specs=pl.BlockSpec((None, 16, bn), lambda l, j: (l, 0, j)),
        out_shape=jax.ShapeDtypeStruct((nl, 16, cols), F32),
        compiler_params=_cparams("parallel", "parallel"),
    )(c16, ada_w, ada_b_cols)


def _ada_bwd(cond_t, dmod, w, m, v, *, name):
    nl, D, cols = w.shape
    tr = _pick(D, 256, 8)

    def body(ct_ref, dm_ref, w_ref, m_ref, v_ref, g_ref, d_ref, nm_ref, nv_ref, dc_ref):
        ct, dm, wt = ct_ref[...], dm_ref[...], w_ref[...]
        g = ct[:, 0:1] * dm[0:1, :]
        for r in range(1, N_DEV + 1):
            g = g + ct[:, r:r + 1] * dm[r:r + 1, :]
        delta, nm, nv = _adamw_math(wt, g, m_ref[...], v_ref[...])
        g_ref[...], d_ref[...], nm_ref[...], nv_ref[...] = g, delta, nm, nv
        dc_ref[...] = jnp.sum(wt * dm[N_DEV:N_DEV + 1, :], axis=-1, keepdims=True)

    wspec = pl.BlockSpec((None, tr, cols), lambda l, i: (l, i, 0))
    return pl.pallas_call(
        body, name=name, grid=(nl, D // tr),
        in_specs=[pl.BlockSpec((tr, 16), lambda l, i: (i, 0)), pl.BlockSpec((None, 16, cols), lambda l, i: (l, 0, 0)),
                  wspec, wspec, wspec],
        out_specs=[wspec] * 4 + [pl.BlockSpec((None, tr, 1), lambda l, i: (l, i, 0))],
        out_shape=[jax.ShapeDtypeStruct((nl, D, cols), F32)] * 4 + [jax.ShapeDtypeStruct((nl, D, 1), F32)],
        compiler_params=_cparams("parallel", "parallel"),
    )(cond_t, dmod, w, m, v)


def _rope_tables(L, CTX):
    def angles(pos, dim):
        inv_freq = ROPE_BASE ** (-jnp.arange(0, dim, 2, dtype=F32) / dim)
        return pos.astype(F32)[:, None] * inv_freq[None, :]

    def pad(cos, sin):
        return (jnp.concatenate([cos, jnp.ones((CTX, LANES), F32)], 0),
                jnp.concatenate([sin, jnp.zeros((CTX, LANES), F32)], 0))

    ret = angles(jnp.arange(L), 2 * LANES)
    ret_cs = pad(jnp.cos(ret), jnp.sin(ret))
    rows = angles(jnp.arange(L) // GRID_W, ATT_HEAD_DIM // 2)
    cols = angles(jnp.arange(L) % GRID_W, ATT_HEAD_DIM // 2)
    cos = jnp.concatenate([jnp.cos(rows)] * 2 + [jnp.cos(cols)] * 2, axis=1)
    sin = jnp.concatenate([-jnp.sin(rows), jnp.sin(rows), -jnp.sin(cols), jnp.sin(cols)], axis=1)
    return ret_cs, pad(cos, sin)


def kernel(x, c, ctx, c_ctx, ada_w, ada_b, norm_mix_g, norm_mlp_g, mlp_w1, mlp_w2, ret_w_in, ret_w_out, ret_decay_fwd, ret_decay_bwd, attn_w_in, attn_w_out, attn_sink, final_norm_g, loss_target, m_c_ctx, m_ada_w, m_ada_b, m_norm_mix_g, m_norm_mlp_g, m_mlp_w1, m_mlp_w2, m_ret_w_in, m_ret_w_out, m_ret_decay_fwd, m_ret_decay_bwd, m_attn_w_in, m_attn_w_out, m_attn_sink, m_final_norm_g, v_c_ctx, v_ada_w, v_ada_b, v_norm_mix_g, v_norm_mlp_g, v_mlp_w1, v_mlp_w2, v_ret_w_in, v_ret_w_out, v_ret_decay_fwd, v_ret_decay_bwd, v_attn_w_in, v_attn_w_out, v_attn_sink, v_final_norm_g):
    L, D = x.shape[1], x.shape[2]
    CTX = ctx.shape[1]
    T = L + CTX
    RH = ret_decay_fwd.shape[-1]
    assert D == RH * 2 * LANES and ada_w.shape[0] == 2 and ret_w_in.shape[0] == 1 and attn_w_in.shape[0] == 1
    Hq = attn_sink.shape[-1]
    Hkv = (attn_w_in.shape[-1] * N_DEV // ATT_HEAD_DIM - Hq) // 2
    G = Hq // Hkv
    FF = mlp_w1.shape[-1] * N_DEV
    Wq_r, Wv_r = RH * 2 * LANES, RH * 4 * LANES
    acols = ada_w.shape[-1]
    tr = _pick(CTX, 256, 8)
    tr_wide = _pick(CTX, 128, 8)
    bmT = T // 4 if (T % 64 == 0) else T
    bmL = L // 4 if (L % 64 == 0) else L
    x_idx, y_idx, c_idx = lax.axis_index("x"), lax.axis_index("y"), lax.axis_index("c")
    me = 4 * x_idx + 2 * y_idx + c_idx
    me_arr = jnp.reshape(me, (1,)).astype(jnp.int32)

    (rcos, rsin), (acos, asin) = _rope_tables(L, CTX)
    lg_f = jax.nn.log_sigmoid(ret_decay_fwd.astype(F32))
    lg_b = jax.nn.log_sigmoid(ret_decay_bwd.astype(F32))

    c_pad = jnp.concatenate([c.astype(F32), jnp.zeros((7, D), F32)], 0)
    c_all = _all_gather_small(c_pad, name="ag_c")[:, 0, :]
    c16 = jnp.concatenate([c_all, c_ctx[None, :], jnp.zeros((7, D), F32)], 0)
    ada_b_cols = lax.dynamic_slice_in_dim(ada_b, me * acols, acols, axis=1)[:, None, :]
    mods_shard = _ada_mods(c16, ada_w, ada_b_cols, name="ada_mods")
    mods_all = _all_gather_small(mods_shard.reshape(32, acols), name="ag_mods")

    wdefs = {"ret_in": (ret_w_in, 0, 1), "ret_out": (ret_w_out, 0, 0), "w1_0": (mlp_w1, 0, 1), "w2_0": (mlp_w2, 0, 0),
             "attn_in": (attn_w_in, 0, 1), "attn_out": (attn_w_out, 0, 0), "w1_1": (mlp_w1, 1, 1), "w2_1": (mlp_w2, 1, 0)}
    groups = [["ret_in"], ["ret_out", "w1_0", "w2_0"], ["attn_in", "attn_out"], ["w1_1", "w2_1"]]

    placed = {}

    def ag_start(gi, after):
        g_axes = [wdefs[k][2] for k in groups[gi]]
        ssem, rsem, lands, tok_ = _gather_start([placed[k] for k in groups[gi]], g_axes, after, name=f"ag_start{gi}")
        return dict(a=(ssem, rsem), lands=lands, axes=g_axes, gi=gi), tok_

    def ag_forward(g, after):
        fs, fr, g["lands"], tok_ = _gather_forward(*g["a"], g["lands"], g["axes"], after, name=f"ag_forward{g['gi']}")
        g["f"] = (fs, fr)
        return tok_

    def ag_wait(g, after):
        return _gather_wait(*g["a"], *g["f"], g["lands"], g["axes"], after, name=f"ag_wait{g['gi']}")

    placed["ret_in"] = _cast_place(*wdefs["ret_in"], me_arr, mods_all, name="place_ret_in")
    g0, tok = ag_start(0, mods_all)
    last_cast = tok
    for keys in groups[1:]:
        for k in keys:
            last_cast = placed[k] = _cast_place(*wdefs[k], me_arr, last_cast, name=f"place_{k}")
    mods_all = (mods_all + tok[0, 0]).reshape(N_DEV, 2, 16, acols).transpose(1, 2, 0, 3).reshape(2, 16, 6, D)
    mod_lat = lax.dynamic_index_in_dim(mods_all, me, axis=1, keepdims=False)
    mod_ctx = mods_all[:, N_DEV]

    def pack(i, ks, kc):
        return jnp.stack([mod_lat[i, ks], mod_lat[i, kc], mod_ctx[i, ks], mod_ctx[i, kc]], 0)

    def gates(i, k):
        return jnp.stack([mod_lat[i, k], mod_ctx[i, k]], 0)

    def gate_epilogue(x_rows_lat_only):
        def epi(acc, i, j, xt, gv):
            if x_rows_lat_only:
                gate = gv[0:1, :]
            else:
                row = i * acc.shape[0] + lax.broadcasted_iota(jnp.int32, (acc.shape[0], 1), 0)
                gate = jnp.where(row >= L, gv[1:2, :], gv[0:1, :])
            return xt + gate * acc, acc
        return epi

    w1, w2 = {}, {}

    mmT = dict(M=T, bm=bmT)
    mmL = dict(M=L, bm=bmL)

    def bn_of(n, off=0):
        b = MM_BN
        while n % b or off % b:
            b -= LANES
        return b

    X0 = jnp.concatenate([x[0], ctx[0]], axis=0)
    g_mix0, g_mlp0 = norm_mix_g[0:1], norm_mlp_g[0:1]
    g_mix1, g_mlp1 = norm_mix_g[1:2], norm_mlp_g[1:2]
    a0 = _normmod(X0, g_mix0, pack(0, 0, 1), R=T, L=L, tr=tr, name="normmod_mix0", dep=last_cast)
    tok = ag_forward(g0, a0)
    (wr_in,) = ag_wait(g0, tok)
    g1, tok = ag_start(1, wr_in)

    bn_qk = _pick(Wq_r, MM_BN, 2 * LANES)
    nq_blocks = Wq_r // bn_qk
    kscale = float(2 * LANES) ** -0.5

    def rope_epi(acc, i, j, cos, sin):
        parts = []
        for h in range(acc.shape[1] // (2 * LANES)):
            x1 = acc[:, h * 2 * LANES:h * 2 * LANES + LANES]
            x2 = acc[:, h * 2 * LANES + LANES:(h + 1) * 2 * LANES]
            parts += [x1 * cos - x2 * sin, x2 * cos + x1 * sin]
        return (jnp.concatenate(parts, axis=1) * jnp.where(j < nq_blocks, 1.0, kscale),)

    def row_tile(arr, bm):
        return (arr, (bm, LANES), lambda i, j: (i, 0))

    (qk0,) = _mm(a0, wr_in, "nn", [BF16], N=2 * Wq_r, K=D, bn=bn_qk, bk=D, name="ret_qk", epilogue=rope_epi,
                 extras=[row_tile(rcos, bmT), row_tile(rsin, bmT)], dep=tok, **mmT)
    bn_vg = bn_of(2 * Wv_r, 2 * Wq_r)
    (vg0,) = _mm(a0, wr_in, "nn", [BF16], N=2 * Wv_r, K=D, bn=bn_vg, bk=D, name="ret_vg", b_col0=2 * Wq_r, dep=tok,
                 **mmT)

    of, st_f = _ret_fwd(qk0, vg0, lg_f, None, T=T, L=L, H=RH, rev=False, name="ret_scan_f")
    o0, st_b = _ret_fwd(qk0, vg0, lg_b, of, T=T, L=L, H=RH, rev=True, name="ret_scan_b")
    tok = ag_forward(g1, o0)
    z0 = _readout(o0, vg0, T=T, L=L, H=RH, tr=tr_wide, name="ret_readout", dep=tok)
    wr_out, w1[0], w2[0] = ag_wait(g1, z0)
    g2, tok = ag_start(2, wr_out)
    g3, tok = ag_start(3, tok)

    bnD = _pick(D, MM_BN)

    def xtile(arr, bm):
        return (arr, (bm, bnD), lambda i, j: (i, j))

    def gtile(gv):
        return (gv, (2, bnD), lambda i, j: (0, j))

    bk_max = 2048

    def quarter(rows, div=4):
        return dict(M=rows["M"], bm=rows["bm"] // div if rows["bm"] % (16 * div) == 0 else rows["bm"])
    X1, ro0 = _mm(z0, wr_out, "nn", [F32, BF16], N=D, K=Wv_r, bn=bnD, bk=Wv_r, name="ret_out", cols_outer=True,
                  epilogue=gate_epilogue(False), extras=[xtile(X0, quarter(mmT, 2)["bm"]), gtile(gates(0, 2))], dep=tok,
                  **quarter(mmT, 2))

    def mlp_fwd(Xin, i, g_mlp, rows, name):
        a = _normmod(Xin, g_mlp, pack(i, 3, 4), R=rows["M"], L=L, tr=tr, name=f"normmod_mlp{name}")

        def relu2(acc, i_, j_):
            u = jnp.maximum(acc, 0.0)
            return u, u * u

        bnF = _pick(FF, MM_BN)
        u, r = _mm(a, w1[i], "nn", [BF16, BF16], N=FF, K=D, bn=bnF, bk=D, name=f"mlp_up{name}", epilogue=relu2, **rows)
        Xout, mo = _mm(r, w2[i], "nn", [F32, BF16], N=D, K=FF, bn=bnD, bk=FF, name=f"mlp_down{name}",
                       epilogue=gate_epilogue(rows["M"] == L), cols_outer=True,
                       extras=[xtile(Xin, quarter(rows)["bm"]), gtile(gates(i, 5))], **quarter(rows))
        return a, u, r, Xout, mo

    a1, u0, r0, X2, mo0 = mlp_fwd(X1, 0, g_mlp0, mmT, "0")

    tok = ag_forward(g2, X2)
    a2 = _normmod(X2, g_mix1, pack(1, 0, 1), R=T, L=L, tr=tr, name="normmod_mix1", dep=tok)
    wa_in, wa_out = ag_wait(g2, a2)
    Wq_a, Wk_a = Hq * LANES, Hkv * LANES

    def arope_epi(acc, i, j, cos, sin):
        heads = acc.shape[1] // LANES
        return (acc * jnp.tile(cos, (1, heads)) + _swap32(acc) * jnp.tile(sin, (1, heads)),)

    bn_q = _pick(Wq_a, MM_BN)
    (q1,) = _mm(a2, wa_in, "nn", [BF16], N=Wq_a, K=D, bn=bn_q, bk=D, name="attn_q", epilogue=arope_epi,
                extras=[row_tile(acos, bmL), row_tile(asin, bmL)], **mmL)
    bn_k = bn_of(Wk_a, Wq_a)
    (k1,) = _mm(a2, wa_in, "nn", [BF16], N=Wk_a, K=D, bn=bn_k, bk=D, name="attn_k", b_col0=Wq_a, epilogue=arope_epi,
                extras=[row_tile(acos, bmT), row_tile(asin, bmT)], **mmT)
    bn_v = bn_of(Wk_a, Wq_a + Wk_a)
    (v1,) = _mm(a2, wa_in, "nn", [BF16], N=Wk_a, K=D, bn=bn_v, bk=D, name="attn_v", b_col0=Wq_a + Wk_a, **mmT)
    tok = ag_forward(g3, q1)
    o1, lse = _attn_fwd(q1, k1, v1, attn_sink + tok[0, 0], L=L, CTX=CTX, Hkv=Hkv, G=G, name="attn_fwd")
    w1[1], w2[1] = ag_wait(g3, o1)
    X3, ao = _mm(o1, wa_out, "nn", [F32, BF16], N=D, K=Wq_a, bn=bnD, bk=_pick(Wq_a, 2048), name="attn_out",
                 epilogue=gate_epilogue(True), extras=[xtile(X2, bmL), gtile(gates(1, 2))], **mmL)
    a3, u1, r1, X4, mo1 = mlp_fwd(X3, 1, g_mlp1, mmL, "1")

    dX4, dmo1, acc_head = _loss_head(X4, loss_target[0], mo1, final_norm_g[None, :], gates(1, 5)[0:1], L=L, tr=tr,
                                     name="loss_head")
    loss_part = jnp.sum(acc_head[0, 0])
    d_gf = acc_head[0, 1]
    zeros_d = jnp.zeros((D,), F32)
    dmod_lat = [[zeros_d] * 6, [zeros_d] * 6]
    dmod_ctx = [[zeros_d] * 6, [zeros_d] * 6]
    dmod_lat[1][5] = acc_head[0, 2]

    def dw_mm(a, b, M, N, K, name):
        return _mm(a, b, "tn", [BF16], M=M, N=N, K=K, bm=_pick(M, 512), bn=_pick(N, MM_BN), bk=K, name=name)[0]

    def mlp_bwd(dmo, a, u, r, i, rows, name):
        Mr = rows["M"]

        def times_2u(acc, i_, j_, ut):
            return (acc * (2.0 * ut.astype(F32)),)

        bnF = _pick(FF, MM_BN)
        dw2 = dw_mm(r, dmo, FF, D, Mr, f"mlp_down_dw{name}")
        tok_ = send_grad(f"w2_{i}", dw2, 0)
        (dh,) = _mm(dmo, w2[i], "nt", [BF16], N=FF, K=D, bn=bnF, bk=D, name=f"mlp_down_dx{name}", epilogue=times_2u,
                    extras=[(u, (rows["bm"], bnF), lambda i_, j_: (i_, j_))], dep=tok_, **rows)
        dw1 = dw_mm(a, dh, D, FF, Mr, f"mlp_up_dw{name}")
        tok_ = send_grad(f"w1_{i}", dw1, 1)
        (da,) = _mm(dh, w1[i], "nt", [BF16], N=D, K=FF, bn=bnD, bk=FF, name=f"mlp_up_dx{name}", dep=tok_,
                    cols_outer=True, **quarter(rows))
        return da

    pending = []

    def send_grad(key, dw, axis):
        ssem, rsem, dw_thru, land, tok_ = _scatter_start(dw, axis, name=f"rs_start_{key}")
        pending.append((key, axis, ssem, rsem, dw_thru, land))
        return tok_

    da3 = mlp_bwd(dmo1, a3, u1, r1, 1, mmL, "1")
    dX3, dao, acc = _normmod_bwd(X3, da3, dX4, False, g_mlp1, pack(1, 3, 4), (ao, gates(1, 2)), R=L, L=L, tr=tr,
                                 name="normmod_mlp1_bwd")
    dmod_lat[1][3], dmod_lat[1][4], d_gmlp1, dmod_lat[1][2] = acc[0, 0], acc[0, 1], acc[0, 2], acc[0, 3]

    dwa_out = dw_mm(o1, dao, Wq_a, D, L, "attn_out_dw")
    tok = send_grad("attn_out", dwa_out, 0)
    (do1,) = _mm(dao, wa_out, "nt", [BF16], N=Wq_a, K=D, bn=_pick(Wq_a, MM_BN), bk=D, name="attn_out_dx", dep=tok, **mmL)
    dq1, dk1, dv1, dkx, dvx, dsink_acc = _attn_bwd(q1, k1, v1, do1, lse, attn_sink, L=L, CTX=CTX, Hkv=Hkv, G=G,
                                                   name="attn_bwd")
    dp1 = _attn_assemble(dq1, dk1, dv1, dkx, dvx, acos, asin, T=T, L=L, CTX=CTX, Hq=Hq, Hkv=Hkv, tr=tr_wide,
                         name="attn_assemble")
    Wa_in = Wq_a + 2 * Wk_a
    dwa_in = dw_mm(a2, dp1, D, Wa_in, T, "attn_in_dw")
    tok = send_grad("attn_in", dwa_in, 1)
    (da2,) = _mm(dp1, wa_in, "nt", [BF16], N=D, K=Wa_in, bn=bnD, bk=_pick(Wa_in, 2 * bk_max), name="attn_in_dx",
                 dep=tok, **mmT)
    dX2, dmo0, acc = _normmod_bwd(X2, da2, dX3, True, g_mix1, pack(1, 0, 1), (mo0, gates(0, 5)), R=T, L=L, tr=tr,
                                  name="normmod_mix1_bwd")
    dmod_lat[1][0], dmod_lat[1][1], d_gmix1, dmod_lat[0][5] = acc[0, 0], acc[0, 1], acc[0, 2] + acc[1, 2], acc[0, 3]
    dmod_ctx[1][0], dmod_ctx[1][1], dmod_ctx[0][5] = acc[1, 0], acc[1, 1], acc[1, 3]

    da1 = mlp_bwd(dmo0, a1, u0, r0, 0, mmT, "0")
    dX1, dro0, acc = _normmod_bwd(X1, da1, dX2, False, g_mlp0, pack(0, 3, 4), (ro0, gates(0, 2)), R=T, L=L, tr=tr,
                                  name="normmod_mlp0_bwd")
    dmod_lat[0][3], dmod_lat[0][4], d_gmlp0, dmod_lat[0][2] = acc[0, 0], acc[0, 1], acc[0, 2] + acc[1, 2], acc[0, 3]
    dmod_ctx[0][3], dmod_ctx[0][4], dmod_ctx[0][2] = acc[1, 0], acc[1, 1], acc[1, 3]

    dwr_out = dw_mm(z0, dro0, Wv_r, D, T, "ret_out_dw")
    tok = send_grad("ret_out", dwr_out, 0)
    (dz0,) = _mm(dro0, wr_out, "nt", [BF16], N=Wv_r, K=D, bn=_pick(Wv_r, MM_BN), bk=D, name="ret_out_dx", dep=tok, **mmT)
    do0, dg0 = _readout_bwd(dz0, o0, vg0, T=T, L=L, H=RH, tr=tr_wide, name="ret_readout_bwd")
    dq_f, dk_f, dv_f, dlg_f = _ret_bwd(qk0, vg0, do0, st_f, lg_f, T=T, L=L, H=RH, rev=False, name="ret_scan_f_bwd")
    dq_b, dk_b, dv_b, dlg_b = _ret_bwd(qk0, vg0, do0, st_b, lg_b, T=T, L=L, H=RH, rev=True, name="ret_scan_b_bwd")
    dp0 = _ret_assemble(dq_f, dq_b, dk_f, dk_b, dv_f, dv_b, dg0, rcos, rsin, T=T, L=L, H=RH, tr=tr_wide,
                        name="ret_assemble")
    Wr_in = 2 * Wq_r + 2 * Wv_r
    dwr_in = dw_mm(a0, dp0, D, Wr_in, T, "ret_in_dw")
    tok = send_grad("ret_in", dwr_in, 1)
    (da0,) = _mm(dp0, wr_in, "nt", [BF16], N=D, K=Wr_in, bn=bnD, bk=Wr_in, name="ret_in_dx", dep=tok, cols_outer=True,
                 b_buffers=1, **quarter(mmT))
    dX0, acc = _normmod_bwd(X0, da0, dX1, False, g_mix0, pack(0, 0, 1), None, R=L, L=L, tr=tr, name="normmod_mix0_bwd")
    _, acc_c = _normmod_bwd(X0, da0, dX1, False, g_mix0, pack(0, 0, 1), None, R=CTX, L=0, tr=tr, row0=L,
                            name="normmod_mix0_bwd_ctx")
    dmod_lat[0][0], dmod_lat[0][1], d_gmix0 = acc[0, 0], acc[0, 1], acc[0, 2] + acc_c[1, 2]
    dmod_ctx[0][0], dmod_ctx[0][1] = acc_c[1, 0], acc_c[1, 1]
    grad_x = dX0[None]

    wmv = {"ret_in": (ret_w_in, m_ret_w_in, v_ret_w_in, 0, "ret_w_in"),
           "ret_out": (ret_w_out, m_ret_w_out, v_ret_w_out, 0, "ret_w_out"),
           "attn_in": (attn_w_in, m_attn_w_in, v_attn_w_in, 0, "attn_w_in"),
           "attn_out": (attn_w_out, m_attn_w_out, v_attn_w_out, 0, "attn_w_out"),
           "w1_0": (mlp_w1, m_mlp_w1, v_mlp_w1, 0, "mlp_w1"), "w1_1": (mlp_w1, m_mlp_w1, v_mlp_w1, 1, "mlp_w1"),
           "w2_0": (mlp_w2, m_mlp_w2, v_mlp_w2, 0, "mlp_w2"), "w2_1": (mlp_w2, m_mlp_w2, v_mlp_w2, 1, "mlp_w2")}
    big = {}

    def finish_grad(entry, after):
        key, axis, ssem, rsem, dw_thru, land = entry
        dw_done, land_done = _scatter_wait(ssem, rsem, dw_thru, land, axis, after, name=f"rs_wait_{key}")
        w_, m_, v_, layer, out_name = wmv[key]
        big[out_name] = _adamw_sharded(w_, m_, v_, layer, dw_done, land_done, axis, me_arr, big.get(out_name),
                                       name=f"adamw_{key}")
        return big[out_name][0]

    after = dX0
    for entry in pending[:-1]:
        after = finish_grad(entry, after)

    misc = jnp.zeros((D,), F32)
    misc = misc.at[0:RH].set(dlg_f[:, 0, 0]).at[RH:2 * RH].set(dlg_b[:, 0, 0])
    misc = misc.at[2 * RH:2 * RH + Hq].set(dsink_acc[:, :G, 0].reshape(Hq)).at[2 * RH + Hq].set(loss_part)
    rows = ([dmod_lat[i][k] for i in range(2) for k in range(6)] + [dmod_ctx[i][k] for i in range(2) for k in range(6)]
            + [d_gmix0, d_gmix1, d_gmlp0, d_gmlp1, d_gf, misc, zeros_d, zeros_d])
    part = jnp.stack(rows, 0)
    part_all = _all_gather_small(part, name="ag_small_grads")
    tot = _sum_devices(part_all, name="sum_small_grads")

    grad_ada_b = (tot[0:12] + tot[12:24]).reshape(2, 6 * D)
    grad_norm_mix_g, grad_norm_mlp_g, grad_final_norm_g = tot[24:26], tot[26:28], tot[28]
    grad_ret_decay_fwd = (tot[29, 0:RH] * jax.nn.sigmoid(-ret_decay_fwd[0]))[None]
    grad_ret_decay_bwd = (tot[29, RH:2 * RH] * jax.nn.sigmoid(-ret_decay_bwd[0]))[None]
    grad_attn_sink = tot[29, 2 * RH:2 * RH + Hq][None]
    loss = tot[29, 2 * RH + Hq]

    dlat_cols = lax.dynamic_slice_in_dim(part_all[:, 0:12].reshape(N_DEV, 2, 6 * D), me * acols, acols, axis=2)
    dctx_cols = lax.dynamic_slice_in_dim(tot[12:24].reshape(2, 6 * D), me * acols, acols, axis=1)
    dmod16 = jnp.concatenate([dlat_cols.transpose(1, 0, 2), dctx_cols[:, None, :], jnp.zeros((2, 7, acols), F32)], 1)
    cond_t = _silu(c16).T
    g_ada, d_ada, nm_ada, nv_ada, dcond_part = _ada_bwd(cond_t, dmod16, ada_w, m_ada_w, v_ada_w, name="ada_bwd")
    dcond = (dcond_part[0, :, 0] + dcond_part[1, :, 0]).reshape(D // LANES, LANES)
    pad_rows = -(D // LANES) % 8
    dcond_pad = jnp.concatenate([dcond, jnp.zeros((pad_rows, LANES), F32)], 0) if pad_rows else dcond
    dcond_all = _all_gather_small(dcond_pad, name="ag_dcond")
    dcond_tot = _sum_devices(dcond_all, name="sum_dcond")[:D // LANES].reshape(D)
    sg = jax.nn.sigmoid(c_ctx)
    grad_c_ctx = dcond_tot * (sg * (1.0 + c_ctx * (1.0 - sg)))

    small_w = [c_ctx, ada_b, norm_mix_g, norm_mlp_g, ret_decay_fwd, ret_decay_bwd, attn_sink, final_norm_g]
    small_g = [grad_c_ctx, grad_ada_b, grad_norm_mix_g, grad_norm_mlp_g, grad_ret_decay_fwd, grad_ret_decay_bwd,
               grad_attn_sink, grad_final_norm_g]
    small_m = [m_c_ctx, m_ada_b, m_norm_mix_g, m_norm_mlp_g, m_ret_decay_fwd, m_ret_decay_bwd, m_attn_sink,
               m_final_norm_g]
    small_v = [v_c_ctx, v_ada_b, v_norm_mix_g, v_norm_mlp_g, v_ret_decay_fwd, v_ret_decay_bwd, v_attn_sink,
               v_final_norm_g]
    sizes = [w_.size for w_ in small_w]
    total = sum(-(-s // LANES) * LANES for s in sizes)
    total_pad = -(-total // (8 * LANES)) * 8 * LANES

    def flat_pack(ts, fill):
        pieces = []
        for t_ in ts:
            f = t_.reshape(-1).astype(F32)
            pad = -f.size % LANES
            pieces.append(jnp.concatenate([f, jnp.full((pad,), fill, F32)]) if pad else f)
        pieces.append(jnp.full((total_pad - total,), fill, F32))
        return jnp.concatenate(pieces).reshape(total_pad // LANES, LANES)

    d_s, nm_s, nv_s = _adamw_flat(flat_pack(small_w, 0.0), flat_pack(small_g, 0.0), flat_pack(small_m, 0.0),
                                  flat_pack(small_v, 1.0), name="adamw_small")

    def unpack(p):
        flat = p.reshape(-1)
        res, off = [], 0
        for w_, s in zip(small_w, sizes):
            res.append(flat[off:off + s].reshape(w_.shape))
            off += -(-s // LANES) * LANES
        return res

    finish_grad(pending[-1], d_s)
    d_small, nm_small, nv_small = unpack(d_s), unpack(nm_s), unpack(nv_s)
    small_names = ["c_ctx", "ada_b", "norm_mix_g", "norm_mlp_g", "ret_decay_fwd", "ret_decay_bwd", "attn_sink",
                   "final_norm_g"]
    sm = {n: (g_, d_, m_, v_) for n, g_, d_, m_, v_ in zip(small_names, small_g, d_small, nm_small, nv_small)}

    def out4(n):
        if n == "ada_w":
            return g_ada, d_ada, nm_ada, nv_ada
        if n in big:
            return tuple(big[n])
        return sm[n]

    order = ["c_ctx", "ada_w", "ada_b", "norm_mix_g", "norm_mlp_g", "mlp_w1", "mlp_w2", "ret_w_in", "ret_w_out",
             "ret_decay_fwd", "ret_decay_bwd", "attn_w_in", "attn_w_out", "attn_sink", "final_norm_g"]
    quads = [out4(n) for n in order]
    return (loss, grad_x, *[q_[0] for q_ in quads], *[q_[1] for q_ in quads], *[q_[2] for q_ in quads],
            *[q_[3] for q_ in quads])
```

```python
import functools

import jax
import jax.numpy as jnp
from jax import lax
from jax.experimental import pallas as pl
from jax.experimental.pallas import tpu as pltpu

F32 = jnp.float32
BF16 = jnp.bfloat16

N_DEV = 8
NORM_EPS = 1e-6
CHUNK = 128
ATT_HEAD_DIM = 128
GRID_W = 64
ROPE_BASE = 10000.0
NEG_INF = -1e30
ADAM_LR, ADAM_B1, ADAM_B2, ADAM_EPS, ADAM_WD, ADAM_STEP = 0.001, 0.9, 0.999, 1e-08, 0.01, 10

V7X_VMEM_LIMIT_BYTES = 56 * 1024 * 1024
MM_BN = 1024
LANES = 128
MESH = pl.DeviceIdType.MESH

_NN = (((1,), (0,)), ((), ()))
_NT = (((1,), (1,)), ((), ()))
_TN = (((0,), (0,)), ((), ()))


def _dot(a, b, dn=_NN):
    return lax.dot_general(a, b, dn, preferred_element_type=F32)


def _cparams(*sem):
    return pltpu.CompilerParams(dimension_semantics=sem, vmem_limit_bytes=V7X_VMEM_LIMIT_BYTES)


def _pick(n, pref, mult=LANES):
    if n <= pref:
        return n
    best = None
    for d in range(mult, pref + 1, mult):
        if n % d == 0:
            best = d
    assert best is not None, (n, pref)
    return best


def _silu(x):
    return x * jax.nn.sigmoid(x)


def _mm(a, b, mode, out_dtypes, *, M, N, K, bm, bn, bk, name, b_col0=0, epilogue=None, extras=(), dep=None,
        cols_outer=False, b_buffers=None):
    assert M % bm == 0 and N % bn == 0 and K % bk == 0 and b_col0 % bn == 0, (name, M, N, K, bm, bn, bk, b_col0)
    nk = K // bk
    c0 = b_col0 // bn
    ax_i, ax_j = (1, 0) if cols_outer else (0, 1)

    def spec(block, f, **kw):
        if cols_outer:
            return pl.BlockSpec(block, lambda j, i, k: f(i, j, k), **kw)
        return pl.BlockSpec(block, f, **kw)

    b_kw = {} if b_buffers is None else dict(pipeline_mode=pl.Buffered(b_buffers))
    if mode == "nn":
        a_spec = spec((bm, bk), lambda i, j, k: (i, k))
        b_spec = spec((bk, bn), lambda i, j, k: (k, j + c0), **b_kw)
    elif mode == "nt":
        a_spec = spec((bm, bk), lambda i, j, k: (i, k))
        b_spec = spec((bn, bk), lambda i, j, k: (j + c0, k), **b_kw)
    else:
        a_spec = spec((bk, bm), lambda i, j, k: (k, i))
        b_spec = spec((bk, bn), lambda i, j, k: (k, j + c0), **b_kw)
    dn = {"nn": _NN, "nt": _NT, "tn": _TN}[mode]
    e_specs = [spec(bs, (lambda i, j, k, f=f: f(i, j))) for (_, bs, f) in extras]
    ne, no = len(extras), len(out_dtypes)
    nd = 0 if dep is None else 1

    def body(a_ref, b_ref, *rest):
        e_refs, o_refs = rest[:ne], rest[ne + nd:ne + nd + no]
        i, j, k = pl.program_id(ax_i), pl.program_id(ax_j), pl.program_id(2)

        def finish(acc):
            outs = (acc,) if epilogue is None else epilogue(acc, i, j, *[e[...] for e in e_refs])
            for o_ref, o in zip(o_refs, outs):
                o_ref[...] = o.astype(o_ref.dtype)

        p = _dot(a_ref[...], b_ref[...], dn)
        if nk == 1:
            finish(p)
        else:
            acc_ref = rest[-1]

            @pl.when(k == 0)
            def _():
                acc_ref[...] = p

            @pl.when(k > 0)
            def _():
                acc_ref[...] += p

            @pl.when(k == nk - 1)
            def _():
                finish(acc_ref[...])

    outs = pl.pallas_call(
        body, name=name, grid=(N // bn, M // bm, nk) if cols_outer else (M // bm, N // bn, nk),
        in_specs=[a_spec, b_spec] + e_specs + [pl.BlockSpec(memory_space=pl.ANY)] * nd,
        out_specs=[spec((bm, bn), lambda i, j, k: (i, j)) for _ in out_dtypes],
        out_shape=[jax.ShapeDtypeStruct((M, N), dt) for dt in out_dtypes],
        scratch_shapes=[pltpu.VMEM((bm, bn), F32)] if nk > 1 else [],
        compiler_params=_cparams("parallel", "parallel", "arbitrary"),
    )(a, b, *[e[0] for e in extras], *([dep] if nd else []))
    return outs


def _rowwise(body, rows, vecs, outs, n_acc, *, R, L, tr, name, acc_width=None, dep=None, row0=0):
    assert row0 % tr == 0
    b0 = row0 // tr
    assert R % tr == 0 and L % tr == 0, (name, R, L, tr)
    nl = L // tr
    n_regions = 2 if R > L else 1
    n_rows, n_vecs, n_outs = len(rows), len(vecs), len(outs)
    deps = () if dep is None else (tuple(dep) if isinstance(dep, (tuple, list)) else (dep,))
    n_dep = len(deps)
    acc_pad = -(-n_acc // 8) * 8 if n_acc else 0

    in_specs = []
    for (_, w, cb, lat_only) in rows:
        if lat_only:
            in_specs.append(pl.BlockSpec((tr, w), lambda i, cb=cb: (jnp.minimum(i, nl - 1), cb)))
        else:
            in_specs.append(pl.BlockSpec((tr, w), lambda i, cb=cb: (i + b0, cb)))
    for v in vecs:
        in_specs.append(pl.BlockSpec(v.shape, lambda i, nd=v.ndim: (0,) * nd))
    in_specs += [pl.BlockSpec(memory_space=pl.ANY)] * n_dep
    out_specs = [pl.BlockSpec((tr, w), lambda i: (i, 0)) for (w, _) in outs]
    out_shape = [jax.ShapeDtypeStruct((R, w), dt) for (w, dt) in outs]
    if n_acc:
        out_specs.append(pl.BlockSpec((None, acc_pad, acc_width), lambda i: (jnp.where(i >= nl, 1, 0), 0, 0)))
        out_shape.append(jax.ShapeDtypeStruct((n_regions, acc_pad, acc_width), F32))

    def kern(*refs):
        i = pl.program_id(0)
        is_ctx = i >= nl
        ins = [r[...] for r in refs[:n_rows + n_vecs]]
        o_refs = refs[n_rows + n_vecs + n_dep:]
        out_tiles, acc_rows = body(is_ctx, *ins)
        for o_ref, o in zip(o_refs[:n_outs], out_tiles):
            o_ref[...] = o.astype(o_ref.dtype)
        if n_acc:
            acc_ref = o_refs[n_outs]

            @pl.when((i == 0) | (i == nl))
            def _():
                acc_ref[...] = jnp.zeros_like(acc_ref)

            for r, row in enumerate(acc_rows):
                acc_ref[r:r + 1, :] += row

    res = pl.pallas_call(
        kern, name=name, grid=(R // tr,), in_specs=in_specs, out_specs=out_specs, out_shape=out_shape,
        compiler_params=_cparams("arbitrary"),
    )(*[r[0] for r in rows], *vecs, *deps)
    return res


def _colsum(x):
    return jnp.sum(x, axis=0, keepdims=True)


def _rms_stats(x):
    r = lax.rsqrt(jnp.mean(x * x, axis=-1, keepdims=True) + NORM_EPS)
    return x * r, r


def _sel(is_ctx, pk, lat_row, ctx_row):
    return jnp.where(is_ctx, pk[ctx_row:ctx_row + 1, :], pk[lat_row:lat_row + 1, :])


def _normmod(x, g, pk, *, R, L, tr, name, dep=None):
    D = x.shape[-1]

    def body(is_ctx, xt, gv, pkv):
        xh, _ = _rms_stats(xt)
        sh, sc = _sel(is_ctx, pkv, 0, 2), _sel(is_ctx, pkv, 1, 3)
        return ((xh * gv) * (1.0 + sc) + sh,), ()

    return _rowwise(body, [(x, D, 0, False)], [g, pk], [(D, BF16)], 0, R=R, L=L, tr=tr, name=name, dep=dep)[0]


def _normmod_bwd(x_in, da, dx_out, dx_out_lat_only, g, pk, prev, *, R, L, tr, name, row0=0):
    D = x_in.shape[-1]
    has_prev = prev is not None

    def body(is_ctx, *t):
        if has_prev:
            xt, dat, dxo, mp, gv, pkv, gates = t
        else:
            xt, dat, dxo, gv, pkv = t
        xh, r = _rms_stats(xt)
        dat = dat.astype(F32)
        sc = _sel(is_ctx, pkv, 1, 3)
        if dx_out_lat_only:
            dxo = jnp.where(is_ctx, 0.0, dxo)
        dn = dat * (1.0 + sc)
        w = dn * gv
        dxi = dxo + r * (w - xh * jnp.mean(w * xh, axis=-1, keepdims=True))
        accs = [_colsum(dat), _colsum(dat * (xh * gv)), _colsum(dn * xh)]
        outs = [dxi]
        if has_prev:
            gate = _sel(is_ctx, gates, 0, 1)
            outs.append(dxi * gate)
            accs.append(_colsum(dxi * mp.astype(F32)))
        return outs, accs

    rows = [(x_in, D, 0, False), (da, D, 0, False), (dx_out, D, 0, dx_out_lat_only)]
    vecs = [g, pk]
    outs = [(D, F32)]
    if has_prev:
        rows.append((prev[0], D, 0, False))
        vecs.append(prev[1])
        outs.append((D, BF16))
    return _rowwise(body, rows, vecs, outs, 4 if has_prev else 3, R=R, L=L, tr=tr, name=name, acc_width=D, row0=row0)


def _loss_head(x4, target, m_prev, gf, gate, *, L, tr, name):
    D = x4.shape[-1]

    def body(is_ctx, xt, tg, mp, gfv, gatev):
        xh, r = _rms_stats(xt)
        e = xh * gfv - tg
        dy = e * (1.0 / D)
        w = dy * gfv
        dx = r * (w - xh * jnp.mean(w * xh, axis=-1, keepdims=True))
        accs = [_colsum(e * e) * (0.5 / D), _colsum(dy * xh), _colsum(dx * mp.astype(F32))]
        return (dx, dx * gatev), accs

    return _rowwise(body, [(x4, D, 0, False), (target, D, 0, False), (m_prev, D, 0, False)], [gf, gate],
                    [(D, F32), (D, BF16)], 3, R=L, L=L, tr=tr, name=name, acc_width=D)


RET_CHUNK = 2 * LANES
RET_HEADS_PER_STEP = 4


def _decays(lgh, rev):
    C = RET_CHUNK
    ii = lax.broadcasted_iota(jnp.int32, (C, C), 0)
    jj = lax.broadcasted_iota(jnp.int32, (C, C), 1)
    ri = lax.broadcasted_iota(jnp.int32, (C, 1), 0).astype(F32)
    diff = (jj - ii if rev else ii - jj)
    amat = jnp.where(diff >= 0, jnp.exp(lgh * jnp.maximum(diff, 0).astype(F32)), 0.0)
    pos = (C - ri) if rev else (ri + 1.0)
    bq = jnp.exp(lgh * pos)
    bk = jnp.exp(lgh * (C - pos))
    return amat, bq, bk, pos


def _ret_geometry(T, L, H, rev, backward):
    C = RET_CHUNK
    assert T % C == 0 and L % C == 0, (T, L)
    nT, nL = T // C, L // C
    hb = RET_HEADS_PER_STEP if H % RET_HEADS_PER_STEP == 0 else 1

    def step(s):
        return (nT - 1 - s) if backward else s

    def chunk(s):
        s = step(s)
        return (nT - 1 - s) if rev else (s + nL) % nT

    return C, nT, hb, chunk, step


def _ret_fwd(qk, vg, lg, other, *, T, L, H, rev, name):
    dk, dv = 2 * LANES, 4 * LANES
    C, nT, hb, chunk, step = _ret_geometry(T, L, H, rev, False)
    n_other = 0 if other is None else 1

    def body(lg_ref, q_ref, k_ref, v_ref, *rest):
        o_ref, st_ref, s_scr = rest[n_other:]
        hg, s = pl.program_id(0), pl.program_id(1)

        @pl.when(s == 0)
        def _():
            s_scr[...] = jnp.zeros_like(s_scr)

        for hh in range(hb):
            lgh = lg_ref[0, hg * hb + hh]
            amat, bq, bk, _ = _decays(lgh, rev)
            q, k = q_ref[:, hh * dk:(hh + 1) * dk], k_ref[:, hh * dk:(hh + 1) * dk]
            v = v_ref[:, hh * dv:(hh + 1) * dv]
            stb = s_scr[hh].astype(BF16)
            st_ref[hh] = stb
            scores = _dot(q, k, _NT) * amat
            o = _dot(scores.astype(BF16), v) + _dot(q, stb) * bq
            if n_other:
                o = rest[0][:, hh * dv:(hh + 1) * dv] + o
            o_ref[:, hh * dv:(hh + 1) * dv] = o
            kd = (k.astype(F32) * bk).astype(BF16)
            s_scr[hh] = s_scr[hh] * jnp.exp(lgh * C) + _dot(kd, v, _TN)

    vspec = pl.BlockSpec((C, hb * dv), lambda h, s: (chunk(s), h))
    return pl.pallas_call(
        body, name=name, grid=(H // hb, nT),
        in_specs=[pl.BlockSpec(memory_space=pltpu.SMEM),
                  pl.BlockSpec((C, hb * dk), lambda h, s: (chunk(s), h)),
                  pl.BlockSpec((C, hb * dk), lambda h, s: (chunk(s), H // hb + h)), vspec] + [vspec] * n_other,
        out_specs=[vspec, pl.BlockSpec((hb, None, dk, dv), lambda h, s: (h, s, 0, 0))],
        out_shape=[jax.ShapeDtypeStruct((T, H * dv), F32), jax.ShapeDtypeStruct((H, nT, dk, dv), BF16)],
        scratch_shapes=[pltpu.VMEM((hb, dk, dv), F32)],
        compiler_params=_cparams("parallel", "arbitrary"),
    )(lg, qk, qk, vg, *([other] if n_other else []))


def _ret_bwd(qk, vg, do, states, lg, *, T, L, H, rev, name):
    dk, dv = 2 * LANES, 4 * LANES
    C, nT, hb, chunk, step = _ret_geometry(T, L, H, rev, True)

    def body(lg_ref, q_ref, k_ref, v_ref, do_ref, st_ref, dq_ref, dk_ref, dv_ref, dlg_ref, ds_scr):
        hg, s = pl.program_id(0), pl.program_id(1)

        @pl.when(s == 0)
        def _():
            ds_scr[...] = jnp.zeros_like(ds_scr)
            dlg_ref[...] = jnp.zeros_like(dlg_ref)

        for hh in range(hb):
            lgh = lg_ref[0, hg * hb + hh]
            amat, bq, bk, pos = _decays(lgh, rev)
            ksl, vsl = slice(hh * dk, (hh + 1) * dk), slice(hh * dv, (hh + 1) * dv)
            q, k, v, dob = q_ref[:, ksl], k_ref[:, ksl], v_ref[:, vsl], do_ref[:, vsl]
            stb = st_ref[hh]
            ds_new = ds_scr[hh]
            dsb = ds_new.astype(BF16)
            qf, kf = q.astype(F32), k.astype(F32)
            scores = (_dot(q, k, _NT) * amat).astype(BF16)
            dqk = (_dot(dob, v, _NT) * amat).astype(BF16)
            dq = _dot(dqk, k) + _dot(dob, stb, _NT) * bq
            dkk = _dot(dqk, q, _TN) + _dot(v, dsb, _NT) * bk
            kd = (kf * bk).astype(BF16)
            dvv = _dot(scores, dob, _TN) + _dot(kd, dsb)
            dod = (dob.astype(F32) * bq).astype(BF16)
            ds_prev = ds_new * jnp.exp(lgh * C) + _dot(q, dod, _TN)
            ds_scr[hh] = ds_prev
            dq_ref[:, ksl] = dq.astype(dq_ref.dtype)
            dk_ref[:, ksl] = dkk.astype(dk_ref.dtype)
            dv_ref[:, vsl] = dvv.astype(dv_ref.dtype)
            dlg = (jnp.sum(pos * jnp.sum(qf * dq - kf * dkk, axis=-1, keepdims=True))
                   + C * jnp.sum(ds_prev * stb.astype(F32)))
            dlg_ref[hh] += dlg

    qspec = pl.BlockSpec((C, hb * dk), lambda h, s: (chunk(s), h))
    vspec = pl.BlockSpec((C, hb * dv), lambda h, s: (chunk(s), h))
    return pl.pallas_call(
        body, name=name, grid=(H // hb, nT),
        in_specs=[pl.BlockSpec(memory_space=pltpu.SMEM), qspec,
                  pl.BlockSpec((C, hb * dk), lambda h, s: (chunk(s), H // hb + h)), vspec, vspec,
                  pl.BlockSpec((hb, None, dk, dv), lambda h, s: (h, step(s), 0, 0))],
        out_specs=[qspec, qspec, vspec, pl.BlockSpec((hb, 8, LANES), lambda h, s: (h, 0, 0))],
        out_shape=[jax.ShapeDtypeStruct((T, H * dk), BF16), jax.ShapeDtypeStruct((T, H * dk), BF16),
                   jax.ShapeDtypeStruct((T, H * dv), BF16), jax.ShapeDtypeStruct((H, 8, LANES), F32)],
        scratch_shapes=[pltpu.VMEM((hb, dk, dv), F32)],
        compiler_params=_cparams("parallel", "arbitrary"),
    )(lg, qk, qk, vg, do, states)


def _readout(o, vg, *, T, L, H, tr, name, dep=None):
    dv = 4 * LANES
    W = H * dv

    def body(is_ctx, o, g):
        parts = []
        for h in range(H):
            oh = o[:, h * dv:(h + 1) * dv]
            parts.append(oh * lax.rsqrt(jnp.mean(oh * oh, axis=-1, keepdims=True) + NORM_EPS))
        y = jnp.concatenate(parts, axis=1)
        return (_silu(g.astype(F32)) * y,), ()

    return _rowwise(body, [(o, W, 0, False), (vg, W, 1, False)], [], [(W, BF16)], 0,
                    R=T, L=L, tr=tr, name=name, dep=dep)[0]


def _readout_bwd(dz, o, vg, *, T, L, H, tr, name):
    dv = 4 * LANES
    W = H * dv

    def body(is_ctx, dzt, o, g):
        gf = g.astype(F32)
        sg = jax.nn.sigmoid(gf)
        dzf = dzt.astype(F32)
        dy = dzf * (gf * sg)
        ys, dos = [], []
        for h in range(H):
            sl = slice(h * dv, (h + 1) * dv)
            oh, dyh = o[:, sl], dy[:, sl]
            r = lax.rsqrt(jnp.mean(oh * oh, axis=-1, keepdims=True) + NORM_EPS)
            yh = oh * r
            ys.append(yh)
            dos.append(r * (dyh - yh * jnp.mean(dyh * yh, axis=-1, keepdims=True)))
        y = jnp.concatenate(ys, axis=1)
        dg = dzf * y * (sg * (1.0 + gf * (1.0 - sg)))
        return (jnp.concatenate(dos, axis=1), dg), ()

    return _rowwise(body, [(dz, W, 0, False), (o, W, 0, False), (vg, W, 1, False)], [],
                    [(W, BF16), (W, BF16)], 0, R=T, L=L, tr=tr, name=name)


def _ret_assemble(dq_f, dq_b, dk_f, dk_b, dv_f, dv_b, dg, cos, sin, *, T, L, H, tr, name):
    dk, dv = 2 * LANES, 4 * LANES
    Wq, Wv = H * dk, H * dv
    kscale = float(dk) ** -0.5

    def unrope(d, c, s_, scale):
        parts = []
        for h in range(H):
            d1, d2 = d[:, h * dk:h * dk + LANES], d[:, h * dk + LANES:(h + 1) * dk]
            parts += [(d1 * c + d2 * s_) * scale, (d2 * c - d1 * s_) * scale]
        return jnp.concatenate(parts, axis=1)

    def body(is_ctx, qf, qb, kf, kb, vf, vb, g, c, s_):
        add = lambda a, b: a.astype(F32) + b.astype(F32)
        dq = unrope(add(qf, qb), c, s_, 1.0)
        dkk = unrope(add(kf, kb), c, s_, kscale)
        return (jnp.concatenate([dq.astype(BF16), dkk.astype(BF16), add(vf, vb).astype(BF16), g], axis=1),), ()

    rows = [(dq_f, Wq, 0, False), (dq_b, Wq, 0, False), (dk_f, Wq, 0, False), (dk_b, Wq, 0, False),
            (dv_f, Wv, 0, False), (dv_b, Wv, 0, False), (dg, Wv, 0, False),
            (cos, LANES, 0, False), (sin, LANES, 0, False)]
    return _rowwise(body, rows, [], [(2 * Wq + 2 * Wv, BF16)], 0, R=T, L=L, tr=tr, name=name)[0]


def _swap32(x):
    n = x.shape[-1]
    lane = lax.broadcasted_iota(jnp.int32, x.shape, x.ndim - 1)
    return jnp.where(lane % 64 < 32, pltpu.roll(x, n - 32, x.ndim - 1), pltpu.roll(x, 32, x.ndim - 1))


ATT_Q_BLOCKS = 4


def _stack_heads_at(ref, rows, G):
    return jnp.concatenate([ref[rows, g * LANES:(g + 1) * LANES] for g in range(G)], axis=0)


def _stack_columns_at(ref, rows, G):
    return jnp.concatenate([ref[rows, g:g + 1] for g in range(G)], axis=0)


def _sink_column(sink_ref, h, G):
    return jnp.concatenate([jnp.full((CHUNK, 1), sink_ref[0, h * G + g], F32) for g in range(G)], axis=0)


def _key_mask(n, nb, CTX, G):
    W = 3 * CHUNK + CTX
    ii = lax.broadcasted_iota(jnp.int32, (G * CHUNK, W), 0) & (CHUNK - 1)
    col = lax.broadcasted_iota(jnp.int32, (G * CHUNK, W), 1)
    is_prev = col < CHUNK
    is_next = (col >= 2 * CHUNK) & (col < 3 * CHUNK)
    prev_ok = is_prev & (col >= ii) & (n > 0)
    next_ok = is_next & ((col - 2 * CHUNK) <= ii) & (n < nb - 1)
    return prev_ok | next_ok | jnp.logical_not(is_prev | is_next)


def _attn_geometry(L, CTX):
    nb = L // CHUNK
    QB = ATT_Q_BLOCKS if nb % ATT_Q_BLOCKS == 0 else 1

    def blk(j):
        return pl.BlockSpec((CHUNK, LANES), lambda h, m: (jnp.clip(m * QB + j - 1, 0, nb - 1), h))

    kvs = [blk(j) for j in range(QB + 2)] + [pl.BlockSpec((CTX, LANES), lambda h, m: (L // CTX, h))]
    return nb, QB, kvs


def _attn_fwd(q, k, v, sink, *, L, CTX, Hkv, G, name):
    scale = float(ATT_HEAD_DIM) ** -0.5
    nb, QB, kvs = _attn_geometry(L, CTX)
    nkv = QB + 3

    def body(sink_ref, q_ref, *rest):
        kb, vb, (o_ref, lse_ref) = rest[:nkv], rest[nkv:2 * nkv], rest[2 * nkv:]
        h, m_ = pl.program_id(0), pl.program_id(1)
        sk = _sink_column(sink_ref, h, G)
        for sub in range(QB):
            rows = slice(sub * CHUNK, (sub + 1) * CHUNK)
            qs = _stack_heads_at(q_ref, rows, G)
            kall = jnp.concatenate([kb[sub + j][...] for j in range(3)] + [kb[-1][...]], axis=0)
            vall = jnp.concatenate([vb[sub + j][...] for j in range(3)] + [vb[-1][...]], axis=0)
            s_ = jnp.where(_key_mask(m_ * QB + sub, nb, CTX, G), _dot(qs, kall, _NT) * scale, NEG_INF)
            m = jnp.maximum(jnp.max(s_, axis=-1, keepdims=True), sk)
            p = jnp.exp(s_ - m)
            den = jnp.sum(p, axis=-1, keepdims=True) + jnp.exp(sk - m)
            o = _dot(p.astype(BF16), vall) / den
            lse = m + jnp.log(den)
            for g in range(G):
                o_ref[rows, g * LANES:(g + 1) * LANES] = o[g * CHUNK:(g + 1) * CHUNK].astype(o_ref.dtype)
                lse_ref[rows, g:g + 1] = lse[g * CHUNK:(g + 1) * CHUNK]

    qspec = pl.BlockSpec((QB * CHUNK, G * LANES), lambda h, m: (m, h))
    return pl.pallas_call(
        body, name=name, grid=(Hkv, nb // QB),
        in_specs=[pl.BlockSpec(memory_space=pltpu.SMEM), qspec] + kvs + kvs,
        out_specs=[qspec, pl.BlockSpec((None, QB * CHUNK, G), lambda h, m: (h, m, 0))],
        out_shape=[jax.ShapeDtypeStruct((L, Hkv * G * LANES), BF16), jax.ShapeDtypeStruct((Hkv, L, G), F32)],
        compiler_params=_cparams("parallel", "parallel"),
    )(sink, q, *([k] * nkv), *([v] * nkv))


def _attn_bwd(q, k, v, do, lse, sink, *, L, CTX, Hkv, G, name):
    scale = float(ATT_HEAD_DIM) ** -0.5
    nb, QB, kvs = _attn_geometry(L, CTX)
    nkv = QB + 3
    qspec = pl.BlockSpec((QB * CHUNK, G * LANES), lambda h, m: (m, h))
    rowspec = pl.BlockSpec((None, QB * CHUNK, G), lambda h, m: (h, m, 0))
    colspec = lambda rows: pl.BlockSpec((rows, LANES), lambda h, m: (0, h))

    def body(sink_ref, q_ref, do_ref, lse_ref, *rest):
        kb, vb = rest[:nkv], rest[nkv:2 * nkv]
        dq_ref, dk_ref, dv_ref, dkx_ref, dvx_ref, dsk_ref = rest[2 * nkv:]
        h, m_ = pl.program_id(0), pl.program_id(1)

        @pl.when(m_ == 0)
        def _():
            for r in (dk_ref, dv_ref, dkx_ref, dvx_ref, dsk_ref):
                r[...] = jnp.zeros_like(r)

        sk = _sink_column(sink_ref, h, G)
        for sub in range(QB):
            n = m_ * QB + sub
            rows = slice(sub * CHUNK, (sub + 1) * CHUNK)
            qs, dos = _stack_heads_at(q_ref, rows, G), _stack_heads_at(do_ref, rows, G)
            kall = jnp.concatenate([kb[sub + j][...] for j in range(3)] + [kb[-1][...]], axis=0)
            vall = jnp.concatenate([vb[sub + j][...] for j in range(3)] + [vb[-1][...]], axis=0)
            lse_c = _stack_columns_at(lse_ref, rows, G)
            p = jnp.where(_key_mask(n, nb, CTX, G), jnp.exp(_dot(qs, kall, _NT) * scale - lse_c), 0.0)
            dp = _dot(dos, vall, _NT)
            delta = jnp.sum(p * dp, axis=-1, keepdims=True)
            ds = (p * (dp - delta) * scale).astype(BF16)
            dq = _dot(ds, kall)
            dk_all = _dot(ds, qs, _TN)
            dv_all = _dot(p.astype(BF16), dos, _TN)
            for g in range(G):
                dq_ref[rows, g * LANES:(g + 1) * LANES] = dq[g * CHUNK:(g + 1) * CHUNK]
            for part, blk in enumerate((jnp.maximum(n - 1, 0), n, jnp.minimum(n + 1, nb - 1))):
                krows = pl.ds(pl.multiple_of(blk * CHUNK, CHUNK), CHUNK)
                dk_ref[krows, :] += dk_all[part * CHUNK:(part + 1) * CHUNK]
                dv_ref[krows, :] += dv_all[part * CHUNK:(part + 1) * CHUNK]
            dkx_ref[...] += dk_all[3 * CHUNK:]
            dvx_ref[...] += dv_all[3 * CHUNK:]
            dsink = -jnp.exp(sk - lse_c) * delta
            for g in range(G):
                dsk_ref[g:g + 1, :] += jnp.sum(dsink[g * CHUNK:(g + 1) * CHUNK])

    return pl.pallas_call(
        body, name=name, grid=(Hkv, nb // QB),
        in_specs=[pl.BlockSpec(memory_space=pltpu.SMEM), qspec, qspec, rowspec] + kvs + kvs,
        out_specs=[qspec, colspec(L), colspec(L), colspec(CTX), colspec(CTX),
                   pl.BlockSpec((None, 8, LANES), lambda h, m: (h, 0, 0))],
        out_shape=[jax.ShapeDtypeStruct((L, Hkv * G * LANES), F32),
                   jax.ShapeDtypeStruct((L, Hkv * LANES), F32), jax.ShapeDtypeStruct((L, Hkv * LANES), F32),
                   jax.ShapeDtypeStruct((CTX, Hkv * LANES), F32), jax.ShapeDtypeStruct((CTX, Hkv * LANES), F32),
                   jax.ShapeDtypeStruct((Hkv, 8, LANES), F32)],
        compiler_params=_cparams("parallel", "arbitrary"),
    )(sink, q, do, lse, *([k] * nkv), *([v] * nkv))


def _attn_assemble(dq, dk_lat, dv_lat, dk_ctx, dv_ctx, cos, sin, *, T, L, CTX, Hq, Hkv, tr, name):
    Wq, Wk = Hq * LANES, Hkv * LANES
    ctx_blocks = CTX // tr
    nl = L // tr

    def unrope(d, c, s_, heads):
        return d * jnp.tile(c, (1, heads)) + _swap32(d * jnp.tile(s_, (1, heads)))

    def body(is_ctx, dqt, dkl, dvl, dkc, dvc, c, s_):
        dq_ = jnp.where(is_ctx, 0.0, unrope(dqt, c, s_, Hq))
        dk_ = unrope(jnp.where(is_ctx, dkc, dkl), c, s_, Hkv)
        dv_ = jnp.where(is_ctx, dvc, dvl)
        return (jnp.concatenate([dq_, dk_, dv_], axis=1),), ()

    def ctx_map(i):
        return (jnp.clip(i - nl, 0, ctx_blocks - 1), 0)

    assert T % tr == 0 and L % tr == 0 and CTX % tr == 0
    in_specs = [pl.BlockSpec((tr, Wq), lambda i: (jnp.minimum(i, nl - 1), 0)),
                pl.BlockSpec((tr, Wk), lambda i: (jnp.minimum(i, nl - 1), 0)),
                pl.BlockSpec((tr, Wk), lambda i: (jnp.minimum(i, nl - 1), 0)),
                pl.BlockSpec((tr, Wk), ctx_map), pl.BlockSpec((tr, Wk), ctx_map),
                pl.BlockSpec((tr, LANES), lambda i: (i, 0)), pl.BlockSpec((tr, LANES), lambda i: (i, 0))]

    def kern(dq_r, dkl_r, dvl_r, dkc_r, dvc_r, c_r, s_r, o_ref):
        is_ctx = pl.program_id(0) >= nl
        (out,), _ = body(is_ctx, dq_r[...], dkl_r[...], dvl_r[...], dkc_r[...], dvc_r[...], c_r[...], s_r[...])
        o_ref[...] = out.astype(o_ref.dtype)

    return pl.pallas_call(
        kern, name=name, grid=(T // tr,), in_specs=in_specs,
        out_specs=pl.BlockSpec((tr, Wq + 2 * Wk), lambda i: (i, 0)),
        out_shape=jax.ShapeDtypeStruct((T, Wq + 2 * Wk), BF16),
        compiler_params=_cparams("parallel"),
    )(dq, dk_lat, dv_lat, dk_ctx, dv_ctx, cos, sin)


def _my_place():
    x, y, c = lax.axis_index("x"), lax.axis_index("y"), lax.axis_index("c")
    return x, y, c


def _all_gather_small(v, *, name):
    R, C = v.shape

    def body(x_ref, out_ref, send_sems, recv_sems, local_sem):
        x, y, c = _my_place()
        me, sibling = (x, y, c), (x, y, 1 - c)
        chips = [(1 - x, y), (x, 1 - y), (1 - x, 1 - y)]

        def slot(px, py, pc):
            return out_ref.at[4 * px + 2 * py + pc]

        def copy(k, block, to, src=None):
            return pltpu.make_async_remote_copy(
                src_ref=slot(*block) if src is None else src, dst_ref=slot(*block),
                send_sem=send_sems.at[k], recv_sem=recv_sems.at[k], device_id=to, device_id_type=MESH)

        mine = pltpu.make_async_copy(x_ref, slot(*me), local_sem)
        mine.start()
        first = [copy(0, me, sibling, src=x_ref)]
        first += [copy(1 + j, me, (*chip, c), src=x_ref) for j, chip in enumerate(chips)]
        for cp in first:
            cp.start()
        passed = [copy(4 + j, (*chip, c), sibling) for j, chip in enumerate(chips)]
        for j, chip in enumerate(chips):
            copy(1 + j, (*chip, c), me).wait_recv()
            passed[j].start()
        copy(0, sibling, me).wait_recv()
        for j, chip in enumerate(chips):
            copy(4 + j, (*chip, 1 - c), me).wait_recv()
        for cp in first + passed:
            cp.wait_send()
        mine.wait()

    return pl.pallas_call(
        body, name=name, out_shape=jax.ShapeDtypeStruct((N_DEV, R, C), v.dtype),
        in_specs=[pl.BlockSpec(memory_space=pltpu.VMEM)], out_specs=pl.BlockSpec(memory_space=pltpu.VMEM),
        scratch_shapes=[pltpu.SemaphoreType.DMA((7,)), pltpu.SemaphoreType.DMA((7,)), pltpu.SemaphoreType.DMA],
    )(v)


_HBM_SPEC = pl.BlockSpec(memory_space=pltpu.HBM)
_SEM_SPEC = pl.BlockSpec(memory_space=pltpu.SEMAPHORE)
_ANY_SPEC = pl.BlockSpec(memory_space=pl.ANY)
_DATAFLOW = pltpu.SideEffectType.DATAFLOW_SIDE_EFFECTING
N_PEERS = N_DEV - 1


def _peer(r):
    x, y, c = _my_place()
    return ((1 - x) if r & 4 else x, (1 - y) if r & 2 else y, (1 - c) if r & 1 else c)


def _index_of(place):
    return 4 * place[0] + 2 * place[1] + place[2]


def _slot(ref, axis, idx, size):
    if axis == 0:
        return ref.at[pl.ds(idx * size, size), :]
    return ref.at[:, pl.ds(idx * size, size)]


def _cast_place(w3, layer, axis, me_arr, dep, *, name):
    Ks, Ns = w3.shape[1], w3.shape[2]
    tr = _pick(Ks, 256, 16)
    nblk = Ks // tr
    full = (Ks * N_DEV, Ns) if axis == 0 else (Ks, Ns * N_DEV)
    if axis == 0:
        out_map = lambda i, me: (me[0] * nblk + i, 0)
    else:
        out_map = lambda i, me: (i, me[0])

    def body(me_ref, w_ref, dep_ref, o_ref):
        o_ref[...] = w_ref[...].astype(BF16)

    return pl.pallas_call(
        body, name=name, out_shape=jax.ShapeDtypeStruct(full, BF16),
        grid_spec=pltpu.PrefetchScalarGridSpec(
            num_scalar_prefetch=1, grid=(nblk,),
            in_specs=[pl.BlockSpec((None, tr, Ns), lambda i, me: (layer, i, 0)), pl.BlockSpec(memory_space=pl.ANY)],
            out_specs=pl.BlockSpec((tr, Ns), out_map)),
        compiler_params=_cparams("parallel"),
    )(me_arr, w3, dep)


AG_FIRST = 4
AG_CHIPS = 3


def _sibling():
    x, y, c = _my_place()
    return (x, y, 1 - c)


def _chip_peer(j, same_core=True):
    x, y, c = _my_place()
    px = (1 - x) if j in (0, 2) else x
    py = (1 - y) if j in (1, 2) else y
    return (px, py, c if same_core else 1 - c)


def _gather_start(lands, axes, after, *, name):
    nt = len(lands)
    sizes = [l.shape[a] // N_DEV for l, a in zip(lands, axes)]

    def body(*refs):
        ins, send_sems, recv_sems, token = refs[:nt], refs[nt + 1], refs[nt + 2], refs[-1]
        my_idx = _index_of(_my_place())
        for t in range(nt):
            mine = _slot(ins[t], axes[t], my_idx, sizes[t])
            for k, to in enumerate([_sibling()] + [_chip_peer(j) for j in range(AG_CHIPS)]):
                pltpu.make_async_remote_copy(src_ref=mine, dst_ref=mine, send_sem=send_sems.at[t * AG_FIRST + k],
                                             recv_sem=recv_sems.at[t * AG_FIRST + k], device_id=to,
                                             device_id_type=MESH).start()
        token[...] = jnp.zeros_like(token)

    res = pl.pallas_call(
        body, name=name,
        out_shape=(pltpu.SemaphoreType.DMA((nt * AG_FIRST,)), pltpu.SemaphoreType.DMA((nt * AG_FIRST,)),
                   *[pltpu.HBM(l.shape, l.dtype) for l in lands], jax.ShapeDtypeStruct((8, LANES), F32)),
        in_specs=[_HBM_SPEC] * nt + [_ANY_SPEC],
        out_specs=(_SEM_SPEC, _SEM_SPEC, *[_HBM_SPEC] * nt, pl.BlockSpec(memory_space=pltpu.VMEM)),
        input_output_aliases={t: 2 + t for t in range(nt)},
        compiler_params=pltpu.CompilerParams(has_side_effects=_DATAFLOW),
    )(*[pltpu.with_memory_space_constraint(l, pltpu.HBM) for l in lands], after)
    return res[0], res[1], list(res[2:2 + nt]), res[-1]


def _gather_forward(send_a, recv_a, lands, axes, after, *, name):
    nt = len(lands)
    sizes = [l.shape[a] // N_DEV for l, a in zip(lands, axes)]

    def body(*refs):
        ins, send_a, recv_a = refs[:nt], refs[nt], refs[nt + 1]
        send_f, recv_f, token = refs[nt + 3], refs[nt + 4], refs[-1]
        my_idx = _index_of(_my_place())
        for t in range(nt):
            for j in range(AG_CHIPS):
                src_dev = _chip_peer(j)
                arrived = _slot(ins[t], axes[t], _index_of(src_dev), sizes[t])
                pltpu.make_async_remote_copy(
                    src_ref=_slot(ins[t], axes[t], my_idx, sizes[t]), dst_ref=arrived,
                    send_sem=send_a.at[t * AG_FIRST + 1 + j], recv_sem=recv_a.at[t * AG_FIRST + 1 + j],
                    device_id=src_dev, device_id_type=MESH).wait_recv()
                pltpu.make_async_remote_copy(src_ref=arrived, dst_ref=arrived, send_sem=send_f.at[t * AG_CHIPS + j],
                                             recv_sem=recv_f.at[t * AG_CHIPS + j], device_id=_sibling(),
                                             device_id_type=MESH).start()
        token[...] = jnp.zeros_like(token)

    res = pl.pallas_call(
        body, name=name,
        out_shape=(pltpu.SemaphoreType.DMA((nt * AG_CHIPS,)), pltpu.SemaphoreType.DMA((nt * AG_CHIPS,)),
                   *[pltpu.HBM(l.shape, l.dtype) for l in lands], jax.ShapeDtypeStruct((8, LANES), F32)),
        in_specs=[_HBM_SPEC] * nt + [_SEM_SPEC, _SEM_SPEC, _ANY_SPEC],
        out_specs=(_SEM_SPEC, _SEM_SPEC, *[_HBM_SPEC] * nt, pl.BlockSpec(memory_space=pltpu.VMEM)),
        input_output_aliases={t: 2 + t for t in range(nt)},
        compiler_params=pltpu.CompilerParams(has_side_effects=_DATAFLOW),
    )(*lands, send_a, recv_a, after)
    return res[0], res[1], list(res[2:2 + nt]), res[-1]


def _gather_wait(send_a, recv_a, send_f, recv_f, lands, axes, after, *, name):
    nt = len(lands)
    sizes = [l.shape[a] // N_DEV for l, a in zip(lands, axes)]

    def body(*refs):
        ins, send_a, recv_a, send_f, recv_f = refs[:nt], refs[nt], refs[nt + 1], refs[nt + 2], refs[nt + 3]
        my_idx = _index_of(_my_place())
        sib = _sibling()
        for t in range(nt):
            mine = _slot(ins[t], axes[t], my_idx, sizes[t])
            for k, to in enumerate([sib] + [_chip_peer(j) for j in range(AG_CHIPS)]):
                pltpu.make_async_remote_copy(src_ref=mine, dst_ref=mine, send_sem=send_a.at[t * AG_FIRST + k],
                                             recv_sem=recv_a.at[t * AG_FIRST + k], device_id=to,
                                             device_id_type=MESH).wait_send()
            pltpu.make_async_remote_copy(src_ref=mine, dst_ref=_slot(ins[t], axes[t], _index_of(sib), sizes[t]),
                                         send_sem=send_a.at[t * AG_FIRST], recv_sem=recv_a.at[t * AG_FIRST],
                                         device_id=sib, device_id_type=MESH).wait_recv()
            for j in range(AG_CHIPS):
                sent = _slot(ins[t], axes[t], _index_of(_chip_peer(j)), sizes[t])
                got = _slot(ins[t], axes[t], _index_of(_chip_peer(j, same_core=False)), sizes[t])
                cp = pltpu.make_async_remote_copy(src_ref=sent, dst_ref=got, send_sem=send_f.at[t * AG_CHIPS + j],
                                                  recv_sem=recv_f.at[t * AG_CHIPS + j], device_id=sib,
                                                  device_id_type=MESH)
                cp.wait_send()
                cp.wait_recv()

    res = pl.pallas_call(
        body, name=name, out_shape=[pltpu.HBM(l.shape, l.dtype) for l in lands],
        in_specs=[_HBM_SPEC] * nt + [_SEM_SPEC] * 4 + [_ANY_SPEC], out_specs=[_HBM_SPEC] * nt,
        input_output_aliases={t: t for t in range(nt)},
        compiler_params=pltpu.CompilerParams(has_side_effects=_DATAFLOW),
    )(*lands, send_a, recv_a, send_f, recv_f, after)
    return list(res)


def _scatter_start(dw, axis, *, name):
    size = dw.shape[axis] // N_DEV
    land_shape = (N_PEERS, size, dw.shape[1]) if axis == 0 else (N_PEERS, dw.shape[0], size)

    def body(dw_ref, land_ref, send_sems, recv_sems, dw_thru, land_thru, token):
        for r in range(1, N_DEV):
            p = _peer(r)
            pltpu.make_async_remote_copy(src_ref=_slot(dw_ref, axis, _index_of(p), size), dst_ref=land_ref.at[r - 1],
                                         send_sem=send_sems.at[r - 1], recv_sem=recv_sems.at[r - 1], device_id=p,
                                         device_id_type=MESH).start()
        token[...] = jnp.zeros_like(token)

    land = pltpu.with_memory_space_constraint(lax.empty(land_shape, dw.dtype), pltpu.HBM)
    return pl.pallas_call(
        body, name=name,
        out_shape=(pltpu.SemaphoreType.DMA((N_PEERS,)), pltpu.SemaphoreType.DMA((N_PEERS,)),
                   pltpu.HBM(dw.shape, dw.dtype), pltpu.HBM(land_shape, dw.dtype), jax.ShapeDtypeStruct((8, LANES), F32)),
        in_specs=[_HBM_SPEC, _HBM_SPEC],
        out_specs=(_SEM_SPEC, _SEM_SPEC, _HBM_SPEC, _HBM_SPEC, pl.BlockSpec(memory_space=pltpu.VMEM)),
        input_output_aliases={0: 2, 1: 3},
        compiler_params=pltpu.CompilerParams(has_side_effects=_DATAFLOW),
    )(pltpu.with_memory_space_constraint(dw, pltpu.HBM), land)


def _scatter_wait(send_sems, recv_sems, dw, land, axis, after, *, name):
    size = dw.shape[axis] // N_DEV

    def body(dw_ref, land_ref, send_sems, recv_sems, after_ref, dw_thru, land_thru):
        for r in range(1, N_DEV):
            p = _peer(r)
            cp = pltpu.make_async_remote_copy(src_ref=_slot(dw_ref, axis, _index_of(p), size), dst_ref=land_ref.at[r - 1],
                                              send_sem=send_sems.at[r - 1], recv_sem=recv_sems.at[r - 1], device_id=p,
                                              device_id_type=MESH)
            cp.wait_send()
            cp.wait_recv()

    return pl.pallas_call(
        body, name=name, out_shape=(pltpu.HBM(dw.shape, dw.dtype), pltpu.HBM(land.shape, land.dtype)),
        in_specs=[_HBM_SPEC, _HBM_SPEC, _SEM_SPEC, _SEM_SPEC, _ANY_SPEC], out_specs=(_HBM_SPEC, _HBM_SPEC),
        input_output_aliases={0: 0, 1: 1},
        compiler_params=pltpu.CompilerParams(has_side_effects=_DATAFLOW),
    )(dw, land, send_sems, recv_sems, after)


def _adamw_math(w, g, m, v):
    m = ADAM_B1 * m + (1.0 - ADAM_B1) * g
    v = ADAM_B2 * v + (1.0 - ADAM_B2) * (g * g)
    m_hat = m / (1.0 - ADAM_B1 ** ADAM_STEP)
    v_hat = v / (1.0 - ADAM_B2 ** ADAM_STEP)
    delta = -ADAM_LR * (m_hat / (jnp.sqrt(v_hat) + ADAM_EPS) + ADAM_WD * w)
    return delta, m, v


def _adamw_sharded(w, m, v, layer, dw, land, axis, me_arr, prev, *, name):
    nl, Ks, Ns = w.shape
    tr = _pick(Ks, 128, 16)
    nblk = Ks // tr
    if axis == 0:
        own_map = lambda i, me: (me[0] * nblk + i, 0)
    else:
        own_map = lambda i, me: (i, me[0])
    wspec = pl.BlockSpec((None, tr, Ns), lambda i, me: (layer, i, 0))
    n_prev = 0 if prev is None else 4

    def body(me_ref, w_ref, m_ref, v_ref, own_ref, r_ref, *rest):
        g_ref, d_ref, nm_ref, nv_ref = rest[n_prev:]
        g = own_ref[...].astype(F32)
        for r in range(N_PEERS):
            g = g + r_ref[r].astype(F32)
        delta, nm, nv = _adamw_math(w_ref[...], g, m_ref[...], v_ref[...])
        g_ref[...], d_ref[...], nm_ref[...], nv_ref[...] = g, delta, nm, nv

    return pl.pallas_call(
        body, name=name, out_shape=[jax.ShapeDtypeStruct((nl, Ks, Ns), F32)] * 4,
        grid_spec=pltpu.PrefetchScalarGridSpec(
            num_scalar_prefetch=1, grid=(nblk,),
            in_specs=[wspec, wspec, wspec, pl.BlockSpec((tr, Ns), own_map),
                      pl.BlockSpec((N_PEERS, tr, Ns), lambda i, me: (0, i, 0))] + [_ANY_SPEC] * n_prev,
            out_specs=[wspec] * 4),
        input_output_aliases={6 + k: k for k in range(n_prev)},
        compiler_params=_cparams("parallel"),
    )(me_arr, w, m, v, dw, land, *(prev or []))


def _adamw_flat(w, g, m, v, *, name):
    def body(w_ref, g_ref, m_ref, v_ref, d_ref, nm_ref, nv_ref):
        d_ref[...], nm_ref[...], nv_ref[...] = _adamw_math(w_ref[...], g_ref[...], m_ref[...], v_ref[...])

    spec = pl.BlockSpec(memory_space=pltpu.VMEM)
    return pl.pallas_call(body, name=name, in_specs=[spec] * 4, out_specs=[spec] * 3,
                          out_shape=[jax.ShapeDtypeStruct(w.shape, F32)] * 3)(w, g, m, v)


def _sum_devices(a, *, name):
    def body(a_ref, o_ref):
        s = a_ref[0]
        for d in range(1, N_DEV):
            s = s + a_ref[d]
        o_ref[...] = s

    spec = pl.BlockSpec(memory_space=pltpu.VMEM)
    return pl.pallas_call(body, name=name, in_specs=[spec], out_specs=spec,
                          out_shape=jax.ShapeDtypeStruct(a.shape[1:], F32))(a)


def _ada_mods(c16, ada_w, ada_b_cols, *, name):
    nl, D, cols = ada_w.shape
    bn = _pick(cols, 512)

    def body(c_ref, w_ref, b_ref, o_ref):
        cond = _silu(c_ref[...]).astype(BF16)
        o_ref[...] = _dot(cond, w_ref[...].astype(BF16)) + b_ref[...]

    return pl.pallas_call(
        body, name=name, grid=(nl, cols // bn),
        in_specs=[pl.BlockSpec((16, D), lambda l, j: (0, 0)), pl.BlockSpec((None, D, bn), lambda l, j: (l, 0, j)),
                  pl.BlockSpec((None, 1, bn), lambda l, j: (l, 0, j))],
        out_specs=pl.BlockSpec((None, 16, bn), lambda l, j: (l, 0, j)),
        out_shape=jax.ShapeDtypeStruct((nl, 16, cols), F32),
        compiler_params=_cparams("parallel", "parallel"),
    )(c16, ada_w, ada_b_cols)


def _ada_bwd(cond_t, dmod, w, m, v, *, name):
    nl, D, cols = w.shape
    tr = _pick(D, 256, 8)

    def body(ct_ref, dm_ref, w_ref, m_ref, v_ref, g_ref, d_ref, nm_ref, nv_ref, dc_ref):
        ct, dm, wt = ct_ref[...], dm_ref[...], w_ref[...]
        g = ct[:, 0:1] * dm[0:1, :]
        for r in range(1, N_DEV + 1):
            g = g + ct[:, r:r + 1] * dm[r:r + 1, :]
        delta, nm, nv = _adamw_math(wt, g, m_ref[...], v_ref[...])
        g_ref[...], d_ref[...], nm_ref[...], nv_ref[...] = g, delta, nm, nv
        dc_ref[...] = jnp.sum(wt * dm[N_DEV:N_DEV + 1, :], axis=-1, keepdims=True)

    wspec = pl.BlockSpec((None, tr, cols), lambda l, i: (l, i, 0))
    return pl.pallas_call(
        body, name=name, grid=(nl, D // tr),
        in_specs=[pl.BlockSpec((tr, 16), lambda l, i: (i, 0)), pl.BlockSpec((None, 16, cols), lambda l, i: (l, 0, 0)),
                  wspec, wspec, wspec],
        out_specs=[wspec] * 4 + [pl.BlockSpec((None, tr, 1), lambda l, i: (l, i, 0))],
        out_shape=[jax.ShapeDtypeStruct((nl, D, cols), F32)] * 4 + [jax.ShapeDtypeStruct((nl, D, 1), F32)],
        compiler_params=_cparams("parallel", "parallel"),
    )(cond_t, dmod, w, m, v)


def _rope_tables(L, CTX):
    def angles(pos, dim):
        inv_freq = ROPE_BASE ** (-jnp.arange(0, dim, 2, dtype=F32) / dim)
        return pos.astype(F32)[:, None] * inv_freq[None, :]

    def pad(cos, sin):
        return (jnp.concatenate([cos, jnp.ones((CTX, LANES), F32)], 0),
                jnp.concatenate([sin, jnp.zeros((CTX, LANES), F32)], 0))

    ret = angles(jnp.arange(L), 2 * LANES)
    ret_cs = pad(jnp.cos(ret), jnp.sin(ret))
    rows = angles(jnp.arange(L) // GRID_W, ATT_HEAD_DIM // 2)
    cols = angles(jnp.arange(L) % GRID_W, ATT_HEAD_DIM // 2)
    cos = jnp.concatenate([jnp.cos(rows)] * 2 + [jnp.cos(cols)] * 2, axis=1)
    sin = jnp.concatenate([-jnp.sin(rows), jnp.sin(rows), -jnp.sin(cols), jnp.sin(cols)], axis=1)
    return ret_cs, pad(cos, sin)


def kernel(x, c, ctx, c_ctx, ada_w, ada_b, norm_mix_g, norm_mlp_g, mlp_w1, mlp_w2, ret_w_in, ret_w_out, ret_decay_fwd, ret_decay_bwd, attn_w_in, attn_w_out, attn_sink, final_norm_g, loss_target, m_c_ctx, m_ada_w, m_ada_b, m_norm_mix_g, m_norm_mlp_g, m_mlp_w1, m_mlp_w2, m_ret_w_in, m_ret_w_out, m_ret_decay_fwd, m_ret_decay_bwd, m_attn_w_in, m_attn_w_out, m_attn_sink, m_final_norm_g, v_c_ctx, v_ada_w, v_ada_b, v_norm_mix_g, v_norm_mlp_g, v_mlp_w1, v_mlp_w2, v_ret_w_in, v_ret_w_out, v_ret_decay_fwd, v_ret_decay_bwd, v_attn_w_in, v_attn_w_out, v_attn_sink, v_final_norm_g):
    L, D = x.shape[1], x.shape[2]
    CTX = ctx.shape[1]
    T = L + CTX
    RH = ret_decay_fwd.shape[-1]
    assert D == RH * 2 * LANES and ada_w.shape[0] == 2 and ret_w_in.shape[0] == 1 and attn_w_in.shape[0] == 1
    Hq = attn_sink.shape[-1]
    Hkv = (attn_w_in.shape[-1] * N_DEV // ATT_HEAD_DIM - Hq) // 2
    G = Hq // Hkv
    FF = mlp_w1.shape[-1] * N_DEV
    Wq_r, Wv_r = RH * 2 * LANES, RH * 4 * LANES
    acols = ada_w.shape[-1]
    tr = _pick(CTX, 256, 8)
    tr_wide = _pick(CTX, 128, 8)
    bmT = T // 4 if (T % 64 == 0) else T
    bmL = L // 4 if (L % 64 == 0) else L
    x_idx, y_idx, c_idx = lax.axis_index("x"), lax.axis_index("y"), lax.axis_index("c")
    me = 4 * x_idx + 2 * y_idx + c_idx
    me_arr = jnp.reshape(me, (1,)).astype(jnp.int32)

    (rcos, rsin), (acos, asin) = _rope_tables(L, CTX)
    lg_f = jax.nn.log_sigmoid(ret_decay_fwd.astype(F32))
    lg_b = jax.nn.log_sigmoid(ret_decay_bwd.astype(F32))

    c_pad = jnp.concatenate([c.astype(F32), jnp.zeros((7, D), F32)], 0)
    c_all = _all_gather_small(c_pad, name="ag_c")[:, 0, :]
    c16 = jnp.concatenate([c_all, c_ctx[None, :], jnp.zeros((7, D), F32)], 0)
    ada_b_cols = lax.dynamic_slice_in_dim(ada_b, me * acols, acols, axis=1)[:, None, :]
    mods_shard = _ada_mods(c16, ada_w, ada_b_cols, name="ada_mods")
    mods_all = _all_gather_small(mods_shard.reshape(32, acols), name="ag_mods")

    wdefs = {"ret_in": (ret_w_in, 0, 1), "ret_out": (ret_w_out, 0, 0), "w1_0": (mlp_w1, 0, 1), "w2_0": (mlp_w2, 0, 0),
             "attn_in": (attn_w_in, 0, 1), "attn_out": (attn_w_out, 0, 0), "w1_1": (mlp_w1, 1, 1), "w2_1": (mlp_w2, 1, 0)}
    groups = [["ret_in"], ["ret_out", "w1_0", "w2_0"], ["attn_in", "attn_out"], ["w1_1", "w2_1"]]

    placed = {}

    def ag_start(gi, after):
        g_axes = [wdefs[k][2] for k in groups[gi]]
        ssem, rsem, lands, tok_ = _gather_start([placed[k] for k in groups[gi]], g_axes, after, name=f"ag_start{gi}")
        return dict(a=(ssem, rsem), lands=lands, axes=g_axes, gi=gi), tok_

    def ag_forward(g, after):
        fs, fr, g["lands"], tok_ = _gather_forward(*g["a"], g["lands"], g["axes"], after, name=f"ag_forward{g['gi']}")
        g["f"] = (fs, fr)
        return tok_

    def ag_wait(g, after):
        return _gather_wait(*g["a"], *g["f"], g["lands"], g["axes"], after, name=f"ag_wait{g['gi']}")

    placed["ret_in"] = _cast_place(*wdefs["ret_in"], me_arr, mods_all, name="place_ret_in")
    g0, tok = ag_start(0, mods_all)
    last_cast = tok
    for keys in groups[1:]:
        for k in keys:
            last_cast = placed[k] = _cast_place(*wdefs[k], me_arr, last_cast, name=f"place_{k}")
    mods_all = (mods_all + tok[0, 0]).reshape(N_DEV, 2, 16, acols).transpose(1, 2, 0, 3).reshape(2, 16, 6, D)
    mod_lat = lax.dynamic_index_in_dim(mods_all, me, axis=1, keepdims=False)
    mod_ctx = mods_all[:, N_DEV]

    def pack(i, ks, kc):
        return jnp.stack([mod_lat[i, ks], mod_lat[i, kc], mod_ctx[i, ks], mod_ctx[i, kc]], 0)

    def gates(i, k):
        return jnp.stack([mod_lat[i, k], mod_ctx[i, k]], 0)

    def gate_epilogue(x_rows_lat_only):
        def epi(acc, i, j, xt, gv):
            if x_rows_lat_only:
                gate = gv[0:1, :]
            else:
                row = i * acc.shape[0] + lax.broadcasted_iota(jnp.int32, (acc.shape[0], 1), 0)
                gate = jnp.where(row >= L, gv[1:2, :], gv[0:1, :])
            return xt + gate * acc, acc
        return epi

    w1, w2 = {}, {}

    mmT = dict(M=T, bm=bmT)
    mmL = dict(M=L, bm=bmL)

    def bn_of(n, off=0):
        b = MM_BN
        while n % b or off % b:
            b -= LANES
        return b

    X0 = jnp.concatenate([x[0], ctx[0]], axis=0)
    g_mix0, g_mlp0 = norm_mix_g[0:1], norm_mlp_g[0:1]
    g_mix1, g_mlp1 = norm_mix_g[1:2], norm_mlp_g[1:2]
    a0 = _normmod(X0, g_mix0, pack(0, 0, 1), R=T, L=L, tr=tr, name="normmod_mix0",
                  dep=(last_cast, rcos, rsin, acos, asin))
    tok = ag_forward(g0, a0)
    (wr_in,) = ag_wait(g0, tok)
    g1, tok = ag_start(1, wr_in)

    bn_qk = _pick(Wq_r, MM_BN, 2 * LANES)
    nq_blocks = Wq_r // bn_qk
    kscale = float(2 * LANES) ** -0.5

    def rope_epi(acc, i, j, cos, sin):
        parts = []
        for h in range(acc.shape[1] // (2 * LANES)):
            x1 = acc[:, h * 2 * LANES:h * 2 * LANES + LANES]
            x2 = acc[:, h * 2 * LANES + LANES:(h + 1) * 2 * LANES]
            parts += [x1 * cos - x2 * sin, x2 * cos + x1 * sin]
        return (jnp.concatenate(parts, axis=1) * jnp.where(j < nq_blocks, 1.0, kscale),)

    def row_tile(arr, bm):
        return (arr, (bm, LANES), lambda i, j: (i, 0))

    (qk0,) = _mm(a0, wr_in, "nn", [BF16], N=2 * Wq_r, K=D, bn=bn_qk, bk=D, name="ret_qk", epilogue=rope_epi,
                 extras=[row_tile(rcos, bmT), row_tile(rsin, bmT)], dep=tok, **mmT)
    bn_vg = bn_of(2 * Wv_r, 2 * Wq_r)
    (vg0,) = _mm(a0, wr_in, "nn", [BF16], N=2 * Wv_r, K=D, bn=bn_vg, bk=D, name="ret_vg", b_col0=2 * Wq_r, dep=tok,
                 **mmT)

    of, st_f = _ret_fwd(qk0, vg0, lg_f, None, T=T, L=L, H=RH, rev=False, name="ret_scan_f")
    o0, st_b = _ret_fwd(qk0, vg0, lg_b, of, T=T, L=L, H=RH, rev=True, name="ret_scan_b")
    tok = ag_forward(g1, o0)
    z0 = _readout(o0, vg0, T=T, L=L, H=RH, tr=tr_wide, name="ret_readout", dep=tok)
    wr_out, w1[0], w2[0] = ag_wait(g1, z0)
    g2, tok = ag_start(2, wr_out)
    g3, tok = ag_start(3, tok)

    bnD = _pick(D, MM_BN)

    def xtile(arr, bm):
        return (arr, (bm, bnD), lambda i, j: (i, j))

    def gtile(gv):
        return (gv, (2, bnD), lambda i, j: (0, j))

    bk_max = 2048

    def quarter(rows, div=4):
        return dict(M=rows["M"], bm=rows["bm"] // div if rows["bm"] % (16 * div) == 0 else rows["bm"])
    X1, ro0 = _mm(z0, wr_out, "nn", [F32, BF16], N=D, K=Wv_r, bn=bnD, bk=Wv_r, name="ret_out", cols_outer=True,
                  epilogue=gate_epilogue(False), extras=[xtile(X0, quarter(mmT, 2)["bm"]), gtile(gates(0, 2))], dep=tok,
                  **quarter(mmT, 2))

    def mlp_fwd(Xin, i, g_mlp, rows, name):
        a = _normmod(Xin, g_mlp, pack(i, 3, 4), R=rows["M"], L=L, tr=tr, name=f"normmod_mlp{name}")

        def relu2(acc, i_, j_):
            u = jnp.maximum(acc, 0.0)
            return u, u * u

        bnF = _pick(FF, MM_BN)
        u, r = _mm(a, w1[i], "nn", [BF16, BF16], N=FF, K=D, bn=bnF, bk=D, name=f"mlp_up{name}", epilogue=relu2, **rows)
        Xout, mo = _mm(r, w2[i], "nn", [F32, BF16], N=D, K=FF, bn=bnD, bk=FF, name=f"mlp_down{name}",
                       epilogue=gate_epilogue(rows["M"] == L), cols_outer=True,
                       extras=[xtile(Xin, quarter(rows)["bm"]), gtile(gates(i, 5))], **quarter(rows))
        return a, u, r, Xout, mo

    a1, u0, r0, X2, mo0 = mlp_fwd(X1, 0, g_mlp0, mmT, "0")

    tok = ag_forward(g2, X2)
    a2 = _normmod(X2, g_mix1, pack(1, 0, 1), R=T, L=L, tr=tr, name="normmod_mix1", dep=tok)
    wa_in, wa_out = ag_wait(g2, a2)
    Wq_a, Wk_a = Hq * LANES, Hkv * LANES

    def arope_epi(acc, i, j, cos, sin):
        heads = acc.shape[1] // LANES
        return (acc * jnp.tile(cos, (1, heads)) + _swap32(acc) * jnp.tile(sin, (1, heads)),)

    bn_q = _pick(Wq_a, MM_BN)
    (q1,) = _mm(a2, wa_in, "nn", [BF16], N=Wq_a, K=D, bn=bn_q, bk=D, name="attn_q", epilogue=arope_epi,
                extras=[row_tile(acos, bmL), row_tile(asin, bmL)], **mmL)
    bn_k = bn_of(Wk_a, Wq_a)
    (k1,) = _mm(a2, wa_in, "nn", [BF16], N=Wk_a, K=D, bn=bn_k, bk=D, name="attn_k", b_col0=Wq_a, epilogue=arope_epi,
                extras=[row_tile(acos, bmT), row_tile(asin, bmT)], **mmT)
    bn_v = bn_of(Wk_a, Wq_a + Wk_a)
    (v1,) = _mm(a2, wa_in, "nn", [BF16], N=Wk_a, K=D, bn=bn_v, bk=D, name="attn_v", b_col0=Wq_a + Wk_a, **mmT)
    tok = ag_forward(g3, q1)
    o1, lse = _attn_fwd(q1, k1, v1, attn_sink + tok[0, 0], L=L, CTX=CTX, Hkv=Hkv, G=G, name="attn_fwd")
    w1[1], w2[1] = ag_wait(g3, o1)
    X3, ao = _mm(o1, wa_out, "nn", [F32, BF16], N=D, K=Wq_a, bn=bnD, bk=_pick(Wq_a, 2048), name="attn_out",
                 epilogue=gate_epilogue(True), extras=[xtile(X2, bmL), gtile(gates(1, 2))], **mmL)
    a3, u1, r1, X4, mo1 = mlp_fwd(X3, 1, g_mlp1, mmL, "1")

    dX4, dmo1, acc_head = _loss_head(X4, loss_target[0], mo1, final_norm_g[None, :], gates(1, 5)[0:1], L=L, tr=tr,
                                     name="loss_head")
    loss_part = jnp.sum(acc_head[0, 0])
    d_gf = acc_head[0, 1]
    zeros_d = jnp.zeros((D,), F32)
    dmod_lat = [[zeros_d] * 6, [zeros_d] * 6]
    dmod_ctx = [[zeros_d] * 6, [zeros_d] * 6]
    dmod_lat[1][5] = acc_head[0, 2]

    def dw_mm(a, b, M, N, K, name):
        return _mm(a, b, "tn", [BF16], M=M, N=N, K=K, bm=_pick(M, 512), bn=_pick(N, MM_BN), bk=K, name=name)[0]

    def mlp_bwd(dmo, a, u, r, i, rows, name):
        Mr = rows["M"]

        def times_2u(acc, i_, j_, ut):
            return (acc * (2.0 * ut.astype(F32)),)

        bnF = _pick(FF, MM_BN)
        dw2 = dw_mm(r, dmo, FF, D, Mr, f"mlp_down_dw{name}")
        tok_ = send_grad(f"w2_{i}", dw2, 0)
        (dh,) = _mm(dmo, w2[i], "nt", [BF16], N=FF, K=D, bn=bnF, bk=D, name=f"mlp_down_dx{name}", epilogue=times_2u,
                    extras=[(u, (rows["bm"], bnF), lambda i_, j_: (i_, j_))], dep=tok_, **rows)
        dw1 = dw_mm(a, dh, D, FF, Mr, f"mlp_up_dw{name}")
        tok_ = send_grad(f"w1_{i}", dw1, 1)
        (da,) = _mm(dh, w1[i], "nt", [BF16], N=D, K=FF, bn=bnD, bk=FF, name=f"mlp_up_dx{name}", dep=tok_,
                    cols_outer=True, **quarter(rows))
        return da

    pending = []

    def send_grad(key, dw, axis):
        ssem, rsem, dw_thru, land, tok_ = _scatter_start(dw, axis, name=f"rs_start_{key}")
        pending.append((key, axis, ssem, rsem, dw_thru, land))
        return tok_

    da3 = mlp_bwd(dmo1, a3, u1, r1, 1, mmL, "1")
    dX3, dao, acc = _normmod_bwd(X3, da3, dX4, False, g_mlp1, pack(1, 3, 4), (ao, gates(1, 2)), R=L, L=L, tr=tr,
                                 name="normmod_mlp1_bwd")
    dmod_lat[1][3], dmod_lat[1][4], d_gmlp1, dmod_lat[1][2] = acc[0, 0], acc[0, 1], acc[0, 2], acc[0, 3]

    dwa_out = dw_mm(o1, dao, Wq_a, D, L, "attn_out_dw")
    tok = send_grad("attn_out", dwa_out, 0)
    (do1,) = _mm(dao, wa_out, "nt", [BF16], N=Wq_a, K=D, bn=_pick(Wq_a, MM_BN), bk=D, name="attn_out_dx", dep=tok, **mmL)
    dq1, dk1, dv1, dkx, dvx, dsink_acc = _attn_bwd(q1, k1, v1, do1, lse, attn_sink, L=L, CTX=CTX, Hkv=Hkv, G=G,
                                                   name="attn_bwd")
    dp1 = _attn_assemble(dq1, dk1, dv1, dkx, dvx, acos, asin, T=T, L=L, CTX=CTX, Hq=Hq, Hkv=Hkv, tr=tr_wide,
                         name="attn_assemble")
    Wa_in = Wq_a + 2 * Wk_a
    dwa_in = dw_mm(a2, dp1, D, Wa_in, T, "attn_in_dw")
    tok = send_grad("attn_in", dwa_in, 1)
    (da2,) = _mm(dp1, wa_in, "nt", [BF16], N=D, K=Wa_in, bn=bnD, bk=_pick(Wa_in, 2 * bk_max), name="attn_in_dx",
                 dep=tok, **mmT)
    dX2, dmo0, acc = _normmod_bwd(X2, da2, dX3, True, g_mix1, pack(1, 0, 1), (mo0, gates(0, 5)), R=T, L=L, tr=tr,
                                  name="normmod_mix1_bwd")
    dmod_lat[1][0], dmod_lat[1][1], d_gmix1, dmod_lat[0][5] = acc[0, 0], acc[0, 1], acc[0, 2] + acc[1, 2], acc[0, 3]
    dmod_ctx[1][0], dmod_ctx[1][1], dmod_ctx[0][5] = acc[1, 0], acc[1, 1], acc[1, 3]

    da1 = mlp_bwd(dmo0, a1, u0, r0, 0, mmT, "0")
    dX1, dro0, acc = _normmod_bwd(X1, da1, dX2, False, g_mlp0, pack(0, 3, 4), (ro0, gates(0, 2)), R=T, L=L, tr=tr,
                                  name="normmod_mlp0_bwd")
    dmod_lat[0][3], dmod_lat[0][4], d_gmlp0, dmod_lat[0][2] = acc[0, 0], acc[0, 1], acc[0, 2] + acc[1, 2], acc[0, 3]
    dmod_ctx[0][3], dmod_ctx[0][4], dmod_ctx[0][2] = acc[1, 0], acc[1, 1], acc[1, 3]

    dwr_out = dw_mm(z0, dro0, Wv_r, D, T, "ret_out_dw")
    tok = send_grad("ret_out", dwr_out, 0)
    (dz0,) = _mm(dro0, wr_out, "nt", [BF16], N=Wv_r, K=D, bn=_pick(Wv_r, MM_BN), bk=D, name="ret_out_dx", dep=tok, **mmT)
    do0, dg0 = _readout_bwd(dz0, o0, vg0, T=T, L=L, H=RH, tr=tr_wide, name="ret_readout_bwd")
    dq_f, dk_f, dv_f, dlg_f = _ret_bwd(qk0, vg0, do0, st_f, lg_f, T=T, L=L, H=RH, rev=False, name="ret_scan_f_bwd")
    dq_b, dk_b, dv_b, dlg_b = _ret_bwd(qk0, vg0, do0, st_b, lg_b, T=T, L=L, H=RH, rev=True, name="ret_scan_b_bwd")
    dp0 = _ret_assemble(dq_f, dq_b, dk_f, dk_b, dv_f, dv_b, dg0, rcos, rsin, T=T, L=L, H=RH, tr=tr_wide,
                        name="ret_assemble")
    Wr_in = 2 * Wq_r + 2 * Wv_r
    dwr_in = dw_mm(a0, dp0, D, Wr_in, T, "ret_in_dw")
    tok = send_grad("ret_in", dwr_in, 1)
    (da0,) = _mm(dp0, wr_in, "nt", [BF16], N=D, K=Wr_in, bn=bnD, bk=Wr_in, name="ret_in_dx", dep=tok, cols_outer=True,
                 b_buffers=1, **quarter(mmT))
    dX0, acc = _normmod_bwd(X0, da0, dX1, False, g_mix0, pack(0, 0, 1), None, R=L, L=L, tr=tr, name="normmod_mix0_bwd")
    _, acc_c = _normmod_bwd(X0, da0, dX1, False, g_mix0, pack(0, 0, 1), None, R=CTX, L=0, tr=tr, row0=L,
                            name="normmod_mix0_bwd_ctx")
    dmod_lat[0][0], dmod_lat[0][1], d_gmix0 = acc[0, 0], acc[0, 1], acc[0, 2] + acc_c[1, 2]
    dmod_ctx[0][0], dmod_ctx[0][1] = acc_c[1, 0], acc_c[1, 1]
    grad_x = dX0[None]

    wmv = {"ret_in": (ret_w_in, m_ret_w_in, v_ret_w_in, 0, "ret_w_in"),
           "ret_out": (ret_w_out, m_ret_w_out, v_ret_w_out, 0, "ret_w_out"),
           "attn_in": (attn_w_in, m_attn_w_in, v_attn_w_in, 0, "attn_w_in"),
           "attn_out": (attn_w_out, m_attn_w_out, v_attn_w_out, 0, "attn_w_out"),
           "w1_0": (mlp_w1, m_mlp_w1, v_mlp_w1, 0, "mlp_w1"), "w1_1": (mlp_w1, m_mlp_w1, v_mlp_w1, 1, "mlp_w1"),
           "w2_0": (mlp_w2, m_mlp_w2, v_mlp_w2, 0, "mlp_w2"), "w2_1": (mlp_w2, m_mlp_w2, v_mlp_w2, 1, "mlp_w2")}
    big = {}

    def finish_grad(entry, after):
        key, axis, ssem, rsem, dw_thru, land = entry
        dw_done, land_done = _scatter_wait(ssem, rsem, dw_thru, land, axis, after, name=f"rs_wait_{key}")
        w_, m_, v_, layer, out_name = wmv[key]
        big[out_name] = _adamw_sharded(w_, m_, v_, layer, dw_done, land_done, axis, me_arr, big.get(out_name),
                                       name=f"adamw_{key}")
        return big[out_name][0]

    after = dX0
    for entry in pending[:-1]:
        after = finish_grad(entry, after)

    misc = jnp.zeros((D,), F32)
    misc = misc.at[0:RH].set(dlg_f[:, 0, 0]).at[RH:2 * RH].set(dlg_b[:, 0, 0])
    misc = misc.at[2 * RH:2 * RH + Hq].set(dsink_acc[:, :G, 0].reshape(Hq)).at[2 * RH + Hq].set(loss_part)
    rows = ([dmod_lat[i][k] for i in range(2) for k in range(6)] + [dmod_ctx[i][k] for i in range(2) for k in range(6)]
            + [d_gmix0, d_gmix1, d_gmlp0, d_gmlp1, d_gf, misc, zeros_d, zeros_d])
    part = jnp.stack(rows, 0)
    part_all = _all_gather_small(part, name="ag_small_grads")
    tot = _sum_devices(part_all, name="sum_small_grads")

    grad_ada_b = (tot[0:12] + tot[12:24]).reshape(2, 6 * D)
    grad_norm_mix_g, grad_norm_mlp_g, grad_final_norm_g = tot[24:26], tot[26:28], tot[28]
    grad_ret_decay_fwd = (tot[29, 0:RH] * jax.nn.sigmoid(-ret_decay_fwd[0]))[None]
    grad_ret_decay_bwd = (tot[29, RH:2 * RH] * jax.nn.sigmoid(-ret_decay_bwd[0]))[None]
    grad_attn_sink = tot[29, 2 * RH:2 * RH + Hq][None]
    loss = tot[29, 2 * RH + Hq]

    dlat_cols = lax.dynamic_slice_in_dim(part_all[:, 0:12].reshape(N_DEV, 2, 6 * D), me * acols, acols, axis=2)
    dctx_cols = lax.dynamic_slice_in_dim(tot[12:24].reshape(2, 6 * D), me * acols, acols, axis=1)
    dmod16 = jnp.concatenate([dlat_cols.transpose(1, 0, 2), dctx_cols[:, None, :], jnp.zeros((2, 7, acols), F32)], 1)
    cond_t = _silu(c16).T
    g_ada, d_ada, nm_ada, nv_ada, dcond_part = _ada_bwd(cond_t, dmod16, ada_w, m_ada_w, v_ada_w, name="ada_bwd")
    dcond = (dcond_part[0, :, 0] + dcond_part[1, :, 0]).reshape(D // LANES, LANES)
    pad_rows = -(D // LANES) % 8
    dcond_pad = jnp.concatenate([dcond, jnp.zeros((pad_rows, LANES), F32)], 0) if pad_rows else dcond
    dcond_all = _all_gather_small(dcond_pad, name="ag_dcond")
    dcond_tot = _sum_devices(dcond_all, name="sum_dcond")[:D // LANES].reshape(D)
    sg = jax.nn.sigmoid(c_ctx)
    grad_c_ctx = dcond_tot * (sg * (1.0 + c_ctx * (1.0 - sg)))

    small_w = [c_ctx, ada_b, norm_mix_g, norm_mlp_g, ret_decay_fwd, ret_decay_bwd, attn_sink, final_norm_g]
    small_g = [grad_c_ctx, grad_ada_b, grad_norm_mix_g, grad_norm_mlp_g, grad_ret_decay_fwd, grad_ret_decay_bwd,
               grad_attn_sink, grad_final_norm_g]
    small_m = [m_c_ctx, m_ada_b, m_norm_mix_g, m_norm_mlp_g, m_ret_decay_fwd, m_ret_decay_bwd, m_attn_sink,
               m_final_norm_g]
    small_v = [v_c_ctx, v_ada_b, v_norm_mix_g, v_norm_mlp_g, v_ret_decay_fwd, v_ret_decay_bwd, v_attn_sink,
               v_final_norm_g]
    sizes = [w_.size for w_ in small_w]
    total = sum(-(-s // LANES) * LANES for s in sizes)
    total_pad = -(-total // (8 * LANES)) * 8 * LANES

    def flat_pack(ts, fill):
        pieces = []
        for t_ in ts:
            f = t_.reshape(-1).astype(F32)
            pad = -f.size % LANES
            pieces.append(jnp.concatenate([f, jnp.full((pad,), fill, F32)]) if pad else f)
        pieces.append(jnp.full((total_pad - total,), fill, F32))
        return jnp.concatenate(pieces).reshape(total_pad // LANES, LANES)

    d_s, nm_s, nv_s = _adamw_flat(flat_pack(small_w, 0.0), flat_pack(small_g, 0.0), flat_pack(small_m, 0.0),
                                  flat_pack(small_v, 1.0), name="adamw_small")

    def unpack(p):
        flat = p.reshape(-1)
        res, off = [], 0
        for w_, s in zip(small_w, sizes):
            res.append(flat[off:off + s].reshape(w_.shape))
            off += -(-s // LANES) * LANES
        return res

    finish_grad(pending[-1], d_s)
    d_small, nm_small, nv_small = unpack(d_s), unpack(nm_s), unpack(nv_s)
    small_names = ["c_ctx", "ada_b", "norm_mix_g", "norm_mlp_g", "ret_decay_fwd", "ret_decay_bwd", "attn_sink",
                   "final_norm_g"]
    sm = {n: (g_, d_, m_, v_) for n, g_, d_, m_, v_ in zip(small_names, small_g, d_small, nm_small, nv_small)}

    def out4(n):
        if n == "ada_w":
            return g_ada, d_ada, nm_ada, nv_ada
        if n in big:
            return tuple(big[n])
        return sm[n]

    order = ["c_ctx", "ada_w", "ada_b", "norm_mix_g", "norm_mlp_g", "mlp_w1", "mlp_w2", "ret_w_in", "ret_w_out",
             "ret_decay_fwd", "ret_decay_bwd", "attn_w_in", "attn_w_out", "attn_sink", "final_norm_g"]
    quads = [out4(n) for n in order]
    return (loss, grad_x, *[q_[0] for q_ in quads], *[q_[1] for q_ in quads], *[q_[2] for q_ in quads],
            *[q_[3] for q_ in quads])
```

```python
import functools

import jax
import jax.numpy as jnp
from jax import lax
from jax.experimental import pallas as pl
from jax.experimental.pallas import tpu as pltpu

F32 = jnp.float32
BF16 = jnp.bfloat16

N_DEV = 8
NORM_EPS = 1e-6
CHUNK = 128
ATT_HEAD_DIM = 128
GRID_W = 64
ROPE_BASE = 10000.0
NEG_INF = -1e30
ADAM_LR, ADAM_B1, ADAM_B2, ADAM_EPS, ADAM_WD, ADAM_STEP = 0.001, 0.9, 0.999, 1e-08, 0.01, 10

V7X_VMEM_LIMIT_BYTES = 56 * 1024 * 1024
MM_BN = 1024
LANES = 128
MESH = pl.DeviceIdType.MESH

_NN = (((1,), (0,)), ((), ()))
_NT = (((1,), (1,)), ((), ()))
_TN = (((0,), (0,)), ((), ()))


def _dot(a, b, dn=_NN):
    return lax.dot_general(a, b, dn, preferred_element_type=F32)


def _cparams(*sem):
    return pltpu.CompilerParams(dimension_semantics=sem, vmem_limit_bytes=V7X_VMEM_LIMIT_BYTES)


def _pick(n, pref, mult=LANES):
    if n <= pref:
        return n
    best = None
    for d in range(mult, pref + 1, mult):
        if n % d == 0:
            best = d
    assert best is not None, (n, pref)
    return best


def _silu(x):
    return x * jax.nn.sigmoid(x)


def _mm(a, b, mode, out_dtypes, *, M, N, K, bm, bn, bk, name, b_col0=0, epilogue=None, extras=(), dep=None,
        cols_outer=False, b_buffers=None):
    assert M % bm == 0 and N % bn == 0 and K % bk == 0 and b_col0 % bn == 0, (name, M, N, K, bm, bn, bk, b_col0)
    nk = K // bk
    c0 = b_col0 // bn
    ax_i, ax_j = (1, 0) if cols_outer else (0, 1)

    def spec(block, f, **kw):
        if cols_outer:
            return pl.BlockSpec(block, lambda j, i, k: f(i, j, k), **kw)
        return pl.BlockSpec(block, f, **kw)

    b_kw = {} if b_buffers is None else dict(pipeline_mode=pl.Buffered(b_buffers))
    if mode == "nn":
        a_spec = spec((bm, bk), lambda i, j, k: (i, k))
        b_spec = spec((bk, bn), lambda i, j, k: (k, j + c0), **b_kw)
    elif mode == "nt":
        a_spec = spec((bm, bk), lambda i, j, k: (i, k))
        b_spec = spec((bn, bk), lambda i, j, k: (j + c0, k), **b_kw)
    else:
        a_spec = spec((bk, bm), lambda i, j, k: (k, i))
        b_spec = spec((bk, bn), lambda i, j, k: (k, j + c0), **b_kw)
    dn = {"nn": _NN, "nt": _NT, "tn": _TN}[mode]
    e_specs = [spec(bs, (lambda i, j, k, f=f: f(i, j))) for (_, bs, f) in extras]
    ne, no = len(extras), len(out_dtypes)
    nd = 0 if dep is None else 1

    def body(a_ref, b_ref, *rest):
        e_refs, o_refs = rest[:ne], rest[ne + nd:ne + nd + no]
        i, j, k = pl.program_id(ax_i), pl.program_id(ax_j), pl.program_id(2)

        def finish(acc):
            outs = (acc,) if epilogue is None else epilogue(acc, i, j, *[e[...] for e in e_refs])
            for o_ref, o in zip(o_refs, outs):
                o_ref[...] = o.astype(o_ref.dtype)

        p = _dot(a_ref[...], b_ref[...], dn)
        if nk == 1:
            finish(p)
        else:
            acc_ref = rest[-1]

            @pl.when(k == 0)
            def _():
                acc_ref[...] = p

            @pl.when(k > 0)
            def _():
                acc_ref[...] += p

            @pl.when(k == nk - 1)
            def _():
                finish(acc_ref[...])

    outs = pl.pallas_call(
        body, name=name, grid=(N // bn, M // bm, nk) if cols_outer else (M // bm, N // bn, nk),
        in_specs=[a_spec, b_spec] + e_specs + [pl.BlockSpec(memory_space=pl.ANY)] * nd,
        out_specs=[spec((bm, bn), lambda i, j, k: (i, j)) for _ in out_dtypes],
        out_shape=[jax.ShapeDtypeStruct((M, N), dt) for dt in out_dtypes],
        scratch_shapes=[pltpu.VMEM((bm, bn), F32)] if nk > 1 else [],
        compiler_params=_cparams("parallel", "parallel", "arbitrary"),
    )(a, b, *[e[0] for e in extras], *([dep] if nd else []))
    return outs


def _rowwise(body, rows, vecs, outs, n_acc, *, R, L, tr, name, acc_width=None, dep=None, row0=0):
    assert row0 % tr == 0
    b0 = row0 // tr
    assert R % tr == 0 and L % tr == 0, (name, R, L, tr)
    nl = L // tr
    n_regions = 2 if R > L else 1
    n_rows, n_vecs, n_outs = len(rows), len(vecs), len(outs)
    deps = () if dep is None else (tuple(dep) if isinstance(dep, (tuple, list)) else (dep,))
    n_dep = len(deps)
    acc_pad = -(-n_acc // 8) * 8 if n_acc else 0

    in_specs = []
    for (_, w, cb, lat_only) in rows:
        if lat_only:
            in_specs.append(pl.BlockSpec((tr, w), lambda i, cb=cb: (jnp.minimum(i, nl - 1), cb)))
        else:
            in_specs.append(pl.BlockSpec((tr, w), lambda i, cb=cb: (i + b0, cb)))
    for v in vecs:
        in_specs.append(pl.BlockSpec(v.shape, lambda i, nd=v.ndim: (0,) * nd))
    in_specs += [pl.BlockSpec(memory_space=pl.ANY)] * n_dep
    out_specs = [pl.BlockSpec((tr, w), lambda i: (i, 0)) for (w, _) in outs]
    out_shape = [jax.ShapeDtypeStruct((R, w), dt) for (w, dt) in outs]
    if n_acc:
        out_specs.append(pl.BlockSpec((None, acc_pad, acc_width), lambda i: (jnp.where(i >= nl, 1, 0), 0, 0)))
        out_shape.append(jax.ShapeDtypeStruct((n_regions, acc_pad, acc_width), F32))

    def kern(*refs):
        i = pl.program_id(0)
        is_ctx = i >= nl
        ins = [r[...] for r in refs[:n_rows + n_vecs]]
        o_refs = refs[n_rows + n_vecs + n_dep:]
        out_tiles, acc_rows = body(is_ctx, *ins)
        for o_ref, o in zip(o_refs[:n_outs], out_tiles):
            o_ref[...] = o.astype(o_ref.dtype)
        if n_acc:
            acc_ref = o_refs[n_outs]

            @pl.when((i == 0) | (i == nl))
            def _():
                acc_ref[...] = jnp.zeros_like(acc_ref)

            for r, row in enumerate(acc_rows):
                acc_ref[r:r + 1, :] += row

    res = pl.pallas_call(
        kern, name=name, grid=(R // tr,), in_specs=in_specs, out_specs=out_specs, out_shape=out_shape,
        compiler_params=_cparams("arbitrary"),
    )(*[r[0] for r in rows], *vecs, *deps)
    return res


def _colsum(x):
    return jnp.sum(x, axis=0, keepdims=True)


def _rms_stats(x):
    r = lax.rsqrt(jnp.mean(x * x, axis=-1, keepdims=True) + NORM_EPS)
    return x * r, r


def _sel(is_ctx, pk, lat_row, ctx_row):
    return jnp.where(is_ctx, pk[ctx_row:ctx_row + 1, :], pk[lat_row:lat_row + 1, :])


def _normmod(x, g, pk, *, R, L, tr, name, dep=None):
    D = x.shape[-1]

    def body(is_ctx, xt, gv, pkv):
        xh, _ = _rms_stats(xt)
        sh, sc = _sel(is_ctx, pkv, 0, 2), _sel(is_ctx, pkv, 1, 3)
        return ((xh * gv) * (1.0 + sc) + sh,), ()

    return _rowwise(body, [(x, D, 0, False)], [g, pk], [(D, BF16)], 0, R=R, L=L, tr=tr, name=name, dep=dep)[0]


def _normmod_bwd(x_in, da, dx_out, dx_out_lat_only, g, pk, prev, *, R, L, tr, name, row0=0):
    D = x_in.shape[-1]
    has_prev = prev is not None

    def body(is_ctx, *t):
        if has_prev:
            xt, dat, dxo, mp, gv, pkv, gates = t
        else:
            xt, dat, dxo, gv, pkv = t
        xh, r = _rms_stats(xt)
        dat = dat.astype(F32)
        sc = _sel(is_ctx, pkv, 1, 3)
        if dx_out_lat_only:
            dxo = jnp.where(is_ctx, 0.0, dxo)
        dn = dat * (1.0 + sc)
        w = dn * gv
        dxi = dxo + r * (w - xh * jnp.mean(w * xh, axis=-1, keepdims=True))
        accs = [_colsum(dat), _colsum(dat * (xh * gv)), _colsum(dn * xh)]
        outs = [dxi]
        if has_prev:
            gate = _sel(is_ctx, gates, 0, 1)
            outs.append(dxi * gate)
            accs.append(_colsum(dxi * mp.astype(F32)))
        return outs, accs

    rows = [(x_in, D, 0, False), (da, D, 0, False), (dx_out, D, 0, dx_out_lat_only)]
    vecs = [g, pk]
    outs = [(D, F32)]
    if has_prev:
        rows.append((prev[0], D, 0, False))
        vecs.append(prev[1])
        outs.append((D, BF16))
    return _rowwise(body, rows, vecs, outs, 4 if has_prev else 3, R=R, L=L, tr=tr, name=name, acc_width=D, row0=row0)


def _loss_head(x4, target, m_prev, gf, gate, *, L, tr, name):
    D = x4.shape[-1]

    def body(is_ctx, xt, tg, mp, gfv, gatev):
        xh, r = _rms_stats(xt)
        e = xh * gfv - tg
        dy = e * (1.0 / D)
        w = dy * gfv
        dx = r * (w - xh * jnp.mean(w * xh, axis=-1, keepdims=True))
        accs = [_colsum(e * e) * (0.5 / D), _colsum(dy * xh), _colsum(dx * mp.astype(F32))]
        return (dx, dx * gatev), accs

    return _rowwise(body, [(x4, D, 0, False), (target, D, 0, False), (m_prev, D, 0, False)], [gf, gate],
                    [(D, F32), (D, BF16)], 3, R=L, L=L, tr=tr, name=name, acc_width=D)


RET_CHUNK = 2 * LANES
RET_HEADS_PER_STEP = 4


def _decays(lgh, rev):
    C = RET_CHUNK
    ii = lax.broadcasted_iota(jnp.int32, (C, C), 0)
    jj = lax.broadcasted_iota(jnp.int32, (C, C), 1)
    ri = lax.broadcasted_iota(jnp.int32, (C, 1), 0).astype(F32)
    diff = (jj - ii if rev else ii - jj)
    amat = jnp.where(diff >= 0, jnp.exp(lgh * jnp.maximum(diff, 0).astype(F32)), 0.0)
    pos = (C - ri) if rev else (ri + 1.0)
    bq = jnp.exp(lgh * pos)
    bk = jnp.exp(lgh * (C - pos))
    return amat, bq, bk, pos


def _ret_geometry(T, L, H, rev, backward):
    C = RET_CHUNK
    assert T % C == 0 and L % C == 0, (T, L)
    nT, nL = T // C, L // C
    hb = RET_HEADS_PER_STEP if H % RET_HEADS_PER_STEP == 0 else 1

    def step(s):
        return (nT - 1 - s) if backward else s

    def chunk(s):
        s = step(s)
        return (nT - 1 - s) if rev else (s + nL) % nT

    return C, nT, hb, chunk, step


def _ret_fwd(qk, vg, lg, other, *, T, L, H, rev, name):
    dk, dv = 2 * LANES, 4 * LANES
    C, nT, hb, chunk, step = _ret_geometry(T, L, H, rev, False)
    n_other = 0 if other is None else 1

    def body(lg_ref, q_ref, k_ref, v_ref, *rest):
        o_ref, st_ref, s_scr = rest[n_other:]
        hg, s = pl.program_id(0), pl.program_id(1)

        @pl.when(s == 0)
        def _():
            s_scr[...] = jnp.zeros_like(s_scr)

        for hh in range(hb):
            lgh = lg_ref[0, hg * hb + hh]
            amat, bq, bk, _ = _decays(lgh, rev)
            q, k = q_ref[:, hh * dk:(hh + 1) * dk], k_ref[:, hh * dk:(hh + 1) * dk]
            v = v_ref[:, hh * dv:(hh + 1) * dv]
            stb = s_scr[hh].astype(BF16)
            st_ref[hh] = stb
            scores = _dot(q, k, _NT) * amat
            o = _dot(scores.astype(BF16), v) + _dot(q, stb) * bq
            if n_other:
                o = rest[0][:, hh * dv:(hh + 1) * dv] + o
            o_ref[:, hh * dv:(hh + 1) * dv] = o
            kd = (k.astype(F32) * bk).astype(BF16)
            s_scr[hh] = s_scr[hh] * jnp.exp(lgh * C) + _dot(kd, v, _TN)

    vspec = pl.BlockSpec((C, hb * dv), lambda h, s: (chunk(s), h))
    return pl.pallas_call(
        body, name=name, grid=(H // hb, nT),
        in_specs=[pl.BlockSpec(memory_space=pltpu.SMEM),
                  pl.BlockSpec((C, hb * dk), lambda h, s: (chunk(s), h)),
                  pl.BlockSpec((C, hb * dk), lambda h, s: (chunk(s), H // hb + h)), vspec] + [vspec] * n_other,
        out_specs=[vspec, pl.BlockSpec((hb, None, dk, dv), lambda h, s: (h, s, 0, 0))],
        out_shape=[jax.ShapeDtypeStruct((T, H * dv), F32), jax.ShapeDtypeStruct((H, nT, dk, dv), BF16)],
        scratch_shapes=[pltpu.VMEM((hb, dk, dv), F32)],
        compiler_params=_cparams("parallel", "arbitrary"),
    )(lg, qk, qk, vg, *([other] if n_other else []))


def _ret_bwd(qk, vg, do, states, lgs, *, T, L, H, name):
    dk, dv = 2 * LANES, 4 * LANES
    revs = (False, True)
    geo = [_ret_geometry(T, L, H, rev, True) for rev in revs]
    C, nT, hb = geo[0][:3]
    N_IN, N_OUT = 6, 4

    def body(*refs):
        ins = [refs[d * N_IN:(d + 1) * N_IN] for d in range(2)]
        outs = [refs[2 * N_IN + d * N_OUT:2 * N_IN + (d + 1) * N_OUT] for d in range(2)]
        ds_scr = refs[-1]
        hg, s = pl.program_id(0), pl.program_id(1)

        @pl.when(s == 0)
        def _():
            ds_scr[...] = jnp.zeros_like(ds_scr)
            for d in range(2):
                outs[d][3][...] = jnp.zeros_like(outs[d][3])

        for hh in range(hb):
            for d, rev in enumerate(revs):
                lg_ref, q_ref, k_ref, v_ref, do_ref, st_ref = ins[d]
                dq_ref, dk_ref, dv_ref, dlg_ref = outs[d]
                lgh = lg_ref[0, hg * hb + hh]
                amat, bq, bk, pos = _decays(lgh, rev)
                ksl, vsl = slice(hh * dk, (hh + 1) * dk), slice(hh * dv, (hh + 1) * dv)
                q, k, v, dob = q_ref[:, ksl], k_ref[:, ksl], v_ref[:, vsl], do_ref[:, vsl]
                stb = st_ref[hh]
                ds_new = ds_scr[d, hh]
                dsb = ds_new.astype(BF16)
                qf, kf = q.astype(F32), k.astype(F32)
                scores = (_dot(q, k, _NT) * amat).astype(BF16)
                dqk = (_dot(dob, v, _NT) * amat).astype(BF16)
                dq = _dot(dqk, k) + _dot(dob, stb, _NT) * bq
                dkk = _dot(dqk, q, _TN) + _dot(v, dsb, _NT) * bk
                kd = (kf * bk).astype(BF16)
                dvv = _dot(scores, dob, _TN) + _dot(kd, dsb)
                dod = (dob.astype(F32) * bq).astype(BF16)
                ds_prev = ds_new * jnp.exp(lgh * C) + _dot(q, dod, _TN)
                ds_scr[d, hh] = ds_prev
                dq_ref[:, ksl] = dq.astype(dq_ref.dtype)
                dk_ref[:, ksl] = dkk.astype(dk_ref.dtype)
                dv_ref[:, vsl] = dvv.astype(dv_ref.dtype)
                dlg = (jnp.sum(pos * jnp.sum(qf * dq - kf * dkk, axis=-1, keepdims=True))
                       + C * jnp.sum(ds_prev * stb.astype(F32)))
                dlg_ref[hh] += dlg

    in_specs, out_specs, operands = [], [], []
    for d in range(2):
        chunk, step = geo[d][3], geo[d][4]
        qspec = pl.BlockSpec((C, hb * dk), lambda h, s, chunk=chunk: (chunk(s), h))
        vspec = pl.BlockSpec((C, hb * dv), lambda h, s, chunk=chunk: (chunk(s), h))
        in_specs += [pl.BlockSpec(memory_space=pltpu.SMEM), qspec,
                     pl.BlockSpec((C, hb * dk), lambda h, s, chunk=chunk: (chunk(s), H // hb + h)), vspec, vspec,
                     pl.BlockSpec((hb, None, dk, dv), lambda h, s, step=step: (h, step(s), 0, 0))]
        out_specs += [qspec, qspec, vspec, pl.BlockSpec((hb, 8, LANES), lambda h, s: (h, 0, 0))]
        operands += [lgs[d], qk, qk, vg, do, states[d]]
    one_dir = [jax.ShapeDtypeStruct((T, H * dk), BF16), jax.ShapeDtypeStruct((T, H * dk), BF16),
               jax.ShapeDtypeStruct((T, H * dv), BF16), jax.ShapeDtypeStruct((H, 8, LANES), F32)]
    res = pl.pallas_call(
        body, name=name, grid=(H // hb, nT), in_specs=in_specs, out_specs=out_specs, out_shape=one_dir * 2,
        scratch_shapes=[pltpu.VMEM((2, hb, dk, dv), F32)],
        compiler_params=_cparams("parallel", "arbitrary"),
    )(*operands)
    return res[:N_OUT], res[N_OUT:]


def _readout(o, vg, *, T, L, H, tr, name, dep=None):
    dv = 4 * LANES
    W = H * dv

    def body(is_ctx, o, g):
        parts = []
        for h in range(H):
            oh = o[:, h * dv:(h + 1) * dv]
            parts.append(oh * lax.rsqrt(jnp.mean(oh * oh, axis=-1, keepdims=True) + NORM_EPS))
        y = jnp.concatenate(parts, axis=1)
        return (_silu(g.astype(F32)) * y,), ()

    return _rowwise(body, [(o, W, 0, False), (vg, W, 1, False)], [], [(W, BF16)], 0,
                    R=T, L=L, tr=tr, name=name, dep=dep)[0]


def _readout_bwd(dz, o, vg, *, T, L, H, tr, name):
    dv = 4 * LANES
    W = H * dv

    def body(is_ctx, dzt, o, g):
        gf = g.astype(F32)
        sg = jax.nn.sigmoid(gf)
        dzf = dzt.astype(F32)
        dy = dzf * (gf * sg)
        ys, dos = [], []
        for h in range(H):
            sl = slice(h * dv, (h + 1) * dv)
            oh, dyh = o[:, sl], dy[:, sl]
            r = lax.rsqrt(jnp.mean(oh * oh, axis=-1, keepdims=True) + NORM_EPS)
            yh = oh * r
            ys.append(yh)
            dos.append(r * (dyh - yh * jnp.mean(dyh * yh, axis=-1, keepdims=True)))
        y = jnp.concatenate(ys, axis=1)
        dg = dzf * y * (sg * (1.0 + gf * (1.0 - sg)))
        return (jnp.concatenate(dos, axis=1), dg), ()

    return _rowwise(body, [(dz, W, 0, False), (o, W, 0, False), (vg, W, 1, False)], [],
                    [(W, BF16), (W, BF16)], 0, R=T, L=L, tr=tr, name=name)


def _ret_assemble(dq_f, dq_b, dk_f, dk_b, dv_f, dv_b, dg, cos, sin, *, T, L, H, tr, name):
    dk, dv = 2 * LANES, 4 * LANES
    Wq, Wv = H * dk, H * dv
    kscale = float(dk) ** -0.5

    def unrope(d, c, s_, scale):
        parts = []
        for h in range(H):
            d1, d2 = d[:, h * dk:h * dk + LANES], d[:, h * dk + LANES:(h + 1) * dk]
            parts += [(d1 * c + d2 * s_) * scale, (d2 * c - d1 * s_) * scale]
        return jnp.concatenate(parts, axis=1)

    def body(is_ctx, qf, qb, kf, kb, vf, vb, g, c, s_):
        add = lambda a, b: a.astype(F32) + b.astype(F32)
        dq = unrope(add(qf, qb), c, s_, 1.0)
        dkk = unrope(add(kf, kb), c, s_, kscale)
        return (jnp.concatenate([dq.astype(BF16), dkk.astype(BF16), add(vf, vb).astype(BF16), g], axis=1),), ()

    rows = [(dq_f, Wq, 0, False), (dq_b, Wq, 0, False), (dk_f, Wq, 0, False), (dk_b, Wq, 0, False),
            (dv_f, Wv, 0, False), (dv_b, Wv, 0, False), (dg, Wv, 0, False),
            (cos, LANES, 0, False), (sin, LANES, 0, False)]
    return _rowwise(body, rows, [], [(2 * Wq + 2 * Wv, BF16)], 0, R=T, L=L, tr=tr, name=name)[0]


def _swap32(x):
    n = x.shape[-1]
    lane = lax.broadcasted_iota(jnp.int32, x.shape, x.ndim - 1)
    return jnp.where(lane % 64 < 32, pltpu.roll(x, n - 32, x.ndim - 1), pltpu.roll(x, 32, x.ndim - 1))


ATT_Q_BLOCKS = 4


def _stack_heads_at(ref, rows, G):
    return jnp.concatenate([ref[rows, g * LANES:(g + 1) * LANES] for g in range(G)], axis=0)


def _stack_columns_at(ref, rows, G):
    return jnp.concatenate([ref[rows, g:g + 1] for g in range(G)], axis=0)


def _sink_column(sink_ref, h, G):
    return jnp.concatenate([jnp.full((CHUNK, 1), sink_ref[0, h * G + g], F32) for g in range(G)], axis=0)


def _key_mask(n, nb, CTX, G):
    W = 3 * CHUNK + CTX
    ii = lax.broadcasted_iota(jnp.int32, (G * CHUNK, W), 0) & (CHUNK - 1)
    col = lax.broadcasted_iota(jnp.int32, (G * CHUNK, W), 1)
    is_prev = col < CHUNK
    is_next = (col >= 2 * CHUNK) & (col < 3 * CHUNK)
    prev_ok = is_prev & (col >= ii) & (n > 0)
    next_ok = is_next & ((col - 2 * CHUNK) <= ii) & (n < nb - 1)
    return prev_ok | next_ok | jnp.logical_not(is_prev | is_next)


def _attn_geometry(L, CTX):
    nb = L // CHUNK
    QB = ATT_Q_BLOCKS if nb % ATT_Q_BLOCKS == 0 else 1

    def blk(j):
        return pl.BlockSpec((CHUNK, LANES), lambda h, m: (jnp.clip(m * QB + j - 1, 0, nb - 1), h))

    kvs = [blk(j) for j in range(QB + 2)] + [pl.BlockSpec((CTX, LANES), lambda h, m: (L // CTX, h))]
    return nb, QB, kvs


def _attn_fwd(q, k, v, sink, *, L, CTX, Hkv, G, name):
    scale = float(ATT_HEAD_DIM) ** -0.5
    nb, QB, kvs = _attn_geometry(L, CTX)
    nkv = QB + 3

    def body(sink_ref, q_ref, *rest):
        kb, vb, (o_ref, lse_ref) = rest[:nkv], rest[nkv:2 * nkv], rest[2 * nkv:]
        h, m_ = pl.program_id(0), pl.program_id(1)
        sk = _sink_column(sink_ref, h, G)
        for sub in range(QB):
            rows = slice(sub * CHUNK, (sub + 1) * CHUNK)
            qs = _stack_heads_at(q_ref, rows, G)
            kall = jnp.concatenate([kb[sub + j][...] for j in range(3)] + [kb[-1][...]], axis=0)
            vall = jnp.concatenate([vb[sub + j][...] for j in range(3)] + [vb[-1][...]], axis=0)
            s_ = jnp.where(_key_mask(m_ * QB + sub, nb, CTX, G), _dot(qs, kall, _NT) * scale, NEG_INF)
            m = jnp.maximum(jnp.max(s_, axis=-1, keepdims=True), sk)
            p = jnp.exp(s_ - m)
            den = jnp.sum(p, axis=-1, keepdims=True) + jnp.exp(sk - m)
            o = _dot(p.astype(BF16), vall) / den
            lse = m + jnp.log(den)
            for g in range(G):
                o_ref[rows, g * LANES:(g + 1) * LANES] = o[g * CHUNK:(g + 1) * CHUNK].astype(o_ref.dtype)
                lse_ref[rows, g:g + 1] = lse[g * CHUNK:(g + 1) * CHUNK]

    qspec = pl.BlockSpec((QB * CHUNK, G * LANES), lambda h, m: (m, h))
    return pl.pallas_call(
        body, name=name, grid=(Hkv, nb // QB),
        in_specs=[pl.BlockSpec(memory_space=pltpu.SMEM), qspec] + kvs + kvs,
        out_specs=[qspec, pl.BlockSpec((None, QB * CHUNK, G), lambda h, m: (h, m, 0))],
        out_shape=[jax.ShapeDtypeStruct((L, Hkv * G * LANES), BF16), jax.ShapeDtypeStruct((Hkv, L, G), F32)],
        compiler_params=_cparams("parallel", "parallel"),
    )(sink, q, *([k] * nkv), *([v] * nkv))


def _attn_bwd(q, k, v, do, lse, sink, *, L, CTX, Hkv, G, name):
    scale = float(ATT_HEAD_DIM) ** -0.5
    nb, QB, kvs = _attn_geometry(L, CTX)
    nkv = QB + 3
    qspec = pl.BlockSpec((QB * CHUNK, G * LANES), lambda h, m: (m, h))
    rowspec = pl.BlockSpec((None, QB * CHUNK, G), lambda h, m: (h, m, 0))
    colspec = lambda rows: pl.BlockSpec((rows, LANES), lambda h, m: (0, h))

    def body(sink_ref, q_ref, do_ref, lse_ref, *rest):
        kb, vb = rest[:nkv], rest[nkv:2 * nkv]
        dq_ref, dk_ref, dv_ref, dkx_ref, dvx_ref, dsk_ref = rest[2 * nkv:]
        h, m_ = pl.program_id(0), pl.program_id(1)

        @pl.when(m_ == 0)
        def _():
            for r in (dk_ref, dv_ref, dkx_ref, dvx_ref, dsk_ref):
                r[...] = jnp.zeros_like(r)

        sk = _sink_column(sink_ref, h, G)
        for sub in range(QB):
            n = m_ * QB + sub
            rows = slice(sub * CHUNK, (sub + 1) * CHUNK)
            qs, dos = _stack_heads_at(q_ref, rows, G), _stack_heads_at(do_ref, rows, G)
            kall = jnp.concatenate([kb[sub + j][...] for j in range(3)] + [kb[-1][...]], axis=0)
            vall = jnp.concatenate([vb[sub + j][...] for j in range(3)] + [vb[-1][...]], axis=0)
            lse_c = _stack_columns_at(lse_ref, rows, G)
            p = jnp.where(_key_mask(n, nb, CTX, G), jnp.exp(_dot(qs, kall, _NT) * scale - lse_c), 0.0)
            dp = _dot(dos, vall, _NT)
            delta = jnp.sum(p * dp, axis=-1, keepdims=True)
            ds = (p * (dp - delta) * scale).astype(BF16)
            dq = _dot(ds, kall)
            dk_all = _dot(ds, qs, _TN)
            dv_all = _dot(p.astype(BF16), dos, _TN)
            for g in range(G):
                dq_ref[rows, g * LANES:(g + 1) * LANES] = dq[g * CHUNK:(g + 1) * CHUNK]
            for part, blk in enumerate((jnp.maximum(n - 1, 0), n, jnp.minimum(n + 1, nb - 1))):
                krows = pl.ds(pl.multiple_of(blk * CHUNK, CHUNK), CHUNK)
                dk_ref[krows, :] += dk_all[part * CHUNK:(part + 1) * CHUNK]
                dv_ref[krows, :] += dv_all[part * CHUNK:(part + 1) * CHUNK]
            dkx_ref[...] += dk_all[3 * CHUNK:]
            dvx_ref[...] += dv_all[3 * CHUNK:]
            dsink = -jnp.exp(sk - lse_c) * delta
            for g in range(G):
                dsk_ref[g:g + 1, :] += jnp.sum(dsink[g * CHUNK:(g + 1) * CHUNK])

    return pl.pallas_call(
        body, name=name, grid=(Hkv, nb // QB),
        in_specs=[pl.BlockSpec(memory_space=pltpu.SMEM), qspec, qspec, rowspec] + kvs + kvs,
        out_specs=[qspec, colspec(L), colspec(L), colspec(CTX), colspec(CTX),
                   pl.BlockSpec((None, 8, LANES), lambda h, m: (h, 0, 0))],
        out_shape=[jax.ShapeDtypeStruct((L, Hkv * G * LANES), F32),
                   jax.ShapeDtypeStruct((L, Hkv * LANES), F32), jax.ShapeDtypeStruct((L, Hkv * LANES), F32),
                   jax.ShapeDtypeStruct((CTX, Hkv * LANES), F32), jax.ShapeDtypeStruct((CTX, Hkv * LANES), F32),
                   jax.ShapeDtypeStruct((Hkv, 8, LANES), F32)],
        compiler_params=_cparams("parallel", "arbitrary"),
    )(sink, q, do, lse, *([k] * nkv), *([v] * nkv))


def _attn_assemble(dq, dk_lat, dv_lat, dk_ctx, dv_ctx, cos, sin, *, T, L, CTX, Hq, Hkv, tr, name):
    Wq, Wk = Hq * LANES, Hkv * LANES
    ctx_blocks = CTX // tr
    nl = L // tr

    def unrope(d, c, s_, heads):
        return d * jnp.tile(c, (1, heads)) + _swap32(d * jnp.tile(s_, (1, heads)))

    def body(is_ctx, dqt, dkl, dvl, dkc, dvc, c, s_):
        dq_ = jnp.where(is_ctx, 0.0, unrope(dqt, c, s_, Hq))
        dk_ = unrope(jnp.where(is_ctx, dkc, dkl), c, s_, Hkv)
        dv_ = jnp.where(is_ctx, dvc, dvl)
        return (jnp.concatenate([dq_, dk_, dv_], axis=1),), ()

    def ctx_map(i):
        return (jnp.clip(i - nl, 0, ctx_blocks - 1), 0)

    assert T % tr == 0 and L % tr == 0 and CTX % tr == 0
    in_specs = [pl.BlockSpec((tr, Wq), lambda i: (jnp.minimum(i, nl - 1), 0)),
                pl.BlockSpec((tr, Wk), lambda i: (jnp.minimum(i, nl - 1), 0)),
                pl.BlockSpec((tr, Wk), lambda i: (jnp.minimum(i, nl - 1), 0)),
                pl.BlockSpec((tr, Wk), ctx_map), pl.BlockSpec((tr, Wk), ctx_map),
                pl.BlockSpec((tr, LANES), lambda i: (i, 0)), pl.BlockSpec((tr, LANES), lambda i: (i, 0))]

    def kern(dq_r, dkl_r, dvl_r, dkc_r, dvc_r, c_r, s_r, o_ref):
        is_ctx = pl.program_id(0) >= nl
        (out,), _ = body(is_ctx, dq_r[...], dkl_r[...], dvl_r[...], dkc_r[...], dvc_r[...], c_r[...], s_r[...])
        o_ref[...] = out.astype(o_ref.dtype)

    return pl.pallas_call(
        kern, name=name, grid=(T // tr,), in_specs=in_specs,
        out_specs=pl.BlockSpec((tr, Wq + 2 * Wk), lambda i: (i, 0)),
        out_shape=jax.ShapeDtypeStruct((T, Wq + 2 * Wk), BF16),
        compiler_params=_cparams("parallel"),
    )(dq, dk_lat, dv_lat, dk_ctx, dv_ctx, cos, sin)


def _my_place():
    x, y, c = lax.axis_index("x"), lax.axis_index("y"), lax.axis_index("c")
    return x, y, c


def _all_gather_small(v, *, name):
    R, C = v.shape

    def body(x_ref, out_ref, send_sems, recv_sems, local_sem):
        x, y, c = _my_place()
        me, sibling = (x, y, c), (x, y, 1 - c)
        chips = [(1 - x, y), (x, 1 - y), (1 - x, 1 - y)]

        def slot(px, py, pc):
            return out_ref.at[4 * px + 2 * py + pc]

        def copy(k, block, to, src=None):
            return pltpu.make_async_remote_copy(
                src_ref=slot(*block) if src is None else src, dst_ref=slot(*block),
                send_sem=send_sems.at[k], recv_sem=recv_sems.at[k], device_id=to, device_id_type=MESH)

        mine = pltpu.make_async_copy(x_ref, slot(*me), local_sem)
        mine.start()
        first = [copy(0, me, sibling, src=x_ref)]
        first += [copy(1 + j, me, (*chip, c), src=x_ref) for j, chip in enumerate(chips)]
        for cp in first:
            cp.start()
        passed = [copy(4 + j, (*chip, c), sibling) for j, chip in enumerate(chips)]
        for j, chip in enumerate(chips):
            copy(1 + j, (*chip, c), me).wait_recv()
            passed[j].start()
        copy(0, sibling, me).wait_recv()
        for j, chip in enumerate(chips):
            copy(4 + j, (*chip, 1 - c), me).wait_recv()
        for cp in first + passed:
            cp.wait_send()
        mine.wait()

    return pl.pallas_call(
        body, name=name, out_shape=jax.ShapeDtypeStruct((N_DEV, R, C), v.dtype),
        in_specs=[pl.BlockSpec(memory_space=pltpu.VMEM)], out_specs=pl.BlockSpec(memory_space=pltpu.VMEM),
        scratch_shapes=[pltpu.SemaphoreType.DMA((7,)), pltpu.SemaphoreType.DMA((7,)), pltpu.SemaphoreType.DMA],
    )(v)


_HBM_SPEC = pl.BlockSpec(memory_space=pltpu.HBM)
_SEM_SPEC = pl.BlockSpec(memory_space=pltpu.SEMAPHORE)
_ANY_SPEC = pl.BlockSpec(memory_space=pl.ANY)
_DATAFLOW = pltpu.SideEffectType.DATAFLOW_SIDE_EFFECTING
N_PEERS = N_DEV - 1


def _peer(r):
    x, y, c = _my_place()
    return ((1 - x) if r & 4 else x, (1 - y) if r & 2 else y, (1 - c) if r & 1 else c)


def _index_of(place):
    return 4 * place[0] + 2 * place[1] + place[2]


def _slot(ref, axis, idx, size):
    if axis == 0:
        return ref.at[pl.ds(idx * size, size), :]
    return ref.at[:, pl.ds(idx * size, size)]


def _cast_place(w3, layer, axis, me_arr, dep, *, name):
    Ks, Ns = w3.shape[1], w3.shape[2]
    tr = _pick(Ks, 256, 16)
    nblk = Ks // tr
    full = (Ks * N_DEV, Ns) if axis == 0 else (Ks, Ns * N_DEV)
    if axis == 0:
        out_map = lambda i, me: (me[0] * nblk + i, 0)
    else:
        out_map = lambda i, me: (i, me[0])

    def body(me_ref, w_ref, dep_ref, o_ref):
        o_ref[...] = w_ref[...].astype(BF16)

    return pl.pallas_call(
        body, name=name, out_shape=jax.ShapeDtypeStruct(full, BF16),
        grid_spec=pltpu.PrefetchScalarGridSpec(
            num_scalar_prefetch=1, grid=(nblk,),
            in_specs=[pl.BlockSpec((None, tr, Ns), lambda i, me: (layer, i, 0)), pl.BlockSpec(memory_space=pl.ANY)],
            out_specs=pl.BlockSpec((tr, Ns), out_map)),
        compiler_params=_cparams("parallel"),
    )(me_arr, w3, dep)


AG_FIRST = 4
AG_CHIPS = 3


def _sibling():
    x, y, c = _my_place()
    return (x, y, 1 - c)


def _chip_peer(j, same_core=True):
    x, y, c = _my_place()
    px = (1 - x) if j in (0, 2) else x
    py = (1 - y) if j in (1, 2) else y
    return (px, py, c if same_core else 1 - c)


def _gather_start(lands, axes, after, *, name):
    nt = len(lands)
    sizes = [l.shape[a] // N_DEV for l, a in zip(lands, axes)]

    def body(*refs):
        ins, send_sems, recv_sems, token = refs[:nt], refs[nt + 1], refs[nt + 2], refs[-1]
        my_idx = _index_of(_my_place())
        for t in range(nt):
            mine = _slot(ins[t], axes[t], my_idx, sizes[t])
            for k, to in enumerate([_sibling()] + [_chip_peer(j) for j in range(AG_CHIPS)]):
                pltpu.make_async_remote_copy(src_ref=mine, dst_ref=mine, send_sem=send_sems.at[t * AG_FIRST + k],
                                             recv_sem=recv_sems.at[t * AG_FIRST + k], device_id=to,
                                             device_id_type=MESH).start()
        token[...] = jnp.zeros_like(token)

    res = pl.pallas_call(
        body, name=name,
        out_shape=(pltpu.SemaphoreType.DMA((nt * AG_FIRST,)), pltpu.SemaphoreType.DMA((nt * AG_FIRST,)),
                   *[pltpu.HBM(l.shape, l.dtype) for l in lands], jax.ShapeDtypeStruct((8, LANES), F32)),
        in_specs=[_HBM_SPEC] * nt + [_ANY_SPEC],
        out_specs=(_SEM_SPEC, _SEM_SPEC, *[_HBM_SPEC] * nt, pl.BlockSpec(memory_space=pltpu.VMEM)),
        input_output_aliases={t: 2 + t for t in range(nt)},
        compiler_params=pltpu.CompilerParams(has_side_effects=_DATAFLOW),
    )(*[pltpu.with_memory_space_constraint(l, pltpu.HBM) for l in lands], after)
    return res[0], res[1], list(res[2:2 + nt]), res[-1]


def _gather_forward(send_a, recv_a, lands, axes, after, *, name):
    nt = len(lands)
    sizes = [l.shape[a] // N_DEV for l, a in zip(lands, axes)]

    def body(*refs):
        ins, send_a, recv_a = refs[:nt], refs[nt], refs[nt + 1]
        send_f, recv_f, token = refs[nt + 3], refs[nt + 4], refs[-1]
        my_idx = _index_of(_my_place())
        for t in range(nt):
            for j in range(AG_CHIPS):
                src_dev = _chip_peer(j)
                arrived = _slot(ins[t], axes[t], _index_of(src_dev), sizes[t])
                pltpu.make_async_remote_copy(
                    src_ref=_slot(ins[t], axes[t], my_idx, sizes[t]), dst_ref=arrived,
                    send_sem=send_a.at[t * AG_FIRST + 1 + j], recv_sem=recv_a.at[t * AG_FIRST + 1 + j],
                    device_id=src_dev, device_id_type=MESH).wait_recv()
                pltpu.make_async_remote_copy(src_ref=arrived, dst_ref=arrived, send_sem=send_f.at[t * AG_CHIPS + j],
                                             recv_sem=recv_f.at[t * AG_CHIPS + j], device_id=_sibling(),
                                             device_id_type=MESH).start()
        token[...] = jnp.zeros_like(token)

    res = pl.pallas_call(
        body, name=name,
        out_shape=(pltpu.SemaphoreType.DMA((nt * AG_CHIPS,)), pltpu.SemaphoreType.DMA((nt * AG_CHIPS,)),
                   *[pltpu.HBM(l.shape, l.dtype) for l in lands], jax.ShapeDtypeStruct((8, LANES), F32)),
        in_specs=[_HBM_SPEC] * nt + [_SEM_SPEC, _SEM_SPEC, _ANY_SPEC],
        out_specs=(_SEM_SPEC, _SEM_SPEC, *[_HBM_SPEC] * nt, pl.BlockSpec(memory_space=pltpu.VMEM)),
        input_output_aliases={t: 2 + t for t in range(nt)},
        compiler_params=pltpu.CompilerParams(has_side_effects=_DATAFLOW),
    )(*lands, send_a, recv_a, after)
    return res[0], res[1], list(res[2:2 + nt]), res[-1]


def _gather_wait(send_a, recv_a, send_f, recv_f, lands, axes, after, *, name):
    nt = len(lands)
    sizes = [l.shape[a] // N_DEV for l, a in zip(lands, axes)]

    def body(*refs):
        ins, send_a, recv_a, send_f, recv_f = refs[:nt], refs[nt], refs[nt + 1], refs[nt + 2], refs[nt + 3]
        my_idx = _index_of(_my_place())
        sib = _sibling()
        for t in range(nt):
            mine = _slot(ins[t], axes[t], my_idx, sizes[t])
            for k, to in enumerate([sib] + [_chip_peer(j) for j in range(AG_CHIPS)]):
                pltpu.make_async_remote_copy(src_ref=mine, dst_ref=mine, send_sem=send_a.at[t * AG_FIRST + k],
                                             recv_sem=recv_a.at[t * AG_FIRST + k], device_id=to,
                                             device_id_type=MESH).wait_send()
            pltpu.make_async_remote_copy(src_ref=mine, dst_ref=_slot(ins[t], axes[t], _index_of(sib), sizes[t]),
                                         send_sem=send_a.at[t * AG_FIRST], recv_sem=recv_a.at[t * AG_FIRST],
                                         device_id=sib, device_id_type=MESH).wait_recv()
            for j in range(AG_CHIPS):
                sent = _slot(ins[t], axes[t], _index_of(_chip_peer(j)), sizes[t])
                got = _slot(ins[t], axes[t], _index_of(_chip_peer(j, same_core=False)), sizes[t])
                cp = pltpu.make_async_remote_copy(src_ref=sent, dst_ref=got, send_sem=send_f.at[t * AG_CHIPS + j],
                                                  recv_sem=recv_f.at[t * AG_CHIPS + j], device_id=sib,
                                                  device_id_type=MESH)
                cp.wait_send()
                cp.wait_recv()

    res = pl.pallas_call(
        body, name=name, out_shape=[pltpu.HBM(l.shape, l.dtype) for l in lands],
        in_specs=[_HBM_SPEC] * nt + [_SEM_SPEC] * 4 + [_ANY_SPEC], out_specs=[_HBM_SPEC] * nt,
        input_output_aliases={t: t for t in range(nt)},
        compiler_params=pltpu.CompilerParams(has_side_effects=_DATAFLOW),
    )(*lands, send_a, recv_a, send_f, recv_f, after)
    return list(res)


def _scatter_start(dw, axis, *, name):
    size = dw.shape[axis] // N_DEV
    land_shape = (N_PEERS, size, dw.shape[1]) if axis == 0 else (N_PEERS, dw.shape[0], size)

    def body(dw_ref, land_ref, send_sems, recv_sems, dw_thru, land_thru, token):
        for r in range(1, N_DEV):
            p = _peer(r)
            pltpu.make_async_remote_copy(src_ref=_slot(dw_ref, axis, _index_of(p), size), dst_ref=land_ref.at[r - 1],
                                         send_sem=send_sems.at[r - 1], recv_sem=recv_sems.at[r - 1], device_id=p,
                                         device_id_type=MESH).start()
        token[...] = jnp.zeros_like(token)

    land = pltpu.with_memory_space_constraint(lax.empty(land_shape, dw.dtype), pltpu.HBM)
    return pl.pallas_call(
        body, name=name,
        out_shape=(pltpu.SemaphoreType.DMA((N_PEERS,)), pltpu.SemaphoreType.DMA((N_PEERS,)),
                   pltpu.HBM(dw.shape, dw.dtype), pltpu.HBM(land_shape, dw.dtype), jax.ShapeDtypeStruct((8, LANES), F32)),
        in_specs=[_HBM_SPEC, _HBM_SPEC],
        out_specs=(_SEM_SPEC, _SEM_SPEC, _HBM_SPEC, _HBM_SPEC, pl.BlockSpec(memory_space=pltpu.VMEM)),
        input_output_aliases={0: 2, 1: 3},
        compiler_params=pltpu.CompilerParams(has_side_effects=_DATAFLOW),
    )(pltpu.with_memory_space_constraint(dw, pltpu.HBM), land)


def _scatter_wait(send_sems, recv_sems, dw, land, axis, after, *, name):
    size = dw.shape[axis] // N_DEV

    def body(dw_ref, land_ref, send_sems, recv_sems, after_ref, dw_thru, land_thru):
        for r in range(1, N_DEV):
            p = _peer(r)
            cp = pltpu.make_async_remote_copy(src_ref=_slot(dw_ref, axis, _index_of(p), size), dst_ref=land_ref.at[r - 1],
                                              send_sem=send_sems.at[r - 1], recv_sem=recv_sems.at[r - 1], device_id=p,
                                              device_id_type=MESH)
            cp.wait_send()
            cp.wait_recv()

    return pl.pallas_call(
        body, name=name, out_shape=(pltpu.HBM(dw.shape, dw.dtype), pltpu.HBM(land.shape, land.dtype)),
        in_specs=[_HBM_SPEC, _HBM_SPEC, _SEM_SPEC, _SEM_SPEC, _ANY_SPEC], out_specs=(_HBM_SPEC, _HBM_SPEC),
        input_output_aliases={0: 0, 1: 1},
        compiler_params=pltpu.CompilerParams(has_side_effects=_DATAFLOW),
    )(dw, land, send_sems, recv_sems, after)


def _adamw_math(w, g, m, v):
    m = ADAM_B1 * m + (1.0 - ADAM_B1) * g
    v = ADAM_B2 * v + (1.0 - ADAM_B2) * (g * g)
    m_hat = m / (1.0 - ADAM_B1 ** ADAM_STEP)
    v_hat = v / (1.0 - ADAM_B2 ** ADAM_STEP)
    delta = -ADAM_LR * (m_hat / (jnp.sqrt(v_hat) + ADAM_EPS) + ADAM_WD * w)
    return delta, m, v


def _adamw_sharded(w, m, v, layer, dw, land, axis, me_arr, prev, *, name):
    nl, Ks, Ns = w.shape
    tr = _pick(Ks, 128, 16)
    nblk = Ks // tr
    if axis == 0:
        own_map = lambda i, me: (me[0] * nblk + i, 0)
    else:
        own_map = lambda i, me: (i, me[0])
    wspec = pl.BlockSpec((None, tr, Ns), lambda i, me: (layer, i, 0))
    n_prev = 0 if prev is None else 4

    def body(me_ref, w_ref, m_ref, v_ref, own_ref, r_ref, *rest):
        g_ref, d_ref, nm_ref, nv_ref = rest[n_prev:]
        g = own_ref[...].astype(F32)
        for r in range(N_PEERS):
            g = g + r_ref[r].astype(F32)
        delta, nm, nv = _adamw_math(w_ref[...], g, m_ref[...], v_ref[...])
        g_ref[...], d_ref[...], nm_ref[...], nv_ref[...] = g, delta, nm, nv

    return pl.pallas_call(
        body, name=name, out_shape=[jax.ShapeDtypeStruct((nl, Ks, Ns), F32)] * 4,
        grid_spec=pltpu.PrefetchScalarGridSpec(
            num_scalar_prefetch=1, grid=(nblk,),
            in_specs=[wspec, wspec, wspec, pl.BlockSpec((tr, Ns), own_map),
                      pl.BlockSpec((N_PEERS, tr, Ns), lambda i, me: (0, i, 0))] + [_ANY_SPEC] * n_prev,
            out_specs=[wspec] * 4),
        input_output_aliases={6 + k: k for k in range(n_prev)},
        compiler_params=_cparams("parallel"),
    )(me_arr, w, m, v, dw, land, *(prev or []))


def _adamw_flat(w, g, m, v, *, name):
    def body(w_ref, g_ref, m_ref, v_ref, d_ref, nm_ref, nv_ref):
        d_ref[...], nm_ref[...], nv_ref[...] = _adamw_math(w_ref[...], g_ref[...], m_ref[...], v_ref[...])

    spec = pl.BlockSpec(memory_space=pltpu.VMEM)
    return pl.pallas_call(body, name=name, in_specs=[spec] * 4, out_specs=[spec] * 3,
                          out_shape=[jax.ShapeDtypeStruct(w.shape, F32)] * 3)(w, g, m, v)


def _sum_devices(a, *, name):
    def body(a_ref, o_ref):
        s = a_ref[0]
        for d in range(1, N_DEV):
            s = s + a_ref[d]
        o_ref[...] = s

    spec = pl.BlockSpec(memory_space=pltpu.VMEM)
    return pl.pallas_call(body, name=name, in_specs=[spec], out_specs=spec,
                          out_shape=jax.ShapeDtypeStruct(a.shape[1:], F32))(a)


def _ada_mods(c16, ada_w, ada_b_cols, *, name):
    nl, D, cols = ada_w.shape
    bn = _pick(cols, 512)

    def body(c_ref, w_ref, b_ref, o_ref):
        cond = _silu(c_ref[...]).astype(BF16)
        o_ref[...] = _dot(cond, w_ref[...].astype(BF16)) + b_ref[...]

    return pl.pallas_call(
        body, name=name, grid=(nl, cols // bn),
        in_specs=[pl.BlockSpec((16, D), lambda l, j: (0, 0)), pl.BlockSpec((None, D, bn), lambda l, j: (l, 0, j)),
                  pl.BlockSpec((None, 1, bn), lambda l, j: (l, 0, j))],
        out_specs=pl.BlockSpec((None, 16, bn), lambda l, j: (l, 0, j)),
        out_shape=jax.ShapeDtypeStruct((nl, 16, cols), F32),
        compiler_params=_cparams("parallel", "parallel"),
    )(c16, ada_w, ada_b_cols)


def _ada_bwd(cond_t, dmod, w, m, v, *, name):
    nl, D, cols = w.shape
    tr = _pick(D, 256, 8)

    def body(ct_ref, dm_ref, w_ref, m_ref, v_ref, g_ref, d_ref, nm_ref, nv_ref, dc_ref):
        ct, dm, wt = ct_ref[...], dm_ref[...], w_ref[...]
        g = ct[:, 0:1] * dm[0:1, :]
        for r in range(1, N_DEV + 1):
            g = g + ct[:, r:r + 1] * dm[r:r + 1, :]
        delta, nm, nv = _adamw_math(wt, g, m_ref[...], v_ref[...])
        g_ref[...], d_ref[...], nm_ref[...], nv_ref[...] = g, delta, nm, nv
        dc_ref[...] = jnp.sum(wt * dm[N_DEV:N_DEV + 1, :], axis=-1, keepdims=True)

    wspec = pl.BlockSpec((None, tr, cols), lambda l, i: (l, i, 0))
    return pl.pallas_call(
        body, name=name, grid=(nl, D // tr),
        in_specs=[pl.BlockSpec((tr, 16), lambda l, i: (i, 0)), pl.BlockSpec((None, 16, cols), lambda l, i: (l, 0, 0)),
                  wspec, wspec, wspec],
        out_specs=[wspec] * 4 + [pl.BlockSpec((None, tr, 1), lambda l, i: (l, i, 0))],
        out_shape=[jax.ShapeDtypeStruct((nl, D, cols), F32)] * 4 + [jax.ShapeDtypeStruct((nl, D, 1), F32)],
        compiler_params=_cparams("parallel", "parallel"),
    )(cond_t, dmod, w, m, v)


def _rope_tables(L, CTX):
    def angles(pos, dim):
        inv_freq = ROPE_BASE ** (-jnp.arange(0, dim, 2, dtype=F32) / dim)
        return pos.astype(F32)[:, None] * inv_freq[None, :]

    def pad(cos, sin):
        return (jnp.concatenate([cos, jnp.ones((CTX, LANES), F32)], 0),
                jnp.concatenate([sin, jnp.zeros((CTX, LANES), F32)], 0))

    ret = angles(jnp.arange(L), 2 * LANES)
    ret_cs = pad(jnp.cos(ret), jnp.sin(ret))
    rows = angles(jnp.arange(L) // GRID_W, ATT_HEAD_DIM // 2)
    cols = angles(jnp.arange(L) % GRID_W, ATT_HEAD_DIM // 2)
    cos = jnp.concatenate([jnp.cos(rows)] * 2 + [jnp.cos(cols)] * 2, axis=1)
    sin = jnp.concatenate([-jnp.sin(rows), jnp.sin(rows), -jnp.sin(cols), jnp.sin(cols)], axis=1)
    return ret_cs, pad(cos, sin)


def kernel(x, c, ctx, c_ctx, ada_w, ada_b, norm_mix_g, norm_mlp_g, mlp_w1, mlp_w2, ret_w_in, ret_w_out, ret_decay_fwd, ret_decay_bwd, attn_w_in, attn_w_out, attn_sink, final_norm_g, loss_target, m_c_ctx, m_ada_w, m_ada_b, m_norm_mix_g, m_norm_mlp_g, m_mlp_w1, m_mlp_w2, m_ret_w_in, m_ret_w_out, m_ret_decay_fwd, m_ret_decay_bwd, m_attn_w_in, m_attn_w_out, m_attn_sink, m_final_norm_g, v_c_ctx, v_ada_w, v_ada_b, v_norm_mix_g, v_norm_mlp_g, v_mlp_w1, v_mlp_w2, v_ret_w_in, v_ret_w_out, v_ret_decay_fwd, v_ret_decay_bwd, v_attn_w_in, v_attn_w_out, v_attn_sink, v_final_norm_g):
    L, D = x.shape[1], x.shape[2]
    CTX = ctx.shape[1]
    T = L + CTX
    RH = ret_decay_fwd.shape[-1]
    assert D == RH * 2 * LANES and ada_w.shape[0] == 2 and ret_w_in.shape[0] == 1 and attn_w_in.shape[0] == 1
    Hq = attn_sink.shape[-1]
    Hkv = (attn_w_in.shape[-1] * N_DEV // ATT_HEAD_DIM - Hq) // 2
    G = Hq // Hkv
    FF = mlp_w1.shape[-1] * N_DEV
    Wq_r, Wv_r = RH * 2 * LANES, RH * 4 * LANES
    acols = ada_w.shape[-1]
    tr = _pick(CTX, 256, 8)
    tr_wide = _pick(CTX, 128, 8)
    bmT = T // 4 if (T % 64 == 0) else T
    bmL = L // 4 if (L % 64 == 0) else L
    x_idx, y_idx, c_idx = lax.axis_index("x"), lax.axis_index("y"), lax.axis_index("c")
    me = 4 * x_idx + 2 * y_idx + c_idx
    me_arr = jnp.reshape(me, (1,)).astype(jnp.int32)

    (rcos, rsin), (acos, asin) = _rope_tables(L, CTX)
    lg_f = jax.nn.log_sigmoid(ret_decay_fwd.astype(F32))
    lg_b = jax.nn.log_sigmoid(ret_decay_bwd.astype(F32))

    c_pad = jnp.concatenate([c.astype(F32), jnp.zeros((7, D), F32)], 0)
    c_all = _all_gather_small(c_pad, name="ag_c")[:, 0, :]
    c16 = jnp.concatenate([c_all, c_ctx[None, :], jnp.zeros((7, D), F32)], 0)
    ada_b_cols = lax.dynamic_slice_in_dim(ada_b, me * acols, acols, axis=1)[:, None, :]
    mods_shard = _ada_mods(c16, ada_w, ada_b_cols, name="ada_mods")
    mods_all = _all_gather_small(mods_shard.reshape(32, acols), name="ag_mods")

    wdefs = {"ret_in": (ret_w_in, 0, 1), "ret_out": (ret_w_out, 0, 0), "w1_0": (mlp_w1, 0, 1), "w2_0": (mlp_w2, 0, 0),
             "attn_in": (attn_w_in, 0, 1), "attn_out": (attn_w_out, 0, 0), "w1_1": (mlp_w1, 1, 1), "w2_1": (mlp_w2, 1, 0)}
    groups = [["ret_in"], ["ret_out", "w1_0", "w2_0"], ["attn_in", "attn_out"], ["w1_1", "w2_1"]]

    placed = {}

    def ag_start(gi, after):
        g_axes = [wdefs[k][2] for k in groups[gi]]
        ssem, rsem, lands, tok_ = _gather_start([placed[k] for k in groups[gi]], g_axes, after, name=f"ag_start{gi}")
        return dict(a=(ssem, rsem), lands=lands, axes=g_axes, gi=gi), tok_

    def ag_forward(g, after):
        fs, fr, g["lands"], tok_ = _gather_forward(*g["a"], g["lands"], g["axes"], after, name=f"ag_forward{g['gi']}")
        g["f"] = (fs, fr)
        return tok_

    def ag_wait(g, after):
        return _gather_wait(*g["a"], *g["f"], g["lands"], g["axes"], after, name=f"ag_wait{g['gi']}")

    placed["ret_in"] = _cast_place(*wdefs["ret_in"], me_arr, mods_all, name="place_ret_in")
    g0, tok = ag_start(0, mods_all)
    last_cast = tok
    for keys in groups[1:]:
        for k in keys:
            last_cast = placed[k] = _cast_place(*wdefs[k], me_arr, last_cast, name=f"place_{k}")
    mods_all = (mods_all + tok[0, 0]).reshape(N_DEV, 2, 16, acols).transpose(1, 2, 0, 3).reshape(2, 16, 6, D)
    mod_lat = lax.dynamic_index_in_dim(mods_all, me, axis=1, keepdims=False)
    mod_ctx = mods_all[:, N_DEV]

    def pack(i, ks, kc):
        return jnp.stack([mod_lat[i, ks], mod_lat[i, kc], mod_ctx[i, ks], mod_ctx[i, kc]], 0)

    def gates(i, k):
        return jnp.stack([mod_lat[i, k], mod_ctx[i, k]], 0)

    def gate_epilogue(x_rows_lat_only):
        def epi(acc, i, j, xt, gv):
            if x_rows_lat_only:
                gate = gv[0:1, :]
            else:
                row = i * acc.shape[0] + lax.broadcasted_iota(jnp.int32, (acc.shape[0], 1), 0)
                gate = jnp.where(row >= L, gv[1:2, :], gv[0:1, :])
            return xt + gate * acc, acc
        return epi

    w1, w2 = {}, {}

    mmT = dict(M=T, bm=bmT)
    mmL = dict(M=L, bm=bmL)

    def bn_of(n, off=0):
        b = MM_BN
        while n % b or off % b:
            b -= LANES
        return b

    X0 = jnp.concatenate([x[0], ctx[0]], axis=0)
    g_mix0, g_mlp0 = norm_mix_g[0:1], norm_mlp_g[0:1]
    g_mix1, g_mlp1 = norm_mix_g[1:2], norm_mlp_g[1:2]
    a0 = _normmod(X0, g_mix0, pack(0, 0, 1), R=T, L=L, tr=tr, name="normmod_mix0",
                  dep=(last_cast, rcos, rsin, acos, asin))
    tok = ag_forward(g0, a0)
    (wr_in,) = ag_wait(g0, tok)
    g1, tok = ag_start(1, wr_in)

    bn_qk = _pick(Wq_r, MM_BN, 2 * LANES)
    nq_blocks = Wq_r // bn_qk
    kscale = float(2 * LANES) ** -0.5

    def rope_epi(acc, i, j, cos, sin):
        parts = []
        for h in range(acc.shape[1] // (2 * LANES)):
            x1 = acc[:, h * 2 * LANES:h * 2 * LANES + LANES]
            x2 = acc[:, h * 2 * LANES + LANES:(h + 1) * 2 * LANES]
            parts += [x1 * cos - x2 * sin, x2 * cos + x1 * sin]
        return (jnp.concatenate(parts, axis=1) * jnp.where(j < nq_blocks, 1.0, kscale),)

    def row_tile(arr, bm):
        return (arr, (bm, LANES), lambda i, j: (i, 0))

    (qk0,) = _mm(a0, wr_in, "nn", [BF16], N=2 * Wq_r, K=D, bn=bn_qk, bk=D, name="ret_qk", epilogue=rope_epi,
                 extras=[row_tile(rcos, bmT), row_tile(rsin, bmT)], dep=tok, **mmT)
    bn_vg = bn_of(2 * Wv_r, 2 * Wq_r)
    (vg0,) = _mm(a0, wr_in, "nn", [BF16], N=2 * Wv_r, K=D, bn=bn_vg, bk=D, name="ret_vg", b_col0=2 * Wq_r, dep=tok,
                 **mmT)

    of, st_f = _ret_fwd(qk0, vg0, lg_f, None, T=T, L=L, H=RH, rev=False, name="ret_scan_f")
    o0, st_b = _ret_fwd(qk0, vg0, lg_b, of, T=T, L=L, H=RH, rev=True, name="ret_scan_b")
    tok = ag_forward(g1, o0)
    z0 = _readout(o0, vg0, T=T, L=L, H=RH, tr=tr_wide, name="ret_readout", dep=tok)
    wr_out, w1[0], w2[0] = ag_wait(g1, z0)
    g2, tok = ag_start(2, wr_out)
    g3, tok = ag_start(3, tok)

    bnD = _pick(D, MM_BN)

    def xtile(arr, bm):
        return (arr, (bm, bnD), lambda i, j: (i, j))

    def gtile(gv):
        return (gv, (2, bnD), lambda i, j: (0, j))

    bk_max = 2048

    def quarter(rows, div=4):
        return dict(M=rows["M"], bm=rows["bm"] // div if rows["bm"] % (16 * div) == 0 else rows["bm"])
    X1, ro0 = _mm(z0, wr_out, "nn", [F32, BF16], N=D, K=Wv_r, bn=bnD, bk=Wv_r, name="ret_out", cols_outer=True,
                  epilogue=gate_epilogue(False), extras=[xtile(X0, quarter(mmT, 2)["bm"]), gtile(gates(0, 2))], dep=tok,
                  **quarter(mmT, 2))

    def mlp_fwd(Xin, i, g_mlp, rows, name):
        a = _normmod(Xin, g_mlp, pack(i, 3, 4), R=rows["M"], L=L, tr=tr, name=f"normmod_mlp{name}")

        def relu2(acc, i_, j_):
            u = jnp.maximum(acc, 0.0)
            return u, u * u

        bnF = _pick(FF, MM_BN)
        u, r = _mm(a, w1[i], "nn", [BF16, BF16], N=FF, K=D, bn=bnF, bk=D, name=f"mlp_up{name}", epilogue=relu2, **rows)
        Xout, mo = _mm(r, w2[i], "nn", [F32, BF16], N=D, K=FF, bn=bnD, bk=FF, name=f"mlp_down{name}",
                       epilogue=gate_epilogue(rows["M"] == L), cols_outer=True,
                       extras=[xtile(Xin, quarter(rows)["bm"]), gtile(gates(i, 5))], **quarter(rows))
        return a, u, r, Xout, mo

    a1, u0, r0, X2, mo0 = mlp_fwd(X1, 0, g_mlp0, mmT, "0")

    tok = ag_forward(g2, X2)
    a2 = _normmod(X2, g_mix1, pack(1, 0, 1), R=T, L=L, tr=tr, name="normmod_mix1", dep=tok)
    wa_in, wa_out = ag_wait(g2, a2)
    Wq_a, Wk_a = Hq * LANES, Hkv * LANES

    def arope_epi(acc, i, j, cos, sin):
        heads = acc.shape[1] // LANES
        return (acc * jnp.tile(cos, (1, heads)) + _swap32(acc) * jnp.tile(sin, (1, heads)),)

    bn_q = _pick(Wq_a, MM_BN)
    (q1,) = _mm(a2, wa_in, "nn", [BF16], N=Wq_a, K=D, bn=bn_q, bk=D, name="attn_q", epilogue=arope_epi,
                extras=[row_tile(acos, bmL), row_tile(asin, bmL)], **mmL)
    bn_k = bn_of(Wk_a, Wq_a)
    (k1,) = _mm(a2, wa_in, "nn", [BF16], N=Wk_a, K=D, bn=bn_k, bk=D, name="attn_k", b_col0=Wq_a, epilogue=arope_epi,
                extras=[row_tile(acos, bmT), row_tile(asin, bmT)], **mmT)
    bn_v = bn_of(Wk_a, Wq_a + Wk_a)
    (v1,) = _mm(a2, wa_in, "nn", [BF16], N=Wk_a, K=D, bn=bn_v, bk=D, name="attn_v", b_col0=Wq_a + Wk_a, **mmT)
    tok = ag_forward(g3, q1)
    o1, lse = _attn_fwd(q1, k1, v1, attn_sink + tok[0, 0], L=L, CTX=CTX, Hkv=Hkv, G=G, name="attn_fwd")
    w1[1], w2[1] = ag_wait(g3, o1)
    X3, ao = _mm(o1, wa_out, "nn", [F32, BF16], N=D, K=Wq_a, bn=bnD, bk=_pick(Wq_a, 2048), name="attn_out",
                 epilogue=gate_epilogue(True), extras=[xtile(X2, bmL), gtile(gates(1, 2))], **mmL)
    a3, u1, r1, X4, mo1 = mlp_fwd(X3, 1, g_mlp1, mmL, "1")

    dX4, dmo1, acc_head = _loss_head(X4, loss_target[0], mo1, final_norm_g[None, :], gates(1, 5)[0:1], L=L, tr=tr,
                                     name="loss_head")
    loss_part = jnp.sum(acc_head[0, 0])
    d_gf = acc_head[0, 1]
    zeros_d = jnp.zeros((D,), F32)
    dmod_lat = [[zeros_d] * 6, [zeros_d] * 6]
    dmod_ctx = [[zeros_d] * 6, [zeros_d] * 6]
    dmod_lat[1][5] = acc_head[0, 2]

    def dw_mm(a, b, M, N, K, name):
        return _mm(a, b, "tn", [BF16], M=M, N=N, K=K, bm=_pick(M, 512), bn=_pick(N, MM_BN), bk=K, name=name)[0]

    def mlp_bwd(dmo, a, u, r, i, rows, name):
        Mr = rows["M"]

        def times_2u(acc, i_, j_, ut):
            return (acc * (2.0 * ut.astype(F32)),)

        bnF = _pick(FF, MM_BN)
        dw2 = dw_mm(r, dmo, FF, D, Mr, f"mlp_down_dw{name}")
        tok_ = send_grad(f"w2_{i}", dw2, 0)
        (dh,) = _mm(dmo, w2[i], "nt", [BF16], N=FF, K=D, bn=bnF, bk=D, name=f"mlp_down_dx{name}", epilogue=times_2u,
                    extras=[(u, (rows["bm"], bnF), lambda i_, j_: (i_, j_))], dep=tok_, **rows)
        dw1 = dw_mm(a, dh, D, FF, Mr, f"mlp_up_dw{name}")
        tok_ = send_grad(f"w1_{i}", dw1, 1)
        (da,) = _mm(dh, w1[i], "nt", [BF16], N=D, K=FF, bn=bnD, bk=FF, name=f"mlp_up_dx{name}", dep=tok_,
                    cols_outer=True, **quarter(rows))
        return da

    pending = []

    def send_grad(key, dw, axis):
        ssem, rsem, dw_thru, land, tok_ = _scatter_start(dw, axis, name=f"rs_start_{key}")
        pending.append((key, axis, ssem, rsem, dw_thru, land))
        return tok_

    da3 = mlp_bwd(dmo1, a3, u1, r1, 1, mmL, "1")
    dX3, dao, acc = _normmod_bwd(X3, da3, dX4, False, g_mlp1, pack(1, 3, 4), (ao, gates(1, 2)), R=L, L=L, tr=tr,
                                 name="normmod_mlp1_bwd")
    dmod_lat[1][3], dmod_lat[1][4], d_gmlp1, dmod_lat[1][2] = acc[0, 0], acc[0, 1], acc[0, 2], acc[0, 3]

    dwa_out = dw_mm(o1, dao, Wq_a, D, L, "attn_out_dw")
    tok = send_grad("attn_out", dwa_out, 0)
    (do1,) = _mm(dao, wa_out, "nt", [BF16], N=Wq_a, K=D, bn=_pick(Wq_a, MM_BN), bk=D, name="attn_out_dx", dep=tok, **mmL)
    dq1, dk1, dv1, dkx, dvx, dsink_acc = _attn_bwd(q1, k1, v1, do1, lse, attn_sink, L=L, CTX=CTX, Hkv=Hkv, G=G,
                                                   name="attn_bwd")
    dp1 = _attn_assemble(dq1, dk1, dv1, dkx, dvx, acos, asin, T=T, L=L, CTX=CTX, Hq=Hq, Hkv=Hkv, tr=tr_wide,
                         name="attn_assemble")
    Wa_in = Wq_a + 2 * Wk_a
    dwa_in = dw_mm(a2, dp1, D, Wa_in, T, "attn_in_dw")
    tok = send_grad("attn_in", dwa_in, 1)
    (da2,) = _mm(dp1, wa_in, "nt", [BF16], N=D, K=Wa_in, bn=bnD, bk=_pick(Wa_in, 2 * bk_max), name="attn_in_dx",
                 dep=tok, **mmT)
    dX2, dmo0, acc = _normmod_bwd(X2, da2, dX3, True, g_mix1, pack(1, 0, 1), (mo0, gates(0, 5)), R=T, L=L, tr=tr,
                                  name="normmod_mix1_bwd")
    dmod_lat[1][0], dmod_lat[1][1], d_gmix1, dmod_lat[0][5] = acc[0, 0], acc[0, 1], acc[0, 2] + acc[1, 2], acc[0, 3]
    dmod_ctx[1][0], dmod_ctx[1][1], dmod_ctx[0][5] = acc[1, 0], acc[1, 1], acc[1, 3]

    da1 = mlp_bwd(dmo0, a1, u0, r0, 0, mmT, "0")
    dX1, dro0, acc = _normmod_bwd(X1, da1, dX2, False, g_mlp0, pack(0, 3, 4), (ro0, gates(0, 2)), R=T, L=L, tr=tr,
                                  name="normmod_mlp0_bwd")
    dmod_lat[0][3], dmod_lat[0][4], d_gmlp0, dmod_lat[0][2] = acc[0, 0], acc[0, 1], acc[0, 2] + acc[1, 2], acc[0, 3]
    dmod_ctx[0][3], dmod_ctx[0][4], dmod_ctx[0][2] = acc[1, 0], acc[1, 1], acc[1, 3]

    dwr_out = dw_mm(z0, dro0, Wv_r, D, T, "ret_out_dw")
    tok = send_grad("ret_out", dwr_out, 0)
    (dz0,) = _mm(dro0, wr_out, "nt", [BF16], N=Wv_r, K=D, bn=_pick(Wv_r, MM_BN), bk=D, name="ret_out_dx", dep=tok, **mmT)
    do0, dg0 = _readout_bwd(dz0, o0, vg0, T=T, L=L, H=RH, tr=tr_wide, name="ret_readout_bwd")
    (dq_f, dk_f, dv_f, dlg_f), (dq_b, dk_b, dv_b, dlg_b) = _ret_bwd(qk0, vg0, do0, (st_f, st_b), (lg_f, lg_b),
                                                                    T=T, L=L, H=RH, name="ret_scan_bwd")
    dp0 = _ret_assemble(dq_f, dq_b, dk_f, dk_b, dv_f, dv_b, dg0, rcos, rsin, T=T, L=L, H=RH, tr=tr_wide,
                        name="ret_assemble")
    Wr_in = 2 * Wq_r + 2 * Wv_r
    dwr_in = dw_mm(a0, dp0, D, Wr_in, T, "ret_in_dw")
    tok = send_grad("ret_in", dwr_in, 1)
    (da0,) = _mm(dp0, wr_in, "nt", [BF16], N=D, K=Wr_in, bn=bnD, bk=Wr_in, name="ret_in_dx", dep=tok, cols_outer=True,
                 b_buffers=1, **quarter(mmT))
    dX0, acc = _normmod_bwd(X0, da0, dX1, False, g_mix0, pack(0, 0, 1), None, R=L, L=L, tr=tr, name="normmod_mix0_bwd")
    _, acc_c = _normmod_bwd(X0, da0, dX1, False, g_mix0, pack(0, 0, 1), None, R=CTX, L=0, tr=tr, row0=L,
                            name="normmod_mix0_bwd_ctx")
    dmod_lat[0][0], dmod_lat[0][1], d_gmix0 = acc[0, 0], acc[0, 1], acc[0, 2] + acc_c[1, 2]
    dmod_ctx[0][0], dmod_ctx[0][1] = acc_c[1, 0], acc_c[1, 1]
    grad_x = dX0[None]

    wmv = {"ret_in": (ret_w_in, m_ret_w_in, v_ret_w_in, 0, "ret_w_in"),
           "ret_out": (ret_w_out, m_ret_w_out, v_ret_w_out, 0, "ret_w_out"),
           "attn_in": (attn_w_in, m_attn_w_in, v_attn_w_in, 0, "attn_w_in"),
           "attn_out": (attn_w_out, m_attn_w_out, v_attn_w_out, 0, "attn_w_out"),
           "w1_0": (mlp_w1, m_mlp_w1, v_mlp_w1, 0, "mlp_w1"), "w1_1": (mlp_w1, m_mlp_w1, v_mlp_w1, 1, "mlp_w1"),
           "w2_0": (mlp_w2, m_mlp_w2, v_mlp_w2, 0, "mlp_w2"), "w2_1": (mlp_w2, m_mlp_w2, v_mlp_w2, 1, "mlp_w2")}
    big = {}

    def finish_grad(entry, after):
        key, axis, ssem, rsem, dw_thru, land = entry
        dw_done, land_done = _scatter_wait(ssem, rsem, dw_thru, land, axis, after, name=f"rs_wait_{key}")
        w_, m_, v_, layer, out_name = wmv[key]
        big[out_name] = _adamw_sharded(w_, m_, v_, layer, dw_done, land_done, axis, me_arr, big.get(out_name),
                                       name=f"adamw_{key}")
        return big[out_name][0]

    after = dX0
    for entry in pending[:-1]:
        after = finish_grad(entry, after)

    misc = jnp.zeros((D,), F32)
    misc = misc.at[0:RH].set(dlg_f[:, 0, 0]).at[RH:2 * RH].set(dlg_b[:, 0, 0])
    misc = misc.at[2 * RH:2 * RH + Hq].set(dsink_acc[:, :G, 0].reshape(Hq)).at[2 * RH + Hq].set(loss_part)
    rows = ([dmod_lat[i][k] for i in range(2) for k in range(6)] + [dmod_ctx[i][k] for i in range(2) for k in range(6)]
            + [d_gmix0, d_gmix1, d_gmlp0, d_gmlp1, d_gf, misc, zeros_d, zeros_d])
    part = jnp.stack(rows, 0)
    part_all = _all_gather_small(part, name="ag_small_grads")
    tot = _sum_devices(part_all, name="sum_small_grads")

    grad_ada_b = (tot[0:12] + tot[12:24]).reshape(2, 6 * D)
    grad_norm_mix_g, grad_norm_mlp_g, grad_final_norm_g = tot[24:26], tot[26:28], tot[28]
    grad_ret_decay_fwd = (tot[29, 0:RH] * jax.nn.sigmoid(-ret_decay_fwd[0]))[None]
    grad_ret_decay_bwd = (tot[29, RH:2 * RH] * jax.nn.sigmoid(-ret_decay_bwd[0]))[None]
    grad_attn_sink = tot[29, 2 * RH:2 * RH + Hq][None]
    loss = tot[29, 2 * RH + Hq]

    dlat_cols = lax.dynamic_slice_in_dim(part_all[:, 0:12].reshape(N_DEV, 2, 6 * D), me * acols, acols, axis=2)
    dctx_cols = lax.dynamic_slice_in_dim(tot[12:24].reshape(2, 6 * D), me * acols, acols, axis=1)
    dmod16 = jnp.concatenate([dlat_cols.transpose(1, 0, 2), dctx_cols[:, None, :], jnp.zeros((2, 7, acols), F32)], 1)
    cond_t = _silu(c16).T
    g_ada, d_ada, nm_ada, nv_ada, dcond_part = _ada_bwd(cond_t, dmod16, ada_w, m_ada_w, v_ada_w, name="ada_bwd")
    dcond = (dcond_part[0, :, 0] + dcond_part[1, :, 0]).reshape(D // LANES, LANES)
    pad_rows = -(D // LANES) % 8
    dcond_pad = jnp.concatenate([dcond, jnp.zeros((pad_rows, LANES), F32)], 0) if pad_rows else dcond
    dcond_all = _all_gather_small(dcond_pad, name="ag_dcond")
    dcond_tot = _sum_devices(dcond_all, name="sum_dcond")[:D // LANES].reshape(D)
    sg = jax.nn.sigmoid(c_ctx)
    grad_c_ctx = dcond_tot * (sg * (1.0 + c_ctx * (1.0 - sg)))

    small_w = [c_ctx, ada_b, norm_mix_g, norm_mlp_g, ret_decay_fwd, ret_decay_bwd, attn_sink, final_norm_g]
    small_g = [grad_c_ctx, grad_ada_b, grad_norm_mix_g, grad_norm_mlp_g, grad_ret_decay_fwd, grad_ret_decay_bwd,
               grad_attn_sink, grad_final_norm_g]
    small_m = [m_c_ctx, m_ada_b, m_norm_mix_g, m_norm_mlp_g, m_ret_decay_fwd, m_ret_decay_bwd, m_attn_sink,
               m_final_norm_g]
    small_v = [v_c_ctx, v_ada_b, v_norm_mix_g, v_norm_mlp_g, v_ret_decay_fwd, v_ret_decay_bwd, v_attn_sink,
               v_final_norm_g]
    sizes = [w_.size for w_ in small_w]
    total = sum(-(-s // LANES) * LANES for s in sizes)
    total_pad = -(-total // (8 * LANES)) * 8 * LANES

    def flat_pack(ts, fill):
        pieces = []
        for t_ in ts:
            f = t_.reshape(-1).astype(F32)
            pad = -f.size % LANES
            pieces.append(jnp.concatenate([f, jnp.full((pad,), fill, F32)]) if pad else f)
        pieces.append(jnp.full((total_pad - total,), fill, F32))
        return jnp.concatenate(pieces).reshape(total_pad // LANES, LANES)

    d_s, nm_s, nv_s = _adamw_flat(flat_pack(small_w, 0.0), flat_pack(small_g, 0.0), flat_pack(small_m, 0.0),
                                  flat_pack(small_v, 1.0), name="adamw_small")

    def unpack(p):
        flat = p.reshape(-1)
        res, off = [], 0
        for w_, s in zip(small_w, sizes):
            res.append(flat[off:off + s].reshape(w_.shape))
            off += -(-s // LANES) * LANES
        return res

    finish_grad(pending[-1], d_s)
    d_small, nm_small, nv_small = unpack(d_s), unpack(nm_s), unpack(nv_s)
    small_names = ["c_ctx", "ada_b", "norm_mix_g", "norm_mlp_g", "ret_decay_fwd", "ret_decay_bwd", "attn_sink",
                   "final_norm_g"]
    sm = {n: (g_, d_, m_, v_) for n, g_, d_, m_, v_ in zip(small_names, small_g, d_small, nm_small, nv_small)}

    def out4(n):
        if n == "ada_w":
            return g_ada, d_ada, nm_ada, nv_ada
        if n in big:
            return tuple(big[n])
        return sm[n]

    order = ["c_ctx", "ada_w", "ada_b", "norm_mix_g", "norm_mlp_g", "mlp_w1", "mlp_w2", "ret_w_in", "ret_w_out",
             "ret_decay_fwd", "ret_decay_bwd", "attn_w_in", "attn_w_out", "attn_sink", "final_norm_g"]
    quads = [out4(n) for n in order]
    return (loss, grad_x, *[q_[0] for q_ in quads], *[q_[1] for q_ in quads], *[q_[2] for q_ in quads],
            *[q_[3] for q_ in quads])
```

```python
import jax
import jax.numpy as jnp
from jax import lax
from jax.experimental import pallas as pl
from jax.experimental.pallas import tpu as pltpu

F32 = jnp.float32
BF16 = jnp.bfloat16

N_DEV = 8
NORM_EPS = 1e-6
CHUNK = 128
ATT_HEAD_DIM = 128
GRID_W = 64
ROPE_BASE = 10000.0
NEG_INF = -1e30
ADAM_LR, ADAM_B1, ADAM_B2, ADAM_EPS, ADAM_WD, ADAM_STEP = 0.001, 0.9, 0.999, 1e-08, 0.01, 10

V7X_VMEM_LIMIT_BYTES = 56 * 1024 * 1024
MM_BN = 1024
LANES = 128
MESH = pl.DeviceIdType.MESH

_NN = (((1,), (0,)), ((), ()))
_NT = (((1,), (1,)), ((), ()))
_TN = (((0,), (0,)), ((), ()))


def _dot(a, b, dn=_NN):
    return lax.dot_general(a, b, dn, preferred_element_type=F32)


def _cparams(*sem):
    return pltpu.CompilerParams(dimension_semantics=sem, vmem_limit_bytes=V7X_VMEM_LIMIT_BYTES)


def _pick(n, pref, mult=LANES):
    if n <= pref:
        return n
    best = None
    for d in range(mult, pref + 1, mult):
        if n % d == 0:
            best = d
    assert best is not None, (n, pref)
    return best


def _silu(x):
    return x * jax.nn.sigmoid(x)


def _mm(a, b, mode, out_dtypes, *, M, N, K, bm, bn, bk, name, b_col0=0, epilogue=None, extras=(), dep=None,
        cols_outer=False, b_buffers=None):
    assert M % bm == 0 and N % bn == 0 and K % bk == 0 and b_col0 % bn == 0, (name, M, N, K, bm, bn, bk, b_col0)
    nk = K // bk
    c0 = b_col0 // bn
    ax_i, ax_j = (1, 0) if cols_outer else (0, 1)

    def spec(block, f, **kw):
        if cols_outer:
            return pl.BlockSpec(block, lambda j, i, k: f(i, j, k), **kw)
        return pl.BlockSpec(block, f, **kw)

    b_kw = {} if b_buffers is None else dict(pipeline_mode=pl.Buffered(b_buffers))
    if mode == "nn":
        a_spec = spec((bm, bk), lambda i, j, k: (i, k))
        b_spec = spec((bk, bn), lambda i, j, k: (k, j + c0), **b_kw)
    elif mode == "nt":
        a_spec = spec((bm, bk), lambda i, j, k: (i, k))
        b_spec = spec((bn, bk), lambda i, j, k: (j + c0, k), **b_kw)
    else:
        a_spec = spec((bk, bm), lambda i, j, k: (k, i))
        b_spec = spec((bk, bn), lambda i, j, k: (k, j + c0), **b_kw)
    dn = {"nn": _NN, "nt": _NT, "tn": _TN}[mode]
    e_specs = [spec(bs, (lambda i, j, k, f=f: f(i, j))) for (_, bs, f) in extras]
    ne, no = len(extras), len(out_dtypes)
    nd = 0 if dep is None else 1

    def body(a_ref, b_ref, *rest):
        e_refs, o_refs = rest[:ne], rest[ne + nd:ne + nd + no]
        i, j, k = pl.program_id(ax_i), pl.program_id(ax_j), pl.program_id(2)

        def finish(acc):
            outs = (acc,) if epilogue is None else epilogue(acc, i, j, *[e[...] for e in e_refs])
            for o_ref, o in zip(o_refs, outs):
                o_ref[...] = o.astype(o_ref.dtype)

        p = _dot(a_ref[...], b_ref[...], dn)
        if nk == 1:
            finish(p)
        else:
            acc_ref = rest[-1]

            @pl.when(k == 0)
            def _():
                acc_ref[...] = p

            @pl.when(k > 0)
            def _():
                acc_ref[...] += p

            @pl.when(k == nk - 1)
            def _():
                finish(acc_ref[...])

    outs = pl.pallas_call(
        body, name=name, grid=(N // bn, M // bm, nk) if cols_outer else (M // bm, N // bn, nk),
        in_specs=[a_spec, b_spec] + e_specs + [pl.BlockSpec(memory_space=pl.ANY)] * nd,
        out_specs=[spec((bm, bn), lambda i, j, k: (i, j)) for _ in out_dtypes],
        out_shape=[jax.ShapeDtypeStruct((M, N), dt) for dt in out_dtypes],
        scratch_shapes=[pltpu.VMEM((bm, bn), F32)] if nk > 1 else [],
        compiler_params=_cparams("parallel", "parallel", "arbitrary"),
    )(a, b, *[e[0] for e in extras], *([dep] if nd else []))
    return outs


def _rowwise(body, rows, vecs, outs, n_acc, *, R, L, tr, name, acc_width=None, dep=None, row0=0):
    assert row0 % tr == 0
    b0 = row0 // tr
    assert R % tr == 0 and L % tr == 0, (name, R, L, tr)
    nl = L // tr
    n_regions = 2 if R > L else 1
    n_rows, n_vecs, n_outs = len(rows), len(vecs), len(outs)
    deps = () if dep is None else (tuple(dep) if isinstance(dep, (tuple, list)) else (dep,))
    n_dep = len(deps)
    acc_pad = -(-n_acc // 8) * 8 if n_acc else 0

    in_specs = []
    for (_, w, cb, lat_only) in rows:
        if lat_only:
            in_specs.append(pl.BlockSpec((tr, w), lambda i, cb=cb: (jnp.minimum(i, nl - 1), cb)))
        else:
            in_specs.append(pl.BlockSpec((tr, w), lambda i, cb=cb: (i + b0, cb)))
    for v in vecs:
        in_specs.append(pl.BlockSpec(v.shape, lambda i, nd=v.ndim: (0,) * nd))
    in_specs += [pl.BlockSpec(memory_space=pl.ANY)] * n_dep
    out_specs = [pl.BlockSpec((tr, w), lambda i: (i, 0)) for (w, _) in outs]
    out_shape = [jax.ShapeDtypeStruct((R, w), dt) for (w, dt) in outs]
    if n_acc:
        out_specs.append(pl.BlockSpec((None, acc_pad, acc_width), lambda i: (jnp.where(i >= nl, 1, 0), 0, 0)))
        out_shape.append(jax.ShapeDtypeStruct((n_regions, acc_pad, acc_width), F32))

    def kern(*refs):
        i = pl.program_id(0)
        is_ctx = i >= nl
        ins = [r[...] for r in refs[:n_rows + n_vecs]]
        o_refs = refs[n_rows + n_vecs + n_dep:]
        out_tiles, acc_rows = body(is_ctx, *ins)
        for o_ref, o in zip(o_refs[:n_outs], out_tiles):
            o_ref[...] = o.astype(o_ref.dtype)
        if n_acc:
            acc_ref = o_refs[n_outs]

            @pl.when((i == 0) | (i == nl))
            def _():
                acc_ref[...] = jnp.zeros_like(acc_ref)

            for r, row in enumerate(acc_rows):
                acc_ref[r:r + 1, :] += row

    res = pl.pallas_call(
        kern, name=name, grid=(R // tr,), in_specs=in_specs, out_specs=out_specs, out_shape=out_shape,
        compiler_params=_cparams("arbitrary"),
    )(*[r[0] for r in rows], *vecs, *deps)
    return res


def _colsum(x):
    return jnp.sum(x, axis=0, keepdims=True)


def _rms_stats(x):
    r = lax.rsqrt(jnp.mean(x * x, axis=-1, keepdims=True) + NORM_EPS)
    return x * r, r


def _sel(is_ctx, pk, lat_row, ctx_row):
    return jnp.where(is_ctx, pk[ctx_row:ctx_row + 1, :], pk[lat_row:lat_row + 1, :])


def _normmod(x, g, pk, *, R, L, tr, name, dep=None):
    D = x.shape[-1]

    def body(is_ctx, xt, gv, pkv):
        xh, _ = _rms_stats(xt)
        sh, sc = _sel(is_ctx, pkv, 0, 2), _sel(is_ctx, pkv, 1, 3)
        return ((xh * gv) * (1.0 + sc) + sh,), ()

    return _rowwise(body, [(x, D, 0, False)], [g, pk], [(D, BF16)], 0, R=R, L=L, tr=tr, name=name, dep=dep)[0]


def _normmod_bwd(x_in, da, dx_out, dx_out_lat_only, g, pk, prev, *, R, L, tr, name, row0=0):
    D = x_in.shape[-1]
    has_prev = prev is not None

    def body(is_ctx, *t):
        if has_prev:
            xt, dat, dxo, mp, gv, pkv, gates = t
        else:
            xt, dat, dxo, gv, pkv = t
        xh, r = _rms_stats(xt)
        dat = dat.astype(F32)
        sc = _sel(is_ctx, pkv, 1, 3)
        if dx_out_lat_only:
            dxo = jnp.where(is_ctx, 0.0, dxo)
        dn = dat * (1.0 + sc)
        w = dn * gv
        dxi = dxo + r * (w - xh * jnp.mean(w * xh, axis=-1, keepdims=True))
        accs = [_colsum(dat), _colsum(dat * (xh * gv)), _colsum(dn * xh)]
        outs = [dxi]
        if has_prev:
            gate = _sel(is_ctx, gates, 0, 1)
            outs.append(dxi * gate)
            accs.append(_colsum(dxi * mp.astype(F32)))
        return outs, accs

    rows = [(x_in, D, 0, False), (da, D, 0, False), (dx_out, D, 0, dx_out_lat_only)]
    vecs = [g, pk]
    outs = [(D, F32)]
    if has_prev:
        rows.append((prev[0], D, 0, False))
        vecs.append(prev[1])
        outs.append((D, BF16))
    return _rowwise(body, rows, vecs, outs, 4 if has_prev else 3, R=R, L=L, tr=tr, name=name, acc_width=D, row0=row0)


def _loss_head(x4, target, m_prev, gf, gate, *, L, tr, name):
    D = x4.shape[-1]

    def body(is_ctx, xt, tg, mp, gfv, gatev):
        xh, r = _rms_stats(xt)
        e = xh * gfv - tg
        dy = e * (1.0 / D)
        w = dy * gfv
        dx = r * (w - xh * jnp.mean(w * xh, axis=-1, keepdims=True))
        accs = [_colsum(e * e) * (0.5 / D), _colsum(dy * xh), _colsum(dx * mp.astype(F32))]
        return (dx, dx * gatev), accs

    return _rowwise(body, [(x4, D, 0, False), (target, D, 0, False), (m_prev, D, 0, False)], [gf, gate],
                    [(D, F32), (D, BF16)], 3, R=L, L=L, tr=tr, name=name, acc_width=D)


RET_CHUNK = 2 * LANES
RET_HEADS_PER_STEP = 4


def _decays(lgh, rev):
    C = RET_CHUNK
    ii = lax.broadcasted_iota(jnp.int32, (C, C), 0)
    jj = lax.broadcasted_iota(jnp.int32, (C, C), 1)
    ri = lax.broadcasted_iota(jnp.int32, (C, 1), 0).astype(F32)
    diff = (jj - ii if rev else ii - jj)
    amat = jnp.where(diff >= 0, jnp.exp(lgh * jnp.maximum(diff, 0).astype(F32)), 0.0)
    pos = (C - ri) if rev else (ri + 1.0)
    bq = jnp.exp(lgh * pos)
    bk = jnp.exp(lgh * (C - pos))
    return amat, bq, bk, pos


def _ret_geometry(T, L, H, rev, backward):
    C = RET_CHUNK
    assert T % C == 0 and L % C == 0, (T, L)
    nT, nL = T // C, L // C
    hb = RET_HEADS_PER_STEP if H % RET_HEADS_PER_STEP == 0 else 1

    def step(s):
        return (nT - 1 - s) if backward else s

    def chunk(s):
        s = step(s)
        return (nT - 1 - s) if rev else (s + nL) % nT

    return C, nT, hb, chunk, step


def _ret_fwd(qk, vg, lg, other, *, T, L, H, rev, name, readout=False):
    dk, dv = 2 * LANES, 4 * LANES
    C, nT, hb, chunk, step = _ret_geometry(T, L, H, rev, False)
    n_other = 0 if other is None else 1
    n_read = 1 if readout else 0

    def body(lg_ref, q_ref, k_ref, v_ref, *rest):
        o_ref, st_ref, s_scr = rest[n_other + n_read], rest[-2], rest[-1]
        hg, s = pl.program_id(0), pl.program_id(1)

        @pl.when(s == 0)
        def _():
            s_scr[...] = jnp.zeros_like(s_scr)

        for hh in range(hb):
            lgh = lg_ref[0, hg * hb + hh]
            amat, bq, bk, _ = _decays(lgh, rev)
            q, k = q_ref[:, hh * dk:(hh + 1) * dk], k_ref[:, hh * dk:(hh + 1) * dk]
            v = v_ref[:, hh * dv:(hh + 1) * dv]
            stb = s_scr[hh].astype(BF16)
            st_ref[hh] = stb
            scores = _dot(q, k, _NT) * amat
            o = _dot(scores.astype(BF16), v) + _dot(q, stb) * bq
            if n_other:
                o = rest[0][:, hh * dv:(hh + 1) * dv] + o
            o_ref[:, hh * dv:(hh + 1) * dv] = o
            if n_read:
                g = rest[n_other][:, hh * dv:(hh + 1) * dv].astype(F32)
                y = o * lax.rsqrt(jnp.mean(o * o, axis=-1, keepdims=True) + NORM_EPS)
                rest[n_other + n_read + 1][:, hh * dv:(hh + 1) * dv] = (_silu(g) * y).astype(BF16)
            kd = (k.astype(F32) * bk).astype(BF16)
            s_scr[hh] = s_scr[hh] * jnp.exp(lgh * C) + _dot(kd, v, _TN)

    vspec = pl.BlockSpec((C, hb * dv), lambda h, s: (chunk(s), h))
    gspec = pl.BlockSpec((C, hb * dv), lambda h, s: (chunk(s), H // hb + h))
    return pl.pallas_call(
        body, name=name, grid=(H // hb, nT),
        in_specs=[pl.BlockSpec(memory_space=pltpu.SMEM),
                  pl.BlockSpec((C, hb * dk), lambda h, s: (chunk(s), h)),
                  pl.BlockSpec((C, hb * dk), lambda h, s: (chunk(s), H // hb + h)), vspec]
        + [vspec] * n_other + [gspec] * n_read,
        out_specs=[vspec] + [vspec] * n_read + [pl.BlockSpec((hb, None, dk, dv), lambda h, s: (h, s, 0, 0))],
        out_shape=[jax.ShapeDtypeStruct((T, H * dv), F32)] + [jax.ShapeDtypeStruct((T, H * dv), BF16)] * n_read
        + [jax.ShapeDtypeStruct((H, nT, dk, dv), BF16)],
        scratch_shapes=[pltpu.VMEM((hb, dk, dv), F32)],
        compiler_params=_cparams("parallel", "arbitrary"),
    )(lg, qk, qk, vg, *([other] if n_other else []), *([vg] if n_read else []))


def _ret_bwd(qk, vg, do, states, lgs, *, T, L, H, name):
    dk, dv = 2 * LANES, 4 * LANES
    revs = (False, True)
    geo = [_ret_geometry(T, L, H, rev, True) for rev in revs]
    C, nT, hb = geo[0][:3]
    N_IN, N_OUT = 6, 4

    def body(*refs):
        ins = [refs[d * N_IN:(d + 1) * N_IN] for d in range(2)]
        outs = [refs[2 * N_IN + d * N_OUT:2 * N_IN + (d + 1) * N_OUT] for d in range(2)]
        ds_scr = refs[-1]
        hg, s = pl.program_id(0), pl.program_id(1)

        @pl.when(s == 0)
        def _():
            ds_scr[...] = jnp.zeros_like(ds_scr)
            for d in range(2):
                outs[d][3][...] = jnp.zeros_like(outs[d][3])

        for hh in range(hb):
            for d, rev in enumerate(revs):
                lg_ref, q_ref, k_ref, v_ref, do_ref, st_ref = ins[d]
                dq_ref, dk_ref, dv_ref, dlg_ref = outs[d]
                lgh = lg_ref[0, hg * hb + hh]
                amat, bq, bk, pos = _decays(lgh, rev)
                ksl, vsl = slice(hh * dk, (hh + 1) * dk), slice(hh * dv, (hh + 1) * dv)
                q, k, v, dob = q_ref[:, ksl], k_ref[:, ksl], v_ref[:, vsl], do_ref[:, vsl]
                stb = st_ref[hh]
                ds_new = ds_scr[d, hh]
                dsb = ds_new.astype(BF16)
                qf, kf = q.astype(F32), k.astype(F32)
                scores = (_dot(q, k, _NT) * amat).astype(BF16)
                dqk = (_dot(dob, v, _NT) * amat).astype(BF16)
                dq = _dot(dqk, k) + _dot(dob, stb, _NT) * bq
                dkk = _dot(dqk, q, _TN) + _dot(v, dsb, _NT) * bk
                kd = (kf * bk).astype(BF16)
                dvv = _dot(scores, dob, _TN) + _dot(kd, dsb)
                dod = (dob.astype(F32) * bq).astype(BF16)
                ds_prev = ds_new * jnp.exp(lgh * C) + _dot(q, dod, _TN)
                ds_scr[d, hh] = ds_prev
                dq_ref[:, ksl] = dq.astype(dq_ref.dtype)
                dk_ref[:, ksl] = dkk.astype(dk_ref.dtype)
                dv_ref[:, vsl] = dvv.astype(dv_ref.dtype)
                dlg = (jnp.sum(pos * jnp.sum(qf * dq - kf * dkk, axis=-1, keepdims=True))
                       + C * jnp.sum(ds_prev * stb.astype(F32)))
                dlg_ref[hh] += dlg

    in_specs, out_specs, operands = [], [], []
    for d in range(2):
        chunk, step = geo[d][3], geo[d][4]
        qspec = pl.BlockSpec((C, hb * dk), lambda h, s, chunk=chunk: (chunk(s), h))
        vspec = pl.BlockSpec((C, hb * dv), lambda h, s, chunk=chunk: (chunk(s), h))
        in_specs += [pl.BlockSpec(memory_space=pltpu.SMEM), qspec,
                     pl.BlockSpec((C, hb * dk), lambda h, s, chunk=chunk: (chunk(s), H // hb + h)), vspec, vspec,
                     pl.BlockSpec((hb, None, dk, dv), lambda h, s, step=step: (h, step(s), 0, 0))]
        out_specs += [qspec, qspec, vspec, pl.BlockSpec((hb, 8, LANES), lambda h, s: (h, 0, 0))]
        operands += [lgs[d], qk, qk, vg, do, states[d]]
    one_dir = [jax.ShapeDtypeStruct((T, H * dk), BF16), jax.ShapeDtypeStruct((T, H * dk), BF16),
               jax.ShapeDtypeStruct((T, H * dv), BF16), jax.ShapeDtypeStruct((H, 8, LANES), F32)]
    res = pl.pallas_call(
        body, name=name, grid=(H // hb, nT), in_specs=in_specs, out_specs=out_specs, out_shape=one_dir * 2,
        scratch_shapes=[pltpu.VMEM((2, hb, dk, dv), F32)],
        compiler_params=_cparams("parallel", "arbitrary"),
    )(*operands)
    return res[:N_OUT], res[N_OUT:]


def _readout_bwd(dz, o, vg, *, T, L, H, tr, name):
    dv = 4 * LANES
    W = H * dv

    def body(is_ctx, dzt, o, g):
        gf = g.astype(F32)
        sg = jax.nn.sigmoid(gf)
        dzf = dzt.astype(F32)
        dy = dzf * (gf * sg)
        ys, dos = [], []
        for h in range(H):
            sl = slice(h * dv, (h + 1) * dv)
            oh, dyh = o[:, sl], dy[:, sl]
            r = lax.rsqrt(jnp.mean(oh * oh, axis=-1, keepdims=True) + NORM_EPS)
            yh = oh * r
            ys.append(yh)
            dos.append(r * (dyh - yh * jnp.mean(dyh * yh, axis=-1, keepdims=True)))
        y = jnp.concatenate(ys, axis=1)
        dg = dzf * y * (sg * (1.0 + gf * (1.0 - sg)))
        return (jnp.concatenate(dos, axis=1), dg), ()

    return _rowwise(body, [(dz, W, 0, False), (o, W, 0, False), (vg, W, 1, False)], [],
                    [(W, BF16), (W, BF16)], 0, R=T, L=L, tr=tr, name=name)


def _ret_assemble(dq_f, dq_b, dk_f, dk_b, dv_f, dv_b, dg, cos, sin, *, T, L, H, tr, name):
    dk, dv = 2 * LANES, 4 * LANES
    Wq, Wv = H * dk, H * dv
    kscale = float(dk) ** -0.5

    def unrope(d, c, s_, scale):
        parts = []
        for h in range(H):
            d1, d2 = d[:, h * dk:h * dk + LANES], d[:, h * dk + LANES:(h + 1) * dk]
            parts += [(d1 * c + d2 * s_) * scale, (d2 * c - d1 * s_) * scale]
        return jnp.concatenate(parts, axis=1)

    def body(is_ctx, qf, qb, kf, kb, vf, vb, g, c, s_):
        add = lambda a, b: a.astype(F32) + b.astype(F32)
        dq = unrope(add(qf, qb), c, s_, 1.0)
        dkk = unrope(add(kf, kb), c, s_, kscale)
        return (jnp.concatenate([dq.astype(BF16), dkk.astype(BF16), add(vf, vb).astype(BF16), g], axis=1),), ()

    rows = [(dq_f, Wq, 0, False), (dq_b, Wq, 0, False), (dk_f, Wq, 0, False), (dk_b, Wq, 0, False),
            (dv_f, Wv, 0, False), (dv_b, Wv, 0, False), (dg, Wv, 0, False),
            (cos, LANES, 0, False), (sin, LANES, 0, False)]
    return _rowwise(body, rows, [], [(2 * Wq + 2 * Wv, BF16)], 0, R=T, L=L, tr=tr, name=name)[0]


def _swap32(x):
    n = x.shape[-1]
    lane = lax.broadcasted_iota(jnp.int32, x.shape, x.ndim - 1)
    return jnp.where(lane % 64 < 32, pltpu.roll(x, n - 32, x.ndim - 1), pltpu.roll(x, 32, x.ndim - 1))


ATT_Q_BLOCKS = 4


def _stack_heads_at(ref, rows, G):
    return jnp.concatenate([ref[rows, g * LANES:(g + 1) * LANES] for g in range(G)], axis=0)


def _stack_columns_at(ref, rows, G):
    return jnp.concatenate([ref[rows, g:g + 1] for g in range(G)], axis=0)


def _sink_column(sink_ref, h, G):
    return jnp.concatenate([jnp.full((CHUNK, 1), sink_ref[0, h * G + g], F32) for g in range(G)], axis=0)


def _key_mask(n, nb, CTX, G):
    W = 3 * CHUNK + CTX
    ii = lax.broadcasted_iota(jnp.int32, (G * CHUNK, W), 0) & (CHUNK - 1)
    col = lax.broadcasted_iota(jnp.int32, (G * CHUNK, W), 1)
    is_prev = col < CHUNK
    is_next = (col >= 2 * CHUNK) & (col < 3 * CHUNK)
    prev_ok = is_prev & (col >= ii) & (n > 0)
    next_ok = is_next & ((col - 2 * CHUNK) <= ii) & (n < nb - 1)
    return prev_ok | next_ok | jnp.logical_not(is_prev | is_next)


def _attn_geometry(L, CTX):
    nb = L // CHUNK
    QB = ATT_Q_BLOCKS if nb % ATT_Q_BLOCKS == 0 else 1

    def blk(j):
        return pl.BlockSpec((CHUNK, LANES), lambda h, m: (jnp.clip(m * QB + j - 1, 0, nb - 1), h))

    kvs = [blk(j) for j in range(QB + 2)] + [pl.BlockSpec((CTX, LANES), lambda h, m: (L // CTX, h))]
    return nb, QB, kvs


def _attn_fwd(q, k, v, sink, *, L, CTX, Hkv, G, name):
    scale = float(ATT_HEAD_DIM) ** -0.5
    nb, QB, kvs = _attn_geometry(L, CTX)
    nkv = QB + 3

    def body(sink_ref, q_ref, *rest):
        kb, vb, (o_ref, lse_ref) = rest[:nkv], rest[nkv:2 * nkv], rest[2 * nkv:]
        h, m_ = pl.program_id(0), pl.program_id(1)
        sk = _sink_column(sink_ref, h, G)
        for sub in range(QB):
            rows = slice(sub * CHUNK, (sub + 1) * CHUNK)
            qs = _stack_heads_at(q_ref, rows, G)
            kall = jnp.concatenate([kb[sub + j][...] for j in range(3)] + [kb[-1][...]], axis=0)
            vall = jnp.concatenate([vb[sub + j][...] for j in range(3)] + [vb[-1][...]], axis=0)
            s_ = jnp.where(_key_mask(m_ * QB + sub, nb, CTX, G), _dot(qs, kall, _NT) * scale, NEG_INF)
            m = jnp.maximum(jnp.max(s_, axis=-1, keepdims=True), sk)
            p = jnp.exp(s_ - m)
            den = jnp.sum(p, axis=-1, keepdims=True) + jnp.exp(sk - m)
            o = _dot(p.astype(BF16), vall) / den
            lse = m + jnp.log(den)
            for g in range(G):
                o_ref[rows, g * LANES:(g + 1) * LANES] = o[g * CHUNK:(g + 1) * CHUNK].astype(o_ref.dtype)
                lse_ref[rows, g:g + 1] = lse[g * CHUNK:(g + 1) * CHUNK]

    qspec = pl.BlockSpec((QB * CHUNK, G * LANES), lambda h, m: (m, h))
    return pl.pallas_call(
        body, name=name, grid=(Hkv, nb // QB),
        in_specs=[pl.BlockSpec(memory_space=pltpu.SMEM), qspec] + kvs + kvs,
        out_specs=[qspec, pl.BlockSpec((None, QB * CHUNK, G), lambda h, m: (h, m, 0))],
        out_shape=[jax.ShapeDtypeStruct((L, Hkv * G * LANES), BF16), jax.ShapeDtypeStruct((Hkv, L, G), F32)],
        compiler_params=_cparams("parallel", "parallel"),
    )(sink, q, *([k] * nkv), *([v] * nkv))


def _attn_bwd(q, k, v, do, lse, sink, *, L, CTX, Hkv, G, name):
    scale = float(ATT_HEAD_DIM) ** -0.5
    nb, QB, kvs = _attn_geometry(L, CTX)
    nkv = QB + 3
    qspec = pl.BlockSpec((QB * CHUNK, G * LANES), lambda h, m: (m, h))
    rowspec = pl.BlockSpec((None, QB * CHUNK, G), lambda h, m: (h, m, 0))
    colspec = lambda rows: pl.BlockSpec((rows, LANES), lambda h, m: (0, h))

    def body(sink_ref, q_ref, do_ref, lse_ref, *rest):
        kb, vb = rest[:nkv], rest[nkv:2 * nkv]
        dq_ref, dk_ref, dv_ref, dkx_ref, dvx_ref, dsk_ref = rest[2 * nkv:]
        h, m_ = pl.program_id(0), pl.program_id(1)

        @pl.when(m_ == 0)
        def _():
            for r in (dk_ref, dv_ref, dkx_ref, dvx_ref, dsk_ref):
                r[...] = jnp.zeros_like(r)

        sk = _sink_column(sink_ref, h, G)
        for sub in range(QB):
            n = m_ * QB + sub
            rows = slice(sub * CHUNK, (sub + 1) * CHUNK)
            qs, dos = _stack_heads_at(q_ref, rows, G), _stack_heads_at(do_ref, rows, G)
            kall = jnp.concatenate([kb[sub + j][...] for j in range(3)] + [kb[-1][...]], axis=0)
            vall = jnp.concatenate([vb[sub + j][...] for j in range(3)] + [vb[-1][...]], axis=0)
            lse_c = _stack_columns_at(lse_ref, rows, G)
            p = jnp.where(_key_mask(n, nb, CTX, G), jnp.exp(_dot(qs, kall, _NT) * scale - lse_c), 0.0)
            dp = _dot(dos, vall, _NT)
            delta = jnp.sum(p * dp, axis=-1, keepdims=True)
            ds = (p * (dp - delta) * scale).astype(BF16)
            dq = _dot(ds, kall)
            dk_all = _dot(ds, qs, _TN)
            dv_all = _dot(p.astype(BF16), dos, _TN)
            for g in range(G):
                dq_ref[rows, g * LANES:(g + 1) * LANES] = dq[g * CHUNK:(g + 1) * CHUNK]
            for part, blk in enumerate((jnp.maximum(n - 1, 0), n, jnp.minimum(n + 1, nb - 1))):
                krows = pl.ds(pl.multiple_of(blk * CHUNK, CHUNK), CHUNK)
                dk_ref[krows, :] += dk_all[part * CHUNK:(part + 1) * CHUNK]
                dv_ref[krows, :] += dv_all[part * CHUNK:(part + 1) * CHUNK]
            dkx_ref[...] += dk_all[3 * CHUNK:]
            dvx_ref[...] += dv_all[3 * CHUNK:]
            dsink = -jnp.exp(sk - lse_c) * delta
            for g in range(G):
                dsk_ref[g:g + 1, :] += jnp.sum(dsink[g * CHUNK:(g + 1) * CHUNK])

    return pl.pallas_call(
        body, name=name, grid=(Hkv, nb // QB),
        in_specs=[pl.BlockSpec(memory_space=pltpu.SMEM), qspec, qspec, rowspec] + kvs + kvs,
        out_specs=[qspec, colspec(L), colspec(L), colspec(CTX), colspec(CTX),
                   pl.BlockSpec((None, 8, LANES), lambda h, m: (h, 0, 0))],
        out_shape=[jax.ShapeDtypeStruct((L, Hkv * G * LANES), F32),
                   jax.ShapeDtypeStruct((L, Hkv * LANES), F32), jax.ShapeDtypeStruct((L, Hkv * LANES), F32),
                   jax.ShapeDtypeStruct((CTX, Hkv * LANES), F32), jax.ShapeDtypeStruct((CTX, Hkv * LANES), F32),
                   jax.ShapeDtypeStruct((Hkv, 8, LANES), F32)],
        compiler_params=_cparams("parallel", "arbitrary"),
    )(sink, q, do, lse, *([k] * nkv), *([v] * nkv))


def _attn_assemble(dq, dk_lat, dv_lat, dk_ctx, dv_ctx, cos, sin, *, T, L, CTX, Hq, Hkv, tr, name):
    Wq, Wk = Hq * LANES, Hkv * LANES
    ctx_blocks = CTX // tr
    nl = L // tr

    def unrope(d, c, s_, heads):
        return d * jnp.tile(c, (1, heads)) + _swap32(d * jnp.tile(s_, (1, heads)))

    def body(is_ctx, dqt, dkl, dvl, dkc, dvc, c, s_):
        dq_ = jnp.where(is_ctx, 0.0, unrope(dqt, c, s_, Hq))
        dk_ = unrope(jnp.where(is_ctx, dkc, dkl), c, s_, Hkv)
        dv_ = jnp.where(is_ctx, dvc, dvl)
        return (jnp.concatenate([dq_, dk_, dv_], axis=1),), ()

    def ctx_map(i):
        return (jnp.clip(i - nl, 0, ctx_blocks - 1), 0)

    assert T % tr == 0 and L % tr == 0 and CTX % tr == 0
    in_specs = [pl.BlockSpec((tr, Wq), lambda i: (jnp.minimum(i, nl - 1), 0)),
                pl.BlockSpec((tr, Wk), lambda i: (jnp.minimum(i, nl - 1), 0)),
                pl.BlockSpec((tr, Wk), lambda i: (jnp.minimum(i, nl - 1), 0)),
                pl.BlockSpec((tr, Wk), ctx_map), pl.BlockSpec((tr, Wk), ctx_map),
                pl.BlockSpec((tr, LANES), lambda i: (i, 0)), pl.BlockSpec((tr, LANES), lambda i: (i, 0))]

    def kern(dq_r, dkl_r, dvl_r, dkc_r, dvc_r, c_r, s_r, o_ref):
        is_ctx = pl.program_id(0) >= nl
        (out,), _ = body(is_ctx, dq_r[...], dkl_r[...], dvl_r[...], dkc_r[...], dvc_r[...], c_r[...], s_r[...])
        o_ref[...] = out.astype(o_ref.dtype)

    return pl.pallas_call(
        kern, name=name, grid=(T // tr,), in_specs=in_specs,
        out_specs=pl.BlockSpec((tr, Wq + 2 * Wk), lambda i: (i, 0)),
        out_shape=jax.ShapeDtypeStruct((T, Wq + 2 * Wk), BF16),
        compiler_params=_cparams("parallel"),
    )(dq, dk_lat, dv_lat, dk_ctx, dv_ctx, cos, sin)


def _my_place():
    x, y, c = lax.axis_index("x"), lax.axis_index("y"), lax.axis_index("c")
    return x, y, c


def _all_gather_small(v, *, name):
    R, C = v.shape

    def body(x_ref, out_ref, send_sems, recv_sems, local_sem):
        x, y, c = _my_place()
        me, sibling = (x, y, c), (x, y, 1 - c)
        chips = [(1 - x, y), (x, 1 - y), (1 - x, 1 - y)]

        def slot(px, py, pc):
            return out_ref.at[4 * px + 2 * py + pc]

        def copy(k, block, to, src=None):
            return pltpu.make_async_remote_copy(
                src_ref=slot(*block) if src is None else src, dst_ref=slot(*block),
                send_sem=send_sems.at[k], recv_sem=recv_sems.at[k], device_id=to, device_id_type=MESH)

        mine = pltpu.make_async_copy(x_ref, slot(*me), local_sem)
        mine.start()
        first = [copy(0, me, sibling, src=x_ref)]
        first += [copy(1 + j, me, (*chip, c), src=x_ref) for j, chip in enumerate(chips)]
        for cp in first:
            cp.start()
        passed = [copy(4 + j, (*chip, c), sibling) for j, chip in enumerate(chips)]
        for j, chip in enumerate(chips):
            copy(1 + j, (*chip, c), me).wait_recv()
            passed[j].start()
        copy(0, sibling, me).wait_recv()
        for j, chip in enumerate(chips):
            copy(4 + j, (*chip, 1 - c), me).wait_recv()
        for cp in first + passed:
            cp.wait_send()
        mine.wait()

    return pl.pallas_call(
        body, name=name, out_shape=jax.ShapeDtypeStruct((N_DEV, R, C), v.dtype),
        in_specs=[pl.BlockSpec(memory_space=pltpu.VMEM)], out_specs=pl.BlockSpec(memory_space=pltpu.VMEM),
        scratch_shapes=[pltpu.SemaphoreType.DMA((7,)), pltpu.SemaphoreType.DMA((7,)), pltpu.SemaphoreType.DMA],
    )(v)


_HBM_SPEC = pl.BlockSpec(memory_space=pltpu.HBM)
_SEM_SPEC = pl.BlockSpec(memory_space=pltpu.SEMAPHORE)
_ANY_SPEC = pl.BlockSpec(memory_space=pl.ANY)
_DATAFLOW = pltpu.SideEffectType.DATAFLOW_SIDE_EFFECTING
N_PEERS = N_DEV - 1


def _peer(r):
    x, y, c = _my_place()
    return ((1 - x) if r & 4 else x, (1 - y) if r & 2 else y, (1 - c) if r & 1 else c)


def _index_of(place):
    return 4 * place[0] + 2 * place[1] + place[2]


def _slot(ref, axis, idx, size):
    if axis == 0:
        return ref.at[pl.ds(idx * size, size), :]
    return ref.at[:, pl.ds(idx * size, size)]


def _cast_place(w3, layer, axis, me_arr, dep, *, name):
    Ks, Ns = w3.shape[1], w3.shape[2]
    tr = _pick(Ks, 256, 16)
    nblk = Ks // tr
    full = (Ks * N_DEV, Ns) if axis == 0 else (Ks, Ns * N_DEV)
    if axis == 0:
        out_map = lambda i, me: (me[0] * nblk + i, 0)
    else:
        out_map = lambda i, me: (i, me[0])

    def body(me_ref, w_ref, dep_ref, o_ref):
        o_ref[...] = w_ref[...].astype(BF16)

    return pl.pallas_call(
        body, name=name, out_shape=jax.ShapeDtypeStruct(full, BF16),
        grid_spec=pltpu.PrefetchScalarGridSpec(
            num_scalar_prefetch=1, grid=(nblk,),
            in_specs=[pl.BlockSpec((None, tr, Ns), lambda i, me: (layer, i, 0)), pl.BlockSpec(memory_space=pl.ANY)],
            out_specs=pl.BlockSpec((tr, Ns), out_map)),
        compiler_params=_cparams("parallel"),
    )(me_arr, w3, dep)


AG_FIRST = 4
AG_CHIPS = 3


def _sibling():
    x, y, c = _my_place()
    return (x, y, 1 - c)


def _chip_peer(j, same_core=True):
    x, y, c = _my_place()
    px = (1 - x) if j in (0, 2) else x
    py = (1 - y) if j in (1, 2) else y
    return (px, py, c if same_core else 1 - c)


def _gather_start(lands, axes, after, *, name):
    nt = len(lands)
    sizes = [l.shape[a] // N_DEV for l, a in zip(lands, axes)]

    def body(*refs):
        ins, send_sems, recv_sems, token = refs[:nt], refs[nt + 1], refs[nt + 2], refs[-1]
        my_idx = _index_of(_my_place())
        for t in range(nt):
            mine = _slot(ins[t], axes[t], my_idx, sizes[t])
            for k, to in enumerate([_sibling()] + [_chip_peer(j) for j in range(AG_CHIPS)]):
                pltpu.make_async_remote_copy(src_ref=mine, dst_ref=mine, send_sem=send_sems.at[t * AG_FIRST + k],
                                             recv_sem=recv_sems.at[t * AG_FIRST + k], device_id=to,
                                             device_id_type=MESH).start()
        token[...] = jnp.zeros_like(token)

    res = pl.pallas_call(
        body, name=name,
        out_shape=(pltpu.SemaphoreType.DMA((nt * AG_FIRST,)), pltpu.SemaphoreType.DMA((nt * AG_FIRST,)),
                   *[pltpu.HBM(l.shape, l.dtype) for l in lands], jax.ShapeDtypeStruct((8, LANES), F32)),
        in_specs=[_HBM_SPEC] * nt + [_ANY_SPEC],
        out_specs=(_SEM_SPEC, _SEM_SPEC, *[_HBM_SPEC] * nt, pl.BlockSpec(memory_space=pltpu.VMEM)),
        input_output_aliases={t: 2 + t for t in range(nt)},
        compiler_params=pltpu.CompilerParams(has_side_effects=_DATAFLOW),
    )(*[pltpu.with_memory_space_constraint(l, pltpu.HBM) for l in lands], after)
    return res[0], res[1], list(res[2:2 + nt]), res[-1]


def _gather_forward(send_a, recv_a, lands, axes, after, *, name):
    nt = len(lands)
    sizes = [l.shape[a] // N_DEV for l, a in zip(lands, axes)]

    def body(*refs):
        ins, send_a, recv_a = refs[:nt], refs[nt], refs[nt + 1]
        send_f, recv_f, token = refs[nt + 3], refs[nt + 4], refs[-1]
        my_idx = _index_of(_my_place())
        for t in range(nt):
            for j in range(AG_CHIPS):
                src_dev = _chip_peer(j)
                arrived = _slot(ins[t], axes[t], _index_of(src_dev), sizes[t])
                pltpu.make_async_remote_copy(
                    src_ref=_slot(ins[t], axes[t], my_idx, sizes[t]), dst_ref=arrived,
                    send_sem=send_a.at[t * AG_FIRST + 1 + j], recv_sem=recv_a.at[t * AG_FIRST + 1 + j],
                    device_id=src_dev, device_id_type=MESH).wait_recv()
                pltpu.make_async_remote_copy(src_ref=arrived, dst_ref=arrived, send_sem=send_f.at[t * AG_CHIPS + j],
                                             recv_sem=recv_f.at[t * AG_CHIPS + j], device_id=_sibling(),
                                             device_id_type=MESH).start()
        token[...] = jnp.zeros_like(token)

    res = pl.pallas_call(
        body, name=name,
        out_shape=(pltpu.SemaphoreType.DMA((nt * AG_CHIPS,)), pltpu.SemaphoreType.DMA((nt * AG_CHIPS,)),
                   *[pltpu.HBM(l.shape, l.dtype) for l in lands], jax.ShapeDtypeStruct((8, LANES), F32)),
        in_specs=[_HBM_SPEC] * nt + [_SEM_SPEC, _SEM_SPEC, _ANY_SPEC],
        out_specs=(_SEM_SPEC, _SEM_SPEC, *[_HBM_SPEC] * nt, pl.BlockSpec(memory_space=pltpu.VMEM)),
        input_output_aliases={t: 2 + t for t in range(nt)},
        compiler_params=pltpu.CompilerParams(has_side_effects=_DATAFLOW),
    )(*lands, send_a, recv_a, after)
    return res[0], res[1], list(res[2:2 + nt]), res[-1]


def _gather_wait(send_a, recv_a, send_f, recv_f, lands, axes, after, *, name):
    nt = len(lands)
    sizes = [l.shape[a] // N_DEV for l, a in zip(lands, axes)]

    def body(*refs):
        ins, send_a, recv_a, send_f, recv_f = refs[:nt], refs[nt], refs[nt + 1], refs[nt + 2], refs[nt + 3]
        my_idx = _index_of(_my_place())
        sib = _sibling()
        for t in range(nt):
            mine = _slot(ins[t], axes[t], my_idx, sizes[t])
            for k, to in enumerate([sib] + [_chip_peer(j) for j in range(AG_CHIPS)]):
                pltpu.make_async_remote_copy(src_ref=mine, dst_ref=mine, send_sem=send_a.at[t * AG_FIRST + k],
                                             recv_sem=recv_a.at[t * AG_FIRST + k], device_id=to,
                                             device_id_type=MESH).wait_send()
            pltpu.make_async_remote_copy(src_ref=mine, dst_ref=_slot(ins[t], axes[t], _index_of(sib), sizes[t]),
                                         send_sem=send_a.at[t * AG_FIRST], recv_sem=recv_a.at[t * AG_FIRST],
                                         device_id=sib, device_id_type=MESH).wait_recv()
            for j in range(AG_CHIPS):
                sent = _slot(ins[t], axes[t], _index_of(_chip_peer(j)), sizes[t])
                got = _slot(ins[t], axes[t], _index_of(_chip_peer(j, same_core=False)), sizes[t])
                cp = pltpu.make_async_remote_copy(src_ref=sent, dst_ref=got, send_sem=send_f.at[t * AG_CHIPS + j],
                                                  recv_sem=recv_f.at[t * AG_CHIPS + j], device_id=sib,
                                                  device_id_type=MESH)
                cp.wait_send()
                cp.wait_recv()

    res = pl.pallas_call(
        body, name=name, out_shape=[pltpu.HBM(l.shape, l.dtype) for l in lands],
        in_specs=[_HBM_SPEC] * nt + [_SEM_SPEC] * 4 + [_ANY_SPEC], out_specs=[_HBM_SPEC] * nt,
        input_output_aliases={t: t for t in range(nt)},
        compiler_params=pltpu.CompilerParams(has_side_effects=_DATAFLOW),
    )(*lands, send_a, recv_a, send_f, recv_f, after)
    return list(res)


def _scatter_start(dw, axis, *, name):
    size = dw.shape[axis] // N_DEV
    land_shape = (N_PEERS, size, dw.shape[1]) if axis == 0 else (N_PEERS, dw.shape[0], size)

    def body(dw_ref, land_ref, send_sems, recv_sems, dw_thru, land_thru, token):
        for r in range(1, N_DEV):
            p = _peer(r)
            pltpu.make_async_remote_copy(src_ref=_slot(dw_ref, axis, _index_of(p), size), dst_ref=land_ref.at[r - 1],
                                         send_sem=send_sems.at[r - 1], recv_sem=recv_sems.at[r - 1], device_id=p,
                                         device_id_type=MESH).start()
        token[...] = jnp.zeros_like(token)

    land = pltpu.with_memory_space_constraint(lax.empty(land_shape, dw.dtype), pltpu.HBM)
    return pl.pallas_call(
        body, name=name,
        out_shape=(pltpu.SemaphoreType.DMA((N_PEERS,)), pltpu.SemaphoreType.DMA((N_PEERS,)),
                   pltpu.HBM(dw.shape, dw.dtype), pltpu.HBM(land_shape, dw.dtype), jax.ShapeDtypeStruct((8, LANES), F32)),
        in_specs=[_HBM_SPEC, _HBM_SPEC],
        out_specs=(_SEM_SPEC, _SEM_SPEC, _HBM_SPEC, _HBM_SPEC, pl.BlockSpec(memory_space=pltpu.VMEM)),
        input_output_aliases={0: 2, 1: 3},
        compiler_params=pltpu.CompilerParams(has_side_effects=_DATAFLOW),
    )(pltpu.with_memory_space_constraint(dw, pltpu.HBM), land)


def _scatter_wait(send_sems, recv_sems, dw, land, axis, after, *, name):
    size = dw.shape[axis] // N_DEV

    def body(dw_ref, land_ref, send_sems, recv_sems, after_ref, dw_thru, land_thru):
        for r in range(1, N_DEV):
            p = _peer(r)
            cp = pltpu.make_async_remote_copy(src_ref=_slot(dw_ref, axis, _index_of(p), size), dst_ref=land_ref.at[r - 1],
                                              send_sem=send_sems.at[r - 1], recv_sem=recv_sems.at[r - 1], device_id=p,
                                              device_id_type=MESH)
            cp.wait_send()
            cp.wait_recv()

    return pl.pallas_call(
        body, name=name, out_shape=(pltpu.HBM(dw.shape, dw.dtype), pltpu.HBM(land.shape, land.dtype)),
        in_specs=[_HBM_SPEC, _HBM_SPEC, _SEM_SPEC, _SEM_SPEC, _ANY_SPEC], out_specs=(_HBM_SPEC, _HBM_SPEC),
        input_output_aliases={0: 0, 1: 1},
        compiler_params=pltpu.CompilerParams(has_side_effects=_DATAFLOW),
    )(dw, land, send_sems, recv_sems, after)


def _adamw_math(w, g, m, v):
    m = ADAM_B1 * m + (1.0 - ADAM_B1) * g
    v = ADAM_B2 * v + (1.0 - ADAM_B2) * (g * g)
    m_hat = m / (1.0 - ADAM_B1 ** ADAM_STEP)
    v_hat = v / (1.0 - ADAM_B2 ** ADAM_STEP)
    delta = -ADAM_LR * (m_hat / (jnp.sqrt(v_hat) + ADAM_EPS) + ADAM_WD * w)
    return delta, m, v


def _adamw_sharded(w, m, v, layer, dw, land, axis, me_arr, prev, *, name):
    nl, Ks, Ns = w.shape
    tr = _pick(Ks, 128, 16)
    nblk = Ks // tr
    if axis == 0:
        own_map = lambda i, me: (me[0] * nblk + i, 0)
    else:
        own_map = lambda i, me: (i, me[0])
    wspec = pl.BlockSpec((None, tr, Ns), lambda i, me: (layer, i, 0))
    n_prev = 0 if prev is None else 4

    def body(me_ref, w_ref, m_ref, v_ref, own_ref, r_ref, *rest):
        g_ref, d_ref, nm_ref, nv_ref = rest[n_prev:]
        g = own_ref[...].astype(F32)
        for r in range(N_PEERS):
            g = g + r_ref[r].astype(F32)
        delta, nm, nv = _adamw_math(w_ref[...], g, m_ref[...], v_ref[...])
        g_ref[...], d_ref[...], nm_ref[...], nv_ref[...] = g, delta, nm, nv

    return pl.pallas_call(
        body, name=name, out_shape=[jax.ShapeDtypeStruct((nl, Ks, Ns), F32)] * 4,
        grid_spec=pltpu.PrefetchScalarGridSpec(
            num_scalar_prefetch=1, grid=(nblk,),
            in_specs=[wspec, wspec, wspec, pl.BlockSpec((tr, Ns), own_map),
                      pl.BlockSpec((N_PEERS, tr, Ns), lambda i, me: (0, i, 0))] + [_ANY_SPEC] * n_prev,
            out_specs=[wspec] * 4),
        input_output_aliases={6 + k: k for k in range(n_prev)},
        compiler_params=_cparams("parallel"),
    )(me_arr, w, m, v, dw, land, *(prev or []))


def _adamw_flat(w, g, m, v, *, name):
    def body(w_ref, g_ref, m_ref, v_ref, d_ref, nm_ref, nv_ref):
        d_ref[...], nm_ref[...], nv_ref[...] = _adamw_math(w_ref[...], g_ref[...], m_ref[...], v_ref[...])

    spec = pl.BlockSpec(memory_space=pltpu.VMEM)
    return pl.pallas_call(body, name=name, in_specs=[spec] * 4, out_specs=[spec] * 3,
                          out_shape=[jax.ShapeDtypeStruct(w.shape, F32)] * 3)(w, g, m, v)


def _sum_devices(a, *, name):
    def body(a_ref, o_ref):
        s = a_ref[0]
        for d in range(1, N_DEV):
            s = s + a_ref[d]
        o_ref[...] = s

    spec = pl.BlockSpec(memory_space=pltpu.VMEM)
    return pl.pallas_call(body, name=name, in_specs=[spec], out_specs=spec,
                          out_shape=jax.ShapeDtypeStruct(a.shape[1:], F32))(a)


def _ada_mods(c16, ada_w, ada_b_cols, *, name):
    nl, D, cols = ada_w.shape
    bn = _pick(cols, 512)

    def body(c_ref, w_ref, b_ref, o_ref):
        cond = _silu(c_ref[...]).astype(BF16)
        o_ref[...] = _dot(cond, w_ref[...].astype(BF16)) + b_ref[...]

    return pl.pallas_call(
        body, name=name, grid=(nl, cols // bn),
        in_specs=[pl.BlockSpec((16, D), lambda l, j: (0, 0)), pl.BlockSpec((None, D, bn), lambda l, j: (l, 0, j)),
                  pl.BlockSpec((None, 1, bn), lambda l, j: (l, 0, j))],
        out_specs=pl.BlockSpec((None, 16, bn), lambda l, j: (l, 0, j)),
        out_shape=jax.ShapeDtypeStruct((nl, 16, cols), F32),
        compiler_params=_cparams("parallel", "parallel"),
    )(c16, ada_w, ada_b_cols)


def _ada_bwd(cond_t, dmod, w, m, v, *, name):
    nl, D, cols = w.shape
    tr = _pick(D, 256, 8)

    def body(ct_ref, dm_ref, w_ref, m_ref, v_ref, g_ref, d_ref, nm_ref, nv_ref, dc_ref):
        ct, dm, wt = ct_ref[...], dm_ref[...], w_ref[...]
        g = ct[:, 0:1] * dm[0:1, :]
        for r in range(1, N_DEV + 1):
            g = g + ct[:, r:r + 1] * dm[r:r + 1, :]
        delta, nm, nv = _adamw_math(wt, g, m_ref[...], v_ref[...])
        g_ref[...], d_ref[...], nm_ref[...], nv_ref[...] = g, delta, nm, nv
        dc_ref[...] = jnp.sum(wt * dm[N_DEV:N_DEV + 1, :], axis=-1, keepdims=True)

    wspec = pl.BlockSpec((None, tr, cols), lambda l, i: (l, i, 0))
    return pl.pallas_call(
        body, name=name, grid=(nl, D // tr),
        in_specs=[pl.BlockSpec((tr, 16), lambda l, i: (i, 0)), pl.BlockSpec((None, 16, cols), lambda l, i: (l, 0, 0)),
                  wspec, wspec, wspec],
        out_specs=[wspec] * 4 + [pl.BlockSpec((None, tr, 1), lambda l, i: (l, i, 0))],
        out_shape=[jax.ShapeDtypeStruct((nl, D, cols), F32)] * 4 + [jax.ShapeDtypeStruct((nl, D, 1), F32)],
        compiler_params=_cparams("parallel", "parallel"),
    )(cond_t, dmod, w, m, v)


def _rope_tables(L, CTX):
    def angles(pos, dim):
        inv_freq = ROPE_BASE ** (-jnp.arange(0, dim, 2, dtype=F32) / dim)
        return pos.astype(F32)[:, None] * inv_freq[None, :]

    def pad(cos, sin):
        return (jnp.concatenate([cos, jnp.ones((CTX, LANES), F32)], 0),
                jnp.concatenate([sin, jnp.zeros((CTX, LANES), F32)], 0))

    ret = angles(jnp.arange(L), 2 * LANES)
    ret_cs = pad(jnp.cos(ret), jnp.sin(ret))
    rows = angles(jnp.arange(L) // GRID_W, ATT_HEAD_DIM // 2)
    cols = angles(jnp.arange(L) % GRID_W, ATT_HEAD_DIM // 2)
    cos = jnp.concatenate([jnp.cos(rows)] * 2 + [jnp.cos(cols)] * 2, axis=1)
    sin = jnp.concatenate([-jnp.sin(rows), jnp.sin(rows), -jnp.sin(cols), jnp.sin(cols)], axis=1)
    return ret_cs, pad(cos, sin)


def kernel(x, c, ctx, c_ctx, ada_w, ada_b, norm_mix_g, norm_mlp_g, mlp_w1, mlp_w2, ret_w_in, ret_w_out, ret_decay_fwd, ret_decay_bwd, attn_w_in, attn_w_out, attn_sink, final_norm_g, loss_target, m_c_ctx, m_ada_w, m_ada_b, m_norm_mix_g, m_norm_mlp_g, m_mlp_w1, m_mlp_w2, m_ret_w_in, m_ret_w_out, m_ret_decay_fwd, m_ret_decay_bwd, m_attn_w_in, m_attn_w_out, m_attn_sink, m_final_norm_g, v_c_ctx, v_ada_w, v_ada_b, v_norm_mix_g, v_norm_mlp_g, v_mlp_w1, v_mlp_w2, v_ret_w_in, v_ret_w_out, v_ret_decay_fwd, v_ret_decay_bwd, v_attn_w_in, v_attn_w_out, v_attn_sink, v_final_norm_g):
    L, D = x.shape[1], x.shape[2]
    CTX = ctx.shape[1]
    T = L + CTX
    RH = ret_decay_fwd.shape[-1]
    assert D == RH * 2 * LANES and ada_w.shape[0] == 2 and ret_w_in.shape[0] == 1 and attn_w_in.shape[0] == 1
    Hq = attn_sink.shape[-1]
    Hkv = (attn_w_in.shape[-1] * N_DEV // ATT_HEAD_DIM - Hq) // 2
    G = Hq // Hkv
    FF = mlp_w1.shape[-1] * N_DEV
    Wq_r, Wv_r = RH * 2 * LANES, RH * 4 * LANES
    acols = ada_w.shape[-1]
    tr = _pick(CTX, 256, 8)
    tr_wide = _pick(CTX, 128, 8)
    bmT = T // 4 if (T % 64 == 0) else T
    bmL = L // 4 if (L % 64 == 0) else L
    x_idx, y_idx, c_idx = lax.axis_index("x"), lax.axis_index("y"), lax.axis_index("c")
    me = 4 * x_idx + 2 * y_idx + c_idx
    me_arr = jnp.reshape(me, (1,)).astype(jnp.int32)

    (rcos, rsin), (acos, asin) = _rope_tables(L, CTX)
    lg_f = jax.nn.log_sigmoid(ret_decay_fwd.astype(F32))
    lg_b = jax.nn.log_sigmoid(ret_decay_bwd.astype(F32))

    c_pad = jnp.concatenate([c.astype(F32), jnp.zeros((7, D), F32)], 0)
    c_all = _all_gather_small(c_pad, name="ag_c")[:, 0, :]
    c16 = jnp.concatenate([c_all, c_ctx[None, :], jnp.zeros((7, D), F32)], 0)
    ada_b_cols = lax.dynamic_slice_in_dim(ada_b, me * acols, acols, axis=1)[:, None, :]
    mods_shard = _ada_mods(c16, ada_w, ada_b_cols, name="ada_mods")
    mods_all = _all_gather_small(mods_shard.reshape(32, acols), name="ag_mods")

    wdefs = {"ret_in": (ret_w_in, 0, 1), "ret_out": (ret_w_out, 0, 0), "w1_0": (mlp_w1, 0, 1), "w2_0": (mlp_w2, 0, 0),
             "attn_in": (attn_w_in, 0, 1), "attn_out": (attn_w_out, 0, 0), "w1_1": (mlp_w1, 1, 1), "w2_1": (mlp_w2, 1, 0)}
    groups = [["ret_in"], ["ret_out"], ["w1_0", "w2_0"], ["attn_in", "attn_out"], ["w1_1", "w2_1"]]

    placed = {}

    def ag_start(gi, after):
        g_axes = [wdefs[k][2] for k in groups[gi]]
        ssem, rsem, lands, tok_ = _gather_start([placed[k] for k in groups[gi]], g_axes, after, name=f"ag_start{gi}")
        return dict(a=(ssem, rsem), lands=lands, axes=g_axes, gi=gi), tok_

    def ag_forward(g, after):
        fs, fr, g["lands"], tok_ = _gather_forward(*g["a"], g["lands"], g["axes"], after, name=f"ag_forward{g['gi']}")
        g["f"] = (fs, fr)
        return tok_

    def ag_wait(g, after):
        return _gather_wait(*g["a"], *g["f"], g["lands"], g["axes"], after, name=f"ag_wait{g['gi']}")

    placed["ret_in"] = _cast_place(*wdefs["ret_in"], me_arr, mods_all, name="place_ret_in")
    g0, tok = ag_start(0, mods_all)
    last_cast = tok
    for keys in groups[1:]:
        for k in keys:
            last_cast = placed[k] = _cast_place(*wdefs[k], me_arr, last_cast, name=f"place_{k}")
    mods_all = (mods_all + tok[0, 0]).reshape(N_DEV, 2, 16, acols).transpose(1, 2, 0, 3).reshape(2, 16, 6, D)
    mod_lat = lax.dynamic_index_in_dim(mods_all, me, axis=1, keepdims=False)
    mod_ctx = mods_all[:, N_DEV]

    def pack(i, ks, kc):
        return jnp.stack([mod_lat[i, ks], mod_lat[i, kc], mod_ctx[i, ks], mod_ctx[i, kc]], 0)

    def gates(i, k):
        return jnp.stack([mod_lat[i, k], mod_ctx[i, k]], 0)

    def gate_epilogue(x_rows_lat_only):
        def epi(acc, i, j, xt, gv):
            if x_rows_lat_only:
                gate = gv[0:1, :]
            else:
                row = i * acc.shape[0] + lax.broadcasted_iota(jnp.int32, (acc.shape[0], 1), 0)
                gate = jnp.where(row >= L, gv[1:2, :], gv[0:1, :])
            return xt + gate * acc, acc
        return epi

    w1, w2 = {}, {}

    mmT = dict(M=T, bm=bmT)
    mmL = dict(M=L, bm=bmL)

    def bn_of(n, off=0):
        b = MM_BN
        while n % b or off % b:
            b -= LANES
        return b

    X0 = jnp.concatenate([x[0], ctx[0]], axis=0)
    g_mix0, g_mlp0 = norm_mix_g[0:1], norm_mlp_g[0:1]
    g_mix1, g_mlp1 = norm_mix_g[1:2], norm_mlp_g[1:2]
    a0 = _normmod(X0, g_mix0, pack(0, 0, 1), R=T, L=L, tr=tr, name="normmod_mix0",
                  dep=(last_cast, rcos, rsin, acos, asin))
    tok = ag_forward(g0, a0)
    (wr_in,) = ag_wait(g0, tok)
    g1, tok = ag_start(1, wr_in)
    g1b, tok = ag_start(2, tok)

    bn_qk = _pick(Wq_r, MM_BN, 2 * LANES)
    nq_blocks = Wq_r // bn_qk
    kscale = float(2 * LANES) ** -0.5

    def rope_epi(acc, i, j, cos, sin):
        parts = []
        for h in range(acc.shape[1] // (2 * LANES)):
            x1 = acc[:, h * 2 * LANES:h * 2 * LANES + LANES]
            x2 = acc[:, h * 2 * LANES + LANES:(h + 1) * 2 * LANES]
            parts += [x1 * cos - x2 * sin, x2 * cos + x1 * sin]
        return (jnp.concatenate(parts, axis=1) * jnp.where(j < nq_blocks, 1.0, kscale),)

    def row_tile(arr, bm):
        return (arr, (bm, LANES), lambda i, j: (i, 0))

    (qk0,) = _mm(a0, wr_in, "nn", [BF16], N=2 * Wq_r, K=D, bn=bn_qk, bk=D, name="ret_qk", epilogue=rope_epi,
                 extras=[row_tile(rcos, bmT), row_tile(rsin, bmT)], dep=tok, **mmT)
    bn_vg = bn_of(2 * Wv_r, 2 * Wq_r)
    (vg0,) = _mm(a0, wr_in, "nn", [BF16], N=2 * Wv_r, K=D, bn=bn_vg, bk=D, name="ret_vg", b_col0=2 * Wq_r, dep=tok,
                 **mmT)

    tok = ag_forward(g1, vg0)
    of, st_f = _ret_fwd(qk0, vg0, lg_f + tok[0, 0], None, T=T, L=L, H=RH, rev=False, name="ret_scan_f")
    (wr_out,) = ag_wait(g1, of)
    o0, z0, st_b = _ret_fwd(qk0, vg0, lg_b, of, T=T, L=L, H=RH, rev=True, name="ret_scan_b", readout=True)
    tok = ag_forward(g1b, o0)

    bnD = _pick(D, MM_BN)

    def xtile(arr, bm):
        return (arr, (bm, bnD), lambda i, j: (i, j))

    def gtile(gv):
        return (gv, (2, bnD), lambda i, j: (0, j))

    bk_max = 2048

    def quarter(rows, div=4):
        return dict(M=rows["M"], bm=rows["bm"] // div if rows["bm"] % (16 * div) == 0 else rows["bm"])
    X1, ro0 = _mm(z0, wr_out, "nn", [F32, BF16], N=D, K=Wv_r, bn=bnD, bk=Wv_r, name="ret_out", cols_outer=True,
                  epilogue=gate_epilogue(False), extras=[xtile(X0, quarter(mmT, 2)["bm"]), gtile(gates(0, 2))], dep=tok,
                  **quarter(mmT, 2))

    def mlp_fwd(Xin, i, g_mlp, rows, name, weights_after_norm=None):
        a = _normmod(Xin, g_mlp, pack(i, 3, 4), R=rows["M"], L=L, tr=tr, name=f"normmod_mlp{name}")
        dep_up = None if weights_after_norm is None else weights_after_norm(a)

        def relu2(acc, i_, j_):
            u = jnp.maximum(acc, 0.0)
            return u, u * u

        bnF = _pick(FF, MM_BN)
        u, r = _mm(a, w1[i], "nn", [BF16, BF16], N=FF, K=D, bn=bnF, bk=D, name=f"mlp_up{name}", epilogue=relu2,
                   dep=dep_up, **rows)
        Xout, mo = _mm(r, w2[i], "nn", [F32, BF16], N=D, K=FF, bn=bnD, bk=FF, name=f"mlp_down{name}",
                       epilogue=gate_epilogue(rows["M"] == L), cols_outer=True,
                       extras=[xtile(Xin, quarter(rows)["bm"]), gtile(gates(i, 5))], **quarter(rows))
        return a, u, r, Xout, mo

    later = {}

    def mlp0_weights(a):
        w1[0], w2[0] = ag_wait(g1b, a)
        later["g2"], tok2 = ag_start(3, w1[0])
        later["g3"], tok3 = ag_start(4, tok2)
        return tok3

    a1, u0, r0, X2, mo0 = mlp_fwd(X1, 0, g_mlp0, mmT, "0", weights_after_norm=mlp0_weights)
    g2, g3 = later["g2"], later["g3"]

    tok = ag_forward(g2, X2)
    a2 = _normmod(X2, g_mix1, pack(1, 0, 1), R=T, L=L, tr=tr, name="normmod_mix1", dep=tok)
    wa_in, wa_out = ag_wait(g2, a2)
    Wq_a, Wk_a = Hq * LANES, Hkv * LANES

    def arope_epi(acc, i, j, cos, sin):
        heads = acc.shape[1] // LANES
        return (acc * jnp.tile(cos, (1, heads)) + _swap32(acc) * jnp.tile(sin, (1, heads)),)

    bn_q = _pick(Wq_a, MM_BN)
    (q1,) = _mm(a2, wa_in, "nn", [BF16], N=Wq_a, K=D, bn=bn_q, bk=D, name="attn_q", epilogue=arope_epi,
                extras=[row_tile(acos, bmL), row_tile(asin, bmL)], **mmL)
    bn_k = bn_of(Wk_a, Wq_a)
    (k1,) = _mm(a2, wa_in, "nn", [BF16], N=Wk_a, K=D, bn=bn_k, bk=D, name="attn_k", b_col0=Wq_a, epilogue=arope_epi,
                extras=[row_tile(acos, bmT), row_tile(asin, bmT)], **mmT)
    bn_v = bn_of(Wk_a, Wq_a + Wk_a)
    (v1,) = _mm(a2, wa_in, "nn", [BF16], N=Wk_a, K=D, bn=bn_v, bk=D, name="attn_v", b_col0=Wq_a + Wk_a, **mmT)
    tok = ag_forward(g3, q1)
    o1, lse = _attn_fwd(q1, k1, v1, attn_sink + tok[0, 0], L=L, CTX=CTX, Hkv=Hkv, G=G, name="attn_fwd")
    w1[1], w2[1] = ag_wait(g3, o1)
    X3, ao = _mm(o1, wa_out, "nn", [F32, BF16], N=D, K=Wq_a, bn=bnD, bk=_pick(Wq_a, 2048), name="attn_out",
                 epilogue=gate_epilogue(True), extras=[xtile(X2, bmL), gtile(gates(1, 2))], **mmL)
    a3, u1, r1, X4, mo1 = mlp_fwd(X3, 1, g_mlp1, mmL, "1")

    dX4, dmo1, acc_head = _loss_head(X4, loss_target[0], mo1, final_norm_g[None, :], gates(1, 5)[0:1], L=L, tr=tr,
                                     name="loss_head")
    loss_part = jnp.sum(acc_head[0, 0])
    d_gf = acc_head[0, 1]
    zeros_d = jnp.zeros((D,), F32)
    dmod_lat = [[zeros_d] * 6, [zeros_d] * 6]
    dmod_ctx = [[zeros_d] * 6, [zeros_d] * 6]
    dmod_lat[1][5] = acc_head[0, 2]

    def dw_mm(a, b, M, N, K, name):
        return _mm(a, b, "tn", [BF16], M=M, N=N, K=K, bm=_pick(M, 512), bn=_pick(N, MM_BN), bk=K, name=name)[0]

    def mlp_bwd(dmo, a, u, r, i, rows, name):
        Mr = rows["M"]

        def times_2u(acc, i_, j_, ut):
            return (acc * (2.0 * ut.astype(F32)),)

        bnF = _pick(FF, MM_BN)
        dw2 = dw_mm(r, dmo, FF, D, Mr, f"mlp_down_dw{name}")
        tok_ = send_grad(f"w2_{i}", dw2, 0)
        (dh,) = _mm(dmo, w2[i], "nt", [BF16], N=FF, K=D, bn=bnF, bk=D, name=f"mlp_down_dx{name}", epilogue=times_2u,
                    extras=[(u, (rows["bm"], bnF), lambda i_, j_: (i_, j_))], dep=tok_, **rows)
        dw1 = dw_mm(a, dh, D, FF, Mr, f"mlp_up_dw{name}")
        tok_ = send_grad(f"w1_{i}", dw1, 1)
        (da,) = _mm(dh, w1[i], "nt", [BF16], N=D, K=FF, bn=bnD, bk=FF, name=f"mlp_up_dx{name}", dep=tok_,
                    cols_outer=True, **quarter(rows))
        return da

    pending = []

    def send_grad(key, dw, axis):
        ssem, rsem, dw_thru, land, tok_ = _scatter_start(dw, axis, name=f"rs_start_{key}")
        pending.append((key, axis, ssem, rsem, dw_thru, land))
        return tok_

    da3 = mlp_bwd(dmo1, a3, u1, r1, 1, mmL, "1")
    dX3, dao, acc = _normmod_bwd(X3, da3, dX4, False, g_mlp1, pack(1, 3, 4), (ao, gates(1, 2)), R=L, L=L, tr=tr,
                                 name="normmod_mlp1_bwd")
    dmod_lat[1][3], dmod_lat[1][4], d_gmlp1, dmod_lat[1][2] = acc[0, 0], acc[0, 1], acc[0, 2], acc[0, 3]

    dwa_out = dw_mm(o1, dao, Wq_a, D, L, "attn_out_dw")
    tok = send_grad("attn_out", dwa_out, 0)
    (do1,) = _mm(dao, wa_out, "nt", [BF16], N=Wq_a, K=D, bn=_pick(Wq_a, MM_BN), bk=D, name="attn_out_dx", dep=tok, **mmL)
    dq1, dk1, dv1, dkx, dvx, dsink_acc = _attn_bwd(q1, k1, v1, do1, lse, attn_sink, L=L, CTX=CTX, Hkv=Hkv, G=G,
                                                   name="attn_bwd")
    dp1 = _attn_assemble(dq1, dk1, dv1, dkx, dvx, acos, asin, T=T, L=L, CTX=CTX, Hq=Hq, Hkv=Hkv, tr=tr_wide,
                         name="attn_assemble")
    Wa_in = Wq_a + 2 * Wk_a
    dwa_in = dw_mm(a2, dp1, D, Wa_in, T, "attn_in_dw")
    tok = send_grad("attn_in", dwa_in, 1)
    (da2,) = _mm(dp1, wa_in, "nt", [BF16], N=D, K=Wa_in, bn=bnD, bk=_pick(Wa_in, 2 * bk_max), name="attn_in_dx",
                 dep=tok, **mmT)
    dX2, dmo0, acc = _normmod_bwd(X2, da2, dX3, True, g_mix1, pack(1, 0, 1), (mo0, gates(0, 5)), R=T, L=L, tr=tr,
                                  name="normmod_mix1_bwd")
    dmod_lat[1][0], dmod_lat[1][1], d_gmix1, dmod_lat[0][5] = acc[0, 0], acc[0, 1], acc[0, 2] + acc[1, 2], acc[0, 3]
    dmod_ctx[1][0], dmod_ctx[1][1], dmod_ctx[0][5] = acc[1, 0], acc[1, 1], acc[1, 3]

    da1 = mlp_bwd(dmo0, a1, u0, r0, 0, mmT, "0")
    dX1, dro0, acc = _normmod_bwd(X1, da1, dX2, False, g_mlp0, pack(0, 3, 4), (ro0, gates(0, 2)), R=T, L=L, tr=tr,
                                  name="normmod_mlp0_bwd")
    dmod_lat[0][3], dmod_lat[0][4], d_gmlp0, dmod_lat[0][2] = acc[0, 0], acc[0, 1], acc[0, 2] + acc[1, 2], acc[0, 3]
    dmod_ctx[0][3], dmod_ctx[0][4], dmod_ctx[0][2] = acc[1, 0], acc[1, 1], acc[1, 3]

    dwr_out = dw_mm(z0, dro0, Wv_r, D, T, "ret_out_dw")
    tok = send_grad("ret_out", dwr_out, 0)
    (dz0,) = _mm(dro0, wr_out, "nt", [BF16], N=Wv_r, K=D, bn=_pick(Wv_r, MM_BN), bk=D, name="ret_out_dx", dep=tok, **mmT)
    do0, dg0 = _readout_bwd(dz0, o0, vg0, T=T, L=L, H=RH, tr=tr_wide, name="ret_readout_bwd")
    (dq_f, dk_f, dv_f, dlg_f), (dq_b, dk_b, dv_b, dlg_b) = _ret_bwd(qk0, vg0, do0, (st_f, st_b), (lg_f, lg_b),
                                                                    T=T, L=L, H=RH, name="ret_scan_bwd")
    dp0 = _ret_assemble(dq_f, dq_b, dk_f, dk_b, dv_f, dv_b, dg0, rcos, rsin, T=T, L=L, H=RH, tr=tr_wide,
                        name="ret_assemble")
    Wr_in = 2 * Wq_r + 2 * Wv_r
    dwr_in = dw_mm(a0, dp0, D, Wr_in, T, "ret_in_dw")
    tok = send_grad("ret_in", dwr_in, 1)
    (da0,) = _mm(dp0, wr_in, "nt", [BF16], N=D, K=Wr_in, bn=bnD, bk=Wr_in, name="ret_in_dx", dep=tok, cols_outer=True,
                 b_buffers=1, **quarter(mmT))
    dX0, acc = _normmod_bwd(X0, da0, dX1, False, g_mix0, pack(0, 0, 1), None, R=L, L=L, tr=tr, name="normmod_mix0_bwd")
    _, acc_c = _normmod_bwd(X0, da0, dX1, False, g_mix0, pack(0, 0, 1), None, R=CTX, L=0, tr=tr, row0=L,
                            name="normmod_mix0_bwd_ctx")
    dmod_lat[0][0], dmod_lat[0][1], d_gmix0 = acc[0, 0], acc[0, 1], acc[0, 2] + acc_c[1, 2]
    dmod_ctx[0][0], dmod_ctx[0][1] = acc_c[1, 0], acc_c[1, 1]
    grad_x = dX0[None]

    wmv = {"ret_in": (ret_w_in, m_ret_w_in, v_ret_w_in, 0, "ret_w_in"),
           "ret_out": (ret_w_out, m_ret_w_out, v_ret_w_out, 0, "ret_w_out"),
           "attn_in": (attn_w_in, m_attn_w_in, v_attn_w_in, 0, "attn_w_in"),
           "attn_out": (attn_w_out, m_attn_w_out, v_attn_w_out, 0, "attn_w_out"),
           "w1_0": (mlp_w1, m_mlp_w1, v_mlp_w1, 0, "mlp_w1"), "w1_1": (mlp_w1, m_mlp_w1, v_mlp_w1, 1, "mlp_w1"),
           "w2_0": (mlp_w2, m_mlp_w2, v_mlp_w2, 0, "mlp_w2"), "w2_1": (mlp_w2, m_mlp_w2, v_mlp_w2, 1, "mlp_w2")}
    big = {}

    def finish_grad(entry, after):
        key, axis, ssem, rsem, dw_thru, land = entry
        dw_done, land_done = _scatter_wait(ssem, rsem, dw_thru, land, axis, after, name=f"rs_wait_{key}")
        w_, m_, v_, layer, out_name = wmv[key]
        big[out_name] = _adamw_sharded(w_, m_, v_, layer, dw_done, land_done, axis, me_arr, big.get(out_name),
                                       name=f"adamw_{key}")
        return big[out_name][0]

    after = dX0
    for entry in pending[:-1]:
        after = finish_grad(entry, after)

    misc = jnp.zeros((D,), F32)
    misc = misc.at[0:RH].set(dlg_f[:, 0, 0]).at[RH:2 * RH].set(dlg_b[:, 0, 0])
    misc = misc.at[2 * RH:2 * RH + Hq].set(dsink_acc[:, :G, 0].reshape(Hq)).at[2 * RH + Hq].set(loss_part)
    rows = ([dmod_lat[i][k] for i in range(2) for k in range(6)] + [dmod_ctx[i][k] for i in range(2) for k in range(6)]
            + [d_gmix0, d_gmix1, d_gmlp0, d_gmlp1, d_gf, misc, zeros_d, zeros_d])
    part = jnp.stack(rows, 0)
    part_all = _all_gather_small(part, name="ag_small_grads")
    tot = _sum_devices(part_all, name="sum_small_grads")

    grad_ada_b = (tot[0:12] + tot[12:24]).reshape(2, 6 * D)
    grad_norm_mix_g, grad_norm_mlp_g, grad_final_norm_g = tot[24:26], tot[26:28], tot[28]
    grad_ret_decay_fwd = (tot[29, 0:RH] * jax.nn.sigmoid(-ret_decay_fwd[0]))[None]
    grad_ret_decay_bwd = (tot[29, RH:2 * RH] * jax.nn.sigmoid(-ret_decay_bwd[0]))[None]
    grad_attn_sink = tot[29, 2 * RH:2 * RH + Hq][None]
    loss = tot[29, 2 * RH + Hq]

    dlat_cols = lax.dynamic_slice_in_dim(part_all[:, 0:12].reshape(N_DEV, 2, 6 * D), me * acols, acols, axis=2)
    dctx_cols = lax.dynamic_slice_in_dim(tot[12:24].reshape(2, 6 * D), me * acols, acols, axis=1)
    dmod16 = jnp.concatenate([dlat_cols.transpose(1, 0, 2), dctx_cols[:, None, :], jnp.zeros((2, 7, acols), F32)], 1)
    cond_t = _silu(c16).T
    g_ada, d_ada, nm_ada, nv_ada, dcond_part = _ada_bwd(cond_t, dmod16, ada_w, m_ada_w, v_ada_w, name="ada_bwd")
    dcond = (dcond_part[0, :, 0] + dcond_part[1, :, 0]).reshape(D // LANES, LANES)
    pad_rows = -(D // LANES) % 8
    dcond_pad = jnp.concatenate([dcond, jnp.zeros((pad_rows, LANES), F32)], 0) if pad_rows else dcond
    dcond_all = _all_gather_small(dcond_pad, name="ag_dcond")
    dcond_tot = _sum_devices(dcond_all, name="sum_dcond")[:D // LANES].reshape(D)
    sg = jax.nn.sigmoid(c_ctx)
    grad_c_ctx = dcond_tot * (sg * (1.0 + c_ctx * (1.0 - sg)))

    small_w = [c_ctx, ada_b, norm_mix_g, norm_mlp_g, ret_decay_fwd, ret_decay_bwd, attn_sink, final_norm_g]
    small_g = [grad_c_ctx, grad_ada_b, grad_norm_mix_g, grad_norm_mlp_g, grad_ret_decay_fwd, grad_ret_decay_bwd,
               grad_attn_sink, grad_final_norm_g]
    small_m = [m_c_ctx, m_ada_b, m_norm_mix_g, m_norm_mlp_g, m_ret_decay_fwd, m_ret_decay_bwd, m_attn_sink,
               m_final_norm_g]
    small_v = [v_c_ctx, v_ada_b, v_norm_mix_g, v_norm_mlp_g, v_ret_decay_fwd, v_ret_decay_bwd, v_attn_sink,
               v_final_norm_g]
    sizes = [w_.size for w_ in small_w]
    total = sum(-(-s // LANES) * LANES for s in sizes)
    total_pad = -(-total // (8 * LANES)) * 8 * LANES

    def flat_pack(ts, fill):
        pieces = []
        for t_ in ts:
            f = t_.reshape(-1).astype(F32)
            pad = -f.size % LANES
            pieces.append(jnp.concatenate([f, jnp.full((pad,), fill, F32)]) if pad else f)
        pieces.append(jnp.full((total_pad - total,), fill, F32))
        return jnp.concatenate(pieces).reshape(total_pad // LANES, LANES)

    d_s, nm_s, nv_s = _adamw_flat(flat_pack(small_w, 0.0), flat_pack(small_g, 0.0), flat_pack(small_m, 0.0),
                                  flat_pack(small_v, 1.0), name="adamw_small")

    def unpack(p):
        flat = p.reshape(-1)
        res, off = [], 0
        for w_, s in zip(small_w, sizes):
            res.append(flat[off:off + s].reshape(w_.shape))
            off += -(-s // LANES) * LANES
        return res

    finish_grad(pending[-1], d_s)
    d_small, nm_small, nv_small = unpack(d_s), unpack(nm_s), unpack(nv_s)
    small_names = ["c_ctx", "ada_b", "norm_mix_g", "norm_mlp_g", "ret_decay_fwd", "ret_decay_bwd", "attn_sink",
                   "final_norm_g"]
    sm = {n: (g_, d_, m_, v_) for n, g_, d_, m_, v_ in zip(small_names, small_g, d_small, nm_small, nv_small)}

    def out4(n):
        if n == "ada_w":
            return g_ada, d_ada, nm_ada, nv_ada
        if n in big:
            return tuple(big[n])
        return sm[n]

    order = ["c_ctx", "ada_w", "ada_b", "norm_mix_g", "norm_mlp_g", "mlp_w1", "mlp_w2", "ret_w_in", "ret_w_out",
             "ret_decay_fwd", "ret_decay_bwd", "attn_w_in", "attn_w_out", "attn_sink", "final_norm_g"]
    quads = [out4(n) for n in order]
    return (loss, grad_x, *[q_[0] for q_ in quads], *[q_[1] for q_ in quads], *[q_[2] for q_ in quads],
            *[q_[3] for q_ in quads])
```

```python
import jax
import jax.numpy as jnp
from jax import lax
from jax.experimental import pallas as pl
from jax.experimental.pallas import tpu as pltpu

F32 = jnp.float32
BF16 = jnp.bfloat16

N_DEV = 8
NORM_EPS = 1e-6
CHUNK = 128
ATT_HEAD_DIM = 128
GRID_W = 64
ROPE_BASE = 10000.0
NEG_INF = -1e30
ADAM_LR, ADAM_B1, ADAM_B2, ADAM_EPS, ADAM_WD, ADAM_STEP = 0.001, 0.9, 0.999, 1e-08, 0.01, 10

V7X_VMEM_LIMIT_BYTES = 56 * 1024 * 1024
MM_BN = 1024
LANES = 128
MESH = pl.DeviceIdType.MESH

_NN = (((1,), (0,)), ((), ()))
_NT = (((1,), (1,)), ((), ()))
_TN = (((0,), (0,)), ((), ()))


def _dot(a, b, dn=_NN):
    return lax.dot_general(a, b, dn, preferred_element_type=F32)


def _cparams(*sem):
    return pltpu.CompilerParams(dimension_semantics=sem, vmem_limit_bytes=V7X_VMEM_LIMIT_BYTES)


def _pick(n, pref, mult=LANES):
    if n <= pref:
        return n
    best = None
    for d in range(mult, pref + 1, mult):
        if n % d == 0:
            best = d
    assert best is not None, (n, pref)
    return best


def _silu(x):
    return x * jax.nn.sigmoid(x)


def _mm(a, b, mode, out_dtypes, *, M, N, K, bm, bn, bk, name, b_col0=0, epilogue=None, extras=(), dep=None,
        cols_outer=False, b_buffers=None):
    assert M % bm == 0 and N % bn == 0 and K % bk == 0 and b_col0 % bn == 0, (name, M, N, K, bm, bn, bk, b_col0)
    nk = K // bk
    c0 = b_col0 // bn
    ax_i, ax_j = (1, 0) if cols_outer else (0, 1)

    def spec(block, f, **kw):
        if cols_outer:
            return pl.BlockSpec(block, lambda j, i, k: f(i, j, k), **kw)
        return pl.BlockSpec(block, f, **kw)

    b_kw = {} if b_buffers is None else dict(pipeline_mode=pl.Buffered(b_buffers))
    if mode == "nn":
        a_spec = spec((bm, bk), lambda i, j, k: (i, k))
        b_spec = spec((bk, bn), lambda i, j, k: (k, j + c0), **b_kw)
    elif mode == "nt":
        a_spec = spec((bm, bk), lambda i, j, k: (i, k))
        b_spec = spec((bn, bk), lambda i, j, k: (j + c0, k), **b_kw)
    else:
        a_spec = spec((bk, bm), lambda i, j, k: (k, i))
        b_spec = spec((bk, bn), lambda i, j, k: (k, j + c0), **b_kw)
    dn = {"nn": _NN, "nt": _NT, "tn": _TN}[mode]
    e_specs = [spec(bs, (lambda i, j, k, f=f: f(i, j))) for (_, bs, f) in extras]
    ne, no = len(extras), len(out_dtypes)
    nd = 0 if dep is None else 1

    def body(a_ref, b_ref, *rest):
        e_refs, o_refs = rest[:ne], rest[ne + nd:ne + nd + no]
        i, j, k = pl.program_id(ax_i), pl.program_id(ax_j), pl.program_id(2)

        def finish(acc):
            outs = (acc,) if epilogue is None else epilogue(acc, i, j, *[e[...] for e in e_refs])
            for o_ref, o in zip(o_refs, outs):
                o_ref[...] = o.astype(o_ref.dtype)

        p = _dot(a_ref[...], b_ref[...], dn)
        if nk == 1:
            finish(p)
        else:
            acc_ref = rest[-1]

            @pl.when(k == 0)
            def _():
                acc_ref[...] = p

            @pl.when(k > 0)
            def _():
                acc_ref[...] += p

            @pl.when(k == nk - 1)
            def _():
                finish(acc_ref[...])

    outs = pl.pallas_call(
        body, name=name, grid=(N // bn, M // bm, nk) if cols_outer else (M // bm, N // bn, nk),
        in_specs=[a_spec, b_spec] + e_specs + [pl.BlockSpec(memory_space=pl.ANY)] * nd,
        out_specs=[spec((bm, bn), lambda i, j, k: (i, j)) for _ in out_dtypes],
        out_shape=[jax.ShapeDtypeStruct((M, N), dt) for dt in out_dtypes],
        scratch_shapes=[pltpu.VMEM((bm, bn), F32)] if nk > 1 else [],
        compiler_params=_cparams("parallel", "parallel", "arbitrary"),
    )(a, b, *[e[0] for e in extras], *([dep] if nd else []))
    return outs


def _rowwise(body, rows, vecs, outs, n_acc, *, R, L, tr, name, acc_width=None, dep=None, row0=0):
    assert row0 % tr == 0
    b0 = row0 // tr
    assert R % tr == 0 and L % tr == 0, (name, R, L, tr)
    nl = L // tr
    n_regions = 2 if R > L else 1
    n_rows, n_vecs, n_outs = len(rows), len(vecs), len(outs)
    deps = () if dep is None else (tuple(dep) if isinstance(dep, (tuple, list)) else (dep,))
    n_dep = len(deps)
    acc_pad = -(-n_acc // 8) * 8 if n_acc else 0

    in_specs = []
    for (_, w, cb, lat_only) in rows:
        if lat_only:
            in_specs.append(pl.BlockSpec((tr, w), lambda i, cb=cb: (jnp.minimum(i, nl - 1), cb)))
        else:
            in_specs.append(pl.BlockSpec((tr, w), lambda i, cb=cb: (i + b0, cb)))
    for v in vecs:
        in_specs.append(pl.BlockSpec(v.shape, lambda i, nd=v.ndim: (0,) * nd))
    in_specs += [pl.BlockSpec(memory_space=pl.ANY)] * n_dep
    out_specs = [pl.BlockSpec((tr, w), lambda i: (i, 0)) for (w, _) in outs]
    out_shape = [jax.ShapeDtypeStruct((R, w), dt) for (w, dt) in outs]
    if n_acc:
        out_specs.append(pl.BlockSpec((None, acc_pad, acc_width), lambda i: (jnp.where(i >= nl, 1, 0), 0, 0)))
        out_shape.append(jax.ShapeDtypeStruct((n_regions, acc_pad, acc_width), F32))

    def kern(*refs):
        i = pl.program_id(0)
        is_ctx = i >= nl
        ins = [r[...] for r in refs[:n_rows + n_vecs]]
        o_refs = refs[n_rows + n_vecs + n_dep:]
        out_tiles, acc_rows = body(is_ctx, *ins)
        for o_ref, o in zip(o_refs[:n_outs], out_tiles):
            o_ref[...] = o.astype(o_ref.dtype)
        if n_acc:
            acc_ref = o_refs[n_outs]

            @pl.when((i == 0) | (i == nl))
            def _():
                acc_ref[...] = jnp.zeros_like(acc_ref)

            for r, row in enumerate(acc_rows):
                acc_ref[r:r + 1, :] += row

    res = pl.pallas_call(
        kern, name=name, grid=(R // tr,), in_specs=in_specs, out_specs=out_specs, out_shape=out_shape,
        compiler_params=_cparams("arbitrary"),
    )(*[r[0] for r in rows], *vecs, *deps)
    return res


def _colsum(x):
    return jnp.sum(x, axis=0, keepdims=True)


def _rms_stats(x):
    r = lax.rsqrt(jnp.mean(x * x, axis=-1, keepdims=True) + NORM_EPS)
    return x * r, r


def _sel(is_ctx, pk, lat_row, ctx_row):
    return jnp.where(is_ctx, pk[ctx_row:ctx_row + 1, :], pk[lat_row:lat_row + 1, :])


def _normmod(x, g, pk, *, R, L, tr, name, dep=None):
    D = x.shape[-1]

    def body(is_ctx, xt, gv, pkv):
        xh, _ = _rms_stats(xt)
        sh, sc = _sel(is_ctx, pkv, 0, 2), _sel(is_ctx, pkv, 1, 3)
        return ((xh * gv) * (1.0 + sc) + sh,), ()

    return _rowwise(body, [(x, D, 0, False)], [g, pk], [(D, BF16)], 0, R=R, L=L, tr=tr, name=name, dep=dep)[0]


def _normmod_bwd(x_in, da, dx_out, dx_out_lat_only, g, pk, prev, *, R, L, tr, name, row0=0):
    D = x_in.shape[-1]
    has_prev = prev is not None

    def body(is_ctx, *t):
        if has_prev:
            xt, dat, dxo, mp, gv, pkv, gates = t
        else:
            xt, dat, dxo, gv, pkv = t
        xh, r = _rms_stats(xt)
        dat = dat.astype(F32)
        sc = _sel(is_ctx, pkv, 1, 3)
        if dx_out_lat_only:
            dxo = jnp.where(is_ctx, 0.0, dxo)
        dn = dat * (1.0 + sc)
        w = dn * gv
        dxi = dxo + r * (w - xh * jnp.mean(w * xh, axis=-1, keepdims=True))
        accs = [_colsum(dat), _colsum(dat * (xh * gv)), _colsum(dn * xh)]
        outs = [dxi]
        if has_prev:
            gate = _sel(is_ctx, gates, 0, 1)
            outs.append(dxi * gate)
            accs.append(_colsum(dxi * mp.astype(F32)))
        return outs, accs

    rows = [(x_in, D, 0, False), (da, D, 0, False), (dx_out, D, 0, dx_out_lat_only)]
    vecs = [g, pk]
    outs = [(D, F32)]
    if has_prev:
        rows.append((prev[0], D, 0, False))
        vecs.append(prev[1])
        outs.append((D, BF16))
    return _rowwise(body, rows, vecs, outs, 4 if has_prev else 3, R=R, L=L, tr=tr, name=name, acc_width=D, row0=row0)


def _loss_head(x4, target, m_prev, gf, gate, *, L, tr, name):
    D = x4.shape[-1]

    def body(is_ctx, xt, tg, mp, gfv, gatev):
        xh, r = _rms_stats(xt)
        e = xh * gfv - tg
        dy = e * (1.0 / D)
        w = dy * gfv
        dx = r * (w - xh * jnp.mean(w * xh, axis=-1, keepdims=True))
        accs = [_colsum(e * e) * (0.5 / D), _colsum(dy * xh), _colsum(dx * mp.astype(F32))]
        return (dx, dx * gatev), accs

    return _rowwise(body, [(x4, D, 0, False), (target, D, 0, False), (m_prev, D, 0, False)], [gf, gate],
                    [(D, F32), (D, BF16)], 3, R=L, L=L, tr=tr, name=name, acc_width=D)


RET_CHUNK = 2 * LANES
RET_HEADS_PER_STEP = 4


def _decays(lgh, rev):
    C = RET_CHUNK
    ii = lax.broadcasted_iota(jnp.int32, (C, C), 0)
    jj = lax.broadcasted_iota(jnp.int32, (C, C), 1)
    ri = lax.broadcasted_iota(jnp.int32, (C, 1), 0).astype(F32)
    diff = (jj - ii if rev else ii - jj)
    amat = jnp.where(diff >= 0, jnp.exp(lgh * jnp.maximum(diff, 0).astype(F32)), 0.0)
    pos = (C - ri) if rev else (ri + 1.0)
    bq = jnp.exp(lgh * pos)
    bk = jnp.exp(lgh * (C - pos))
    return amat, bq, bk, pos


def _ret_geometry(T, L, H, rev, backward):
    C = RET_CHUNK
    assert T % C == 0 and L % C == 0, (T, L)
    nT, nL = T // C, L // C
    hb = RET_HEADS_PER_STEP if H % RET_HEADS_PER_STEP == 0 else 1

    def step(s):
        return (nT - 1 - s) if backward else s

    def chunk(s):
        s = step(s)
        return (nT - 1 - s) if rev else (s + nL) % nT

    return C, nT, hb, chunk, step


def _ret_fwd(qk, vg, lg, other, *, T, L, H, rev, name, readout=False):
    dk, dv = 2 * LANES, 4 * LANES
    C, nT, hb, chunk, step = _ret_geometry(T, L, H, rev, False)
    n_other = 0 if other is None else 1
    n_read = 1 if readout else 0

    def body(lg_ref, q_ref, k_ref, v_ref, *rest):
        o_ref, st_ref, s_scr = rest[n_other + n_read], rest[-2], rest[-1]
        hg, s = pl.program_id(0), pl.program_id(1)

        @pl.when(s == 0)
        def _():
            s_scr[...] = jnp.zeros_like(s_scr)

        for hh in range(hb):
            lgh = lg_ref[0, hg * hb + hh]
            amat, bq, bk, _ = _decays(lgh, rev)
            q, k = q_ref[:, hh * dk:(hh + 1) * dk], k_ref[:, hh * dk:(hh + 1) * dk]
            v = v_ref[:, hh * dv:(hh + 1) * dv]
            stb = s_scr[hh].astype(BF16)
            st_ref[hh] = stb
            scores = _dot(q, k, _NT) * amat
            o = _dot(scores.astype(BF16), v) + _dot(q, stb) * bq
            if n_other:
                o = rest[0][:, hh * dv:(hh + 1) * dv] + o
            o_ref[:, hh * dv:(hh + 1) * dv] = o
            if n_read:
                g = rest[n_other][:, hh * dv:(hh + 1) * dv].astype(F32)
                y = o * lax.rsqrt(jnp.mean(o * o, axis=-1, keepdims=True) + NORM_EPS)
                rest[n_other + n_read + 1][:, hh * dv:(hh + 1) * dv] = (_silu(g) * y).astype(BF16)
            kd = (k.astype(F32) * bk).astype(BF16)
            s_scr[hh] = s_scr[hh] * jnp.exp(lgh * C) + _dot(kd, v, _TN)

    vspec = pl.BlockSpec((C, hb * dv), lambda h, s: (chunk(s), h))
    gspec = pl.BlockSpec((C, hb * dv), lambda h, s: (chunk(s), H // hb + h))
    return pl.pallas_call(
        body, name=name, grid=(H // hb, nT),
        in_specs=[pl.BlockSpec(memory_space=pltpu.SMEM),
                  pl.BlockSpec((C, hb * dk), lambda h, s: (chunk(s), h)),
                  pl.BlockSpec((C, hb * dk), lambda h, s: (chunk(s), H // hb + h)), vspec]
        + [vspec] * n_other + [gspec] * n_read,
        out_specs=[vspec] + [vspec] * n_read + [pl.BlockSpec((hb, None, dk, dv), lambda h, s: (h, s, 0, 0))],
        out_shape=[jax.ShapeDtypeStruct((T, H * dv), F32)] + [jax.ShapeDtypeStruct((T, H * dv), BF16)] * n_read
        + [jax.ShapeDtypeStruct((H, nT, dk, dv), BF16)],
        scratch_shapes=[pltpu.VMEM((hb, dk, dv), F32)],
        compiler_params=_cparams("parallel", "arbitrary"),
    )(lg, qk, qk, vg, *([other] if n_other else []), *([vg] if n_read else []))


def _ret_bwd(qk, vg, do, states, lgs, *, T, L, H, name):
    dk, dv = 2 * LANES, 4 * LANES
    revs = (False, True)
    geo = [_ret_geometry(T, L, H, rev, True) for rev in revs]
    C, nT, hb = geo[0][:3]
    N_IN, N_OUT = 6, 4

    def body(*refs):
        ins = [refs[d * N_IN:(d + 1) * N_IN] for d in range(2)]
        outs = [refs[2 * N_IN + d * N_OUT:2 * N_IN + (d + 1) * N_OUT] for d in range(2)]
        ds_scr = refs[-1]
        hg, s = pl.program_id(0), pl.program_id(1)

        @pl.when(s == 0)
        def _():
            ds_scr[...] = jnp.zeros_like(ds_scr)
            for d in range(2):
                outs[d][3][...] = jnp.zeros_like(outs[d][3])

        for hh in range(hb):
            for d, rev in enumerate(revs):
                lg_ref, q_ref, k_ref, v_ref, do_ref, st_ref = ins[d]
                dq_ref, dk_ref, dv_ref, dlg_ref = outs[d]
                lgh = lg_ref[0, hg * hb + hh]
                amat, bq, bk, pos = _decays(lgh, rev)
                ksl, vsl = slice(hh * dk, (hh + 1) * dk), slice(hh * dv, (hh + 1) * dv)
                q, k, v, dob = q_ref[:, ksl], k_ref[:, ksl], v_ref[:, vsl], do_ref[:, vsl]
                stb = st_ref[hh]
                ds_new = ds_scr[d, hh]
                dsb = ds_new.astype(BF16)
                qf, kf = q.astype(F32), k.astype(F32)
                scores = (_dot(q, k, _NT) * amat).astype(BF16)
                dqk = (_dot(dob, v, _NT) * amat).astype(BF16)
                dq = _dot(dqk, k) + _dot(dob, stb, _NT) * bq
                dkk = _dot(dqk, q, _TN) + _dot(v, dsb, _NT) * bk
                kd = (kf * bk).astype(BF16)
                dvv = _dot(scores, dob, _TN) + _dot(kd, dsb)
                dod = (dob.astype(F32) * bq).astype(BF16)
                ds_prev = ds_new * jnp.exp(lgh * C) + _dot(q, dod, _TN)
                ds_scr[d, hh] = ds_prev
                dq_ref[:, ksl] = dq.astype(dq_ref.dtype)
                dk_ref[:, ksl] = dkk.astype(dk_ref.dtype)
                dv_ref[:, vsl] = dvv.astype(dv_ref.dtype)
                dlg = (jnp.sum(pos * jnp.sum(qf * dq - kf * dkk, axis=-1, keepdims=True))
                       + C * jnp.sum(ds_prev * stb.astype(F32)))
                dlg_ref[hh] += dlg

    in_specs, out_specs, operands = [], [], []
    for d in range(2):
        chunk, step = geo[d][3], geo[d][4]
        qspec = pl.BlockSpec((C, hb * dk), lambda h, s, chunk=chunk: (chunk(s), h))
        vspec = pl.BlockSpec((C, hb * dv), lambda h, s, chunk=chunk: (chunk(s), h))
        in_specs += [pl.BlockSpec(memory_space=pltpu.SMEM), qspec,
                     pl.BlockSpec((C, hb * dk), lambda h, s, chunk=chunk: (chunk(s), H // hb + h)), vspec, vspec,
                     pl.BlockSpec((hb, None, dk, dv), lambda h, s, step=step: (h, step(s), 0, 0))]
        out_specs += [qspec, qspec, vspec, pl.BlockSpec((hb, 8, LANES), lambda h, s: (h, 0, 0))]
        operands += [lgs[d], qk, qk, vg, do, states[d]]
    one_dir = [jax.ShapeDtypeStruct((T, H * dk), BF16), jax.ShapeDtypeStruct((T, H * dk), BF16),
               jax.ShapeDtypeStruct((T, H * dv), BF16), jax.ShapeDtypeStruct((H, 8, LANES), F32)]
    res = pl.pallas_call(
        body, name=name, grid=(H // hb, nT), in_specs=in_specs, out_specs=out_specs, out_shape=one_dir * 2,
        scratch_shapes=[pltpu.VMEM((2, hb, dk, dv), F32)],
        compiler_params=_cparams("parallel", "arbitrary"),
    )(*operands)
    return res[:N_OUT], res[N_OUT:]


def _readout_bwd(dz, o, vg, *, T, L, H, tr, name):
    dv = 4 * LANES
    W = H * dv

    def body(is_ctx, dzt, o, g):
        gf = g.astype(F32)
        sg = jax.nn.sigmoid(gf)
        dzf = dzt.astype(F32)
        dy = dzf * (gf * sg)
        ys, dos = [], []
        for h in range(H):
            sl = slice(h * dv, (h + 1) * dv)
            oh, dyh = o[:, sl], dy[:, sl]
            r = lax.rsqrt(jnp.mean(oh * oh, axis=-1, keepdims=True) + NORM_EPS)
            yh = oh * r
            ys.append(yh)
            dos.append(r * (dyh - yh * jnp.mean(dyh * yh, axis=-1, keepdims=True)))
        y = jnp.concatenate(ys, axis=1)
        dg = dzf * y * (sg * (1.0 + gf * (1.0 - sg)))
        return (jnp.concatenate(dos, axis=1), dg), ()

    return _rowwise(body, [(dz, W, 0, False), (o, W, 0, False), (vg, W, 1, False)], [],
                    [(W, BF16), (W, BF16)], 0, R=T, L=L, tr=tr, name=name)


def _ret_assemble(dq_f, dq_b, dk_f, dk_b, dv_f, dv_b, dg, cos, sin, *, T, L, H, tr, name):
    dk, dv = 2 * LANES, 4 * LANES
    Wq, Wv = H * dk, H * dv
    kscale = float(dk) ** -0.5

    def unrope(d, c, s_, scale):
        parts = []
        for h in range(H):
            d1, d2 = d[:, h * dk:h * dk + LANES], d[:, h * dk + LANES:(h + 1) * dk]
            parts += [(d1 * c + d2 * s_) * scale, (d2 * c - d1 * s_) * scale]
        return jnp.concatenate(parts, axis=1)

    def body(is_ctx, qf, qb, kf, kb, vf, vb, g, c, s_):
        add = lambda a, b: a.astype(F32) + b.astype(F32)
        dq = unrope(add(qf, qb), c, s_, 1.0)
        dkk = unrope(add(kf, kb), c, s_, kscale)
        return (jnp.concatenate([dq.astype(BF16), dkk.astype(BF16), add(vf, vb).astype(BF16), g], axis=1),), ()

    rows = [(dq_f, Wq, 0, False), (dq_b, Wq, 0, False), (dk_f, Wq, 0, False), (dk_b, Wq, 0, False),
            (dv_f, Wv, 0, False), (dv_b, Wv, 0, False), (dg, Wv, 0, False),
            (cos, LANES, 0, False), (sin, LANES, 0, False)]
    return _rowwise(body, rows, [], [(2 * Wq + 2 * Wv, BF16)], 0, R=T, L=L, tr=tr, name=name)[0]


def _swap32(x):
    n = x.shape[-1]
    lane = lax.broadcasted_iota(jnp.int32, x.shape, x.ndim - 1)
    return jnp.where(lane % 64 < 32, pltpu.roll(x, n - 32, x.ndim - 1), pltpu.roll(x, 32, x.ndim - 1))


ATT_Q_BLOCKS = 4


def _stack_heads_at(ref, rows, G):
    return jnp.concatenate([ref[rows, g * LANES:(g + 1) * LANES] for g in range(G)], axis=0)


def _stack_columns_at(ref, rows, G):
    return jnp.concatenate([ref[rows, g:g + 1] for g in range(G)], axis=0)


def _sink_column(sink_ref, h, G):
    return jnp.concatenate([jnp.full((CHUNK, 1), sink_ref[0, h * G + g], F32) for g in range(G)], axis=0)


def _key_mask(n, nb, CTX, G):
    W = 3 * CHUNK + CTX
    ii = lax.broadcasted_iota(jnp.int32, (G * CHUNK, W), 0) & (CHUNK - 1)
    col = lax.broadcasted_iota(jnp.int32, (G * CHUNK, W), 1)
    is_prev = col < CHUNK
    is_next = (col >= 2 * CHUNK) & (col < 3 * CHUNK)
    prev_ok = is_prev & (col >= ii) & (n > 0)
    next_ok = is_next & ((col - 2 * CHUNK) <= ii) & (n < nb - 1)
    return prev_ok | next_ok | jnp.logical_not(is_prev | is_next)


def _attn_geometry(L, CTX):
    nb = L // CHUNK
    QB = ATT_Q_BLOCKS if nb % ATT_Q_BLOCKS == 0 else 1

    def blk(j):
        return pl.BlockSpec((CHUNK, LANES), lambda h, m: (jnp.clip(m * QB + j - 1, 0, nb - 1), h))

    kvs = [blk(j) for j in range(QB + 2)] + [pl.BlockSpec((CTX, LANES), lambda h, m: (L // CTX, h))]
    return nb, QB, kvs


def _attn_fwd(q, k, v, sink, *, L, CTX, Hkv, G, name):
    scale = float(ATT_HEAD_DIM) ** -0.5
    nb, QB, kvs = _attn_geometry(L, CTX)
    nkv = QB + 3

    def body(sink_ref, q_ref, *rest):
        kb, vb, (o_ref, lse_ref) = rest[:nkv], rest[nkv:2 * nkv], rest[2 * nkv:]
        h, m_ = pl.program_id(0), pl.program_id(1)
        sk = _sink_column(sink_ref, h, G)
        for sub in range(QB):
            rows = slice(sub * CHUNK, (sub + 1) * CHUNK)
            qs = _stack_heads_at(q_ref, rows, G)
            kall = jnp.concatenate([kb[sub + j][...] for j in range(3)] + [kb[-1][...]], axis=0)
            vall = jnp.concatenate([vb[sub + j][...] for j in range(3)] + [vb[-1][...]], axis=0)
            s_ = jnp.where(_key_mask(m_ * QB + sub, nb, CTX, G), _dot(qs, kall, _NT) * scale, NEG_INF)
            m = jnp.maximum(jnp.max(s_, axis=-1, keepdims=True), sk)
            p = jnp.exp(s_ - m)
            den = jnp.sum(p, axis=-1, keepdims=True) + jnp.exp(sk - m)
            o = _dot(p.astype(BF16), vall) / den
            lse = m + jnp.log(den)
            for g in range(G):
                o_ref[rows, g * LANES:(g + 1) * LANES] = o[g * CHUNK:(g + 1) * CHUNK].astype(o_ref.dtype)
                lse_ref[rows, g:g + 1] = lse[g * CHUNK:(g + 1) * CHUNK]

    qspec = pl.BlockSpec((QB * CHUNK, G * LANES), lambda h, m: (m, h))
    return pl.pallas_call(
        body, name=name, grid=(Hkv, nb // QB),
        in_specs=[pl.BlockSpec(memory_space=pltpu.SMEM), qspec] + kvs + kvs,
        out_specs=[qspec, pl.BlockSpec((None, QB * CHUNK, G), lambda h, m: (h, m, 0))],
        out_shape=[jax.ShapeDtypeStruct((L, Hkv * G * LANES), BF16), jax.ShapeDtypeStruct((Hkv, L, G), F32)],
        compiler_params=_cparams("parallel", "parallel"),
    )(sink, q, *([k] * nkv), *([v] * nkv))


def _attn_bwd(q, k, v, do, lse, sink, *, L, CTX, Hkv, G, name):
    scale = float(ATT_HEAD_DIM) ** -0.5
    nb, QB, kvs = _attn_geometry(L, CTX)
    nkv = QB + 3
    qspec = pl.BlockSpec((QB * CHUNK, G * LANES), lambda h, m: (m, h))
    rowspec = pl.BlockSpec((None, QB * CHUNK, G), lambda h, m: (h, m, 0))
    colspec = lambda rows: pl.BlockSpec((rows, LANES), lambda h, m: (0, h))

    def body(sink_ref, q_ref, do_ref, lse_ref, *rest):
        kb, vb = rest[:nkv], rest[nkv:2 * nkv]
        dq_ref, dk_ref, dv_ref, dkx_ref, dvx_ref, dsk_ref = rest[2 * nkv:]
        h, m_ = pl.program_id(0), pl.program_id(1)

        @pl.when(m_ == 0)
        def _():
            for r in (dk_ref, dv_ref, dkx_ref, dvx_ref, dsk_ref):
                r[...] = jnp.zeros_like(r)

        sk = _sink_column(sink_ref, h, G)
        for sub in range(QB):
            n = m_ * QB + sub
            rows = slice(sub * CHUNK, (sub + 1) * CHUNK)
            qs, dos = _stack_heads_at(q_ref, rows, G), _stack_heads_at(do_ref, rows, G)
            kall = jnp.concatenate([kb[sub + j][...] for j in range(3)] + [kb[-1][...]], axis=0)
            vall = jnp.concatenate([vb[sub + j][...] for j in range(3)] + [vb[-1][...]], axis=0)
            lse_c = _stack_columns_at(lse_ref, rows, G)
            p = jnp.where(_key_mask(n, nb, CTX, G), jnp.exp(_dot(qs, kall, _NT) * scale - lse_c), 0.0)
            dp = _dot(dos, vall, _NT)
            delta = jnp.sum(p * dp, axis=-1, keepdims=True)
            ds = (p * (dp - delta) * scale).astype(BF16)
            dq = _dot(ds, kall)
            dk_all = _dot(ds, qs, _TN)
            dv_all = _dot(p.astype(BF16), dos, _TN)
            for g in range(G):
                dq_ref[rows, g * LANES:(g + 1) * LANES] = dq[g * CHUNK:(g + 1) * CHUNK]
            for part, blk in enumerate((jnp.maximum(n - 1, 0), n, jnp.minimum(n + 1, nb - 1))):
                krows = pl.ds(pl.multiple_of(blk * CHUNK, CHUNK), CHUNK)
                dk_ref[krows, :] += dk_all[part * CHUNK:(part + 1) * CHUNK]
                dv_ref[krows, :] += dv_all[part * CHUNK:(part + 1) * CHUNK]
            dkx_ref[...] += dk_all[3 * CHUNK:]
            dvx_ref[...] += dv_all[3 * CHUNK:]
            dsink = -jnp.exp(sk - lse_c) * delta
            for g in range(G):
                dsk_ref[g:g + 1, :] += jnp.sum(dsink[g * CHUNK:(g + 1) * CHUNK])

    return pl.pallas_call(
        body, name=name, grid=(Hkv, nb // QB),
        in_specs=[pl.BlockSpec(memory_space=pltpu.SMEM), qspec, qspec, rowspec] + kvs + kvs,
        out_specs=[qspec, colspec(L), colspec(L), colspec(CTX), colspec(CTX),
                   pl.BlockSpec((None, 8, LANES), lambda h, m: (h, 0, 0))],
        out_shape=[jax.ShapeDtypeStruct((L, Hkv * G * LANES), F32),
                   jax.ShapeDtypeStruct((L, Hkv * LANES), F32), jax.ShapeDtypeStruct((L, Hkv * LANES), F32),
                   jax.ShapeDtypeStruct((CTX, Hkv * LANES), F32), jax.ShapeDtypeStruct((CTX, Hkv * LANES), F32),
                   jax.ShapeDtypeStruct((Hkv, 8, LANES), F32)],
        compiler_params=_cparams("parallel", "arbitrary"),
    )(sink, q, do, lse, *([k] * nkv), *([v] * nkv))


def _attn_assemble(dq, dk_lat, dv_lat, dk_ctx, dv_ctx, cos, sin, *, T, L, CTX, Hq, Hkv, tr, name):
    Wq, Wk = Hq * LANES, Hkv * LANES
    ctx_blocks = CTX // tr
    nl = L // tr

    def unrope(d, c, s_, heads):
        return d * jnp.tile(c, (1, heads)) + _swap32(d * jnp.tile(s_, (1, heads)))

    def body(is_ctx, dqt, dkl, dvl, dkc, dvc, c, s_):
        dq_ = jnp.where(is_ctx, 0.0, unrope(dqt, c, s_, Hq))
        dk_ = unrope(jnp.where(is_ctx, dkc, dkl), c, s_, Hkv)
        dv_ = jnp.where(is_ctx, dvc, dvl)
        return (jnp.concatenate([dq_, dk_, dv_], axis=1),), ()

    def ctx_map(i):
        return (jnp.clip(i - nl, 0, ctx_blocks - 1), 0)

    assert T % tr == 0 and L % tr == 0 and CTX % tr == 0
    in_specs = [pl.BlockSpec((tr, Wq), lambda i: (jnp.minimum(i, nl - 1), 0)),
                pl.BlockSpec((tr, Wk), lambda i: (jnp.minimum(i, nl - 1), 0)),
                pl.BlockSpec((tr, Wk), lambda i: (jnp.minimum(i, nl - 1), 0)),
                pl.BlockSpec((tr, Wk), ctx_map), pl.BlockSpec((tr, Wk), ctx_map),
                pl.BlockSpec((tr, LANES), lambda i: (i, 0)), pl.BlockSpec((tr, LANES), lambda i: (i, 0))]

    def kern(dq_r, dkl_r, dvl_r, dkc_r, dvc_r, c_r, s_r, o_ref):
        is_ctx = pl.program_id(0) >= nl
        (out,), _ = body(is_ctx, dq_r[...], dkl_r[...], dvl_r[...], dkc_r[...], dvc_r[...], c_r[...], s_r[...])
        o_ref[...] = out.astype(o_ref.dtype)

    return pl.pallas_call(
        kern, name=name, grid=(T // tr,), in_specs=in_specs,
        out_specs=pl.BlockSpec((tr, Wq + 2 * Wk), lambda i: (i, 0)),
        out_shape=jax.ShapeDtypeStruct((T, Wq + 2 * Wk), BF16),
        compiler_params=_cparams("parallel"),
    )(dq, dk_lat, dv_lat, dk_ctx, dv_ctx, cos, sin)


def _my_place():
    x, y, c = lax.axis_index("x"), lax.axis_index("y"), lax.axis_index("c")
    return x, y, c


def _all_gather_small(v, *, name, dep=None):
    R, C = v.shape
    n_dep = 0 if dep is None else 1

    def body(x_ref, *rest):
        out_ref, send_sems, recv_sems, local_sem = rest[n_dep:]
        x, y, c = _my_place()
        me, sibling = (x, y, c), (x, y, 1 - c)
        chips = [(1 - x, y), (x, 1 - y), (1 - x, 1 - y)]

        def slot(px, py, pc):
            return out_ref.at[4 * px + 2 * py + pc]

        def copy(k, block, to, src=None):
            return pltpu.make_async_remote_copy(
                src_ref=slot(*block) if src is None else src, dst_ref=slot(*block),
                send_sem=send_sems.at[k], recv_sem=recv_sems.at[k], device_id=to, device_id_type=MESH)

        mine = pltpu.make_async_copy(x_ref, slot(*me), local_sem)
        mine.start()
        first = [copy(0, me, sibling, src=x_ref)]
        first += [copy(1 + j, me, (*chip, c), src=x_ref) for j, chip in enumerate(chips)]
        for cp in first:
            cp.start()
        passed = [copy(4 + j, (*chip, c), sibling) for j, chip in enumerate(chips)]
        for j, chip in enumerate(chips):
            copy(1 + j, (*chip, c), me).wait_recv()
            passed[j].start()
        copy(0, sibling, me).wait_recv()
        for j, chip in enumerate(chips):
            copy(4 + j, (*chip, 1 - c), me).wait_recv()
        for cp in first + passed:
            cp.wait_send()
        mine.wait()

    return pl.pallas_call(
        body, name=name, out_shape=jax.ShapeDtypeStruct((N_DEV, R, C), v.dtype),
        in_specs=[pl.BlockSpec(memory_space=pltpu.VMEM)] + [pl.BlockSpec(memory_space=pl.ANY)] * n_dep,
        out_specs=pl.BlockSpec(memory_space=pltpu.VMEM),
        scratch_shapes=[pltpu.SemaphoreType.DMA((7,)), pltpu.SemaphoreType.DMA((7,)), pltpu.SemaphoreType.DMA],
    )(v, *([dep] if n_dep else []))


_HBM_SPEC = pl.BlockSpec(memory_space=pltpu.HBM)
_SEM_SPEC = pl.BlockSpec(memory_space=pltpu.SEMAPHORE)
_ANY_SPEC = pl.BlockSpec(memory_space=pl.ANY)
_DATAFLOW = pltpu.SideEffectType.DATAFLOW_SIDE_EFFECTING
N_PEERS = N_DEV - 1


def _peer(r):
    x, y, c = _my_place()
    return ((1 - x) if r & 4 else x, (1 - y) if r & 2 else y, (1 - c) if r & 1 else c)


def _index_of(place):
    return 4 * place[0] + 2 * place[1] + place[2]


def _slot(ref, axis, idx, size):
    if axis == 0:
        return ref.at[pl.ds(idx * size, size), :]
    return ref.at[:, pl.ds(idx * size, size)]


def _cast_place(w3, layer, axis, me_arr, dep, *, name):
    Ks, Ns = w3.shape[1], w3.shape[2]
    tr = _pick(Ks, 256, 16)
    nblk = Ks // tr
    full = (Ks * N_DEV, Ns) if axis == 0 else (Ks, Ns * N_DEV)
    if axis == 0:
        out_map = lambda i, me: (me[0] * nblk + i, 0)
    else:
        out_map = lambda i, me: (i, me[0])

    def body(me_ref, w_ref, dep_ref, o_ref):
        o_ref[...] = w_ref[...].astype(BF16)

    return pl.pallas_call(
        body, name=name, out_shape=jax.ShapeDtypeStruct(full, BF16),
        grid_spec=pltpu.PrefetchScalarGridSpec(
            num_scalar_prefetch=1, grid=(nblk,),
            in_specs=[pl.BlockSpec((None, tr, Ns), lambda i, me: (layer, i, 0)), pl.BlockSpec(memory_space=pl.ANY)],
            out_specs=pl.BlockSpec((tr, Ns), out_map)),
        compiler_params=_cparams("parallel"),
    )(me_arr, w3, dep)


AG_FIRST = 4
AG_CHIPS = 3


def _sibling():
    x, y, c = _my_place()
    return (x, y, 1 - c)


def _chip_peer(j, same_core=True):
    x, y, c = _my_place()
    px = (1 - x) if j in (0, 2) else x
    py = (1 - y) if j in (1, 2) else y
    return (px, py, c if same_core else 1 - c)


def _gather_start(lands, axes, after, *, name):
    nt = len(lands)
    sizes = [l.shape[a] // N_DEV for l, a in zip(lands, axes)]

    def body(*refs):
        ins, send_sems, recv_sems, token = refs[:nt], refs[nt + 1], refs[nt + 2], refs[-1]
        my_idx = _index_of(_my_place())
        for t in range(nt):
            mine = _slot(ins[t], axes[t], my_idx, sizes[t])
            for k, to in enumerate([_sibling()] + [_chip_peer(j) for j in range(AG_CHIPS)]):
                pltpu.make_async_remote_copy(src_ref=mine, dst_ref=mine, send_sem=send_sems.at[t * AG_FIRST + k],
                                             recv_sem=recv_sems.at[t * AG_FIRST + k], device_id=to,
                                             device_id_type=MESH).start()
        token[...] = jnp.zeros_like(token)

    res = pl.pallas_call(
        body, name=name,
        out_shape=(pltpu.SemaphoreType.DMA((nt * AG_FIRST,)), pltpu.SemaphoreType.DMA((nt * AG_FIRST,)),
                   *[pltpu.HBM(l.shape, l.dtype) for l in lands], jax.ShapeDtypeStruct((8, LANES), F32)),
        in_specs=[_HBM_SPEC] * nt + [_ANY_SPEC],
        out_specs=(_SEM_SPEC, _SEM_SPEC, *[_HBM_SPEC] * nt, pl.BlockSpec(memory_space=pltpu.VMEM)),
        input_output_aliases={t: 2 + t for t in range(nt)},
        compiler_params=pltpu.CompilerParams(has_side_effects=_DATAFLOW),
    )(*[pltpu.with_memory_space_constraint(l, pltpu.HBM) for l in lands], after)
    return res[0], res[1], list(res[2:2 + nt]), res[-1]


def _gather_forward(send_a, recv_a, lands, axes, after, *, name):
    nt = len(lands)
    sizes = [l.shape[a] // N_DEV for l, a in zip(lands, axes)]

    def body(*refs):
        ins, send_a, recv_a = refs[:nt], refs[nt], refs[nt + 1]
        send_f, recv_f, token = refs[nt + 3], refs[nt + 4], refs[-1]
        my_idx = _index_of(_my_place())
        for t in range(nt):
            for j in range(AG_CHIPS):
                src_dev = _chip_peer(j)
                arrived = _slot(ins[t], axes[t], _index_of(src_dev), sizes[t])
                pltpu.make_async_remote_copy(
                    src_ref=_slot(ins[t], axes[t], my_idx, sizes[t]), dst_ref=arrived,
                    send_sem=send_a.at[t * AG_FIRST + 1 + j], recv_sem=recv_a.at[t * AG_FIRST + 1 + j],
                    device_id=src_dev, device_id_type=MESH).wait_recv()
                pltpu.make_async_remote_copy(src_ref=arrived, dst_ref=arrived, send_sem=send_f.at[t * AG_CHIPS + j],
                                             recv_sem=recv_f.at[t * AG_CHIPS + j], device_id=_sibling(),
                                             device_id_type=MESH).start()
        token[...] = jnp.zeros_like(token)

    res = pl.pallas_call(
        body, name=name,
        out_shape=(pltpu.SemaphoreType.DMA((nt * AG_CHIPS,)), pltpu.SemaphoreType.DMA((nt * AG_CHIPS,)),
                   *[pltpu.HBM(l.shape, l.dtype) for l in lands], jax.ShapeDtypeStruct((8, LANES), F32)),
        in_specs=[_HBM_SPEC] * nt + [_SEM_SPEC, _SEM_SPEC, _ANY_SPEC],
        out_specs=(_SEM_SPEC, _SEM_SPEC, *[_HBM_SPEC] * nt, pl.BlockSpec(memory_space=pltpu.VMEM)),
        input_output_aliases={t: 2 + t for t in range(nt)},
        compiler_params=pltpu.CompilerParams(has_side_effects=_DATAFLOW),
    )(*lands, send_a, recv_a, after)
    return res[0], res[1], list(res[2:2 + nt]), res[-1]


def _gather_wait(send_a, recv_a, send_f, recv_f, lands, axes, after, *, name):
    nt = len(lands)
    sizes = [l.shape[a] // N_DEV for l, a in zip(lands, axes)]

    def body(*refs):
        ins, send_a, recv_a, send_f, recv_f = refs[:nt], refs[nt], refs[nt + 1], refs[nt + 2], refs[nt + 3]
        my_idx = _index_of(_my_place())
        sib = _sibling()
        for t in range(nt):
            mine = _slot(ins[t], axes[t], my_idx, sizes[t])
            for k, to in enumerate([sib] + [_chip_peer(j) for j in range(AG_CHIPS)]):
                pltpu.make_async_remote_copy(src_ref=mine, dst_ref=mine, send_sem=send_a.at[t * AG_FIRST + k],
                                             recv_sem=recv_a.at[t * AG_FIRST + k], device_id=to,
                                             device_id_type=MESH).wait_send()
            pltpu.make_async_remote_copy(src_ref=mine, dst_ref=_slot(ins[t], axes[t], _index_of(sib), sizes[t]),
                                         send_sem=send_a.at[t * AG_FIRST], recv_sem=recv_a.at[t * AG_FIRST],
                                         device_id=sib, device_id_type=MESH).wait_recv()
            for j in range(AG_CHIPS):
                sent = _slot(ins[t], axes[t], _index_of(_chip_peer(j)), sizes[t])
                got = _slot(ins[t], axes[t], _index_of(_chip_peer(j, same_core=False)), sizes[t])
                cp = pltpu.make_async_remote_copy(src_ref=sent, dst_ref=got, send_sem=send_f.at[t * AG_CHIPS + j],
                                                  recv_sem=recv_f.at[t * AG_CHIPS + j], device_id=sib,
                                                  device_id_type=MESH)
                cp.wait_send()
                cp.wait_recv()

    res = pl.pallas_call(
        body, name=name, out_shape=[pltpu.HBM(l.shape, l.dtype) for l in lands],
        in_specs=[_HBM_SPEC] * nt + [_SEM_SPEC] * 4 + [_ANY_SPEC], out_specs=[_HBM_SPEC] * nt,
        input_output_aliases={t: t for t in range(nt)},
        compiler_params=pltpu.CompilerParams(has_side_effects=_DATAFLOW),
    )(*lands, send_a, recv_a, send_f, recv_f, after)
    return list(res)


def _scatter_start(dw, axis, *, name):
    size = dw.shape[axis] // N_DEV
    land_shape = (N_PEERS, size, dw.shape[1]) if axis == 0 else (N_PEERS, dw.shape[0], size)

    def body(dw_ref, land_ref, send_sems, recv_sems, dw_thru, land_thru, token):
        for r in range(1, N_DEV):
            p = _peer(r)
            pltpu.make_async_remote_copy(src_ref=_slot(dw_ref, axis, _index_of(p), size), dst_ref=land_ref.at[r - 1],
                                         send_sem=send_sems.at[r - 1], recv_sem=recv_sems.at[r - 1], device_id=p,
                                         device_id_type=MESH).start()
        token[...] = jnp.zeros_like(token)

    land = pltpu.with_memory_space_constraint(lax.empty(land_shape, dw.dtype), pltpu.HBM)
    return pl.pallas_call(
        body, name=name,
        out_shape=(pltpu.SemaphoreType.DMA((N_PEERS,)), pltpu.SemaphoreType.DMA((N_PEERS,)),
                   pltpu.HBM(dw.shape, dw.dtype), pltpu.HBM(land_shape, dw.dtype), jax.ShapeDtypeStruct((8, LANES), F32)),
        in_specs=[_HBM_SPEC, _HBM_SPEC],
        out_specs=(_SEM_SPEC, _SEM_SPEC, _HBM_SPEC, _HBM_SPEC, pl.BlockSpec(memory_space=pltpu.VMEM)),
        input_output_aliases={0: 2, 1: 3},
        compiler_params=pltpu.CompilerParams(has_side_effects=_DATAFLOW),
    )(pltpu.with_memory_space_constraint(dw, pltpu.HBM), land)


def _scatter_wait(send_sems, recv_sems, dw, land, axis, after, *, name):
    size = dw.shape[axis] // N_DEV

    def body(dw_ref, land_ref, send_sems, recv_sems, after_ref, dw_thru, land_thru):
        for r in range(1, N_DEV):
            p = _peer(r)
            cp = pltpu.make_async_remote_copy(src_ref=_slot(dw_ref, axis, _index_of(p), size), dst_ref=land_ref.at[r - 1],
                                              send_sem=send_sems.at[r - 1], recv_sem=recv_sems.at[r - 1], device_id=p,
                                              device_id_type=MESH)
            cp.wait_send()
            cp.wait_recv()

    return pl.pallas_call(
        body, name=name, out_shape=(pltpu.HBM(dw.shape, dw.dtype), pltpu.HBM(land.shape, land.dtype)),
        in_specs=[_HBM_SPEC, _HBM_SPEC, _SEM_SPEC, _SEM_SPEC, _ANY_SPEC], out_specs=(_HBM_SPEC, _HBM_SPEC),
        input_output_aliases={0: 0, 1: 1},
        compiler_params=pltpu.CompilerParams(has_side_effects=_DATAFLOW),
    )(dw, land, send_sems, recv_sems, after)


def _adamw_math(w, g, m, v):
    m = ADAM_B1 * m + (1.0 - ADAM_B1) * g
    v = ADAM_B2 * v + (1.0 - ADAM_B2) * (g * g)
    m_hat = m / (1.0 - ADAM_B1 ** ADAM_STEP)
    v_hat = v / (1.0 - ADAM_B2 ** ADAM_STEP)
    delta = -ADAM_LR * (m_hat / (jnp.sqrt(v_hat) + ADAM_EPS) + ADAM_WD * w)
    return delta, m, v


def _adamw_sharded(w, m, v, layer, dw, land, axis, me_arr, prev, *, name):
    nl, Ks, Ns = w.shape
    tr = _pick(Ks, 128, 16)
    nblk = Ks // tr
    if axis == 0:
        own_map = lambda i, me: (me[0] * nblk + i, 0)
    else:
        own_map = lambda i, me: (i, me[0])
    wspec = pl.BlockSpec((None, tr, Ns), lambda i, me: (layer, i, 0))
    n_prev = 0 if prev is None else 4

    def body(me_ref, w_ref, m_ref, v_ref, own_ref, r_ref, *rest):
        g_ref, d_ref, nm_ref, nv_ref = rest[n_prev:]
        g = own_ref[...].astype(F32)
        for r in range(N_PEERS):
            g = g + r_ref[r].astype(F32)
        delta, nm, nv = _adamw_math(w_ref[...], g, m_ref[...], v_ref[...])
        g_ref[...], d_ref[...], nm_ref[...], nv_ref[...] = g, delta, nm, nv

    return pl.pallas_call(
        body, name=name, out_shape=[jax.ShapeDtypeStruct((nl, Ks, Ns), F32)] * 4,
        grid_spec=pltpu.PrefetchScalarGridSpec(
            num_scalar_prefetch=1, grid=(nblk,),
            in_specs=[wspec, wspec, wspec, pl.BlockSpec((tr, Ns), own_map),
                      pl.BlockSpec((N_PEERS, tr, Ns), lambda i, me: (0, i, 0))] + [_ANY_SPEC] * n_prev,
            out_specs=[wspec] * 4),
        input_output_aliases={6 + k: k for k in range(n_prev)},
        compiler_params=_cparams("parallel"),
    )(me_arr, w, m, v, dw, land, *(prev or []))


def _adamw_flat(w, g, m, v, *, name):
    def body(w_ref, g_ref, m_ref, v_ref, d_ref, nm_ref, nv_ref):
        d_ref[...], nm_ref[...], nv_ref[...] = _adamw_math(w_ref[...], g_ref[...], m_ref[...], v_ref[...])

    spec = pl.BlockSpec(memory_space=pltpu.VMEM)
    return pl.pallas_call(body, name=name, in_specs=[spec] * 4, out_specs=[spec] * 3,
                          out_shape=[jax.ShapeDtypeStruct(w.shape, F32)] * 3)(w, g, m, v)


def _sum_devices(a, *, name):
    def body(a_ref, o_ref):
        s = a_ref[0]
        for d in range(1, N_DEV):
            s = s + a_ref[d]
        o_ref[...] = s

    spec = pl.BlockSpec(memory_space=pltpu.VMEM)
    return pl.pallas_call(body, name=name, in_specs=[spec], out_specs=spec,
                          out_shape=jax.ShapeDtypeStruct(a.shape[1:], F32))(a)


def _ada_mods(c16, ada_w, ada_b_cols, *, name):
    nl, D, cols = ada_w.shape
    bn = _pick(cols, 512)

    def body(c_ref, w_ref, b_ref, o_ref):
        cond = _silu(c_ref[...]).astype(BF16)
        o_ref[...] = _dot(cond, w_ref[...].astype(BF16)) + b_ref[...]

    return pl.pallas_call(
        body, name=name, grid=(nl, cols // bn),
        in_specs=[pl.BlockSpec((16, D), lambda l, j: (0, 0)), pl.BlockSpec((None, D, bn), lambda l, j: (l, 0, j)),
                  pl.BlockSpec((None, 1, bn), lambda l, j: (l, 0, j))],
        out_specs=pl.BlockSpec((None, 16, bn), lambda l, j: (l, 0, j)),
        out_shape=jax.ShapeDtypeStruct((nl, 16, cols), F32),
        compiler_params=_cparams("parallel", "parallel"),
    )(c16, ada_w, ada_b_cols)


def _ada_bwd(cond_t, dmod, w, m, v, *, name):
    nl, D, cols = w.shape
    tr = _pick(D, 256, 8)

    def body(ct_ref, dm_ref, w_ref, m_ref, v_ref, g_ref, d_ref, nm_ref, nv_ref, dc_ref):
        ct, dm, wt = ct_ref[...], dm_ref[...], w_ref[...]
        g = ct[:, 0:1] * dm[0:1, :]
        for r in range(1, N_DEV + 1):
            g = g + ct[:, r:r + 1] * dm[r:r + 1, :]
        delta, nm, nv = _adamw_math(wt, g, m_ref[...], v_ref[...])
        g_ref[...], d_ref[...], nm_ref[...], nv_ref[...] = g, delta, nm, nv
        dc_ref[...] = jnp.sum(wt * dm[N_DEV:N_DEV + 1, :], axis=-1, keepdims=True)

    wspec = pl.BlockSpec((None, tr, cols), lambda l, i: (l, i, 0))
    return pl.pallas_call(
        body, name=name, grid=(nl, D // tr),
        in_specs=[pl.BlockSpec((tr, 16), lambda l, i: (i, 0)), pl.BlockSpec((None, 16, cols), lambda l, i: (l, 0, 0)),
                  wspec, wspec, wspec],
        out_specs=[wspec] * 4 + [pl.BlockSpec((None, tr, 1), lambda l, i: (l, i, 0))],
        out_shape=[jax.ShapeDtypeStruct((nl, D, cols), F32)] * 4 + [jax.ShapeDtypeStruct((nl, D, 1), F32)],
        compiler_params=_cparams("parallel", "parallel"),
    )(cond_t, dmod, w, m, v)


def _rope_tables(L, CTX):
    def angles(pos, dim):
        inv_freq = ROPE_BASE ** (-jnp.arange(0, dim, 2, dtype=F32) / dim)
        return pos.astype(F32)[:, None] * inv_freq[None, :]

    def pad(cos, sin):
        return (jnp.concatenate([cos, jnp.ones((CTX, LANES), F32)], 0),
                jnp.concatenate([sin, jnp.zeros((CTX, LANES), F32)], 0))

    ret = angles(jnp.arange(L), 2 * LANES)
    ret_cs = pad(jnp.cos(ret), jnp.sin(ret))
    rows = angles(jnp.arange(L) // GRID_W, ATT_HEAD_DIM // 2)
    cols = angles(jnp.arange(L) % GRID_W, ATT_HEAD_DIM // 2)
    cos = jnp.concatenate([jnp.cos(rows)] * 2 + [jnp.cos(cols)] * 2, axis=1)
    sin = jnp.concatenate([-jnp.sin(rows), jnp.sin(rows), -jnp.sin(cols), jnp.sin(cols)], axis=1)
    return ret_cs, pad(cos, sin)


def kernel(x, c, ctx, c_ctx, ada_w, ada_b, norm_mix_g, norm_mlp_g, mlp_w1, mlp_w2, ret_w_in, ret_w_out, ret_decay_fwd, ret_decay_bwd, attn_w_in, attn_w_out, attn_sink, final_norm_g, loss_target, m_c_ctx, m_ada_w, m_ada_b, m_norm_mix_g, m_norm_mlp_g, m_mlp_w1, m_mlp_w2, m_ret_w_in, m_ret_w_out, m_ret_decay_fwd, m_ret_decay_bwd, m_attn_w_in, m_attn_w_out, m_attn_sink, m_final_norm_g, v_c_ctx, v_ada_w, v_ada_b, v_norm_mix_g, v_norm_mlp_g, v_mlp_w1, v_mlp_w2, v_ret_w_in, v_ret_w_out, v_ret_decay_fwd, v_ret_decay_bwd, v_attn_w_in, v_attn_w_out, v_attn_sink, v_final_norm_g):
    L, D = x.shape[1], x.shape[2]
    CTX = ctx.shape[1]
    T = L + CTX
    RH = ret_decay_fwd.shape[-1]
    assert D == RH * 2 * LANES and ada_w.shape[0] == 2 and ret_w_in.shape[0] == 1 and attn_w_in.shape[0] == 1
    Hq = attn_sink.shape[-1]
    Hkv = (attn_w_in.shape[-1] * N_DEV // ATT_HEAD_DIM - Hq) // 2
    G = Hq // Hkv
    FF = mlp_w1.shape[-1] * N_DEV
    Wq_r, Wv_r = RH * 2 * LANES, RH * 4 * LANES
    acols = ada_w.shape[-1]
    tr = _pick(CTX, 256, 8)
    tr_wide = _pick(CTX, 128, 8)
    bmT = T // 4 if (T % 64 == 0) else T
    bmL = L // 4 if (L % 64 == 0) else L
    x_idx, y_idx, c_idx = lax.axis_index("x"), lax.axis_index("y"), lax.axis_index("c")
    me = 4 * x_idx + 2 * y_idx + c_idx
    me_arr = jnp.reshape(me, (1,)).astype(jnp.int32)

    (rcos, rsin), (acos, asin) = _rope_tables(L, CTX)
    lg_f = jax.nn.log_sigmoid(ret_decay_fwd.astype(F32))
    lg_b = jax.nn.log_sigmoid(ret_decay_bwd.astype(F32))

    c_pad = jnp.concatenate([c.astype(F32), jnp.zeros((7, D), F32)], 0)
    c_all = _all_gather_small(c_pad, name="ag_c")[:, 0, :]
    c16 = jnp.concatenate([c_all, c_ctx[None, :], jnp.zeros((7, D), F32)], 0)
    ada_b_cols = lax.dynamic_slice_in_dim(ada_b, me * acols, acols, axis=1)[:, None, :]
    mods_shard = _ada_mods(c16, ada_w, ada_b_cols, name="ada_mods")
    mods_all = _all_gather_small(mods_shard.reshape(32, acols), name="ag_mods")

    wdefs = {"ret_in": (ret_w_in, 0, 1), "ret_out": (ret_w_out, 0, 0), "w1_0": (mlp_w1, 0, 1), "w2_0": (mlp_w2, 0, 0),
             "attn_in": (attn_w_in, 0, 1), "attn_out": (attn_w_out, 0, 0), "w1_1": (mlp_w1, 1, 1), "w2_1": (mlp_w2, 1, 0)}
    groups = [["ret_in"], ["ret_out"], ["w1_0", "w2_0"], ["attn_in", "attn_out"], ["w1_1", "w2_1"]]

    placed = {}

    def ag_start(gi, after):
        g_axes = [wdefs[k][2] for k in groups[gi]]
        ssem, rsem, lands, tok_ = _gather_start([placed[k] for k in groups[gi]], g_axes, after, name=f"ag_start{gi}")
        return dict(a=(ssem, rsem), lands=lands, axes=g_axes, gi=gi), tok_

    def ag_forward(g, after):
        fs, fr, g["lands"], tok_ = _gather_forward(*g["a"], g["lands"], g["axes"], after, name=f"ag_forward{g['gi']}")
        g["f"] = (fs, fr)
        return tok_

    def ag_wait(g, after):
        return _gather_wait(*g["a"], *g["f"], g["lands"], g["axes"], after, name=f"ag_wait{g['gi']}")

    placed["ret_in"] = _cast_place(*wdefs["ret_in"], me_arr, mods_all, name="place_ret_in")
    g0, tok = ag_start(0, mods_all)
    last_cast = tok
    for keys in groups[1:]:
        for k in keys:
            last_cast = placed[k] = _cast_place(*wdefs[k], me_arr, last_cast, name=f"place_{k}")
    mods_all = (mods_all + tok[0, 0]).reshape(N_DEV, 2, 16, acols).transpose(1, 2, 0, 3).reshape(2, 16, 6, D)
    mod_lat = lax.dynamic_index_in_dim(mods_all, me, axis=1, keepdims=False)
    mod_ctx = mods_all[:, N_DEV]

    def pack(i, ks, kc):
        return jnp.stack([mod_lat[i, ks], mod_lat[i, kc], mod_ctx[i, ks], mod_ctx[i, kc]], 0)

    def gates(i, k):
        return jnp.stack([mod_lat[i, k], mod_ctx[i, k]], 0)

    def gate_epilogue(x_rows_lat_only):
        def epi(acc, i, j, xt, gv):
            if x_rows_lat_only:
                gate = gv[0:1, :]
            else:
                row = i * acc.shape[0] + lax.broadcasted_iota(jnp.int32, (acc.shape[0], 1), 0)
                gate = jnp.where(row >= L, gv[1:2, :], gv[0:1, :])
            return xt + gate * acc, acc
        return epi

    w1, w2 = {}, {}

    mmT = dict(M=T, bm=bmT)
    mmL = dict(M=L, bm=bmL)

    def bn_of(n, off=0):
        b = MM_BN
        while n % b or off % b:
            b -= LANES
        return b

    X0 = jnp.concatenate([x[0], ctx[0]], axis=0)
    g_mix0, g_mlp0 = norm_mix_g[0:1], norm_mlp_g[0:1]
    g_mix1, g_mlp1 = norm_mix_g[1:2], norm_mlp_g[1:2]
    a0 = _normmod(X0, g_mix0, pack(0, 0, 1), R=T, L=L, tr=tr, name="normmod_mix0",
                  dep=(last_cast, rcos, rsin, acos, asin))
    tok = ag_forward(g0, a0)
    (wr_in,) = ag_wait(g0, tok)
    g1, tok = ag_start(1, wr_in)
    g1b, tok = ag_start(2, tok)

    bn_qk = _pick(Wq_r, MM_BN, 2 * LANES)
    nq_blocks = Wq_r // bn_qk
    kscale = float(2 * LANES) ** -0.5

    def rope_epi(acc, i, j, cos, sin):
        parts = []
        for h in range(acc.shape[1] // (2 * LANES)):
            x1 = acc[:, h * 2 * LANES:h * 2 * LANES + LANES]
            x2 = acc[:, h * 2 * LANES + LANES:(h + 1) * 2 * LANES]
            parts += [x1 * cos - x2 * sin, x2 * cos + x1 * sin]
        return (jnp.concatenate(parts, axis=1) * jnp.where(j < nq_blocks, 1.0, kscale),)

    def row_tile(arr, bm):
        return (arr, (bm, LANES), lambda i, j: (i, 0))

    (qk0,) = _mm(a0, wr_in, "nn", [BF16], N=2 * Wq_r, K=D, bn=bn_qk, bk=D, name="ret_qk", epilogue=rope_epi,
                 extras=[row_tile(rcos, bmT), row_tile(rsin, bmT)], dep=tok, **mmT)
    bn_vg = bn_of(2 * Wv_r, 2 * Wq_r)
    (vg0,) = _mm(a0, wr_in, "nn", [BF16], N=2 * Wv_r, K=D, bn=bn_vg, bk=D, name="ret_vg", b_col0=2 * Wq_r, dep=tok,
                 **mmT)

    tok = ag_forward(g1, vg0)
    of, st_f = _ret_fwd(qk0, vg0, lg_f + tok[0, 0], None, T=T, L=L, H=RH, rev=False, name="ret_scan_f")
    (wr_out,) = ag_wait(g1, of)
    o0, z0, st_b = _ret_fwd(qk0, vg0, lg_b, of, T=T, L=L, H=RH, rev=True, name="ret_scan_b", readout=True)
    tok = ag_forward(g1b, o0)

    bnD = _pick(D, MM_BN)

    def xtile(arr, bm):
        return (arr, (bm, bnD), lambda i, j: (i, j))

    def gtile(gv):
        return (gv, (2, bnD), lambda i, j: (0, j))

    bk_max = 2048

    def quarter(rows, div=4):
        return dict(M=rows["M"], bm=rows["bm"] // div if rows["bm"] % (16 * div) == 0 else rows["bm"])
    X1, ro0 = _mm(z0, wr_out, "nn", [F32, BF16], N=D, K=Wv_r, bn=bnD, bk=Wv_r, name="ret_out", cols_outer=True,
                  epilogue=gate_epilogue(False), extras=[xtile(X0, quarter(mmT, 2)["bm"]), gtile(gates(0, 2))], dep=tok,
                  **quarter(mmT, 2))

    def mlp_fwd(Xin, i, g_mlp, rows, name, weights_after_norm=None):
        a = _normmod(Xin, g_mlp, pack(i, 3, 4), R=rows["M"], L=L, tr=tr, name=f"normmod_mlp{name}")
        dep_up = None if weights_after_norm is None else weights_after_norm(a)

        def relu2(acc, i_, j_):
            u = jnp.maximum(acc, 0.0)
            return u, u * u

        bnF = _pick(FF, MM_BN)
        u, r = _mm(a, w1[i], "nn", [BF16, BF16], N=FF, K=D, bn=bnF, bk=D, name=f"mlp_up{name}", epilogue=relu2,
                   dep=dep_up, **rows)
        Xout, mo = _mm(r, w2[i], "nn", [F32, BF16], N=D, K=FF, bn=bnD, bk=FF, name=f"mlp_down{name}",
                       epilogue=gate_epilogue(rows["M"] == L), cols_outer=True,
                       extras=[xtile(Xin, quarter(rows)["bm"]), gtile(gates(i, 5))], **quarter(rows))
        return a, u, r, Xout, mo

    later = {}

    def mlp0_weights(a):
        w1[0], w2[0] = ag_wait(g1b, a)
        later["g2"], tok2 = ag_start(3, w1[0])
        later["g3"], tok3 = ag_start(4, tok2)
        return tok3

    a1, u0, r0, X2, mo0 = mlp_fwd(X1, 0, g_mlp0, mmT, "0", weights_after_norm=mlp0_weights)
    g2, g3 = later["g2"], later["g3"]

    tok = ag_forward(g2, X2)
    a2 = _normmod(X2, g_mix1, pack(1, 0, 1), R=T, L=L, tr=tr, name="normmod_mix1", dep=tok)
    wa_in, wa_out = ag_wait(g2, a2)
    Wq_a, Wk_a = Hq * LANES, Hkv * LANES

    def arope_epi(acc, i, j, cos, sin):
        heads = acc.shape[1] // LANES
        return (acc * jnp.tile(cos, (1, heads)) + _swap32(acc) * jnp.tile(sin, (1, heads)),)

    bn_q = _pick(Wq_a, MM_BN)
    (q1,) = _mm(a2, wa_in, "nn", [BF16], N=Wq_a, K=D, bn=bn_q, bk=D, name="attn_q", epilogue=arope_epi,
                extras=[row_tile(acos, bmL), row_tile(asin, bmL)], **mmL)
    bn_k = bn_of(Wk_a, Wq_a)
    (k1,) = _mm(a2, wa_in, "nn", [BF16], N=Wk_a, K=D, bn=bn_k, bk=D, name="attn_k", b_col0=Wq_a, epilogue=arope_epi,
                extras=[row_tile(acos, bmT), row_tile(asin, bmT)], **mmT)
    bn_v = bn_of(Wk_a, Wq_a + Wk_a)
    (v1,) = _mm(a2, wa_in, "nn", [BF16], N=Wk_a, K=D, bn=bn_v, bk=D, name="attn_v", b_col0=Wq_a + Wk_a, **mmT)
    o1, lse = _attn_fwd(q1, k1, v1, attn_sink, L=L, CTX=CTX, Hkv=Hkv, G=G, name="attn_fwd")
    tok = ag_forward(g3, o1)
    X3, ao = _mm(o1, wa_out, "nn", [F32, BF16], N=D, K=Wq_a, bn=bnD, bk=_pick(Wq_a, 2048), name="attn_out",
                 epilogue=gate_epilogue(True), extras=[xtile(X2, bmL), gtile(gates(1, 2))], dep=tok, **mmL)

    def mlp1_weights(a):
        w1[1], w2[1] = ag_wait(g3, a)
        return None

    a3, u1, r1, X4, mo1 = mlp_fwd(X3, 1, g_mlp1, mmL, "1", weights_after_norm=mlp1_weights)

    dX4, dmo1, acc_head = _loss_head(X4, loss_target[0], mo1, final_norm_g[None, :], gates(1, 5)[0:1], L=L, tr=tr,
                                     name="loss_head")
    loss_part = jnp.sum(acc_head[0, 0])
    d_gf = acc_head[0, 1]
    zeros_d = jnp.zeros((D,), F32)
    dmod_lat = [[zeros_d] * 6, [zeros_d] * 6]
    dmod_ctx = [[zeros_d] * 6, [zeros_d] * 6]
    dmod_lat[1][5] = acc_head[0, 2]

    def dw_mm(a, b, M, N, K, name):
        return _mm(a, b, "tn", [BF16], M=M, N=N, K=K, bm=_pick(M, 512), bn=_pick(N, MM_BN), bk=K, name=name)[0]

    def mlp_bwd(dmo, a, u, r, i, rows, name):
        Mr = rows["M"]

        def times_2u(acc, i_, j_, ut):
            return (acc * (2.0 * ut.astype(F32)),)

        bnF = _pick(FF, MM_BN)
        dw2 = dw_mm(r, dmo, FF, D, Mr, f"mlp_down_dw{name}")
        tok_ = send_grad(f"w2_{i}", dw2, 0)
        (dh,) = _mm(dmo, w2[i], "nt", [BF16], N=FF, K=D, bn=bnF, bk=D, name=f"mlp_down_dx{name}", epilogue=times_2u,
                    extras=[(u, (rows["bm"], bnF), lambda i_, j_: (i_, j_))], dep=tok_, **rows)
        dw1 = dw_mm(a, dh, D, FF, Mr, f"mlp_up_dw{name}")
        tok_ = send_grad(f"w1_{i}", dw1, 1)
        (da,) = _mm(dh, w1[i], "nt", [BF16], N=D, K=FF, bn=bnD, bk=FF, name=f"mlp_up_dx{name}", dep=tok_,
                    cols_outer=True, **quarter(rows))
        return da

    pending = []

    def send_grad(key, dw, axis):
        ssem, rsem, dw_thru, land, tok_ = _scatter_start(dw, axis, name=f"rs_start_{key}")
        pending.append((key, axis, ssem, rsem, dw_thru, land))
        return tok_

    da3 = mlp_bwd(dmo1, a3, u1, r1, 1, mmL, "1")
    dX3, dao, acc = _normmod_bwd(X3, da3, dX4, False, g_mlp1, pack(1, 3, 4), (ao, gates(1, 2)), R=L, L=L, tr=tr,
                                 name="normmod_mlp1_bwd")
    dmod_lat[1][3], dmod_lat[1][4], d_gmlp1, dmod_lat[1][2] = acc[0, 0], acc[0, 1], acc[0, 2], acc[0, 3]

    dwa_out = dw_mm(o1, dao, Wq_a, D, L, "attn_out_dw")
    tok = send_grad("attn_out", dwa_out, 0)
    (do1,) = _mm(dao, wa_out, "nt", [BF16], N=Wq_a, K=D, bn=_pick(Wq_a, MM_BN), bk=D, name="attn_out_dx", dep=tok, **mmL)
    dq1, dk1, dv1, dkx, dvx, dsink_acc = _attn_bwd(q1, k1, v1, do1, lse, attn_sink, L=L, CTX=CTX, Hkv=Hkv, G=G,
                                                   name="attn_bwd")
    dp1 = _attn_assemble(dq1, dk1, dv1, dkx, dvx, acos, asin, T=T, L=L, CTX=CTX, Hq=Hq, Hkv=Hkv, tr=tr_wide,
                         name="attn_assemble")
    Wa_in = Wq_a + 2 * Wk_a
    dwa_in = dw_mm(a2, dp1, D, Wa_in, T, "attn_in_dw")
    tok = send_grad("attn_in", dwa_in, 1)
    (da2,) = _mm(dp1, wa_in, "nt", [BF16], N=D, K=Wa_in, bn=bnD, bk=_pick(Wa_in, 2 * bk_max), name="attn_in_dx",
                 dep=tok, **mmT)
    dX2, dmo0, acc = _normmod_bwd(X2, da2, dX3, True, g_mix1, pack(1, 0, 1), (mo0, gates(0, 5)), R=T, L=L, tr=tr,
                                  name="normmod_mix1_bwd")
    dmod_lat[1][0], dmod_lat[1][1], d_gmix1, dmod_lat[0][5] = acc[0, 0], acc[0, 1], acc[0, 2] + acc[1, 2], acc[0, 3]
    dmod_ctx[1][0], dmod_ctx[1][1], dmod_ctx[0][5] = acc[1, 0], acc[1, 1], acc[1, 3]

    da1 = mlp_bwd(dmo0, a1, u0, r0, 0, mmT, "0")
    dX1, dro0, acc = _normmod_bwd(X1, da1, dX2, False, g_mlp0, pack(0, 3, 4), (ro0, gates(0, 2)), R=T, L=L, tr=tr,
                                  name="normmod_mlp0_bwd")
    dmod_lat[0][3], dmod_lat[0][4], d_gmlp0, dmod_lat[0][2] = acc[0, 0], acc[0, 1], acc[0, 2] + acc[1, 2], acc[0, 3]
    dmod_ctx[0][3], dmod_ctx[0][4], dmod_ctx[0][2] = acc[1, 0], acc[1, 1], acc[1, 3]

    dwr_out = dw_mm(z0, dro0, Wv_r, D, T, "ret_out_dw")
    tok = send_grad("ret_out", dwr_out, 0)
    (dz0,) = _mm(dro0, wr_out, "nt", [BF16], N=Wv_r, K=D, bn=_pick(Wv_r, MM_BN), bk=D, name="ret_out_dx", dep=tok, **mmT)
    do0, dg0 = _readout_bwd(dz0, o0, vg0, T=T, L=L, H=RH, tr=tr_wide, name="ret_readout_bwd")
    (dq_f, dk_f, dv_f, dlg_f), (dq_b, dk_b, dv_b, dlg_b) = _ret_bwd(qk0, vg0, do0, (st_f, st_b), (lg_f, lg_b),
                                                                    T=T, L=L, H=RH, name="ret_scan_bwd")
    dp0 = _ret_assemble(dq_f, dq_b, dk_f, dk_b, dv_f, dv_b, dg0, rcos, rsin, T=T, L=L, H=RH, tr=tr_wide,
                        name="ret_assemble")
    Wr_in = 2 * Wq_r + 2 * Wv_r
    dwr_in = dw_mm(a0, dp0, D, Wr_in, T, "ret_in_dw")
    tok = send_grad("ret_in", dwr_in, 1)
    (da0,) = _mm(dp0, wr_in, "nt", [BF16], N=D, K=Wr_in, bn=bnD, bk=Wr_in, name="ret_in_dx", dep=tok, cols_outer=True,
                 b_buffers=1, **quarter(mmT))
    dX0, acc = _normmod_bwd(X0, da0, dX1, False, g_mix0, pack(0, 0, 1), None, R=L, L=L, tr=tr, name="normmod_mix0_bwd")
    _, acc_c = _normmod_bwd(X0, da0, dX1, False, g_mix0, pack(0, 0, 1), None, R=CTX, L=0, tr=tr, row0=L,
                            name="normmod_mix0_bwd_ctx")
    dmod_lat[0][0], dmod_lat[0][1], d_gmix0 = acc[0, 0], acc[0, 1], acc[0, 2] + acc_c[1, 2]
    dmod_ctx[0][0], dmod_ctx[0][1] = acc_c[1, 0], acc_c[1, 1]
    grad_x = dX0[None]

    wmv = {"ret_in": (ret_w_in, m_ret_w_in, v_ret_w_in, 0, "ret_w_in"),
           "ret_out": (ret_w_out, m_ret_w_out, v_ret_w_out, 0, "ret_w_out"),
           "attn_in": (attn_w_in, m_attn_w_in, v_attn_w_in, 0, "attn_w_in"),
           "attn_out": (attn_w_out, m_attn_w_out, v_attn_w_out, 0, "attn_w_out"),
           "w1_0": (mlp_w1, m_mlp_w1, v_mlp_w1, 0, "mlp_w1"), "w1_1": (mlp_w1, m_mlp_w1, v_mlp_w1, 1, "mlp_w1"),
           "w2_0": (mlp_w2, m_mlp_w2, v_mlp_w2, 0, "mlp_w2"), "w2_1": (mlp_w2, m_mlp_w2, v_mlp_w2, 1, "mlp_w2")}
    big = {}

    def finish_grad(entry, after):
        key, axis, ssem, rsem, dw_thru, land = entry
        dw_done, land_done = _scatter_wait(ssem, rsem, dw_thru, land, axis, after, name=f"rs_wait_{key}")
        w_, m_, v_, layer, out_name = wmv[key]
        big[out_name] = _adamw_sharded(w_, m_, v_, layer, dw_done, land_done, axis, me_arr, big.get(out_name),
                                       name=f"adamw_{key}")
        return big[out_name][0]

    after = dX0
    for entry in pending[:-1]:
        after = finish_grad(entry, after)

    misc = jnp.zeros((D,), F32)
    misc = misc.at[0:RH].set(dlg_f[:, 0, 0]).at[RH:2 * RH].set(dlg_b[:, 0, 0])
    misc = misc.at[2 * RH:2 * RH + Hq].set(dsink_acc[:, :G, 0].reshape(Hq)).at[2 * RH + Hq].set(loss_part)
    rows = ([dmod_lat[i][k] for i in range(2) for k in range(6)] + [dmod_ctx[i][k] for i in range(2) for k in range(6)]
            + [d_gmix0, d_gmix1, d_gmlp0, d_gmlp1, d_gf, misc, zeros_d, zeros_d])
    part = jnp.stack(rows, 0)
    part_all = _all_gather_small(part, name="ag_small_grads", dep=after)
    tot = _sum_devices(part_all, name="sum_small_grads")

    grad_ada_b = (tot[0:12] + tot[12:24]).reshape(2, 6 * D)
    grad_norm_mix_g, grad_norm_mlp_g, grad_final_norm_g = tot[24:26], tot[26:28], tot[28]
    grad_ret_decay_fwd = (tot[29, 0:RH] * jax.nn.sigmoid(-ret_decay_fwd[0]))[None]
    grad_ret_decay_bwd = (tot[29, RH:2 * RH] * jax.nn.sigmoid(-ret_decay_bwd[0]))[None]
    grad_attn_sink = tot[29, 2 * RH:2 * RH + Hq][None]
    loss = tot[29, 2 * RH + Hq]

    dlat_cols = lax.dynamic_slice_in_dim(part_all[:, 0:12].reshape(N_DEV, 2, 6 * D), me * acols, acols, axis=2)
    dctx_cols = lax.dynamic_slice_in_dim(tot[12:24].reshape(2, 6 * D), me * acols, acols, axis=1)
    dmod16 = jnp.concatenate([dlat_cols.transpose(1, 0, 2), dctx_cols[:, None, :], jnp.zeros((2, 7, acols), F32)], 1)
    cond_t = _silu(c16).T
    g_ada, d_ada, nm_ada, nv_ada, dcond_part = _ada_bwd(cond_t, dmod16, ada_w, m_ada_w, v_ada_w, name="ada_bwd")
    dcond = (dcond_part[0, :, 0] + dcond_part[1, :, 0]).reshape(D // LANES, LANES)
    pad_rows = -(D // LANES) % 8
    dcond_pad = jnp.concatenate([dcond, jnp.zeros((pad_rows, LANES), F32)], 0) if pad_rows else dcond
    dcond_all = _all_gather_small(dcond_pad, name="ag_dcond")
    dcond_tot = _sum_devices(dcond_all, name="sum_dcond")[:D // LANES].reshape(D)
    sg = jax.nn.sigmoid(c_ctx)
    grad_c_ctx = dcond_tot * (sg * (1.0 + c_ctx * (1.0 - sg)))

    small_w = [c_ctx, ada_b, norm_mix_g, norm_mlp_g, ret_decay_fwd, ret_decay_bwd, attn_sink, final_norm_g]
    small_g = [grad_c_ctx, grad_ada_b, grad_norm_mix_g, grad_norm_mlp_g, grad_ret_decay_fwd, grad_ret_decay_bwd,
               grad_attn_sink, grad_final_norm_g]
    small_m = [m_c_ctx, m_ada_b, m_norm_mix_g, m_norm_mlp_g, m_ret_decay_fwd, m_ret_decay_bwd, m_attn_sink,
               m_final_norm_g]
    small_v = [v_c_ctx, v_ada_b, v_norm_mix_g, v_norm_mlp_g, v_ret_decay_fwd, v_ret_decay_bwd, v_attn_sink,
               v_final_norm_g]
    sizes = [w_.size for w_ in small_w]
    total = sum(-(-s // LANES) * LANES for s in sizes)
    total_pad = -(-total // (8 * LANES)) * 8 * LANES

    def flat_pack(ts, fill):
        pieces = []
        for t_ in ts:
            f = t_.reshape(-1).astype(F32)
            pad = -f.size % LANES
            pieces.append(jnp.concatenate([f, jnp.full((pad,), fill, F32)]) if pad else f)
        pieces.append(jnp.full((total_pad - total,), fill, F32))
        return jnp.concatenate(pieces).reshape(total_pad // LANES, LANES)

    d_s, nm_s, nv_s = _adamw_flat(flat_pack(small_w, 0.0), flat_pack(small_g, 0.0), flat_pack(small_m, 0.0),
                                  flat_pack(small_v, 1.0), name="adamw_small")

    def unpack(p):
        flat = p.reshape(-1)
        res, off = [], 0
        for w_, s in zip(small_w, sizes):
            res.append(flat[off:off + s].reshape(w_.shape))
            off += -(-s // LANES) * LANES
        return res

    finish_grad(pending[-1], d_s)
    d_small, nm_small, nv_small = unpack(d_s), unpack(nm_s), unpack(nv_s)
    small_names = ["c_ctx", "ada_b", "norm_mix_g", "norm_mlp_g", "ret_decay_fwd", "ret_decay_bwd", "attn_sink",
                   "final_norm_g"]
    sm = {n: (g_, d_, m_, v_) for n, g_, d_, m_, v_ in zip(small_names, small_g, d_small, nm_small, nv_small)}

    def out4(n):
        if n == "ada_w":
            return g_ada, d_ada, nm_ada, nv_ada
        if n in big:
            return tuple(big[n])
        return sm[n]

    order = ["c_ctx", "ada_w", "ada_b", "norm_mix_g", "norm_mlp_g", "mlp_w1", "mlp_w2", "ret_w_in", "ret_w_out",
             "ret_decay_fwd", "ret_decay_bwd", "attn_w_in", "attn_w_out", "attn_sink", "final_norm_g"]
    quads = [out4(n) for n in order]
    return (loss, grad_x, *[q_[0] for q_ in quads], *[q_[1] for q_ in quads], *[q_[2] for q_ in quads],
            *[q_[3] for q_ in quads])
```

```python
import jax
import jax.numpy as jnp
from jax import lax
from jax.experimental import pallas as pl
from jax.experimental.pallas import tpu as pltpu

F32 = jnp.float32
BF16 = jnp.bfloat16

N_DEV = 8
NORM_EPS = 1e-6
CHUNK = 128
ATT_HEAD_DIM = 128
GRID_W = 64
ROPE_BASE = 10000.0
NEG_INF = -1e30
ADAM_LR, ADAM_B1, ADAM_B2, ADAM_EPS, ADAM_WD, ADAM_STEP = 0.001, 0.9, 0.999, 1e-08, 0.01, 10

V7X_VMEM_LIMIT_BYTES = 56 * 1024 * 1024
MM_BN = 1024
LANES = 128
MESH = pl.DeviceIdType.MESH

_NN = (((1,), (0,)), ((), ()))
_NT = (((1,), (1,)), ((), ()))
_TN = (((0,), (0,)), ((), ()))


def _dot(a, b, dn=_NN):
    return lax.dot_general(a, b, dn, preferred_element_type=F32)


def _cparams(*sem):
    return pltpu.CompilerParams(dimension_semantics=sem, vmem_limit_bytes=V7X_VMEM_LIMIT_BYTES)


def _pick(n, pref, mult=LANES):
    if n <= pref:
        return n
    best = None
    for d in range(mult, pref + 1, mult):
        if n % d == 0:
            best = d
    assert best is not None, (n, pref)
    return best


def _silu(x):
    return x * jax.nn.sigmoid(x)


def _mm(a, b, mode, out_dtypes, *, M, N, K, bm, bn, bk, name, b_col0=0, epilogue=None, extras=(), dep=None,
        cols_outer=False, b_buffers=None):
    assert M % bm == 0 and N % bn == 0 and K % bk == 0 and b_col0 % bn == 0, (name, M, N, K, bm, bn, bk, b_col0)
    nk = K // bk
    c0 = b_col0 // bn
    ax_i, ax_j = (1, 0) if cols_outer else (0, 1)

    def spec(block, f, **kw):
        if cols_outer:
            return pl.BlockSpec(block, lambda j, i, k: f(i, j, k), **kw)
        return pl.BlockSpec(block, f, **kw)

    b_kw = {} if b_buffers is None else dict(pipeline_mode=pl.Buffered(b_buffers))
    if mode == "nn":
        a_spec = spec((bm, bk), lambda i, j, k: (i, k))
        b_spec = spec((bk, bn), lambda i, j, k: (k, j + c0), **b_kw)
    elif mode == "nt":
        a_spec = spec((bm, bk), lambda i, j, k: (i, k))
        b_spec = spec((bn, bk), lambda i, j, k: (j + c0, k), **b_kw)
    else:
        a_spec = spec((bk, bm), lambda i, j, k: (k, i))
        b_spec = spec((bk, bn), lambda i, j, k: (k, j + c0), **b_kw)
    dn = {"nn": _NN, "nt": _NT, "tn": _TN}[mode]
    e_specs = [spec(bs, (lambda i, j, k, f=f: f(i, j))) for (_, bs, f) in extras]
    ne, no = len(extras), len(out_dtypes)
    nd = 0 if dep is None else 1

    def body(a_ref, b_ref, *rest):
        e_refs, o_refs = rest[:ne], rest[ne + nd:ne + nd + no]
        i, j, k = pl.program_id(ax_i), pl.program_id(ax_j), pl.program_id(2)

        def finish(acc):
            outs = (acc,) if epilogue is None else epilogue(acc, i, j, *[e[...] for e in e_refs])
            for o_ref, o in zip(o_refs, outs):
                o_ref[...] = o.astype(o_ref.dtype)

        p = _dot(a_ref[...], b_ref[...], dn)
        if nk == 1:
            finish(p)
        else:
            acc_ref = rest[-1]

            @pl.when(k == 0)
            def _():
                acc_ref[...] = p

            @pl.when(k > 0)
            def _():
                acc_ref[...] += p

            @pl.when(k == nk - 1)
            def _():
                finish(acc_ref[...])

    outs = pl.pallas_call(
        body, name=name, grid=(N // bn, M // bm, nk) if cols_outer else (M // bm, N // bn, nk),
        in_specs=[a_spec, b_spec] + e_specs + [pl.BlockSpec(memory_space=pl.ANY)] * nd,
        out_specs=[spec((bm, bn), lambda i, j, k: (i, j)) for _ in out_dtypes],
        out_shape=[jax.ShapeDtypeStruct((M, N), dt) for dt in out_dtypes],
        scratch_shapes=[pltpu.VMEM((bm, bn), F32)] if nk > 1 else [],
        compiler_params=_cparams("parallel", "parallel", "arbitrary"),
    )(a, b, *[e[0] for e in extras], *([dep] if nd else []))
    return outs


def _rowwise(body, rows, vecs, outs, n_acc, *, R, L, tr, name, acc_width=None, dep=None, row0=0):
    assert row0 % tr == 0
    b0 = row0 // tr
    assert R % tr == 0 and L % tr == 0, (name, R, L, tr)
    nl = L // tr
    n_regions = 2 if R > L else 1
    n_rows, n_vecs, n_outs = len(rows), len(vecs), len(outs)
    deps = () if dep is None else (tuple(dep) if isinstance(dep, (tuple, list)) else (dep,))
    n_dep = len(deps)
    acc_pad = -(-n_acc // 8) * 8 if n_acc else 0

    in_specs = []
    for (_, w, cb, lat_only) in rows:
        if lat_only:
            in_specs.append(pl.BlockSpec((tr, w), lambda i, cb=cb: (jnp.minimum(i, nl - 1), cb)))
        else:
            in_specs.append(pl.BlockSpec((tr, w), lambda i, cb=cb: (i + b0, cb)))
    for v in vecs:
        in_specs.append(pl.BlockSpec(v.shape, lambda i, nd=v.ndim: (0,) * nd))
    in_specs += [pl.BlockSpec(memory_space=pl.ANY)] * n_dep
    out_specs = [pl.BlockSpec((tr, w), lambda i: (i, 0)) for (w, _) in outs]
    out_shape = [jax.ShapeDtypeStruct((R, w), dt) for (w, dt) in outs]
    if n_acc:
        out_specs.append(pl.BlockSpec((None, acc_pad, acc_width), lambda i: (jnp.where(i >= nl, 1, 0), 0, 0)))
        out_shape.append(jax.ShapeDtypeStruct((n_regions, acc_pad, acc_width), F32))

    def kern(*refs):
        i = pl.program_id(0)
        is_ctx = i >= nl
        ins = [r[...] for r in refs[:n_rows + n_vecs]]
        o_refs = refs[n_rows + n_vecs + n_dep:]
        out_tiles, acc_rows = body(is_ctx, *ins)
        for o_ref, o in zip(o_refs[:n_outs], out_tiles):
            o_ref[...] = o.astype(o_ref.dtype)
        if n_acc:
            acc_ref = o_refs[n_outs]

            @pl.when((i == 0) | (i == nl))
            def _():
                acc_ref[...] = jnp.zeros_like(acc_ref)

            for r, row in enumerate(acc_rows):
                acc_ref[r:r + 1, :] += row

    res = pl.pallas_call(
        kern, name=name, grid=(R // tr,), in_specs=in_specs, out_specs=out_specs, out_shape=out_shape,
        compiler_params=_cparams("arbitrary"),
    )(*[r[0] for r in rows], *vecs, *deps)
    return res


def _colsum(x):
    return jnp.sum(x, axis=0, keepdims=True)


def _rms_stats(x):
    r = lax.rsqrt(jnp.mean(x * x, axis=-1, keepdims=True) + NORM_EPS)
    return x * r, r


def _sel(is_ctx, pk, lat_row, ctx_row):
    return jnp.where(is_ctx, pk[ctx_row:ctx_row + 1, :], pk[lat_row:lat_row + 1, :])


def _normmod(x, g, pk, *, R, L, tr, name, dep=None):
    D = x.shape[-1]

    def body(is_ctx, xt, gv, pkv):
        xh, _ = _rms_stats(xt)
        sh, sc = _sel(is_ctx, pkv, 0, 2), _sel(is_ctx, pkv, 1, 3)
        return ((xh * gv) * (1.0 + sc) + sh,), ()

    return _rowwise(body, [(x, D, 0, False)], [g, pk], [(D, BF16)], 0, R=R, L=L, tr=tr, name=name, dep=dep)[0]


def _normmod_bwd(x_in, da, dx_out, dx_out_lat_only, g, pk, prev, *, R, L, tr, name, row0=0):
    D = x_in.shape[-1]
    has_prev = prev is not None

    def body(is_ctx, *t):
        if has_prev:
            xt, dat, dxo, mp, gv, pkv, gates = t
        else:
            xt, dat, dxo, gv, pkv = t
        xh, r = _rms_stats(xt)
        dat = dat.astype(F32)
        sc = _sel(is_ctx, pkv, 1, 3)
        if dx_out_lat_only:
            dxo = jnp.where(is_ctx, 0.0, dxo)
        dn = dat * (1.0 + sc)
        w = dn * gv
        dxi = dxo + r * (w - xh * jnp.mean(w * xh, axis=-1, keepdims=True))
        accs = [_colsum(dat), _colsum(dat * (xh * gv)), _colsum(dn * xh)]
        outs = [dxi]
        if has_prev:
            gate = _sel(is_ctx, gates, 0, 1)
            outs.append(dxi * gate)
            accs.append(_colsum(dxi * mp.astype(F32)))
        return outs, accs

    rows = [(x_in, D, 0, False), (da, D, 0, False), (dx_out, D, 0, dx_out_lat_only)]
    vecs = [g, pk]
    outs = [(D, F32)]
    if has_prev:
        rows.append((prev[0], D, 0, False))
        vecs.append(prev[1])
        outs.append((D, BF16))
    return _rowwise(body, rows, vecs, outs, 4 if has_prev else 3, R=R, L=L, tr=tr, name=name, acc_width=D, row0=row0)


def _loss_head(x4, target, m_prev, gf, gate, *, L, tr, name):
    D = x4.shape[-1]

    def body(is_ctx, xt, tg, mp, gfv, gatev):
        xh, r = _rms_stats(xt)
        e = xh * gfv - tg
        dy = e * (1.0 / D)
        w = dy * gfv
        dx = r * (w - xh * jnp.mean(w * xh, axis=-1, keepdims=True))
        accs = [_colsum(e * e) * (0.5 / D), _colsum(dy * xh), _colsum(dx * mp.astype(F32))]
        return (dx, dx * gatev), accs

    return _rowwise(body, [(x4, D, 0, False), (target, D, 0, False), (m_prev, D, 0, False)], [gf, gate],
                    [(D, F32), (D, BF16)], 3, R=L, L=L, tr=tr, name=name, acc_width=D)


RET_CHUNK = 2 * LANES
RET_HEADS_PER_STEP = 4


def _decays(lgh, rev):
    C = RET_CHUNK
    ii = lax.broadcasted_iota(jnp.int32, (C, C), 0)
    jj = lax.broadcasted_iota(jnp.int32, (C, C), 1)
    ri = lax.broadcasted_iota(jnp.int32, (C, 1), 0).astype(F32)
    diff = (jj - ii if rev else ii - jj)
    amat = jnp.where(diff >= 0, jnp.exp(lgh * jnp.maximum(diff, 0).astype(F32)), 0.0)
    pos = (C - ri) if rev else (ri + 1.0)
    bq = jnp.exp(lgh * pos)
    bk = jnp.exp(lgh * (C - pos))
    return amat, bq, bk, pos


def _ret_geometry(T, L, H, rev, backward):
    C = RET_CHUNK
    assert T % C == 0 and L % C == 0, (T, L)
    nT, nL = T // C, L // C
    hb = RET_HEADS_PER_STEP if H % RET_HEADS_PER_STEP == 0 else 1

    def step(s):
        return (nT - 1 - s) if backward else s

    def chunk(s):
        s = step(s)
        return (nT - 1 - s) if rev else (s + nL) % nT

    return C, nT, hb, chunk, step


def _ret_fwd(qk, vg, lg, other, *, T, L, H, rev, name, readout=False):
    dk, dv = 2 * LANES, 4 * LANES
    C, nT, hb, chunk, step = _ret_geometry(T, L, H, rev, False)
    n_other = 0 if other is None else 1
    n_read = 1 if readout else 0

    def body(lg_ref, q_ref, k_ref, v_ref, *rest):
        o_ref, st_ref, s_scr = rest[n_other + n_read], rest[-2], rest[-1]
        hg, s = pl.program_id(0), pl.program_id(1)

        @pl.when(s == 0)
        def _():
            s_scr[...] = jnp.zeros_like(s_scr)

        for hh in range(hb):
            lgh = lg_ref[0, hg * hb + hh]
            amat, bq, bk, _ = _decays(lgh, rev)
            q, k = q_ref[:, hh * dk:(hh + 1) * dk], k_ref[:, hh * dk:(hh + 1) * dk]
            v = v_ref[:, hh * dv:(hh + 1) * dv]
            stb = s_scr[hh].astype(BF16)
            st_ref[hh] = stb
            scores = _dot(q, k, _NT) * amat
            o = _dot(scores.astype(BF16), v) + _dot(q, stb) * bq
            if n_other:
                o = rest[0][:, hh * dv:(hh + 1) * dv] + o
            o_ref[:, hh * dv:(hh + 1) * dv] = o
            if n_read:
                g = rest[n_other][:, hh * dv:(hh + 1) * dv].astype(F32)
                y = o * lax.rsqrt(jnp.mean(o * o, axis=-1, keepdims=True) + NORM_EPS)
                rest[n_other + n_read + 1][:, hh * dv:(hh + 1) * dv] = (_silu(g) * y).astype(BF16)
            kd = (k.astype(F32) * bk).astype(BF16)
            s_scr[hh] = s_scr[hh] * jnp.exp(lgh * C) + _dot(kd, v, _TN)

    vspec = pl.BlockSpec((C, hb * dv), lambda h, s: (chunk(s), h))
    gspec = pl.BlockSpec((C, hb * dv), lambda h, s: (chunk(s), H // hb + h))
    return pl.pallas_call(
        body, name=name, grid=(H // hb, nT),
        in_specs=[pl.BlockSpec(memory_space=pltpu.SMEM),
                  pl.BlockSpec((C, hb * dk), lambda h, s: (chunk(s), h)),
                  pl.BlockSpec((C, hb * dk), lambda h, s: (chunk(s), H // hb + h)), vspec]
        + [vspec] * n_other + [gspec] * n_read,
        out_specs=[vspec] + [vspec] * n_read + [pl.BlockSpec((hb, None, dk, dv), lambda h, s: (h, s, 0, 0))],
        out_shape=[jax.ShapeDtypeStruct((T, H * dv), F32)] + [jax.ShapeDtypeStruct((T, H * dv), BF16)] * n_read
        + [jax.ShapeDtypeStruct((H, nT, dk, dv), BF16)],
        scratch_shapes=[pltpu.VMEM((hb, dk, dv), F32)],
        compiler_params=_cparams("parallel", "arbitrary"),
    )(lg, qk, qk, vg, *([other] if n_other else []), *([vg] if n_read else []))


def _ret_bwd(qk, vg, do, states, lgs, *, T, L, H, name):
    dk, dv = 2 * LANES, 4 * LANES
    revs = (False, True)
    geo = [_ret_geometry(T, L, H, rev, True) for rev in revs]
    C, nT, hb = geo[0][:3]
    N_IN, N_OUT = 6, 4

    def body(*refs):
        ins = [refs[d * N_IN:(d + 1) * N_IN] for d in range(2)]
        outs = [refs[2 * N_IN + d * N_OUT:2 * N_IN + (d + 1) * N_OUT] for d in range(2)]
        ds_scr = refs[-1]
        hg, s = pl.program_id(0), pl.program_id(1)

        @pl.when(s == 0)
        def _():
            ds_scr[...] = jnp.zeros_like(ds_scr)
            for d in range(2):
                outs[d][3][...] = jnp.zeros_like(outs[d][3])

        for hh in range(hb):
            for d, rev in enumerate(revs):
                lg_ref, q_ref, k_ref, v_ref, do_ref, st_ref = ins[d]
                dq_ref, dk_ref, dv_ref, dlg_ref = outs[d]
                lgh = lg_ref[0, hg * hb + hh]
                amat, bq, bk, pos = _decays(lgh, rev)
                ksl, vsl = slice(hh * dk, (hh + 1) * dk), slice(hh * dv, (hh + 1) * dv)
                q, k, v, dob = q_ref[:, ksl], k_ref[:, ksl], v_ref[:, vsl], do_ref[:, vsl]
                stb = st_ref[hh]
                ds_new = ds_scr[d, hh]
                dsb = ds_new.astype(BF16)
                qf, kf = q.astype(F32), k.astype(F32)
                scores = (_dot(q, k, _NT) * amat).astype(BF16)
                dqk = (_dot(dob, v, _NT) * amat).astype(BF16)
                dq = _dot(dqk, k) + _dot(dob, stb, _NT) * bq
                dkk = _dot(dqk, q, _TN) + _dot(v, dsb, _NT) * bk
                kd = (kf * bk).astype(BF16)
                dvv = _dot(scores, dob, _TN) + _dot(kd, dsb)
                dod = (dob.astype(F32) * bq).astype(BF16)
                ds_prev = ds_new * jnp.exp(lgh * C) + _dot(q, dod, _TN)
                ds_scr[d, hh] = ds_prev
                dq_ref[:, ksl] = dq.astype(dq_ref.dtype)
                dk_ref[:, ksl] = dkk.astype(dk_ref.dtype)
                dv_ref[:, vsl] = dvv.astype(dv_ref.dtype)
                dlg = (jnp.sum(pos * jnp.sum(qf * dq - kf * dkk, axis=-1, keepdims=True))
                       + C * jnp.sum(ds_prev * stb.astype(F32)))
                dlg_ref[hh] += dlg

    in_specs, out_specs, operands = [], [], []
    for d in range(2):
        chunk, step = geo[d][3], geo[d][4]
        qspec = pl.BlockSpec((C, hb * dk), lambda h, s, chunk=chunk: (chunk(s), h))
        vspec = pl.BlockSpec((C, hb * dv), lambda h, s, chunk=chunk: (chunk(s), h))
        in_specs += [pl.BlockSpec(memory_space=pltpu.SMEM), qspec,
                     pl.BlockSpec((C, hb * dk), lambda h, s, chunk=chunk: (chunk(s), H // hb + h)), vspec, vspec,
                     pl.BlockSpec((hb, None, dk, dv), lambda h, s, step=step: (h, step(s), 0, 0))]
        out_specs += [qspec, qspec, vspec, pl.BlockSpec((hb, 8, LANES), lambda h, s: (h, 0, 0))]
        operands += [lgs[d], qk, qk, vg, do, states[d]]
    one_dir = [jax.ShapeDtypeStruct((T, H * dk), BF16), jax.ShapeDtypeStruct((T, H * dk), BF16),
               jax.ShapeDtypeStruct((T, H * dv), BF16), jax.ShapeDtypeStruct((H, 8, LANES), F32)]
    res = pl.pallas_call(
        body, name=name, grid=(H // hb, nT), in_specs=in_specs, out_specs=out_specs, out_shape=one_dir * 2,
        scratch_shapes=[pltpu.VMEM((2, hb, dk, dv), F32)],
        compiler_params=_cparams("parallel", "arbitrary"),
    )(*operands)
    return res[:N_OUT], res[N_OUT:]


def _readout_bwd(dz, o, vg, *, T, L, H, tr, name):
    dv = 4 * LANES
    W = H * dv

    def body(is_ctx, dzt, o, g):
        gf = g.astype(F32)
        sg = jax.nn.sigmoid(gf)
        dzf = dzt.astype(F32)
        dy = dzf * (gf * sg)
        ys, dos = [], []
        for h in range(H):
            sl = slice(h * dv, (h + 1) * dv)
            oh, dyh = o[:, sl], dy[:, sl]
            r = lax.rsqrt(jnp.mean(oh * oh, axis=-1, keepdims=True) + NORM_EPS)
            yh = oh * r
            ys.append(yh)
            dos.append(r * (dyh - yh * jnp.mean(dyh * yh, axis=-1, keepdims=True)))
        y = jnp.concatenate(ys, axis=1)
        dg = dzf * y * (sg * (1.0 + gf * (1.0 - sg)))
        return (jnp.concatenate(dos, axis=1), dg), ()

    return _rowwise(body, [(dz, W, 0, False), (o, W, 0, False), (vg, W, 1, False)], [],
                    [(W, BF16), (W, BF16)], 0, R=T, L=L, tr=tr, name=name)


def _ret_assemble(dq_f, dq_b, dk_f, dk_b, dv_f, dv_b, dg, cos, sin, *, T, L, H, tr, name):
    dk, dv = 2 * LANES, 4 * LANES
    Wq, Wv = H * dk, H * dv
    kscale = float(dk) ** -0.5

    def unrope(d, c, s_, scale):
        parts = []
        for h in range(H):
            d1, d2 = d[:, h * dk:h * dk + LANES], d[:, h * dk + LANES:(h + 1) * dk]
            parts += [(d1 * c + d2 * s_) * scale, (d2 * c - d1 * s_) * scale]
        return jnp.concatenate(parts, axis=1)

    def body(is_ctx, qf, qb, kf, kb, vf, vb, g, c, s_):
        add = lambda a, b: a.astype(F32) + b.astype(F32)
        dq = unrope(add(qf, qb), c, s_, 1.0)
        dkk = unrope(add(kf, kb), c, s_, kscale)
        return (jnp.concatenate([dq.astype(BF16), dkk.astype(BF16), add(vf, vb).astype(BF16), g], axis=1),), ()

    rows = [(dq_f, Wq, 0, False), (dq_b, Wq, 0, False), (dk_f, Wq, 0, False), (dk_b, Wq, 0, False),
            (dv_f, Wv, 0, False), (dv_b, Wv, 0, False), (dg, Wv, 0, False),
            (cos, LANES, 0, False), (sin, LANES, 0, False)]
    return _rowwise(body, rows, [], [(2 * Wq + 2 * Wv, BF16)], 0, R=T, L=L, tr=tr, name=name)[0]


def _swap32(x):
    n = x.shape[-1]
    lane = lax.broadcasted_iota(jnp.int32, x.shape, x.ndim - 1)
    return jnp.where(lane % 64 < 32, pltpu.roll(x, n - 32, x.ndim - 1), pltpu.roll(x, 32, x.ndim - 1))


ATT_Q_BLOCKS = 4


def _stack_heads_at(ref, rows, G):
    return jnp.concatenate([ref[rows, g * LANES:(g + 1) * LANES] for g in range(G)], axis=0)


def _stack_columns_at(ref, rows, G):
    return jnp.concatenate([ref[rows, g:g + 1] for g in range(G)], axis=0)


def _sink_column(sink_ref, h, G):
    return jnp.concatenate([jnp.full((CHUNK, 1), sink_ref[0, h * G + g], F32) for g in range(G)], axis=0)


def _key_mask(n, nb, CTX, G):
    W = 3 * CHUNK + CTX
    ii = lax.broadcasted_iota(jnp.int32, (G * CHUNK, W), 0) & (CHUNK - 1)
    col = lax.broadcasted_iota(jnp.int32, (G * CHUNK, W), 1)
    is_prev = col < CHUNK
    is_next = (col >= 2 * CHUNK) & (col < 3 * CHUNK)
    prev_ok = is_prev & (col >= ii) & (n > 0)
    next_ok = is_next & ((col - 2 * CHUNK) <= ii) & (n < nb - 1)
    return prev_ok | next_ok | jnp.logical_not(is_prev | is_next)


def _attn_geometry(L, CTX):
    nb = L // CHUNK
    QB = ATT_Q_BLOCKS if nb % ATT_Q_BLOCKS == 0 else 1

    def blk(j):
        return pl.BlockSpec((CHUNK, LANES), lambda h, m: (jnp.clip(m * QB + j - 1, 0, nb - 1), h))

    kvs = [blk(j) for j in range(QB + 2)] + [pl.BlockSpec((CTX, LANES), lambda h, m: (L // CTX, h))]
    return nb, QB, kvs


def _attn_fwd(q, k, v, sink, *, L, CTX, Hkv, G, name):
    scale = float(ATT_HEAD_DIM) ** -0.5
    nb, QB, kvs = _attn_geometry(L, CTX)
    nkv = QB + 3

    def body(sink_ref, q_ref, *rest):
        kb, vb, (o_ref, lse_ref) = rest[:nkv], rest[nkv:2 * nkv], rest[2 * nkv:]
        h, m_ = pl.program_id(0), pl.program_id(1)
        sk = _sink_column(sink_ref, h, G)
        for sub in range(QB):
            rows = slice(sub * CHUNK, (sub + 1) * CHUNK)
            qs = _stack_heads_at(q_ref, rows, G)
            kall = jnp.concatenate([kb[sub + j][...] for j in range(3)] + [kb[-1][...]], axis=0)
            vall = jnp.concatenate([vb[sub + j][...] for j in range(3)] + [vb[-1][...]], axis=0)
            s_ = jnp.where(_key_mask(m_ * QB + sub, nb, CTX, G), _dot(qs, kall, _NT) * scale, NEG_INF)
            m = jnp.maximum(jnp.max(s_, axis=-1, keepdims=True), sk)
            p = jnp.exp(s_ - m)
            den = jnp.sum(p, axis=-1, keepdims=True) + jnp.exp(sk - m)
            o = _dot(p.astype(BF16), vall) / den
            lse = m + jnp.log(den)
            for g in range(G):
                o_ref[rows, g * LANES:(g + 1) * LANES] = o[g * CHUNK:(g + 1) * CHUNK].astype(o_ref.dtype)
                lse_ref[rows, g:g + 1] = lse[g * CHUNK:(g + 1) * CHUNK]

    qspec = pl.BlockSpec((QB * CHUNK, G * LANES), lambda h, m: (m, h))
    return pl.pallas_call(
        body, name=name, grid=(Hkv, nb // QB),
        in_specs=[pl.BlockSpec(memory_space=pltpu.SMEM), qspec] + kvs + kvs,
        out_specs=[qspec, pl.BlockSpec((None, QB * CHUNK, G), lambda h, m: (h, m, 0))],
        out_shape=[jax.ShapeDtypeStruct((L, Hkv * G * LANES), BF16), jax.ShapeDtypeStruct((Hkv, L, G), F32)],
        compiler_params=_cparams("parallel", "parallel"),
    )(sink, q, *([k] * nkv), *([v] * nkv))


def _attn_bwd(q, k, v, do, lse, sink, *, L, CTX, Hkv, G, name):
    scale = float(ATT_HEAD_DIM) ** -0.5
    nb, QB, kvs = _attn_geometry(L, CTX)
    nkv = QB + 3
    qspec = pl.BlockSpec((QB * CHUNK, G * LANES), lambda h, m: (m, h))
    rowspec = pl.BlockSpec((None, QB * CHUNK, G), lambda h, m: (h, m, 0))
    colspec = lambda rows: pl.BlockSpec((rows, LANES), lambda h, m: (0, h))

    def body(sink_ref, q_ref, do_ref, lse_ref, *rest):
        kb, vb = rest[:nkv], rest[nkv:2 * nkv]
        dq_ref, dk_ref, dv_ref, dkx_ref, dvx_ref, dsk_ref = rest[2 * nkv:]
        h, m_ = pl.program_id(0), pl.program_id(1)

        @pl.when(m_ == 0)
        def _():
            for r in (dk_ref, dv_ref, dkx_ref, dvx_ref, dsk_ref):
                r[...] = jnp.zeros_like(r)

        sk = _sink_column(sink_ref, h, G)
        for sub in range(QB):
            n = m_ * QB + sub
            rows = slice(sub * CHUNK, (sub + 1) * CHUNK)
            qs, dos = _stack_heads_at(q_ref, rows, G), _stack_heads_at(do_ref, rows, G)
            kall = jnp.concatenate([kb[sub + j][...] for j in range(3)] + [kb[-1][...]], axis=0)
            vall = jnp.concatenate([vb[sub + j][...] for j in range(3)] + [vb[-1][...]], axis=0)
            lse_c = _stack_columns_at(lse_ref, rows, G)
            p = jnp.where(_key_mask(n, nb, CTX, G), jnp.exp(_dot(qs, kall, _NT) * scale - lse_c), 0.0)
            dp = _dot(dos, vall, _NT)
            delta = jnp.sum(p * dp, axis=-1, keepdims=True)
            ds = (p * (dp - delta) * scale).astype(BF16)
            dq = _dot(ds, kall)
            dk_all = _dot(ds, qs, _TN)
            dv_all = _dot(p.astype(BF16), dos, _TN)
            for g in range(G):
                dq_ref[rows, g * LANES:(g + 1) * LANES] = dq[g * CHUNK:(g + 1) * CHUNK]
            for part, blk in enumerate((jnp.maximum(n - 1, 0), n, jnp.minimum(n + 1, nb - 1))):
                krows = pl.ds(pl.multiple_of(blk * CHUNK, CHUNK), CHUNK)
                dk_ref[krows, :] += dk_all[part * CHUNK:(part + 1) * CHUNK]
                dv_ref[krows, :] += dv_all[part * CHUNK:(part + 1) * CHUNK]
            dkx_ref[...] += dk_all[3 * CHUNK:]
            dvx_ref[...] += dv_all[3 * CHUNK:]
            dsink = -jnp.exp(sk - lse_c) * delta
            for g in range(G):
                dsk_ref[g:g + 1, :] += jnp.sum(dsink[g * CHUNK:(g + 1) * CHUNK])

    return pl.pallas_call(
        body, name=name, grid=(Hkv, nb // QB),
        in_specs=[pl.BlockSpec(memory_space=pltpu.SMEM), qspec, qspec, rowspec] + kvs + kvs,
        out_specs=[qspec, colspec(L), colspec(L), colspec(CTX), colspec(CTX),
                   pl.BlockSpec((None, 8, LANES), lambda h, m: (h, 0, 0))],
        out_shape=[jax.ShapeDtypeStruct((L, Hkv * G * LANES), F32),
                   jax.ShapeDtypeStruct((L, Hkv * LANES), F32), jax.ShapeDtypeStruct((L, Hkv * LANES), F32),
                   jax.ShapeDtypeStruct((CTX, Hkv * LANES), F32), jax.ShapeDtypeStruct((CTX, Hkv * LANES), F32),
                   jax.ShapeDtypeStruct((Hkv, 8, LANES), F32)],
        compiler_params=_cparams("parallel", "arbitrary"),
    )(sink, q, do, lse, *([k] * nkv), *([v] * nkv))


def _attn_assemble(dq, dk_lat, dv_lat, dk_ctx, dv_ctx, cos, sin, *, T, L, CTX, Hq, Hkv, tr, name):
    Wq, Wk = Hq * LANES, Hkv * LANES
    ctx_blocks = CTX // tr
    nl = L // tr

    def unrope(d, c, s_, heads):
        return d * jnp.tile(c, (1, heads)) + _swap32(d * jnp.tile(s_, (1, heads)))

    def body(is_ctx, dqt, dkl, dvl, dkc, dvc, c, s_):
        dq_ = jnp.where(is_ctx, 0.0, unrope(dqt, c, s_, Hq))
        dk_ = unrope(jnp.where(is_ctx, dkc, dkl), c, s_, Hkv)
        dv_ = jnp.where(is_ctx, dvc, dvl)
        return (jnp.concatenate([dq_, dk_, dv_], axis=1),), ()

    def ctx_map(i):
        return (jnp.clip(i - nl, 0, ctx_blocks - 1), 0)

    assert T % tr == 0 and L % tr == 0 and CTX % tr == 0
    in_specs = [pl.BlockSpec((tr, Wq), lambda i: (jnp.minimum(i, nl - 1), 0)),
                pl.BlockSpec((tr, Wk), lambda i: (jnp.minimum(i, nl - 1), 0)),
                pl.BlockSpec((tr, Wk), lambda i: (jnp.minimum(i, nl - 1), 0)),
                pl.BlockSpec((tr, Wk), ctx_map), pl.BlockSpec((tr, Wk), ctx_map),
                pl.BlockSpec((tr, LANES), lambda i: (i, 0)), pl.BlockSpec((tr, LANES), lambda i: (i, 0))]

    def kern(dq_r, dkl_r, dvl_r, dkc_r, dvc_r, c_r, s_r, o_ref):
        is_ctx = pl.program_id(0) >= nl
        (out,), _ = body(is_ctx, dq_r[...], dkl_r[...], dvl_r[...], dkc_r[...], dvc_r[...], c_r[...], s_r[...])
        o_ref[...] = out.astype(o_ref.dtype)

    return pl.pallas_call(
        kern, name=name, grid=(T // tr,), in_specs=in_specs,
        out_specs=pl.BlockSpec((tr, Wq + 2 * Wk), lambda i: (i, 0)),
        out_shape=jax.ShapeDtypeStruct((T, Wq + 2 * Wk), BF16),
        compiler_params=_cparams("parallel"),
    )(dq, dk_lat, dv_lat, dk_ctx, dv_ctx, cos, sin)


def _my_place():
    x, y, c = lax.axis_index("x"), lax.axis_index("y"), lax.axis_index("c")
    return x, y, c


def _all_gather_small(v, *, name, dep=None):
    R, C = v.shape
    n_dep = 0 if dep is None else 1

    def body(x_ref, *rest):
        out_ref, send_sems, recv_sems, local_sem = rest[n_dep:]
        x, y, c = _my_place()
        me, sibling = (x, y, c), (x, y, 1 - c)
        chips = [(1 - x, y), (x, 1 - y), (1 - x, 1 - y)]

        def slot(px, py, pc):
            return out_ref.at[4 * px + 2 * py + pc]

        def copy(k, block, to, src=None):
            return pltpu.make_async_remote_copy(
                src_ref=slot(*block) if src is None else src, dst_ref=slot(*block),
                send_sem=send_sems.at[k], recv_sem=recv_sems.at[k], device_id=to, device_id_type=MESH)

        mine = pltpu.make_async_copy(x_ref, slot(*me), local_sem)
        mine.start()
        first = [copy(0, me, sibling, src=x_ref)]
        first += [copy(1 + j, me, (*chip, c), src=x_ref) for j, chip in enumerate(chips)]
        for cp in first:
            cp.start()
        passed = [copy(4 + j, (*chip, c), sibling) for j, chip in enumerate(chips)]
        for j, chip in enumerate(chips):
            copy(1 + j, (*chip, c), me).wait_recv()
            passed[j].start()
        copy(0, sibling, me).wait_recv()
        for j, chip in enumerate(chips):
            copy(4 + j, (*chip, 1 - c), me).wait_recv()
        for cp in first + passed:
            cp.wait_send()
        mine.wait()

    return pl.pallas_call(
        body, name=name, out_shape=jax.ShapeDtypeStruct((N_DEV, R, C), v.dtype),
        in_specs=[pl.BlockSpec(memory_space=pltpu.VMEM)] + [pl.BlockSpec(memory_space=pl.ANY)] * n_dep,
        out_specs=pl.BlockSpec(memory_space=pltpu.VMEM),
        scratch_shapes=[pltpu.SemaphoreType.DMA((7,)), pltpu.SemaphoreType.DMA((7,)), pltpu.SemaphoreType.DMA],
    )(v, *([dep] if n_dep else []))


_HBM_SPEC = pl.BlockSpec(memory_space=pltpu.HBM)
_SEM_SPEC = pl.BlockSpec(memory_space=pltpu.SEMAPHORE)
_ANY_SPEC = pl.BlockSpec(memory_space=pl.ANY)
_DATAFLOW = pltpu.SideEffectType.DATAFLOW_SIDE_EFFECTING
N_PEERS = N_DEV - 1


def _peer(r):
    x, y, c = _my_place()
    return ((1 - x) if r & 4 else x, (1 - y) if r & 2 else y, (1 - c) if r & 1 else c)


def _index_of(place):
    return 4 * place[0] + 2 * place[1] + place[2]


def _slot(ref, axis, idx, size):
    if axis == 0:
        return ref.at[pl.ds(idx * size, size), :]
    return ref.at[:, pl.ds(idx * size, size)]


def _cast_place(w3, layer, axis, me_arr, dep, *, name):
    Ks, Ns = w3.shape[1], w3.shape[2]
    tr = _pick(Ks, 256, 16)
    nblk = Ks // tr
    full = (Ks * N_DEV, Ns) if axis == 0 else (Ks, Ns * N_DEV)
    if axis == 0:
        out_map = lambda i, me: (me[0] * nblk + i, 0)
    else:
        out_map = lambda i, me: (i, me[0])

    def body(me_ref, w_ref, dep_ref, o_ref):
        o_ref[...] = w_ref[...].astype(BF16)

    return pl.pallas_call(
        body, name=name, out_shape=jax.ShapeDtypeStruct(full, BF16),
        grid_spec=pltpu.PrefetchScalarGridSpec(
            num_scalar_prefetch=1, grid=(nblk,),
            in_specs=[pl.BlockSpec((None, tr, Ns), lambda i, me: (layer, i, 0)), pl.BlockSpec(memory_space=pl.ANY)],
            out_specs=pl.BlockSpec((tr, Ns), out_map)),
        compiler_params=_cparams("parallel"),
    )(me_arr, w3, dep)


AG_FIRST = 4
AG_CHIPS = 3


def _sibling():
    x, y, c = _my_place()
    return (x, y, 1 - c)


def _chip_peer(j, same_core=True):
    x, y, c = _my_place()
    px = (1 - x) if j in (0, 2) else x
    py = (1 - y) if j in (1, 2) else y
    return (px, py, c if same_core else 1 - c)


def _gather_start(lands, axes, after, *, name):
    nt = len(lands)
    sizes = [l.shape[a] // N_DEV for l, a in zip(lands, axes)]

    def body(*refs):
        ins, send_sems, recv_sems, token = refs[:nt], refs[nt + 1], refs[nt + 2], refs[-1]
        my_idx = _index_of(_my_place())
        for t in range(nt):
            mine = _slot(ins[t], axes[t], my_idx, sizes[t])
            for k, to in enumerate([_sibling()] + [_chip_peer(j) for j in range(AG_CHIPS)]):
                pltpu.make_async_remote_copy(src_ref=mine, dst_ref=mine, send_sem=send_sems.at[t * AG_FIRST + k],
                                             recv_sem=recv_sems.at[t * AG_FIRST + k], device_id=to,
                                             device_id_type=MESH).start()
        token[...] = jnp.zeros_like(token)

    res = pl.pallas_call(
        body, name=name,
        out_shape=(pltpu.SemaphoreType.DMA((nt * AG_FIRST,)), pltpu.SemaphoreType.DMA((nt * AG_FIRST,)),
                   *[pltpu.HBM(l.shape, l.dtype) for l in lands], jax.ShapeDtypeStruct((8, LANES), F32)),
        in_specs=[_HBM_SPEC] * nt + [_ANY_SPEC],
        out_specs=(_SEM_SPEC, _SEM_SPEC, *[_HBM_SPEC] * nt, pl.BlockSpec(memory_space=pltpu.VMEM)),
        input_output_aliases={t: 2 + t for t in range(nt)},
        compiler_params=pltpu.CompilerParams(has_side_effects=_DATAFLOW),
    )(*[pltpu.with_memory_space_constraint(l, pltpu.HBM) for l in lands], after)
    return res[0], res[1], list(res[2:2 + nt]), res[-1]


def _gather_forward(send_a, recv_a, lands, axes, after, *, name):
    nt = len(lands)
    sizes = [l.shape[a] // N_DEV for l, a in zip(lands, axes)]
    afters = tuple(after) if isinstance(after, (tuple, list)) else (after,)

    def body(*refs):
        ins, send_a, recv_a = refs[:nt], refs[nt], refs[nt + 1]
        send_f, recv_f, token = refs[nt + 2 + len(afters)], refs[nt + 3 + len(afters)], refs[-1]
        my_idx = _index_of(_my_place())
        for t in range(nt):
            for j in range(AG_CHIPS):
                src_dev = _chip_peer(j)
                arrived = _slot(ins[t], axes[t], _index_of(src_dev), sizes[t])
                pltpu.make_async_remote_copy(
                    src_ref=_slot(ins[t], axes[t], my_idx, sizes[t]), dst_ref=arrived,
                    send_sem=send_a.at[t * AG_FIRST + 1 + j], recv_sem=recv_a.at[t * AG_FIRST + 1 + j],
                    device_id=src_dev, device_id_type=MESH).wait_recv()
                pltpu.make_async_remote_copy(src_ref=arrived, dst_ref=arrived, send_sem=send_f.at[t * AG_CHIPS + j],
                                             recv_sem=recv_f.at[t * AG_CHIPS + j], device_id=_sibling(),
                                             device_id_type=MESH).start()
        token[...] = jnp.zeros_like(token)

    res = pl.pallas_call(
        body, name=name,
        out_shape=(pltpu.SemaphoreType.DMA((nt * AG_CHIPS,)), pltpu.SemaphoreType.DMA((nt * AG_CHIPS,)),
                   *[pltpu.HBM(l.shape, l.dtype) for l in lands], jax.ShapeDtypeStruct((8, LANES), F32)),
        in_specs=[_HBM_SPEC] * nt + [_SEM_SPEC, _SEM_SPEC] + [_ANY_SPEC] * len(afters),
        out_specs=(_SEM_SPEC, _SEM_SPEC, *[_HBM_SPEC] * nt, pl.BlockSpec(memory_space=pltpu.VMEM)),
        input_output_aliases={t: 2 + t for t in range(nt)},
        compiler_params=pltpu.CompilerParams(has_side_effects=_DATAFLOW),
    )(*lands, send_a, recv_a, *afters)
    return res[0], res[1], list(res[2:2 + nt]), res[-1]


def _gather_wait(send_a, recv_a, send_f, recv_f, lands, axes, after, *, name):
    nt = len(lands)
    sizes = [l.shape[a] // N_DEV for l, a in zip(lands, axes)]

    def body(*refs):
        ins, send_a, recv_a, send_f, recv_f = refs[:nt], refs[nt], refs[nt + 1], refs[nt + 2], refs[nt + 3]
        my_idx = _index_of(_my_place())
        sib = _sibling()
        for t in range(nt):
            mine = _slot(ins[t], axes[t], my_idx, sizes[t])
            for k, to in enumerate([sib] + [_chip_peer(j) for j in range(AG_CHIPS)]):
                pltpu.make_async_remote_copy(src_ref=mine, dst_ref=mine, send_sem=send_a.at[t * AG_FIRST + k],
                                             recv_sem=recv_a.at[t * AG_FIRST + k], device_id=to,
                                             device_id_type=MESH).wait_send()
            pltpu.make_async_remote_copy(src_ref=mine, dst_ref=_slot(ins[t], axes[t], _index_of(sib), sizes[t]),
                                         send_sem=send_a.at[t * AG_FIRST], recv_sem=recv_a.at[t * AG_FIRST],
                                         device_id=sib, device_id_type=MESH).wait_recv()
            for j in range(AG_CHIPS):
                sent = _slot(ins[t], axes[t], _index_of(_chip_peer(j)), sizes[t])
                got = _slot(ins[t], axes[t], _index_of(_chip_peer(j, same_core=False)), sizes[t])
                cp = pltpu.make_async_remote_copy(src_ref=sent, dst_ref=got, send_sem=send_f.at[t * AG_CHIPS + j],
                                                  recv_sem=recv_f.at[t * AG_CHIPS + j], device_id=sib,
                                                  device_id_type=MESH)
                cp.wait_send()
                cp.wait_recv()

    res = pl.pallas_call(
        body, name=name, out_shape=[pltpu.HBM(l.shape, l.dtype) for l in lands],
        in_specs=[_HBM_SPEC] * nt + [_SEM_SPEC] * 4 + [_ANY_SPEC], out_specs=[_HBM_SPEC] * nt,
        input_output_aliases={t: t for t in range(nt)},
        compiler_params=pltpu.CompilerParams(has_side_effects=_DATAFLOW),
    )(*lands, send_a, recv_a, send_f, recv_f, after)
    return list(res)


def _scatter_start(dw, axis, *, name):
    size = dw.shape[axis] // N_DEV
    land_shape = (N_PEERS, size, dw.shape[1]) if axis == 0 else (N_PEERS, dw.shape[0], size)

    def body(dw_ref, land_ref, send_sems, recv_sems, dw_thru, land_thru, token):
        for r in range(1, N_DEV):
            p = _peer(r)
            pltpu.make_async_remote_copy(src_ref=_slot(dw_ref, axis, _index_of(p), size), dst_ref=land_ref.at[r - 1],
                                         send_sem=send_sems.at[r - 1], recv_sem=recv_sems.at[r - 1], device_id=p,
                                         device_id_type=MESH).start()
        token[...] = jnp.zeros_like(token)

    land = pltpu.with_memory_space_constraint(lax.empty(land_shape, dw.dtype), pltpu.HBM)
    return pl.pallas_call(
        body, name=name,
        out_shape=(pltpu.SemaphoreType.DMA((N_PEERS,)), pltpu.SemaphoreType.DMA((N_PEERS,)),
                   pltpu.HBM(dw.shape, dw.dtype), pltpu.HBM(land_shape, dw.dtype), jax.ShapeDtypeStruct((8, LANES), F32)),
        in_specs=[_HBM_SPEC, _HBM_SPEC],
        out_specs=(_SEM_SPEC, _SEM_SPEC, _HBM_SPEC, _HBM_SPEC, pl.BlockSpec(memory_space=pltpu.VMEM)),
        input_output_aliases={0: 2, 1: 3},
        compiler_params=pltpu.CompilerParams(has_side_effects=_DATAFLOW),
    )(pltpu.with_memory_space_constraint(dw, pltpu.HBM), land)


def _scatter_wait(send_sems, recv_sems, dw, land, axis, after, *, name):
    size = dw.shape[axis] // N_DEV

    def body(dw_ref, land_ref, send_sems, recv_sems, after_ref, dw_thru, land_thru):
        for r in range(1, N_DEV):
            p = _peer(r)
            cp = pltpu.make_async_remote_copy(src_ref=_slot(dw_ref, axis, _index_of(p), size), dst_ref=land_ref.at[r - 1],
                                              send_sem=send_sems.at[r - 1], recv_sem=recv_sems.at[r - 1], device_id=p,
                                              device_id_type=MESH)
            cp.wait_send()
            cp.wait_recv()

    return pl.pallas_call(
        body, name=name, out_shape=(pltpu.HBM(dw.shape, dw.dtype), pltpu.HBM(land.shape, land.dtype)),
        in_specs=[_HBM_SPEC, _HBM_SPEC, _SEM_SPEC, _SEM_SPEC, _ANY_SPEC], out_specs=(_HBM_SPEC, _HBM_SPEC),
        input_output_aliases={0: 0, 1: 1},
        compiler_params=pltpu.CompilerParams(has_side_effects=_DATAFLOW),
    )(dw, land, send_sems, recv_sems, after)


def _adamw_math(w, g, m, v):
    m = ADAM_B1 * m + (1.0 - ADAM_B1) * g
    v = ADAM_B2 * v + (1.0 - ADAM_B2) * (g * g)
    m_hat = m / (1.0 - ADAM_B1 ** ADAM_STEP)
    v_hat = v / (1.0 - ADAM_B2 ** ADAM_STEP)
    delta = -ADAM_LR * (m_hat / (jnp.sqrt(v_hat) + ADAM_EPS) + ADAM_WD * w)
    return delta, m, v


def _adamw_sharded(w, m, v, layer, dw, land, axis, me_arr, prev, *, name):
    nl, Ks, Ns = w.shape
    tr = _pick(Ks, 128, 16)
    nblk = Ks // tr
    if axis == 0:
        own_map = lambda i, me: (me[0] * nblk + i, 0)
    else:
        own_map = lambda i, me: (i, me[0])
    wspec = pl.BlockSpec((None, tr, Ns), lambda i, me: (layer, i, 0))
    n_prev = 0 if prev is None else 4

    def body(me_ref, w_ref, m_ref, v_ref, own_ref, r_ref, *rest):
        g_ref, d_ref, nm_ref, nv_ref = rest[n_prev:]
        g = own_ref[...].astype(F32)
        for r in range(N_PEERS):
            g = g + r_ref[r].astype(F32)
        delta, nm, nv = _adamw_math(w_ref[...], g, m_ref[...], v_ref[...])
        g_ref[...], d_ref[...], nm_ref[...], nv_ref[...] = g, delta, nm, nv

    return pl.pallas_call(
        body, name=name, out_shape=[jax.ShapeDtypeStruct((nl, Ks, Ns), F32)] * 4,
        grid_spec=pltpu.PrefetchScalarGridSpec(
            num_scalar_prefetch=1, grid=(nblk,),
            in_specs=[wspec, wspec, wspec, pl.BlockSpec((tr, Ns), own_map),
                      pl.BlockSpec((N_PEERS, tr, Ns), lambda i, me: (0, i, 0))] + [_ANY_SPEC] * n_prev,
            out_specs=[wspec] * 4),
        input_output_aliases={6 + k: k for k in range(n_prev)},
        compiler_params=_cparams("parallel"),
    )(me_arr, w, m, v, dw, land, *(prev or []))


def _adamw_flat(w, g, m, v, *, name):
    def body(w_ref, g_ref, m_ref, v_ref, d_ref, nm_ref, nv_ref):
        d_ref[...], nm_ref[...], nv_ref[...] = _adamw_math(w_ref[...], g_ref[...], m_ref[...], v_ref[...])

    spec = pl.BlockSpec(memory_space=pltpu.VMEM)
    return pl.pallas_call(body, name=name, in_specs=[spec] * 4, out_specs=[spec] * 3,
                          out_shape=[jax.ShapeDtypeStruct(w.shape, F32)] * 3)(w, g, m, v)


def _sum_devices(a, *, name):
    def body(a_ref, o_ref):
        s = a_ref[0]
        for d in range(1, N_DEV):
            s = s + a_ref[d]
        o_ref[...] = s

    spec = pl.BlockSpec(memory_space=pltpu.VMEM)
    return pl.pallas_call(body, name=name, in_specs=[spec], out_specs=spec,
                          out_shape=jax.ShapeDtypeStruct(a.shape[1:], F32))(a)


def _ada_mods(c16, ada_w, ada_b_cols, *, name):
    nl, D, cols = ada_w.shape
    bn = _pick(cols, 512)

    def body(c_ref, w_ref, b_ref, o_ref):
        cond = _silu(c_ref[...]).astype(BF16)
        o_ref[...] = _dot(cond, w_ref[...].astype(BF16)) + b_ref[...]

    return pl.pallas_call(
        body, name=name, grid=(nl, cols // bn),
        in_specs=[pl.BlockSpec((16, D), lambda l, j: (0, 0)), pl.BlockSpec((None, D, bn), lambda l, j: (l, 0, j)),
                  pl.BlockSpec((None, 1, bn), lambda l, j: (l, 0, j))],
        out_specs=pl.BlockSpec((None, 16, bn), lambda l, j: (l, 0, j)),
        out_shape=jax.ShapeDtypeStruct((nl, 16, cols), F32),
        compiler_params=_cparams("parallel", "parallel"),
    )(c16, ada_w, ada_b_cols)


def _ada_bwd(cond_t, dmod, w, m, v, *, name):
    nl, D, cols = w.shape
    tr = _pick(D, 256, 8)

    def body(ct_ref, dm_ref, w_ref, m_ref, v_ref, g_ref, d_ref, nm_ref, nv_ref, dc_ref):
        ct, dm, wt = ct_ref[...], dm_ref[...], w_ref[...]
        g = ct[:, 0:1] * dm[0:1, :]
        for r in range(1, N_DEV + 1):
            g = g + ct[:, r:r + 1] * dm[r:r + 1, :]
        delta, nm, nv = _adamw_math(wt, g, m_ref[...], v_ref[...])
        g_ref[...], d_ref[...], nm_ref[...], nv_ref[...] = g, delta, nm, nv
        dc_ref[...] = jnp.sum(wt * dm[N_DEV:N_DEV + 1, :], axis=-1, keepdims=True)

    wspec = pl.BlockSpec((None, tr, cols), lambda l, i: (l, i, 0))
    return pl.pallas_call(
        body, name=name, grid=(nl, D // tr),
        in_specs=[pl.BlockSpec((tr, 16), lambda l, i: (i, 0)), pl.BlockSpec((None, 16, cols), lambda l, i: (l, 0, 0)),
                  wspec, wspec, wspec],
        out_specs=[wspec] * 4 + [pl.BlockSpec((None, tr, 1), lambda l, i: (l, i, 0))],
        out_shape=[jax.ShapeDtypeStruct((nl, D, cols), F32)] * 4 + [jax.ShapeDtypeStruct((nl, D, 1), F32)],
        compiler_params=_cparams("parallel", "parallel"),
    )(cond_t, dmod, w, m, v)


def _rope_tables(L, CTX):
    def angles(pos, dim):
        inv_freq = ROPE_BASE ** (-jnp.arange(0, dim, 2, dtype=F32) / dim)
        return pos.astype(F32)[:, None] * inv_freq[None, :]

    def pad(cos, sin):
        return (jnp.concatenate([cos, jnp.ones((CTX, LANES), F32)], 0),
                jnp.concatenate([sin, jnp.zeros((CTX, LANES), F32)], 0))

    ret = angles(jnp.arange(L), 2 * LANES)
    ret_cs = pad(jnp.cos(ret), jnp.sin(ret))
    rows = angles(jnp.arange(L) // GRID_W, ATT_HEAD_DIM // 2)
    cols = angles(jnp.arange(L) % GRID_W, ATT_HEAD_DIM // 2)
    cos = jnp.concatenate([jnp.cos(rows)] * 2 + [jnp.cos(cols)] * 2, axis=1)
    sin = jnp.concatenate([-jnp.sin(rows), jnp.sin(rows), -jnp.sin(cols), jnp.sin(cols)], axis=1)
    return ret_cs, pad(cos, sin)


def kernel(x, c, ctx, c_ctx, ada_w, ada_b, norm_mix_g, norm_mlp_g, mlp_w1, mlp_w2, ret_w_in, ret_w_out, ret_decay_fwd, ret_decay_bwd, attn_w_in, attn_w_out, attn_sink, final_norm_g, loss_target, m_c_ctx, m_ada_w, m_ada_b, m_norm_mix_g, m_norm_mlp_g, m_mlp_w1, m_mlp_w2, m_ret_w_in, m_ret_w_out, m_ret_decay_fwd, m_ret_decay_bwd, m_attn_w_in, m_attn_w_out, m_attn_sink, m_final_norm_g, v_c_ctx, v_ada_w, v_ada_b, v_norm_mix_g, v_norm_mlp_g, v_mlp_w1, v_mlp_w2, v_ret_w_in, v_ret_w_out, v_ret_decay_fwd, v_ret_decay_bwd, v_attn_w_in, v_attn_w_out, v_attn_sink, v_final_norm_g):
    L, D = x.shape[1], x.shape[2]
    CTX = ctx.shape[1]
    T = L + CTX
    RH = ret_decay_fwd.shape[-1]
    assert D == RH * 2 * LANES and ada_w.shape[0] == 2 and ret_w_in.shape[0] == 1 and attn_w_in.shape[0] == 1
    Hq = attn_sink.shape[-1]
    Hkv = (attn_w_in.shape[-1] * N_DEV // ATT_HEAD_DIM - Hq) // 2
    G = Hq // Hkv
    FF = mlp_w1.shape[-1] * N_DEV
    Wq_r, Wv_r = RH * 2 * LANES, RH * 4 * LANES
    acols = ada_w.shape[-1]
    tr = _pick(CTX, 256, 8)
    tr_wide = _pick(CTX, 128, 8)
    bmT = T // 4 if (T % 64 == 0) else T
    bmL = L // 4 if (L % 64 == 0) else L
    x_idx, y_idx, c_idx = lax.axis_index("x"), lax.axis_index("y"), lax.axis_index("c")
    me = 4 * x_idx + 2 * y_idx + c_idx
    me_arr = jnp.reshape(me, (1,)).astype(jnp.int32)

    (rcos, rsin), (acos, asin) = _rope_tables(L, CTX)
    lg_f = jax.nn.log_sigmoid(ret_decay_fwd.astype(F32))
    lg_b = jax.nn.log_sigmoid(ret_decay_bwd.astype(F32))

    c_pad = jnp.concatenate([c.astype(F32), jnp.zeros((7, D), F32)], 0)
    c_all = _all_gather_small(c_pad, name="ag_c")[:, 0, :]
    c16 = jnp.concatenate([c_all, c_ctx[None, :], jnp.zeros((7, D), F32)], 0)
    ada_b_cols = lax.dynamic_slice_in_dim(ada_b, me * acols, acols, axis=1)[:, None, :]
    mods_shard = _ada_mods(c16, ada_w, ada_b_cols, name="ada_mods")
    mods_all = _all_gather_small(mods_shard.reshape(32, acols), name="ag_mods")

    wdefs = {"ret_in": (ret_w_in, 0, 1), "ret_out": (ret_w_out, 0, 0), "w1_0": (mlp_w1, 0, 1), "w2_0": (mlp_w2, 0, 0),
             "attn_in": (attn_w_in, 0, 1), "attn_out": (attn_w_out, 0, 0), "w1_1": (mlp_w1, 1, 1), "w2_1": (mlp_w2, 1, 0)}
    groups = [["ret_in"], ["ret_out"], ["w1_0", "w2_0"], ["attn_in", "attn_out"], ["w1_1", "w2_1"]]

    placed = {}

    def ag_start(gi, after):
        g_axes = [wdefs[k][2] for k in groups[gi]]
        ssem, rsem, lands, tok_ = _gather_start([placed[k] for k in groups[gi]], g_axes, after, name=f"ag_start{gi}")
        return dict(a=(ssem, rsem), lands=lands, axes=g_axes, gi=gi), tok_

    def ag_forward(g, after):
        fs, fr, g["lands"], tok_ = _gather_forward(*g["a"], g["lands"], g["axes"], after, name=f"ag_forward{g['gi']}")
        g["f"] = (fs, fr)
        return tok_

    def ag_wait(g, after):
        return _gather_wait(*g["a"], *g["f"], g["lands"], g["axes"], after, name=f"ag_wait{g['gi']}")

    placed["ret_in"] = _cast_place(*wdefs["ret_in"], me_arr, mods_all, name="place_ret_in")
    g0, tok = ag_start(0, mods_all)
    last_cast = tok
    for keys in groups[1:]:
        for k in keys:
            last_cast = placed[k] = _cast_place(*wdefs[k], me_arr, last_cast, name=f"place_{k}")
    mods_all = (mods_all + tok[0, 0]).reshape(N_DEV, 2, 16, acols).transpose(1, 2, 0, 3).reshape(2, 16, 6, D)
    mod_lat = lax.dynamic_index_in_dim(mods_all, me, axis=1, keepdims=False)
    mod_ctx = mods_all[:, N_DEV]

    def pack(i, ks, kc):
        return jnp.stack([mod_lat[i, ks], mod_lat[i, kc], mod_ctx[i, ks], mod_ctx[i, kc]], 0)

    def gates(i, k):
        return jnp.stack([mod_lat[i, k], mod_ctx[i, k]], 0)

    def gate_epilogue(x_rows_lat_only):
        def epi(acc, i, j, xt, gv):
            if x_rows_lat_only:
                gate = gv[0:1, :]
            else:
                row = i * acc.shape[0] + lax.broadcasted_iota(jnp.int32, (acc.shape[0], 1), 0)
                gate = jnp.where(row >= L, gv[1:2, :], gv[0:1, :])
            return xt + gate * acc, acc
        return epi

    w1, w2 = {}, {}

    mmT = dict(M=T, bm=bmT)
    mmL = dict(M=L, bm=bmL)

    def bn_of(n, off=0):
        b = MM_BN
        while n % b or off % b:
            b -= LANES
        return b

    X0 = jnp.concatenate([x[0], ctx[0]], axis=0)
    g_mix0, g_mlp0 = norm_mix_g[0:1], norm_mlp_g[0:1]
    g_mix1, g_mlp1 = norm_mix_g[1:2], norm_mlp_g[1:2]
    tok = ag_forward(g0, (last_cast, rcos, rsin, acos, asin, X0))
    a0 = _normmod(X0, g_mix0, pack(0, 0, 1), R=T, L=L, tr=tr, name="normmod_mix0", dep=tok)
    (wr_in,) = ag_wait(g0, a0)
    g1, tok = ag_start(1, wr_in)
    g1b, tok = ag_start(2, tok)

    bn_qk = _pick(Wq_r, MM_BN, 2 * LANES)
    nq_blocks = Wq_r // bn_qk
    kscale = float(2 * LANES) ** -0.5

    def rope_epi(acc, i, j, cos, sin):
        parts = []
        for h in range(acc.shape[1] // (2 * LANES)):
            x1 = acc[:, h * 2 * LANES:h * 2 * LANES + LANES]
            x2 = acc[:, h * 2 * LANES + LANES:(h + 1) * 2 * LANES]
            parts += [x1 * cos - x2 * sin, x2 * cos + x1 * sin]
        return (jnp.concatenate(parts, axis=1) * jnp.where(j < nq_blocks, 1.0, kscale),)

    def row_tile(arr, bm):
        return (arr, (bm, LANES), lambda i, j: (i, 0))

    (qk0,) = _mm(a0, wr_in, "nn", [BF16], N=2 * Wq_r, K=D, bn=bn_qk, bk=D, name="ret_qk", epilogue=rope_epi,
                 extras=[row_tile(rcos, bmT), row_tile(rsin, bmT)], dep=tok, **mmT)
    bn_vg = bn_of(2 * Wv_r, 2 * Wq_r)
    (vg0,) = _mm(a0, wr_in, "nn", [BF16], N=2 * Wv_r, K=D, bn=bn_vg, bk=D, name="ret_vg", b_col0=2 * Wq_r, dep=tok,
                 **mmT)

    tok = ag_forward(g1, vg0)
    of, st_f = _ret_fwd(qk0, vg0, lg_f + tok[0, 0], None, T=T, L=L, H=RH, rev=False, name="ret_scan_f")
    (wr_out,) = ag_wait(g1, of)
    o0, z0, st_b = _ret_fwd(qk0, vg0, lg_b, of, T=T, L=L, H=RH, rev=True, name="ret_scan_b", readout=True)
    tok = ag_forward(g1b, o0)

    bnD = _pick(D, MM_BN)

    def xtile(arr, bm):
        return (arr, (bm, bnD), lambda i, j: (i, j))

    def gtile(gv):
        return (gv, (2, bnD), lambda i, j: (0, j))

    bk_max = 2048

    def quarter(rows, div=4):
        return dict(M=rows["M"], bm=rows["bm"] // div if rows["bm"] % (16 * div) == 0 else rows["bm"])
    X1, ro0 = _mm(z0, wr_out, "nn", [F32, BF16], N=D, K=Wv_r, bn=bnD, bk=Wv_r, name="ret_out", cols_outer=True,
                  epilogue=gate_epilogue(False), extras=[xtile(X0, quarter(mmT, 2)["bm"]), gtile(gates(0, 2))], dep=tok,
                  **quarter(mmT, 2))

    def mlp_fwd(Xin, i, g_mlp, rows, name, weights_after_norm=None):
        a = _normmod(Xin, g_mlp, pack(i, 3, 4), R=rows["M"], L=L, tr=tr, name=f"normmod_mlp{name}")
        dep_up = None if weights_after_norm is None else weights_after_norm(a)

        def relu2(acc, i_, j_):
            u = jnp.maximum(acc, 0.0)
            return u, u * u

        bnF = _pick(FF, MM_BN)
        u, r = _mm(a, w1[i], "nn", [BF16, BF16], N=FF, K=D, bn=bnF, bk=D, name=f"mlp_up{name}", epilogue=relu2,
                   dep=dep_up, **rows)
        Xout, mo = _mm(r, w2[i], "nn", [F32, BF16], N=D, K=FF, bn=bnD, bk=FF, name=f"mlp_down{name}",
                       epilogue=gate_epilogue(rows["M"] == L), cols_outer=True,
                       extras=[xtile(Xin, quarter(rows)["bm"]), gtile(gates(i, 5))], **quarter(rows))
        return a, u, r, Xout, mo

    later = {}

    def mlp0_weights(a):
        w1[0], w2[0] = ag_wait(g1b, a)
        later["g2"], tok2 = ag_start(3, w1[0])
        later["g3"], tok3 = ag_start(4, tok2)
        return tok3

    a1, u0, r0, X2, mo0 = mlp_fwd(X1, 0, g_mlp0, mmT, "0", weights_after_norm=mlp0_weights)
    g2, g3 = later["g2"], later["g3"]

    tok = ag_forward(g2, X2)
    a2 = _normmod(X2, g_mix1, pack(1, 0, 1), R=T, L=L, tr=tr, name="normmod_mix1", dep=tok)
    wa_in, wa_out = ag_wait(g2, a2)
    Wq_a, Wk_a = Hq * LANES, Hkv * LANES

    def arope_epi(acc, i, j, cos, sin):
        heads = acc.shape[1] // LANES
        return (acc * jnp.tile(cos, (1, heads)) + _swap32(acc) * jnp.tile(sin, (1, heads)),)

    bn_q = _pick(Wq_a, MM_BN)
    (q1,) = _mm(a2, wa_in, "nn", [BF16], N=Wq_a, K=D, bn=bn_q, bk=D, name="attn_q", epilogue=arope_epi,
                extras=[row_tile(acos, bmL), row_tile(asin, bmL)], **mmL)
    bn_k = bn_of(Wk_a, Wq_a)
    (k1,) = _mm(a2, wa_in, "nn", [BF16], N=Wk_a, K=D, bn=bn_k, bk=D, name="attn_k", b_col0=Wq_a, epilogue=arope_epi,
                extras=[row_tile(acos, bmT), row_tile(asin, bmT)], **mmT)
    bn_v = bn_of(Wk_a, Wq_a + Wk_a)
    (v1,) = _mm(a2, wa_in, "nn", [BF16], N=Wk_a, K=D, bn=bn_v, bk=D, name="attn_v", b_col0=Wq_a + Wk_a, **mmT)
    o1, lse = _attn_fwd(q1, k1, v1, attn_sink, L=L, CTX=CTX, Hkv=Hkv, G=G, name="attn_fwd")
    tok = ag_forward(g3, o1)
    X3, ao = _mm(o1, wa_out, "nn", [F32, BF16], N=D, K=Wq_a, bn=bnD, bk=_pick(Wq_a, 2048), name="attn_out",
                 epilogue=gate_epilogue(True), extras=[xtile(X2, bmL), gtile(gates(1, 2))], dep=tok, **mmL)

    def mlp1_weights(a):
        w1[1], w2[1] = ag_wait(g3, a)
        return None

    a3, u1, r1, X4, mo1 = mlp_fwd(X3, 1, g_mlp1, mmL, "1", weights_after_norm=mlp1_weights)

    dX4, dmo1, acc_head = _loss_head(X4, loss_target[0], mo1, final_norm_g[None, :], gates(1, 5)[0:1], L=L, tr=tr,
                                     name="loss_head")
    loss_part = jnp.sum(acc_head[0, 0])
    d_gf = acc_head[0, 1]
    zeros_d = jnp.zeros((D,), F32)
    dmod_lat = [[zeros_d] * 6, [zeros_d] * 6]
    dmod_ctx = [[zeros_d] * 6, [zeros_d] * 6]
    dmod_lat[1][5] = acc_head[0, 2]

    def dw_mm(a, b, M, N, K, name):
        return _mm(a, b, "tn", [BF16], M=M, N=N, K=K, bm=_pick(M, 512), bn=_pick(N, MM_BN), bk=K, name=name)[0]

    def mlp_bwd(dmo, a, u, r, i, rows, name):
        Mr = rows["M"]

        def times_2u(acc, i_, j_, ut):
            return (acc * (2.0 * ut.astype(F32)),)

        bnF = _pick(FF, MM_BN)
        dw2 = dw_mm(r, dmo, FF, D, Mr, f"mlp_down_dw{name}")
        tok_ = send_grad(f"w2_{i}", dw2, 0)
        (dh,) = _mm(dmo, w2[i], "nt", [BF16], N=FF, K=D, bn=bnF, bk=D, name=f"mlp_down_dx{name}", epilogue=times_2u,
                    extras=[(u, (rows["bm"], bnF), lambda i_, j_: (i_, j_))], dep=tok_, **rows)
        dw1 = dw_mm(a, dh, D, FF, Mr, f"mlp_up_dw{name}")
        tok_ = send_grad(f"w1_{i}", dw1, 1)
        (da,) = _mm(dh, w1[i], "nt", [BF16], N=D, K=FF, bn=bnD, bk=FF, name=f"mlp_up_dx{name}", dep=tok_,
                    cols_outer=True, **quarter(rows))
        return da

    pending = []

    def send_grad(key, dw, axis):
        ssem, rsem, dw_thru, land, tok_ = _scatter_start(dw, axis, name=f"rs_start_{key}")
        pending.append((key, axis, ssem, rsem, dw_thru, land))
        return tok_

    da3 = mlp_bwd(dmo1, a3, u1, r1, 1, mmL, "1")
    dX3, dao, acc = _normmod_bwd(X3, da3, dX4, False, g_mlp1, pack(1, 3, 4), (ao, gates(1, 2)), R=L, L=L, tr=tr,
                                 name="normmod_mlp1_bwd")
    dmod_lat[1][3], dmod_lat[1][4], d_gmlp1, dmod_lat[1][2] = acc[0, 0], acc[0, 1], acc[0, 2], acc[0, 3]

    dwa_out = dw_mm(o1, dao, Wq_a, D, L, "attn_out_dw")
    tok = send_grad("attn_out", dwa_out, 0)
    (do1,) = _mm(dao, wa_out, "nt", [BF16], N=Wq_a, K=D, bn=_pick(Wq_a, MM_BN), bk=D, name="attn_out_dx", dep=tok, **mmL)
    dq1, dk1, dv1, dkx, dvx, dsink_acc = _attn_bwd(q1, k1, v1, do1, lse, attn_sink, L=L, CTX=CTX, Hkv=Hkv, G=G,
                                                   name="attn_bwd")
    dp1 = _attn_assemble(dq1, dk1, dv1, dkx, dvx, acos, asin, T=T, L=L, CTX=CTX, Hq=Hq, Hkv=Hkv, tr=tr_wide,
                         name="attn_assemble")
    Wa_in = Wq_a + 2 * Wk_a
    dwa_in = dw_mm(a2, dp1, D, Wa_in, T, "attn_in_dw")
    tok = send_grad("attn_in", dwa_in, 1)
    (da2,) = _mm(dp1, wa_in, "nt", [BF16], N=D, K=Wa_in, bn=bnD, bk=_pick(Wa_in, 2 * bk_max), name="attn_in_dx",
                 dep=tok, **mmT)
    dX2, dmo0, acc = _normmod_bwd(X2, da2, dX3, True, g_mix1, pack(1, 0, 1), (mo0, gates(0, 5)), R=T, L=L, tr=tr,
                                  name="normmod_mix1_bwd")
    dmod_lat[1][0], dmod_lat[1][1], d_gmix1, dmod_lat[0][5] = acc[0, 0], acc[0, 1], acc[0, 2] + acc[1, 2], acc[0, 3]
    dmod_ctx[1][0], dmod_ctx[1][1], dmod_ctx[0][5] = acc[1, 0], acc[1, 1], acc[1, 3]

    da1 = mlp_bwd(dmo0, a1, u0, r0, 0, mmT, "0")
    dX1, dro0, acc = _normmod_bwd(X1, da1, dX2, False, g_mlp0, pack(0, 3, 4), (ro0, gates(0, 2)), R=T, L=L, tr=tr,
                                  name="normmod_mlp0_bwd")
    dmod_lat[0][3], dmod_lat[0][4], d_gmlp0, dmod_lat[0][2] = acc[0, 0], acc[0, 1], acc[0, 2] + acc[1, 2], acc[0, 3]
    dmod_ctx[0][3], dmod_ctx[0][4], dmod_ctx[0][2] = acc[1, 0], acc[1, 1], acc[1, 3]

    dwr_out = dw_mm(z0, dro0, Wv_r, D, T, "ret_out_dw")
    tok = send_grad("ret_out", dwr_out, 0)
    (dz0,) = _mm(dro0, wr_out, "nt", [BF16], N=Wv_r, K=D, bn=_pick(Wv_r, MM_BN), bk=D, name="ret_out_dx", dep=tok, **mmT)
    do0, dg0 = _readout_bwd(dz0, o0, vg0, T=T, L=L, H=RH, tr=tr_wide, name="ret_readout_bwd")
    (dq_f, dk_f, dv_f, dlg_f), (dq_b, dk_b, dv_b, dlg_b) = _ret_bwd(qk0, vg0, do0, (st_f, st_b), (lg_f, lg_b),
                                                                    T=T, L=L, H=RH, name="ret_scan_bwd")
    dp0 = _ret_assemble(dq_f, dq_b, dk_f, dk_b, dv_f, dv_b, dg0, rcos, rsin, T=T, L=L, H=RH, tr=tr_wide,
                        name="ret_assemble")
    Wr_in = 2 * Wq_r + 2 * Wv_r
    dwr_in = dw_mm(a0, dp0, D, Wr_in, T, "ret_in_dw")
    tok = send_grad("ret_in", dwr_in, 1)
    (da0,) = _mm(dp0, wr_in, "nt", [BF16], N=D, K=Wr_in, bn=bnD, bk=Wr_in, name="ret_in_dx", dep=tok, cols_outer=True,
                 b_buffers=1, **quarter(mmT))
    dX0, acc = _normmod_bwd(X0, da0, dX1, False, g_mix0, pack(0, 0, 1), None, R=L, L=L, tr=tr, name="normmod_mix0_bwd")
    _, acc_c = _normmod_bwd(X0, da0, dX1, False, g_mix0, pack(0, 0, 1), None, R=CTX, L=0, tr=tr, row0=L,
                            name="normmod_mix0_bwd_ctx")
    dmod_lat[0][0], dmod_lat[0][1], d_gmix0 = acc[0, 0], acc[0, 1], acc[0, 2] + acc_c[1, 2]
    dmod_ctx[0][0], dmod_ctx[0][1] = acc_c[1, 0], acc_c[1, 1]
    grad_x = dX0[None]

    wmv = {"ret_in": (ret_w_in, m_ret_w_in, v_ret_w_in, 0, "ret_w_in"),
           "ret_out": (ret_w_out, m_ret_w_out, v_ret_w_out, 0, "ret_w_out"),
           "attn_in": (attn_w_in, m_attn_w_in, v_attn_w_in, 0, "attn_w_in"),
           "attn_out": (attn_w_out, m_attn_w_out, v_attn_w_out, 0, "attn_w_out"),
           "w1_0": (mlp_w1, m_mlp_w1, v_mlp_w1, 0, "mlp_w1"), "w1_1": (mlp_w1, m_mlp_w1, v_mlp_w1, 1, "mlp_w1"),
           "w2_0": (mlp_w2, m_mlp_w2, v_mlp_w2, 0, "mlp_w2"), "w2_1": (mlp_w2, m_mlp_w2, v_mlp_w2, 1, "mlp_w2")}
    big = {}

    def finish_grad(entry, after):
        key, axis, ssem, rsem, dw_thru, land = entry
        dw_done, land_done = _scatter_wait(ssem, rsem, dw_thru, land, axis, after, name=f"rs_wait_{key}")
        w_, m_, v_, layer, out_name = wmv[key]
        big[out_name] = _adamw_sharded(w_, m_, v_, layer, dw_done, land_done, axis, me_arr, big.get(out_name),
                                       name=f"adamw_{key}")
        return big[out_name][0]

    after = dX0
    for entry in pending[:-1]:
        after = finish_grad(entry, after)

    misc = jnp.zeros((D,), F32)
    misc = misc.at[0:RH].set(dlg_f[:, 0, 0]).at[RH:2 * RH].set(dlg_b[:, 0, 0])
    misc = misc.at[2 * RH:2 * RH + Hq].set(dsink_acc[:, :G, 0].reshape(Hq)).at[2 * RH + Hq].set(loss_part)
    rows = ([dmod_lat[i][k] for i in range(2) for k in range(6)] + [dmod_ctx[i][k] for i in range(2) for k in range(6)]
            + [d_gmix0, d_gmix1, d_gmlp0, d_gmlp1, d_gf, misc, zeros_d, zeros_d])
    part = jnp.stack(rows, 0)
    part_all = _all_gather_small(part, name="ag_small_grads", dep=after)
    tot = _sum_devices(part_all, name="sum_small_grads")

    grad_ada_b = (tot[0:12] + tot[12:24]).reshape(2, 6 * D)
    grad_norm_mix_g, grad_norm_mlp_g, grad_final_norm_g = tot[24:26], tot[26:28], tot[28]
    grad_ret_decay_fwd = (tot[29, 0:RH] * jax.nn.sigmoid(-ret_decay_fwd[0]))[None]
    grad_ret_decay_bwd = (tot[29, RH:2 * RH] * jax.nn.sigmoid(-ret_decay_bwd[0]))[None]
    grad_attn_sink = tot[29, 2 * RH:2 * RH + Hq][None]
    loss = tot[29, 2 * RH + Hq]

    dlat_cols = lax.dynamic_slice_in_dim(part_all[:, 0:12].reshape(N_DEV, 2, 6 * D), me * acols, acols, axis=2)
    dctx_cols = lax.dynamic_slice_in_dim(tot[12:24].reshape(2, 6 * D), me * acols, acols, axis=1)
    dmod16 = jnp.concatenate([dlat_cols.transpose(1, 0, 2), dctx_cols[:, None, :], jnp.zeros((2, 7, acols), F32)], 1)
    cond_t = _silu(c16).T
    g_ada, d_ada, nm_ada, nv_ada, dcond_part = _ada_bwd(cond_t, dmod16, ada_w, m_ada_w, v_ada_w, name="ada_bwd")
    dcond = (dcond_part[0, :, 0] + dcond_part[1, :, 0]).reshape(D // LANES, LANES)
    pad_rows = -(D // LANES) % 8
    dcond_pad = jnp.concatenate([dcond, jnp.zeros((pad_rows, LANES), F32)], 0) if pad_rows else dcond
    dcond_all = _all_gather_small(dcond_pad, name="ag_dcond")
    dcond_tot = _sum_devices(dcond_all, name="sum_dcond")[:D // LANES].reshape(D)
    sg = jax.nn.sigmoid(c_ctx)
    grad_c_ctx = dcond_tot * (sg * (1.0 + c_ctx * (1.0 - sg)))

    small_w = [c_ctx, ada_b, norm_mix_g, norm_mlp_g, ret_decay_fwd, ret_decay_bwd, attn_sink, final_norm_g]
    small_g = [grad_c_ctx, grad_ada_b, grad_norm_mix_g, grad_norm_mlp_g, grad_ret_decay_fwd, grad_ret_decay_bwd,
               grad_attn_sink, grad_final_norm_g]
    small_m = [m_c_ctx, m_ada_b, m_norm_mix_g, m_norm_mlp_g, m_ret_decay_fwd, m_ret_decay_bwd, m_attn_sink,
               m_final_norm_g]
    small_v = [v_c_ctx, v_ada_b, v_norm_mix_g, v_norm_mlp_g, v_ret_decay_fwd, v_ret_decay_bwd, v_attn_sink,
               v_final_norm_g]
    sizes = [w_.size for w_ in small_w]
    total = sum(-(-s // LANES) * LANES for s in sizes)
    total_pad = -(-total // (8 * LANES)) * 8 * LANES

    def flat_pack(ts, fill):
        pieces = []
        for t_ in ts:
            f = t_.reshape(-1).astype(F32)
            pad = -f.size % LANES
            pieces.append(jnp.concatenate([f, jnp.full((pad,), fill, F32)]) if pad else f)
        pieces.append(jnp.full((total_pad - total,), fill, F32))
        return jnp.concatenate(pieces).reshape(total_pad // LANES, LANES)

    d_s, nm_s, nv_s = _adamw_flat(flat_pack(small_w, 0.0), flat_pack(small_g, 0.0), flat_pack(small_m, 0.0),
                                  flat_pack(small_v, 1.0), name="adamw_small")

    def unpack(p):
        flat = p.reshape(-1)
        res, off = [], 0
        for w_, s in zip(small_w, sizes):
            res.append(flat[off:off + s].reshape(w_.shape))
            off += -(-s // LANES) * LANES
        return res

    finish_grad(pending[-1], d_s)
    d_small, nm_small, nv_small = unpack(d_s), unpack(nm_s), unpack(nv_s)
    small_names = ["c_ctx", "ada_b", "norm_mix_g", "norm_mlp_g", "ret_decay_fwd", "ret_decay_bwd", "attn_sink",
                   "final_norm_g"]
    sm = {n: (g_, d_, m_, v_) for n, g_, d_, m_, v_ in zip(small_names, small_g, d_small, nm_small, nv_small)}

    def out4(n):
        if n == "ada_w":
            return g_ada, d_ada, nm_ada, nv_ada
        if n in big:
            return tuple(big[n])
        return sm[n]

    order = ["c_ctx", "ada_w", "ada_b", "norm_mix_g", "norm_mlp_g", "mlp_w1", "mlp_w2", "ret_w_in", "ret_w_out",
             "ret_decay_fwd", "ret_decay_bwd", "attn_w_in", "attn_w_out", "attn_sink", "final_norm_g"]
    quads = [out4(n) for n in order]
    return (loss, grad_x, *[q_[0] for q_ in quads], *[q_[1] for q_ in quads], *[q_[2] for q_ in quads],
            *[q_[3] for q_ in quads])
```
